```python
import math
import jax, jax.numpy as jnp
from jax import lax
import numpy as np

D_MODEL = 2048
BATCH = 2
SEQ = 4096
DEPTH = 4

N_MIXERS = 4
PLE_DIM = 256
EPS = 1e-6
BLOCK = 128

SWA_HEADS = 32
SWA_KV_HEADS = 4
SWA_HEAD_DIM = 64
SWA_GROUP = SWA_HEADS // SWA_KV_HEADS
SWA_WIDTH = SWA_HEADS * SWA_HEAD_DIM
SWA_KV_WIDTH = SWA_KV_HEADS * SWA_HEAD_DIM
WINDOW = 128
REL_BUCKETS = 32
REL_MAX_DIST = 128

CONV_WIDTH = D_MODEL
CONV_TAPS = 3

SSM_WIDTH = D_MODEL
SSM_GROUP = 16
SSM_STATE = 64
SSM_GROUPS = SSM_WIDTH // SSM_GROUP
DT_MIN = 1e-3
DT_MAX = 1e-1

FOX_HEADS = 32
FOX_HEAD_DIM = 64
FOX_WIDTH = FOX_HEADS * FOX_HEAD_DIM

kernel_name = 'hybrid_swa_conv_s5_fox_trunk'


def n_layers_of(m):
    return len(range(m, DEPTH, N_MIXERS))


def rmsnorm(x, g):
    x32 = x.astype(jnp.float32)
    y = x32 * lax.rsqrt(jnp.mean(x32 * x32, axis=-1, keepdims=True) + EPS)
    return (y * g.astype(jnp.float32)).astype(x.dtype)


def t5_bucket(dist):
    max_exact = REL_BUCKETS // 2
    d = np.maximum(dist, 1).astype(np.float32)
    large = max_exact + (np.log(d / max_exact) / np.log(REL_MAX_DIST / max_exact) * (REL_BUCKETS - max_exact)).astype(np.int32)
    large = np.minimum(large, REL_BUCKETS - 1)
    return np.where(dist < max_exact, dist, large).astype(np.int32)


def swa_mixer(h, w_in, w_out, sinks, rel_bias):
    bsz, seq, _ = h.shape
    nb = seq // BLOCK
    q, k, v, gate = jnp.split(h @ w_in, [SWA_WIDTH, SWA_WIDTH + SWA_KV_WIDTH, SWA_WIDTH + 2 * SWA_KV_WIDTH], axis=-1)
    q = q.reshape(bsz, nb, BLOCK, SWA_KV_HEADS, SWA_GROUP, SWA_HEAD_DIM)
    k = k.reshape(bsz, nb, BLOCK, SWA_KV_HEADS, SWA_HEAD_DIM)
    v = v.reshape(bsz, nb, BLOCK, SWA_KV_HEADS, SWA_HEAD_DIM)
    prev = lambda t: jnp.concatenate([jnp.zeros_like(t[:, :1]), t[:, :-1]], axis=1)
    kb = jnp.concatenate([prev(k), k], axis=2)
    vb = jnp.concatenate([prev(v), v], axis=2)
    qi = np.arange(BLOCK)[:, None]
    kj = np.arange(2 * BLOCK)[None, :]
    dist = qi + BLOCK - kj
    band = (dist >= 0) & (dist < WINDOW)
    exists = (np.arange(nb)[:, None, None] * BLOCK - BLOCK + kj[None]) >= 0
    mask = jnp.asarray(band[None] & exists)[None, :, None, None]
    bucket = t5_bucket(np.clip(dist, 0, None))
    bias = rel_bias.astype(jnp.float32)[bucket]
    bias = bias.transpose(2, 0, 1).reshape(SWA_KV_HEADS, SWA_GROUP, BLOCK, 2 * BLOCK)
    scores = jnp.einsum('bnqhgd,bnjhd->bnhgqj', q, kb).astype(jnp.float32) * (SWA_HEAD_DIM ** -0.5) + bias
    scores = jnp.where(mask, scores, jnp.finfo(jnp.float32).min)
    sink = sinks.astype(jnp.float32).reshape(SWA_KV_HEADS, SWA_GROUP)[:, :, None, None]
    m = jnp.maximum(scores.max(axis=-1, keepdims=True), sink)
    e = jnp.exp(scores - m)
    probs = e / (e.sum(axis=-1, keepdims=True) + jnp.exp(sink - m))
    out = jnp.einsum('bnhgqj,bnjhd->bnqhgd', probs.astype(vb.dtype), vb).reshape(bsz, seq, SWA_WIDTH)
    return (out * jax.nn.silu(gate)) @ w_out


def conv_mixer(h, w_in, conv_kernel, w_out):
    seq = h.shape[1]
    bg, cg, u, gate = jnp.split(h @ w_in, 4, axis=-1)
    z = cg * u
    zp = jnp.pad(z, ((0, 0), (CONV_TAPS - 1, 0), (0, 0)))
    conv = zp[:, 0:seq] * conv_kernel[0]
    for tap in range(1, CONV_TAPS):
        conv = conv + zp[:, tap:tap + seq] * conv_kernel[tap]
    y = bg * conv
    return (y * jax.nn.silu(gate)) @ w_out


def _complex_linear_combine(left, right):
    a1r, a1i, b1r, b1i = left
    a2r, a2i, b2r, b2i = right
    return (a1r * a2r - a1i * a2i, a1r * a2i + a1i * a2r,
            a2r * b1r - a2i * b1i + b2r, a2r * b1i + a2i * b1r + b2i)


def ssm_mixer(h, w_in, lam_re, lam_im, log_dt, b_re, b_im, c_re, c_im, d_skip, w_glu, b_glu, w_out):
    f32 = jnp.float32
    bsz, seq, _ = h.shape
    u, gate = jnp.split(h @ w_in, 2, axis=-1)
    u32 = u.astype(f32)
    ug = u32.reshape(bsz, seq, SSM_GROUPS, SSM_GROUP)
    dt = jnp.exp(log_dt.astype(f32))[:, None]
    lr = lam_re.astype(f32)
    li = lam_im.astype(f32)
    mag = jnp.exp(lr * dt)
    ab_re = mag * jnp.cos(li * dt)
    ab_im = mag * jnp.sin(li * dt)
    den = lr * lr + li * li
    nr = ab_re - 1.0
    coef_re = ((nr * lr + ab_im * li) / den)[..., None]
    coef_im = ((ab_im * lr - nr * li) / den)[..., None]
    br = b_re.astype(f32)
    bi = b_im.astype(f32)
    bb_re = coef_re * br - coef_im * bi
    bb_im = coef_re * bi + coef_im * br
    bu_re = jnp.einsum('bsgc,gnc->bsgn', ug, bb_re)
    bu_im = jnp.einsum('bsgc,gnc->bsgn', ug, bb_im)
    a_re = jnp.broadcast_to(ab_re[None, None], (1, seq, SSM_GROUPS, SSM_STATE))
    a_im = jnp.broadcast_to(ab_im[None, None], (1, seq, SSM_GROUPS, SSM_STATE))
    _, _, s_re, s_im = lax.associative_scan(_complex_linear_combine, (a_re, a_im, bu_re, bu_im), axis=1)
    y = jnp.einsum('bsgn,gcn->bsgc', s_re, c_re.astype(f32)) - jnp.einsum('bsgn,gcn->bsgc', s_im, c_im.astype(f32))
    y = y.reshape(bsz, seq, SSM_WIDTH) + d_skip.astype(f32) * u32
    y = jax.nn.gelu(y)
    ga, gb = jnp.split(y @ w_glu.astype(f32) + b_glu.astype(f32), 2, axis=-1)
    y = (ga * jax.nn.sigmoid(gb)).astype(h.dtype)
    return (y * jax.nn.silu(gate)) @ w_out


def fox_mixer(h, w_in, w_fg, b_fg, w_out):
    f32 = jnp.float32
    bsz, seq, _ = h.shape
    nb = seq // BLOCK
    q, k, v, gate = jnp.split(h @ w_in, 4, axis=-1)
    q = q.reshape(bsz, nb, BLOCK, FOX_HEADS, FOX_HEAD_DIM).transpose(1, 0, 2, 3, 4)
    k = k.reshape(bsz, seq, FOX_HEADS, FOX_HEAD_DIM)
    v = v.reshape(bsz, seq, FOX_HEADS, FOX_HEAD_DIM)
    log_f = jax.nn.log_sigmoid((h @ w_fg).astype(f32) + b_fg.astype(f32))
    csum = jnp.cumsum(log_f, axis=1)
    c_keys = csum.transpose(0, 2, 1)[:, :, None, :]
    c_q = csum.reshape(bsz, nb, BLOCK, FOX_HEADS).transpose(1, 0, 3, 2)
    kpos = jnp.arange(seq)
    neg = jnp.finfo(f32).min

    def block(args):
        n, qn, cn = args
        s = jnp.einsum('bqhd,bkhd->bhqk', qn, k).astype(f32) * (FOX_HEAD_DIM ** -0.5)
        s = s + cn[..., None] - c_keys
        qpos = n * BLOCK + jnp.arange(BLOCK)
        s = jnp.where(kpos[None, :] <= qpos[:, None], s, neg)
        pr = jax.nn.softmax(s, axis=-1)
        return jnp.einsum('bhqk,bkhd->bqhd', pr.astype(v.dtype), v)

    out = lax.map(block, (jnp.arange(nb), q, c_q))
    out = out.transpose(1, 0, 2, 3, 4).reshape(bsz, seq, FOX_WIDTH)
    return (out * jax.nn.silu(gate)) @ w_out


def setup_inputs(seed: int = 0) -> dict:
    key = jax.random.key(seed)
    ks = iter(jax.random.split(key, 40))
    f32 = jnp.float32
    nrm = lambda shape, scale: jax.random.normal(next(ks), shape, f32) * scale
    na, nc, ns, nf = (n_layers_of(m) for m in range(N_MIXERS))
    return {
        'x': nrm((BATCH, SEQ, D_MODEL), 1.0),
        'p': nrm((DEPTH, BATCH, SEQ, PLE_DIM), 1.0),
        'norm_g': 1.0 + nrm((DEPTH, D_MODEL), 0.01),
        'final_g': 1.0 + nrm((D_MODEL,), 0.01),
        'rel_bias': nrm((REL_BUCKETS, SWA_HEADS), 0.5),
        'swa_w_in': nrm((na, D_MODEL, 2 * SWA_WIDTH + 2 * SWA_KV_WIDTH), D_MODEL ** -0.5),
        'swa_w_out': nrm((na, SWA_WIDTH, D_MODEL), SWA_WIDTH ** -0.5),
        'swa_sinks': nrm((na, SWA_HEADS), 1.0),
        'conv_w_in': nrm((nc, D_MODEL, 4 * CONV_WIDTH), D_MODEL ** -0.5),
        'conv_kernel': nrm((nc, CONV_TAPS, CONV_WIDTH), CONV_TAPS ** -0.5),
        'conv_w_out': nrm((nc, CONV_WIDTH, D_MODEL), CONV_WIDTH ** -0.5),
        'ssm_w_in': nrm((ns, D_MODEL, 2 * SSM_WIDTH), D_MODEL ** -0.5),
        'ssm_lam_re': -0.5 + nrm((ns, SSM_GROUPS, SSM_STATE), 0.01),
        'ssm_lam_im': jnp.broadcast_to(math.pi * jnp.arange(SSM_STATE, dtype=f32), (ns, SSM_GROUPS, SSM_STATE)) + nrm((ns, SSM_GROUPS, SSM_STATE), 0.01),
        'ssm_log_dt': jax.random.uniform(next(ks), (ns, SSM_GROUPS), f32, math.log(DT_MIN), math.log(DT_MAX)),
        'ssm_b_re': nrm((ns, SSM_GROUPS, SSM_STATE, SSM_GROUP), (2 * SSM_GROUP) ** -0.5),
        'ssm_b_im': nrm((ns, SSM_GROUPS, SSM_STATE, SSM_GROUP), (2 * SSM_GROUP) ** -0.5),
        'ssm_c_re': nrm((ns, SSM_GROUPS, SSM_GROUP, SSM_STATE), SSM_STATE ** -0.5),
        'ssm_c_im': nrm((ns, SSM_GROUPS, SSM_GROUP, SSM_STATE), SSM_STATE ** -0.5),
        'ssm_d': nrm((ns, SSM_WIDTH), 1.0),
        'ssm_w_glu': nrm((ns, SSM_WIDTH, 2 * SSM_WIDTH), SSM_WIDTH ** -0.5),
        'ssm_b_glu': nrm((ns, 2 * SSM_WIDTH), 0.01),
        'ssm_w_out': nrm((ns, SSM_WIDTH, D_MODEL), SSM_WIDTH ** -0.5),
        'fox_w_in': nrm((nf, D_MODEL, 4 * FOX_WIDTH), D_MODEL ** -0.5),
        'fox_w_fg': nrm((nf, D_MODEL, FOX_HEADS), D_MODEL ** -0.5),
        'fox_b_fg': jax.random.uniform(next(ks), (nf, FOX_HEADS), f32, 1.0, 5.0),
        'fox_w_out': nrm((nf, FOX_WIDTH, D_MODEL), FOX_WIDTH ** -0.5),
        'ple_proj': nrm((DEPTH, PLE_DIM, D_MODEL), PLE_DIM ** -0.5),
        'ple_norm': 1.0 + nrm((DEPTH, D_MODEL), 0.01),
        'ple_gate': nrm((DEPTH, D_MODEL, D_MODEL), D_MODEL ** -0.5),
    }


def reference(x, p, norm_g, final_g, rel_bias, swa_w_in, swa_w_out, swa_sinks, conv_w_in, conv_kernel, conv_w_out, ssm_w_in, ssm_lam_re, ssm_lam_im, ssm_log_dt, ssm_b_re, ssm_b_im, ssm_c_re, ssm_c_im, ssm_d, ssm_w_glu, ssm_b_glu, ssm_w_out, fox_w_in, fox_w_fg, fox_b_fg, fox_w_out, ple_proj, ple_norm, ple_gate):
    for i in range(DEPTH):
        mixer, j = i % N_MIXERS, i // N_MIXERS
        hn = rmsnorm(x, norm_g[i])
        if mixer == 0:
            y = swa_mixer(hn, swa_w_in[j], swa_w_out[j], swa_sinks[j], rel_bias)
        elif mixer == 1:
            y = conv_mixer(hn, conv_w_in[j], conv_kernel[j], conv_w_out[j])
        elif mixer == 2:
            y = ssm_mixer(hn, ssm_w_in[j], ssm_lam_re[j], ssm_lam_im[j], ssm_log_dt[j], ssm_b_re[j], ssm_b_im[j], ssm_c_re[j], ssm_c_im[j], ssm_d[j], ssm_w_glu[j], ssm_b_glu[j], ssm_w_out[j])
        else:
            y = fox_mixer(hn, fox_w_in[j], fox_w_fg[j], fox_b_fg[j], fox_w_out[j])
        x = x + y
        emb = p[i] @ ple_proj[i]
        g = jax.nn.sigmoid(rmsnorm(x, ple_norm[i]) @ ple_gate[i])
        x = x + emb * g
    return rmsnorm(x, final_g)
```

```python
import functools
import math

import numpy as np
import jax
import jax.numpy as jnp
from jax import lax
from jax.experimental import pallas as pl
from jax.experimental.pallas import tpu as pltpu

F32 = jnp.float32
BF16 = jnp.bfloat16

EPS = 1e-6
N_MIXERS = 4
PLE_DIM = 256

SWA_HEADS = 32
SWA_KV_HEADS = 4
SWA_GROUP = SWA_HEADS // SWA_KV_HEADS
HEAD_DIM = 64
SWA_BLOCK = 128
WINDOW = 128
REL_BUCKETS = 32
REL_MAX_DIST = 128

CONV_TAPS = 3

SSM_GROUP = 16
SSM_STATE = 64
SSM_CHUNK = 16

FOX_HEADS = 32

LANES = 128
VMEM_LIMIT = 48 * 1024 * 1024

NEG = float(jnp.finfo(jnp.float32).min)


def _params(n_axes):
    return pltpu.CompilerParams(dimension_semantics=("arbitrary",) * n_axes, vmem_limit_bytes=VMEM_LIMIT)


def _dot(a, b):
    return jnp.dot(a, b, preferred_element_type=F32)


def _dot_nt(a, b):
    return lax.dot_general(a, b, (((1,), (1,)), ((), ())), preferred_element_type=F32)


def _rmsnorm_rows(x, g):
    return x * lax.rsqrt(jnp.mean(x * x, axis=-1, keepdims=True) + EPS) * g


def _silu(x):
    return x * jax.nn.sigmoid(x)


ROW_CHUNK = 64


def _for_row_chunks(n_rows, fn):
    def body(c, carry):
        fn(pl.ds(pl.multiple_of(c * ROW_CHUNK, ROW_CHUNK), ROW_CHUNK))
        return carry
    lax.fori_loop(0, n_rows // ROW_CHUNK, body, 0)


def _norm_matmul_kernel(x_ref, g_ref, w_ref, o_ref, a_ref):
    @pl.when(pl.program_id(1) == 0)
    def _():
        def chunk(rows):
            a_ref[rows, :] = _rmsnorm_rows(x_ref[rows, :], g_ref[...]).astype(BF16)
        _for_row_chunks(a_ref.shape[0], chunk)

    o_ref[...] = _dot(a_ref[...], w_ref[...]).astype(o_ref.dtype)


def _norm_matmul(x, g, w, out_dtype, tm=512, tn=512):
    m, k = x.shape
    n = w.shape[1]
    tm, tn = min(tm, m), min(tn, n)
    return pl.pallas_call(
        _norm_matmul_kernel,
        grid=(m // tm, n // tn),
        in_specs=[
            pl.BlockSpec((tm, k), lambda i, j: (i, 0)),
            pl.BlockSpec((1, k), lambda i, j: (0, 0)),
            pl.BlockSpec((k, tn), lambda i, j: (0, j)),
        ],
        out_specs=pl.BlockSpec((tm, tn), lambda i, j: (i, j)),
        out_shape=jax.ShapeDtypeStruct((m, n), out_dtype),
        scratch_shapes=[pltpu.VMEM((tm, k), BF16)],
        compiler_params=_params(2),
        name="norm_matmul",
    )(x, g.reshape(1, k), w)


def _out_proj_gated_kernel(a_ref, gate_ref, w_ref, x_ref, o_ref, s_ref):
    @pl.when(pl.program_id(1) == 0)
    def _():
        def chunk(rows):
            s_ref[rows, :] = (a_ref[rows, :].astype(F32) * _silu(gate_ref[rows, :])).astype(BF16)
        _for_row_chunks(s_ref.shape[0], chunk)

    o_ref[...] = x_ref[...] + _dot(s_ref[...], w_ref[...])


def _out_proj_kernel(a_ref, w_ref, x_ref, o_ref):
    o_ref[...] = x_ref[...] + _dot(a_ref[...], w_ref[...])


def _out_proj(a, gate_arr, gate_col_block, w, x, tm=512, tn=512):
    m, k = a.shape
    n = w.shape[1]
    tm, tn = min(tm, m), min(tn, n)
    a_spec = pl.BlockSpec((tm, k), lambda i, j: (i, 0))
    w_spec = pl.BlockSpec((k, tn), lambda i, j: (0, j))
    x_spec = pl.BlockSpec((tm, tn), lambda i, j: (i, j))
    common = dict(
        grid=(m // tm, n // tn),
        out_specs=pl.BlockSpec((tm, tn), lambda i, j: (i, j)),
        out_shape=jax.ShapeDtypeStruct((m, n), F32),
        compiler_params=_params(2),
    )
    if gate_arr is None:
        return pl.pallas_call(_out_proj_kernel, in_specs=[a_spec, w_spec, x_spec], name="out_proj", **common)(a, w, x)
    gate_spec = pl.BlockSpec((tm, k), lambda i, j: (i, gate_col_block))
    return pl.pallas_call(
        _out_proj_gated_kernel,
        in_specs=[a_spec, gate_spec, w_spec, x_spec],
        scratch_shapes=[pltpu.VMEM((tm, k), BF16)],
        name="out_proj_gated",
        **common,
    )(a, gate_arr, w, x)


def _ple_kernel(x_ref, g_ref, wg_ref, p_ref, wp_ref, xt_ref, o_ref, a_ref):
    @pl.when(pl.program_id(1) == 0)
    def _():
        def chunk(rows):
            a_ref[rows, :] = _rmsnorm_rows(x_ref[rows, :], g_ref[...]).astype(BF16)
        _for_row_chunks(a_ref.shape[0], chunk)

    emb = _dot(p_ref[...].astype(BF16), wp_ref[...])
    gate = jax.nn.sigmoid(_dot(a_ref[...], wg_ref[...]))
    o_ref[...] = xt_ref[...] + emb * gate


def _ple(x, g, w_gate, p, w_proj, tm=512, tn=512):
    m, k = x.shape
    n = w_gate.shape[1]
    pd = p.shape[1]
    tm, tn = min(tm, m), min(tn, n)
    return pl.pallas_call(
        _ple_kernel,
        grid=(m // tm, n // tn),
        in_specs=[
            pl.BlockSpec((tm, k), lambda i, j: (i, 0)),
            pl.BlockSpec((1, k), lambda i, j: (0, 0)),
            pl.BlockSpec((k, tn), lambda i, j: (0, j)),
            pl.BlockSpec((tm, pd), lambda i, j: (i, 0)),
            pl.BlockSpec((pd, tn), lambda i, j: (0, j)),
            pl.BlockSpec((tm, tn), lambda i, j: (i, j)),
        ],
        out_specs=pl.BlockSpec((tm, tn), lambda i, j: (i, j)),
        out_shape=jax.ShapeDtypeStruct((m, n), F32),
        scratch_shapes=[pltpu.VMEM((tm, k), BF16)],
        compiler_params=_params(2),
        name="ple",
    )(x, g.reshape(1, k), w_gate, p, w_proj, x)


def _final_norm_kernel(x_ref, g_ref, o_ref):
    o_ref[...] = _rmsnorm_rows(x_ref[...], g_ref[...])


def _final_norm(x, g, tm=256):
    m, k = x.shape
    tm = min(tm, m)
    return pl.pallas_call(
        _final_norm_kernel,
        grid=(m // tm,),
        in_specs=[pl.BlockSpec((tm, k), lambda i: (i, 0)), pl.BlockSpec((1, k), lambda i: (0, 0))],
        out_specs=pl.BlockSpec((tm, k), lambda i: (i, 0)),
        out_shape=jax.ShapeDtypeStruct((m, k), F32),
        compiler_params=_params(1),
        name="final_norm",
    )(x, g.reshape(1, k))


def _t5_bucket(dist):
    max_exact = REL_BUCKETS // 2
    d = np.maximum(dist, 1).astype(np.float32)
    large = max_exact + (np.log(d / max_exact) / np.log(REL_MAX_DIST / max_exact) * (REL_BUCKETS - max_exact)).astype(np.int32)
    large = np.minimum(large, REL_BUCKETS - 1)
    return np.where(dist < max_exact, dist, large).astype(np.int32)


def _swa_bias_table(rel_bias):
    qi = np.arange(SWA_BLOCK)[:, None]
    kj = np.arange(2 * SWA_BLOCK)[None, :]
    bucket = _t5_bucket(np.clip(qi + SWA_BLOCK - kj, 0, None))
    return rel_bias.astype(F32)[bucket].transpose(2, 0, 1)


def _swa_kernel(sink_ref, q_ref, kp_ref, kc_ref, vp_ref, vc_ref, bias_ref, o_ref):
    n = pl.program_id(1)
    blk = SWA_BLOCK
    lane = lax.broadcasted_iota(jnp.int32, (1, LANES), 1)
    qi = lax.broadcasted_iota(jnp.int32, (blk, blk), 0)
    kj = lax.broadcasted_iota(jnp.int32, (blk, blk), 1)
    mask_prev = jnp.logical_and(kj > qi, n > 0)
    mask_cur = kj <= qi
    for kvh in range(SWA_KV_HEADS):
        cols = slice(kvh * LANES, (kvh + 1) * LANES)
        kp, kc, vp, vc = kp_ref[:, cols], kc_ref[:, cols], vp_ref[:, cols], vc_ref[:, cols]
        for pair in range(SWA_GROUP // 2):
            qcols = slice((kvh * SWA_GROUP // 2 + pair) * LANES, (kvh * SWA_GROUP // 2 + pair + 1) * LANES)
            q_pair = q_ref[:, qcols]
            halves = []
            for hh in range(2):
                head = kvh * SWA_GROUP + pair * 2 + hh
                qm = jnp.where((lane // HEAD_DIM) == hh, q_pair, jnp.zeros_like(q_pair))
                sink = sink_ref[head]
                s_p = _dot_nt(qm, kp) * (HEAD_DIM ** -0.5) + bias_ref[head, :, :blk]
                s_c = _dot_nt(qm, kc) * (HEAD_DIM ** -0.5) + bias_ref[head, :, blk:]
                s_p = jnp.where(mask_prev, s_p, NEG)
                s_c = jnp.where(mask_cur, s_c, NEG)
                m = jnp.maximum(jnp.maximum(s_p.max(axis=-1, keepdims=True), s_c.max(axis=-1, keepdims=True)), sink)
                e_p = jnp.exp(s_p - m)
                e_c = jnp.exp(s_c - m)
                den = e_p.sum(axis=-1, keepdims=True) + e_c.sum(axis=-1, keepdims=True) + jnp.exp(sink - m)
                acc = _dot(e_p.astype(BF16), vp) + _dot(e_c.astype(BF16), vc)
                halves.append(acc / den)
            o_ref[:, qcols] = jnp.where(lane < HEAD_DIM, halves[0], halves[1]).astype(o_ref.dtype)


def _swa_attention(qkv, sinks, bias, bsz, seq):
    nb = seq // SWA_BLOCK
    width = SWA_HEADS * HEAD_DIM
    kvw = SWA_KV_HEADS * LANES
    k_blk = width // kvw
    cur = lambda b, n: b * nb + n
    prev = lambda b, n: b * nb + jnp.maximum(n - 1, 0)
    return pl.pallas_call(
        _swa_kernel,
        grid=(bsz, nb),
        in_specs=[
            pl.BlockSpec(memory_space=pltpu.SMEM),
            pl.BlockSpec((SWA_BLOCK, width), lambda b, n: (cur(b, n), 0)),
            pl.BlockSpec((SWA_BLOCK, kvw), lambda b, n: (prev(b, n), k_blk)),
            pl.BlockSpec((SWA_BLOCK, kvw), lambda b, n: (cur(b, n), k_blk)),
            pl.BlockSpec((SWA_BLOCK, kvw), lambda b, n: (prev(b, n), k_blk + 1)),
            pl.BlockSpec((SWA_BLOCK, kvw), lambda b, n: (cur(b, n), k_blk + 1)),
            pl.BlockSpec((SWA_HEADS, SWA_BLOCK, 2 * SWA_BLOCK), lambda b, n: (0, 0, 0)),
        ],
        out_specs=pl.BlockSpec((SWA_BLOCK, width), lambda b, n: (cur(b, n), 0)),
        out_shape=jax.ShapeDtypeStruct((bsz * seq, width), BF16),
        compiler_params=_params(2),
        name="swa_attention",
    )(sinks.astype(F32), qkv, qkv, qkv, qkv, qkv, bias)


def _dup_heads(w, n_heads):
    k = w.shape[0]
    w = w.reshape(k, n_heads, 1, HEAD_DIM)
    return jnp.broadcast_to(w, (k, n_heads, 2, HEAD_DIM)).reshape(k, n_heads * 2 * HEAD_DIM)


def _swa_layer(x, g, w_in, w_out, sinks, rel_bias, bsz, seq):
    width = SWA_HEADS * HEAD_DIM
    kvw = SWA_KV_HEADS * HEAD_DIM
    w_q, w_k, w_v, w_g = (w_in[:, :width], w_in[:, width:width + kvw],
                          w_in[:, width + kvw:width + 2 * kvw], w_in[:, width + 2 * kvw:])
    w_qkv = jnp.concatenate([w_q, _dup_heads(w_k, SWA_KV_HEADS), _dup_heads(w_v, SWA_KV_HEADS)], axis=1).astype(BF16)
    qkv = _norm_matmul(x, g, w_qkv, BF16)
    gate = _norm_matmul(x, g, w_g.astype(BF16), F32)
    attn = _swa_attention(qkv, sinks, _swa_bias_table(rel_bias), bsz, seq)
    return _out_proj(attn, gate, 0, w_out.astype(BF16), x)


HALO = 8


def _conv_kernel(x_ref, g_ref, wb_ref, wc_ref, wu_ref, wg_ref, ck_ref, o_ref, a_ref, z_ref, carry_ref, *, tiles_per_seq):
    i, j = pl.program_id(0), pl.program_id(1)
    tm = a_ref.shape[0]

    @pl.when(j == 0)
    def _():
        def chunk(rows):
            a_ref[rows, :] = _rmsnorm_rows(x_ref[rows, :], g_ref[...]).astype(BF16)
        _for_row_chunks(tm, chunk)

    a = a_ref[...]
    z = _dot(a, wc_ref[...]) * _dot(a, wu_ref[...])
    first = (i % tiles_per_seq) == 0

    @pl.when(first)
    def _():
        z_ref[:HALO, :] = jnp.zeros((HALO, z_ref.shape[1]), F32)

    @pl.when(jnp.logical_not(first))
    def _():
        z_ref[:HALO, :] = carry_ref[j]

    z_ref[HALO:, :] = z
    carry_ref[j] = z[tm - HALO:, :]
    conv = z_ref[HALO - 2:HALO - 2 + tm, :] * ck_ref[0:1, :]
    conv = conv + z_ref[HALO - 1:HALO - 1 + tm, :] * ck_ref[1:2, :]
    conv = conv + z * ck_ref[2:3, :]
    y = _dot(a, wb_ref[...]) * conv
    o_ref[...] = (y * _silu(_dot(a, wg_ref[...]))).astype(o_ref.dtype)


def _conv_mixer(x, g, w_in, conv_kernel, seq, tm=512, tn=512):
    m, k = x.shape
    width = w_in.shape[1] // 4
    tm, tn = min(tm, seq), min(tn, width)
    nt = width // tn
    w_spec = lambda q: pl.BlockSpec((k, tn), lambda i, j: (0, q * nt + j))
    return pl.pallas_call(
        functools.partial(_conv_kernel, tiles_per_seq=seq // tm),
        grid=(m // tm, nt),
        in_specs=[
            pl.BlockSpec((tm, k), lambda i, j: (i, 0)),
            pl.BlockSpec((1, k), lambda i, j: (0, 0)),
            w_spec(0), w_spec(1), w_spec(2), w_spec(3),
            pl.BlockSpec((CONV_TAPS, tn), lambda i, j: (0, j)),
        ],
        out_specs=pl.BlockSpec((tm, tn), lambda i, j: (i, j)),
        out_shape=jax.ShapeDtypeStruct((m, width), BF16),
        scratch_shapes=[
            pltpu.VMEM((tm, k), BF16),
            pltpu.VMEM((HALO + tm, tn), F32),
            pltpu.VMEM((nt, HALO, tn), F32),
        ],
        compiler_params=_params(2),
        name="conv_mixer",
    )(x, g.reshape(1, k), w_in, w_in, w_in, w_in, conv_kernel.astype(F32))


def _conv_layer(x, g, w_in, conv_kernel, w_out, seq):
    a = _conv_mixer(x, g, w_in.astype(BF16), conv_kernel, seq)
    return _out_proj(a, None, 0, w_out.astype(BF16), x)


def _ssm_operators(lam_re, lam_im, log_dt, b_re, b_im, c_re, c_im, d_skip):
    hi = lax.Precision.HIGHEST
    n_groups = lam_re.shape[0]
    L, C, N = SSM_CHUNK, SSM_GROUP, SSM_STATE
    dt = jnp.exp(log_dt.astype(F32))[:, None]
    lr, li = lam_re.astype(F32), lam_im.astype(F32)
    mag = jnp.exp(lr * dt)
    ab_re, ab_im = mag * jnp.cos(li * dt), mag * jnp.sin(li * dt)
    den = lr * lr + li * li
    nr = ab_re - 1.0
    coef_re = ((nr * lr + ab_im * li) / den)[..., None]
    coef_im = ((ab_im * lr - nr * li) / den)[..., None]
    br, bi = b_re.astype(F32), b_im.astype(F32)
    bb_re = coef_re * br - coef_im * bi
    bb_im = coef_re * bi + coef_im * br
    cr, ci = c_re.astype(F32), c_im.astype(F32)
    p_re, p_im = [jnp.ones_like(ab_re)], [jnp.zeros_like(ab_im)]
    for _ in range(L):
        p_re, p_im = (p_re + [p_re[-1] * ab_re - p_im[-1] * ab_im], p_im + [p_re[-1] * ab_im + p_im[-1] * ab_re])
    p_re, p_im = jnp.stack(p_re), jnp.stack(p_im)
    w_re = cr[None] * p_re[:L, :, None, :] - ci[None] * p_im[:L, :, None, :]
    w_im = cr[None] * p_im[:L, :, None, :] + ci[None] * p_re[:L, :, None, :]
    lag_k = (jnp.einsum('tgcn,gnd->tgdc', w_re, bb_re, precision=hi)
             - jnp.einsum('tgcn,gnd->tgdc', w_im, bb_im, precision=hi))
    r = np.arange(L)
    lag = r[None, :] - r[:, None]
    toe = jnp.where((lag >= 0)[:, :, None, None, None], lag_k[np.clip(lag, 0, None)], 0.0)
    toe = toe.transpose(2, 0, 3, 1, 4).reshape(n_groups // 2, 2, L * C, L * C)
    q_re, q_im = p_re[:L][::-1], p_im[:L][::-1]
    bc_re = q_re[:, :, :, None] * bb_re[None] - q_im[:, :, :, None] * bb_im[None]
    bc_im = q_re[:, :, :, None] * bb_im[None] + q_im[:, :, :, None] * bb_re[None]
    to_rows = lambda t: t.transpose(1, 0, 3, 2).reshape(n_groups, L * C, N)
    o_re = cr[None] * p_re[1:, :, None, :] - ci[None] * p_im[1:, :, None, :]
    o_im = cr[None] * p_im[1:, :, None, :] + ci[None] * p_re[1:, :, None, :]
    to_cols = lambda t: t.transpose(1, 3, 0, 2).reshape(n_groups, N, L * C)

    def pair_rows(t):
        t = t.reshape(n_groups // 2, 2, L * C, N)
        z = jnp.zeros_like(t[:, 0])
        return jnp.concatenate([jnp.concatenate([t[:, 0], z], axis=2), jnp.concatenate([z, t[:, 1]], axis=2)], axis=1)

    def pair_cols(t):
        t = t.reshape(n_groups // 2, 2, N, L * C)
        z = jnp.zeros_like(t[:, 0])
        return jnp.concatenate([jnp.concatenate([t[:, 0], z], axis=2), jnp.concatenate([z, t[:, 1]], axis=2)], axis=1)

    pair_vec = lambda t: t.reshape(n_groups // 2, 1, 2 * N)
    d_pair = jnp.broadcast_to(d_skip.astype(F32).reshape(n_groups // 2, 2, 1, C), (n_groups // 2, 2, L, C))
    return dict(
        toe=toe.astype(BF16),
        bc_re=pair_rows(to_rows(bc_re)).astype(BF16), bc_im=pair_rows(to_rows(bc_im)).astype(BF16),
        oc_re=pair_cols(to_cols(o_re)).astype(BF16), oc_im_neg=pair_cols(to_cols(-o_im)).astype(BF16),
        al_re=pair_vec(p_re[L]), al_im=pair_vec(p_im[L]),
        d=d_pair.reshape(n_groups // 2, 1, 2 * L * C),
    )


def _ssm_kernel(u_ref, toe_ref, bcr_ref, bci_ref, ocr_ref, oci_ref, alr_ref, ali_ref, d_ref, o_ref,
                zr_ref, zi_ref, hr_ref, hi_ref, *, bsz):
    rows = u_ref.shape[0]
    chunks = rows // bsz
    half = SSM_CHUNK * SSM_GROUP
    u = u_ref[...]
    ub = u.astype(BF16)
    zr_ref[...] = _dot(ub, bcr_ref[...])
    zi_ref[...] = _dot(ub, bci_ref[...])
    a_re, a_im = alr_ref[...], ali_ref[...]

    def step(k, carry):
        new = []
        for b in range(bsz):
            h_re, h_im = carry[2 * b], carry[2 * b + 1]
            row = pl.ds(b * chunks + k, 1)
            hr_ref[row, :] = h_re
            hi_ref[row, :] = h_im
            new.append(a_re * h_re - a_im * h_im + zr_ref[row, :])
            new.append(a_re * h_im + a_im * h_re + zi_ref[row, :])
        return tuple(new)

    zero = jnp.zeros((1, a_re.shape[1]), F32)
    lax.fori_loop(0, chunks, step, (zero,) * (2 * bsz))

    y = jnp.concatenate([_dot(ub[:, :half], toe_ref[0]), _dot(ub[:, half:], toe_ref[1])], axis=1)
    y = y + _dot(hr_ref[...].astype(BF16), ocr_ref[...]) + _dot(hi_ref[...].astype(BF16), oci_ref[...])
    y = y + d_ref[...] * u
    o_ref[...] = jax.nn.gelu(y).astype(o_ref.dtype)


def _ssm_core(u, ops, bsz, seq):
    m, width = u.shape
    L, C = SSM_CHUNK, SSM_GROUP
    n_pairs = width // (2 * C)
    chunks = seq // L
    rows = bsz * chunks
    pw = 2 * L * C
    uc = u.reshape(bsz, chunks, L, n_pairs, 2, C).transpose(3, 0, 1, 4, 2, 5).reshape(n_pairs, rows, pw)
    n_state = 2 * SSM_STATE
    blk = lambda *shape: pl.BlockSpec((None,) + shape, lambda p: (p,) + (0,) * len(shape))
    yc = pl.pallas_call(
        functools.partial(_ssm_kernel, bsz=bsz),
        grid=(n_pairs,),
        in_specs=[
            blk(rows, pw), blk(2, L * C, L * C), blk(pw, n_state), blk(pw, n_state),
            blk(n_state, pw), blk(n_state, pw), blk(1, n_state), blk(1, n_state), blk(1, pw),
        ],
        out_specs=blk(rows, pw),
        out_shape=jax.ShapeDtypeStruct((n_pairs, rows, pw), BF16),
        scratch_shapes=[pltpu.VMEM((rows, n_state), F32)] * 4,
        compiler_params=_params(1),
        name="ssm_core",
    )(uc, ops['toe'], ops['bc_re'], ops['bc_im'], ops['oc_re'], ops['oc_im_neg'], ops['al_re'], ops['al_im'], ops['d'])
    return yc.reshape(n_pairs, bsz, chunks, 2, L, C).transpose(1, 2, 4, 0, 3, 5).reshape(m, width)


def _glu_kernel(y_ref, wa_ref, wb_ref, ba_ref, bb_ref, gate_ref, o_ref):
    y = y_ref[...]
    ga = _dot(y, wa_ref[...]) + ba_ref[...]
    gb = _dot(y, wb_ref[...]) + bb_ref[...]
    o_ref[...] = ((ga * jax.nn.sigmoid(gb)) * _silu(gate_ref[...])).astype(o_ref.dtype)


def _glu(y, w_glu, b_glu, proj, tm=512, tn=512):
    m, k = y.shape
    width = w_glu.shape[1] // 2
    tm, tn = min(tm, m), min(tn, width)
    nt = width // tn
    return pl.pallas_call(
        _glu_kernel,
        grid=(m // tm, nt),
        in_specs=[
            pl.BlockSpec((tm, k), lambda i, j: (i, 0)),
            pl.BlockSpec((k, tn), lambda i, j: (0, j)),
            pl.BlockSpec((k, tn), lambda i, j: (0, nt + j)),
            pl.BlockSpec((1, tn), lambda i, j: (0, j)),
            pl.BlockSpec((1, tn), lambda i, j: (0, nt + j)),
            pl.BlockSpec((tm, tn), lambda i, j: (i, nt + j)),
        ],
        out_specs=pl.BlockSpec((tm, tn), lambda i, j: (i, j)),
        out_shape=jax.ShapeDtypeStruct((m, width), BF16),
        compiler_params=_params(2),
        name="ssm_glu",
    )(y, w_glu, w_glu, b_glu, b_glu, proj)


def _ssm_layer(x, g, w_in, lam_re, lam_im, log_dt, b_re, b_im, c_re, c_im, d_skip, w_glu, b_glu, w_out, bsz, seq):
    width = w_in.shape[1] // 2
    proj = _norm_matmul(x, g, w_in.astype(BF16), F32)
    ops = _ssm_operators(lam_re, lam_im, log_dt, b_re, b_im, c_re, c_im, d_skip)
    y = _ssm_core(proj[:, :width], ops, bsz, seq)
    a = _glu(y, w_glu.astype(BF16), b_glu.astype(F32).reshape(1, -1), proj)
    return _out_proj(a, None, 0, w_out.astype(BF16), x)


CUM_BLOCK = 128


def _split3(x):
    x1 = x.astype(BF16)
    r1 = x - x1.astype(F32)
    x2 = r1.astype(BF16)
    x3 = (r1 - x2.astype(F32)).astype(BF16)
    return x1, x2, x3


def _forget_cumsum_kernel(z_ref, b_ref, o_ref):
    n_blocks = z_ref.shape[0] // CUM_BLOCK
    ri = lax.broadcasted_iota(jnp.int32, (CUM_BLOCK, CUM_BLOCK), 0)
    ci = lax.broadcasted_iota(jnp.int32, (CUM_BLOCK, CUM_BLOCK), 1)
    tri = jnp.where(ci <= ri, 1.0, 0.0).astype(BF16)

    def body(i, carry):
        rows = pl.ds(pl.multiple_of(i * CUM_BLOCK, CUM_BLOCK), CUM_BLOCK)
        z = z_ref[rows, :] + b_ref[...]
        log_f = jnp.minimum(z, 0.0) - jnp.log1p(jnp.exp(-jnp.abs(z)))
        x1, x2, x3 = _split3(log_f)
        c = _dot(tri, x1) + _dot(tri, x2) + _dot(tri, x3) + carry
        o_ref[rows, :] = c
        return c[CUM_BLOCK - 1:, :]

    lax.fori_loop(0, n_blocks, body, jnp.zeros((1, z_ref.shape[1]), F32))


def _forget_cumsum(z, b, bsz, seq):
    lanes = z.shape[1]
    return pl.pallas_call(
        _forget_cumsum_kernel,
        grid=(bsz,),
        in_specs=[pl.BlockSpec((seq, lanes), lambda i: (i, 0)), pl.BlockSpec((1, lanes), lambda i: (0, 0))],
        out_specs=pl.BlockSpec((seq, lanes), lambda i: (i, 0)),
        out_shape=jax.ShapeDtypeStruct(z.shape, F32),
        compiler_params=_params(1),
        name="forget_cumsum",
    )(z, b)


def _fox_kernel(q_ref, k_ref, v_ref, cq_ref, ck_ref, o_ref, *, tk):
    hp, qi = pl.program_id(1), pl.program_id(2)
    tq = q_ref.shape[0]
    q = q_ref[...]
    cq_tile = cq_ref[...]
    lane = lax.broadcasted_iota(jnp.int32, (1, LANES), 1)
    qpos = lax.broadcasted_iota(jnp.int32, (tq, tk), 0)
    kpos = lax.broadcasted_iota(jnp.int32, (tq, tk), 1)
    causal = kpos <= qpos
    halves = []
    for hh in range(2):
        qm = jnp.where((lane // HEAD_DIM) == hh, q, jnp.zeros_like(q))
        cq = jnp.sum(jnp.where(lane == 2 * hp + hh, cq_tile, 0.0), axis=-1, keepdims=True)

        def scores(kj):
            rows = pl.ds(pl.multiple_of(kj * tk, tk), tk)
            s = _dot_nt(qm, k_ref[rows, :]) * (HEAD_DIM ** -0.5)
            return s + cq - ck_ref[hh, pl.ds(kj, 1), :], rows

        def update(s, rows, carry):
            m, l, acc = carry
            m_new = jnp.maximum(m, s.max(axis=-1, keepdims=True))
            alpha = jnp.exp(m - m_new)
            p = jnp.exp(s - m_new)
            l = alpha * l + p.sum(axis=-1, keepdims=True)
            acc = alpha * acc + _dot(p.astype(BF16), v_ref[rows, :])
            return m_new, l, acc

        def body(kj, carry):
            s, rows = scores(kj)
            return update(s, rows, carry)

        init = (jnp.full((tq, 1), NEG, F32), jnp.zeros((tq, 1), F32), jnp.zeros((tq, LANES), F32))
        carry = lax.fori_loop(0, qi, body, init)
        s, rows = scores(qi)
        m, l, acc = update(jnp.where(causal, s, NEG), rows, carry)
        halves.append(acc / l)
    o_ref[...] = jnp.where(lane < HEAD_DIM, halves[0], halves[1]).astype(o_ref.dtype)


def _fox_attention(qkv, csum, ck, bsz, seq, t=512):
    width = FOX_HEADS * HEAD_DIM
    n_pairs = width // LANES
    t = min(t, seq)
    nq = seq // t
    return pl.pallas_call(
        functools.partial(_fox_kernel, tk=t),
        grid=(bsz, n_pairs, nq),
        in_specs=[
            pl.BlockSpec((t, LANES), lambda b, h, i: (b * nq + i, h)),
            pl.BlockSpec((seq, LANES), lambda b, h, i: (b, n_pairs + h)),
            pl.BlockSpec((seq, LANES), lambda b, h, i: (b, 2 * n_pairs + h)),
            pl.BlockSpec((t, LANES), lambda b, h, i: (b * nq + i, 0)),
            pl.BlockSpec((None, None, 2, nq, t), lambda b, h, i: (b, h, 0, 0, 0)),
        ],
        out_specs=pl.BlockSpec((t, LANES), lambda b, h, i: (b * nq + i, h)),
        out_shape=jax.ShapeDtypeStruct((bsz * seq, width), BF16),
        compiler_params=_params(3),
        name="fox_attention",
    )(qkv, qkv, qkv, csum, ck)


def _fox_layer(x, g, w_in, w_fg, b_fg, w_out, bsz, seq, t=512):
    width = FOX_HEADS * HEAD_DIM
    t = min(t, seq)
    qkv = _norm_matmul(x, g, w_in[:, :3 * width].astype(BF16), BF16)
    gate = _norm_matmul(x, g, w_in[:, 3 * width:].astype(BF16), F32)
    pad = LANES - FOX_HEADS
    z = _norm_matmul(x, g, jnp.pad(w_fg, ((0, 0), (0, pad))).astype(BF16), F32)
    csum = _forget_cumsum(z, jnp.pad(b_fg.astype(F32), (0, pad)).reshape(1, LANES), bsz, seq)
    ck = csum.reshape(bsz, seq, LANES)[:, :, :FOX_HEADS].transpose(0, 2, 1).reshape(bsz, FOX_HEADS // 2, 2, seq // t, t)
    attn = _fox_attention(qkv, csum, ck, bsz, seq, t)
    return _out_proj(attn, gate, 0, w_out.astype(BF16), x)


def kernel(x, p, norm_g, final_g, rel_bias, swa_w_in, swa_w_out, swa_sinks, conv_w_in, conv_kernel, conv_w_out, ssm_w_in, ssm_lam_re, ssm_lam_im, ssm_log_dt, ssm_b_re, ssm_b_im, ssm_c_re, ssm_c_im, ssm_d, ssm_w_glu, ssm_b_glu, ssm_w_out, fox_w_in, fox_w_fg, fox_b_fg, fox_w_out, ple_proj, ple_norm, ple_gate):
    bsz, seq, d_model = x.shape
    depth = p.shape[0]
    h = x.astype(F32).reshape(bsz * seq, d_model)
    for i in range(depth):
        mixer, j = i % N_MIXERS, i // N_MIXERS
        if mixer == 0:
            h = _swa_layer(h, norm_g[i], swa_w_in[j], swa_w_out[j], swa_sinks[j], rel_bias, bsz, seq)
        elif mixer == 1:
            h = _conv_layer(h, norm_g[i], conv_w_in[j], conv_kernel[j], conv_w_out[j], seq)
        elif mixer == 2:
            h = _ssm_layer(h, norm_g[i], ssm_w_in[j], ssm_lam_re[j], ssm_lam_im[j], ssm_log_dt[j], ssm_b_re[j],
                           ssm_b_im[j], ssm_c_re[j], ssm_c_im[j], ssm_d[j], ssm_w_glu[j], ssm_b_glu[j], ssm_w_out[j],
                           bsz, seq)
        else:
            h = _fox_layer(h, norm_g[i], fox_w_in[j], fox_w_fg[j], fox_b_fg[j], fox_w_out[j], bsz, seq)
        h = _ple(h, ple_norm[i], ple_gate[i].astype(BF16), p[i].reshape(bsz * seq, -1), ple_proj[i].astype(BF16))
    return _final_norm(h, final_g).reshape(bsz, seq, d_model).astype(x.dtype)
```

```python
import functools
import math

import numpy as np
import jax
import jax.numpy as jnp
from jax import lax
from jax.experimental import pallas as pl
from jax.experimental.pallas import tpu as pltpu

F32 = jnp.float32
BF16 = jnp.bfloat16

EPS = 1e-6
N_MIXERS = 4
PLE_DIM = 256

SWA_HEADS = 32
SWA_KV_HEADS = 4
SWA_GROUP = SWA_HEADS // SWA_KV_HEADS
HEAD_DIM = 64
SWA_BLOCK = 128
WINDOW = 128
REL_BUCKETS = 32
REL_MAX_DIST = 128

CONV_TAPS = 3

SSM_GROUP = 16
SSM_STATE = 64
SSM_CHUNK = 16

FOX_HEADS = 32

LANES = 128
GROUPS_PER_TILE = LANES // SSM_GROUP
VMEM_LIMIT = 48 * 1024 * 1024

NEG = float(jnp.finfo(jnp.float32).min)


def _params(n_axes):
    return pltpu.CompilerParams(dimension_semantics=("arbitrary",) * n_axes, vmem_limit_bytes=VMEM_LIMIT)


def _dot(a, b):
    return jnp.dot(a, b, preferred_element_type=F32)


def _dot_nt(a, b):
    return lax.dot_general(a, b, (((1,), (1,)), ((), ())), preferred_element_type=F32)


def _rmsnorm_rows(x, g):
    return x * lax.rsqrt(jnp.mean(x * x, axis=-1, keepdims=True) + EPS) * g


def _silu(x):
    return x * jax.nn.sigmoid(x)


ROW_CHUNK = 64


def _for_row_chunks(n_rows, fn):
    def body(c, carry):
        fn(pl.ds(pl.multiple_of(c * ROW_CHUNK, ROW_CHUNK), ROW_CHUNK))
        return carry
    lax.fori_loop(0, n_rows // ROW_CHUNK, body, 0)


def _norm_matmul_kernel(x_ref, g_ref, w_ref, o_ref, a_ref):
    @pl.when(pl.program_id(1) == 0)
    def _():
        def chunk(rows):
            a_ref[rows, :] = _rmsnorm_rows(x_ref[rows, :], g_ref[...]).astype(BF16)
        _for_row_chunks(a_ref.shape[0], chunk)

    o_ref[...] = _dot(a_ref[...], w_ref[...]).astype(o_ref.dtype)


def _norm_matmul(x, g, w, out_dtype, tm=512, tn=512):
    m, k = x.shape
    n = w.shape[1]
    tm, tn = min(tm, m), min(tn, n)
    return pl.pallas_call(
        _norm_matmul_kernel,
        grid=(m // tm, n // tn),
        in_specs=[
            pl.BlockSpec((tm, k), lambda i, j: (i, 0)),
            pl.BlockSpec((1, k), lambda i, j: (0, 0)),
            pl.BlockSpec((k, tn), lambda i, j: (0, j)),
        ],
        out_specs=pl.BlockSpec((tm, tn), lambda i, j: (i, j)),
        out_shape=jax.ShapeDtypeStruct((m, n), out_dtype),
        scratch_shapes=[pltpu.VMEM((tm, k), BF16)],
        compiler_params=_params(2),
        name="norm_matmul",
    )(x, g.reshape(1, k), w)


def _out_proj_gated_kernel(a_ref, gate_ref, w_ref, x_ref, o_ref, s_ref):
    @pl.when(pl.program_id(1) == 0)
    def _():
        def chunk(rows):
            s_ref[rows, :] = (a_ref[rows, :].astype(F32) * _silu(gate_ref[rows, :])).astype(BF16)
        _for_row_chunks(s_ref.shape[0], chunk)

    o_ref[...] = x_ref[...] + _dot(s_ref[...], w_ref[...])


def _out_proj_kernel(a_ref, w_ref, x_ref, o_ref):
    o_ref[...] = x_ref[...] + _dot(a_ref[...], w_ref[...])


def _out_proj(a, gate_arr, gate_col_block, w, x, tm=512, tn=512):
    m, k = a.shape
    n = w.shape[1]
    tm, tn = min(tm, m), min(tn, n)
    a_spec = pl.BlockSpec((tm, k), lambda i, j: (i, 0))
    w_spec = pl.BlockSpec((k, tn), lambda i, j: (0, j))
    x_spec = pl.BlockSpec((tm, tn), lambda i, j: (i, j))
    common = dict(
        grid=(m // tm, n // tn),
        out_specs=pl.BlockSpec((tm, tn), lambda i, j: (i, j)),
        out_shape=jax.ShapeDtypeStruct((m, n), F32),
        compiler_params=_params(2),
    )
    if gate_arr is None:
        return pl.pallas_call(_out_proj_kernel, in_specs=[a_spec, w_spec, x_spec], name="out_proj", **common)(a, w, x)
    gate_spec = pl.BlockSpec((tm, k), lambda i, j: (i, gate_col_block))
    return pl.pallas_call(
        _out_proj_gated_kernel,
        in_specs=[a_spec, gate_spec, w_spec, x_spec],
        scratch_shapes=[pltpu.VMEM((tm, k), BF16)],
        name="out_proj_gated",
        **common,
    )(a, gate_arr, w, x)


def _ple_kernel(x_ref, g_ref, wg_ref, p_ref, wp_ref, xt_ref, o_ref, a_ref):
    @pl.when(pl.program_id(1) == 0)
    def _():
        def chunk(rows):
            a_ref[rows, :] = _rmsnorm_rows(x_ref[rows, :], g_ref[...]).astype(BF16)
        _for_row_chunks(a_ref.shape[0], chunk)

    emb = _dot(p_ref[...].astype(BF16), wp_ref[...])
    gate = jax.nn.sigmoid(_dot(a_ref[...], wg_ref[...]))
    o_ref[...] = xt_ref[...] + emb * gate


def _ple(x, g, w_gate, p, w_proj, tm=512, tn=512):
    m, k = x.shape
    n = w_gate.shape[1]
    pd = p.shape[1]
    tm, tn = min(tm, m), min(tn, n)
    return pl.pallas_call(
        _ple_kernel,
        grid=(m // tm, n // tn),
        in_specs=[
            pl.BlockSpec((tm, k), lambda i, j: (i, 0)),
            pl.BlockSpec((1, k), lambda i, j: (0, 0)),
            pl.BlockSpec((k, tn), lambda i, j: (0, j)),
            pl.BlockSpec((tm, pd), lambda i, j: (i, 0)),
            pl.BlockSpec((pd, tn), lambda i, j: (0, j)),
            pl.BlockSpec((tm, tn), lambda i, j: (i, j)),
        ],
        out_specs=pl.BlockSpec((tm, tn), lambda i, j: (i, j)),
        out_shape=jax.ShapeDtypeStruct((m, n), F32),
        scratch_shapes=[pltpu.VMEM((tm, k), BF16)],
        compiler_params=_params(2),
        name="ple",
    )(x, g.reshape(1, k), w_gate, p, w_proj, x)


def _final_norm_kernel(x_ref, g_ref, o_ref):
    o_ref[...] = _rmsnorm_rows(x_ref[...], g_ref[...])


def _final_norm(x, g, tm=256):
    m, k = x.shape
    tm = min(tm, m)
    return pl.pallas_call(
        _final_norm_kernel,
        grid=(m // tm,),
        in_specs=[pl.BlockSpec((tm, k), lambda i: (i, 0)), pl.BlockSpec((1, k), lambda i: (0, 0))],
        out_specs=pl.BlockSpec((tm, k), lambda i: (i, 0)),
        out_shape=jax.ShapeDtypeStruct((m, k), F32),
        compiler_params=_params(1),
        name="final_norm",
    )(x, g.reshape(1, k))


def _t5_bucket(dist):
    max_exact = REL_BUCKETS // 2
    d = np.maximum(dist, 1).astype(np.float32)
    large = max_exact + (np.log(d / max_exact) / np.log(REL_MAX_DIST / max_exact) * (REL_BUCKETS - max_exact)).astype(np.int32)
    large = np.minimum(large, REL_BUCKETS - 1)
    return np.where(dist < max_exact, dist, large).astype(np.int32)


def _swa_bucket_table():
    qi = np.arange(SWA_BLOCK)[None, :]
    kj = np.arange(2 * SWA_BLOCK)[:, None]
    return _t5_bucket(np.clip(qi + SWA_BLOCK - kj, 0, None))


def _swa_kernel(sink_ref, rel_ref, bucket_ref, q_ref, kp_ref, kc_ref, vp_ref, vc_ref, o_ref, bias_ref, qs_ref):
    n = pl.program_id(1)
    blk = SWA_BLOCK
    lane = lax.broadcasted_iota(jnp.int32, (1, LANES), 1)

    @pl.when(jnp.logical_and(pl.program_id(0) == 0, n == 0))
    def _():
        bucket = bucket_ref[...]
        kj = lax.broadcasted_iota(jnp.int32, (2 * blk, blk), 0)
        qi = lax.broadcasted_iota(jnp.int32, (2 * blk, blk), 1)
        band = jnp.logical_or(jnp.logical_and(kj < blk, kj > qi), jnp.logical_and(kj >= blk, kj - blk <= qi))

        def per_head(head, carry):
            acc = jnp.zeros((2 * blk, blk), F32)
            for b in range(REL_BUCKETS):
                acc = jnp.where(bucket == b, rel_ref[b, head], acc)
            bias_ref[head] = jnp.where(band, acc, NEG)
            return carry
        lax.fori_loop(0, SWA_HEADS, per_head, 0)

    for kvh in range(SWA_KV_HEADS):
        cols = slice(kvh * LANES, (kvh + 1) * LANES)
        heads = range(kvh * SWA_GROUP, (kvh + 1) * SWA_GROUP)
        for g, head in enumerate(heads):
            q_pair = q_ref[:, head // 2 * LANES:(head // 2 + 1) * LANES]
            qs_ref[g * blk:(g + 1) * blk, :] = jnp.where((lane // HEAD_DIM) == head % 2, q_pair, jnp.zeros_like(q_pair))
        keys = jnp.concatenate([kp_ref[:, cols], kc_ref[:, cols]], axis=0)
        vals = jnp.concatenate([vp_ref[:, cols], vc_ref[:, cols]], axis=0)
        s = _dot_nt(keys, qs_ref[...]) + jnp.concatenate([bias_ref[head] for head in heads], axis=1)
        s_prev = jnp.where(n > 0, s[:blk], NEG)
        s_cur = s[blk:]
        sink = jnp.concatenate([jnp.full((1, blk), sink_ref[head], F32) for head in heads], axis=1)
        m = jnp.maximum(jnp.maximum(s_prev, s_cur).max(axis=0, keepdims=True), sink)
        e_prev, e_cur = jnp.exp(s_prev - m), jnp.exp(s_cur - m)
        den = e_prev.sum(axis=0, keepdims=True) + e_cur.sum(axis=0, keepdims=True) + jnp.exp(sink - m)
        e = jnp.concatenate([e_prev, e_cur], axis=0).astype(BF16)
        out = _dot(vals.astype(F32).T.astype(BF16), e) / den
        for pair in range(SWA_GROUP // 2):
            head = kvh * SWA_GROUP + 2 * pair
            even = out[:HEAD_DIM, 2 * pair * blk:(2 * pair + 1) * blk]
            odd = out[HEAD_DIM:, (2 * pair + 1) * blk:(2 * pair + 2) * blk]
            o_ref[:, head // 2 * LANES:(head // 2 + 1) * LANES] = jnp.concatenate([even, odd], axis=0).T.astype(o_ref.dtype)


def _swa_attention(qkv, sinks, rel_bias, bsz, seq):
    nb = seq // SWA_BLOCK
    width = SWA_HEADS * HEAD_DIM
    kvw = SWA_KV_HEADS * LANES
    k_blk = width // kvw
    cur = lambda b, n: b * nb + n
    prev = lambda b, n: b * nb + jnp.maximum(n - 1, 0)
    smem = pl.BlockSpec(memory_space=pltpu.SMEM)
    return pl.pallas_call(
        _swa_kernel,
        grid=(bsz, nb),
        in_specs=[
            smem, smem,
            pl.BlockSpec((2 * SWA_BLOCK, SWA_BLOCK), lambda b, n: (0, 0)),
            pl.BlockSpec((SWA_BLOCK, width), lambda b, n: (cur(b, n), 0)),
            pl.BlockSpec((SWA_BLOCK, kvw), lambda b, n: (prev(b, n), k_blk)),
            pl.BlockSpec((SWA_BLOCK, kvw), lambda b, n: (cur(b, n), k_blk)),
            pl.BlockSpec((SWA_BLOCK, kvw), lambda b, n: (prev(b, n), k_blk + 1)),
            pl.BlockSpec((SWA_BLOCK, kvw), lambda b, n: (cur(b, n), k_blk + 1)),
        ],
        out_specs=pl.BlockSpec((SWA_BLOCK, width), lambda b, n: (cur(b, n), 0)),
        out_shape=jax.ShapeDtypeStruct((bsz * seq, width), BF16),
        scratch_shapes=[
            pltpu.VMEM((SWA_HEADS, 2 * SWA_BLOCK, SWA_BLOCK), F32),
            pltpu.VMEM((SWA_GROUP * SWA_BLOCK, LANES), BF16),
        ],
        compiler_params=_params(2),
        name="swa_attention",
    )(sinks.astype(F32), rel_bias.astype(F32), jnp.asarray(_swa_bucket_table()), qkv, qkv, qkv, qkv, qkv)


def _dup_heads(w, n_heads):
    k = w.shape[0]
    w = w.reshape(k, n_heads, 1, HEAD_DIM)
    return jnp.broadcast_to(w, (k, n_heads, 2, HEAD_DIM)).reshape(k, n_heads * 2 * HEAD_DIM)


def _swa_layer(x, g, w_in, w_out, sinks, rel_bias, bsz, seq):
    width = SWA_HEADS * HEAD_DIM
    kvw = SWA_KV_HEADS * HEAD_DIM
    w_q, w_k, w_v, w_g = (w_in[:, :width], w_in[:, width:width + kvw],
                          w_in[:, width + kvw:width + 2 * kvw], w_in[:, width + 2 * kvw:])
    w_qkv = jnp.concatenate([w_q * (HEAD_DIM ** -0.5), _dup_heads(w_k, SWA_KV_HEADS), _dup_heads(w_v, SWA_KV_HEADS)],
                            axis=1).astype(BF16)
    qkv = _norm_matmul(x, g, w_qkv, BF16)
    gate = _norm_matmul(x, g, w_g.astype(BF16), F32)
    attn = _swa_attention(qkv, sinks, rel_bias, bsz, seq)
    return _out_proj(attn, gate, 0, w_out.astype(BF16), x)


HALO = 8


def _conv_kernel(x_ref, g_ref, wb_ref, wc_ref, wu_ref, wg_ref, ck_ref, o_ref, a_ref, z_ref, carry_ref, *, tiles_per_seq):
    i, j = pl.program_id(0), pl.program_id(1)
    tm = a_ref.shape[0]

    @pl.when(j == 0)
    def _():
        def chunk(rows):
            a_ref[rows, :] = _rmsnorm_rows(x_ref[rows, :], g_ref[...]).astype(BF16)
        _for_row_chunks(tm, chunk)

    a = a_ref[...]
    z = _dot(a, wc_ref[...]) * _dot(a, wu_ref[...])
    first = (i % tiles_per_seq) == 0

    @pl.when(first)
    def _():
        z_ref[:HALO, :] = jnp.zeros((HALO, z_ref.shape[1]), F32)

    @pl.when(jnp.logical_not(first))
    def _():
        z_ref[:HALO, :] = carry_ref[j]

    z_ref[HALO:, :] = z
    carry_ref[j] = z[tm - HALO:, :]
    conv = z_ref[HALO - 2:HALO - 2 + tm, :] * ck_ref[0:1, :]
    conv = conv + z_ref[HALO - 1:HALO - 1 + tm, :] * ck_ref[1:2, :]
    conv = conv + z * ck_ref[2:3, :]
    y = _dot(a, wb_ref[...]) * conv
    o_ref[...] = (y * _silu(_dot(a, wg_ref[...]))).astype(o_ref.dtype)


def _conv_mixer(x, g, w_in, conv_kernel, seq, tm=512, tn=512):
    m, k = x.shape
    width = w_in.shape[1] // 4
    tm, tn = min(tm, seq), min(tn, width)
    nt = width // tn
    w_spec = lambda q: pl.BlockSpec((k, tn), lambda i, j: (0, q * nt + j))
    return pl.pallas_call(
        functools.partial(_conv_kernel, tiles_per_seq=seq // tm),
        grid=(m // tm, nt),
        in_specs=[
            pl.BlockSpec((tm, k), lambda i, j: (i, 0)),
            pl.BlockSpec((1, k), lambda i, j: (0, 0)),
            w_spec(0), w_spec(1), w_spec(2), w_spec(3),
            pl.BlockSpec((CONV_TAPS, tn), lambda i, j: (0, j)),
        ],
        out_specs=pl.BlockSpec((tm, tn), lambda i, j: (i, j)),
        out_shape=jax.ShapeDtypeStruct((m, width), BF16),
        scratch_shapes=[
            pltpu.VMEM((tm, k), BF16),
            pltpu.VMEM((HALO + tm, tn), F32),
            pltpu.VMEM((nt, HALO, tn), F32),
        ],
        compiler_params=_params(2),
        name="conv_mixer",
    )(x, g.reshape(1, k), w_in, w_in, w_in, w_in, conv_kernel.astype(F32))


def _conv_layer(x, g, w_in, conv_kernel, w_out, seq):
    a = _conv_mixer(x, g, w_in.astype(BF16), conv_kernel, seq)
    return _out_proj(a, None, 0, w_out.astype(BF16), x)


def _ssm_operators(lam_re, lam_im, log_dt, b_re, b_im, c_re, c_im, d_skip):
    n_groups = lam_re.shape[0]
    n_oct = n_groups // GROUPS_PER_TILE
    L, C, N = SSM_CHUNK, SSM_GROUP, SSM_STATE
    dt = jnp.exp(log_dt.astype(F32))[:, None]
    lr, li = lam_re.astype(F32), lam_im.astype(F32)
    mag = jnp.exp(lr * dt)
    ab_re, ab_im = mag * jnp.cos(li * dt), mag * jnp.sin(li * dt)
    den = lr * lr + li * li
    nr = ab_re - 1.0
    coef_re = ((nr * lr + ab_im * li) / den)[..., None]
    coef_im = ((ab_im * lr - nr * li) / den)[..., None]
    br, bi = b_re.astype(F32), b_im.astype(F32)
    bb_re = coef_re * br - coef_im * bi
    bb_im = coef_re * bi + coef_im * br
    cr, ci = c_re.astype(F32), c_im.astype(F32)
    p_re, p_im = [jnp.ones_like(ab_re)], [jnp.zeros_like(ab_im)]
    for _ in range(L):
        p_re, p_im = (p_re + [p_re[-1] * ab_re - p_im[-1] * ab_im], p_im + [p_re[-1] * ab_im + p_im[-1] * ab_re])
    p_re, p_im = jnp.stack(p_re), jnp.stack(p_im)
    w_re = cr[None] * p_re[:L, :, None, :] - ci[None] * p_im[:L, :, None, :]
    w_im = cr[None] * p_im[:L, :, None, :] + ci[None] * p_re[:L, :, None, :]
    bt_re, bt_im = bb_re.transpose(0, 2, 1), bb_im.transpose(0, 2, 1)
    lag_k = jnp.sum(w_re[:, :, None] * bt_re[None, :, :, None] - w_im[:, :, None] * bt_im[None, :, :, None], axis=-1)
    eye = jnp.eye(GROUPS_PER_TILE, dtype=F32)
    blk = lag_k.reshape(L, n_oct, GROUPS_PER_TILE, C, 1, C) * eye[None, None, :, None, :, None]
    blk = jnp.concatenate([blk.reshape(L, n_oct, LANES, LANES), jnp.zeros((1, n_oct, LANES, LANES), F32)])
    lag_idx = np.array([[[2 * dd + b - a for b in range(2)] for a in range(2)] for dd in range(L // 2)])
    toe = blk[lag_idx]
    toe = toe.transpose(3, 0, 1, 4, 2, 5).reshape(n_oct, L // 2, 2 * LANES, 2 * LANES)
    q_re, q_im = p_re[:L][::-1], p_im[:L][::-1]
    bc_re = q_re[:, :, None, :] * bt_re[None] - q_im[:, :, None, :] * bt_im[None]
    bc_im = q_re[:, :, None, :] * bt_im[None] + q_im[:, :, None, :] * bt_re[None]

    def to_rows(t):
        t = t.reshape(L, n_oct, GROUPS_PER_TILE, C, 1, N).transpose(1, 0, 2, 3, 4, 5)
        return (t * eye[None, None, :, None, :, None]).reshape(n_oct, L * LANES, GROUPS_PER_TILE * N)

    o_re = cr[None] * p_re[1:, :, None, :] - ci[None] * p_im[1:, :, None, :]
    o_im = cr[None] * p_im[1:, :, None, :] + ci[None] * p_re[1:, :, None, :]

    def to_cols(t):
        t = t.reshape(L, n_oct, GROUPS_PER_TILE, 1, C, N).transpose(1, 2, 5, 0, 3, 4)
        return (t * eye[None, :, None, None, :, None]).reshape(n_oct, GROUPS_PER_TILE * N, L * LANES)

    oct_vec = lambda t: t.reshape(n_oct, 1, GROUPS_PER_TILE * N)
    d_oct = jnp.broadcast_to(d_skip.astype(F32).reshape(n_oct, 1, LANES), (n_oct, L, LANES))
    return dict(
        toe=toe.astype(BF16),
        bc_re=to_rows(bc_re).astype(BF16), bc_im=to_rows(bc_im).astype(BF16),
        oc_re=to_cols(o_re).astype(BF16), oc_im_neg=to_cols(-o_im).astype(BF16),
        al_re=oct_vec(p_re[L]), al_im=oct_vec(p_im[L]),
        d=d_oct.reshape(n_oct, 1, L * LANES),
    )


def _ssm_kernel(u_ref, toe_ref, bcr_ref, bci_ref, ocr_ref, oci_ref, alr_ref, ali_ref, d_ref, o_ref,
                ub_ref, zr_ref, zi_ref, hr_ref, hi_ref, *, bsz):
    rows = u_ref.shape[0]
    chunks = rows // bsz
    tile = 2 * LANES
    n_tiles = u_ref.shape[1] // tile
    ub_ref[...] = u_ref[...].astype(BF16)
    zr_ref[...] = _dot(ub_ref[...], bcr_ref[...])
    zi_ref[...] = _dot(ub_ref[...], bci_ref[...])
    a_re, a_im = alr_ref[...], ali_ref[...]

    def step(k, carry):
        new = []
        for b in range(bsz):
            h_re, h_im = carry[2 * b], carry[2 * b + 1]
            row = pl.ds(b * chunks + k, 1)
            hr_ref[row, :] = h_re
            hi_ref[row, :] = h_im
            new.append(a_re * h_re - a_im * h_im + zr_ref[row, :])
            new.append(a_re * h_im + a_im * h_re + zi_ref[row, :])
        return tuple(new)

    zero = jnp.zeros((1, a_re.shape[1]), F32)
    lax.fori_loop(0, chunks, step, (zero,) * (2 * bsz))

    hb_re, hb_im = hr_ref[...].astype(BF16), hi_ref[...].astype(BF16)
    for t2 in range(n_tiles):
        cols = slice(t2 * tile, (t2 + 1) * tile)
        y = _dot(hb_re, ocr_ref[:, cols]) + _dot(hb_im, oci_ref[:, cols])
        for r2 in range(t2 + 1):
            y = y + _dot(ub_ref[:, r2 * tile:(r2 + 1) * tile], toe_ref[t2 - r2])
        y = y + d_ref[:, cols] * u_ref[:, cols]
        o_ref[:, cols] = jax.nn.gelu(y).astype(o_ref.dtype)


def _ssm_core(u, ops, bsz, seq):
    m, width = u.shape
    L = SSM_CHUNK
    n_oct = width // LANES
    chunks = seq // L
    rows = bsz * chunks
    ow = L * LANES
    uc = u.reshape(bsz, chunks, L, n_oct, LANES).transpose(3, 0, 1, 2, 4).reshape(n_oct, rows, ow)
    n_state = GROUPS_PER_TILE * SSM_STATE
    blk = lambda *shape: pl.BlockSpec((None,) + shape, lambda p: (p,) + (0,) * len(shape))
    yc = pl.pallas_call(
        functools.partial(_ssm_kernel, bsz=bsz),
        grid=(n_oct,),
        in_specs=[
            blk(rows, ow), blk(L // 2, 2 * LANES, 2 * LANES), blk(ow, n_state), blk(ow, n_state),
            blk(n_state, ow), blk(n_state, ow), blk(1, n_state), blk(1, n_state), blk(1, ow),
        ],
        out_specs=blk(rows, ow),
        out_shape=jax.ShapeDtypeStruct((n_oct, rows, ow), BF16),
        scratch_shapes=[pltpu.VMEM((rows, ow), BF16)] + [pltpu.VMEM((rows, n_state), F32)] * 4,
        compiler_params=_params(1),
        name="ssm_core",
    )(uc, ops['toe'], ops['bc_re'], ops['bc_im'], ops['oc_re'], ops['oc_im_neg'], ops['al_re'], ops['al_im'], ops['d'])
    return yc.reshape(n_oct, bsz, chunks, L, LANES).transpose(1, 2, 3, 0, 4).reshape(m, width)


def _glu_kernel(y_ref, wa_ref, wb_ref, ba_ref, bb_ref, gate_ref, o_ref):
    y = y_ref[...]
    ga = _dot(y, wa_ref[...]) + ba_ref[...]
    gb = _dot(y, wb_ref[...]) + bb_ref[...]
    o_ref[...] = ((ga * jax.nn.sigmoid(gb)) * _silu(gate_ref[...])).astype(o_ref.dtype)


def _glu(y, w_glu, b_glu, proj, tm=512, tn=512):
    m, k = y.shape
    width = w_glu.shape[1] // 2
    tm, tn = min(tm, m), min(tn, width)
    nt = width // tn
    return pl.pallas_call(
        _glu_kernel,
        grid=(m // tm, nt),
        in_specs=[
            pl.BlockSpec((tm, k), lambda i, j: (i, 0)),
            pl.BlockSpec((k, tn), lambda i, j: (0, j)),
            pl.BlockSpec((k, tn), lambda i, j: (0, nt + j)),
            pl.BlockSpec((1, tn), lambda i, j: (0, j)),
            pl.BlockSpec((1, tn), lambda i, j: (0, nt + j)),
            pl.BlockSpec((tm, tn), lambda i, j: (i, nt + j)),
        ],
        out_specs=pl.BlockSpec((tm, tn), lambda i, j: (i, j)),
        out_shape=jax.ShapeDtypeStruct((m, width), BF16),
        compiler_params=_params(2),
        name="ssm_glu",
    )(y, w_glu, w_glu, b_glu, b_glu, proj)


def _ssm_layer(x, g, w_in, lam_re, lam_im, log_dt, b_re, b_im, c_re, c_im, d_skip, w_glu, b_glu, w_out, bsz, seq):
    width = w_in.shape[1] // 2
    proj = _norm_matmul(x, g, w_in.astype(BF16), F32)
    ops = _ssm_operators(lam_re, lam_im, log_dt, b_re, b_im, c_re, c_im, d_skip)
    y = _ssm_core(proj[:, :width], ops, bsz, seq)
    a = _glu(y, w_glu.astype(BF16), b_glu.astype(F32).reshape(1, -1), proj)
    return _out_proj(a, None, 0, w_out.astype(BF16), x)


CUM_BLOCK = 128


def _split3(x):
    x1 = x.astype(BF16)
    r1 = x - x1.astype(F32)
    x2 = r1.astype(BF16)
    x3 = (r1 - x2.astype(F32)).astype(BF16)
    return x1, x2, x3


def _forget_cumsum_kernel(z_ref, b_ref, o_ref):
    n_blocks = z_ref.shape[0] // CUM_BLOCK
    ri = lax.broadcasted_iota(jnp.int32, (CUM_BLOCK, CUM_BLOCK), 0)
    ci = lax.broadcasted_iota(jnp.int32, (CUM_BLOCK, CUM_BLOCK), 1)
    tri = jnp.where(ci <= ri, 1.0, 0.0).astype(BF16)

    def body(i, carry):
        rows = pl.ds(pl.multiple_of(i * CUM_BLOCK, CUM_BLOCK), CUM_BLOCK)
        z = z_ref[rows, :] + b_ref[...]
        log_f = jnp.minimum(z, 0.0) - jnp.log1p(jnp.exp(-jnp.abs(z)))
        x1, x2, x3 = _split3(log_f)
        c = _dot(tri, x1) + _dot(tri, x2) + _dot(tri, x3) + carry
        o_ref[rows, :] = c
        return c[CUM_BLOCK - 1:, :]

    lax.fori_loop(0, n_blocks, body, jnp.zeros((1, z_ref.shape[1]), F32))


def _forget_cumsum(z, b, bsz, seq):
    lanes = z.shape[1]
    return pl.pallas_call(
        _forget_cumsum_kernel,
        grid=(bsz,),
        in_specs=[pl.BlockSpec((seq, lanes), lambda i: (i, 0)), pl.BlockSpec((1, lanes), lambda i: (0, 0))],
        out_specs=pl.BlockSpec((seq, lanes), lambda i: (i, 0)),
        out_shape=jax.ShapeDtypeStruct(z.shape, F32),
        compiler_params=_params(1),
        name="forget_cumsum",
    )(z, b)


def _fox_kernel(q_ref, k_ref, v_ref, cq_ref, ck_ref, o_ref, *, tk):
    hp, qi = pl.program_id(1), pl.program_id(2)
    tq = q_ref.shape[0]
    q = q_ref[...]
    cq_tile = cq_ref[...]
    lane = lax.broadcasted_iota(jnp.int32, (1, LANES), 1)
    qpos = lax.broadcasted_iota(jnp.int32, (tq, tk), 0)
    kpos = lax.broadcasted_iota(jnp.int32, (tq, tk), 1)
    causal = kpos <= qpos
    halves = []
    for hh in range(2):
        qm = jnp.where((lane // HEAD_DIM) == hh, q, jnp.zeros_like(q))
        cq = jnp.sum(jnp.where(lane == 2 * hp + hh, cq_tile, 0.0), axis=-1, keepdims=True)

        def scores(kj):
            rows = pl.ds(pl.multiple_of(kj * tk, tk), tk)
            s = _dot_nt(qm, k_ref[rows, :]) * (HEAD_DIM ** -0.5)
            return s + cq - ck_ref[hh, pl.ds(kj, 1), :], rows

        def update(s, rows, carry):
            m, l, acc = carry
            m_new = jnp.maximum(m, s.max(axis=-1, keepdims=True))
            alpha = jnp.exp(m - m_new)
            p = jnp.exp(s - m_new)
            l = alpha * l + p.sum(axis=-1, keepdims=True)
            acc = alpha * acc + _dot(p.astype(BF16), v_ref[rows, :])
            return m_new, l, acc

        def body(kj, carry):
            s, rows = scores(kj)
            return update(s, rows, carry)

        init = (jnp.full((tq, 1), NEG, F32), jnp.zeros((tq, 1), F32), jnp.zeros((tq, LANES), F32))
        carry = lax.fori_loop(0, qi, body, init)
        s, rows = scores(qi)
        m, l, acc = update(jnp.where(causal, s, NEG), rows, carry)
        halves.append(acc / l)
    o_ref[...] = jnp.where(lane < HEAD_DIM, halves[0], halves[1]).astype(o_ref.dtype)


def _fox_attention(qkv, csum, ck, bsz, seq, t=512):
    width = FOX_HEADS * HEAD_DIM
    n_pairs = width // LANES
    t = min(t, seq)
    nq = seq // t
    return pl.pallas_call(
        functools.partial(_fox_kernel, tk=t),
        grid=(bsz, n_pairs, nq),
        in_specs=[
            pl.BlockSpec((t, LANES), lambda b, h, i: (b * nq + i, h)),
            pl.BlockSpec((seq, LANES), lambda b, h, i: (b, n_pairs + h)),
            pl.BlockSpec((seq, LANES), lambda b, h, i: (b, 2 * n_pairs + h)),
            pl.BlockSpec((t, LANES), lambda b, h, i: (b * nq + i, 0)),
            pl.BlockSpec((None, None, 2, nq, t), lambda b, h, i: (b, h, 0, 0, 0)),
        ],
        out_specs=pl.BlockSpec((t, LANES), lambda b, h, i: (b * nq + i, h)),
        out_shape=jax.ShapeDtypeStruct((bsz * seq, width), BF16),
        compiler_params=_params(3),
        name="fox_attention",
    )(qkv, qkv, qkv, csum, ck)


def _fox_layer(x, g, w_in, w_fg, b_fg, w_out, bsz, seq, t=512):
    width = FOX_HEADS * HEAD_DIM
    t = min(t, seq)
    qkv = _norm_matmul(x, g, w_in[:, :3 * width].astype(BF16), BF16)
    gate = _norm_matmul(x, g, w_in[:, 3 * width:].astype(BF16), F32)
    pad = LANES - FOX_HEADS
    z = _norm_matmul(x, g, jnp.pad(w_fg, ((0, 0), (0, pad))).astype(BF16), F32)
    csum = _forget_cumsum(z, jnp.pad(b_fg.astype(F32), (0, pad)).reshape(1, LANES), bsz, seq)
    ck = csum.reshape(bsz, seq, LANES)[:, :, :FOX_HEADS].transpose(0, 2, 1).reshape(bsz, FOX_HEADS // 2, 2, seq // t, t)
    attn = _fox_attention(qkv, csum, ck, bsz, seq, t)
    return _out_proj(attn, gate, 0, w_out.astype(BF16), x)


def kernel(x, p, norm_g, final_g, rel_bias, swa_w_in, swa_w_out, swa_sinks, conv_w_in, conv_kernel, conv_w_out, ssm_w_in, ssm_lam_re, ssm_lam_im, ssm_log_dt, ssm_b_re, ssm_b_im, ssm_c_re, ssm_c_im, ssm_d, ssm_w_glu, ssm_b_glu, ssm_w_out, fox_w_in, fox_w_fg, fox_b_fg, fox_w_out, ple_proj, ple_norm, ple_gate):
    bsz, seq, d_model = x.shape
    depth = p.shape[0]
    h = x.astype(F32).reshape(bsz * seq, d_model)
    for i in range(depth):
        mixer, j = i % N_MIXERS, i // N_MIXERS
        if mixer == 0:
            h = _swa_layer(h, norm_g[i], swa_w_in[j], swa_w_out[j], swa_sinks[j], rel_bias, bsz, seq)
        elif mixer == 1:
            h = _conv_layer(h, norm_g[i], conv_w_in[j], conv_kernel[j], conv_w_out[j], seq)
        elif mixer == 2:
            h = _ssm_layer(h, norm_g[i], ssm_w_in[j], ssm_lam_re[j], ssm_lam_im[j], ssm_log_dt[j], ssm_b_re[j],
                           ssm_b_im[j], ssm_c_re[j], ssm_c_im[j], ssm_d[j], ssm_w_glu[j], ssm_b_glu[j], ssm_w_out[j],
                           bsz, seq)
        else:
            h = _fox_layer(h, norm_g[i], fox_w_in[j], fox_w_fg[j], fox_b_fg[j], fox_w_out[j], bsz, seq)
        h = _ple(h, ple_norm[i], ple_gate[i].astype(BF16), p[i].reshape(bsz * seq, -1), ple_proj[i].astype(BF16))
    return _final_norm(h, final_g).reshape(bsz, seq, d_model).astype(x.dtype)
```

```python
import functools
import math

import numpy as np
import jax
import jax.numpy as jnp
from jax import lax
from jax.experimental import pallas as pl
from jax.experimental.pallas import tpu as pltpu

F32 = jnp.float32
BF16 = jnp.bfloat16

EPS = 1e-6
N_MIXERS = 4
PLE_DIM = 256

SWA_HEADS = 32
SWA_KV_HEADS = 4
SWA_GROUP = SWA_HEADS // SWA_KV_HEADS
HEAD_DIM = 64
SWA_BLOCK = 128
WINDOW = 128
REL_BUCKETS = 32
REL_MAX_DIST = 128

CONV_TAPS = 3

SSM_GROUP = 16
SSM_STATE = 64
SSM_CHUNK = 16

FOX_HEADS = 32

LANES = 128
GROUPS_PER_TILE = LANES // SSM_GROUP
VMEM_LIMIT = 48 * 1024 * 1024

NEG = float(jnp.finfo(jnp.float32).min)


def _params(n_axes):
    return pltpu.CompilerParams(dimension_semantics=("arbitrary",) * n_axes, vmem_limit_bytes=VMEM_LIMIT)


def _dot(a, b):
    return jnp.dot(a, b, preferred_element_type=F32)


def _dot_nt(a, b):
    return lax.dot_general(a, b, (((1,), (1,)), ((), ())), preferred_element_type=F32)


def _rmsnorm_rows(x, g):
    return x * lax.rsqrt(jnp.mean(x * x, axis=-1, keepdims=True) + EPS) * g


def _silu(x):
    return x * jax.nn.sigmoid(x)


def _tile_cols(w, tn):
    k, n = w.shape
    return w.astype(BF16).reshape(k, n // tn, tn).transpose(1, 0, 2)


ROW_CHUNK = 64


def _for_row_chunks(n_rows, fn):
    def body(c, carry):
        fn(pl.ds(pl.multiple_of(c * ROW_CHUNK, ROW_CHUNK), ROW_CHUNK))
        return carry
    lax.fori_loop(0, n_rows // ROW_CHUNK, body, 0)


def _norm_matmul_kernel(x_ref, g_ref, w_ref, o_ref, a_ref):
    @pl.when(pl.program_id(1) == 0)
    def _():
        def chunk(rows):
            a_ref[rows, :] = _rmsnorm_rows(x_ref[rows, :], g_ref[...]).astype(BF16)
        _for_row_chunks(a_ref.shape[0], chunk)

    o_ref[...] = _dot(a_ref[...], w_ref[...]).astype(o_ref.dtype)


def _norm_matmul(x, g, w, out_dtype, tm=1024, tn=512):
    m, k = x.shape
    n = w.shape[1]
    tm, tn = min(tm, m), min(tn, n)
    return pl.pallas_call(
        _norm_matmul_kernel,
        grid=(m // tm, n // tn),
        in_specs=[
            pl.BlockSpec((tm, k), lambda i, j: (i, 0)),
            pl.BlockSpec((1, k), lambda i, j: (0, 0)),
            pl.BlockSpec((None, k, tn), lambda i, j: (j, 0, 0)),
        ],
        out_specs=pl.BlockSpec((tm, tn), lambda i, j: (i, j)),
        out_shape=jax.ShapeDtypeStruct((m, n), out_dtype),
        scratch_shapes=[pltpu.VMEM((tm, k), BF16)],
        compiler_params=_params(2),
        name="norm_matmul",
    )(x, g.reshape(1, k), _tile_cols(w, tn))


def _out_proj_gated_kernel(a_ref, gate_ref, w_ref, x_ref, o_ref, s_ref):
    @pl.when(pl.program_id(1) == 0)
    def _():
        def chunk(rows):
            s_ref[rows, :] = (a_ref[rows, :].astype(F32) * _silu(gate_ref[rows, :])).astype(BF16)
        _for_row_chunks(s_ref.shape[0], chunk)

    o_ref[...] = x_ref[...] + _dot(s_ref[...], w_ref[...])


def _out_proj_kernel(a_ref, w_ref, x_ref, o_ref):
    o_ref[...] = x_ref[...] + _dot(a_ref[...], w_ref[...])


def _out_proj(a, gate_arr, gate_col_block, w, x, tm=512, tn=512):
    m, k = a.shape
    n = w.shape[1]
    tm, tn = min(tm, m), min(tn, n)
    a_spec = pl.BlockSpec((tm, k), lambda i, j: (i, 0))
    w_spec = pl.BlockSpec((None, k, tn), lambda i, j: (j, 0, 0))
    x_spec = pl.BlockSpec((tm, tn), lambda i, j: (i, j))
    w = _tile_cols(w, tn)
    common = dict(
        grid=(m // tm, n // tn),
        out_specs=pl.BlockSpec((tm, tn), lambda i, j: (i, j)),
        out_shape=jax.ShapeDtypeStruct((m, n), F32),
        compiler_params=_params(2),
    )
    if gate_arr is None:
        return pl.pallas_call(_out_proj_kernel, in_specs=[a_spec, w_spec, x_spec], name="out_proj", **common)(a, w, x)
    gate_spec = pl.BlockSpec((tm, k), lambda i, j: (i, gate_col_block))
    return pl.pallas_call(
        _out_proj_gated_kernel,
        in_specs=[a_spec, gate_spec, w_spec, x_spec],
        scratch_shapes=[pltpu.VMEM((tm, k), BF16)],
        name="out_proj_gated",
        **common,
    )(a, gate_arr, w, x)


def _ple_kernel(x_ref, g_ref, wg_ref, p_ref, wp_ref, xt_ref, o_ref, a_ref):
    @pl.when(pl.program_id(1) == 0)
    def _():
        def chunk(rows):
            a_ref[rows, :] = _rmsnorm_rows(x_ref[rows, :], g_ref[...]).astype(BF16)
        _for_row_chunks(a_ref.shape[0], chunk)

    emb = _dot(p_ref[...].astype(BF16), wp_ref[...])
    gate = jax.nn.sigmoid(_dot(a_ref[...], wg_ref[...]))
    o_ref[...] = xt_ref[...] + emb * gate


def _ple(x, g, w_gate, p, w_proj, tm=1024, tn=512):
    m, k = x.shape
    n = w_gate.shape[1]
    pd = p.shape[1]
    tm, tn = min(tm, m), min(tn, n)
    return pl.pallas_call(
        _ple_kernel,
        grid=(m // tm, n // tn),
        in_specs=[
            pl.BlockSpec((tm, k), lambda i, j: (i, 0)),
            pl.BlockSpec((1, k), lambda i, j: (0, 0)),
            pl.BlockSpec((None, k, tn), lambda i, j: (j, 0, 0)),
            pl.BlockSpec((tm, pd), lambda i, j: (i, 0)),
            pl.BlockSpec((None, pd, tn), lambda i, j: (j, 0, 0)),
            pl.BlockSpec((tm, tn), lambda i, j: (i, j)),
        ],
        out_specs=pl.BlockSpec((tm, tn), lambda i, j: (i, j)),
        out_shape=jax.ShapeDtypeStruct((m, n), F32),
        scratch_shapes=[pltpu.VMEM((tm, k), BF16)],
        compiler_params=_params(2),
        name="ple",
    )(x, g.reshape(1, k), _tile_cols(w_gate, tn), p, _tile_cols(w_proj, tn), x)


def _final_norm_kernel(x_ref, g_ref, o_ref):
    o_ref[...] = _rmsnorm_rows(x_ref[...], g_ref[...])


def _final_norm(x, g, tm=256):
    m, k = x.shape
    tm = min(tm, m)
    return pl.pallas_call(
        _final_norm_kernel,
        grid=(m // tm,),
        in_specs=[pl.BlockSpec((tm, k), lambda i: (i, 0)), pl.BlockSpec((1, k), lambda i: (0, 0))],
        out_specs=pl.BlockSpec((tm, k), lambda i: (i, 0)),
        out_shape=jax.ShapeDtypeStruct((m, k), F32),
        compiler_params=_params(1),
        name="final_norm",
    )(x, g.reshape(1, k))


def _t5_bucket(dist):
    max_exact = REL_BUCKETS // 2
    d = np.maximum(dist, 1).astype(np.float32)
    large = max_exact + (np.log(d / max_exact) / np.log(REL_MAX_DIST / max_exact) * (REL_BUCKETS - max_exact)).astype(np.int32)
    large = np.minimum(large, REL_BUCKETS - 1)
    return np.where(dist < max_exact, dist, large).astype(np.int32)


def _swa_bucket_table():
    qi = np.arange(SWA_BLOCK)[None, :]
    kj = np.arange(2 * SWA_BLOCK)[:, None]
    return _t5_bucket(np.clip(qi + SWA_BLOCK - kj, 0, None))


def _swa_kernel(sink_ref, rel_ref, bucket_ref, q_ref, kp_ref, kc_ref, vp_ref, vc_ref, o_ref, bias_ref, qs_ref):
    n = pl.program_id(1)
    blk = SWA_BLOCK
    lane = lax.broadcasted_iota(jnp.int32, (1, LANES), 1)

    @pl.when(jnp.logical_and(pl.program_id(0) == 0, n == 0))
    def _():
        bucket = bucket_ref[...]
        kj = lax.broadcasted_iota(jnp.int32, (2 * blk, blk), 0)
        qi = lax.broadcasted_iota(jnp.int32, (2 * blk, blk), 1)
        band = jnp.logical_or(jnp.logical_and(kj < blk, kj > qi), jnp.logical_and(kj >= blk, kj - blk <= qi))

        def per_head(head, carry):
            acc = jnp.zeros((2 * blk, blk), F32)
            for b in range(REL_BUCKETS):
                acc = jnp.where(bucket == b, rel_ref[b, head], acc)
            bias_ref[head] = jnp.where(band, acc, NEG)
            return carry
        lax.fori_loop(0, SWA_HEADS, per_head, 0)

    for kvh in range(SWA_KV_HEADS):
        cols = slice(kvh * LANES, (kvh + 1) * LANES)
        heads = range(kvh * SWA_GROUP, (kvh + 1) * SWA_GROUP)
        for g, head in enumerate(heads):
            q_pair = q_ref[:, head // 2 * LANES:(head // 2 + 1) * LANES]
            qs_ref[g * blk:(g + 1) * blk, :] = jnp.where((lane // HEAD_DIM) == head % 2, q_pair, jnp.zeros_like(q_pair))
        keys = jnp.concatenate([kp_ref[:, cols], kc_ref[:, cols]], axis=0)
        vals = jnp.concatenate([vp_ref[:, cols], vc_ref[:, cols]], axis=0)
        s = _dot_nt(keys, qs_ref[...]) + jnp.concatenate([bias_ref[head] for head in heads], axis=1)
        s_prev = jnp.where(n > 0, s[:blk], NEG)
        s_cur = s[blk:]
        sink = jnp.concatenate([jnp.full((1, blk), sink_ref[head], F32) for head in heads], axis=1)
        m = jnp.maximum(jnp.maximum(s_prev, s_cur).max(axis=0, keepdims=True), sink)
        e_prev, e_cur = jnp.exp(s_prev - m), jnp.exp(s_cur - m)
        den = e_prev.sum(axis=0, keepdims=True) + e_cur.sum(axis=0, keepdims=True) + jnp.exp(sink - m)
        e = jnp.concatenate([e_prev, e_cur], axis=0).astype(BF16)
        out = _dot(vals.astype(F32).T.astype(BF16), e) / den
        for pair in range(SWA_GROUP // 2):
            head = kvh * SWA_GROUP + 2 * pair
            even = out[:HEAD_DIM, 2 * pair * blk:(2 * pair + 1) * blk]
            odd = out[HEAD_DIM:, (2 * pair + 1) * blk:(2 * pair + 2) * blk]
            o_ref[:, head // 2 * LANES:(head // 2 + 1) * LANES] = jnp.concatenate([even, odd], axis=0).T.astype(o_ref.dtype)


def _swa_attention(qkv, sinks, rel_bias, bsz, seq):
    nb = seq // SWA_BLOCK
    width = SWA_HEADS * HEAD_DIM
    kvw = SWA_KV_HEADS * LANES
    k_blk = width // kvw
    cur = lambda b, n: b * nb + n
    prev = lambda b, n: b * nb + jnp.maximum(n - 1, 0)
    smem = pl.BlockSpec(memory_space=pltpu.SMEM)
    return pl.pallas_call(
        _swa_kernel,
        grid=(bsz, nb),
        in_specs=[
            smem, smem,
            pl.BlockSpec((2 * SWA_BLOCK, SWA_BLOCK), lambda b, n: (0, 0)),
            pl.BlockSpec((SWA_BLOCK, width), lambda b, n: (cur(b, n), 0)),
            pl.BlockSpec((SWA_BLOCK, kvw), lambda b, n: (prev(b, n), k_blk)),
            pl.BlockSpec((SWA_BLOCK, kvw), lambda b, n: (cur(b, n), k_blk)),
            pl.BlockSpec((SWA_BLOCK, kvw), lambda b, n: (prev(b, n), k_blk + 1)),
            pl.BlockSpec((SWA_BLOCK, kvw), lambda b, n: (cur(b, n), k_blk + 1)),
        ],
        out_specs=pl.BlockSpec((SWA_BLOCK, width), lambda b, n: (cur(b, n), 0)),
        out_shape=jax.ShapeDtypeStruct((bsz * seq, width), BF16),
        scratch_shapes=[
            pltpu.VMEM((SWA_HEADS, 2 * SWA_BLOCK, SWA_BLOCK), F32),
            pltpu.VMEM((SWA_GROUP * SWA_BLOCK, LANES), BF16),
        ],
        compiler_params=_params(2),
        name="swa_attention",
    )(sinks.astype(F32), rel_bias.astype(F32), jnp.asarray(_swa_bucket_table()), qkv, qkv, qkv, qkv, qkv)


def _dup_heads(w, n_heads):
    k = w.shape[0]
    w = w.reshape(k, n_heads, 1, HEAD_DIM)
    return jnp.broadcast_to(w, (k, n_heads, 2, HEAD_DIM)).reshape(k, n_heads * 2 * HEAD_DIM)


def _swa_layer(x, g, w_in, w_out, sinks, rel_bias, bsz, seq):
    width = SWA_HEADS * HEAD_DIM
    kvw = SWA_KV_HEADS * HEAD_DIM
    w_q, w_k, w_v, w_g = (w_in[:, :width], w_in[:, width:width + kvw],
                          w_in[:, width + kvw:width + 2 * kvw], w_in[:, width + 2 * kvw:])
    w_qkv = jnp.concatenate([w_q * (HEAD_DIM ** -0.5), _dup_heads(w_k, SWA_KV_HEADS), _dup_heads(w_v, SWA_KV_HEADS)],
                            axis=1).astype(BF16)
    qkv = _norm_matmul(x, g, w_qkv, BF16)
    gate = _norm_matmul(x, g, w_g.astype(BF16), F32)
    attn = _swa_attention(qkv, sinks, rel_bias, bsz, seq)
    return _out_proj(attn, gate, 0, w_out.astype(BF16), x)


HALO = 8


def _conv_kernel(x_ref, g_ref, wb_ref, wc_ref, wu_ref, wg_ref, ck_ref, o_ref, a_ref, z_ref, carry_ref, *, tiles_per_seq):
    i, j = pl.program_id(0), pl.program_id(1)
    tm = a_ref.shape[0]

    @pl.when(j == 0)
    def _():
        def chunk(rows):
            a_ref[rows, :] = _rmsnorm_rows(x_ref[rows, :], g_ref[...]).astype(BF16)
        _for_row_chunks(tm, chunk)

    a = a_ref[...]
    z = _dot(a, wc_ref[...]) * _dot(a, wu_ref[...])
    first = (i % tiles_per_seq) == 0

    @pl.when(first)
    def _():
        z_ref[:HALO, :] = jnp.zeros((HALO, z_ref.shape[1]), F32)

    @pl.when(jnp.logical_not(first))
    def _():
        z_ref[:HALO, :] = carry_ref[j]

    z_ref[HALO:, :] = z
    carry_ref[j] = z[tm - HALO:, :]
    conv = z_ref[HALO - 2:HALO - 2 + tm, :] * ck_ref[0:1, :]
    conv = conv + z_ref[HALO - 1:HALO - 1 + tm, :] * ck_ref[1:2, :]
    conv = conv + z * ck_ref[2:3, :]
    y = _dot(a, wb_ref[...]) * conv
    o_ref[...] = (y * _silu(_dot(a, wg_ref[...]))).astype(o_ref.dtype)


def _conv_mixer(x, g, w_in, conv_kernel, seq, tm=512, tn=512):
    m, k = x.shape
    width = w_in.shape[1] // 4
    tm, tn = min(tm, seq), min(tn, width)
    nt = width // tn
    w_spec = lambda q: pl.BlockSpec((None, k, tn), lambda i, j: (q * nt + j, 0, 0))
    w_in = _tile_cols(w_in, tn)
    return pl.pallas_call(
        functools.partial(_conv_kernel, tiles_per_seq=seq // tm),
        grid=(m // tm, nt),
        in_specs=[
            pl.BlockSpec((tm, k), lambda i, j: (i, 0)),
            pl.BlockSpec((1, k), lambda i, j: (0, 0)),
            w_spec(0), w_spec(1), w_spec(2), w_spec(3),
            pl.BlockSpec((CONV_TAPS, tn), lambda i, j: (0, j)),
        ],
        out_specs=pl.BlockSpec((tm, tn), lambda i, j: (i, j)),
        out_shape=jax.ShapeDtypeStruct((m, width), BF16),
        scratch_shapes=[
            pltpu.VMEM((tm, k), BF16),
            pltpu.VMEM((HALO + tm, tn), F32),
            pltpu.VMEM((nt, HALO, tn), F32),
        ],
        compiler_params=_params(2),
        name="conv_mixer",
    )(x, g.reshape(1, k), w_in, w_in, w_in, w_in, conv_kernel.astype(F32))


def _conv_layer(x, g, w_in, conv_kernel, w_out, seq):
    a = _conv_mixer(x, g, w_in.astype(BF16), conv_kernel, seq)
    return _out_proj(a, None, 0, w_out.astype(BF16), x)


def _ssm_operators(lam_re, lam_im, log_dt, b_re, b_im, c_re, c_im, d_skip):
    n_groups = lam_re.shape[0]
    n_oct = n_groups // GROUPS_PER_TILE
    L, C, N = SSM_CHUNK, SSM_GROUP, SSM_STATE
    dt = jnp.exp(log_dt.astype(F32))[:, None]
    lr, li = lam_re.astype(F32), lam_im.astype(F32)
    mag = jnp.exp(lr * dt)
    ab_re, ab_im = mag * jnp.cos(li * dt), mag * jnp.sin(li * dt)
    den = lr * lr + li * li
    nr = ab_re - 1.0
    coef_re = ((nr * lr + ab_im * li) / den)[..., None]
    coef_im = ((ab_im * lr - nr * li) / den)[..., None]
    br, bi = b_re.astype(F32), b_im.astype(F32)
    bb_re = coef_re * br - coef_im * bi
    bb_im = coef_re * bi + coef_im * br
    cr, ci = c_re.astype(F32), c_im.astype(F32)
    p_re, p_im = [jnp.ones_like(ab_re)], [jnp.zeros_like(ab_im)]
    for _ in range(L):
        p_re, p_im = (p_re + [p_re[-1] * ab_re - p_im[-1] * ab_im], p_im + [p_re[-1] * ab_im + p_im[-1] * ab_re])
    p_re, p_im = jnp.stack(p_re), jnp.stack(p_im)
    w_re = cr[None] * p_re[:L, :, None, :] - ci[None] * p_im[:L, :, None, :]
    w_im = cr[None] * p_im[:L, :, None, :] + ci[None] * p_re[:L, :, None, :]
    bt_re, bt_im = bb_re.transpose(0, 2, 1), bb_im.transpose(0, 2, 1)
    lag_k = jnp.sum(w_re[:, :, None] * bt_re[None, :, :, None] - w_im[:, :, None] * bt_im[None, :, :, None], axis=-1)
    q_re, q_im = p_re[:L][::-1], p_im[:L][::-1]
    bc_re = q_re[:, :, None, :] * bt_re[None] - q_im[:, :, None, :] * bt_im[None]
    bc_im = q_re[:, :, None, :] * bt_im[None] + q_im[:, :, None, :] * bt_re[None]
    o_re = cr[None] * p_re[1:, :, None, :] - ci[None] * p_im[1:, :, None, :]
    o_im = cr[None] * p_im[1:, :, None, :] + ci[None] * p_re[1:, :, None, :]
    octs = lambda t: t.reshape(L, n_oct, GROUPS_PER_TILE, C, -1)
    to_lag = lambda t: octs(t).transpose(1, 0, 3, 2, 4).reshape(n_oct, L, C, LANES)
    to_rows = lambda t: octs(t).transpose(1, 0, 3, 2, 4).reshape(n_oct, L * C, GROUPS_PER_TILE * N)
    to_cols = lambda t: octs(t).transpose(1, 2, 4, 0, 3).reshape(n_oct, GROUPS_PER_TILE * N, L * C)
    oct_vec = lambda t: t.reshape(n_oct, 1, -1)
    return dict(
        lag=to_lag(lag_k).astype(BF16),
        bc_re=to_rows(bc_re).astype(BF16), bc_im=to_rows(bc_im).astype(BF16),
        oc_re=to_cols(o_re).astype(BF16), oc_im_neg=to_cols(-o_im).astype(BF16),
        al_re=oct_vec(p_re[L]), al_im=oct_vec(p_im[L]), d=oct_vec(d_skip.astype(F32)),
    )


def _ssm_expanders():
    L, C, N, G8 = SSM_CHUNK, SSM_GROUP, SSM_STATE, GROUPS_PER_TILE
    wide = np.arange(L * LANES)
    w_step, w_group, w_chan = wide // LANES, (wide // C) % G8, wide % C
    small = np.arange(L * C)
    s_step, s_chan = small // C, small % C
    state_group = np.arange(G8 * N) // N
    spread = (w_step[:, None] == s_step[None, :]) & (w_chan[:, None] == s_chan[None, :])
    lane = np.arange(LANES)
    as_bf16 = lambda a: jnp.asarray(a.astype(np.float32), dtype=BF16)
    return dict(
        spread_rows=as_bf16(spread), spread_cols=as_bf16(spread.T),
        keep_rows=jnp.asarray((w_group[:, None] == state_group[None, :]).astype(np.float32)),
        keep_cols=jnp.asarray((state_group[:, None] == w_group[None, :]).astype(np.float32)),
        spread_lag=as_bf16((lane % C)[:, None] == np.arange(C)[None, :]),
        keep_lag=jnp.asarray(((lane // C)[:, None] == (lane // C)[None, :]).astype(np.float32)),
    )


def _ssm_kernel(u_ref, lag_ref, bxr_ref, bxi_ref, cxr_ref, cxi_ref, alr_ref, ali_ref, d_ref,
                sr_ref, sc_ref, kr_ref, kc_ref, sl_ref, kl_ref, o_ref,
                ub_ref, bcr_ref, bci_ref, ocr_ref, oci_ref, zr_ref, zi_ref, hr_ref, hi_ref, *, bsz):
    L = u_ref.shape[0]
    rows = u_ref.shape[1]
    chunks = rows // bsz
    tile = 2 * LANES
    for r in range(L):
        ub_ref[:, r * LANES:(r + 1) * LANES] = u_ref[r].astype(BF16)
    bcr_ref[...] = (_dot(sr_ref[...], bxr_ref[...]) * kr_ref[...]).astype(BF16)
    bci_ref[...] = (_dot(sr_ref[...], bxi_ref[...]) * kr_ref[...]).astype(BF16)
    zr_ref[...] = _dot(ub_ref[...], bcr_ref[...])
    zi_ref[...] = _dot(ub_ref[...], bci_ref[...])
    a_re, a_im = alr_ref[...], ali_ref[...]

    def step(k, carry):
        new = []
        for b in range(bsz):
            h_re, h_im = carry[2 * b], carry[2 * b + 1]
            row = pl.ds(b * chunks + k, 1)
            hr_ref[row, :] = h_re
            hi_ref[row, :] = h_im
            new.append(a_re * h_re - a_im * h_im + zr_ref[row, :])
            new.append(a_re * h_im + a_im * h_re + zi_ref[row, :])
        return tuple(new)

    zero = jnp.zeros((1, a_re.shape[1]), F32)
    lax.fori_loop(0, chunks, step, (zero,) * (2 * bsz))

    ocr_ref[...] = (_dot(cxr_ref[...], sc_ref[...]) * kc_ref[...]).astype(BF16)
    oci_ref[...] = (_dot(cxi_ref[...], sc_ref[...]) * kc_ref[...]).astype(BF16)
    lag = [(_dot(sl_ref[...], lag_ref[d]) * kl_ref[...]).astype(BF16) for d in range(L)]
    lag_tile = []
    for dd in range(L // 2):
        below = lag[2 * dd - 1] if dd > 0 else jnp.zeros((LANES, LANES), BF16)
        lag_tile.append(jnp.concatenate([jnp.concatenate([lag[2 * dd], lag[2 * dd + 1]], axis=1),
                                         jnp.concatenate([below, lag[2 * dd]], axis=1)], axis=0))
    hb_re, hb_im = hr_ref[...].astype(BF16), hi_ref[...].astype(BF16)
    for t2 in range(L // 2):
        cols = slice(t2 * tile, (t2 + 1) * tile)
        y = _dot(hb_re, ocr_ref[:, cols]) + _dot(hb_im, oci_ref[:, cols])
        for r2 in range(t2 + 1):
            y = y + _dot(ub_ref[:, r2 * tile:(r2 + 1) * tile], lag_tile[t2 - r2])
        for half in range(2):
            t = 2 * t2 + half
            o_ref[t] = jax.nn.gelu(y[:, half * LANES:(half + 1) * LANES] + d_ref[...] * u_ref[t]).astype(o_ref.dtype)


def _ssm_core(u, ops, bsz, seq):
    m, width = u.shape
    L, C = SSM_CHUNK, SSM_GROUP
    n_oct = width // LANES
    chunks = seq // L
    rows = bsz * chunks
    ow = L * LANES
    uc = u.reshape(bsz, chunks, L, n_oct, LANES).transpose(3, 2, 0, 1, 4).reshape(n_oct, L, rows, LANES)
    n_state = GROUPS_PER_TILE * SSM_STATE
    ex = _ssm_expanders()
    per_oct = lambda *shape: pl.BlockSpec((None,) + shape, lambda p: (p,) + (0,) * len(shape))
    const = lambda a: pl.BlockSpec(a.shape, lambda p: (0,) * a.ndim)
    consts = [ex['spread_rows'], ex['spread_cols'], ex['keep_rows'], ex['keep_cols'], ex['spread_lag'], ex['keep_lag']]
    yc = pl.pallas_call(
        functools.partial(_ssm_kernel, bsz=bsz),
        grid=(n_oct,),
        in_specs=[
            per_oct(L, rows, LANES), per_oct(L, C, LANES), per_oct(L * C, n_state), per_oct(L * C, n_state),
            per_oct(n_state, L * C), per_oct(n_state, L * C), per_oct(1, n_state), per_oct(1, n_state), per_oct(1, LANES),
        ] + [const(a) for a in consts],
        out_specs=per_oct(L, rows, LANES),
        out_shape=jax.ShapeDtypeStruct((n_oct, L, rows, LANES), BF16),
        scratch_shapes=[pltpu.VMEM((rows, ow), BF16)] + [pltpu.VMEM((ow, n_state), BF16)] * 2
        + [pltpu.VMEM((n_state, ow), BF16)] * 2 + [pltpu.VMEM((rows, n_state), F32)] * 4,
        compiler_params=_params(1),
        name="ssm_core",
    )(uc, ops['lag'], ops['bc_re'], ops['bc_im'], ops['oc_re'], ops['oc_im_neg'], ops['al_re'], ops['al_im'], ops['d'], *consts)
    return yc.reshape(n_oct, L, bsz, chunks, LANES).transpose(2, 3, 1, 0, 4).reshape(m, width)


def _glu_kernel(y_ref, wa_ref, wb_ref, ba_ref, bb_ref, gate_ref, o_ref):
    y = y_ref[...]
    ga = _dot(y, wa_ref[...]) + ba_ref[...]
    gb = _dot(y, wb_ref[...]) + bb_ref[...]
    o_ref[...] = ((ga * jax.nn.sigmoid(gb)) * _silu(gate_ref[...])).astype(o_ref.dtype)


def _glu(y, w_glu, b_glu, proj, tm=512, tn=512):
    m, k = y.shape
    width = w_glu.shape[1] // 2
    tm, tn = min(tm, m), min(tn, width)
    nt = width // tn
    w_tiles = _tile_cols(w_glu, tn)
    return pl.pallas_call(
        _glu_kernel,
        grid=(m // tm, nt),
        in_specs=[
            pl.BlockSpec((tm, k), lambda i, j: (i, 0)),
            pl.BlockSpec((None, k, tn), lambda i, j: (j, 0, 0)),
            pl.BlockSpec((None, k, tn), lambda i, j: (nt + j, 0, 0)),
            pl.BlockSpec((1, tn), lambda i, j: (0, j)),
            pl.BlockSpec((1, tn), lambda i, j: (0, nt + j)),
            pl.BlockSpec((tm, tn), lambda i, j: (i, nt + j)),
        ],
        out_specs=pl.BlockSpec((tm, tn), lambda i, j: (i, j)),
        out_shape=jax.ShapeDtypeStruct((m, width), BF16),
        compiler_params=_params(2),
        name="ssm_glu",
    )(y, w_tiles, w_tiles, b_glu, b_glu, proj)


def _ssm_layer(x, g, w_in, lam_re, lam_im, log_dt, b_re, b_im, c_re, c_im, d_skip, w_glu, b_glu, w_out, bsz, seq):
    width = w_in.shape[1] // 2
    proj = _norm_matmul(x, g, w_in.astype(BF16), F32)
    ops = _ssm_operators(lam_re, lam_im, log_dt, b_re, b_im, c_re, c_im, d_skip)
    y = _ssm_core(proj[:, :width], ops, bsz, seq)
    a = _glu(y, w_glu.astype(BF16), b_glu.astype(F32).reshape(1, -1), proj)
    return _out_proj(a, None, 0, w_out.astype(BF16), x)


CUM_BLOCK = 128


def _split3(x):
    x1 = x.astype(BF16)
    r1 = x - x1.astype(F32)
    x2 = r1.astype(BF16)
    x3 = (r1 - x2.astype(F32)).astype(BF16)
    return x1, x2, x3


def _forget_cumsum_kernel(z_ref, b_ref, o_ref):
    n_blocks = z_ref.shape[0] // CUM_BLOCK
    ri = lax.broadcasted_iota(jnp.int32, (CUM_BLOCK, CUM_BLOCK), 0)
    ci = lax.broadcasted_iota(jnp.int32, (CUM_BLOCK, CUM_BLOCK), 1)
    tri = jnp.where(ci <= ri, 1.0, 0.0).astype(BF16)

    def body(i, carry):
        rows = pl.ds(pl.multiple_of(i * CUM_BLOCK, CUM_BLOCK), CUM_BLOCK)
        z = z_ref[rows, :] + b_ref[...]
        log_f = jnp.minimum(z, 0.0) - jnp.log1p(jnp.exp(-jnp.abs(z)))
        x1, x2, x3 = _split3(log_f)
        c = _dot(tri, x1) + _dot(tri, x2) + _dot(tri, x3) + carry
        o_ref[rows, :] = c
        return c[CUM_BLOCK - 1:, :]

    lax.fori_loop(0, n_blocks, body, jnp.zeros((1, z_ref.shape[1]), F32))


def _forget_cumsum(z, b, bsz, seq):
    lanes = z.shape[1]
    return pl.pallas_call(
        _forget_cumsum_kernel,
        grid=(bsz,),
        in_specs=[pl.BlockSpec((seq, lanes), lambda i: (i, 0)), pl.BlockSpec((1, lanes), lambda i: (0, 0))],
        out_specs=pl.BlockSpec((seq, lanes), lambda i: (i, 0)),
        out_shape=jax.ShapeDtypeStruct(z.shape, F32),
        compiler_params=_params(1),
        name="forget_cumsum",
    )(z, b)


def _fox_kernel(q_ref, k_ref, v_ref, cq_ref, ck_ref, o_ref, *, tk):
    hp, qi = pl.program_id(1), pl.program_id(2)
    tq = q_ref.shape[0]
    q = q_ref[...]
    cq_tile = cq_ref[...]
    lane = lax.broadcasted_iota(jnp.int32, (1, LANES), 1)
    qpos = lax.broadcasted_iota(jnp.int32, (tq, tk), 0)
    kpos = lax.broadcasted_iota(jnp.int32, (tq, tk), 1)
    causal = kpos <= qpos
    halves = []
    for hh in range(2):
        qm = jnp.where((lane // HEAD_DIM) == hh, q, jnp.zeros_like(q))
        cq = jnp.sum(jnp.where(lane == 2 * hp + hh, cq_tile, 0.0), axis=-1, keepdims=True)

        def scores(kj):
            rows = pl.ds(pl.multiple_of(kj * tk, tk), tk)
            s = _dot_nt(qm, k_ref[rows, :]) * (HEAD_DIM ** -0.5)
            return s + cq - ck_ref[hh, pl.ds(kj, 1), :], rows

        def update(s, rows, carry):
            m, l, acc = carry
            m_new = jnp.maximum(m, s.max(axis=-1, keepdims=True))
            alpha = jnp.exp(m - m_new)
            p = jnp.exp(s - m_new)
            l = alpha * l + p.sum(axis=-1, keepdims=True)
            acc = alpha * acc + _dot(p.astype(BF16), v_ref[rows, :])
            return m_new, l, acc

        def body(kj, carry):
            s, rows = scores(kj)
            return update(s, rows, carry)

        init = (jnp.full((tq, 1), NEG, F32), jnp.zeros((tq, 1), F32), jnp.zeros((tq, LANES), F32))
        carry = lax.fori_loop(0, qi, body, init)
        s, rows = scores(qi)
        m, l, acc = update(jnp.where(causal, s, NEG), rows, carry)
        halves.append(acc / l)
    o_ref[...] = jnp.where(lane < HEAD_DIM, halves[0], halves[1]).astype(o_ref.dtype)


def _fox_attention(qkv, csum, ck, bsz, seq, t=512):
    width = FOX_HEADS * HEAD_DIM
    n_pairs = width // LANES
    t = min(t, seq)
    nq = seq // t
    return pl.pallas_call(
        functools.partial(_fox_kernel, tk=t),
        grid=(bsz, n_pairs, nq),
        in_specs=[
            pl.BlockSpec((t, LANES), lambda b, h, i: (b * nq + i, h)),
            pl.BlockSpec((seq, LANES), lambda b, h, i: (b, n_pairs + h)),
            pl.BlockSpec((seq, LANES), lambda b, h, i: (b, 2 * n_pairs + h)),
            pl.BlockSpec((t, LANES), lambda b, h, i: (b * nq + i, 0)),
            pl.BlockSpec((None, None, 2, nq, t), lambda b, h, i: (b, h, 0, 0, 0)),
        ],
        out_specs=pl.BlockSpec((t, LANES), lambda b, h, i: (b * nq + i, h)),
        out_shape=jax.ShapeDtypeStruct((bsz * seq, width), BF16),
        compiler_params=_params(3),
        name="fox_attention",
    )(qkv, qkv, qkv, csum, ck)


def _fox_layer(x, g, w_in, w_fg, b_fg, w_out, bsz, seq, t=512):
    width = FOX_HEADS * HEAD_DIM
    t = min(t, seq)
    qkv = _norm_matmul(x, g, w_in[:, :3 * width].astype(BF16), BF16)
    gate = _norm_matmul(x, g, w_in[:, 3 * width:].astype(BF16), F32)
    pad = LANES - FOX_HEADS
    z = _norm_matmul(x, g, jnp.pad(w_fg, ((0, 0), (0, pad))).astype(BF16), F32)
    csum = _forget_cumsum(z, jnp.pad(b_fg.astype(F32), (0, pad)).reshape(1, LANES), bsz, seq)
    ck = csum.reshape(bsz, seq, LANES)[:, :, :FOX_HEADS].transpose(0, 2, 1).reshape(bsz, FOX_HEADS // 2, 2, seq // t, t)
    attn = _fox_attention(qkv, csum, ck, bsz, seq, t)
    return _out_proj(attn, gate, 0, w_out.astype(BF16), x)


def kernel(x, p, norm_g, final_g, rel_bias, swa_w_in, swa_w_out, swa_sinks, conv_w_in, conv_kernel, conv_w_out, ssm_w_in, ssm_lam_re, ssm_lam_im, ssm_log_dt, ssm_b_re, ssm_b_im, ssm_c_re, ssm_c_im, ssm_d, ssm_w_glu, ssm_b_glu, ssm_w_out, fox_w_in, fox_w_fg, fox_b_fg, fox_w_out, ple_proj, ple_norm, ple_gate):
    bsz, seq, d_model = x.shape
    depth = p.shape[0]
    h = x.astype(F32).reshape(bsz * seq, d_model)
    for i in range(depth):
        mixer, j = i % N_MIXERS, i // N_MIXERS
        if mixer == 0:
            h = _swa_layer(h, norm_g[i], swa_w_in[j], swa_w_out[j], swa_sinks[j], rel_bias, bsz, seq)
        elif mixer == 1:
            h = _conv_layer(h, norm_g[i], conv_w_in[j], conv_kernel[j], conv_w_out[j], seq)
        elif mixer == 2:
            h = _ssm_layer(h, norm_g[i], ssm_w_in[j], ssm_lam_re[j], ssm_lam_im[j], ssm_log_dt[j], ssm_b_re[j],
                           ssm_b_im[j], ssm_c_re[j], ssm_c_im[j], ssm_d[j], ssm_w_glu[j], ssm_b_glu[j], ssm_w_out[j],
                           bsz, seq)
        else:
            h = _fox_layer(h, norm_g[i], fox_w_in[j], fox_w_fg[j], fox_b_fg[j], fox_w_out[j], bsz, seq)
        h = _ple(h, ple_norm[i], ple_gate[i].astype(BF16), p[i].reshape(bsz * seq, -1), ple_proj[i].astype(BF16))
    return _final_norm(h, final_g).reshape(bsz, seq, d_model).astype(x.dtype)
```

```python
import functools
import math

import numpy as np
import jax
import jax.numpy as jnp
from jax import lax
from jax.experimental import pallas as pl
from jax.experimental.pallas import tpu as pltpu

F32 = jnp.float32
BF16 = jnp.bfloat16

EPS = 1e-6
N_MIXERS = 4
PLE_DIM = 256

SWA_HEADS = 32
SWA_KV_HEADS = 4
SWA_GROUP = SWA_HEADS // SWA_KV_HEADS
HEAD_DIM = 64
SWA_BLOCK = 128
WINDOW = 128
REL_BUCKETS = 32
REL_MAX_DIST = 128

CONV_TAPS = 3

SSM_GROUP = 16
SSM_STATE = 64
SSM_CHUNK = 16

FOX_HEADS = 32

LANES = 128
GROUPS_PER_TILE = LANES // SSM_GROUP
VMEM_LIMIT = 48 * 1024 * 1024

NEG = float(jnp.finfo(jnp.float32).min)


def _params(n_axes):
    return pltpu.CompilerParams(dimension_semantics=("arbitrary",) * n_axes, vmem_limit_bytes=VMEM_LIMIT)


def _dot(a, b):
    return jnp.dot(a, b, preferred_element_type=F32)


def _dot_nt(a, b):
    return lax.dot_general(a, b, (((1,), (1,)), ((), ())), preferred_element_type=F32)


def _rmsnorm_rows(x, g):
    return x * lax.rsqrt(jnp.mean(x * x, axis=-1, keepdims=True) + EPS) * g


def _silu(x):
    return x * jax.nn.sigmoid(x)


def _tile_cols(w, tn):
    k, n = w.shape
    return w.astype(BF16).reshape(k, n // tn, tn).transpose(1, 0, 2)


ROW_CHUNK = 64


def _for_row_chunks(n_rows, fn):
    def body(c, carry):
        fn(pl.ds(pl.multiple_of(c * ROW_CHUNK, ROW_CHUNK), ROW_CHUNK))
        return carry
    lax.fori_loop(0, n_rows // ROW_CHUNK, body, 0)


def _norm_matmul_kernel(x_ref, g_ref, w_ref, o_ref, a_ref):
    @pl.when(pl.program_id(1) == 0)
    def _():
        def chunk(rows):
            a_ref[rows, :] = _rmsnorm_rows(x_ref[rows, :], g_ref[...]).astype(BF16)
        _for_row_chunks(a_ref.shape[0], chunk)

    o_ref[...] = _dot(a_ref[...], w_ref[...]).astype(o_ref.dtype)


def _norm_matmul(x, g, w, out_dtype, tm=1024, tn=512):
    m, k = x.shape
    n = w.shape[1]
    tm, tn = min(tm, m), min(tn, n)
    return pl.pallas_call(
        _norm_matmul_kernel,
        grid=(m // tm, n // tn),
        in_specs=[
            pl.BlockSpec((tm, k), lambda i, j: (i, 0)),
            pl.BlockSpec((1, k), lambda i, j: (0, 0)),
            pl.BlockSpec((None, k, tn), lambda i, j: (j, 0, 0)),
        ],
        out_specs=pl.BlockSpec((tm, tn), lambda i, j: (i, j)),
        out_shape=jax.ShapeDtypeStruct((m, n), out_dtype),
        scratch_shapes=[pltpu.VMEM((tm, k), BF16)],
        compiler_params=_params(2),
        name="norm_matmul",
    )(x, g.reshape(1, k), _tile_cols(w, tn))


def _out_proj_gated_kernel(a_ref, gate_ref, w_ref, x_ref, o_ref, s_ref):
    @pl.when(pl.program_id(1) == 0)
    def _():
        def chunk(rows):
            s_ref[rows, :] = (a_ref[rows, :].astype(F32) * _silu(gate_ref[rows, :])).astype(BF16)
        _for_row_chunks(s_ref.shape[0], chunk)

    o_ref[...] = x_ref[...] + _dot(s_ref[...], w_ref[...])


def _out_proj_kernel(a_ref, w_ref, x_ref, o_ref):
    o_ref[...] = x_ref[...] + _dot(a_ref[...], w_ref[...])


def _out_proj(a, gate_arr, gate_col_block, w, x, tm=512, tn=512):
    m, k = a.shape
    n = w.shape[1]
    tm, tn = min(tm, m), min(tn, n)
    a_spec = pl.BlockSpec((tm, k), lambda i, j: (i, 0))
    w_spec = pl.BlockSpec((None, k, tn), lambda i, j: (j, 0, 0))
    x_spec = pl.BlockSpec((tm, tn), lambda i, j: (i, j))
    w = _tile_cols(w, tn)
    common = dict(
        grid=(m // tm, n // tn),
        out_specs=pl.BlockSpec((tm, tn), lambda i, j: (i, j)),
        out_shape=jax.ShapeDtypeStruct((m, n), F32),
        compiler_params=_params(2),
    )
    if gate_arr is None:
        return pl.pallas_call(_out_proj_kernel, in_specs=[a_spec, w_spec, x_spec], name="out_proj", **common)(a, w, x)
    gate_spec = pl.BlockSpec((tm, k), lambda i, j: (i, gate_col_block))
    return pl.pallas_call(
        _out_proj_gated_kernel,
        in_specs=[a_spec, gate_spec, w_spec, x_spec],
        scratch_shapes=[pltpu.VMEM((tm, k), BF16)],
        name="out_proj_gated",
        **common,
    )(a, gate_arr, w, x)


def _ple_kernel(x_ref, g_ref, wg_ref, p_ref, wp_ref, xt_ref, o_ref, a_ref):
    @pl.when(pl.program_id(1) == 0)
    def _():
        def chunk(rows):
            a_ref[rows, :] = _rmsnorm_rows(x_ref[rows, :], g_ref[...]).astype(BF16)
        _for_row_chunks(a_ref.shape[0], chunk)

    emb = _dot(p_ref[...].astype(BF16), wp_ref[...])
    gate = jax.nn.sigmoid(_dot(a_ref[...], wg_ref[...]))
    o_ref[...] = xt_ref[...] + emb * gate


def _ple(x, g, w_gate, p, w_proj, tm=1024, tn=512):
    m, k = x.shape
    n = w_gate.shape[1]
    pd = p.shape[1]
    tm, tn = min(tm, m), min(tn, n)
    return pl.pallas_call(
        _ple_kernel,
        grid=(m // tm, n // tn),
        in_specs=[
            pl.BlockSpec((tm, k), lambda i, j: (i, 0)),
            pl.BlockSpec((1, k), lambda i, j: (0, 0)),
            pl.BlockSpec((None, k, tn), lambda i, j: (j, 0, 0)),
            pl.BlockSpec((tm, pd), lambda i, j: (i, 0)),
            pl.BlockSpec((None, pd, tn), lambda i, j: (j, 0, 0)),
            pl.BlockSpec((tm, tn), lambda i, j: (i, j)),
        ],
        out_specs=pl.BlockSpec((tm, tn), lambda i, j: (i, j)),
        out_shape=jax.ShapeDtypeStruct((m, n), F32),
        scratch_shapes=[pltpu.VMEM((tm, k), BF16)],
        compiler_params=_params(2),
        name="ple",
    )(x, g.reshape(1, k), _tile_cols(w_gate, tn), p, _tile_cols(w_proj, tn), x)


def _final_norm_kernel(x_ref, g_ref, o_ref):
    o_ref[...] = _rmsnorm_rows(x_ref[...], g_ref[...])


def _final_norm(x, g, tm=256):
    m, k = x.shape
    tm = min(tm, m)
    return pl.pallas_call(
        _final_norm_kernel,
        grid=(m // tm,),
        in_specs=[pl.BlockSpec((tm, k), lambda i: (i, 0)), pl.BlockSpec((1, k), lambda i: (0, 0))],
        out_specs=pl.BlockSpec((tm, k), lambda i: (i, 0)),
        out_shape=jax.ShapeDtypeStruct((m, k), F32),
        compiler_params=_params(1),
        name="final_norm",
    )(x, g.reshape(1, k))


def _t5_bucket(dist):
    max_exact = REL_BUCKETS // 2
    d = np.maximum(dist, 1).astype(np.float32)
    large = max_exact + (np.log(d / max_exact) / np.log(REL_MAX_DIST / max_exact) * (REL_BUCKETS - max_exact)).astype(np.int32)
    large = np.minimum(large, REL_BUCKETS - 1)
    return np.where(dist < max_exact, dist, large).astype(np.int32)


def _swa_bucket_table():
    qi = np.arange(SWA_BLOCK)[None, :]
    kj = np.arange(2 * SWA_BLOCK)[:, None]
    return _t5_bucket(np.clip(qi + SWA_BLOCK - kj, 0, None))


def _swa_kernel(sink_ref, rel_ref, bucket_ref, q_ref, kp_ref, kc_ref, vp_ref, vc_ref, o_ref, bias_ref, qs_ref):
    n = pl.program_id(1)
    blk = SWA_BLOCK
    lane = lax.broadcasted_iota(jnp.int32, (1, LANES), 1)

    @pl.when(jnp.logical_and(pl.program_id(0) == 0, n == 0))
    def _():
        bucket = bucket_ref[...]
        kj = lax.broadcasted_iota(jnp.int32, (2 * blk, blk), 0)
        qi = lax.broadcasted_iota(jnp.int32, (2 * blk, blk), 1)
        band = jnp.logical_or(jnp.logical_and(kj < blk, kj > qi), jnp.logical_and(kj >= blk, kj - blk <= qi))

        def per_head(head, carry):
            acc = jnp.zeros((2 * blk, blk), F32)
            for b in range(REL_BUCKETS):
                acc = jnp.where(bucket == b, rel_ref[b, head], acc)
            bias_ref[head] = jnp.where(band, acc, NEG)
            return carry
        lax.fori_loop(0, SWA_HEADS, per_head, 0)

    for kvh in range(SWA_KV_HEADS):
        cols = slice(kvh * LANES, (kvh + 1) * LANES)
        heads = range(kvh * SWA_GROUP, (kvh + 1) * SWA_GROUP)
        for g, head in enumerate(heads):
            q_pair = q_ref[:, head // 2 * LANES:(head // 2 + 1) * LANES]
            qs_ref[g * blk:(g + 1) * blk, :] = jnp.where((lane // HEAD_DIM) == head % 2, q_pair, jnp.zeros_like(q_pair))
        keys = jnp.concatenate([kp_ref[:, cols], kc_ref[:, cols]], axis=0)
        vals = jnp.concatenate([vp_ref[:, cols], vc_ref[:, cols]], axis=0)
        s = _dot_nt(keys, qs_ref[...]) + jnp.concatenate([bias_ref[head] for head in heads], axis=1)
        s_prev = jnp.where(n > 0, s[:blk], NEG)
        s_cur = s[blk:]
        sink = jnp.concatenate([jnp.full((1, blk), sink_ref[head], F32) for head in heads], axis=1)
        m = jnp.maximum(jnp.maximum(s_prev, s_cur).max(axis=0, keepdims=True), sink)
        e_prev, e_cur = jnp.exp(s_prev - m), jnp.exp(s_cur - m)
        den = e_prev.sum(axis=0, keepdims=True) + e_cur.sum(axis=0, keepdims=True) + jnp.exp(sink - m)
        e = jnp.concatenate([e_prev, e_cur], axis=0).astype(BF16)
        out = _dot(vals.astype(F32).T.astype(BF16), e) / den
        for pair in range(SWA_GROUP // 2):
            head = kvh * SWA_GROUP + 2 * pair
            even = out[:HEAD_DIM, 2 * pair * blk:(2 * pair + 1) * blk]
            odd = out[HEAD_DIM:, (2 * pair + 1) * blk:(2 * pair + 2) * blk]
            o_ref[:, head // 2 * LANES:(head // 2 + 1) * LANES] = jnp.concatenate([even, odd], axis=0).T.astype(o_ref.dtype)


def _swa_attention(qkv, sinks, rel_bias, bsz, seq):
    nb = seq // SWA_BLOCK
    width = SWA_HEADS * HEAD_DIM
    kvw = SWA_KV_HEADS * LANES
    k_blk = width // kvw
    cur = lambda b, n: b * nb + n
    prev = lambda b, n: b * nb + jnp.maximum(n - 1, 0)
    smem = pl.BlockSpec(memory_space=pltpu.SMEM)
    return pl.pallas_call(
        _swa_kernel,
        grid=(bsz, nb),
        in_specs=[
            smem, smem,
            pl.BlockSpec((2 * SWA_BLOCK, SWA_BLOCK), lambda b, n: (0, 0)),
            pl.BlockSpec((SWA_BLOCK, width), lambda b, n: (cur(b, n), 0)),
            pl.BlockSpec((SWA_BLOCK, kvw), lambda b, n: (prev(b, n), k_blk)),
            pl.BlockSpec((SWA_BLOCK, kvw), lambda b, n: (cur(b, n), k_blk)),
            pl.BlockSpec((SWA_BLOCK, kvw), lambda b, n: (prev(b, n), k_blk + 1)),
            pl.BlockSpec((SWA_BLOCK, kvw), lambda b, n: (cur(b, n), k_blk + 1)),
        ],
        out_specs=pl.BlockSpec((SWA_BLOCK, width), lambda b, n: (cur(b, n), 0)),
        out_shape=jax.ShapeDtypeStruct((bsz * seq, width), BF16),
        scratch_shapes=[
            pltpu.VMEM((SWA_HEADS, 2 * SWA_BLOCK, SWA_BLOCK), F32),
            pltpu.VMEM((SWA_GROUP * SWA_BLOCK, LANES), BF16),
        ],
        compiler_params=_params(2),
        name="swa_attention",
    )(sinks.astype(F32), rel_bias.astype(F32), jnp.asarray(_swa_bucket_table()), qkv, qkv, qkv, qkv, qkv)


def _dup_heads(w, n_heads):
    k = w.shape[0]
    w = w.reshape(k, n_heads, 1, HEAD_DIM)
    return jnp.broadcast_to(w, (k, n_heads, 2, HEAD_DIM)).reshape(k, n_heads * 2 * HEAD_DIM)


def _swa_layer(x, g, w_in, w_out, sinks, rel_bias, bsz, seq):
    width = SWA_HEADS * HEAD_DIM
    kvw = SWA_KV_HEADS * HEAD_DIM
    w_q, w_k, w_v, w_g = (w_in[:, :width], w_in[:, width:width + kvw],
                          w_in[:, width + kvw:width + 2 * kvw], w_in[:, width + 2 * kvw:])
    w_qkv = jnp.concatenate([w_q * (HEAD_DIM ** -0.5), _dup_heads(w_k, SWA_KV_HEADS), _dup_heads(w_v, SWA_KV_HEADS)],
                            axis=1).astype(BF16)
    qkv = _norm_matmul(x, g, w_qkv, BF16)
    gate = _norm_matmul(x, g, w_g.astype(BF16), F32)
    attn = _swa_attention(qkv, sinks, rel_bias, bsz, seq)
    return _out_proj(attn, gate, 0, w_out.astype(BF16), x)


HALO = 8


def _conv_kernel(x_ref, g_ref, wb_ref, wc_ref, wu_ref, wg_ref, ck_ref, o_ref, a_ref, z_ref, carry_ref, *, tiles_per_seq):
    i, j = pl.program_id(0), pl.program_id(1)
    tm = a_ref.shape[0]

    @pl.when(j == 0)
    def _():
        def chunk(rows):
            a_ref[rows, :] = _rmsnorm_rows(x_ref[rows, :], g_ref[...]).astype(BF16)
        _for_row_chunks(tm, chunk)

    a = a_ref[...]
    z = _dot(a, wc_ref[...]) * _dot(a, wu_ref[...])
    first = (i % tiles_per_seq) == 0

    @pl.when(first)
    def _():
        z_ref[:HALO, :] = jnp.zeros((HALO, z_ref.shape[1]), F32)

    @pl.when(jnp.logical_not(first))
    def _():
        z_ref[:HALO, :] = carry_ref[j]

    z_ref[HALO:, :] = z
    carry_ref[j] = z[tm - HALO:, :]
    conv = z_ref[HALO - 2:HALO - 2 + tm, :] * ck_ref[0:1, :]
    conv = conv + z_ref[HALO - 1:HALO - 1 + tm, :] * ck_ref[1:2, :]
    conv = conv + z * ck_ref[2:3, :]
    y = _dot(a, wb_ref[...]) * conv
    o_ref[...] = (y * _silu(_dot(a, wg_ref[...]))).astype(o_ref.dtype)


def _conv_mixer(x, g, w_in, conv_kernel, seq, tm=512, tn=512):
    m, k = x.shape
    width = w_in.shape[1] // 4
    tm, tn = min(tm, seq), min(tn, width)
    nt = width // tn
    w_spec = lambda q: pl.BlockSpec((None, k, tn), lambda i, j: (q * nt + j, 0, 0))
    w_in = _tile_cols(w_in, tn)
    return pl.pallas_call(
        functools.partial(_conv_kernel, tiles_per_seq=seq // tm),
        grid=(m // tm, nt),
        in_specs=[
            pl.BlockSpec((tm, k), lambda i, j: (i, 0)),
            pl.BlockSpec((1, k), lambda i, j: (0, 0)),
            w_spec(0), w_spec(1), w_spec(2), w_spec(3),
            pl.BlockSpec((CONV_TAPS, tn), lambda i, j: (0, j)),
        ],
        out_specs=pl.BlockSpec((tm, tn), lambda i, j: (i, j)),
        out_shape=jax.ShapeDtypeStruct((m, width), BF16),
        scratch_shapes=[
            pltpu.VMEM((tm, k), BF16),
            pltpu.VMEM((HALO + tm, tn), F32),
            pltpu.VMEM((nt, HALO, tn), F32),
        ],
        compiler_params=_params(2),
        name="conv_mixer",
    )(x, g.reshape(1, k), w_in, w_in, w_in, w_in, conv_kernel.astype(F32))


def _conv_layer(x, g, w_in, conv_kernel, w_out, seq):
    a = _conv_mixer(x, g, w_in.astype(BF16), conv_kernel, seq)
    return _out_proj(a, None, 0, w_out.astype(BF16), x)


def _ssm_operators(lam_re, lam_im, log_dt, b_re, b_im, c_re, c_im, d_skip):
    n_groups = lam_re.shape[0]
    n_oct = n_groups // GROUPS_PER_TILE
    L, C, N = SSM_CHUNK, SSM_GROUP, SSM_STATE
    dt = jnp.exp(log_dt.astype(F32))[:, None]
    lr, li = lam_re.astype(F32), lam_im.astype(F32)
    mag = jnp.exp(lr * dt)
    ab_re, ab_im = mag * jnp.cos(li * dt), mag * jnp.sin(li * dt)
    den = lr * lr + li * li
    nr = ab_re - 1.0
    coef_re = ((nr * lr + ab_im * li) / den)[..., None]
    coef_im = ((ab_im * lr - nr * li) / den)[..., None]
    br, bi = b_re.astype(F32), b_im.astype(F32)
    bb_re = coef_re * br - coef_im * bi
    bb_im = coef_re * bi + coef_im * br
    cr, ci = c_re.astype(F32), c_im.astype(F32)
    p_re, p_im = [jnp.ones_like(ab_re)], [jnp.zeros_like(ab_im)]
    for _ in range(L):
        p_re, p_im = (p_re + [p_re[-1] * ab_re - p_im[-1] * ab_im], p_im + [p_re[-1] * ab_im + p_im[-1] * ab_re])
    p_re, p_im = jnp.stack(p_re), jnp.stack(p_im)
    w_re = cr[None] * p_re[:L, :, None, :] - ci[None] * p_im[:L, :, None, :]
    w_im = cr[None] * p_im[:L, :, None, :] + ci[None] * p_re[:L, :, None, :]
    bt_re, bt_im = bb_re.transpose(0, 2, 1), bb_im.transpose(0, 2, 1)
    lag_k = jnp.sum(w_re[:, :, None] * bt_re[None, :, :, None] - w_im[:, :, None] * bt_im[None, :, :, None], axis=-1)
    q_re, q_im = p_re[:L][::-1], p_im[:L][::-1]
    bc_re = q_re[:, :, None, :] * bt_re[None] - q_im[:, :, None, :] * bt_im[None]
    bc_im = q_re[:, :, None, :] * bt_im[None] + q_im[:, :, None, :] * bt_re[None]
    o_re = cr[None] * p_re[1:, :, None, :] - ci[None] * p_im[1:, :, None, :]
    o_im = cr[None] * p_im[1:, :, None, :] + ci[None] * p_re[1:, :, None, :]
    octs = lambda t: t.reshape(L, n_oct, GROUPS_PER_TILE, C, -1)
    to_lag = lambda t: octs(t).transpose(1, 0, 3, 2, 4).reshape(n_oct, L, C, LANES)
    to_rows = lambda t: octs(t).transpose(1, 0, 3, 2, 4).reshape(n_oct, L * C, GROUPS_PER_TILE * N)
    to_cols = lambda t: octs(t).transpose(1, 2, 4, 0, 3).reshape(n_oct, GROUPS_PER_TILE * N, L * C)
    oct_vec = lambda t: t.reshape(n_oct, 1, -1)
    return dict(
        lag=to_lag(lag_k).astype(BF16),
        bc_re=to_rows(bc_re).astype(BF16), bc_im=to_rows(bc_im).astype(BF16),
        oc_re=to_cols(o_re).astype(BF16), oc_im_neg=to_cols(-o_im).astype(BF16),
        al_re=oct_vec(p_re[L]), al_im=oct_vec(p_im[L]), d=oct_vec(d_skip.astype(F32)),
    )


def _ssm_expanders():
    L, C, N, G8 = SSM_CHUNK, SSM_GROUP, SSM_STATE, GROUPS_PER_TILE
    wide = np.arange(L * LANES)
    w_step, w_group, w_chan = wide // LANES, (wide // C) % G8, wide % C
    small = np.arange(L * C)
    s_step, s_chan = small // C, small % C
    state_group = np.arange(G8 * N) // N
    spread = (w_step[:, None] == s_step[None, :]) & (w_chan[:, None] == s_chan[None, :])
    lane = np.arange(LANES)
    as_bf16 = lambda a: jnp.asarray(a.astype(np.float32), dtype=BF16)
    return dict(
        spread_rows=as_bf16(spread), spread_cols=as_bf16(spread.T),
        keep_rows=jnp.asarray((w_group[:, None] == state_group[None, :]).astype(np.float32)),
        keep_cols=jnp.asarray((state_group[:, None] == w_group[None, :]).astype(np.float32)),
        spread_lag=as_bf16((lane % C)[:, None] == np.arange(C)[None, :]),
        keep_lag=jnp.asarray(((lane // C)[:, None] == (lane // C)[None, :]).astype(np.float32)),
    )


def _ssm_kernel(u_ref, lag_ref, bxr_ref, bxi_ref, cxr_ref, cxi_ref, alr_ref, ali_ref, d_ref,
                sr_ref, sc_ref, kr_ref, kc_ref, sl_ref, kl_ref, o_ref,
                ub_ref, bcr_ref, bci_ref, ocr_ref, oci_ref, zr_ref, zi_ref, hr_ref, hi_ref, *, bsz):
    L = u_ref.shape[0]
    rows = u_ref.shape[1]
    chunks = rows // bsz
    tile = 2 * LANES
    for r in range(L):
        ub_ref[:, r * LANES:(r + 1) * LANES] = u_ref[r].astype(BF16)
    bcr_ref[...] = (_dot(sr_ref[...], bxr_ref[...]) * kr_ref[...]).astype(BF16)
    bci_ref[...] = (_dot(sr_ref[...], bxi_ref[...]) * kr_ref[...]).astype(BF16)
    zr_ref[...] = _dot(ub_ref[...], bcr_ref[...])
    zi_ref[...] = _dot(ub_ref[...], bci_ref[...])
    a_re, a_im = alr_ref[...], ali_ref[...]

    def step(k, carry):
        new = []
        for b in range(bsz):
            h_re, h_im = carry[2 * b], carry[2 * b + 1]
            row = pl.ds(b * chunks + k, 1)
            hr_ref[row, :] = h_re
            hi_ref[row, :] = h_im
            new.append(a_re * h_re - a_im * h_im + zr_ref[row, :])
            new.append(a_re * h_im + a_im * h_re + zi_ref[row, :])
        return tuple(new)

    zero = jnp.zeros((1, a_re.shape[1]), F32)
    lax.fori_loop(0, chunks, step, (zero,) * (2 * bsz))

    ocr_ref[...] = (_dot(cxr_ref[...], sc_ref[...]) * kc_ref[...]).astype(BF16)
    oci_ref[...] = (_dot(cxi_ref[...], sc_ref[...]) * kc_ref[...]).astype(BF16)
    lag = [(_dot(sl_ref[...], lag_ref[d]) * kl_ref[...]).astype(BF16) for d in range(L)]
    lag_tile = []
    for dd in range(L // 2):
        below = lag[2 * dd - 1] if dd > 0 else jnp.zeros((LANES, LANES), BF16)
        lag_tile.append(jnp.concatenate([jnp.concatenate([lag[2 * dd], lag[2 * dd + 1]], axis=1),
                                         jnp.concatenate([below, lag[2 * dd]], axis=1)], axis=0))
    hb_re, hb_im = hr_ref[...].astype(BF16), hi_ref[...].astype(BF16)
    for t2 in range(L // 2):
        cols = slice(t2 * tile, (t2 + 1) * tile)
        y = _dot(hb_re, ocr_ref[:, cols]) + _dot(hb_im, oci_ref[:, cols])
        for r2 in range(t2 + 1):
            y = y + _dot(ub_ref[:, r2 * tile:(r2 + 1) * tile], lag_tile[t2 - r2])
        for half in range(2):
            t = 2 * t2 + half
            o_ref[t] = jax.nn.gelu(y[:, half * LANES:(half + 1) * LANES] + d_ref[...] * u_ref[t]).astype(o_ref.dtype)


def _ssm_core(u, ops, bsz, seq):
    m, width = u.shape
    L, C = SSM_CHUNK, SSM_GROUP
    n_oct = width // LANES
    chunks = seq // L
    rows = bsz * chunks
    ow = L * LANES
    uc = u.reshape(bsz, chunks, L, n_oct, LANES).transpose(3, 2, 0, 1, 4).reshape(n_oct, L, rows, LANES)
    n_state = GROUPS_PER_TILE * SSM_STATE
    ex = _ssm_expanders()
    per_oct = lambda *shape: pl.BlockSpec((None,) + shape, lambda p: (p,) + (0,) * len(shape))
    const = lambda a: pl.BlockSpec(a.shape, lambda p: (0,) * a.ndim)
    consts = [ex['spread_rows'], ex['spread_cols'], ex['keep_rows'], ex['keep_cols'], ex['spread_lag'], ex['keep_lag']]
    yc = pl.pallas_call(
        functools.partial(_ssm_kernel, bsz=bsz),
        grid=(n_oct,),
        in_specs=[
            per_oct(L, rows, LANES), per_oct(L, C, LANES), per_oct(L * C, n_state), per_oct(L * C, n_state),
            per_oct(n_state, L * C), per_oct(n_state, L * C), per_oct(1, n_state), per_oct(1, n_state), per_oct(1, LANES),
        ] + [const(a) for a in consts],
        out_specs=per_oct(L, rows, LANES),
        out_shape=jax.ShapeDtypeStruct((n_oct, L, rows, LANES), BF16),
        scratch_shapes=[pltpu.VMEM((rows, ow), BF16)] + [pltpu.VMEM((ow, n_state), BF16)] * 2
        + [pltpu.VMEM((n_state, ow), BF16)] * 2 + [pltpu.VMEM((rows, n_state), F32)] * 4,
        compiler_params=_params(1),
        name="ssm_core",
    )(uc, ops['lag'], ops['bc_re'], ops['bc_im'], ops['oc_re'], ops['oc_im_neg'], ops['al_re'], ops['al_im'], ops['d'], *consts)
    return yc.reshape(n_oct, L, bsz, chunks, LANES).transpose(2, 3, 1, 0, 4).reshape(m, width)


def _glu_kernel(y_ref, wa_ref, wb_ref, ba_ref, bb_ref, gate_ref, o_ref):
    y = y_ref[...]
    ga = _dot(y, wa_ref[...]) + ba_ref[...]
    gb = _dot(y, wb_ref[...]) + bb_ref[...]
    o_ref[...] = ((ga * jax.nn.sigmoid(gb)) * _silu(gate_ref[...])).astype(o_ref.dtype)


def _glu(y, w_glu, b_glu, proj, tm=512, tn=512):
    m, k = y.shape
    width = w_glu.shape[1] // 2
    tm, tn = min(tm, m), min(tn, width)
    nt = width // tn
    w_tiles = _tile_cols(w_glu, tn)
    return pl.pallas_call(
        _glu_kernel,
        grid=(m // tm, nt),
        in_specs=[
            pl.BlockSpec((tm, k), lambda i, j: (i, 0)),
            pl.BlockSpec((None, k, tn), lambda i, j: (j, 0, 0)),
            pl.BlockSpec((None, k, tn), lambda i, j: (nt + j, 0, 0)),
            pl.BlockSpec((1, tn), lambda i, j: (0, j)),
            pl.BlockSpec((1, tn), lambda i, j: (0, nt + j)),
            pl.BlockSpec((tm, tn), lambda i, j: (i, nt + j)),
        ],
        out_specs=pl.BlockSpec((tm, tn), lambda i, j: (i, j)),
        out_shape=jax.ShapeDtypeStruct((m, width), BF16),
        compiler_params=_params(2),
        name="ssm_glu",
    )(y, w_tiles, w_tiles, b_glu, b_glu, proj)


def _ssm_layer(x, g, w_in, lam_re, lam_im, log_dt, b_re, b_im, c_re, c_im, d_skip, w_glu, b_glu, w_out, bsz, seq):
    width = w_in.shape[1] // 2
    proj = _norm_matmul(x, g, w_in.astype(BF16), F32)
    ops = _ssm_operators(lam_re, lam_im, log_dt, b_re, b_im, c_re, c_im, d_skip)
    y = _ssm_core(proj[:, :width], ops, bsz, seq)
    a = _glu(y, w_glu.astype(BF16), b_glu.astype(F32).reshape(1, -1), proj)
    return _out_proj(a, None, 0, w_out.astype(BF16), x)


CUM_BLOCK = 128


def _split3(x):
    x1 = x.astype(BF16)
    r1 = x - x1.astype(F32)
    x2 = r1.astype(BF16)
    x3 = (r1 - x2.astype(F32)).astype(BF16)
    return x1, x2, x3


def _forget_cumsum_kernel(z_ref, b_ref, o_ref):
    n_blocks = z_ref.shape[0] // CUM_BLOCK
    ri = lax.broadcasted_iota(jnp.int32, (CUM_BLOCK, CUM_BLOCK), 0)
    ci = lax.broadcasted_iota(jnp.int32, (CUM_BLOCK, CUM_BLOCK), 1)
    tri = jnp.where(ci <= ri, 1.0, 0.0).astype(BF16)

    def body(i, carry):
        rows = pl.ds(pl.multiple_of(i * CUM_BLOCK, CUM_BLOCK), CUM_BLOCK)
        z = z_ref[rows, :] + b_ref[...]
        log_f = jnp.minimum(z, 0.0) - jnp.log1p(jnp.exp(-jnp.abs(z)))
        x1, x2, x3 = _split3(log_f)
        c = _dot(tri, x1) + _dot(tri, x2) + _dot(tri, x3) + carry
        o_ref[rows, :] = c
        return c[CUM_BLOCK - 1:, :]

    lax.fori_loop(0, n_blocks, body, jnp.zeros((1, z_ref.shape[1]), F32))


def _forget_cumsum(z, b, bsz, seq):
    lanes = z.shape[1]
    return pl.pallas_call(
        _forget_cumsum_kernel,
        grid=(bsz,),
        in_specs=[pl.BlockSpec((seq, lanes), lambda i: (i, 0)), pl.BlockSpec((1, lanes), lambda i: (0, 0))],
        out_specs=pl.BlockSpec((seq, lanes), lambda i: (i, 0)),
        out_shape=jax.ShapeDtypeStruct(z.shape, F32),
        compiler_params=_params(1),
        name="forget_cumsum",
    )(z, b)


def _fox_kernel(q_ref, k_ref, v_ref, cq_ref, ck_ref, o_ref, ka_ref, vt_ref, qa_ref, s_ref, p_ref, acc_ref, pv_ref, *, tk):
    hp, qi = pl.program_id(1), pl.program_id(2)
    tq = q_ref.shape[0]
    lane = lax.broadcasted_iota(jnp.int32, (1, LANES), 1)
    free = (HEAD_DIM, 0)
    n_pieces = 3

    def bias_lanes(c, hh, c_offset, one_offset):
        rr = lax.broadcasted_iota(jnp.int32, (n_pieces * LANES, LANES), 0)
        cc = lax.broadcasted_iota(jnp.int32, (n_pieces * LANES, LANES), 1)
        place = jnp.logical_and(rr % LANES == 2 * hp + hh, cc == free[hh] + c_offset + rr // LANES)
        placed = _dot(jnp.concatenate(_split3(c), axis=1), jnp.where(place, 1.0, 0.0).astype(BF16))
        first = free[hh] + one_offset
        return (placed + jnp.where(jnp.logical_and(lane >= first, lane < first + n_pieces), 1.0, 0.0)).astype(BF16)

    @pl.when(qi == 0)
    def _():
        def chunk(j, carry):
            rows = pl.ds(pl.multiple_of(j * tk, tk), tk)
            keys = k_ref[rows, :]
            neg_c = -ck_ref[rows, :]
            for hh in range(2):
                ka_ref[hh, j] = jnp.where((lane // HEAD_DIM) == hh, keys, bias_lanes(neg_c, hh, 0, n_pieces))
            vt_ref[j] = v_ref[rows, :].astype(F32).T.astype(BF16)
            return carry
        lax.fori_loop(0, k_ref.shape[0] // tk, chunk, 0)

    cq = cq_ref[...]
    for hh in range(2):
        qa_ref[hh] = jnp.where((lane // HEAD_DIM) == hh, q_ref[...], bias_lanes(cq, hh, n_pieces, 0))

    def scores(hh, j):
        s_ref[hh] = _dot_nt(ka_ref[hh, j], qa_ref[hh])

    rows8 = 8
    rows16 = 16

    def absorb(hh, j, m, l, diagonal):
        def strip(r, n):
            blk = s_ref[hh, r:r + n, :]
            if diagonal:
                kpos = lax.broadcasted_iota(jnp.int32, (n, tq), 0)
                qpos = lax.broadcasted_iota(jnp.int32, (n, tq), 1)
                blk = jnp.where(kpos + r <= qpos, blk, NEG)
            return blk
        top = strip(0, rows8)
        for r in range(rows8, tk, rows8):
            top = jnp.maximum(top, strip(r, rows8))
        m_new = jnp.maximum(m, top.max(axis=0, keepdims=True))
        alpha = jnp.exp(m - m_new)
        m_rows = jnp.broadcast_to(m_new, (rows16, tq))
        total = None
        for r in range(0, tk, rows16):
            p = jnp.exp(strip(r, rows16) - m_rows)
            total = p if total is None else total + p
            p_ref[hh, r:r + rows16, :] = p.astype(BF16)
        acc_ref[hh] = alpha * (acc_ref[hh] + pv_ref[hh])
        pv_ref[hh] = _dot(vt_ref[j], p_ref[hh])
        return m_new, alpha * l + total.sum(axis=0, keepdims=True)

    def step(j, carry, diagonal):
        m0, l0, m1, l1 = carry
        scores(1, j)
        m0, l0 = absorb(0, j, m0, l0, diagonal)
        if not diagonal:
            scores(0, j + 1)
        m1, l1 = absorb(1, j, m1, l1, diagonal)
        return m0, l0, m1, l1

    acc_ref[...] = jnp.zeros(acc_ref.shape, F32)
    pv_ref[...] = jnp.zeros(pv_ref.shape, F32)
    scores(0, 0)
    init = (jnp.full((1, tq), NEG, F32), jnp.zeros((1, tq), F32)) * 2
    carry = lax.fori_loop(0, qi, lambda j, c: step(j, c, False), init)
    _, l0, _, l1 = step(qi, carry, True)
    row = lax.broadcasted_iota(jnp.int32, (LANES, 1), 0)
    out = jnp.where(row < HEAD_DIM, (acc_ref[0] + pv_ref[0]) / l0, (acc_ref[1] + pv_ref[1]) / l1)
    o_ref[...] = out.T.astype(o_ref.dtype)


def _fox_attention(qkv, csum, bsz, seq, t=512):
    width = FOX_HEADS * HEAD_DIM
    n_pairs = width // LANES
    t = min(t, seq)
    nq = seq // t
    return pl.pallas_call(
        functools.partial(_fox_kernel, tk=t),
        grid=(bsz, n_pairs, nq),
        in_specs=[
            pl.BlockSpec((t, LANES), lambda b, h, i: (b * nq + i, h)),
            pl.BlockSpec((seq, LANES), lambda b, h, i: (b, n_pairs + h)),
            pl.BlockSpec((seq, LANES), lambda b, h, i: (b, 2 * n_pairs + h)),
            pl.BlockSpec((t, LANES), lambda b, h, i: (b * nq + i, 0)),
            pl.BlockSpec((seq, LANES), lambda b, h, i: (b, 0)),
        ],
        out_specs=pl.BlockSpec((t, LANES), lambda b, h, i: (b * nq + i, h)),
        out_shape=jax.ShapeDtypeStruct((bsz * seq, width), BF16),
        scratch_shapes=[
            pltpu.VMEM((2, nq, t, LANES), BF16), pltpu.VMEM((nq, LANES, t), BF16), pltpu.VMEM((2, t, LANES), BF16),
            pltpu.VMEM((2, t, t), F32), pltpu.VMEM((2, t, t), BF16),
            pltpu.VMEM((2, LANES, t), F32), pltpu.VMEM((2, LANES, t), F32),
        ],
        compiler_params=_params(3),
        name="fox_attention",
    )(qkv, qkv, qkv, csum, csum)


def _fox_layer(x, g, w_in, w_fg, b_fg, w_out, bsz, seq, t=512):
    width = FOX_HEADS * HEAD_DIM
    w_qkv = jnp.concatenate([w_in[:, :width] * (HEAD_DIM ** -0.5), w_in[:, width:3 * width]], axis=1)
    qkv = _norm_matmul(x, g, w_qkv, BF16)
    gate = _norm_matmul(x, g, w_in[:, 3 * width:], F32)
    pad = LANES - FOX_HEADS
    z = _norm_matmul(x, g, jnp.pad(w_fg, ((0, 0), (0, pad))), F32)
    csum = _forget_cumsum(z, jnp.pad(b_fg.astype(F32), (0, pad)).reshape(1, LANES), bsz, seq)
    attn = _fox_attention(qkv, csum, bsz, seq, t)
    return _out_proj(attn, gate, 0, w_out, x)


def kernel(x, p, norm_g, final_g, rel_bias, swa_w_in, swa_w_out, swa_sinks, conv_w_in, conv_kernel, conv_w_out, ssm_w_in, ssm_lam_re, ssm_lam_im, ssm_log_dt, ssm_b_re, ssm_b_im, ssm_c_re, ssm_c_im, ssm_d, ssm_w_glu, ssm_b_glu, ssm_w_out, fox_w_in, fox_w_fg, fox_b_fg, fox_w_out, ple_proj, ple_norm, ple_gate):
    bsz, seq, d_model = x.shape
    depth = p.shape[0]
    h = x.astype(F32).reshape(bsz * seq, d_model)
    for i in range(depth):
        mixer, j = i % N_MIXERS, i // N_MIXERS
        if mixer == 0:
            h = _swa_layer(h, norm_g[i], swa_w_in[j], swa_w_out[j], swa_sinks[j], rel_bias, bsz, seq)
        elif mixer == 1:
            h = _conv_layer(h, norm_g[i], conv_w_in[j], conv_kernel[j], conv_w_out[j], seq)
        elif mixer == 2:
            h = _ssm_layer(h, norm_g[i], ssm_w_in[j], ssm_lam_re[j], ssm_lam_im[j], ssm_log_dt[j], ssm_b_re[j],
                           ssm_b_im[j], ssm_c_re[j], ssm_c_im[j], ssm_d[j], ssm_w_glu[j], ssm_b_glu[j], ssm_w_out[j],
                           bsz, seq)
        else:
            h = _fox_layer(h, norm_g[i], fox_w_in[j], fox_w_fg[j], fox_b_fg[j], fox_w_out[j], bsz, seq)
        h = _ple(h, ple_norm[i], ple_gate[i].astype(BF16), p[i].reshape(bsz * seq, -1), ple_proj[i].astype(BF16))
    return _final_norm(h, final_g).reshape(bsz, seq, d_model).astype(x.dtype)
```

```python
import functools
import math

import numpy as np
import jax
import jax.numpy as jnp
from jax import lax
from jax.experimental import pallas as pl
from jax.experimental.pallas import tpu as pltpu

F32 = jnp.float32
BF16 = jnp.bfloat16

EPS = 1e-6
N_MIXERS = 4
PLE_DIM = 256

SWA_HEADS = 32
SWA_KV_HEADS = 4
SWA_GROUP = SWA_HEADS // SWA_KV_HEADS
HEAD_DIM = 64
SWA_BLOCK = 128
WINDOW = 128
REL_BUCKETS = 32
REL_MAX_DIST = 128

CONV_TAPS = 3

SSM_GROUP = 16
SSM_STATE = 64
SSM_CHUNK = 16

FOX_HEADS = 32

LANES = 128
GROUPS_PER_TILE = LANES // SSM_GROUP
VMEM_LIMIT = 48 * 1024 * 1024

NEG = float(jnp.finfo(jnp.float32).min)


def _params(n_axes):
    return pltpu.CompilerParams(dimension_semantics=("arbitrary",) * n_axes, vmem_limit_bytes=VMEM_LIMIT)


def _dot(a, b):
    return jnp.dot(a, b, preferred_element_type=F32)


def _dot_nt(a, b):
    return lax.dot_general(a, b, (((1,), (1,)), ((), ())), preferred_element_type=F32)


def _rmsnorm_rows(x, g):
    return x * lax.rsqrt(jnp.mean(x * x, axis=-1, keepdims=True) + EPS) * g


def _silu(x):
    return x * jax.nn.sigmoid(x)


def _tile_cols(w, tn):
    k, n = w.shape
    return w.astype(BF16).reshape(k, n // tn, tn).transpose(1, 0, 2)


ROW_CHUNK = 64


def _for_row_chunks(n_rows, fn):
    def body(c, carry):
        fn(pl.ds(pl.multiple_of(c * ROW_CHUNK, ROW_CHUNK), ROW_CHUNK))
        return carry
    lax.fori_loop(0, n_rows // ROW_CHUNK, body, 0)


def _norm_matmul_kernel(x_ref, g_ref, w_ref, o_ref, a_ref, *, silu_tiles):
    j = pl.program_id(1)

    @pl.when(j == 0)
    def _():
        def chunk(rows):
            a_ref[rows, :] = _rmsnorm_rows(x_ref[rows, :], g_ref[...]).astype(BF16)
        _for_row_chunks(a_ref.shape[0], chunk)

    acc = _dot(a_ref[...], w_ref[...])
    if silu_tiles is None:
        o_ref[...] = acc.astype(o_ref.dtype)
    else:
        gated = jnp.logical_and(j >= silu_tiles[0], j < silu_tiles[1])

        @pl.when(gated)
        def _():
            o_ref[...] = _silu(acc).astype(o_ref.dtype)

        @pl.when(jnp.logical_not(gated))
        def _():
            o_ref[...] = acc.astype(o_ref.dtype)


def _norm_matmul(x, g, w, out_dtype, silu_cols=None, tm=1024, tn=512):
    m, k = x.shape
    n = w.shape[1]
    tm, tn = min(tm, m), min(tn, n)
    silu_tiles = None if silu_cols is None else (silu_cols[0] // tn, silu_cols[1] // tn)
    return pl.pallas_call(
        functools.partial(_norm_matmul_kernel, silu_tiles=silu_tiles),
        grid=(m // tm, n // tn),
        in_specs=[
            pl.BlockSpec((tm, k), lambda i, j: (i, 0)),
            pl.BlockSpec((1, k), lambda i, j: (0, 0)),
            pl.BlockSpec((None, k, tn), lambda i, j: (j, 0, 0)),
        ],
        out_specs=pl.BlockSpec((tm, tn), lambda i, j: (i, j)),
        out_shape=jax.ShapeDtypeStruct((m, n), out_dtype),
        scratch_shapes=[pltpu.VMEM((tm, k), BF16)],
        compiler_params=_params(2),
        name="norm_matmul",
    )(x, g.reshape(1, k), _tile_cols(w, tn))


def _out_ple_kernel(a_ref, wo_ref, x_ref, g_ref, wg_ref, p_ref, wp_ref, gf_ref, o_ref, x1_ref, hn_ref, *, nt, final):
    j = pl.program_id(1)
    tn = x_ref.shape[1]

    @pl.when(j < nt)
    def _():
        x1_ref[j] = x_ref[...] + _dot(a_ref[...], wo_ref[j])

    def row_scale(ref, rows):
        ss = sum(jnp.sum(ref[t, rows, :] * ref[t, rows, :], axis=-1, keepdims=True) for t in range(nt))
        return lax.rsqrt(ss / (nt * tn) + EPS)

    @pl.when(j == nt)
    def _():
        def chunk(rows):
            inv = row_scale(x1_ref, rows)
            for t in range(nt):
                cols = slice(t * tn, (t + 1) * tn)
                hn_ref[rows, cols] = (x1_ref[t, rows, :] * inv * g_ref[:, cols]).astype(BF16)
        _for_row_chunks(hn_ref.shape[0], chunk)

    @pl.when(j >= nt)
    def _():
        t = j - nt
        emb = _dot(p_ref[...].astype(BF16), wp_ref[t])
        x2 = x1_ref[t] + emb * jax.nn.sigmoid(_dot(hn_ref[...], wg_ref[t]))
        if final:
            x1_ref[t] = x2
        else:
            o_ref[...] = x2

    if final:
        @pl.when(j == 2 * nt - 1)
        def _():
            def chunk(rows):
                inv = row_scale(x1_ref, rows)
                for t in range(nt):
                    cols = slice(t * tn, (t + 1) * tn)
                    o_ref[rows, cols] = x1_ref[t, rows, :] * inv * gf_ref[:, cols]
            _for_row_chunks(hn_ref.shape[0], chunk)


def _out_ple(a, w_out, x, g, w_gate, p, w_proj, final_g=None, tm=512, tn=512):
    m, k = a.shape
    n = w_out.shape[1]
    pd = p.shape[1]
    tm, tn = min(tm, m), min(tn, n)
    nt = n // tn
    final = final_g is not None
    resident = lambda rows: pl.BlockSpec((nt, rows, tn), lambda i, j: (0, 0, 0), pipeline_mode=pl.Buffered(1))
    row_vec = pl.BlockSpec((1, n), lambda i, j: (0, 0))
    if final:
        out_spec = pl.BlockSpec((tm, n), lambda i, j: (i, 0))
    else:
        out_spec = pl.BlockSpec((tm, tn), lambda i, j: (i, jnp.maximum(j - nt, 0)))
    return pl.pallas_call(
        functools.partial(_out_ple_kernel, nt=nt, final=final),
        grid=(m // tm, 2 * nt),
        in_specs=[
            pl.BlockSpec((tm, k), lambda i, j: (i, 0)),
            resident(k),
            pl.BlockSpec((tm, tn), lambda i, j: (i, jnp.minimum(j, nt - 1))),
            row_vec,
            resident(n),
            pl.BlockSpec((tm, pd), lambda i, j: (i, 0)),
            resident(pd),
            row_vec,
        ],
        out_specs=out_spec,
        out_shape=jax.ShapeDtypeStruct((m, n), F32),
        scratch_shapes=[pltpu.VMEM((nt, tm, tn), F32), pltpu.VMEM((tm, n), BF16)],
        compiler_params=_params(2),
        name="out_ple_final" if final else "out_ple",
    )(a, _tile_cols(w_out, tn), x, g.reshape(1, n), _tile_cols(w_gate, tn), p, _tile_cols(w_proj, tn),
      (final_g if final else g).reshape(1, n))


def _t5_bucket(dist):
    max_exact = REL_BUCKETS // 2
    d = np.maximum(dist, 1).astype(np.float32)
    large = max_exact + (np.log(d / max_exact) / np.log(REL_MAX_DIST / max_exact) * (REL_BUCKETS - max_exact)).astype(np.int32)
    large = np.minimum(large, REL_BUCKETS - 1)
    return np.where(dist < max_exact, dist, large).astype(np.int32)


def _swa_bucket_table():
    qi = np.arange(SWA_BLOCK)[None, :]
    kj = np.arange(2 * SWA_BLOCK)[:, None]
    return _t5_bucket(np.clip(qi + SWA_BLOCK - kj, 0, None))


def _swa_kernel(sink_ref, rel_ref, bucket_ref, q_ref, sg_ref, kp_ref, kc_ref, vp_ref, vc_ref, o_ref, bias_ref, qs_ref):
    n = pl.program_id(1)
    blk = SWA_BLOCK
    lane = lax.broadcasted_iota(jnp.int32, (1, LANES), 1)

    @pl.when(jnp.logical_and(pl.program_id(0) == 0, n == 0))
    def _():
        bucket = bucket_ref[...]
        kj = lax.broadcasted_iota(jnp.int32, (2 * blk, blk), 0)
        qi = lax.broadcasted_iota(jnp.int32, (2 * blk, blk), 1)
        band = jnp.logical_or(jnp.logical_and(kj < blk, kj > qi), jnp.logical_and(kj >= blk, kj - blk <= qi))

        def per_head(head, carry):
            acc = jnp.zeros((2 * blk, blk), F32)
            for b in range(REL_BUCKETS):
                acc = jnp.where(bucket == b, rel_ref[b, head], acc)
            bias_ref[head] = jnp.where(band, acc, NEG)
            return carry
        lax.fori_loop(0, SWA_HEADS, per_head, 0)

    for kvh in range(SWA_KV_HEADS):
        cols = slice(kvh * LANES, (kvh + 1) * LANES)
        heads = range(kvh * SWA_GROUP, (kvh + 1) * SWA_GROUP)
        for g, head in enumerate(heads):
            q_pair = q_ref[:, head // 2 * LANES:(head // 2 + 1) * LANES]
            qs_ref[g * blk:(g + 1) * blk, :] = jnp.where((lane // HEAD_DIM) == head % 2, q_pair, jnp.zeros_like(q_pair))
        keys = jnp.concatenate([kp_ref[:, cols], kc_ref[:, cols]], axis=0)
        vals = jnp.concatenate([vp_ref[:, cols], vc_ref[:, cols]], axis=0)
        s = _dot_nt(keys, qs_ref[...]) + jnp.concatenate([bias_ref[head] for head in heads], axis=1)
        s_prev = jnp.where(n > 0, s[:blk], NEG)
        s_cur = s[blk:]
        sink = jnp.concatenate([jnp.full((1, blk), sink_ref[head], F32) for head in heads], axis=1)
        m = jnp.maximum(jnp.maximum(s_prev, s_cur).max(axis=0, keepdims=True), sink)
        e_prev, e_cur = jnp.exp(s_prev - m), jnp.exp(s_cur - m)
        den = e_prev.sum(axis=0, keepdims=True) + e_cur.sum(axis=0, keepdims=True) + jnp.exp(sink - m)
        e = jnp.concatenate([e_prev, e_cur], axis=0).astype(BF16)
        out = _dot(vals.astype(F32).T.astype(BF16), e) / den
        for pair in range(SWA_GROUP // 2):
            head = kvh * SWA_GROUP + 2 * pair
            even = out[:HEAD_DIM, 2 * pair * blk:(2 * pair + 1) * blk]
            odd = out[HEAD_DIM:, (2 * pair + 1) * blk:(2 * pair + 2) * blk]
            cols = slice(head // 2 * LANES, (head // 2 + 1) * LANES)
            o_ref[:, cols] = (jnp.concatenate([even, odd], axis=0).T * sg_ref[:, cols].astype(F32)).astype(o_ref.dtype)


def _swa_attention(proj, sinks, rel_bias, bsz, seq):
    nb = seq // SWA_BLOCK
    width = SWA_HEADS * HEAD_DIM
    kvw = SWA_KV_HEADS * LANES
    k_blk = 2 * width // kvw
    cur = lambda b, n: b * nb + n
    prev = lambda b, n: b * nb + jnp.maximum(n - 1, 0)
    smem = pl.BlockSpec(memory_space=pltpu.SMEM)
    return pl.pallas_call(
        _swa_kernel,
        grid=(bsz, nb),
        in_specs=[
            smem, smem,
            pl.BlockSpec((2 * SWA_BLOCK, SWA_BLOCK), lambda b, n: (0, 0)),
            pl.BlockSpec((SWA_BLOCK, width), lambda b, n: (cur(b, n), 0)),
            pl.BlockSpec((SWA_BLOCK, width), lambda b, n: (cur(b, n), 1)),
            pl.BlockSpec((SWA_BLOCK, kvw), lambda b, n: (prev(b, n), k_blk)),
            pl.BlockSpec((SWA_BLOCK, kvw), lambda b, n: (cur(b, n), k_blk)),
            pl.BlockSpec((SWA_BLOCK, kvw), lambda b, n: (prev(b, n), k_blk + 1)),
            pl.BlockSpec((SWA_BLOCK, kvw), lambda b, n: (cur(b, n), k_blk + 1)),
        ],
        out_specs=pl.BlockSpec((SWA_BLOCK, width), lambda b, n: (cur(b, n), 0)),
        out_shape=jax.ShapeDtypeStruct((bsz * seq, width), BF16),
        scratch_shapes=[
            pltpu.VMEM((SWA_HEADS, 2 * SWA_BLOCK, SWA_BLOCK), F32),
            pltpu.VMEM((SWA_GROUP * SWA_BLOCK, LANES), BF16),
        ],
        compiler_params=_params(2),
        name="swa_attention",
    )(sinks.astype(F32), rel_bias.astype(F32), jnp.asarray(_swa_bucket_table()), proj, proj, proj, proj, proj, proj)


def _dup_heads(w, n_heads):
    k = w.shape[0]
    w = w.reshape(k, n_heads, 1, HEAD_DIM)
    return jnp.broadcast_to(w, (k, n_heads, 2, HEAD_DIM)).reshape(k, n_heads * 2 * HEAD_DIM)


def _swa_mixer(x, g, w_in, sinks, rel_bias, bsz, seq):
    width = SWA_HEADS * HEAD_DIM
    kvw = SWA_KV_HEADS * HEAD_DIM
    w_q, w_k, w_v, w_g = (w_in[:, :width], w_in[:, width:width + kvw],
                          w_in[:, width + kvw:width + 2 * kvw], w_in[:, width + 2 * kvw:])
    w = jnp.concatenate([w_q * (HEAD_DIM ** -0.5), w_g, _dup_heads(w_k, SWA_KV_HEADS), _dup_heads(w_v, SWA_KV_HEADS)], axis=1)
    proj = _norm_matmul(x, g, w, BF16, silu_cols=(width, 2 * width))
    return _swa_attention(proj, sinks, rel_bias, bsz, seq)


HALO = 8


def _conv_kernel(x_ref, g_ref, wb_ref, wc_ref, wu_ref, wg_ref, ck_ref, o_ref, a_ref, z_ref, carry_ref, *, tiles_per_seq):
    i, j = pl.program_id(0), pl.program_id(1)
    tm = a_ref.shape[0]

    @pl.when(j == 0)
    def _():
        def chunk(rows):
            a_ref[rows, :] = _rmsnorm_rows(x_ref[rows, :], g_ref[...]).astype(BF16)
        _for_row_chunks(tm, chunk)

    a = a_ref[...]
    z = _dot(a, wc_ref[...]) * _dot(a, wu_ref[...])
    first = (i % tiles_per_seq) == 0

    @pl.when(first)
    def _():
        z_ref[:HALO, :] = jnp.zeros((HALO, z_ref.shape[1]), F32)

    @pl.when(jnp.logical_not(first))
    def _():
        z_ref[:HALO, :] = carry_ref[j]

    z_ref[HALO:, :] = z
    carry_ref[j] = z[tm - HALO:, :]
    conv = z_ref[HALO - 2:HALO - 2 + tm, :] * ck_ref[0:1, :]
    conv = conv + z_ref[HALO - 1:HALO - 1 + tm, :] * ck_ref[1:2, :]
    conv = conv + z * ck_ref[2:3, :]
    y = _dot(a, wb_ref[...]) * conv
    o_ref[...] = (y * _silu(_dot(a, wg_ref[...]))).astype(o_ref.dtype)


def _conv_mixer(x, g, w_in, conv_kernel, seq, tm=512, tn=512):
    m, k = x.shape
    width = w_in.shape[1] // 4
    tm, tn = min(tm, seq), min(tn, width)
    nt = width // tn
    w_spec = lambda q: pl.BlockSpec((None, k, tn), lambda i, j: (q * nt + j, 0, 0))
    w_in = _tile_cols(w_in, tn)
    return pl.pallas_call(
        functools.partial(_conv_kernel, tiles_per_seq=seq // tm),
        grid=(m // tm, nt),
        in_specs=[
            pl.BlockSpec((tm, k), lambda i, j: (i, 0)),
            pl.BlockSpec((1, k), lambda i, j: (0, 0)),
            w_spec(0), w_spec(1), w_spec(2), w_spec(3),
            pl.BlockSpec((CONV_TAPS, tn), lambda i, j: (0, j)),
        ],
        out_specs=pl.BlockSpec((tm, tn), lambda i, j: (i, j)),
        out_shape=jax.ShapeDtypeStruct((m, width), BF16),
        scratch_shapes=[
            pltpu.VMEM((tm, k), BF16),
            pltpu.VMEM((HALO + tm, tn), F32),
            pltpu.VMEM((nt, HALO, tn), F32),
        ],
        compiler_params=_params(2),
        name="conv_mixer",
    )(x, g.reshape(1, k), w_in, w_in, w_in, w_in, conv_kernel.astype(F32))


def _ssm_operators(lam_re, lam_im, log_dt, b_re, b_im, c_re, c_im, d_skip):
    n_groups = lam_re.shape[0]
    n_oct = n_groups // GROUPS_PER_TILE
    L, C, N = SSM_CHUNK, SSM_GROUP, SSM_STATE
    dt = jnp.exp(log_dt.astype(F32))[:, None]
    lr, li = lam_re.astype(F32), lam_im.astype(F32)
    mag = jnp.exp(lr * dt)
    ab_re, ab_im = mag * jnp.cos(li * dt), mag * jnp.sin(li * dt)
    den = lr * lr + li * li
    nr = ab_re - 1.0
    coef_re = ((nr * lr + ab_im * li) / den)[..., None]
    coef_im = ((ab_im * lr - nr * li) / den)[..., None]
    br, bi = b_re.astype(F32), b_im.astype(F32)
    bb_re = coef_re * br - coef_im * bi
    bb_im = coef_re * bi + coef_im * br
    cr, ci = c_re.astype(F32), c_im.astype(F32)
    p_re, p_im = [jnp.ones_like(ab_re)], [jnp.zeros_like(ab_im)]
    for _ in range(L):
        p_re, p_im = (p_re + [p_re[-1] * ab_re - p_im[-1] * ab_im], p_im + [p_re[-1] * ab_im + p_im[-1] * ab_re])
    p_re, p_im = jnp.stack(p_re), jnp.stack(p_im)
    w_re = cr[None] * p_re[:L, :, None, :] - ci[None] * p_im[:L, :, None, :]
    w_im = cr[None] * p_im[:L, :, None, :] + ci[None] * p_re[:L, :, None, :]
    bt_re, bt_im = bb_re.transpose(0, 2, 1), bb_im.transpose(0, 2, 1)
    lag_k = jnp.sum(w_re[:, :, None] * bt_re[None, :, :, None] - w_im[:, :, None] * bt_im[None, :, :, None], axis=-1)
    q_re, q_im = p_re[:L][::-1], p_im[:L][::-1]
    bc_re = q_re[:, :, None, :] * bt_re[None] - q_im[:, :, None, :] * bt_im[None]
    bc_im = q_re[:, :, None, :] * bt_im[None] + q_im[:, :, None, :] * bt_re[None]
    o_re = cr[None] * p_re[1:, :, None, :] - ci[None] * p_im[1:, :, None, :]
    o_im = cr[None] * p_im[1:, :, None, :] + ci[None] * p_re[1:, :, None, :]
    octs = lambda t: t.reshape(L, n_oct, GROUPS_PER_TILE, C, -1)
    to_lag = lambda t: octs(t).transpose(1, 0, 3, 2, 4).reshape(n_oct, L, C, LANES)
    to_rows = lambda t: octs(t).transpose(1, 0, 3, 2, 4).reshape(n_oct, L * C, GROUPS_PER_TILE * N)
    to_cols = lambda t: octs(t).transpose(1, 2, 4, 0, 3).reshape(n_oct, GROUPS_PER_TILE * N, L * C)
    oct_vec = lambda t: t.reshape(n_oct, 1, -1)
    return dict(
        lag=to_lag(lag_k).astype(BF16),
        bc_re=to_rows(bc_re).astype(BF16), bc_im=to_rows(bc_im).astype(BF16),
        oc_re=to_cols(o_re).astype(BF16), oc_im_neg=to_cols(-o_im).astype(BF16),
        al_re=oct_vec(p_re[L]), al_im=oct_vec(p_im[L]), d=oct_vec(d_skip.astype(F32)),
    )


def _ssm_expanders():
    L, C, N, G8 = SSM_CHUNK, SSM_GROUP, SSM_STATE, GROUPS_PER_TILE
    wide = np.arange(L * LANES)
    w_step, w_group, w_chan = wide // LANES, (wide // C) % G8, wide % C
    small = np.arange(L * C)
    s_step, s_chan = small // C, small % C
    state_group = np.arange(G8 * N) // N
    spread = (w_step[:, None] == s_step[None, :]) & (w_chan[:, None] == s_chan[None, :])
    lane = np.arange(LANES)
    as_bf16 = lambda a: jnp.asarray(a.astype(np.float32), dtype=BF16)
    return dict(
        spread_rows=as_bf16(spread), spread_cols=as_bf16(spread.T),
        keep_rows=jnp.asarray((w_group[:, None] == state_group[None, :]).astype(np.float32)),
        keep_cols=jnp.asarray((state_group[:, None] == w_group[None, :]).astype(np.float32)),
        spread_lag=as_bf16((lane % C)[:, None] == np.arange(C)[None, :]),
        keep_lag=jnp.asarray(((lane // C)[:, None] == (lane // C)[None, :]).astype(np.float32)),
    )


def _ssm_kernel(u_ref, lag_ref, bxr_ref, bxi_ref, cxr_ref, cxi_ref, alr_ref, ali_ref, d_ref,
                sr_ref, sc_ref, kr_ref, kc_ref, sl_ref, kl_ref, o_ref,
                ub_ref, bcr_ref, bci_ref, ocr_ref, oci_ref, zr_ref, zi_ref, hr_ref, hi_ref, *, bsz):
    L = u_ref.shape[0]
    rows = u_ref.shape[1]
    chunks = rows // bsz
    tile = 2 * LANES
    for r in range(L):
        ub_ref[:, r * LANES:(r + 1) * LANES] = u_ref[r].astype(BF16)
    bcr_ref[...] = (_dot(sr_ref[...], bxr_ref[...]) * kr_ref[...]).astype(BF16)
    bci_ref[...] = (_dot(sr_ref[...], bxi_ref[...]) * kr_ref[...]).astype(BF16)
    zr_ref[...] = _dot(ub_ref[...], bcr_ref[...])
    zi_ref[...] = _dot(ub_ref[...], bci_ref[...])
    a_re, a_im = alr_ref[...], ali_ref[...]

    def step(k, carry):
        new = []
        for b in range(bsz):
            h_re, h_im = carry[2 * b], carry[2 * b + 1]
            row = pl.ds(b * chunks + k, 1)
            hr_ref[row, :] = h_re
            hi_ref[row, :] = h_im
            new.append(a_re * h_re - a_im * h_im + zr_ref[row, :])
            new.append(a_re * h_im + a_im * h_re + zi_ref[row, :])
        return tuple(new)

    zero = jnp.zeros((1, a_re.shape[1]), F32)
    lax.fori_loop(0, chunks, step, (zero,) * (2 * bsz))

    ocr_ref[...] = (_dot(cxr_ref[...], sc_ref[...]) * kc_ref[...]).astype(BF16)
    oci_ref[...] = (_dot(cxi_ref[...], sc_ref[...]) * kc_ref[...]).astype(BF16)
    lag = [(_dot(sl_ref[...], lag_ref[d]) * kl_ref[...]).astype(BF16) for d in range(L)]
    lag_tile = []
    for dd in range(L // 2):
        below = lag[2 * dd - 1] if dd > 0 else jnp.zeros((LANES, LANES), BF16)
        lag_tile.append(jnp.concatenate([jnp.concatenate([lag[2 * dd], lag[2 * dd + 1]], axis=1),
                                         jnp.concatenate([below, lag[2 * dd]], axis=1)], axis=0))
    hb_re, hb_im = hr_ref[...].astype(BF16), hi_ref[...].astype(BF16)
    for t2 in range(L // 2):
        cols = slice(t2 * tile, (t2 + 1) * tile)
        y = _dot(hb_re, ocr_ref[:, cols]) + _dot(hb_im, oci_ref[:, cols])
        for r2 in range(t2 + 1):
            y = y + _dot(ub_ref[:, r2 * tile:(r2 + 1) * tile], lag_tile[t2 - r2])
        for half in range(2):
            t = 2 * t2 + half
            o_ref[t] = jax.nn.gelu(y[:, half * LANES:(half + 1) * LANES] + d_ref[...] * u_ref[t]).astype(o_ref.dtype)


def _ssm_core(u, ops, bsz, seq):
    m, width = u.shape
    L, C = SSM_CHUNK, SSM_GROUP
    n_oct = width // LANES
    chunks = seq // L
    rows = bsz * chunks
    ow = L * LANES
    uc = u.reshape(bsz, chunks, L, n_oct, LANES).transpose(3, 2, 0, 1, 4).reshape(n_oct, L, rows, LANES)
    n_state = GROUPS_PER_TILE * SSM_STATE
    ex = _ssm_expanders()
    per_oct = lambda *shape: pl.BlockSpec((None,) + shape, lambda p: (p,) + (0,) * len(shape))
    const = lambda a: pl.BlockSpec(a.shape, lambda p: (0,) * a.ndim)
    consts = [ex['spread_rows'], ex['spread_cols'], ex['keep_rows'], ex['keep_cols'], ex['spread_lag'], ex['keep_lag']]
    yc = pl.pallas_call(
        functools.partial(_ssm_kernel, bsz=bsz),
        grid=(n_oct,),
        in_specs=[
            per_oct(L, rows, LANES), per_oct(L, C, LANES), per_oct(L * C, n_state), per_oct(L * C, n_state),
            per_oct(n_state, L * C), per_oct(n_state, L * C), per_oct(1, n_state), per_oct(1, n_state), per_oct(1, LANES),
        ] + [const(a) for a in consts],
        out_specs=per_oct(L, rows, LANES),
        out_shape=jax.ShapeDtypeStruct((n_oct, L, rows, LANES), BF16),
        scratch_shapes=[pltpu.VMEM((rows, ow), BF16)] + [pltpu.VMEM((ow, n_state), BF16)] * 2
        + [pltpu.VMEM((n_state, ow), BF16)] * 2 + [pltpu.VMEM((rows, n_state), F32)] * 4,
        compiler_params=_params(1),
        name="ssm_core",
    )(uc, ops['lag'], ops['bc_re'], ops['bc_im'], ops['oc_re'], ops['oc_im_neg'], ops['al_re'], ops['al_im'], ops['d'], *consts)
    return yc.reshape(n_oct, L, bsz, chunks, LANES).transpose(2, 3, 1, 0, 4).reshape(m, width)


def _glu_kernel(y_ref, wa_ref, wb_ref, ba_ref, bb_ref, gate_ref, o_ref):
    y = y_ref[...]
    ga = _dot(y, wa_ref[...]) + ba_ref[...]
    gb = _dot(y, wb_ref[...]) + bb_ref[...]
    o_ref[...] = ((ga * jax.nn.sigmoid(gb)) * _silu(gate_ref[...])).astype(o_ref.dtype)


def _glu(y, w_glu, b_glu, proj, tm=512, tn=512):
    m, k = y.shape
    width = w_glu.shape[1] // 2
    tm, tn = min(tm, m), min(tn, width)
    nt = width // tn
    w_tiles = _tile_cols(w_glu, tn)
    return pl.pallas_call(
        _glu_kernel,
        grid=(m // tm, nt),
        in_specs=[
            pl.BlockSpec((tm, k), lambda i, j: (i, 0)),
            pl.BlockSpec((None, k, tn), lambda i, j: (j, 0, 0)),
            pl.BlockSpec((None, k, tn), lambda i, j: (nt + j, 0, 0)),
            pl.BlockSpec((1, tn), lambda i, j: (0, j)),
            pl.BlockSpec((1, tn), lambda i, j: (0, nt + j)),
            pl.BlockSpec((tm, tn), lambda i, j: (i, nt + j)),
        ],
        out_specs=pl.BlockSpec((tm, tn), lambda i, j: (i, j)),
        out_shape=jax.ShapeDtypeStruct((m, width), BF16),
        compiler_params=_params(2),
        name="ssm_glu",
    )(y, w_tiles, w_tiles, b_glu, b_glu, proj)


def _ssm_mixer(x, g, w_in, lam_re, lam_im, log_dt, b_re, b_im, c_re, c_im, d_skip, w_glu, b_glu, bsz, seq):
    width = w_in.shape[1] // 2
    proj = _norm_matmul(x, g, w_in, F32)
    ops = _ssm_operators(lam_re, lam_im, log_dt, b_re, b_im, c_re, c_im, d_skip)
    y = _ssm_core(proj[:, :width], ops, bsz, seq)
    return _glu(y, w_glu, b_glu.astype(F32).reshape(1, -1), proj)


CUM_BLOCK = 128


def _split3(x):
    x1 = x.astype(BF16)
    r1 = x - x1.astype(F32)
    x2 = r1.astype(BF16)
    x3 = (r1 - x2.astype(F32)).astype(BF16)
    return x1, x2, x3


def _forget_cumsum_kernel(z_ref, b_ref, o_ref):
    n_blocks = z_ref.shape[0] // CUM_BLOCK
    ri = lax.broadcasted_iota(jnp.int32, (CUM_BLOCK, CUM_BLOCK), 0)
    ci = lax.broadcasted_iota(jnp.int32, (CUM_BLOCK, CUM_BLOCK), 1)
    tri = jnp.where(ci <= ri, 1.0, 0.0).astype(BF16)

    def body(i, carry):
        rows = pl.ds(pl.multiple_of(i * CUM_BLOCK, CUM_BLOCK), CUM_BLOCK)
        z = z_ref[rows, :] + b_ref[...]
        log_f = jnp.minimum(z, 0.0) - jnp.log1p(jnp.exp(-jnp.abs(z)))
        x1, x2, x3 = _split3(log_f)
        c = _dot(tri, x1) + _dot(tri, x2) + _dot(tri, x3) + carry
        o_ref[rows, :] = c
        return c[CUM_BLOCK - 1:, :]

    lax.fori_loop(0, n_blocks, body, jnp.zeros((1, z_ref.shape[1]), F32))


def _forget_cumsum(z, b, bsz, seq):
    lanes = z.shape[1]
    return pl.pallas_call(
        _forget_cumsum_kernel,
        grid=(bsz,),
        in_specs=[pl.BlockSpec((seq, lanes), lambda i: (i, 0)), pl.BlockSpec((1, lanes), lambda i: (0, 0))],
        out_specs=pl.BlockSpec((seq, lanes), lambda i: (i, 0)),
        out_shape=jax.ShapeDtypeStruct(z.shape, F32),
        compiler_params=_params(1),
        name="forget_cumsum",
    )(z, b)


def _fox_kernel(q_ref, k_ref, v_ref, sg_ref, cq_ref, ck_ref, o_ref, ka_ref, vt_ref, qa_ref, s_ref, p_ref, acc_ref, pv_ref, *, tk):
    hp, qi = pl.program_id(1), pl.program_id(2)
    tq = q_ref.shape[0]
    lane = lax.broadcasted_iota(jnp.int32, (1, LANES), 1)
    free = (HEAD_DIM, 0)
    n_pieces = 3

    def bias_lanes(c, hh, c_offset, one_offset):
        rr = lax.broadcasted_iota(jnp.int32, (n_pieces * LANES, LANES), 0)
        cc = lax.broadcasted_iota(jnp.int32, (n_pieces * LANES, LANES), 1)
        place = jnp.logical_and(rr % LANES == 2 * hp + hh, cc == free[hh] + c_offset + rr // LANES)
        placed = _dot(jnp.concatenate(_split3(c), axis=1), jnp.where(place, 1.0, 0.0).astype(BF16))
        first = free[hh] + one_offset
        return (placed + jnp.where(jnp.logical_and(lane >= first, lane < first + n_pieces), 1.0, 0.0)).astype(BF16)

    @pl.when(qi == 0)
    def _():
        def chunk(j, carry):
            rows = pl.ds(pl.multiple_of(j * tk, tk), tk)
            keys = k_ref[rows, :]
            neg_c = -ck_ref[rows, :]
            for hh in range(2):
                ka_ref[hh, j] = jnp.where((lane // HEAD_DIM) == hh, keys, bias_lanes(neg_c, hh, 0, n_pieces))
            vt_ref[j] = v_ref[rows, :].astype(F32).T.astype(BF16)
            return carry
        lax.fori_loop(0, k_ref.shape[0] // tk, chunk, 0)

    cq = cq_ref[...]
    for hh in range(2):
        qa_ref[hh] = jnp.where((lane // HEAD_DIM) == hh, q_ref[...], bias_lanes(cq, hh, n_pieces, 0))

    def scores(hh, j):
        s_ref[hh] = _dot_nt(ka_ref[hh, j], qa_ref[hh])

    rows8 = 8
    rows16 = 16

    def absorb(hh, j, m, l, diagonal):
        def strip(r, n):
            blk = s_ref[hh, r:r + n, :]
            if diagonal:
                kpos = lax.broadcasted_iota(jnp.int32, (n, tq), 0)
                qpos = lax.broadcasted_iota(jnp.int32, (n, tq), 1)
                blk = jnp.where(kpos + r <= qpos, blk, NEG)
            return blk
        top = strip(0, rows8)
        for r in range(rows8, tk, rows8):
            top = jnp.maximum(top, strip(r, rows8))
        m_new = jnp.maximum(m, top.max(axis=0, keepdims=True))
        alpha = jnp.exp(m - m_new)
        m_rows = jnp.broadcast_to(m_new, (rows16, tq))
        total = None
        for r in range(0, tk, rows16):
            p = jnp.exp(strip(r, rows16) - m_rows)
            total = p if total is None else total + p
            p_ref[hh, r:r + rows16, :] = p.astype(BF16)
        acc_ref[hh] = alpha * (acc_ref[hh] + pv_ref[hh])
        pv_ref[hh] = _dot(vt_ref[j], p_ref[hh])
        return m_new, alpha * l + total.sum(axis=0, keepdims=True)

    def step(j, carry, diagonal):
        m0, l0, m1, l1 = carry
        scores(1, j)
        m0, l0 = absorb(0, j, m0, l0, diagonal)
        if not diagonal:
            scores(0, j + 1)
        m1, l1 = absorb(1, j, m1, l1, diagonal)
        return m0, l0, m1, l1

    acc_ref[...] = jnp.zeros(acc_ref.shape, F32)
    pv_ref[...] = jnp.zeros(pv_ref.shape, F32)
    scores(0, 0)
    init = (jnp.full((1, tq), NEG, F32), jnp.zeros((1, tq), F32)) * 2
    carry = lax.fori_loop(0, qi, lambda j, c: step(j, c, False), init)
    _, l0, _, l1 = step(qi, carry, True)
    row = lax.broadcasted_iota(jnp.int32, (LANES, 1), 0)
    out = jnp.where(row < HEAD_DIM, (acc_ref[0] + pv_ref[0]) / l0, (acc_ref[1] + pv_ref[1]) / l1)
    o_ref[...] = (out.T * sg_ref[...].astype(F32)).astype(o_ref.dtype)


def _fox_attention(proj, csum, bsz, seq, t=512):
    width = FOX_HEADS * HEAD_DIM
    n_pairs = width // LANES
    t = min(t, seq)
    nq = seq // t
    return pl.pallas_call(
        functools.partial(_fox_kernel, tk=t),
        grid=(bsz, n_pairs, nq),
        in_specs=[
            pl.BlockSpec((t, LANES), lambda b, h, i: (b * nq + i, h)),
            pl.BlockSpec((seq, LANES), lambda b, h, i: (b, n_pairs + h)),
            pl.BlockSpec((seq, LANES), lambda b, h, i: (b, 2 * n_pairs + h)),
            pl.BlockSpec((t, LANES), lambda b, h, i: (b * nq + i, 3 * n_pairs + h)),
            pl.BlockSpec((t, LANES), lambda b, h, i: (b * nq + i, 0)),
            pl.BlockSpec((seq, LANES), lambda b, h, i: (b, 0)),
        ],
        out_specs=pl.BlockSpec((t, LANES), lambda b, h, i: (b * nq + i, h)),
        out_shape=jax.ShapeDtypeStruct((bsz * seq, width), BF16),
        scratch_shapes=[
            pltpu.VMEM((2, nq, t, LANES), BF16), pltpu.VMEM((nq, LANES, t), BF16), pltpu.VMEM((2, t, LANES), BF16),
            pltpu.VMEM((2, t, t), F32), pltpu.VMEM((2, t, t), BF16),
            pltpu.VMEM((2, LANES, t), F32), pltpu.VMEM((2, LANES, t), F32),
        ],
        compiler_params=_params(3),
        name="fox_attention",
    )(proj, proj, proj, proj, csum, csum)


def _fox_mixer(x, g, w_in, w_fg, b_fg, bsz, seq, t=512):
    width = FOX_HEADS * HEAD_DIM
    w = jnp.concatenate([w_in[:, :width] * (HEAD_DIM ** -0.5), w_in[:, width:]], axis=1)
    proj = _norm_matmul(x, g, w, BF16, silu_cols=(3 * width, 4 * width))
    pad = LANES - FOX_HEADS
    z = _norm_matmul(x, g, jnp.pad(w_fg, ((0, 0), (0, pad))), F32)
    csum = _forget_cumsum(z, jnp.pad(b_fg.astype(F32), (0, pad)).reshape(1, LANES), bsz, seq)
    return _fox_attention(proj, csum, bsz, seq, t)


def kernel(x, p, norm_g, final_g, rel_bias, swa_w_in, swa_w_out, swa_sinks, conv_w_in, conv_kernel, conv_w_out, ssm_w_in, ssm_lam_re, ssm_lam_im, ssm_log_dt, ssm_b_re, ssm_b_im, ssm_c_re, ssm_c_im, ssm_d, ssm_w_glu, ssm_b_glu, ssm_w_out, fox_w_in, fox_w_fg, fox_b_fg, fox_w_out, ple_proj, ple_norm, ple_gate):
    bsz, seq, d_model = x.shape
    depth = p.shape[0]
    h = x.astype(F32).reshape(bsz * seq, d_model)
    for i in range(depth):
        mixer, j = i % N_MIXERS, i // N_MIXERS
        if mixer == 0:
            a, w_out = _swa_mixer(h, norm_g[i], swa_w_in[j], swa_sinks[j], rel_bias, bsz, seq), swa_w_out[j]
        elif mixer == 1:
            a, w_out = _conv_mixer(h, norm_g[i], conv_w_in[j], conv_kernel[j], seq), conv_w_out[j]
        elif mixer == 2:
            a = _ssm_mixer(h, norm_g[i], ssm_w_in[j], ssm_lam_re[j], ssm_lam_im[j], ssm_log_dt[j], ssm_b_re[j], ssm_b_im[j],
                           ssm_c_re[j], ssm_c_im[j], ssm_d[j], ssm_w_glu[j], ssm_b_glu[j], bsz, seq)
            w_out = ssm_w_out[j]
        else:
            a, w_out = _fox_mixer(h, norm_g[i], fox_w_in[j], fox_w_fg[j], fox_b_fg[j], bsz, seq), fox_w_out[j]
        h = _out_ple(a, w_out, h, ple_norm[i], ple_gate[i], p[i].reshape(bsz * seq, -1), ple_proj[i],
                     final_g=final_g if i == depth - 1 else None)
    return h.reshape(bsz, seq, d_model).astype(x.dtype)
```

```python
import functools
import math

import numpy as np
import jax
import jax.numpy as jnp
from jax import lax
from jax.experimental import pallas as pl
from jax.experimental.pallas import tpu as pltpu

F32 = jnp.float32
BF16 = jnp.bfloat16

EPS = 1e-6
N_MIXERS = 4
PLE_DIM = 256

SWA_HEADS = 32
SWA_KV_HEADS = 4
SWA_GROUP = SWA_HEADS // SWA_KV_HEADS
HEAD_DIM = 64
SWA_BLOCK = 128
WINDOW = 128
REL_BUCKETS = 32
REL_MAX_DIST = 128

CONV_TAPS = 3

SSM_GROUP = 16
SSM_STATE = 64
SSM_CHUNK = 16

FOX_HEADS = 32

LANES = 128
GROUPS_PER_TILE = LANES // SSM_GROUP
VMEM_LIMIT = 48 * 1024 * 1024

NEG = float(jnp.finfo(jnp.float32).min)
LOG2E = math.log2(math.e)


def _params(n_axes):
    return pltpu.CompilerParams(dimension_semantics=("arbitrary",) * n_axes, vmem_limit_bytes=VMEM_LIMIT)


def _dot(a, b):
    return jnp.dot(a, b, preferred_element_type=F32)


def _dot_nt(a, b):
    return lax.dot_general(a, b, (((1,), (1,)), ((), ())), preferred_element_type=F32)


def _rmsnorm_rows(x, g):
    return x * lax.rsqrt(jnp.mean(x * x, axis=-1, keepdims=True) + EPS) * g


def _silu(x):
    return x * jax.nn.sigmoid(x)


def _tile_cols(w, tn):
    k, n = w.shape
    return w.astype(BF16).reshape(k, n // tn, tn).transpose(1, 0, 2)


ROW_CHUNK = 64


def _for_row_chunks(n_rows, fn):
    def body(c, carry):
        fn(pl.ds(pl.multiple_of(c * ROW_CHUNK, ROW_CHUNK), ROW_CHUNK))
        return carry
    lax.fori_loop(0, n_rows // ROW_CHUNK, body, 0)


def _norm_matmul_kernel(x_ref, g_ref, w_ref, o_ref, a_ref, *, silu_tiles):
    j = pl.program_id(1)

    @pl.when(j == 0)
    def _():
        def chunk(rows):
            a_ref[rows, :] = _rmsnorm_rows(x_ref[rows, :], g_ref[...]).astype(BF16)
        _for_row_chunks(a_ref.shape[0], chunk)

    acc = _dot(a_ref[...], w_ref[...])
    if silu_tiles is None:
        o_ref[...] = acc.astype(o_ref.dtype)
    else:
        gated = jnp.logical_and(j >= silu_tiles[0], j < silu_tiles[1])

        @pl.when(gated)
        def _():
            o_ref[...] = _silu(acc).astype(o_ref.dtype)

        @pl.when(jnp.logical_not(gated))
        def _():
            o_ref[...] = acc.astype(o_ref.dtype)


def _norm_matmul(x, g, w, out_dtype, silu_cols=None, tm=1024, tn=512):
    m, k = x.shape
    n = w.shape[1]
    tm, tn = min(tm, m), min(tn, n)
    silu_tiles = None if silu_cols is None else (silu_cols[0] // tn, silu_cols[1] // tn)
    return pl.pallas_call(
        functools.partial(_norm_matmul_kernel, silu_tiles=silu_tiles),
        grid=(m // tm, n // tn),
        in_specs=[
            pl.BlockSpec((tm, k), lambda i, j: (i, 0)),
            pl.BlockSpec((1, k), lambda i, j: (0, 0)),
            pl.BlockSpec((None, k, tn), lambda i, j: (j, 0, 0)),
        ],
        out_specs=pl.BlockSpec((tm, tn), lambda i, j: (i, j)),
        out_shape=jax.ShapeDtypeStruct((m, n), out_dtype),
        scratch_shapes=[pltpu.VMEM((tm, k), BF16)],
        compiler_params=_params(2),
        name="norm_matmul",
    )(x, g.reshape(1, k), _tile_cols(w, tn))


def _out_ple_kernel(a_ref, wo_ref, x_ref, g_ref, wg_ref, p_ref, wp_ref, gf_ref, o_ref, x1_ref, hn_ref, ss_ref, *, nt, final):
    j = pl.program_id(1)
    tn = x_ref.shape[1]
    n = nt * tn

    @pl.when(j == 0)
    def _():
        ss_ref[...] = jnp.zeros(ss_ref.shape, F32)

    @pl.when(j < nt)
    def _():
        x1 = x_ref[...] + _dot(a_ref[...], wo_ref[j])
        x1_ref[j] = x1
        hn_ref[j] = (x1 * g_ref[j]).astype(BF16)
        ss_ref[0] += jnp.sum(x1 * x1, axis=-1, keepdims=True)

    @pl.when(j >= nt)
    def _():
        t = j - nt
        emb = _dot(p_ref[...].astype(BF16), wp_ref[t])
        acc = _dot(hn_ref[0], wg_ref[t, :tn, :])
        for kt in range(1, nt):
            acc = acc + _dot(hn_ref[kt], wg_ref[t, kt * tn:(kt + 1) * tn, :])
        x2 = x1_ref[t] + emb * jax.nn.sigmoid(acc * lax.rsqrt(ss_ref[0] / n + EPS))
        if final:
            x1_ref[t] = x2
            ss_ref[1] += jnp.sum(x2 * x2, axis=-1, keepdims=True)
        else:
            o_ref[...] = x2

    if final:
        @pl.when(j == 2 * nt - 1)
        def _():
            inv = lax.rsqrt(ss_ref[1] / n + EPS)
            for t in range(nt):
                o_ref[:, t * tn:(t + 1) * tn] = x1_ref[t] * inv * gf_ref[t]


def _out_ple(a, w_out, x, g, w_gate, p, w_proj, final_g=None, tm=512, tn=512):
    m, k = a.shape
    n = w_out.shape[1]
    pd = p.shape[1]
    tm, tn = min(tm, m), min(tn, n)
    nt = n // tn
    final = final_g is not None
    resident = lambda rows: pl.BlockSpec((nt, rows, tn), lambda i, j: (0, 0, 0), pipeline_mode=pl.Buffered(1))
    row_vec = pl.BlockSpec((nt, 1, tn), lambda i, j: (0, 0, 0))
    if final:
        out_spec = pl.BlockSpec((tm, n), lambda i, j: (i, 0))
    else:
        out_spec = pl.BlockSpec((tm, tn), lambda i, j: (i, jnp.maximum(j - nt, 0)))
    return pl.pallas_call(
        functools.partial(_out_ple_kernel, nt=nt, final=final),
        grid=(m // tm, 2 * nt),
        in_specs=[
            pl.BlockSpec((tm, k), lambda i, j: (i, 0)),
            resident(k),
            pl.BlockSpec((tm, tn), lambda i, j: (i, jnp.minimum(j, nt - 1))),
            row_vec,
            resident(n),
            pl.BlockSpec((tm, pd), lambda i, j: (i, 0)),
            resident(pd),
            row_vec,
        ],
        out_specs=out_spec,
        out_shape=jax.ShapeDtypeStruct((m, n), F32),
        scratch_shapes=[pltpu.VMEM((nt, tm, tn), F32), pltpu.VMEM((nt, tm, tn), BF16), pltpu.VMEM((2, tm, 1), F32)],
        compiler_params=_params(2),
        name="out_ple_final" if final else "out_ple",
    )(a, _tile_cols(w_out, tn), x, g.reshape(nt, 1, tn), _tile_cols(w_gate, tn), p, _tile_cols(w_proj, tn),
      (final_g if final else g).reshape(nt, 1, tn))


def _t5_bucket(dist):
    max_exact = REL_BUCKETS // 2
    d = np.maximum(dist, 1).astype(np.float32)
    large = max_exact + (np.log(d / max_exact) / np.log(REL_MAX_DIST / max_exact) * (REL_BUCKETS - max_exact)).astype(np.int32)
    large = np.minimum(large, REL_BUCKETS - 1)
    return np.where(dist < max_exact, dist, large).astype(np.int32)


def _swa_bucket_table():
    qi = np.arange(SWA_BLOCK)[None, :]
    kj = np.arange(2 * SWA_BLOCK)[:, None]
    return _t5_bucket(np.clip(qi + SWA_BLOCK - kj, 0, None))


def _swa_kernel(sink_ref, rel_ref, bucket_ref, q_ref, sg_ref, kp_ref, kc_ref, vp_ref, vc_ref, o_ref, bias_ref, qs_ref):
    n = pl.program_id(1)
    blk = SWA_BLOCK
    lane = lax.broadcasted_iota(jnp.int32, (1, LANES), 1)

    @pl.when(jnp.logical_and(pl.program_id(0) == 0, n == 0))
    def _():
        bucket = bucket_ref[...]
        kj = lax.broadcasted_iota(jnp.int32, (2 * blk, blk), 0)
        qi = lax.broadcasted_iota(jnp.int32, (2 * blk, blk), 1)
        band = jnp.logical_or(jnp.logical_and(kj < blk, kj > qi), jnp.logical_and(kj >= blk, kj - blk <= qi))

        def per_head(head, carry):
            acc = jnp.zeros((2 * blk, blk), F32)
            for b in range(REL_BUCKETS):
                acc = jnp.where(bucket == b, rel_ref[b, head], acc)
            bias_ref[head] = jnp.where(band, acc, NEG)
            return carry
        lax.fori_loop(0, SWA_HEADS, per_head, 0)

    for kvh in range(SWA_KV_HEADS):
        cols = slice(kvh * LANES, (kvh + 1) * LANES)
        heads = range(kvh * SWA_GROUP, (kvh + 1) * SWA_GROUP)
        for g, head in enumerate(heads):
            q_pair = q_ref[:, head // 2 * LANES:(head // 2 + 1) * LANES]
            qs_ref[g * blk:(g + 1) * blk, :] = jnp.where((lane // HEAD_DIM) == head % 2, q_pair, jnp.zeros_like(q_pair))
        keys = jnp.concatenate([kp_ref[:, cols], kc_ref[:, cols]], axis=0)
        vals = jnp.concatenate([vp_ref[:, cols], vc_ref[:, cols]], axis=0)
        s = _dot_nt(keys, qs_ref[...]) + jnp.concatenate([bias_ref[head] for head in heads], axis=1)
        s_prev = jnp.where(n > 0, s[:blk], NEG)
        s_cur = s[blk:]
        sink = jnp.concatenate([jnp.full((1, blk), sink_ref[head], F32) for head in heads], axis=1)
        m = jnp.maximum(jnp.maximum(s_prev, s_cur).max(axis=0, keepdims=True), sink)
        e_prev, e_cur = jnp.exp(s_prev - m), jnp.exp(s_cur - m)
        den = e_prev.sum(axis=0, keepdims=True) + e_cur.sum(axis=0, keepdims=True) + jnp.exp(sink - m)
        e = jnp.concatenate([e_prev, e_cur], axis=0).astype(BF16)
        out = _dot(vals.astype(F32).T.astype(BF16), e) / den
        for pair in range(SWA_GROUP // 2):
            head = kvh * SWA_GROUP + 2 * pair
            even = out[:HEAD_DIM, 2 * pair * blk:(2 * pair + 1) * blk]
            odd = out[HEAD_DIM:, (2 * pair + 1) * blk:(2 * pair + 2) * blk]
            cols = slice(head // 2 * LANES, (head // 2 + 1) * LANES)
            o_ref[:, cols] = (jnp.concatenate([even, odd], axis=0).T * sg_ref[:, cols].astype(F32)).astype(o_ref.dtype)


def _swa_attention(proj, sinks, rel_bias, bsz, seq):
    nb = seq // SWA_BLOCK
    width = SWA_HEADS * HEAD_DIM
    kvw = SWA_KV_HEADS * LANES
    k_blk = 2 * width // kvw
    cur = lambda b, n: b * nb + n
    prev = lambda b, n: b * nb + jnp.maximum(n - 1, 0)
    smem = pl.BlockSpec(memory_space=pltpu.SMEM)
    return pl.pallas_call(
        _swa_kernel,
        grid=(bsz, nb),
        in_specs=[
            smem, smem,
            pl.BlockSpec((2 * SWA_BLOCK, SWA_BLOCK), lambda b, n: (0, 0)),
            pl.BlockSpec((SWA_BLOCK, width), lambda b, n: (cur(b, n), 0)),
            pl.BlockSpec((SWA_BLOCK, width), lambda b, n: (cur(b, n), 1)),
            pl.BlockSpec((SWA_BLOCK, kvw), lambda b, n: (prev(b, n), k_blk)),
            pl.BlockSpec((SWA_BLOCK, kvw), lambda b, n: (cur(b, n), k_blk)),
            pl.BlockSpec((SWA_BLOCK, kvw), lambda b, n: (prev(b, n), k_blk + 1)),
            pl.BlockSpec((SWA_BLOCK, kvw), lambda b, n: (cur(b, n), k_blk + 1)),
        ],
        out_specs=pl.BlockSpec((SWA_BLOCK, width), lambda b, n: (cur(b, n), 0)),
        out_shape=jax.ShapeDtypeStruct((bsz * seq, width), BF16),
        scratch_shapes=[
            pltpu.VMEM((SWA_HEADS, 2 * SWA_BLOCK, SWA_BLOCK), F32),
            pltpu.VMEM((SWA_GROUP * SWA_BLOCK, LANES), BF16),
        ],
        compiler_params=_params(2),
        name="swa_attention",
    )(sinks.astype(F32), rel_bias.astype(F32), jnp.asarray(_swa_bucket_table()), proj, proj, proj, proj, proj, proj)


def _dup_heads(w, n_heads):
    k = w.shape[0]
    w = w.reshape(k, n_heads, 1, HEAD_DIM)
    return jnp.broadcast_to(w, (k, n_heads, 2, HEAD_DIM)).reshape(k, n_heads * 2 * HEAD_DIM)


def _swa_mixer(x, g, w_in, sinks, rel_bias, bsz, seq):
    width = SWA_HEADS * HEAD_DIM
    kvw = SWA_KV_HEADS * HEAD_DIM
    w_q, w_k, w_v, w_g = (w_in[:, :width], w_in[:, width:width + kvw],
                          w_in[:, width + kvw:width + 2 * kvw], w_in[:, width + 2 * kvw:])
    w = jnp.concatenate([w_q * (HEAD_DIM ** -0.5), w_g, _dup_heads(w_k, SWA_KV_HEADS), _dup_heads(w_v, SWA_KV_HEADS)], axis=1)
    proj = _norm_matmul(x, g, w, BF16, silu_cols=(width, 2 * width))
    return _swa_attention(proj, sinks, rel_bias, bsz, seq)


HALO = 8


def _conv_kernel(x_ref, g_ref, wb_ref, wc_ref, wu_ref, wg_ref, ck_ref, o_ref, a_ref, z_ref, carry_ref, *, tiles_per_seq):
    i, j = pl.program_id(0), pl.program_id(1)
    tm = a_ref.shape[0]

    @pl.when(j == 0)
    def _():
        def chunk(rows):
            a_ref[rows, :] = _rmsnorm_rows(x_ref[rows, :], g_ref[...]).astype(BF16)
        _for_row_chunks(tm, chunk)

    a = a_ref[...]
    z = _dot(a, wc_ref[...]) * _dot(a, wu_ref[...])
    first = (i % tiles_per_seq) == 0

    @pl.when(first)
    def _():
        z_ref[:HALO, :] = jnp.zeros((HALO, z_ref.shape[1]), F32)

    @pl.when(jnp.logical_not(first))
    def _():
        z_ref[:HALO, :] = carry_ref[j]

    z_ref[HALO:, :] = z
    carry_ref[j] = z[tm - HALO:, :]
    conv = z_ref[HALO - 2:HALO - 2 + tm, :] * ck_ref[0:1, :]
    conv = conv + z_ref[HALO - 1:HALO - 1 + tm, :] * ck_ref[1:2, :]
    conv = conv + z * ck_ref[2:3, :]
    y = _dot(a, wb_ref[...]) * conv
    o_ref[...] = (y * _silu(_dot(a, wg_ref[...]))).astype(o_ref.dtype)


def _conv_mixer(x, g, w_in, conv_kernel, seq, tm=512, tn=512):
    m, k = x.shape
    width = w_in.shape[1] // 4
    tm, tn = min(tm, seq), min(tn, width)
    nt = width // tn
    w_spec = lambda q: pl.BlockSpec((None, k, tn), lambda i, j: (q * nt + j, 0, 0))
    w_in = _tile_cols(w_in, tn)
    return pl.pallas_call(
        functools.partial(_conv_kernel, tiles_per_seq=seq // tm),
        grid=(m // tm, nt),
        in_specs=[
            pl.BlockSpec((tm, k), lambda i, j: (i, 0)),
            pl.BlockSpec((1, k), lambda i, j: (0, 0)),
            w_spec(0), w_spec(1), w_spec(2), w_spec(3),
            pl.BlockSpec((CONV_TAPS, tn), lambda i, j: (0, j)),
        ],
        out_specs=pl.BlockSpec((tm, tn), lambda i, j: (i, j)),
        out_shape=jax.ShapeDtypeStruct((m, width), BF16),
        scratch_shapes=[
            pltpu.VMEM((tm, k), BF16),
            pltpu.VMEM((HALO + tm, tn), F32),
            pltpu.VMEM((nt, HALO, tn), F32),
        ],
        compiler_params=_params(2),
        name="conv_mixer",
    )(x, g.reshape(1, k), w_in, w_in, w_in, w_in, conv_kernel.astype(F32))


def _ssm_operators(lam_re, lam_im, log_dt, b_re, b_im, c_re, c_im, d_skip):
    n_groups = lam_re.shape[0]
    n_oct = n_groups // GROUPS_PER_TILE
    L, C, N = SSM_CHUNK, SSM_GROUP, SSM_STATE
    dt = jnp.exp(log_dt.astype(F32))[:, None]
    lr, li = lam_re.astype(F32), lam_im.astype(F32)
    mag = jnp.exp(lr * dt)
    ab_re, ab_im = mag * jnp.cos(li * dt), mag * jnp.sin(li * dt)
    den = lr * lr + li * li
    nr = ab_re - 1.0
    coef_re = ((nr * lr + ab_im * li) / den)[..., None]
    coef_im = ((ab_im * lr - nr * li) / den)[..., None]
    br, bi = b_re.astype(F32), b_im.astype(F32)
    bb_re = coef_re * br - coef_im * bi
    bb_im = coef_re * bi + coef_im * br
    cr, ci = c_re.astype(F32), c_im.astype(F32)
    p_re, p_im = [jnp.ones_like(ab_re)], [jnp.zeros_like(ab_im)]
    for _ in range(L):
        p_re, p_im = (p_re + [p_re[-1] * ab_re - p_im[-1] * ab_im], p_im + [p_re[-1] * ab_im + p_im[-1] * ab_re])
    p_re, p_im = jnp.stack(p_re), jnp.stack(p_im)
    w_re = cr[None] * p_re[:L, :, None, :] - ci[None] * p_im[:L, :, None, :]
    w_im = cr[None] * p_im[:L, :, None, :] + ci[None] * p_re[:L, :, None, :]
    bt_re, bt_im = bb_re.transpose(0, 2, 1), bb_im.transpose(0, 2, 1)
    lag_k = jnp.sum(w_re[:, :, None] * bt_re[None, :, :, None] - w_im[:, :, None] * bt_im[None, :, :, None], axis=-1)
    q_re, q_im = p_re[:L][::-1], p_im[:L][::-1]
    bc_re = q_re[:, :, None, :] * bt_re[None] - q_im[:, :, None, :] * bt_im[None]
    bc_im = q_re[:, :, None, :] * bt_im[None] + q_im[:, :, None, :] * bt_re[None]
    o_re = cr[None] * p_re[1:, :, None, :] - ci[None] * p_im[1:, :, None, :]
    o_im = cr[None] * p_im[1:, :, None, :] + ci[None] * p_re[1:, :, None, :]
    octs = lambda t: t.reshape(L, n_oct, GROUPS_PER_TILE, C, -1)
    to_lag = lambda t: octs(t).transpose(1, 0, 3, 2, 4).reshape(n_oct, L, C, LANES)
    to_rows = lambda t: octs(t).transpose(1, 0, 3, 2, 4).reshape(n_oct, L * C, GROUPS_PER_TILE * N)
    to_cols = lambda t: octs(t).transpose(1, 2, 4, 0, 3).reshape(n_oct, GROUPS_PER_TILE * N, L * C)
    oct_vec = lambda t: t.reshape(n_oct, 1, -1)
    return dict(
        lag=to_lag(lag_k).astype(BF16),
        bc_re=to_rows(bc_re).astype(BF16), bc_im=to_rows(bc_im).astype(BF16),
        oc_re=to_cols(o_re).astype(BF16), oc_im_neg=to_cols(-o_im).astype(BF16),
        al_re=oct_vec(p_re[L]), al_im=oct_vec(p_im[L]), d=oct_vec(d_skip.astype(F32)),
    )


def _ssm_expanders():
    L, C, N, G8 = SSM_CHUNK, SSM_GROUP, SSM_STATE, GROUPS_PER_TILE
    wide = np.arange(L * LANES)
    w_step, w_group, w_chan = wide // LANES, (wide // C) % G8, wide % C
    small = np.arange(L * C)
    s_step, s_chan = small // C, small % C
    state_group = np.arange(G8 * N) // N
    spread = (w_step[:, None] == s_step[None, :]) & (w_chan[:, None] == s_chan[None, :])
    lane = np.arange(LANES)
    as_bf16 = lambda a: jnp.asarray(a.astype(np.float32), dtype=BF16)
    return dict(
        spread_rows=as_bf16(spread), spread_cols=as_bf16(spread.T),
        keep_rows=as_bf16(w_group[:, None] == state_group[None, :]),
        keep_cols=as_bf16(state_group[:, None] == w_group[None, :]),
        spread_lag=as_bf16((lane % C)[:, None] == np.arange(C)[None, :]),
        keep_lag=jnp.asarray(((lane // C)[:, None] == (lane // C)[None, :]).astype(np.float32)),
    )


def _ssm_kernel(u_ref, lag_ref, bxr_ref, bxi_ref, cxr_ref, cxi_ref, alr_ref, ali_ref, d_ref,
                sr_ref, sc_ref, kr_ref, kc_ref, sl_ref, kl_ref, o_ref,
                ub_ref, bcr_ref, bci_ref, ocr_ref, oci_ref, zr_ref, zi_ref, hr_ref, hi_ref, *, bsz):
    L = SSM_CHUNK
    rows = u_ref.shape[0] // L
    chunks = rows // bsz
    tile = 2 * LANES
    step_rows = lambda r: pl.ds(r, rows, stride=L)
    for r in range(L):
        ub_ref[:, r * LANES:(r + 1) * LANES] = u_ref[step_rows(r), :].astype(BF16)
    bcr_ref[...] = (_dot(sr_ref[...], bxr_ref[...]) * kr_ref[...]).astype(BF16)
    bci_ref[...] = (_dot(sr_ref[...], bxi_ref[...]) * kr_ref[...]).astype(BF16)
    zr_ref[...] = _dot(ub_ref[...], bcr_ref[...])
    zi_ref[...] = _dot(ub_ref[...], bci_ref[...])
    a_re, a_im = alr_ref[...], ali_ref[...]

    def step(k, carry):
        new = []
        for b in range(bsz):
            h_re, h_im = carry[2 * b], carry[2 * b + 1]
            row = pl.ds(b * chunks + k, 1)
            hr_ref[row, :] = h_re
            hi_ref[row, :] = h_im
            new.append(a_re * h_re - a_im * h_im + zr_ref[row, :])
            new.append(a_re * h_im + a_im * h_re + zi_ref[row, :])
        return tuple(new)

    zero = jnp.zeros((1, a_re.shape[1]), F32)
    lax.fori_loop(0, chunks, step, (zero,) * (2 * bsz))

    ocr_ref[...] = (_dot(cxr_ref[...], sc_ref[...]) * kc_ref[...]).astype(BF16)
    oci_ref[...] = (_dot(cxi_ref[...], sc_ref[...]) * kc_ref[...]).astype(BF16)
    lag = [(_dot(sl_ref[...], lag_ref[d]) * kl_ref[...]).astype(BF16) for d in range(L)]
    lag_tile = []
    for dd in range(L // 2):
        below = lag[2 * dd - 1] if dd > 0 else jnp.zeros((LANES, LANES), BF16)
        lag_tile.append(jnp.concatenate([jnp.concatenate([lag[2 * dd], lag[2 * dd + 1]], axis=1),
                                         jnp.concatenate([below, lag[2 * dd]], axis=1)], axis=0))
    hb_re, hb_im = hr_ref[...].astype(BF16), hi_ref[...].astype(BF16)
    for t2 in range(L // 2):
        cols = slice(t2 * tile, (t2 + 1) * tile)
        y = _dot(hb_re, ocr_ref[:, cols]) + _dot(hb_im, oci_ref[:, cols])
        for r2 in range(t2 + 1):
            y = y + _dot(ub_ref[:, r2 * tile:(r2 + 1) * tile], lag_tile[t2 - r2])
        for half in range(2):
            t = 2 * t2 + half
            o_ref[step_rows(t), :] = jax.nn.gelu(y[:, half * LANES:(half + 1) * LANES] + d_ref[...] * u_ref[step_rows(t), :])


def _ssm_core(proj, ops, bsz, seq):
    m = proj.shape[0]
    width = proj.shape[1] // 2
    L, C = SSM_CHUNK, SSM_GROUP
    n_oct = width // LANES
    rows = m // L
    ow = L * LANES
    n_state = GROUPS_PER_TILE * SSM_STATE
    ex = _ssm_expanders()
    per_oct = lambda *shape: pl.BlockSpec((None,) + shape, lambda p: (p,) + (0,) * len(shape))
    const = lambda a: pl.BlockSpec(a.shape, lambda p: (0,) * a.ndim, pipeline_mode=pl.Buffered(1))
    consts = [ex['spread_rows'], ex['spread_cols'], ex['keep_rows'], ex['keep_cols'], ex['spread_lag'], ex['keep_lag']]
    return pl.pallas_call(
        functools.partial(_ssm_kernel, bsz=bsz),
        grid=(n_oct,),
        in_specs=[
            pl.BlockSpec((m, LANES), lambda p: (0, p)),
            per_oct(L, C, LANES), per_oct(L * C, n_state), per_oct(L * C, n_state),
            per_oct(n_state, L * C), per_oct(n_state, L * C), per_oct(1, n_state), per_oct(1, n_state), per_oct(1, LANES),
        ] + [const(a) for a in consts],
        out_specs=pl.BlockSpec((m, LANES), lambda p: (0, p)),
        out_shape=jax.ShapeDtypeStruct((m, width), F32),
        scratch_shapes=[pltpu.VMEM((rows, ow), BF16)] + [pltpu.VMEM((ow, n_state), BF16)] * 2
        + [pltpu.VMEM((n_state, ow), BF16)] * 2 + [pltpu.VMEM((rows, n_state), F32)] * 4,
        compiler_params=_params(1),
        name="ssm_core",
    )(proj, ops['lag'], ops['bc_re'], ops['bc_im'], ops['oc_re'], ops['oc_im_neg'], ops['al_re'], ops['al_im'], ops['d'], *consts)


def _glu_kernel(y_ref, wa_ref, wb_ref, ba_ref, bb_ref, gate_ref, o_ref, yb_ref):
    @pl.when(pl.program_id(1) == 0)
    def _():
        def chunk(rows):
            yb_ref[rows, :] = y_ref[rows, :].astype(BF16)
        _for_row_chunks(yb_ref.shape[0], chunk)

    y = yb_ref[...]
    ga = _dot(y, wa_ref[...]) + ba_ref[...]
    gb = _dot(y, wb_ref[...]) + bb_ref[...]
    o_ref[...] = ((ga * jax.nn.sigmoid(gb)) * _silu(gate_ref[...])).astype(o_ref.dtype)


def _glu(y, w_glu, b_glu, proj, tm=512, tn=512):
    m, k = y.shape
    width = w_glu.shape[1] // 2
    tm, tn = min(tm, m), min(tn, width)
    nt = width // tn
    w_tiles = _tile_cols(w_glu, tn)
    return pl.pallas_call(
        _glu_kernel,
        grid=(m // tm, nt),
        in_specs=[
            pl.BlockSpec((tm, k), lambda i, j: (i, 0)),
            pl.BlockSpec((None, k, tn), lambda i, j: (j, 0, 0)),
            pl.BlockSpec((None, k, tn), lambda i, j: (nt + j, 0, 0)),
            pl.BlockSpec((1, tn), lambda i, j: (0, j)),
            pl.BlockSpec((1, tn), lambda i, j: (0, nt + j)),
            pl.BlockSpec((tm, tn), lambda i, j: (i, nt + j)),
        ],
        out_specs=pl.BlockSpec((tm, tn), lambda i, j: (i, j)),
        out_shape=jax.ShapeDtypeStruct((m, width), BF16),
        scratch_shapes=[pltpu.VMEM((tm, k), BF16)],
        compiler_params=_params(2),
        name="ssm_glu",
    )(y, w_tiles, w_tiles, b_glu, b_glu, proj)


def _ssm_mixer(x, g, w_in, lam_re, lam_im, log_dt, b_re, b_im, c_re, c_im, d_skip, w_glu, b_glu, bsz, seq):
    proj = _norm_matmul(x, g, w_in, F32)
    ops = _ssm_operators(lam_re, lam_im, log_dt, b_re, b_im, c_re, c_im, d_skip)
    y = _ssm_core(proj, ops, bsz, seq)
    return _glu(y, w_glu, b_glu.astype(F32).reshape(1, -1), proj)


CUM_BLOCK = 128


def _split3(x):
    x1 = x.astype(BF16)
    r1 = x - x1.astype(F32)
    x2 = r1.astype(BF16)
    x3 = (r1 - x2.astype(F32)).astype(BF16)
    return x1, x2, x3


def _forget_cumsum_kernel(z_ref, b_ref, o_ref):
    n_blocks = z_ref.shape[0] // CUM_BLOCK
    ri = lax.broadcasted_iota(jnp.int32, (CUM_BLOCK, CUM_BLOCK), 0)
    ci = lax.broadcasted_iota(jnp.int32, (CUM_BLOCK, CUM_BLOCK), 1)
    tri = jnp.where(ci <= ri, 1.0, 0.0).astype(BF16)

    def body(i, carry):
        rows = pl.ds(pl.multiple_of(i * CUM_BLOCK, CUM_BLOCK), CUM_BLOCK)
        z = z_ref[rows, :] + b_ref[...]
        log_f = jnp.minimum(z, 0.0) - jnp.log1p(jnp.exp(-jnp.abs(z)))
        x1, x2, x3 = _split3(log_f)
        c = _dot(tri, x1) + _dot(tri, x2) + _dot(tri, x3) + carry
        o_ref[rows, :] = c
        return c[CUM_BLOCK - 1:, :]

    lax.fori_loop(0, n_blocks, body, jnp.zeros((1, z_ref.shape[1]), F32))


def _forget_cumsum(z, b, bsz, seq):
    lanes = z.shape[1]
    return pl.pallas_call(
        _forget_cumsum_kernel,
        grid=(bsz,),
        in_specs=[pl.BlockSpec((seq, lanes), lambda i: (i, 0)), pl.BlockSpec((1, lanes), lambda i: (0, 0))],
        out_specs=pl.BlockSpec((seq, lanes), lambda i: (i, 0)),
        out_shape=jax.ShapeDtypeStruct(z.shape, F32),
        compiler_params=_params(1),
        name="forget_cumsum",
    )(z, b)


def _fox_kernel(q_ref, k_ref, v_ref, sg_ref, cq_ref, ck_ref, o_ref, ka_ref, vt_ref, *head_refs, tk):
    hp, qi = pl.program_id(1), pl.program_id(2)
    tq = q_ref.shape[0]
    qa_ref, s_ref, p_ref, acc_ref, pv_ref = (head_refs[i::5] for i in range(5))
    lane = lax.broadcasted_iota(jnp.int32, (1, LANES), 1)
    free = (HEAD_DIM, 0)
    n_pieces = 3

    def bias_lanes(c, hh, c_offset, one_offset):
        rr = lax.broadcasted_iota(jnp.int32, (n_pieces * LANES, LANES), 0)
        cc = lax.broadcasted_iota(jnp.int32, (n_pieces * LANES, LANES), 1)
        place = jnp.logical_and(rr % LANES == 2 * hp + hh, cc == free[hh] + c_offset + rr // LANES)
        placed = _dot(jnp.concatenate(_split3(c), axis=1), jnp.where(place, 1.0, 0.0).astype(BF16))
        first = free[hh] + one_offset
        return (placed + jnp.where(jnp.logical_and(lane >= first, lane < first + n_pieces), 1.0, 0.0)).astype(BF16)

    @pl.when(qi == 0)
    def _():
        def chunk(j, carry):
            rows = pl.ds(pl.multiple_of(j * tk, tk), tk)
            keys = k_ref[rows, :]
            neg_c = ck_ref[rows, :] * -LOG2E
            for hh in range(2):
                ka_ref[hh, j] = jnp.where((lane // HEAD_DIM) == hh, keys, bias_lanes(neg_c, hh, 0, n_pieces))
            vt_ref[j] = v_ref[rows, :].astype(F32).T.astype(BF16)
            return carry
        lax.fori_loop(0, k_ref.shape[0] // tk, chunk, 0)

    cq = cq_ref[...] * LOG2E
    for hh in range(2):
        qa_ref[hh][...] = jnp.where((lane // HEAD_DIM) == hh, q_ref[...], bias_lanes(cq, hh, n_pieces, 0))

    def scores(hh, j):
        s_ref[hh][...] = _dot_nt(ka_ref[hh, j], qa_ref[hh][...])

    rows8 = 8
    rows16 = 16

    def absorb(hh, j, m, l, diagonal):
        def strip(r, n):
            blk = s_ref[hh][r:r + n, :]
            if diagonal:
                kpos = lax.broadcasted_iota(jnp.int32, (n, tq), 0)
                qpos = lax.broadcasted_iota(jnp.int32, (n, tq), 1)
                blk = jnp.where(kpos + r <= qpos, blk, NEG)
            return blk
        top = strip(0, rows8)
        for r in range(rows8, tk, rows8):
            top = jnp.maximum(top, strip(r, rows8))
        m_new = jnp.maximum(m, top.max(axis=0, keepdims=True))
        alpha = jnp.exp2(m - m_new)
        m_rows = jnp.broadcast_to(m_new, (rows16, tq))
        total = None
        for r in range(0, tk, rows16):
            p = jnp.exp2(strip(r, rows16) - m_rows)
            total = p if total is None else total + p
            p_ref[hh][r:r + rows16, :] = p.astype(BF16)
        acc_ref[hh][...] = alpha * (acc_ref[hh][...] + pv_ref[hh][...])
        return m_new, alpha * l + total.sum(axis=0, keepdims=True)

    def values(hh, j):
        pv_ref[hh][...] = _dot(vt_ref[j], p_ref[hh][...])

    def step(j, carry, diagonal):
        m0, l0, m1, l1 = carry
        values(1, jnp.maximum(j - 1, 0))
        scores(1, j)
        m0, l0 = absorb(0, j, m0, l0, diagonal)
        values(0, j)
        if not diagonal:
            scores(0, j + 1)
        m1, l1 = absorb(1, j, m1, l1, diagonal)
        return m0, l0, m1, l1

    for hh in range(2):
        acc_ref[hh][...] = jnp.zeros(acc_ref[hh].shape, F32)
        pv_ref[hh][...] = jnp.zeros(pv_ref[hh].shape, F32)
    p_ref[1][...] = jnp.zeros(p_ref[1].shape, BF16)
    scores(0, 0)
    init = (jnp.full((1, tq), NEG, F32), jnp.zeros((1, tq), F32)) * 2
    carry = lax.fori_loop(0, qi, lambda j, c: step(j, c, False), init)
    _, l0, _, l1 = step(qi, carry, True)
    values(1, qi)
    row = lax.broadcasted_iota(jnp.int32, (LANES, 1), 0)
    out = jnp.where(row < HEAD_DIM, (acc_ref[0][...] + pv_ref[0][...]) / l0, (acc_ref[1][...] + pv_ref[1][...]) / l1)
    o_ref[...] = (out.T * sg_ref[...].astype(F32)).astype(o_ref.dtype)


def _fox_attention(proj, csum, bsz, seq, t=512):
    width = FOX_HEADS * HEAD_DIM
    n_pairs = width // LANES
    t = min(t, seq)
    nq = seq // t
    return pl.pallas_call(
        functools.partial(_fox_kernel, tk=t),
        grid=(bsz, n_pairs, nq),
        in_specs=[
            pl.BlockSpec((t, LANES), lambda b, h, i: (b * nq + i, h)),
            pl.BlockSpec((seq, LANES), lambda b, h, i: (b, n_pairs + h)),
            pl.BlockSpec((seq, LANES), lambda b, h, i: (b, 2 * n_pairs + h)),
            pl.BlockSpec((t, LANES), lambda b, h, i: (b * nq + i, 3 * n_pairs + h)),
            pl.BlockSpec((t, LANES), lambda b, h, i: (b * nq + i, 0)),
            pl.BlockSpec((seq, LANES), lambda b, h, i: (b, 0)),
        ],
        out_specs=pl.BlockSpec((t, LANES), lambda b, h, i: (b * nq + i, h)),
        out_shape=jax.ShapeDtypeStruct((bsz * seq, width), BF16),
        scratch_shapes=[
            pltpu.VMEM((2, nq, t, LANES), BF16), pltpu.VMEM((nq, LANES, t), BF16),
        ] + 2 * [pltpu.VMEM((t, LANES), BF16), pltpu.VMEM((t, t), F32), pltpu.VMEM((t, t), BF16),
                 pltpu.VMEM((LANES, t), F32), pltpu.VMEM((LANES, t), F32)],
        compiler_params=_params(3),
        name="fox_attention",
    )(proj, proj, proj, proj, csum, csum)


def _fox_mixer(x, g, w_in, w_fg, b_fg, bsz, seq, t=512):
    width = FOX_HEADS * HEAD_DIM
    w = jnp.concatenate([w_in[:, :width] * (LOG2E * HEAD_DIM ** -0.5), w_in[:, width:]], axis=1)
    proj = _norm_matmul(x, g, w, BF16, silu_cols=(3 * width, 4 * width))
    pad = LANES - FOX_HEADS
    z = _norm_matmul(x, g, jnp.pad(w_fg, ((0, 0), (0, pad))), F32)
    csum = _forget_cumsum(z, jnp.pad(b_fg.astype(F32), (0, pad)).reshape(1, LANES), bsz, seq)
    return _fox_attention(proj, csum, bsz, seq, t)


def kernel(x, p, norm_g, final_g, rel_bias, swa_w_in, swa_w_out, swa_sinks, conv_w_in, conv_kernel, conv_w_out, ssm_w_in, ssm_lam_re, ssm_lam_im, ssm_log_dt, ssm_b_re, ssm_b_im, ssm_c_re, ssm_c_im, ssm_d, ssm_w_glu, ssm_b_glu, ssm_w_out, fox_w_in, fox_w_fg, fox_b_fg, fox_w_out, ple_proj, ple_norm, ple_gate):
    bsz, seq, d_model = x.shape
    depth = p.shape[0]
    h = x.astype(F32).reshape(bsz * seq, d_model)
    for i in range(depth):
        mixer, j = i % N_MIXERS, i // N_MIXERS
        if mixer == 0:
            a, w_out = _swa_mixer(h, norm_g[i], swa_w_in[j], swa_sinks[j], rel_bias, bsz, seq), swa_w_out[j]
        elif mixer == 1:
            a, w_out = _conv_mixer(h, norm_g[i], conv_w_in[j], conv_kernel[j], seq), conv_w_out[j]
        elif mixer == 2:
            a = _ssm_mixer(h, norm_g[i], ssm_w_in[j], ssm_lam_re[j], ssm_lam_im[j], ssm_log_dt[j], ssm_b_re[j], ssm_b_im[j],
                           ssm_c_re[j], ssm_c_im[j], ssm_d[j], ssm_w_glu[j], ssm_b_glu[j], bsz, seq)
            w_out = ssm_w_out[j]
        else:
            a, w_out = _fox_mixer(h, norm_g[i], fox_w_in[j], fox_w_fg[j], fox_b_fg[j], bsz, seq), fox_w_out[j]
        h = _out_ple(a, w_out, h, ple_norm[i], ple_gate[i], p[i].reshape(bsz * seq, -1), ple_proj[i],
                     final_g=final_g if i == depth - 1 else None)
    return h.reshape(bsz, seq, d_model).astype(x.dtype)
```

```python
import functools
import math

import numpy as np
import jax
import jax.numpy as jnp
from jax import lax
from jax.experimental import pallas as pl
from jax.experimental.pallas import tpu as pltpu

F32 = jnp.float32
BF16 = jnp.bfloat16

EPS = 1e-6
N_MIXERS = 4
PLE_DIM = 256

SWA_HEADS = 32
SWA_KV_HEADS = 4
SWA_GROUP = SWA_HEADS // SWA_KV_HEADS
HEAD_DIM = 64
SWA_BLOCK = 128
WINDOW = 128
REL_BUCKETS = 32
REL_MAX_DIST = 128

CONV_TAPS = 3

SSM_GROUP = 16
SSM_STATE = 64
SSM_CHUNK = 16

FOX_HEADS = 32

LANES = 128
GROUPS_PER_TILE = LANES // SSM_GROUP
VMEM_LIMIT = 48 * 1024 * 1024

NEG = float(jnp.finfo(jnp.float32).min)
LOG2E = math.log2(math.e)


def _params(n_axes):
    return pltpu.CompilerParams(dimension_semantics=("arbitrary",) * n_axes, vmem_limit_bytes=VMEM_LIMIT)


def _dot(a, b):
    return jnp.dot(a, b, preferred_element_type=F32)


def _dot_nt(a, b):
    return lax.dot_general(a, b, (((1,), (1,)), ((), ())), preferred_element_type=F32)


def _rmsnorm_rows(x, g):
    return x * lax.rsqrt(jnp.mean(x * x, axis=-1, keepdims=True) + EPS) * g


def _silu(x):
    return x * jax.nn.sigmoid(x)


def _tile_cols(w, tn):
    k, n = w.shape
    return w.astype(BF16).reshape(k, n // tn, tn).transpose(1, 0, 2)


ROW_CHUNK = 64


def _for_row_chunks(n_rows, fn):
    def body(c, carry):
        fn(pl.ds(pl.multiple_of(c * ROW_CHUNK, ROW_CHUNK), ROW_CHUNK))
        return carry
    lax.fori_loop(0, n_rows // ROW_CHUNK, body, 0)


def _norm_matmul_kernel(x_ref, g_ref, w_ref, o_ref, a_ref, *, silu_tiles):
    j = pl.program_id(1)

    @pl.when(j == 0)
    def _():
        def chunk(rows):
            a_ref[rows, :] = _rmsnorm_rows(x_ref[rows, :], g_ref[...]).astype(BF16)
        _for_row_chunks(a_ref.shape[0], chunk)

    acc = _dot(a_ref[...], w_ref[...])
    if silu_tiles is None:
        o_ref[...] = acc.astype(o_ref.dtype)
    else:
        gated = jnp.logical_and(j >= silu_tiles[0], j < silu_tiles[1])

        @pl.when(gated)
        def _():
            o_ref[...] = _silu(acc).astype(o_ref.dtype)

        @pl.when(jnp.logical_not(gated))
        def _():
            o_ref[...] = acc.astype(o_ref.dtype)


def _norm_matmul(x, g, w, out_dtype, silu_cols=None, tm=1024, tn=1024):
    m, k = x.shape
    n = w.shape[1]
    tm, tn = min(tm, m), min(tn, n)
    silu_tiles = None if silu_cols is None else (silu_cols[0] // tn, silu_cols[1] // tn)
    return pl.pallas_call(
        functools.partial(_norm_matmul_kernel, silu_tiles=silu_tiles),
        grid=(m // tm, n // tn),
        in_specs=[
            pl.BlockSpec((tm, k), lambda i, j: (i, 0)),
            pl.BlockSpec((1, k), lambda i, j: (0, 0)),
            pl.BlockSpec((None, k, tn), lambda i, j: (j, 0, 0)),
        ],
        out_specs=pl.BlockSpec((tm, tn), lambda i, j: (i, j)),
        out_shape=jax.ShapeDtypeStruct((m, n), out_dtype),
        scratch_shapes=[pltpu.VMEM((tm, k), BF16)],
        compiler_params=_params(2),
        name="norm_matmul",
    )(x, g.reshape(1, k), _tile_cols(w, tn))


def _out_ple_kernel(a_ref, wo_ref, x_ref, g_ref, wg_ref, p_ref, wp_ref, gf_ref, o_ref, x1_ref, hn_ref, ss_ref, *, nt, final):
    j = pl.program_id(1)
    tn = x_ref.shape[1]
    n = nt * tn

    @pl.when(j == 0)
    def _():
        ss_ref[...] = jnp.zeros(ss_ref.shape, F32)

    @pl.when(j < nt)
    def _():
        x1 = x_ref[...] + _dot(a_ref[...], wo_ref[j])
        x1_ref[j] = x1
        hn_ref[j] = (x1 * g_ref[j]).astype(BF16)
        ss_ref[0] += jnp.sum(x1 * x1, axis=-1, keepdims=True)

    @pl.when(j >= nt)
    def _():
        t = j - nt
        emb = _dot(p_ref[...].astype(BF16), wp_ref[t])
        acc = _dot(hn_ref[0], wg_ref[t, :tn, :])
        for kt in range(1, nt):
            acc = acc + _dot(hn_ref[kt], wg_ref[t, kt * tn:(kt + 1) * tn, :])
        x2 = x1_ref[t] + emb * jax.nn.sigmoid(acc * lax.rsqrt(ss_ref[0] / n + EPS))
        if final:
            x1_ref[t] = x2
            ss_ref[1] += jnp.sum(x2 * x2, axis=-1, keepdims=True)
        else:
            o_ref[...] = x2

    if final:
        @pl.when(j == 2 * nt - 1)
        def _():
            inv = lax.rsqrt(ss_ref[1] / n + EPS)
            for t in range(nt):
                o_ref[:, t * tn:(t + 1) * tn] = x1_ref[t] * inv * gf_ref[t]


def _out_ple(a, w_out, x, g, w_gate, p, w_proj, final_g=None, tm=512, tn=512):
    m, k = a.shape
    n = w_out.shape[1]
    pd = p.shape[1]
    tm, tn = min(tm, m), min(tn, n)
    nt = n // tn
    final = final_g is not None
    resident = lambda rows: pl.BlockSpec((nt, rows, tn), lambda i, j: (0, 0, 0), pipeline_mode=pl.Buffered(1))
    row_vec = pl.BlockSpec((nt, 1, tn), lambda i, j: (0, 0, 0))
    if final:
        out_spec = pl.BlockSpec((tm, n), lambda i, j: (i, 0))
    else:
        out_spec = pl.BlockSpec((tm, tn), lambda i, j: (i, jnp.maximum(j - nt, 0)))
    return pl.pallas_call(
        functools.partial(_out_ple_kernel, nt=nt, final=final),
        grid=(m // tm, 2 * nt),
        in_specs=[
            pl.BlockSpec((tm, k), lambda i, j: (i, 0)),
            resident(k),
            pl.BlockSpec((tm, tn), lambda i, j: (i, jnp.minimum(j, nt - 1))),
            row_vec,
            resident(n),
            pl.BlockSpec((tm, pd), lambda i, j: (i, 0)),
            resident(pd),
            row_vec,
        ],
        out_specs=out_spec,
        out_shape=jax.ShapeDtypeStruct((m, n), F32),
        scratch_shapes=[pltpu.VMEM((nt, tm, tn), F32), pltpu.VMEM((nt, tm, tn), BF16), pltpu.VMEM((2, tm, 1), F32)],
        compiler_params=_params(2),
        name="out_ple_final" if final else "out_ple",
    )(a, _tile_cols(w_out, tn), x, g.reshape(nt, 1, tn), _tile_cols(w_gate, tn), p, _tile_cols(w_proj, tn),
      (final_g if final else g).reshape(nt, 1, tn))


def _t5_bucket(dist):
    max_exact = REL_BUCKETS // 2
    d = np.maximum(dist, 1).astype(np.float32)
    large = max_exact + (np.log(d / max_exact) / np.log(REL_MAX_DIST / max_exact) * (REL_BUCKETS - max_exact)).astype(np.int32)
    large = np.minimum(large, REL_BUCKETS - 1)
    return np.where(dist < max_exact, dist, large).astype(np.int32)


def _swa_bucket_table():
    qi = np.arange(SWA_BLOCK)[None, :]
    kj = np.arange(2 * SWA_BLOCK)[:, None]
    return _t5_bucket(np.clip(qi + SWA_BLOCK - kj, 0, None))


def _swa_kernel(sink_ref, rel_ref, bucket_ref, q_ref, sg_ref, kp_ref, kc_ref, vp_ref, vc_ref, o_ref, bias_ref, qs_ref):
    n = pl.program_id(1)
    blk = SWA_BLOCK
    lane = lax.broadcasted_iota(jnp.int32, (1, LANES), 1)

    @pl.when(jnp.logical_and(pl.program_id(0) == 0, n == 0))
    def _():
        bucket = bucket_ref[...]
        kj = lax.broadcasted_iota(jnp.int32, (2 * blk, blk), 0)
        qi = lax.broadcasted_iota(jnp.int32, (2 * blk, blk), 1)
        band = jnp.logical_or(jnp.logical_and(kj < blk, kj > qi), jnp.logical_and(kj >= blk, kj - blk <= qi))

        def per_head(head, carry):
            acc = jnp.zeros((2 * blk, blk), F32)
            for b in range(REL_BUCKETS):
                acc = jnp.where(bucket == b, rel_ref[b, head], acc)
            bias_ref[head] = jnp.where(band, acc, NEG)
            return carry
        lax.fori_loop(0, SWA_HEADS, per_head, 0)

    for kvh in range(SWA_KV_HEADS):
        cols = slice(kvh * LANES, (kvh + 1) * LANES)
        heads = range(kvh * SWA_GROUP, (kvh + 1) * SWA_GROUP)
        for g, head in enumerate(heads):
            q_pair = q_ref[:, head // 2 * LANES:(head // 2 + 1) * LANES]
            qs_ref[g * blk:(g + 1) * blk, :] = jnp.where((lane // HEAD_DIM) == head % 2, q_pair, jnp.zeros_like(q_pair))
        keys = jnp.concatenate([kp_ref[:, cols], kc_ref[:, cols]], axis=0)
        vals = jnp.concatenate([vp_ref[:, cols], vc_ref[:, cols]], axis=0)
        s = _dot_nt(keys, qs_ref[...]) + jnp.concatenate([bias_ref[head] for head in heads], axis=1)
        s_prev = jnp.where(n > 0, s[:blk], NEG)
        s_cur = s[blk:]
        sink = jnp.concatenate([jnp.full((1, blk), sink_ref[head], F32) for head in heads], axis=1)
        m = jnp.maximum(jnp.maximum(s_prev, s_cur).max(axis=0, keepdims=True), sink)
        e_prev, e_cur = jnp.exp(s_prev - m), jnp.exp(s_cur - m)
        den = e_prev.sum(axis=0, keepdims=True) + e_cur.sum(axis=0, keepdims=True) + jnp.exp(sink - m)
        e = jnp.concatenate([e_prev, e_cur], axis=0).astype(BF16)
        out = _dot(vals.astype(F32).T.astype(BF16), e) / den
        for pair in range(SWA_GROUP // 2):
            head = kvh * SWA_GROUP + 2 * pair
            even = out[:HEAD_DIM, 2 * pair * blk:(2 * pair + 1) * blk]
            odd = out[HEAD_DIM:, (2 * pair + 1) * blk:(2 * pair + 2) * blk]
            cols = slice(head // 2 * LANES, (head // 2 + 1) * LANES)
            o_ref[:, cols] = (jnp.concatenate([even, odd], axis=0).T * sg_ref[:, cols].astype(F32)).astype(o_ref.dtype)


def _swa_attention(proj, sinks, rel_bias, bsz, seq):
    nb = seq // SWA_BLOCK
    width = SWA_HEADS * HEAD_DIM
    kvw = SWA_KV_HEADS * LANES
    k_blk = 2 * width // kvw
    cur = lambda b, n: b * nb + n
    prev = lambda b, n: b * nb + jnp.maximum(n - 1, 0)
    smem = pl.BlockSpec(memory_space=pltpu.SMEM)
    return pl.pallas_call(
        _swa_kernel,
        grid=(bsz, nb),
        in_specs=[
            smem, smem,
            pl.BlockSpec((2 * SWA_BLOCK, SWA_BLOCK), lambda b, n: (0, 0)),
            pl.BlockSpec((SWA_BLOCK, width), lambda b, n: (cur(b, n), 0)),
            pl.BlockSpec((SWA_BLOCK, width), lambda b, n: (cur(b, n), 1)),
            pl.BlockSpec((SWA_BLOCK, kvw), lambda b, n: (prev(b, n), k_blk)),
            pl.BlockSpec((SWA_BLOCK, kvw), lambda b, n: (cur(b, n), k_blk)),
            pl.BlockSpec((SWA_BLOCK, kvw), lambda b, n: (prev(b, n), k_blk + 1)),
            pl.BlockSpec((SWA_BLOCK, kvw), lambda b, n: (cur(b, n), k_blk + 1)),
        ],
        out_specs=pl.BlockSpec((SWA_BLOCK, width), lambda b, n: (cur(b, n), 0)),
        out_shape=jax.ShapeDtypeStruct((bsz * seq, width), BF16),
        scratch_shapes=[
            pltpu.VMEM((SWA_HEADS, 2 * SWA_BLOCK, SWA_BLOCK), F32),
            pltpu.VMEM((SWA_GROUP * SWA_BLOCK, LANES), BF16),
        ],
        compiler_params=_params(2),
        name="swa_attention",
    )(sinks.astype(F32), rel_bias.astype(F32), jnp.asarray(_swa_bucket_table()), proj, proj, proj, proj, proj, proj)


def _dup_heads(w, n_heads):
    k = w.shape[0]
    w = w.reshape(k, n_heads, 1, HEAD_DIM)
    return jnp.broadcast_to(w, (k, n_heads, 2, HEAD_DIM)).reshape(k, n_heads * 2 * HEAD_DIM)


def _swa_mixer(x, g, w_in, sinks, rel_bias, bsz, seq):
    width = SWA_HEADS * HEAD_DIM
    kvw = SWA_KV_HEADS * HEAD_DIM
    w_q, w_k, w_v, w_g = (w_in[:, :width], w_in[:, width:width + kvw],
                          w_in[:, width + kvw:width + 2 * kvw], w_in[:, width + 2 * kvw:])
    w = jnp.concatenate([w_q * (HEAD_DIM ** -0.5), w_g, _dup_heads(w_k, SWA_KV_HEADS), _dup_heads(w_v, SWA_KV_HEADS)], axis=1)
    proj = _norm_matmul(x, g, w, BF16, silu_cols=(width, 2 * width))
    return _swa_attention(proj, sinks, rel_bias, bsz, seq)


HALO = 8


def _conv_kernel(x_ref, g_ref, wb_ref, wc_ref, wu_ref, wg_ref, ck_ref, o_ref, a_ref, z_ref, carry_ref, *, tiles_per_seq):
    i, j = pl.program_id(0), pl.program_id(1)
    tm = a_ref.shape[0]

    @pl.when(j == 0)
    def _():
        def chunk(rows):
            a_ref[rows, :] = _rmsnorm_rows(x_ref[rows, :], g_ref[...]).astype(BF16)
        _for_row_chunks(tm, chunk)

    a = a_ref[...]
    z = _dot(a, wc_ref[...]) * _dot(a, wu_ref[...])
    first = (i % tiles_per_seq) == 0

    @pl.when(first)
    def _():
        z_ref[:HALO, :] = jnp.zeros((HALO, z_ref.shape[1]), F32)

    @pl.when(jnp.logical_not(first))
    def _():
        z_ref[:HALO, :] = carry_ref[j]

    z_ref[HALO:, :] = z
    carry_ref[j] = z[tm - HALO:, :]
    conv = z_ref[HALO - 2:HALO - 2 + tm, :] * ck_ref[0:1, :]
    conv = conv + z_ref[HALO - 1:HALO - 1 + tm, :] * ck_ref[1:2, :]
    conv = conv + z * ck_ref[2:3, :]
    y = _dot(a, wb_ref[...]) * conv
    o_ref[...] = (y * _silu(_dot(a, wg_ref[...]))).astype(o_ref.dtype)


def _conv_mixer(x, g, w_in, conv_kernel, seq, tm=512, tn=512):
    m, k = x.shape
    width = w_in.shape[1] // 4
    tm, tn = min(tm, seq), min(tn, width)
    nt = width // tn
    w_spec = lambda q: pl.BlockSpec((None, k, tn), lambda i, j: (q * nt + j, 0, 0))
    w_in = _tile_cols(w_in, tn)
    return pl.pallas_call(
        functools.partial(_conv_kernel, tiles_per_seq=seq // tm),
        grid=(m // tm, nt),
        in_specs=[
            pl.BlockSpec((tm, k), lambda i, j: (i, 0)),
            pl.BlockSpec((1, k), lambda i, j: (0, 0)),
            w_spec(0), w_spec(1), w_spec(2), w_spec(3),
            pl.BlockSpec((CONV_TAPS, tn), lambda i, j: (0, j)),
        ],
        out_specs=pl.BlockSpec((tm, tn), lambda i, j: (i, j)),
        out_shape=jax.ShapeDtypeStruct((m, width), BF16),
        scratch_shapes=[
            pltpu.VMEM((tm, k), BF16),
            pltpu.VMEM((HALO + tm, tn), F32),
            pltpu.VMEM((nt, HALO, tn), F32),
        ],
        compiler_params=_params(2),
        name="conv_mixer",
    )(x, g.reshape(1, k), w_in, w_in, w_in, w_in, conv_kernel.astype(F32))


def _ssm_operators(lam_re, lam_im, log_dt, b_re, b_im, c_re, c_im, d_skip):
    n_groups = lam_re.shape[0]
    n_oct = n_groups // GROUPS_PER_TILE
    L, C, N = SSM_CHUNK, SSM_GROUP, SSM_STATE
    dt = jnp.exp(log_dt.astype(F32))[:, None]
    lr, li = lam_re.astype(F32), lam_im.astype(F32)
    mag = jnp.exp(lr * dt)
    ab_re, ab_im = mag * jnp.cos(li * dt), mag * jnp.sin(li * dt)
    den = lr * lr + li * li
    nr = ab_re - 1.0
    coef_re = ((nr * lr + ab_im * li) / den)[..., None]
    coef_im = ((ab_im * lr - nr * li) / den)[..., None]
    br, bi = b_re.astype(F32), b_im.astype(F32)
    bb_re = coef_re * br - coef_im * bi
    bb_im = coef_re * bi + coef_im * br
    cr, ci = c_re.astype(F32), c_im.astype(F32)
    p_re, p_im = [jnp.ones_like(ab_re)], [jnp.zeros_like(ab_im)]
    for _ in range(L):
        p_re, p_im = (p_re + [p_re[-1] * ab_re - p_im[-1] * ab_im], p_im + [p_re[-1] * ab_im + p_im[-1] * ab_re])
    p_re, p_im = jnp.stack(p_re), jnp.stack(p_im)
    w_re = cr[None] * p_re[:L, :, None, :] - ci[None] * p_im[:L, :, None, :]
    w_im = cr[None] * p_im[:L, :, None, :] + ci[None] * p_re[:L, :, None, :]
    bt_re, bt_im = bb_re.transpose(0, 2, 1), bb_im.transpose(0, 2, 1)
    lag_k = jnp.sum(w_re[:, :, None] * bt_re[None, :, :, None] - w_im[:, :, None] * bt_im[None, :, :, None], axis=-1)
    q_re, q_im = p_re[:L][::-1], p_im[:L][::-1]
    bc_re = q_re[:, :, None, :] * bt_re[None] - q_im[:, :, None, :] * bt_im[None]
    bc_im = q_re[:, :, None, :] * bt_im[None] + q_im[:, :, None, :] * bt_re[None]
    o_re = cr[None] * p_re[1:, :, None, :] - ci[None] * p_im[1:, :, None, :]
    o_im = cr[None] * p_im[1:, :, None, :] + ci[None] * p_re[1:, :, None, :]
    octs = lambda t: t.reshape(L, n_oct, GROUPS_PER_TILE, C, -1)
    to_lag = lambda t: octs(t).transpose(1, 0, 3, 2, 4).reshape(n_oct, L, C, LANES)
    to_rows = lambda t: octs(t).transpose(1, 0, 3, 2, 4).reshape(n_oct, L * C, GROUPS_PER_TILE * N)
    to_cols = lambda t: octs(t).transpose(1, 2, 4, 0, 3).reshape(n_oct, GROUPS_PER_TILE * N, L * C)
    oct_vec = lambda t: t.reshape(n_oct, 1, -1)
    return dict(
        lag=to_lag(lag_k).astype(BF16),
        bc_re=to_rows(bc_re).astype(BF16), bc_im=to_rows(bc_im).astype(BF16),
        oc_re=to_cols(o_re).astype(BF16), oc_im_neg=to_cols(-o_im).astype(BF16),
        al_re=oct_vec(p_re[L]), al_im=oct_vec(p_im[L]), d=oct_vec(d_skip.astype(F32)),
    )


def _ssm_expanders():
    L, C, N, G8 = SSM_CHUNK, SSM_GROUP, SSM_STATE, GROUPS_PER_TILE
    wide = np.arange(L * LANES)
    w_step, w_group, w_chan = wide // LANES, (wide // C) % G8, wide % C
    small = np.arange(L * C)
    s_step, s_chan = small // C, small % C
    state_group = np.arange(G8 * N) // N
    spread = (w_step[:, None] == s_step[None, :]) & (w_chan[:, None] == s_chan[None, :])
    lane = np.arange(LANES)
    as_bf16 = lambda a: jnp.asarray(a.astype(np.float32), dtype=BF16)
    return dict(
        spread_rows=as_bf16(spread), spread_cols=as_bf16(spread.T),
        keep_rows=as_bf16(w_group[:, None] == state_group[None, :]),
        keep_cols=as_bf16(state_group[:, None] == w_group[None, :]),
        spread_lag=as_bf16((lane % C)[:, None] == np.arange(C)[None, :]),
        keep_lag=jnp.asarray(((lane // C)[:, None] == (lane // C)[None, :]).astype(np.float32)),
    )


def _ssm_kernel(u_ref, lag_ref, bxr_ref, bxi_ref, cxr_ref, cxi_ref, alr_ref, ali_ref, d_ref,
                sr_ref, sc_ref, kr_ref, kc_ref, sl_ref, kl_ref, o_ref,
                ub_ref, bcr_ref, bci_ref, ocr_ref, oci_ref, zr_ref, zi_ref, hr_ref, hi_ref, *, bsz):
    L = SSM_CHUNK
    rows = u_ref.shape[0] // L
    chunks = rows // bsz
    tile = 2 * LANES
    step_rows = lambda r: pl.ds(r, rows, stride=L)
    for r in range(L):
        ub_ref[:, r * LANES:(r + 1) * LANES] = u_ref[step_rows(r), :].astype(BF16)
    bcr_ref[...] = (_dot(sr_ref[...], bxr_ref[...]) * kr_ref[...]).astype(BF16)
    bci_ref[...] = (_dot(sr_ref[...], bxi_ref[...]) * kr_ref[...]).astype(BF16)
    zr_ref[...] = _dot(ub_ref[...], bcr_ref[...])
    zi_ref[...] = _dot(ub_ref[...], bci_ref[...])
    a_re, a_im = alr_ref[...], ali_ref[...]

    def step(k, carry):
        new = []
        for b in range(bsz):
            h_re, h_im = carry[2 * b], carry[2 * b + 1]
            row = pl.ds(b * chunks + k, 1)
            hr_ref[row, :] = h_re
            hi_ref[row, :] = h_im
            new.append(a_re * h_re - a_im * h_im + zr_ref[row, :])
            new.append(a_re * h_im + a_im * h_re + zi_ref[row, :])
        return tuple(new)

    zero = jnp.zeros((1, a_re.shape[1]), F32)
    lax.fori_loop(0, chunks, step, (zero,) * (2 * bsz))

    ocr_ref[...] = (_dot(cxr_ref[...], sc_ref[...]) * kc_ref[...]).astype(BF16)
    oci_ref[...] = (_dot(cxi_ref[...], sc_ref[...]) * kc_ref[...]).astype(BF16)
    lag = [(_dot(sl_ref[...], lag_ref[d]) * kl_ref[...]).astype(BF16) for d in range(L)]
    lag_tile = []
    for dd in range(L // 2):
        below = lag[2 * dd - 1] if dd > 0 else jnp.zeros((LANES, LANES), BF16)
        lag_tile.append(jnp.concatenate([jnp.concatenate([lag[2 * dd], lag[2 * dd + 1]], axis=1),
                                         jnp.concatenate([below, lag[2 * dd]], axis=1)], axis=0))
    hb_re, hb_im = hr_ref[...].astype(BF16), hi_ref[...].astype(BF16)
    for t2 in range(L // 2):
        cols = slice(t2 * tile, (t2 + 1) * tile)
        y = _dot(hb_re, ocr_ref[:, cols]) + _dot(hb_im, oci_ref[:, cols])
        for r2 in range(t2 + 1):
            y = y + _dot(ub_ref[:, r2 * tile:(r2 + 1) * tile], lag_tile[t2 - r2])
        for half in range(2):
            t = 2 * t2 + half
            o_ref[step_rows(t), :] = jax.nn.gelu(y[:, half * LANES:(half + 1) * LANES] + d_ref[...] * u_ref[step_rows(t), :])


def _ssm_core(proj, ops, bsz, seq):
    m = proj.shape[0]
    width = proj.shape[1] // 2
    L, C = SSM_CHUNK, SSM_GROUP
    n_oct = width // LANES
    rows = m // L
    ow = L * LANES
    n_state = GROUPS_PER_TILE * SSM_STATE
    ex = _ssm_expanders()
    per_oct = lambda *shape: pl.BlockSpec((None,) + shape, lambda p: (p,) + (0,) * len(shape))
    const = lambda a: pl.BlockSpec(a.shape, lambda p: (0,) * a.ndim, pipeline_mode=pl.Buffered(1))
    consts = [ex['spread_rows'], ex['spread_cols'], ex['keep_rows'], ex['keep_cols'], ex['spread_lag'], ex['keep_lag']]
    return pl.pallas_call(
        functools.partial(_ssm_kernel, bsz=bsz),
        grid=(n_oct,),
        in_specs=[
            pl.BlockSpec((m, LANES), lambda p: (0, p)),
            per_oct(L, C, LANES), per_oct(L * C, n_state), per_oct(L * C, n_state),
            per_oct(n_state, L * C), per_oct(n_state, L * C), per_oct(1, n_state), per_oct(1, n_state), per_oct(1, LANES),
        ] + [const(a) for a in consts],
        out_specs=pl.BlockSpec((m, LANES), lambda p: (0, p)),
        out_shape=jax.ShapeDtypeStruct((m, width), F32),
        scratch_shapes=[pltpu.VMEM((rows, ow), BF16)] + [pltpu.VMEM((ow, n_state), BF16)] * 2
        + [pltpu.VMEM((n_state, ow), BF16)] * 2 + [pltpu.VMEM((rows, n_state), F32)] * 4,
        compiler_params=_params(1),
        name="ssm_core",
    )(proj, ops['lag'], ops['bc_re'], ops['bc_im'], ops['oc_re'], ops['oc_im_neg'], ops['al_re'], ops['al_im'], ops['d'], *consts)


def _glu_kernel(y_ref, wa_ref, wb_ref, ba_ref, bb_ref, gate_ref, o_ref, yb_ref):
    @pl.when(pl.program_id(1) == 0)
    def _():
        def chunk(rows):
            yb_ref[rows, :] = y_ref[rows, :].astype(BF16)
        _for_row_chunks(yb_ref.shape[0], chunk)

    y = yb_ref[...]
    ga = _dot(y, wa_ref[...]) + ba_ref[...]
    gb = _dot(y, wb_ref[...]) + bb_ref[...]
    o_ref[...] = ((ga * jax.nn.sigmoid(gb)) * _silu(gate_ref[...])).astype(o_ref.dtype)


def _glu(y, w_glu, b_glu, proj, tm=512, tn=512):
    m, k = y.shape
    width = w_glu.shape[1] // 2
    tm, tn = min(tm, m), min(tn, width)
    nt = width // tn
    w_tiles = _tile_cols(w_glu, tn)
    return pl.pallas_call(
        _glu_kernel,
        grid=(m // tm, nt),
        in_specs=[
            pl.BlockSpec((tm, k), lambda i, j: (i, 0)),
            pl.BlockSpec((None, k, tn), lambda i, j: (j, 0, 0)),
            pl.BlockSpec((None, k, tn), lambda i, j: (nt + j, 0, 0)),
            pl.BlockSpec((1, tn), lambda i, j: (0, j)),
            pl.BlockSpec((1, tn), lambda i, j: (0, nt + j)),
            pl.BlockSpec((tm, tn), lambda i, j: (i, nt + j)),
        ],
        out_specs=pl.BlockSpec((tm, tn), lambda i, j: (i, j)),
        out_shape=jax.ShapeDtypeStruct((m, width), BF16),
        scratch_shapes=[pltpu.VMEM((tm, k), BF16)],
        compiler_params=_params(2),
        name="ssm_glu",
    )(y, w_tiles, w_tiles, b_glu, b_glu, proj)


def _ssm_mixer(x, g, w_in, lam_re, lam_im, log_dt, b_re, b_im, c_re, c_im, d_skip, w_glu, b_glu, bsz, seq):
    proj = _norm_matmul(x, g, w_in, F32)
    ops = _ssm_operators(lam_re, lam_im, log_dt, b_re, b_im, c_re, c_im, d_skip)
    y = _ssm_core(proj, ops, bsz, seq)
    return _glu(y, w_glu, b_glu.astype(F32).reshape(1, -1), proj)


CUM_BLOCK = 128


def _split3(x):
    x1 = x.astype(BF16)
    r1 = x - x1.astype(F32)
    x2 = r1.astype(BF16)
    x3 = (r1 - x2.astype(F32)).astype(BF16)
    return x1, x2, x3


def _forget_cumsum_kernel(z_ref, b_ref, o_ref):
    n_blocks = z_ref.shape[0] // CUM_BLOCK
    ri = lax.broadcasted_iota(jnp.int32, (CUM_BLOCK, CUM_BLOCK), 0)
    ci = lax.broadcasted_iota(jnp.int32, (CUM_BLOCK, CUM_BLOCK), 1)
    tri = jnp.where(ci <= ri, 1.0, 0.0).astype(BF16)

    def body(i, carry):
        rows = pl.ds(pl.multiple_of(i * CUM_BLOCK, CUM_BLOCK), CUM_BLOCK)
        z = z_ref[rows, :] + b_ref[...]
        log_f = jnp.minimum(z, 0.0) - jnp.log1p(jnp.exp(-jnp.abs(z)))
        x1, x2, x3 = _split3(log_f)
        c = _dot(tri, x1) + _dot(tri, x2) + _dot(tri, x3) + carry
        o_ref[rows, :] = c
        return c[CUM_BLOCK - 1:, :]

    lax.fori_loop(0, n_blocks, body, jnp.zeros((1, z_ref.shape[1]), F32))


def _forget_cumsum(z, b, bsz, seq):
    lanes = z.shape[1]
    return pl.pallas_call(
        _forget_cumsum_kernel,
        grid=(bsz,),
        in_specs=[pl.BlockSpec((seq, lanes), lambda i: (i, 0)), pl.BlockSpec((1, lanes), lambda i: (0, 0))],
        out_specs=pl.BlockSpec((seq, lanes), lambda i: (i, 0)),
        out_shape=jax.ShapeDtypeStruct(z.shape, F32),
        compiler_params=_params(1),
        name="forget_cumsum",
    )(z, b)


def _fox_kernel(q_ref, k_ref, v_ref, sg_ref, cq_ref, ck_ref, o_ref, ka_ref, vt_ref, *head_refs, tk):
    hp, qi = pl.program_id(1), pl.program_id(2)
    tq = q_ref.shape[0]
    qa_ref, s_ref, p_ref, acc_ref, pv_ref = (head_refs[i::5] for i in range(5))
    lane = lax.broadcasted_iota(jnp.int32, (1, LANES), 1)
    free = (HEAD_DIM, 0)
    n_pieces = 3

    def bias_lanes(c, hh, c_offset, one_offset):
        rr = lax.broadcasted_iota(jnp.int32, (n_pieces * LANES, LANES), 0)
        cc = lax.broadcasted_iota(jnp.int32, (n_pieces * LANES, LANES), 1)
        place = jnp.logical_and(rr % LANES == 2 * hp + hh, cc == free[hh] + c_offset + rr // LANES)
        placed = _dot(jnp.concatenate(_split3(c), axis=1), jnp.where(place, 1.0, 0.0).astype(BF16))
        first = free[hh] + one_offset
        return (placed + jnp.where(jnp.logical_and(lane >= first, lane < first + n_pieces), 1.0, 0.0)).astype(BF16)

    @pl.when(qi == 0)
    def _():
        def chunk(j, carry):
            rows = pl.ds(pl.multiple_of(j * tk, tk), tk)
            keys = k_ref[rows, :]
            neg_c = ck_ref[rows, :] * -LOG2E
            for hh in range(2):
                ka_ref[hh, j] = jnp.where((lane // HEAD_DIM) == hh, keys, bias_lanes(neg_c, hh, 0, n_pieces))
            vt_ref[j] = v_ref[rows, :].astype(F32).T.astype(BF16)
            return carry
        lax.fori_loop(0, k_ref.shape[0] // tk, chunk, 0)

    cq = cq_ref[...] * LOG2E
    for hh in range(2):
        qa = jnp.where((lane // HEAD_DIM) == hh, q_ref[...], bias_lanes(cq, hh, n_pieces, 0))
        qa_ref[hh][...] = qa.astype(F32).T.astype(BF16)

    def scores(hh, j):
        s_ref[hh][...] = _dot(ka_ref[hh, j], qa_ref[hh][...])

    rows8 = 8
    rows16 = 16

    def absorb(hh, j, m, l, diagonal):
        def strip(r, n):
            blk = s_ref[hh][r:r + n, :]
            if diagonal:
                kpos = lax.broadcasted_iota(jnp.int32, (n, tq), 0)
                qpos = lax.broadcasted_iota(jnp.int32, (n, tq), 1)
                blk = jnp.where(kpos + r <= qpos, blk, NEG)
            return blk
        top = strip(0, rows8)
        for r in range(rows8, tk, rows8):
            top = jnp.maximum(top, strip(r, rows8))
        m_new = jnp.maximum(m, top.max(axis=0, keepdims=True))
        alpha = jnp.exp2(m - m_new)
        m_rows = jnp.broadcast_to(m_new, (rows16, tq))
        total = None
        for r in range(0, tk, rows16):
            p = jnp.exp2(strip(r, rows16) - m_rows)
            total = p if total is None else total + p
            p_ref[hh][r:r + rows16, :] = p.astype(BF16)
        acc_ref[hh][...] = alpha * (acc_ref[hh][...] + pv_ref[hh][...])
        return m_new, alpha * l + total.sum(axis=0, keepdims=True)

    def values(hh, j):
        pv_ref[hh][...] = _dot(vt_ref[j], p_ref[hh][...])

    def step(j, carry, diagonal):
        m0, l0, m1, l1 = carry
        values(1, jnp.maximum(j - 1, 0))
        scores(1, j)
        m0, l0 = absorb(0, j, m0, l0, diagonal)
        values(0, j)
        if not diagonal:
            scores(0, j + 1)
        m1, l1 = absorb(1, j, m1, l1, diagonal)
        return m0, l0, m1, l1

    for hh in range(2):
        acc_ref[hh][...] = jnp.zeros(acc_ref[hh].shape, F32)
        pv_ref[hh][...] = jnp.zeros(pv_ref[hh].shape, F32)
    p_ref[1][...] = jnp.zeros(p_ref[1].shape, BF16)
    scores(0, 0)
    init = (jnp.full((1, tq), NEG, F32), jnp.zeros((1, tq), F32)) * 2
    carry = lax.fori_loop(0, qi, lambda j, c: step(j, c, False), init)
    _, l0, _, l1 = step(qi, carry, True)
    values(1, qi)
    row = lax.broadcasted_iota(jnp.int32, (LANES, 1), 0)
    out = jnp.where(row < HEAD_DIM, (acc_ref[0][...] + pv_ref[0][...]) / l0, (acc_ref[1][...] + pv_ref[1][...]) / l1)
    o_ref[...] = (out.T * sg_ref[...].astype(F32)).astype(o_ref.dtype)


def _fox_attention(proj, csum, bsz, seq, t=512):
    width = FOX_HEADS * HEAD_DIM
    n_pairs = width // LANES
    t = min(t, seq)
    nq = seq // t
    return pl.pallas_call(
        functools.partial(_fox_kernel, tk=t),
        grid=(bsz, n_pairs, nq),
        in_specs=[
            pl.BlockSpec((t, LANES), lambda b, h, i: (b * nq + i, h)),
            pl.BlockSpec((seq, LANES), lambda b, h, i: (b, n_pairs + h)),
            pl.BlockSpec((seq, LANES), lambda b, h, i: (b, 2 * n_pairs + h)),
            pl.BlockSpec((t, LANES), lambda b, h, i: (b * nq + i, 3 * n_pairs + h)),
            pl.BlockSpec((t, LANES), lambda b, h, i: (b * nq + i, 0)),
            pl.BlockSpec((seq, LANES), lambda b, h, i: (b, 0)),
        ],
        out_specs=pl.BlockSpec((t, LANES), lambda b, h, i: (b * nq + i, h)),
        out_shape=jax.ShapeDtypeStruct((bsz * seq, width), BF16),
        scratch_shapes=[
            pltpu.VMEM((2, nq, t, LANES), BF16), pltpu.VMEM((nq, LANES, t), BF16),
        ] + 2 * [pltpu.VMEM((LANES, t), BF16), pltpu.VMEM((t, t), F32), pltpu.VMEM((t, t), BF16),
                 pltpu.VMEM((LANES, t), F32), pltpu.VMEM((LANES, t), F32)],
        compiler_params=_params(3),
        name="fox_attention",
    )(proj, proj, proj, proj, csum, csum)


def _fox_mixer(x, g, w_in, w_fg, b_fg, bsz, seq, t=512):
    width = FOX_HEADS * HEAD_DIM
    w = jnp.concatenate([w_in[:, :width] * (LOG2E * HEAD_DIM ** -0.5), w_in[:, width:]], axis=1)
    proj = _norm_matmul(x, g, w, BF16, silu_cols=(3 * width, 4 * width))
    pad = LANES - FOX_HEADS
    z = _norm_matmul(x, g, jnp.pad(w_fg, ((0, 0), (0, pad))), F32)
    csum = _forget_cumsum(z, jnp.pad(b_fg.astype(F32), (0, pad)).reshape(1, LANES), bsz, seq)
    return _fox_attention(proj, csum, bsz, seq, t)


def kernel(x, p, norm_g, final_g, rel_bias, swa_w_in, swa_w_out, swa_sinks, conv_w_in, conv_kernel, conv_w_out, ssm_w_in, ssm_lam_re, ssm_lam_im, ssm_log_dt, ssm_b_re, ssm_b_im, ssm_c_re, ssm_c_im, ssm_d, ssm_w_glu, ssm_b_glu, ssm_w_out, fox_w_in, fox_w_fg, fox_b_fg, fox_w_out, ple_proj, ple_norm, ple_gate):
    bsz, seq, d_model = x.shape
    depth = p.shape[0]
    h = x.astype(F32).reshape(bsz * seq, d_model)
    for i in range(depth):
        mixer, j = i % N_MIXERS, i // N_MIXERS
        if mixer == 0:
            a, w_out = _swa_mixer(h, norm_g[i], swa_w_in[j], swa_sinks[j], rel_bias, bsz, seq), swa_w_out[j]
        elif mixer == 1:
            a, w_out = _conv_mixer(h, norm_g[i], conv_w_in[j], conv_kernel[j], seq), conv_w_out[j]
        elif mixer == 2:
            a = _ssm_mixer(h, norm_g[i], ssm_w_in[j], ssm_lam_re[j], ssm_lam_im[j], ssm_log_dt[j], ssm_b_re[j], ssm_b_im[j],
                           ssm_c_re[j], ssm_c_im[j], ssm_d[j], ssm_w_glu[j], ssm_b_glu[j], bsz, seq)
            w_out = ssm_w_out[j]
        else:
            a, w_out = _fox_mixer(h, norm_g[i], fox_w_in[j], fox_w_fg[j], fox_b_fg[j], bsz, seq), fox_w_out[j]
        h = _out_ple(a, w_out, h, ple_norm[i], ple_gate[i], p[i].reshape(bsz * seq, -1), ple_proj[i],
                     final_g=final_g if i == depth - 1 else None)
    return h.reshape(bsz, seq, d_model).astype(x.dtype)
```

```python
import functools
import math

import numpy as np
import jax
import jax.numpy as jnp
from jax import lax
from jax.experimental import pallas as pl
from jax.experimental.pallas import tpu as pltpu

F32 = jnp.float32
BF16 = jnp.bfloat16

EPS = 1e-6
N_MIXERS = 4
PLE_DIM = 256

SWA_HEADS = 32
SWA_KV_HEADS = 4
SWA_GROUP = SWA_HEADS // SWA_KV_HEADS
HEAD_DIM = 64
SWA_BLOCK = 128
WINDOW = 128
REL_BUCKETS = 32
REL_MAX_DIST = 128

CONV_TAPS = 3

SSM_GROUP = 16
SSM_STATE = 64
SSM_CHUNK = 16

FOX_HEADS = 32

LANES = 128
GROUPS_PER_TILE = LANES // SSM_GROUP
VMEM_LIMIT = 48 * 1024 * 1024

NEG = float(jnp.finfo(jnp.float32).min)
LOG2E = math.log2(math.e)


def _params(n_axes):
    return pltpu.CompilerParams(dimension_semantics=("arbitrary",) * n_axes, vmem_limit_bytes=VMEM_LIMIT)


def _dot(a, b):
    return jnp.dot(a, b, preferred_element_type=F32)


def _dot_nt(a, b):
    return lax.dot_general(a, b, (((1,), (1,)), ((), ())), preferred_element_type=F32)


def _rmsnorm_rows(x, g):
    return x * lax.rsqrt(jnp.mean(x * x, axis=-1, keepdims=True) + EPS) * g


def _silu(x):
    return x * jax.nn.sigmoid(x)


ROW_CHUNK = 64


def _for_row_chunks(n_rows, fn):
    def body(c, carry):
        fn(pl.ds(pl.multiple_of(c * ROW_CHUNK, ROW_CHUNK), ROW_CHUNK))
        return carry
    lax.fori_loop(0, n_rows // ROW_CHUNK, body, 0)


def _tile_cols_xla(w, tn):
    k, n = w.shape
    return w.astype(BF16).reshape(k, n // tn, tn).transpose(1, 0, 2)


def _cast_tiles_kernel(w_ref, o_ref):
    def chunk(rows):
        o_ref[rows, :] = w_ref[rows, :].astype(BF16)
    _for_row_chunks(o_ref.shape[0], chunk)


def _scale_cast_tiles_kernel(w_ref, s_ref, o_ref):
    def chunk(rows):
        o_ref[rows, :] = (w_ref[rows, :] * s_ref[...]).astype(BF16)
    _for_row_chunks(o_ref.shape[0], chunk)


def _tile_cols(w, layer, tn, col_scale=None):
    _, k, n = w.shape
    tn = min(tn, n)
    w_spec = pl.BlockSpec((None, k, tn), lambda j: (layer, 0, j))
    common = dict(
        grid=(n // tn,),
        out_specs=pl.BlockSpec((None, k, tn), lambda j: (j, 0, 0)),
        out_shape=jax.ShapeDtypeStruct((n // tn, k, tn), BF16),
        compiler_params=_params(1),
        name="weight_tiles",
    )
    if col_scale is None:
        return pl.pallas_call(_cast_tiles_kernel, in_specs=[w_spec], **common)(w)
    s_spec = pl.BlockSpec((1, tn), lambda j: (0, j))
    return pl.pallas_call(_scale_cast_tiles_kernel, in_specs=[w_spec, s_spec], **common)(w, col_scale.reshape(1, n))


def _norm_matmul_kernel(x_ref, g_ref, w_ref, o_ref, a_ref, *, silu_tiles):
    j = pl.program_id(1)

    @pl.when(j == 0)
    def _():
        def chunk(rows):
            a_ref[rows, :] = _rmsnorm_rows(x_ref[rows, :], g_ref[...]).astype(BF16)
        _for_row_chunks(a_ref.shape[0], chunk)

    acc = _dot(a_ref[...], w_ref[...])
    if silu_tiles is None:
        o_ref[...] = acc.astype(o_ref.dtype)
    else:
        gated = jnp.logical_and(j >= silu_tiles[0], j < silu_tiles[1])

        @pl.when(gated)
        def _():
            o_ref[...] = _silu(acc).astype(o_ref.dtype)

        @pl.when(jnp.logical_not(gated))
        def _():
            o_ref[...] = acc.astype(o_ref.dtype)


MATMUL_TN = 1024


def _norm_matmul(x, g, w_tiles, out_dtype, silu_cols=None, tm=1024):
    m, k = x.shape
    tn = w_tiles.shape[2]
    n = w_tiles.shape[0] * tn
    tm = min(tm, m)
    silu_tiles = None if silu_cols is None else (silu_cols[0] // tn, silu_cols[1] // tn)
    return pl.pallas_call(
        functools.partial(_norm_matmul_kernel, silu_tiles=silu_tiles),
        grid=(m // tm, n // tn),
        in_specs=[
            pl.BlockSpec((tm, k), lambda i, j: (i, 0)),
            pl.BlockSpec((1, k), lambda i, j: (0, 0)),
            pl.BlockSpec((None, k, tn), lambda i, j: (j, 0, 0)),
        ],
        out_specs=pl.BlockSpec((tm, tn), lambda i, j: (i, j)),
        out_shape=jax.ShapeDtypeStruct((m, n), out_dtype),
        scratch_shapes=[pltpu.VMEM((tm, k), BF16)],
        compiler_params=_params(2),
        name="norm_matmul",
    )(x, g.reshape(1, k), w_tiles)


def _out_ple_kernel(a_ref, wo_ref, x_ref, g_ref, wg_ref, p_ref, wp_ref, gf_ref, o_ref, x1_ref, hn_ref, ss_ref, *, nt, final):
    j = pl.program_id(1)
    tn = x_ref.shape[1]
    n = nt * tn

    @pl.when(j == 0)
    def _():
        ss_ref[...] = jnp.zeros(ss_ref.shape, F32)

    @pl.when(j < nt)
    def _():
        x1 = x_ref[...] + _dot(a_ref[...], wo_ref[j])
        x1_ref[j] = x1
        hn_ref[j] = (x1 * g_ref[j]).astype(BF16)
        ss_ref[0] += jnp.sum(x1 * x1, axis=-1, keepdims=True)

    @pl.when(j >= nt)
    def _():
        t = j - nt
        emb = _dot(p_ref[...].astype(BF16), wp_ref[t])
        acc = _dot(hn_ref[0], wg_ref[t, :tn, :])
        for kt in range(1, nt):
            acc = acc + _dot(hn_ref[kt], wg_ref[t, kt * tn:(kt + 1) * tn, :])
        x2 = x1_ref[t] + emb * jax.nn.sigmoid(acc * lax.rsqrt(ss_ref[0] / n + EPS))
        if final:
            x1_ref[t] = x2
            ss_ref[1] += jnp.sum(x2 * x2, axis=-1, keepdims=True)
        else:
            o_ref[...] = x2

    if final:
        @pl.when(j == 2 * nt - 1)
        def _():
            inv = lax.rsqrt(ss_ref[1] / n + EPS)
            for t in range(nt):
                o_ref[:, t * tn:(t + 1) * tn] = x1_ref[t] * inv * gf_ref[t]


def _out_ple(a, w_out, x, g, w_gate, p, w_proj, layer, final_g=None, tm=512, tn=512):
    m, k = a.shape
    n = w_out.shape[1]
    pd = p.shape[2]
    tm, tn = min(tm, m), min(tn, n)
    nt = n // tn
    final = final_g is not None
    resident = lambda rows: pl.BlockSpec((nt, rows, tn), lambda i, j: (0, 0, 0), pipeline_mode=pl.Buffered(1))
    row_vec = pl.BlockSpec((nt, 1, tn), lambda i, j: (0, 0, 0))
    if final:
        out_spec = pl.BlockSpec((tm, n), lambda i, j: (i, 0))
    else:
        out_spec = pl.BlockSpec((tm, tn), lambda i, j: (i, jnp.maximum(j - nt, 0)))
    return pl.pallas_call(
        functools.partial(_out_ple_kernel, nt=nt, final=final),
        grid=(m // tm, 2 * nt),
        in_specs=[
            pl.BlockSpec((tm, k), lambda i, j: (i, 0)),
            resident(k),
            pl.BlockSpec((tm, tn), lambda i, j: (i, jnp.minimum(j, nt - 1))),
            row_vec,
            resident(n),
            pl.BlockSpec((None, tm, pd), lambda i, j: (layer, i, 0)),
            resident(pd),
            row_vec,
        ],
        out_specs=out_spec,
        out_shape=jax.ShapeDtypeStruct((m, n), F32),
        scratch_shapes=[pltpu.VMEM((nt, tm, tn), F32), pltpu.VMEM((nt, tm, tn), BF16), pltpu.VMEM((2, tm, 1), F32)],
        compiler_params=_params(2),
        name="out_ple_final" if final else "out_ple",
    )(a, _tile_cols(w_out[None], 0, tn), x, g.reshape(nt, 1, tn), _tile_cols(w_gate, layer, tn), p,
      _tile_cols(w_proj, layer, tn), (final_g if final else g).reshape(nt, 1, tn))


def _t5_bucket(dist):
    max_exact = REL_BUCKETS // 2
    d = np.maximum(dist, 1).astype(np.float32)
    large = max_exact + (np.log(d / max_exact) / np.log(REL_MAX_DIST / max_exact) * (REL_BUCKETS - max_exact)).astype(np.int32)
    large = np.minimum(large, REL_BUCKETS - 1)
    return np.where(dist < max_exact, dist, large).astype(np.int32)


def _swa_bucket_table():
    qi = np.arange(SWA_BLOCK)[None, :]
    kj = np.arange(2 * SWA_BLOCK)[:, None]
    return _t5_bucket(np.clip(qi + SWA_BLOCK - kj, 0, None))


def _swa_kernel(sink_ref, rel_ref, bucket_ref, q_ref, sg_ref, kp_ref, kc_ref, vp_ref, vc_ref, o_ref, bias_ref, qs_ref):
    n = pl.program_id(1)
    blk = SWA_BLOCK
    lane = lax.broadcasted_iota(jnp.int32, (1, LANES), 1)

    @pl.when(jnp.logical_and(pl.program_id(0) == 0, n == 0))
    def _():
        bucket = bucket_ref[...]
        kj = lax.broadcasted_iota(jnp.int32, (2 * blk, blk), 0)
        qi = lax.broadcasted_iota(jnp.int32, (2 * blk, blk), 1)
        band = jnp.logical_or(jnp.logical_and(kj < blk, kj > qi), jnp.logical_and(kj >= blk, kj - blk <= qi))

        def per_head(head, carry):
            acc = jnp.zeros((2 * blk, blk), F32)
            for b in range(REL_BUCKETS):
                acc = jnp.where(bucket == b, rel_ref[b, head], acc)
            bias_ref[head] = jnp.where(band, acc, NEG)
            return carry
        lax.fori_loop(0, SWA_HEADS, per_head, 0)

    for kvh in range(SWA_KV_HEADS):
        cols = slice(kvh * LANES, (kvh + 1) * LANES)
        heads = range(kvh * SWA_GROUP, (kvh + 1) * SWA_GROUP)
        for g, head in enumerate(heads):
            q_pair = q_ref[:, head // 2 * LANES:(head // 2 + 1) * LANES]
            qs_ref[g * blk:(g + 1) * blk, :] = jnp.where((lane // HEAD_DIM) == head % 2, q_pair, jnp.zeros_like(q_pair))
        keys = jnp.concatenate([kp_ref[:, cols], kc_ref[:, cols]], axis=0)
        vals = jnp.concatenate([vp_ref[:, cols], vc_ref[:, cols]], axis=0)
        s = _dot_nt(keys, qs_ref[...]) + jnp.concatenate([bias_ref[head] for head in heads], axis=1)
        s_prev = jnp.where(n > 0, s[:blk], NEG)
        s_cur = s[blk:]
        sink = jnp.concatenate([jnp.full((1, blk), sink_ref[head], F32) for head in heads], axis=1)
        m = jnp.maximum(jnp.maximum(s_prev, s_cur).max(axis=0, keepdims=True), sink)
        e_prev, e_cur = jnp.exp(s_prev - m), jnp.exp(s_cur - m)
        den = e_prev.sum(axis=0, keepdims=True) + e_cur.sum(axis=0, keepdims=True) + jnp.exp(sink - m)
        e = jnp.concatenate([e_prev, e_cur], axis=0).astype(BF16)
        out = _dot(vals.astype(F32).T.astype(BF16), e) / den
        for pair in range(SWA_GROUP // 2):
            head = kvh * SWA_GROUP + 2 * pair
            even = out[:HEAD_DIM, 2 * pair * blk:(2 * pair + 1) * blk]
            odd = out[HEAD_DIM:, (2 * pair + 1) * blk:(2 * pair + 2) * blk]
            cols = slice(head // 2 * LANES, (head // 2 + 1) * LANES)
            o_ref[:, cols] = (jnp.concatenate([even, odd], axis=0).T * sg_ref[:, cols].astype(F32)).astype(o_ref.dtype)


def _swa_attention(proj, sinks, rel_bias, bsz, seq):
    nb = seq // SWA_BLOCK
    width = SWA_HEADS * HEAD_DIM
    kvw = SWA_KV_HEADS * LANES
    k_blk = 2 * width // kvw
    cur = lambda b, n: b * nb + n
    prev = lambda b, n: b * nb + jnp.maximum(n - 1, 0)
    smem = pl.BlockSpec(memory_space=pltpu.SMEM)
    return pl.pallas_call(
        _swa_kernel,
        grid=(bsz, nb),
        in_specs=[
            smem, smem,
            pl.BlockSpec((2 * SWA_BLOCK, SWA_BLOCK), lambda b, n: (0, 0)),
            pl.BlockSpec((SWA_BLOCK, width), lambda b, n: (cur(b, n), 0)),
            pl.BlockSpec((SWA_BLOCK, width), lambda b, n: (cur(b, n), 1)),
            pl.BlockSpec((SWA_BLOCK, kvw), lambda b, n: (prev(b, n), k_blk)),
            pl.BlockSpec((SWA_BLOCK, kvw), lambda b, n: (cur(b, n), k_blk)),
            pl.BlockSpec((SWA_BLOCK, kvw), lambda b, n: (prev(b, n), k_blk + 1)),
            pl.BlockSpec((SWA_BLOCK, kvw), lambda b, n: (cur(b, n), k_blk + 1)),
        ],
        out_specs=pl.BlockSpec((SWA_BLOCK, width), lambda b, n: (cur(b, n), 0)),
        out_shape=jax.ShapeDtypeStruct((bsz * seq, width), BF16),
        scratch_shapes=[
            pltpu.VMEM((SWA_HEADS, 2 * SWA_BLOCK, SWA_BLOCK), F32),
            pltpu.VMEM((SWA_GROUP * SWA_BLOCK, LANES), BF16),
        ],
        compiler_params=_params(2),
        name="swa_attention",
    )(sinks.astype(F32), rel_bias.astype(F32), jnp.asarray(_swa_bucket_table()), proj, proj, proj, proj, proj, proj)


def _dup_heads(w, n_heads):
    k = w.shape[0]
    w = w.reshape(k, n_heads, 1, HEAD_DIM)
    return jnp.broadcast_to(w, (k, n_heads, 2, HEAD_DIM)).reshape(k, n_heads * 2 * HEAD_DIM)


def _swa_mixer(x, g, w_in, sinks, rel_bias, bsz, seq):
    width = SWA_HEADS * HEAD_DIM
    kvw = SWA_KV_HEADS * HEAD_DIM
    w_q, w_k, w_v, w_g = (w_in[:, :width], w_in[:, width:width + kvw],
                          w_in[:, width + kvw:width + 2 * kvw], w_in[:, width + 2 * kvw:])
    w = jnp.concatenate([w_q * (HEAD_DIM ** -0.5), w_g, _dup_heads(w_k, SWA_KV_HEADS), _dup_heads(w_v, SWA_KV_HEADS)], axis=1)
    proj = _norm_matmul(x, g, _tile_cols_xla(w, MATMUL_TN), BF16, silu_cols=(width, 2 * width))
    return _swa_attention(proj, sinks, rel_bias, bsz, seq)


HALO = 8


def _conv_kernel(x_ref, g_ref, wb_ref, wc_ref, wu_ref, wg_ref, ck_ref, o_ref, a_ref, z_ref, carry_ref, *, tiles_per_seq):
    i, j = pl.program_id(0), pl.program_id(1)
    tm = a_ref.shape[0]

    @pl.when(j == 0)
    def _():
        def chunk(rows):
            a_ref[rows, :] = _rmsnorm_rows(x_ref[rows, :], g_ref[...]).astype(BF16)
        _for_row_chunks(tm, chunk)

    a = a_ref[...]
    z = _dot(a, wc_ref[...]) * _dot(a, wu_ref[...])
    first = (i % tiles_per_seq) == 0

    @pl.when(first)
    def _():
        z_ref[:HALO, :] = jnp.zeros((HALO, z_ref.shape[1]), F32)

    @pl.when(jnp.logical_not(first))
    def _():
        z_ref[:HALO, :] = carry_ref[j]

    z_ref[HALO:, :] = z
    carry_ref[j] = z[tm - HALO:, :]
    conv = z_ref[HALO - 2:HALO - 2 + tm, :] * ck_ref[0:1, :]
    conv = conv + z_ref[HALO - 1:HALO - 1 + tm, :] * ck_ref[1:2, :]
    conv = conv + z * ck_ref[2:3, :]
    y = _dot(a, wb_ref[...]) * conv
    o_ref[...] = (y * _silu(_dot(a, wg_ref[...]))).astype(o_ref.dtype)


def _conv_mixer(x, g, w_in, layer, conv_kernel, seq, tm=512, tn=512):
    m, k = x.shape
    width = w_in.shape[2] // 4
    tm, tn = min(tm, seq), min(tn, width)
    nt = width // tn
    w_spec = lambda q: pl.BlockSpec((None, k, tn), lambda i, j: (q * nt + j, 0, 0))
    w_in = _tile_cols(w_in, layer, tn)
    return pl.pallas_call(
        functools.partial(_conv_kernel, tiles_per_seq=seq // tm),
        grid=(m // tm, nt),
        in_specs=[
            pl.BlockSpec((tm, k), lambda i, j: (i, 0)),
            pl.BlockSpec((1, k), lambda i, j: (0, 0)),
            w_spec(0), w_spec(1), w_spec(2), w_spec(3),
            pl.BlockSpec((CONV_TAPS, tn), lambda i, j: (0, j)),
        ],
        out_specs=pl.BlockSpec((tm, tn), lambda i, j: (i, j)),
        out_shape=jax.ShapeDtypeStruct((m, width), BF16),
        scratch_shapes=[
            pltpu.VMEM((tm, k), BF16),
            pltpu.VMEM((HALO + tm, tn), F32),
            pltpu.VMEM((nt, HALO, tn), F32),
        ],
        compiler_params=_params(2),
        name="conv_mixer",
    )(x, g.reshape(1, k), w_in, w_in, w_in, w_in, conv_kernel.astype(F32))


def _ssm_operators(lam_re, lam_im, log_dt, b_re, b_im, c_re, c_im, d_skip):
    n_groups = lam_re.shape[0]
    n_oct = n_groups // GROUPS_PER_TILE
    L, C, N = SSM_CHUNK, SSM_GROUP, SSM_STATE
    dt = jnp.exp(log_dt.astype(F32))[:, None]
    lr, li = lam_re.astype(F32), lam_im.astype(F32)
    mag = jnp.exp(lr * dt)
    ab_re, ab_im = mag * jnp.cos(li * dt), mag * jnp.sin(li * dt)
    den = lr * lr + li * li
    nr = ab_re - 1.0
    coef_re = ((nr * lr + ab_im * li) / den)[..., None]
    coef_im = ((ab_im * lr - nr * li) / den)[..., None]
    br, bi = b_re.astype(F32), b_im.astype(F32)
    bb_re = coef_re * br - coef_im * bi
    bb_im = coef_re * bi + coef_im * br
    cr, ci = c_re.astype(F32), c_im.astype(F32)
    p_re, p_im = [jnp.ones_like(ab_re)], [jnp.zeros_like(ab_im)]
    for _ in range(L):
        p_re, p_im = (p_re + [p_re[-1] * ab_re - p_im[-1] * ab_im], p_im + [p_re[-1] * ab_im + p_im[-1] * ab_re])
    p_re, p_im = jnp.stack(p_re), jnp.stack(p_im)
    w_re = cr[None] * p_re[:L, :, None, :] - ci[None] * p_im[:L, :, None, :]
    w_im = cr[None] * p_im[:L, :, None, :] + ci[None] * p_re[:L, :, None, :]
    bt_re, bt_im = bb_re.transpose(0, 2, 1), bb_im.transpose(0, 2, 1)
    lag_k = jnp.sum(w_re[:, :, None] * bt_re[None, :, :, None] - w_im[:, :, None] * bt_im[None, :, :, None], axis=-1)
    q_re, q_im = p_re[:L][::-1], p_im[:L][::-1]
    bc_re = q_re[:, :, None, :] * bt_re[None] - q_im[:, :, None, :] * bt_im[None]
    bc_im = q_re[:, :, None, :] * bt_im[None] + q_im[:, :, None, :] * bt_re[None]
    o_re = cr[None] * p_re[1:, :, None, :] - ci[None] * p_im[1:, :, None, :]
    o_im = cr[None] * p_im[1:, :, None, :] + ci[None] * p_re[1:, :, None, :]
    octs = lambda t: t.reshape(L, n_oct, GROUPS_PER_TILE, C, -1)
    to_lag = lambda t: octs(t).transpose(1, 0, 3, 2, 4).reshape(n_oct, L, C, LANES)
    to_rows = lambda t: octs(t).transpose(1, 0, 3, 2, 4).reshape(n_oct, L * C, GROUPS_PER_TILE * N)
    to_cols = lambda t: octs(t).transpose(1, 2, 4, 0, 3).reshape(n_oct, GROUPS_PER_TILE * N, L * C)
    oct_vec = lambda t: t.reshape(n_oct, 1, -1)
    return dict(
        lag=to_lag(lag_k).astype(BF16),
        bc_re=to_rows(bc_re).astype(BF16), bc_im=to_rows(bc_im).astype(BF16),
        oc_re=to_cols(o_re).astype(BF16), oc_im_neg=to_cols(-o_im).astype(BF16),
        al_re=oct_vec(p_re[L]), al_im=oct_vec(p_im[L]), d=oct_vec(d_skip.astype(F32)),
    )


def _ssm_expanders():
    L, C, N, G8 = SSM_CHUNK, SSM_GROUP, SSM_STATE, GROUPS_PER_TILE
    wide = np.arange(L * LANES)
    w_step, w_group, w_chan = wide // LANES, (wide // C) % G8, wide % C
    small = np.arange(L * C)
    s_step, s_chan = small // C, small % C
    state_group = np.arange(G8 * N) // N
    spread = (w_step[:, None] == s_step[None, :]) & (w_chan[:, None] == s_chan[None, :])
    lane = np.arange(LANES)
    as_bf16 = lambda a: jnp.asarray(a.astype(np.float32), dtype=BF16)
    return dict(
        spread_rows=as_bf16(spread), spread_cols=as_bf16(spread.T),
        keep_rows=as_bf16(w_group[:, None] == state_group[None, :]),
        keep_cols=as_bf16(state_group[:, None] == w_group[None, :]),
        spread_lag=as_bf16((lane % C)[:, None] == np.arange(C)[None, :]),
        keep_lag=jnp.asarray(((lane // C)[:, None] == (lane // C)[None, :]).astype(np.float32)),
    )


def _ssm_kernel(u_ref, lag_ref, bxr_ref, bxi_ref, cxr_ref, cxi_ref, alr_ref, ali_ref, d_ref,
                sr_ref, sc_ref, kr_ref, kc_ref, sl_ref, kl_ref, o_ref,
                ub_ref, bcr_ref, bci_ref, ocr_ref, oci_ref, zr_ref, zi_ref, hr_ref, hi_ref, *, bsz):
    L = SSM_CHUNK
    rows = u_ref.shape[0] // L
    chunks = rows // bsz
    tile = 2 * LANES
    step_rows = lambda r: pl.ds(r, rows, stride=L)
    for r in range(L):
        ub_ref[:, r * LANES:(r + 1) * LANES] = u_ref[step_rows(r), :].astype(BF16)
    bcr_ref[...] = (_dot(sr_ref[...], bxr_ref[...]) * kr_ref[...]).astype(BF16)
    bci_ref[...] = (_dot(sr_ref[...], bxi_ref[...]) * kr_ref[...]).astype(BF16)
    zr_ref[...] = _dot(ub_ref[...], bcr_ref[...])
    zi_ref[...] = _dot(ub_ref[...], bci_ref[...])
    a_re, a_im = alr_ref[...], ali_ref[...]

    def step(k, carry):
        new = []
        for b in range(bsz):
            h_re, h_im = carry[2 * b], carry[2 * b + 1]
            row = pl.ds(b * chunks + k, 1)
            hr_ref[row, :] = h_re
            hi_ref[row, :] = h_im
            new.append(a_re * h_re - a_im * h_im + zr_ref[row, :])
            new.append(a_re * h_im + a_im * h_re + zi_ref[row, :])
        return tuple(new)

    zero = jnp.zeros((1, a_re.shape[1]), F32)
    lax.fori_loop(0, chunks, step, (zero,) * (2 * bsz))

    ocr_ref[...] = (_dot(cxr_ref[...], sc_ref[...]) * kc_ref[...]).astype(BF16)
    oci_ref[...] = (_dot(cxi_ref[...], sc_ref[...]) * kc_ref[...]).astype(BF16)
    lag = [(_dot(sl_ref[...], lag_ref[d]) * kl_ref[...]).astype(BF16) for d in range(L)]
    lag_tile = []
    for dd in range(L // 2):
        below = lag[2 * dd - 1] if dd > 0 else jnp.zeros((LANES, LANES), BF16)
        lag_tile.append(jnp.concatenate([jnp.concatenate([lag[2 * dd], lag[2 * dd + 1]], axis=1),
                                         jnp.concatenate([below, lag[2 * dd]], axis=1)], axis=0))
    hb_re, hb_im = hr_ref[...].astype(BF16), hi_ref[...].astype(BF16)
    for t2 in range(L // 2):
        cols = slice(t2 * tile, (t2 + 1) * tile)
        y = _dot(hb_re, ocr_ref[:, cols]) + _dot(hb_im, oci_ref[:, cols])
        for r2 in range(t2 + 1):
            y = y + _dot(ub_ref[:, r2 * tile:(r2 + 1) * tile], lag_tile[t2 - r2])
        for half in range(2):
            t = 2 * t2 + half
            o_ref[step_rows(t), :] = jax.nn.gelu(y[:, half * LANES:(half + 1) * LANES] + d_ref[...] * u_ref[step_rows(t), :])


def _ssm_core(proj, ops, bsz, seq):
    m = proj.shape[0]
    width = proj.shape[1] // 2
    L, C = SSM_CHUNK, SSM_GROUP
    n_oct = width // LANES
    rows = m // L
    ow = L * LANES
    n_state = GROUPS_PER_TILE * SSM_STATE
    ex = _ssm_expanders()
    per_oct = lambda *shape: pl.BlockSpec((None,) + shape, lambda p: (p,) + (0,) * len(shape))
    const = lambda a: pl.BlockSpec(a.shape, lambda p: (0,) * a.ndim, pipeline_mode=pl.Buffered(1))
    consts = [ex['spread_rows'], ex['spread_cols'], ex['keep_rows'], ex['keep_cols'], ex['spread_lag'], ex['keep_lag']]
    return pl.pallas_call(
        functools.partial(_ssm_kernel, bsz=bsz),
        grid=(n_oct,),
        in_specs=[
            pl.BlockSpec((m, LANES), lambda p: (0, p)),
            per_oct(L, C, LANES), per_oct(L * C, n_state), per_oct(L * C, n_state),
            per_oct(n_state, L * C), per_oct(n_state, L * C), per_oct(1, n_state), per_oct(1, n_state), per_oct(1, LANES),
        ] + [const(a) for a in consts],
        out_specs=pl.BlockSpec((m, LANES), lambda p: (0, p)),
        out_shape=jax.ShapeDtypeStruct((m, width), F32),
        scratch_shapes=[pltpu.VMEM((rows, ow), BF16)] + [pltpu.VMEM((ow, n_state), BF16)] * 2
        + [pltpu.VMEM((n_state, ow), BF16)] * 2 + [pltpu.VMEM((rows, n_state), F32)] * 4,
        compiler_params=_params(1),
        name="ssm_core",
    )(proj, ops['lag'], ops['bc_re'], ops['bc_im'], ops['oc_re'], ops['oc_im_neg'], ops['al_re'], ops['al_im'], ops['d'], *consts)


def _glu_kernel(y_ref, wa_ref, wb_ref, ba_ref, bb_ref, gate_ref, o_ref, yb_ref):
    @pl.when(pl.program_id(1) == 0)
    def _():
        def chunk(rows):
            yb_ref[rows, :] = y_ref[rows, :].astype(BF16)
        _for_row_chunks(yb_ref.shape[0], chunk)

    y = yb_ref[...]
    ga = _dot(y, wa_ref[...]) + ba_ref[...]
    gb = _dot(y, wb_ref[...]) + bb_ref[...]
    o_ref[...] = ((ga * jax.nn.sigmoid(gb)) * _silu(gate_ref[...])).astype(o_ref.dtype)


def _glu(y, w_glu, layer, b_glu, proj, tm=512, tn=512):
    m, k = y.shape
    width = w_glu.shape[2] // 2
    tm, tn = min(tm, m), min(tn, width)
    nt = width // tn
    w_tiles = _tile_cols(w_glu, layer, tn)
    return pl.pallas_call(
        _glu_kernel,
        grid=(m // tm, nt),
        in_specs=[
            pl.BlockSpec((tm, k), lambda i, j: (i, 0)),
            pl.BlockSpec((None, k, tn), lambda i, j: (j, 0, 0)),
            pl.BlockSpec((None, k, tn), lambda i, j: (nt + j, 0, 0)),
            pl.BlockSpec((1, tn), lambda i, j: (0, j)),
            pl.BlockSpec((1, tn), lambda i, j: (0, nt + j)),
            pl.BlockSpec((tm, tn), lambda i, j: (i, nt + j)),
        ],
        out_specs=pl.BlockSpec((tm, tn), lambda i, j: (i, j)),
        out_shape=jax.ShapeDtypeStruct((m, width), BF16),
        scratch_shapes=[pltpu.VMEM((tm, k), BF16)],
        compiler_params=_params(2),
        name="ssm_glu",
    )(y, w_tiles, w_tiles, b_glu, b_glu, proj)


def _ssm_mixer(x, g, w_in, layer, lam_re, lam_im, log_dt, b_re, b_im, c_re, c_im, d_skip, w_glu, b_glu, bsz, seq):
    proj = _norm_matmul(x, g, _tile_cols(w_in, layer, MATMUL_TN), F32)
    ops = _ssm_operators(lam_re, lam_im, log_dt, b_re, b_im, c_re, c_im, d_skip)
    y = _ssm_core(proj, ops, bsz, seq)
    return _glu(y, w_glu, layer, b_glu.astype(F32).reshape(1, -1), proj)


CUM_BLOCK = 128


def _split3(x):
    x1 = x.astype(BF16)
    r1 = x - x1.astype(F32)
    x2 = r1.astype(BF16)
    x3 = (r1 - x2.astype(F32)).astype(BF16)
    return x1, x2, x3


def _forget_cumsum_kernel(z_ref, b_ref, o_ref):
    n_blocks = z_ref.shape[0] // CUM_BLOCK
    ri = lax.broadcasted_iota(jnp.int32, (CUM_BLOCK, CUM_BLOCK), 0)
    ci = lax.broadcasted_iota(jnp.int32, (CUM_BLOCK, CUM_BLOCK), 1)
    tri = jnp.where(ci <= ri, 1.0, 0.0).astype(BF16)

    def body(i, carry):
        rows = pl.ds(pl.multiple_of(i * CUM_BLOCK, CUM_BLOCK), CUM_BLOCK)
        z = z_ref[rows, :] + b_ref[...]
        log_f = jnp.minimum(z, 0.0) - jnp.log1p(jnp.exp(-jnp.abs(z)))
        x1, x2, x3 = _split3(log_f)
        c = _dot(tri, x1) + _dot(tri, x2) + _dot(tri, x3) + carry
        o_ref[rows, :] = c
        return c[CUM_BLOCK - 1:, :]

    lax.fori_loop(0, n_blocks, body, jnp.zeros((1, z_ref.shape[1]), F32))


def _forget_cumsum(z, b, bsz, seq):
    lanes = z.shape[1]
    return pl.pallas_call(
        _forget_cumsum_kernel,
        grid=(bsz,),
        in_specs=[pl.BlockSpec((seq, lanes), lambda i: (i, 0)), pl.BlockSpec((1, lanes), lambda i: (0, 0))],
        out_specs=pl.BlockSpec((seq, lanes), lambda i: (i, 0)),
        out_shape=jax.ShapeDtypeStruct(z.shape, F32),
        compiler_params=_params(1),
        name="forget_cumsum",
    )(z, b)


def _fox_kernel(q_ref, k_ref, v_ref, sg_ref, cq_ref, ck_ref, o_ref, ka_ref, vt_ref, *head_refs, tk):
    hp, qi = pl.program_id(1), pl.program_id(2)
    tq = q_ref.shape[0]
    qa_ref, s_ref, p_ref, acc_ref, pv_ref = (head_refs[i::5] for i in range(5))
    lane = lax.broadcasted_iota(jnp.int32, (1, LANES), 1)
    free = (HEAD_DIM, 0)
    n_pieces = 3

    def bias_lanes(c, hh, c_offset, one_offset):
        rr = lax.broadcasted_iota(jnp.int32, (n_pieces * LANES, LANES), 0)
        cc = lax.broadcasted_iota(jnp.int32, (n_pieces * LANES, LANES), 1)
        place = jnp.logical_and(rr % LANES == 2 * hp + hh, cc == free[hh] + c_offset + rr // LANES)
        placed = _dot(jnp.concatenate(_split3(c), axis=1), jnp.where(place, 1.0, 0.0).astype(BF16))
        first = free[hh] + one_offset
        return (placed + jnp.where(jnp.logical_and(lane >= first, lane < first + n_pieces), 1.0, 0.0)).astype(BF16)

    @pl.when(qi == 0)
    def _():
        def chunk(j, carry):
            rows = pl.ds(pl.multiple_of(j * tk, tk), tk)
            keys = k_ref[rows, :]
            neg_c = ck_ref[rows, :] * -LOG2E
            for hh in range(2):
                ka_ref[hh, j] = jnp.where((lane // HEAD_DIM) == hh, keys, bias_lanes(neg_c, hh, 0, n_pieces))
            vt_ref[j] = v_ref[rows, :].astype(F32).T.astype(BF16)
            return carry
        lax.fori_loop(0, k_ref.shape[0] // tk, chunk, 0)

    cq = cq_ref[...] * LOG2E
    for hh in range(2):
        qa = jnp.where((lane // HEAD_DIM) == hh, q_ref[...], bias_lanes(cq, hh, n_pieces, 0))
        qa_ref[hh][...] = qa.astype(F32).T.astype(BF16)

    def scores(hh, j):
        s_ref[hh][...] = _dot(ka_ref[hh, j], qa_ref[hh][...])

    rows8 = 8
    rows16 = 16

    def absorb(hh, j, m, l, diagonal):
        def strip(r, n):
            blk = s_ref[hh][r:r + n, :]
            if diagonal:
                kpos = lax.broadcasted_iota(jnp.int32, (n, tq), 0)
                qpos = lax.broadcasted_iota(jnp.int32, (n, tq), 1)
                blk = jnp.where(kpos + r <= qpos, blk, NEG)
            return blk
        top = strip(0, rows8)
        for r in range(rows8, tk, rows8):
            top = jnp.maximum(top, strip(r, rows8))
        m_new = jnp.maximum(m, top.max(axis=0, keepdims=True))
        alpha = jnp.exp2(m - m_new)
        m_rows = jnp.broadcast_to(m_new, (rows16, tq))
        total = None
        for r in range(0, tk, rows16):
            p = jnp.exp2(strip(r, rows16) - m_rows)
            total = p if total is None else total + p
            p_ref[hh][r:r + rows16, :] = p.astype(BF16)
        acc_ref[hh][...] = alpha * (acc_ref[hh][...] + pv_ref[hh][...])
        return m_new, alpha * l + total.sum(axis=0, keepdims=True)

    def values(hh, j):
        pv_ref[hh][...] = _dot(vt_ref[j], p_ref[hh][...])

    def step(j, carry, diagonal):
        m0, l0, m1, l1 = carry
        values(1, jnp.maximum(j - 1, 0))
        scores(1, j)
        m0, l0 = absorb(0, j, m0, l0, diagonal)
        values(0, j)
        if not diagonal:
            scores(0, j + 1)
        m1, l1 = absorb(1, j, m1, l1, diagonal)
        return m0, l0, m1, l1

    for hh in range(2):
        acc_ref[hh][...] = jnp.zeros(acc_ref[hh].shape, F32)
        pv_ref[hh][...] = jnp.zeros(pv_ref[hh].shape, F32)
    p_ref[1][...] = jnp.zeros(p_ref[1].shape, BF16)
    scores(0, 0)
    init = (jnp.full((1, tq), NEG, F32), jnp.zeros((1, tq), F32)) * 2
    carry = lax.fori_loop(0, qi, lambda j, c: step(j, c, False), init)
    _, l0, _, l1 = step(qi, carry, True)
    values(1, qi)
    row = lax.broadcasted_iota(jnp.int32, (LANES, 1), 0)
    out = jnp.where(row < HEAD_DIM, (acc_ref[0][...] + pv_ref[0][...]) / l0, (acc_ref[1][...] + pv_ref[1][...]) / l1)
    o_ref[...] = (out.T * sg_ref[...].astype(F32)).astype(o_ref.dtype)


def _fox_attention(proj, csum, bsz, seq, t=512):
    width = FOX_HEADS * HEAD_DIM
    n_pairs = width // LANES
    t = min(t, seq)
    nq = seq // t
    return pl.pallas_call(
        functools.partial(_fox_kernel, tk=t),
        grid=(bsz, n_pairs, nq),
        in_specs=[
            pl.BlockSpec((t, LANES), lambda b, h, i: (b * nq + i, h)),
            pl.BlockSpec((seq, LANES), lambda b, h, i: (b, n_pairs + h)),
            pl.BlockSpec((seq, LANES), lambda b, h, i: (b, 2 * n_pairs + h)),
            pl.BlockSpec((t, LANES), lambda b, h, i: (b * nq + i, 3 * n_pairs + h)),
            pl.BlockSpec((t, LANES), lambda b, h, i: (b * nq + i, 0)),
            pl.BlockSpec((seq, LANES), lambda b, h, i: (b, 0)),
        ],
        out_specs=pl.BlockSpec((t, LANES), lambda b, h, i: (b * nq + i, h)),
        out_shape=jax.ShapeDtypeStruct((bsz * seq, width), BF16),
        scratch_shapes=[
            pltpu.VMEM((2, nq, t, LANES), BF16), pltpu.VMEM((nq, LANES, t), BF16),
        ] + 2 * [pltpu.VMEM((LANES, t), BF16), pltpu.VMEM((t, t), F32), pltpu.VMEM((t, t), BF16),
                 pltpu.VMEM((LANES, t), F32), pltpu.VMEM((LANES, t), F32)],
        compiler_params=_params(3),
        name="fox_attention",
    )(proj, proj, proj, proj, csum, csum)


def _fox_mixer(x, g, w_in, layer, w_fg, b_fg, bsz, seq, t=512):
    width = FOX_HEADS * HEAD_DIM
    col_scale = jnp.where(jnp.arange(4 * width) < width, LOG2E * HEAD_DIM ** -0.5, 1.0).astype(F32)
    proj = _norm_matmul(x, g, _tile_cols(w_in, layer, MATMUL_TN, col_scale), BF16, silu_cols=(3 * width, 4 * width))
    pad = LANES - FOX_HEADS
    z = _norm_matmul(x, g, _tile_cols_xla(jnp.pad(w_fg, ((0, 0), (0, pad))), LANES), F32)
    csum = _forget_cumsum(z, jnp.pad(b_fg.astype(F32), (0, pad)).reshape(1, LANES), bsz, seq)
    return _fox_attention(proj, csum, bsz, seq, t)


def kernel(x, p, norm_g, final_g, rel_bias, swa_w_in, swa_w_out, swa_sinks, conv_w_in, conv_kernel, conv_w_out, ssm_w_in, ssm_lam_re, ssm_lam_im, ssm_log_dt, ssm_b_re, ssm_b_im, ssm_c_re, ssm_c_im, ssm_d, ssm_w_glu, ssm_b_glu, ssm_w_out, fox_w_in, fox_w_fg, fox_b_fg, fox_w_out, ple_proj, ple_norm, ple_gate):
    bsz, seq, d_model = x.shape
    depth = p.shape[0]
    h = x.astype(F32).reshape(bsz * seq, d_model)
    for i in range(depth):
        mixer, j = i % N_MIXERS, i // N_MIXERS
        if mixer == 0:
            a, w_out = _swa_mixer(h, norm_g[i], swa_w_in[j], swa_sinks[j], rel_bias, bsz, seq), swa_w_out[j]
        elif mixer == 1:
            a, w_out = _conv_mixer(h, norm_g[i], conv_w_in, j, conv_kernel[j], seq), conv_w_out[j]
        elif mixer == 2:
            a = _ssm_mixer(h, norm_g[i], ssm_w_in, j, ssm_lam_re[j], ssm_lam_im[j], ssm_log_dt[j], ssm_b_re[j], ssm_b_im[j],
                           ssm_c_re[j], ssm_c_im[j], ssm_d[j], ssm_w_glu, ssm_b_glu[j], bsz, seq)
            w_out = ssm_w_out[j]
        else:
            a, w_out = _fox_mixer(h, norm_g[i], fox_w_in, j, fox_w_fg[j], fox_b_fg[j], bsz, seq), fox_w_out[j]
        h = _out_ple(a, w_out, h, ple_norm[i], ple_gate, p.reshape(depth, bsz * seq, -1), ple_proj, i,
                     final_g=final_g if i == depth - 1 else None)
    return h.reshape(bsz, seq, d_model).astype(x.dtype)
```

```python
import functools
import math

import numpy as np
import jax
import jax.numpy as jnp
from jax import lax
from jax.experimental import pallas as pl
from jax.experimental.pallas import tpu as pltpu

F32 = jnp.float32
BF16 = jnp.bfloat16

EPS = 1e-6
N_MIXERS = 4
PLE_DIM = 256

SWA_HEADS = 32
SWA_KV_HEADS = 4
SWA_GROUP = SWA_HEADS // SWA_KV_HEADS
HEAD_DIM = 64
SWA_BLOCK = 128
WINDOW = 128
REL_BUCKETS = 32
REL_MAX_DIST = 128

CONV_TAPS = 3

SSM_GROUP = 16
SSM_STATE = 64
SSM_CHUNK = 16

FOX_HEADS = 32

LANES = 128
GROUPS_PER_TILE = LANES // SSM_GROUP
VMEM_LIMIT = 48 * 1024 * 1024
VMEM_LIMIT_BIG = 60 * 1024 * 1024

NEG = float(jnp.finfo(jnp.float32).min)
LOG2E = math.log2(math.e)


def _params(n_axes, vmem_limit=VMEM_LIMIT):
    return pltpu.CompilerParams(dimension_semantics=("arbitrary",) * n_axes, vmem_limit_bytes=vmem_limit)


def _dot(a, b):
    return jnp.dot(a, b, preferred_element_type=F32)


def _dot_nt(a, b):
    return lax.dot_general(a, b, (((1,), (1,)), ((), ())), preferred_element_type=F32)


def _rmsnorm_rows(x, g):
    return x * lax.rsqrt(jnp.mean(x * x, axis=-1, keepdims=True) + EPS) * g


def _silu(x):
    return x * jax.nn.sigmoid(x)


ROW_CHUNK = 64


def _for_row_chunks(n_rows, fn):
    def body(c, carry):
        fn(pl.ds(pl.multiple_of(c * ROW_CHUNK, ROW_CHUNK), ROW_CHUNK))
        return carry
    lax.fori_loop(0, n_rows // ROW_CHUNK, body, 0)


def _tile_cols_xla(w, tn):
    k, n = w.shape
    return w.astype(BF16).reshape(k, n // tn, tn).transpose(1, 0, 2)


def _cast_tiles_kernel(w_ref, o_ref):
    def chunk(rows):
        o_ref[rows, :] = w_ref[rows, :].astype(BF16)
    _for_row_chunks(o_ref.shape[0], chunk)


def _scale_cast_tiles_kernel(w_ref, s_ref, o_ref):
    def chunk(rows):
        o_ref[rows, :] = (w_ref[rows, :] * s_ref[...]).astype(BF16)
    _for_row_chunks(o_ref.shape[0], chunk)


def _tile_cols(w, layer, tn, col_scale=None):
    _, k, n = w.shape
    tn = min(tn, n)
    w_spec = pl.BlockSpec((None, k, tn), lambda j: (layer, 0, j))
    common = dict(
        grid=(n // tn,),
        out_specs=pl.BlockSpec((None, k, tn), lambda j: (j, 0, 0)),
        out_shape=jax.ShapeDtypeStruct((n // tn, k, tn), BF16),
        compiler_params=_params(1),
        name="weight_tiles",
    )
    if col_scale is None:
        return pl.pallas_call(_cast_tiles_kernel, in_specs=[w_spec], **common)(w)
    s_spec = pl.BlockSpec((1, tn), lambda j: (0, j))
    return pl.pallas_call(_scale_cast_tiles_kernel, in_specs=[w_spec, s_spec], **common)(w, col_scale.reshape(1, n))


def _norm_matmul_kernel(x_ref, g_ref, w_ref, o_ref, a_ref, *, silu_tiles):
    j = pl.program_id(1)

    @pl.when(j == 0)
    def _():
        def chunk(rows):
            a_ref[rows, :] = _rmsnorm_rows(x_ref[rows, :], g_ref[...]).astype(BF16)
        _for_row_chunks(a_ref.shape[0], chunk)

    acc = _dot(a_ref[...], w_ref[...])
    if silu_tiles is None:
        o_ref[...] = acc.astype(o_ref.dtype)
    else:
        gated = jnp.logical_and(j >= silu_tiles[0], j < silu_tiles[1])

        @pl.when(gated)
        def _():
            o_ref[...] = _silu(acc).astype(o_ref.dtype)

        @pl.when(jnp.logical_not(gated))
        def _():
            o_ref[...] = acc.astype(o_ref.dtype)


MATMUL_TN = 1024


def _norm_matmul(x, g, w_tiles, out_dtype, silu_cols=None, tm=1024):
    m, k = x.shape
    tn = w_tiles.shape[2]
    n = w_tiles.shape[0] * tn
    tm = min(tm, m)
    silu_tiles = None if silu_cols is None else (silu_cols[0] // tn, silu_cols[1] // tn)
    return pl.pallas_call(
        functools.partial(_norm_matmul_kernel, silu_tiles=silu_tiles),
        grid=(m // tm, n // tn),
        in_specs=[
            pl.BlockSpec((tm, k), lambda i, j: (i, 0)),
            pl.BlockSpec((1, k), lambda i, j: (0, 0)),
            pl.BlockSpec((None, k, tn), lambda i, j: (j, 0, 0)),
        ],
        out_specs=pl.BlockSpec((tm, tn), lambda i, j: (i, j)),
        out_shape=jax.ShapeDtypeStruct((m, n), out_dtype),
        scratch_shapes=[pltpu.VMEM((tm, k), BF16)],
        compiler_params=_params(2),
        name="norm_matmul",
    )(x, g.reshape(1, k), w_tiles)


def _out_ple_kernel(a_ref, wo_ref, x_ref, g_ref, wg_ref, p_ref, wp_ref, gf_ref, o_ref, x1_ref, hn_ref, ss_ref, *, nt, final):
    j = pl.program_id(1)
    tn = x_ref.shape[1]
    n = nt * tn

    @pl.when(j == 0)
    def _():
        ss_ref[...] = jnp.zeros(ss_ref.shape, F32)

    @pl.when(j < nt)
    def _():
        x1 = x_ref[...] + _dot(a_ref[...], wo_ref[j])
        x1_ref[j] = x1
        hn_ref[j] = (x1 * g_ref[j]).astype(BF16)
        ss_ref[0] += jnp.sum(x1 * x1, axis=-1, keepdims=True)

    @pl.when(j >= nt)
    def _():
        t = j - nt
        emb = _dot(p_ref[...].astype(BF16), wp_ref[t])
        acc = _dot(hn_ref[0], wg_ref[t, :tn, :])
        for kt in range(1, nt):
            acc = acc + _dot(hn_ref[kt], wg_ref[t, kt * tn:(kt + 1) * tn, :])
        x2 = x1_ref[t] + emb * jax.nn.sigmoid(acc * lax.rsqrt(ss_ref[0] / n + EPS))
        if final:
            x1_ref[t] = x2
            ss_ref[1] += jnp.sum(x2 * x2, axis=-1, keepdims=True)
        else:
            o_ref[...] = x2

    if final:
        @pl.when(j == 2 * nt - 1)
        def _():
            inv = lax.rsqrt(ss_ref[1] / n + EPS)
            for t in range(nt):
                o_ref[:, t * tn:(t + 1) * tn] = x1_ref[t] * inv * gf_ref[t]


def _out_ple(a, w_out, x, g, w_gate, p, w_proj, layer, final_g=None, tn=512):
    m, k = a.shape
    n = w_out.shape[1]
    pd = p.shape[2]
    final = final_g is not None
    tm = 512 if final else 1024
    tm, tn = min(tm, m), min(tn, n)
    nt = n // tn
    resident = lambda rows: pl.BlockSpec((nt, rows, tn), lambda i, j: (0, 0, 0), pipeline_mode=pl.Buffered(1))
    row_vec = pl.BlockSpec((nt, 1, tn), lambda i, j: (0, 0, 0))
    if final:
        out_spec = pl.BlockSpec((tm, n), lambda i, j: (i, 0))
    else:
        out_spec = pl.BlockSpec((tm, tn), lambda i, j: (i, jnp.maximum(j - nt, 0)))
    return pl.pallas_call(
        functools.partial(_out_ple_kernel, nt=nt, final=final),
        grid=(m // tm, 2 * nt),
        in_specs=[
            pl.BlockSpec((tm, k), lambda i, j: (i, 0)),
            resident(k),
            pl.BlockSpec((tm, tn), lambda i, j: (i, jnp.minimum(j, nt - 1))),
            row_vec,
            resident(n),
            pl.BlockSpec((None, tm, pd), lambda i, j: (layer, i, 0)),
            resident(pd),
            row_vec,
        ],
        out_specs=out_spec,
        out_shape=jax.ShapeDtypeStruct((m, n), F32),
        scratch_shapes=[pltpu.VMEM((nt, tm, tn), F32), pltpu.VMEM((nt, tm, tn), BF16), pltpu.VMEM((2, tm, 1), F32)],
        compiler_params=_params(2, VMEM_LIMIT_BIG),
        name="out_ple_final" if final else "out_ple",
    )(a, _tile_cols(w_out[None], 0, tn), x, g.reshape(nt, 1, tn), _tile_cols(w_gate, layer, tn), p,
      _tile_cols(w_proj, layer, tn), (final_g if final else g).reshape(nt, 1, tn))


def _t5_bucket(dist):
    max_exact = REL_BUCKETS // 2
    d = np.maximum(dist, 1).astype(np.float32)
    large = max_exact + (np.log(d / max_exact) / np.log(REL_MAX_DIST / max_exact) * (REL_BUCKETS - max_exact)).astype(np.int32)
    large = np.minimum(large, REL_BUCKETS - 1)
    return np.where(dist < max_exact, dist, large).astype(np.int32)


def _swa_bucket_table():
    qi = np.arange(SWA_BLOCK)[None, :]
    kj = np.arange(2 * SWA_BLOCK)[:, None]
    return _t5_bucket(np.clip(qi + SWA_BLOCK - kj, 0, None))


def _swa_kernel(sink_ref, rel_ref, bucket_ref, q_ref, sg_ref, kp_ref, kc_ref, vp_ref, vc_ref, o_ref, bias_ref, qs_ref):
    n = pl.program_id(1)
    blk = SWA_BLOCK
    lane = lax.broadcasted_iota(jnp.int32, (1, LANES), 1)

    @pl.when(jnp.logical_and(pl.program_id(0) == 0, n == 0))
    def _():
        bucket = bucket_ref[...]
        kj = lax.broadcasted_iota(jnp.int32, (2 * blk, blk), 0)
        qi = lax.broadcasted_iota(jnp.int32, (2 * blk, blk), 1)
        band = jnp.logical_or(jnp.logical_and(kj < blk, kj > qi), jnp.logical_and(kj >= blk, kj - blk <= qi))

        def per_head(head, carry):
            acc = jnp.zeros((2 * blk, blk), F32)
            for b in range(REL_BUCKETS):
                acc = jnp.where(bucket == b, rel_ref[b, head], acc)
            bias_ref[head] = jnp.where(band, acc, NEG)
            return carry
        lax.fori_loop(0, SWA_HEADS, per_head, 0)

    for kvh in range(SWA_KV_HEADS):
        cols = slice(kvh * LANES, (kvh + 1) * LANES)
        heads = range(kvh * SWA_GROUP, (kvh + 1) * SWA_GROUP)
        for g, head in enumerate(heads):
            q_pair = q_ref[:, head // 2 * LANES:(head // 2 + 1) * LANES]
            qs_ref[g * blk:(g + 1) * blk, :] = jnp.where((lane // HEAD_DIM) == head % 2, q_pair, jnp.zeros_like(q_pair))
        keys = jnp.concatenate([kp_ref[:, cols], kc_ref[:, cols]], axis=0)
        vals = jnp.concatenate([vp_ref[:, cols], vc_ref[:, cols]], axis=0)
        s = _dot_nt(keys, qs_ref[...]) + jnp.concatenate([bias_ref[head] for head in heads], axis=1)
        s_prev = jnp.where(n > 0, s[:blk], NEG)
        s_cur = s[blk:]
        sink = jnp.concatenate([jnp.full((1, blk), sink_ref[head], F32) for head in heads], axis=1)
        m = jnp.maximum(jnp.maximum(s_prev, s_cur).max(axis=0, keepdims=True), sink)
        e_prev, e_cur = jnp.exp(s_prev - m), jnp.exp(s_cur - m)
        den = e_prev.sum(axis=0, keepdims=True) + e_cur.sum(axis=0, keepdims=True) + jnp.exp(sink - m)
        e = jnp.concatenate([e_prev, e_cur], axis=0).astype(BF16)
        out = _dot(vals.astype(F32).T.astype(BF16), e) / den
        for pair in range(SWA_GROUP // 2):
            head = kvh * SWA_GROUP + 2 * pair
            even = out[:HEAD_DIM, 2 * pair * blk:(2 * pair + 1) * blk]
            odd = out[HEAD_DIM:, (2 * pair + 1) * blk:(2 * pair + 2) * blk]
            cols = slice(head // 2 * LANES, (head // 2 + 1) * LANES)
            o_ref[:, cols] = (jnp.concatenate([even, odd], axis=0).T * sg_ref[:, cols].astype(F32)).astype(o_ref.dtype)


def _swa_attention(proj, sinks, rel_bias, bsz, seq):
    nb = seq // SWA_BLOCK
    width = SWA_HEADS * HEAD_DIM
    kvw = SWA_KV_HEADS * LANES
    k_blk = 2 * width // kvw
    cur = lambda b, n: b * nb + n
    prev = lambda b, n: b * nb + jnp.maximum(n - 1, 0)
    smem = pl.BlockSpec(memory_space=pltpu.SMEM)
    return pl.pallas_call(
        _swa_kernel,
        grid=(bsz, nb),
        in_specs=[
            smem, smem,
            pl.BlockSpec((2 * SWA_BLOCK, SWA_BLOCK), lambda b, n: (0, 0)),
            pl.BlockSpec((SWA_BLOCK, width), lambda b, n: (cur(b, n), 0)),
            pl.BlockSpec((SWA_BLOCK, width), lambda b, n: (cur(b, n), 1)),
            pl.BlockSpec((SWA_BLOCK, kvw), lambda b, n: (prev(b, n), k_blk)),
            pl.BlockSpec((SWA_BLOCK, kvw), lambda b, n: (cur(b, n), k_blk)),
            pl.BlockSpec((SWA_BLOCK, kvw), lambda b, n: (prev(b, n), k_blk + 1)),
            pl.BlockSpec((SWA_BLOCK, kvw), lambda b, n: (cur(b, n), k_blk + 1)),
        ],
        out_specs=pl.BlockSpec((SWA_BLOCK, width), lambda b, n: (cur(b, n), 0)),
        out_shape=jax.ShapeDtypeStruct((bsz * seq, width), BF16),
        scratch_shapes=[
            pltpu.VMEM((SWA_HEADS, 2 * SWA_BLOCK, SWA_BLOCK), F32),
            pltpu.VMEM((SWA_GROUP * SWA_BLOCK, LANES), BF16),
        ],
        compiler_params=_params(2),
        name="swa_attention",
    )(sinks.astype(F32), rel_bias.astype(F32), jnp.asarray(_swa_bucket_table()), proj, proj, proj, proj, proj, proj)


def _dup_heads(w, n_heads):
    k = w.shape[0]
    w = w.reshape(k, n_heads, 1, HEAD_DIM)
    return jnp.broadcast_to(w, (k, n_heads, 2, HEAD_DIM)).reshape(k, n_heads * 2 * HEAD_DIM)


def _swa_mixer(x, g, w_in, sinks, rel_bias, bsz, seq):
    width = SWA_HEADS * HEAD_DIM
    kvw = SWA_KV_HEADS * HEAD_DIM
    w_q, w_k, w_v, w_g = (w_in[:, :width], w_in[:, width:width + kvw],
                          w_in[:, width + kvw:width + 2 * kvw], w_in[:, width + 2 * kvw:])
    w = jnp.concatenate([w_q * (HEAD_DIM ** -0.5), w_g, _dup_heads(w_k, SWA_KV_HEADS), _dup_heads(w_v, SWA_KV_HEADS)], axis=1)
    proj = _norm_matmul(x, g, _tile_cols_xla(w, MATMUL_TN), BF16, silu_cols=(width, 2 * width))
    return _swa_attention(proj, sinks, rel_bias, bsz, seq)


HALO = 8


def _conv_kernel(x_ref, g_ref, wb_ref, wc_ref, wu_ref, wg_ref, ck_ref, o_ref, a_ref, z_ref, carry_ref, *, tiles_per_seq):
    i, j = pl.program_id(0), pl.program_id(1)
    tm = a_ref.shape[0]

    @pl.when(j == 0)
    def _():
        def chunk(rows):
            a_ref[rows, :] = _rmsnorm_rows(x_ref[rows, :], g_ref[...]).astype(BF16)
        _for_row_chunks(tm, chunk)

    a = a_ref[...]
    z = _dot(a, wc_ref[...]) * _dot(a, wu_ref[...])
    first = (i % tiles_per_seq) == 0

    @pl.when(first)
    def _():
        z_ref[:HALO, :] = jnp.zeros((HALO, z_ref.shape[1]), F32)

    @pl.when(jnp.logical_not(first))
    def _():
        z_ref[:HALO, :] = carry_ref[j]

    z_ref[HALO:, :] = z
    carry_ref[j] = z[tm - HALO:, :]
    conv = z_ref[HALO - 2:HALO - 2 + tm, :] * ck_ref[0:1, :]
    conv = conv + z_ref[HALO - 1:HALO - 1 + tm, :] * ck_ref[1:2, :]
    conv = conv + z * ck_ref[2:3, :]
    y = _dot(a, wb_ref[...]) * conv
    o_ref[...] = (y * _silu(_dot(a, wg_ref[...]))).astype(o_ref.dtype)


def _conv_mixer(x, g, w_in, layer, conv_kernel, seq, tm=1024, tn=512):
    m, k = x.shape
    width = w_in.shape[2] // 4
    tm, tn = min(tm, seq), min(tn, width)
    nt = width // tn
    w_spec = lambda q: pl.BlockSpec((None, k, tn), lambda i, j: (q * nt + j, 0, 0))
    w_in = _tile_cols(w_in, layer, tn)
    return pl.pallas_call(
        functools.partial(_conv_kernel, tiles_per_seq=seq // tm),
        grid=(m // tm, nt),
        in_specs=[
            pl.BlockSpec((tm, k), lambda i, j: (i, 0)),
            pl.BlockSpec((1, k), lambda i, j: (0, 0)),
            w_spec(0), w_spec(1), w_spec(2), w_spec(3),
            pl.BlockSpec((CONV_TAPS, tn), lambda i, j: (0, j)),
        ],
        out_specs=pl.BlockSpec((tm, tn), lambda i, j: (i, j)),
        out_shape=jax.ShapeDtypeStruct((m, width), BF16),
        scratch_shapes=[
            pltpu.VMEM((tm, k), BF16),
            pltpu.VMEM((HALO + tm, tn), F32),
            pltpu.VMEM((nt, HALO, tn), F32),
        ],
        compiler_params=_params(2, VMEM_LIMIT_BIG),
        name="conv_mixer",
    )(x, g.reshape(1, k), w_in, w_in, w_in, w_in, conv_kernel.astype(F32))


def _ssm_operators(lam_re, lam_im, log_dt, b_re, b_im, c_re, c_im, d_skip):
    n_groups = lam_re.shape[0]
    n_oct = n_groups // GROUPS_PER_TILE
    L, C, N = SSM_CHUNK, SSM_GROUP, SSM_STATE
    dt = jnp.exp(log_dt.astype(F32))[None, :]
    lr, li = lam_re.astype(F32).T, lam_im.astype(F32).T
    mag = jnp.exp(lr * dt)
    ab_re, ab_im = mag * jnp.cos(li * dt), mag * jnp.sin(li * dt)
    den = lr * lr + li * li
    nr = ab_re - 1.0
    coef_re = (nr * lr + ab_im * li) / den
    coef_im = (ab_im * lr - nr * li) / den
    br, bi = b_re.astype(F32).transpose(2, 1, 0), b_im.astype(F32).transpose(2, 1, 0)
    bb_re = coef_re * br - coef_im * bi
    bb_im = coef_re * bi + coef_im * br
    cr, ci = c_re.astype(F32).transpose(1, 2, 0), c_im.astype(F32).transpose(1, 2, 0)
    pw_re, pw_im = [jnp.ones_like(ab_re)], [jnp.zeros_like(ab_im)]
    for _ in range(L):
        pw_re, pw_im = (pw_re + [pw_re[-1] * ab_re - pw_im[-1] * ab_im], pw_im + [pw_re[-1] * ab_im + pw_im[-1] * ab_re])
    p_re, p_im = jnp.stack(pw_re), jnp.stack(pw_im)
    w_re = cr[None] * p_re[:L, None] - ci[None] * p_im[:L, None]
    w_im = cr[None] * p_im[:L, None] + ci[None] * p_re[:L, None]
    lag_k = jnp.sum(w_re[:, None] * bb_re[None, :, None] - w_im[:, None] * bb_im[None, :, None], axis=3)
    q_re, q_im = jnp.stack(pw_re[L - 1::-1]), jnp.stack(pw_im[L - 1::-1])
    bc_re = q_re[:, None] * bb_re[None] - q_im[:, None] * bb_im[None]
    bc_im = q_re[:, None] * bb_im[None] + q_im[:, None] * bb_re[None]
    o_re = cr[None] * p_re[1:, None] - ci[None] * p_im[1:, None]
    o_im = cr[None] * p_im[1:, None] + ci[None] * p_re[1:, None]
    octs = lambda t: t.reshape(t.shape[:-1] + (n_oct, GROUPS_PER_TILE))
    to_lag = lambda t: octs(t).transpose(3, 0, 1, 4, 2).reshape(n_oct, L, C, LANES)
    to_rows = lambda t: octs(t).transpose(3, 0, 1, 4, 2).reshape(n_oct, L * C, GROUPS_PER_TILE * N)
    to_cols = lambda t: octs(t).transpose(3, 4, 2, 0, 1).reshape(n_oct, GROUPS_PER_TILE * N, L * C)
    to_vec = lambda t: octs(t).transpose(1, 2, 0).reshape(n_oct, 1, GROUPS_PER_TILE * N)
    return dict(
        lag=to_lag(lag_k).astype(BF16),
        bc_re=to_rows(bc_re).astype(BF16), bc_im=to_rows(bc_im).astype(BF16),
        oc_re=to_cols(o_re).astype(BF16), oc_im_neg=to_cols(-o_im).astype(BF16),
        al_re=to_vec(p_re[L]), al_im=to_vec(p_im[L]), d=d_skip.astype(F32).reshape(n_oct, 1, LANES),
    )


def _ssm_expanders():
    L, C, N, G8 = SSM_CHUNK, SSM_GROUP, SSM_STATE, GROUPS_PER_TILE
    wide = np.arange(L * LANES)
    w_step, w_group, w_chan = wide // LANES, (wide // C) % G8, wide % C
    small = np.arange(L * C)
    s_step, s_chan = small // C, small % C
    state_group = np.arange(G8 * N) // N
    spread = (w_step[:, None] == s_step[None, :]) & (w_chan[:, None] == s_chan[None, :])
    lane = np.arange(LANES)
    as_bf16 = lambda a: jnp.asarray(a.astype(np.float32), dtype=BF16)
    return dict(
        spread_rows=as_bf16(spread), spread_cols=as_bf16(spread.T),
        keep_rows=as_bf16(w_group[:, None] == state_group[None, :]),
        keep_cols=as_bf16(state_group[:, None] == w_group[None, :]),
        spread_lag=as_bf16((lane % C)[:, None] == np.arange(C)[None, :]),
        keep_lag=jnp.asarray(((lane // C)[:, None] == (lane // C)[None, :]).astype(np.float32)),
    )


def _ssm_kernel(u_ref, lag_ref, bxr_ref, bxi_ref, cxr_ref, cxi_ref, alr_ref, ali_ref, d_ref,
                sr_ref, sc_ref, kr_ref, kc_ref, sl_ref, kl_ref, o_ref,
                ub_ref, bcr_ref, bci_ref, ocr_ref, oci_ref, zr_ref, zi_ref, hr_ref, hi_ref, *, bsz):
    L = SSM_CHUNK
    rows = u_ref.shape[0] // L
    chunks = rows // bsz
    tile = 2 * LANES
    step_rows = lambda r: pl.ds(r, rows, stride=L)
    for r in range(L):
        ub_ref[:, r * LANES:(r + 1) * LANES] = u_ref[step_rows(r), :].astype(BF16)
    bcr_ref[...] = (_dot(sr_ref[...], bxr_ref[...]) * kr_ref[...]).astype(BF16)
    bci_ref[...] = (_dot(sr_ref[...], bxi_ref[...]) * kr_ref[...]).astype(BF16)
    zr_ref[...] = _dot(ub_ref[...], bcr_ref[...])
    zi_ref[...] = _dot(ub_ref[...], bci_ref[...])
    a_re, a_im = alr_ref[...], ali_ref[...]

    def step(k, carry):
        new = []
        for b in range(bsz):
            h_re, h_im = carry[2 * b], carry[2 * b + 1]
            row = pl.ds(b * chunks + k, 1)
            hr_ref[row, :] = h_re
            hi_ref[row, :] = h_im
            new.append(a_re * h_re - a_im * h_im + zr_ref[row, :])
            new.append(a_re * h_im + a_im * h_re + zi_ref[row, :])
        return tuple(new)

    zero = jnp.zeros((1, a_re.shape[1]), F32)
    lax.fori_loop(0, chunks, step, (zero,) * (2 * bsz))

    ocr_ref[...] = (_dot(cxr_ref[...], sc_ref[...]) * kc_ref[...]).astype(BF16)
    oci_ref[...] = (_dot(cxi_ref[...], sc_ref[...]) * kc_ref[...]).astype(BF16)
    lag = [(_dot(sl_ref[...], lag_ref[d]) * kl_ref[...]).astype(BF16) for d in range(L)]
    lag_tile = []
    for dd in range(L // 2):
        below = lag[2 * dd - 1] if dd > 0 else jnp.zeros((LANES, LANES), BF16)
        lag_tile.append(jnp.concatenate([jnp.concatenate([lag[2 * dd], lag[2 * dd + 1]], axis=1),
                                         jnp.concatenate([below, lag[2 * dd]], axis=1)], axis=0))
    hb_re, hb_im = hr_ref[...].astype(BF16), hi_ref[...].astype(BF16)
    for t2 in range(L // 2):
        cols = slice(t2 * tile, (t2 + 1) * tile)
        y = _dot(hb_re, ocr_ref[:, cols]) + _dot(hb_im, oci_ref[:, cols])
        for r2 in range(t2 + 1):
            y = y + _dot(ub_ref[:, r2 * tile:(r2 + 1) * tile], lag_tile[t2 - r2])
        for half in range(2):
            t = 2 * t2 + half
            o_ref[step_rows(t), :] = jax.nn.gelu(y[:, half * LANES:(half + 1) * LANES] + d_ref[...] * u_ref[step_rows(t), :])


def _ssm_core(proj, ops, bsz, seq):
    m = proj.shape[0]
    width = proj.shape[1] // 2
    L, C = SSM_CHUNK, SSM_GROUP
    n_oct = width // LANES
    rows = m // L
    ow = L * LANES
    n_state = GROUPS_PER_TILE * SSM_STATE
    ex = _ssm_expanders()
    per_oct = lambda *shape: pl.BlockSpec((None,) + shape, lambda p: (p,) + (0,) * len(shape))
    const = lambda a: pl.BlockSpec(a.shape, lambda p: (0,) * a.ndim, pipeline_mode=pl.Buffered(1))
    consts = [ex['spread_rows'], ex['spread_cols'], ex['keep_rows'], ex['keep_cols'], ex['spread_lag'], ex['keep_lag']]
    return pl.pallas_call(
        functools.partial(_ssm_kernel, bsz=bsz),
        grid=(n_oct,),
        in_specs=[
            pl.BlockSpec((m, LANES), lambda p: (0, p)),
            per_oct(L, C, LANES), per_oct(L * C, n_state), per_oct(L * C, n_state),
            per_oct(n_state, L * C), per_oct(n_state, L * C), per_oct(1, n_state), per_oct(1, n_state), per_oct(1, LANES),
        ] + [const(a) for a in consts],
        out_specs=pl.BlockSpec((m, LANES), lambda p: (0, p)),
        out_shape=jax.ShapeDtypeStruct((m, width), F32),
        scratch_shapes=[pltpu.VMEM((rows, ow), BF16)] + [pltpu.VMEM((ow, n_state), BF16)] * 2
        + [pltpu.VMEM((n_state, ow), BF16)] * 2 + [pltpu.VMEM((rows, n_state), F32)] * 4,
        compiler_params=_params(1),
        name="ssm_core",
    )(proj, ops['lag'], ops['bc_re'], ops['bc_im'], ops['oc_re'], ops['oc_im_neg'], ops['al_re'], ops['al_im'], ops['d'], *consts)


def _glu_kernel(y_ref, wa_ref, wb_ref, ba_ref, bb_ref, gate_ref, o_ref, yb_ref):
    @pl.when(pl.program_id(1) == 0)
    def _():
        def chunk(rows):
            yb_ref[rows, :] = y_ref[rows, :].astype(BF16)
        _for_row_chunks(yb_ref.shape[0], chunk)

    y = yb_ref[...]
    ga = _dot(y, wa_ref[...]) + ba_ref[...]
    gb = _dot(y, wb_ref[...]) + bb_ref[...]
    o_ref[...] = ((ga * jax.nn.sigmoid(gb)) * _silu(gate_ref[...])).astype(o_ref.dtype)


def _glu(y, w_glu, layer, b_glu, proj, tm=1024, tn=512):
    m, k = y.shape
    width = w_glu.shape[2] // 2
    tm, tn = min(tm, m), min(tn, width)
    nt = width // tn
    w_tiles = _tile_cols(w_glu, layer, tn)
    return pl.pallas_call(
        _glu_kernel,
        grid=(m // tm, nt),
        in_specs=[
            pl.BlockSpec((tm, k), lambda i, j: (i, 0)),
            pl.BlockSpec((None, k, tn), lambda i, j: (j, 0, 0)),
            pl.BlockSpec((None, k, tn), lambda i, j: (nt + j, 0, 0)),
            pl.BlockSpec((1, tn), lambda i, j: (0, j)),
            pl.BlockSpec((1, tn), lambda i, j: (0, nt + j)),
            pl.BlockSpec((tm, tn), lambda i, j: (i, nt + j)),
        ],
        out_specs=pl.BlockSpec((tm, tn), lambda i, j: (i, j)),
        out_shape=jax.ShapeDtypeStruct((m, width), BF16),
        scratch_shapes=[pltpu.VMEM((tm, k), BF16)],
        compiler_params=_params(2),
        name="ssm_glu",
    )(y, w_tiles, w_tiles, b_glu, b_glu, proj)


def _ssm_mixer(x, g, w_in, layer, lam_re, lam_im, log_dt, b_re, b_im, c_re, c_im, d_skip, w_glu, b_glu, bsz, seq):
    proj = _norm_matmul(x, g, _tile_cols(w_in, layer, MATMUL_TN), F32)
    ops = _ssm_operators(lam_re, lam_im, log_dt, b_re, b_im, c_re, c_im, d_skip)
    y = _ssm_core(proj, ops, bsz, seq)
    return _glu(y, w_glu, layer, b_glu.astype(F32).reshape(1, -1), proj)


CUM_BLOCK = 128


def _split3(x):
    x1 = x.astype(BF16)
    r1 = x - x1.astype(F32)
    x2 = r1.astype(BF16)
    x3 = (r1 - x2.astype(F32)).astype(BF16)
    return x1, x2, x3


def _forget_cumsum_kernel(z_ref, b_ref, o_ref):
    n_blocks = z_ref.shape[0] // CUM_BLOCK
    ri = lax.broadcasted_iota(jnp.int32, (CUM_BLOCK, CUM_BLOCK), 0)
    ci = lax.broadcasted_iota(jnp.int32, (CUM_BLOCK, CUM_BLOCK), 1)
    tri = jnp.where(ci <= ri, 1.0, 0.0).astype(BF16)

    def body(i, carry):
        rows = pl.ds(pl.multiple_of(i * CUM_BLOCK, CUM_BLOCK), CUM_BLOCK)
        z = z_ref[rows, :] + b_ref[...]
        log_f = jnp.minimum(z, 0.0) - jnp.log1p(jnp.exp(-jnp.abs(z)))
        x1, x2, x3 = _split3(log_f)
        c = _dot(tri, x1) + _dot(tri, x2) + _dot(tri, x3) + carry
        o_ref[rows, :] = c
        return c[CUM_BLOCK - 1:, :]

    lax.fori_loop(0, n_blocks, body, jnp.zeros((1, z_ref.shape[1]), F32))


def _forget_cumsum(z, b, bsz, seq):
    lanes = z.shape[1]
    return pl.pallas_call(
        _forget_cumsum_kernel,
        grid=(bsz,),
        in_specs=[pl.BlockSpec((seq, lanes), lambda i: (i, 0)), pl.BlockSpec((1, lanes), lambda i: (0, 0))],
        out_specs=pl.BlockSpec((seq, lanes), lambda i: (i, 0)),
        out_shape=jax.ShapeDtypeStruct(z.shape, F32),
        compiler_params=_params(1),
        name="forget_cumsum",
    )(z, b)


def _fox_kernel(q_ref, k_ref, v_ref, sg_ref, cq_ref, ck_ref, o_ref, ka_ref, vt_ref, *head_refs, tk):
    hp, qi = pl.program_id(1), pl.program_id(2)
    tq = q_ref.shape[0]
    qa_ref, s_ref, p_ref, acc_ref, pv_ref = (head_refs[i::5] for i in range(5))
    lane = lax.broadcasted_iota(jnp.int32, (1, LANES), 1)
    free = (HEAD_DIM, 0)
    n_pieces = 3

    def bias_lanes(c, hh, c_offset, one_offset):
        rr = lax.broadcasted_iota(jnp.int32, (n_pieces * LANES, LANES), 0)
        cc = lax.broadcasted_iota(jnp.int32, (n_pieces * LANES, LANES), 1)
        place = jnp.logical_and(rr % LANES == 2 * hp + hh, cc == free[hh] + c_offset + rr // LANES)
        placed = _dot(jnp.concatenate(_split3(c), axis=1), jnp.where(place, 1.0, 0.0).astype(BF16))
        first = free[hh] + one_offset
        return (placed + jnp.where(jnp.logical_and(lane >= first, lane < first + n_pieces), 1.0, 0.0)).astype(BF16)

    @pl.when(qi == 0)
    def _():
        def chunk(j, carry):
            rows = pl.ds(pl.multiple_of(j * tk, tk), tk)
            keys = k_ref[rows, :]
            neg_c = ck_ref[rows, :] * -LOG2E
            for hh in range(2):
                ka_ref[hh, j] = jnp.where((lane // HEAD_DIM) == hh, keys, bias_lanes(neg_c, hh, 0, n_pieces))
            vt_ref[j] = v_ref[rows, :].astype(F32).T.astype(BF16)
            return carry
        lax.fori_loop(0, k_ref.shape[0] // tk, chunk, 0)

    cq = cq_ref[...] * LOG2E
    for hh in range(2):
        qa = jnp.where((lane // HEAD_DIM) == hh, q_ref[...], bias_lanes(cq, hh, n_pieces, 0))
        qa_ref[hh][...] = qa.astype(F32).T.astype(BF16)

    def scores(hh, j):
        s_ref[hh][...] = _dot(ka_ref[hh, j], qa_ref[hh][...])

    rows8 = 8
    rows16 = 16

    def absorb(hh, j, m, l, diagonal):
        def strip(r, n):
            blk = s_ref[hh][r:r + n, :]
            if diagonal:
                kpos = lax.broadcasted_iota(jnp.int32, (n, tq), 0)
                qpos = lax.broadcasted_iota(jnp.int32, (n, tq), 1)
                blk = jnp.where(kpos + r <= qpos, blk, NEG)
            return blk
        top = strip(0, rows8)
        for r in range(rows8, tk, rows8):
            top = jnp.maximum(top, strip(r, rows8))
        m_new = jnp.maximum(m, top.max(axis=0, keepdims=True))
        alpha = jnp.exp2(m - m_new)
        m_rows = jnp.broadcast_to(m_new, (rows16, tq))
        total = None
        for r in range(0, tk, rows16):
            p = jnp.exp2(strip(r, rows16) - m_rows)
            total = p if total is None else total + p
            p_ref[hh][r:r + rows16, :] = p.astype(BF16)
        acc_ref[hh][...] = alpha * (acc_ref[hh][...] + pv_ref[hh][...])
        return m_new, alpha * l + total.sum(axis=0, keepdims=True)

    def values(hh, j):
        pv_ref[hh][...] = _dot(vt_ref[j, hh * HEAD_DIM:(hh + 1) * HEAD_DIM, :], p_ref[hh][...])

    def step(j, carry, diagonal):
        m0, l0, m1, l1 = carry
        values(1, jnp.maximum(j - 1, 0))
        scores(1, j)
        m0, l0 = absorb(0, j, m0, l0, diagonal)
        values(0, j)
        if not diagonal:
            scores(0, j + 1)
        m1, l1 = absorb(1, j, m1, l1, diagonal)
        return m0, l0, m1, l1

    for hh in range(2):
        acc_ref[hh][...] = jnp.zeros(acc_ref[hh].shape, F32)
        pv_ref[hh][...] = jnp.zeros(pv_ref[hh].shape, F32)
    p_ref[1][...] = jnp.zeros(p_ref[1].shape, BF16)
    scores(0, 0)
    init = (jnp.full((1, tq), NEG, F32), jnp.zeros((1, tq), F32)) * 2
    carry = lax.fori_loop(0, qi, lambda j, c: step(j, c, False), init)
    _, l0, _, l1 = step(qi, carry, True)
    values(1, qi)
    out = jnp.concatenate([(acc_ref[0][...] + pv_ref[0][...]) / l0, (acc_ref[1][...] + pv_ref[1][...]) / l1], axis=0)
    o_ref[...] = (out.T * sg_ref[...].astype(F32)).astype(o_ref.dtype)


def _fox_attention(proj, csum, bsz, seq, t=512):
    width = FOX_HEADS * HEAD_DIM
    n_pairs = width // LANES
    t = min(t, seq)
    nq = seq // t
    return pl.pallas_call(
        functools.partial(_fox_kernel, tk=t),
        grid=(bsz, n_pairs, nq),
        in_specs=[
            pl.BlockSpec((t, LANES), lambda b, h, i: (b * nq + i, h)),
            pl.BlockSpec((seq, LANES), lambda b, h, i: (b, n_pairs + h)),
            pl.BlockSpec((seq, LANES), lambda b, h, i: (b, 2 * n_pairs + h)),
            pl.BlockSpec((t, LANES), lambda b, h, i: (b * nq + i, 3 * n_pairs + h)),
            pl.BlockSpec((t, LANES), lambda b, h, i: (b * nq + i, 0)),
            pl.BlockSpec((seq, LANES), lambda b, h, i: (b, 0)),
        ],
        out_specs=pl.BlockSpec((t, LANES), lambda b, h, i: (b * nq + i, h)),
        out_shape=jax.ShapeDtypeStruct((bsz * seq, width), BF16),
        scratch_shapes=[
            pltpu.VMEM((2, nq, t, LANES), BF16), pltpu.VMEM((nq, LANES, t), BF16),
        ] + 2 * [pltpu.VMEM((LANES, t), BF16), pltpu.VMEM((t, t), F32), pltpu.VMEM((t, t), BF16),
                 pltpu.VMEM((HEAD_DIM, t), F32), pltpu.VMEM((HEAD_DIM, t), F32)],
        compiler_params=_params(3),
        name="fox_attention",
    )(proj, proj, proj, proj, csum, csum)


def _fox_mixer(x, g, w_in, layer, w_fg, b_fg, bsz, seq, t=512):
    width = FOX_HEADS * HEAD_DIM
    col_scale = jnp.where(jnp.arange(4 * width) < width, LOG2E * HEAD_DIM ** -0.5, 1.0).astype(F32)
    proj = _norm_matmul(x, g, _tile_cols(w_in, layer, MATMUL_TN, col_scale), BF16, silu_cols=(3 * width, 4 * width))
    pad = LANES - FOX_HEADS
    z = _norm_matmul(x, g, _tile_cols_xla(jnp.pad(w_fg, ((0, 0), (0, pad))), LANES), F32)
    csum = _forget_cumsum(z, jnp.pad(b_fg.astype(F32), (0, pad)).reshape(1, LANES), bsz, seq)
    return _fox_attention(proj, csum, bsz, seq, t)


def kernel(x, p, norm_g, final_g, rel_bias, swa_w_in, swa_w_out, swa_sinks, conv_w_in, conv_kernel, conv_w_out, ssm_w_in, ssm_lam_re, ssm_lam_im, ssm_log_dt, ssm_b_re, ssm_b_im, ssm_c_re, ssm_c_im, ssm_d, ssm_w_glu, ssm_b_glu, ssm_w_out, fox_w_in, fox_w_fg, fox_b_fg, fox_w_out, ple_proj, ple_norm, ple_gate):
    bsz, seq, d_model = x.shape
    depth = p.shape[0]
    h = x.astype(F32).reshape(bsz * seq, d_model)
    for i in range(depth):
        mixer, j = i % N_MIXERS, i // N_MIXERS
        if mixer == 0:
            a, w_out = _swa_mixer(h, norm_g[i], swa_w_in[j], swa_sinks[j], rel_bias, bsz, seq), swa_w_out[j]
        elif mixer == 1:
            a, w_out = _conv_mixer(h, norm_g[i], conv_w_in, j, conv_kernel[j], seq), conv_w_out[j]
        elif mixer == 2:
            a = _ssm_mixer(h, norm_g[i], ssm_w_in, j, ssm_lam_re[j], ssm_lam_im[j], ssm_log_dt[j], ssm_b_re[j], ssm_b_im[j],
                           ssm_c_re[j], ssm_c_im[j], ssm_d[j], ssm_w_glu, ssm_b_glu[j], bsz, seq)
            w_out = ssm_w_out[j]
        else:
            a, w_out = _fox_mixer(h, norm_g[i], fox_w_in, j, fox_w_fg[j], fox_b_fg[j], bsz, seq), fox_w_out[j]
        h = _out_ple(a, w_out, h, ple_norm[i], ple_gate, p.reshape(depth, bsz * seq, -1), ple_proj, i,
                     final_g=final_g if i == depth - 1 else None)
    return h.reshape(bsz, seq, d_model).astype(x.dtype)
```

```python
import functools
import math

import numpy as np
import jax
import jax.numpy as jnp
from jax import lax
from jax.experimental import pallas as pl
from jax.experimental.pallas import tpu as pltpu

F32 = jnp.float32
BF16 = jnp.bfloat16

EPS = 1e-6
N_MIXERS = 4
PLE_DIM = 256

SWA_HEADS = 32
SWA_KV_HEADS = 4
SWA_GROUP = SWA_HEADS // SWA_KV_HEADS
HEAD_DIM = 64
SWA_BLOCK = 128
WINDOW = 128
REL_BUCKETS = 32
REL_MAX_DIST = 128

CONV_TAPS = 3

SSM_GROUP = 16
SSM_STATE = 64
SSM_CHUNK = 16

FOX_HEADS = 32

LANES = 128
GROUPS_PER_TILE = LANES // SSM_GROUP
VMEM_LIMIT = 48 * 1024 * 1024
VMEM_LIMIT_BIG = 60 * 1024 * 1024

NEG = float(jnp.finfo(jnp.float32).min)
LOG2E = math.log2(math.e)


def _params(n_axes, vmem_limit=VMEM_LIMIT):
    return pltpu.CompilerParams(dimension_semantics=("arbitrary",) * n_axes, vmem_limit_bytes=vmem_limit)


def _dot(a, b):
    return jnp.dot(a, b, preferred_element_type=F32)


def _dot_nt(a, b):
    return lax.dot_general(a, b, (((1,), (1,)), ((), ())), preferred_element_type=F32)


def _rmsnorm_rows(x, g):
    return x * lax.rsqrt(jnp.mean(x * x, axis=-1, keepdims=True) + EPS) * g


def _silu(x):
    return x * jax.nn.sigmoid(x)


ROW_CHUNK = 64


def _for_row_chunks(n_rows, fn):
    def body(c, carry):
        fn(pl.ds(pl.multiple_of(c * ROW_CHUNK, ROW_CHUNK), ROW_CHUNK))
        return carry
    lax.fori_loop(0, n_rows // ROW_CHUNK, body, 0)


def _tile_cols_xla(w, tn):
    k, n = w.shape
    return w.astype(BF16).reshape(k, n // tn, tn).transpose(1, 0, 2)


def _cast_tiles_kernel(w_ref, o_ref):
    def chunk(rows):
        o_ref[rows, :] = w_ref[rows, :].astype(BF16)
    _for_row_chunks(o_ref.shape[0], chunk)


def _scale_cast_tiles_kernel(w_ref, s_ref, o_ref):
    def chunk(rows):
        o_ref[rows, :] = (w_ref[rows, :] * s_ref[...]).astype(BF16)
    _for_row_chunks(o_ref.shape[0], chunk)


def _tile_cols(w, layer, tn, col_scale=None):
    _, k, n = w.shape
    tn = min(tn, n)
    w_spec = pl.BlockSpec((None, k, tn), lambda j: (layer, 0, j))
    common = dict(
        grid=(n // tn,),
        out_specs=pl.BlockSpec((None, k, tn), lambda j: (j, 0, 0)),
        out_shape=jax.ShapeDtypeStruct((n // tn, k, tn), BF16),
        compiler_params=_params(1),
        name="weight_tiles",
    )
    if col_scale is None:
        return pl.pallas_call(_cast_tiles_kernel, in_specs=[w_spec], **common)(w)
    s_spec = pl.BlockSpec((1, tn), lambda j: (0, j))
    return pl.pallas_call(_scale_cast_tiles_kernel, in_specs=[w_spec, s_spec], **common)(w, col_scale.reshape(1, n))


def _norm_matmul_kernel(x_ref, g_ref, w_ref, o_ref, a_ref, *, silu_tiles):
    j = pl.program_id(1)

    @pl.when(j == 0)
    def _():
        def chunk(rows):
            a_ref[rows, :] = _rmsnorm_rows(x_ref[rows, :], g_ref[...]).astype(BF16)
        _for_row_chunks(a_ref.shape[0], chunk)

    acc = _dot(a_ref[...], w_ref[...])
    if silu_tiles is None:
        o_ref[...] = acc.astype(o_ref.dtype)
    else:
        gated = jnp.logical_and(j >= silu_tiles[0], j < silu_tiles[1])

        @pl.when(gated)
        def _():
            o_ref[...] = _silu(acc).astype(o_ref.dtype)

        @pl.when(jnp.logical_not(gated))
        def _():
            o_ref[...] = acc.astype(o_ref.dtype)


MATMUL_TN = 1024


def _norm_matmul(x, g, w_tiles, out_dtype, silu_cols=None, tm=1024):
    m, k = x.shape
    tn = w_tiles.shape[2]
    n = w_tiles.shape[0] * tn
    tm = min(tm, m)
    silu_tiles = None if silu_cols is None else (silu_cols[0] // tn, silu_cols[1] // tn)
    return pl.pallas_call(
        functools.partial(_norm_matmul_kernel, silu_tiles=silu_tiles),
        grid=(m // tm, n // tn),
        in_specs=[
            pl.BlockSpec((tm, k), lambda i, j: (i, 0)),
            pl.BlockSpec((1, k), lambda i, j: (0, 0)),
            pl.BlockSpec((None, k, tn), lambda i, j: (j, 0, 0)),
        ],
        out_specs=pl.BlockSpec((tm, tn), lambda i, j: (i, j)),
        out_shape=jax.ShapeDtypeStruct((m, n), out_dtype),
        scratch_shapes=[pltpu.VMEM((tm, k), BF16)],
        compiler_params=_params(2),
        name="norm_matmul",
    )(x, g.reshape(1, k), w_tiles)


def _out_ple_kernel(a_ref, wo_ref, x_ref, g_ref, wg_ref, p_ref, wp_ref, gf_ref, o_ref, x1_ref, hn_ref, ss_ref, *, nt, final):
    j = pl.program_id(1)
    tn = x_ref.shape[1]
    n = nt * tn

    @pl.when(j == 0)
    def _():
        ss_ref[...] = jnp.zeros(ss_ref.shape, F32)

    @pl.when(j < nt)
    def _():
        x1 = x_ref[...] + _dot(a_ref[...], wo_ref[j])
        x1_ref[j] = x1
        hn_ref[j] = (x1 * g_ref[j]).astype(BF16)
        ss_ref[0] += jnp.sum(x1 * x1, axis=-1, keepdims=True)

    @pl.when(jnp.logical_and(j >= nt, j < 2 * nt))
    def _():
        t = j - nt
        emb = _dot(p_ref[...].astype(BF16), wp_ref[t])
        acc = _dot(hn_ref[0], wg_ref[t, :tn, :])
        for kt in range(1, nt):
            acc = acc + _dot(hn_ref[kt], wg_ref[t, kt * tn:(kt + 1) * tn, :])
        x2 = x1_ref[t] + emb * jax.nn.sigmoid(acc * lax.rsqrt(ss_ref[0] / n + EPS))
        if final:
            x1_ref[t] = x2
            ss_ref[1] += jnp.sum(x2 * x2, axis=-1, keepdims=True)
        else:
            o_ref[...] = x2

    if final:
        @pl.when(j >= 2 * nt)
        def _():
            t = j - 2 * nt
            o_ref[...] = x1_ref[t] * lax.rsqrt(ss_ref[1] / n + EPS) * gf_ref[t]


def _out_ple(a, w_out, x, g, w_gate, p, w_proj, layer, final_g=None, tm=1024, tn=512):
    m, k = a.shape
    n = w_out.shape[1]
    pd = p.shape[2]
    final = final_g is not None
    tm, tn = min(tm, m), min(tn, n)
    nt = n // tn
    passes = 3 if final else 2
    out_tile = lambda j: jnp.maximum(j - (passes - 1) * nt, 0)
    resident = lambda rows: pl.BlockSpec((nt, rows, tn), lambda i, j: (0, 0, 0), pipeline_mode=pl.Buffered(1))
    row_vec = pl.BlockSpec((nt, 1, tn), lambda i, j: (0, 0, 0))
    return pl.pallas_call(
        functools.partial(_out_ple_kernel, nt=nt, final=final),
        grid=(m // tm, passes * nt),
        in_specs=[
            pl.BlockSpec((tm, k), lambda i, j: (i, 0)),
            resident(k),
            pl.BlockSpec((tm, tn), lambda i, j: (i, jnp.minimum(j, nt - 1))),
            row_vec,
            resident(n),
            pl.BlockSpec((None, tm, pd), lambda i, j: (layer, i, 0)),
            resident(pd),
            row_vec,
        ],
        out_specs=pl.BlockSpec((tm, tn), lambda i, j: (i, out_tile(j))),
        out_shape=jax.ShapeDtypeStruct((m, n), F32),
        scratch_shapes=[pltpu.VMEM((nt, tm, tn), F32), pltpu.VMEM((nt, tm, tn), BF16), pltpu.VMEM((2, tm, 1), F32)],
        compiler_params=_params(2, VMEM_LIMIT_BIG),
        name="out_ple_final" if final else "out_ple",
    )(a, _tile_cols(w_out[None], 0, tn), x, g.reshape(nt, 1, tn), _tile_cols(w_gate, layer, tn), p,
      _tile_cols(w_proj, layer, tn), (final_g if final else g).reshape(nt, 1, tn))


def _t5_bucket(dist):
    max_exact = REL_BUCKETS // 2
    d = np.maximum(dist, 1).astype(np.float32)
    large = max_exact + (np.log(d / max_exact) / np.log(REL_MAX_DIST / max_exact) * (REL_BUCKETS - max_exact)).astype(np.int32)
    large = np.minimum(large, REL_BUCKETS - 1)
    return np.where(dist < max_exact, dist, large).astype(np.int32)


def _swa_bucket_table():
    qi = np.arange(SWA_BLOCK)[None, :]
    kj = np.arange(2 * SWA_BLOCK)[:, None]
    return _t5_bucket(np.clip(qi + SWA_BLOCK - kj, 0, None))


def _swa_kernel(sink_ref, rel_ref, bucket_ref, q_ref, sg_ref, kp_ref, kc_ref, vp_ref, vc_ref, o_ref, bias_ref, qs_ref):
    n = pl.program_id(1)
    blk = SWA_BLOCK
    lane = lax.broadcasted_iota(jnp.int32, (1, LANES), 1)

    @pl.when(jnp.logical_and(pl.program_id(0) == 0, n == 0))
    def _():
        bucket = bucket_ref[...]
        kj = lax.broadcasted_iota(jnp.int32, (2 * blk, blk), 0)
        qi = lax.broadcasted_iota(jnp.int32, (2 * blk, blk), 1)
        band = jnp.logical_or(jnp.logical_and(kj < blk, kj > qi), jnp.logical_and(kj >= blk, kj - blk <= qi))

        def per_head(head, carry):
            acc = jnp.zeros((2 * blk, blk), F32)
            for b in range(REL_BUCKETS):
                acc = jnp.where(bucket == b, rel_ref[b, head], acc)
            bias_ref[head] = jnp.where(band, acc, NEG)
            return carry
        lax.fori_loop(0, SWA_HEADS, per_head, 0)

    for kvh in range(SWA_KV_HEADS):
        cols = slice(kvh * LANES, (kvh + 1) * LANES)
        heads = range(kvh * SWA_GROUP, (kvh + 1) * SWA_GROUP)
        for g, head in enumerate(heads):
            q_pair = q_ref[:, head // 2 * LANES:(head // 2 + 1) * LANES]
            qs_ref[g * blk:(g + 1) * blk, :] = jnp.where((lane // HEAD_DIM) == head % 2, q_pair, jnp.zeros_like(q_pair))
        keys = jnp.concatenate([kp_ref[:, cols], kc_ref[:, cols]], axis=0)
        vals = jnp.concatenate([vp_ref[:, cols], vc_ref[:, cols]], axis=0)
        s = _dot_nt(keys, qs_ref[...]) + jnp.concatenate([bias_ref[head] for head in heads], axis=1)
        s_prev = jnp.where(n > 0, s[:blk], NEG)
        s_cur = s[blk:]
        sink = jnp.concatenate([jnp.full((1, blk), sink_ref[head], F32) for head in heads], axis=1)
        m = jnp.maximum(jnp.maximum(s_prev, s_cur).max(axis=0, keepdims=True), sink)
        e_prev, e_cur = jnp.exp(s_prev - m), jnp.exp(s_cur - m)
        den = e_prev.sum(axis=0, keepdims=True) + e_cur.sum(axis=0, keepdims=True) + jnp.exp(sink - m)
        e = jnp.concatenate([e_prev, e_cur], axis=0).astype(BF16)
        out = _dot(vals.astype(F32).T.astype(BF16), e) / den
        for pair in range(SWA_GROUP // 2):
            head = kvh * SWA_GROUP + 2 * pair
            even = out[:HEAD_DIM, 2 * pair * blk:(2 * pair + 1) * blk]
            odd = out[HEAD_DIM:, (2 * pair + 1) * blk:(2 * pair + 2) * blk]
            cols = slice(head // 2 * LANES, (head // 2 + 1) * LANES)
            o_ref[:, cols] = (jnp.concatenate([even, odd], axis=0).T * sg_ref[:, cols].astype(F32)).astype(o_ref.dtype)


def _swa_attention(proj, sinks, rel_bias, bsz, seq):
    nb = seq // SWA_BLOCK
    width = SWA_HEADS * HEAD_DIM
    kvw = SWA_KV_HEADS * LANES
    k_blk = 2 * width // kvw
    cur = lambda b, n: b * nb + n
    prev = lambda b, n: b * nb + jnp.maximum(n - 1, 0)
    smem = pl.BlockSpec(memory_space=pltpu.SMEM)
    return pl.pallas_call(
        _swa_kernel,
        grid=(bsz, nb),
        in_specs=[
            smem, smem,
            pl.BlockSpec((2 * SWA_BLOCK, SWA_BLOCK), lambda b, n: (0, 0)),
            pl.BlockSpec((SWA_BLOCK, width), lambda b, n: (cur(b, n), 0)),
            pl.BlockSpec((SWA_BLOCK, width), lambda b, n: (cur(b, n), 1)),
            pl.BlockSpec((SWA_BLOCK, kvw), lambda b, n: (prev(b, n), k_blk)),
            pl.BlockSpec((SWA_BLOCK, kvw), lambda b, n: (cur(b, n), k_blk)),
            pl.BlockSpec((SWA_BLOCK, kvw), lambda b, n: (prev(b, n), k_blk + 1)),
            pl.BlockSpec((SWA_BLOCK, kvw), lambda b, n: (cur(b, n), k_blk + 1)),
        ],
        out_specs=pl.BlockSpec((SWA_BLOCK, width), lambda b, n: (cur(b, n), 0)),
        out_shape=jax.ShapeDtypeStruct((bsz * seq, width), BF16),
        scratch_shapes=[
            pltpu.VMEM((SWA_HEADS, 2 * SWA_BLOCK, SWA_BLOCK), F32),
            pltpu.VMEM((SWA_GROUP * SWA_BLOCK, LANES), BF16),
        ],
        compiler_params=_params(2),
        name="swa_attention",
    )(sinks.astype(F32), rel_bias.astype(F32), jnp.asarray(_swa_bucket_table()), proj, proj, proj, proj, proj, proj)


def _dup_heads(w, n_heads):
    k = w.shape[0]
    w = w.reshape(k, n_heads, 1, HEAD_DIM)
    return jnp.broadcast_to(w, (k, n_heads, 2, HEAD_DIM)).reshape(k, n_heads * 2 * HEAD_DIM)


def _swa_mixer(x, g, w_in, sinks, rel_bias, bsz, seq):
    width = SWA_HEADS * HEAD_DIM
    kvw = SWA_KV_HEADS * HEAD_DIM
    w_q, w_k, w_v, w_g = (w_in[:, :width], w_in[:, width:width + kvw],
                          w_in[:, width + kvw:width + 2 * kvw], w_in[:, width + 2 * kvw:])
    w = jnp.concatenate([w_q * (HEAD_DIM ** -0.5), w_g, _dup_heads(w_k, SWA_KV_HEADS), _dup_heads(w_v, SWA_KV_HEADS)], axis=1)
    proj = _norm_matmul(x, g, _tile_cols_xla(w, MATMUL_TN), BF16, silu_cols=(width, 2 * width))
    return _swa_attention(proj, sinks, rel_bias, bsz, seq)


HALO = 8


def _conv_kernel(x_ref, g_ref, wb_ref, wc_ref, wu_ref, wg_ref, ck_ref, o_ref, a_ref, z_ref, carry_ref, *, tiles_per_seq):
    i, j = pl.program_id(0), pl.program_id(1)
    tm = a_ref.shape[0]

    @pl.when(j == 0)
    def _():
        def chunk(rows):
            a_ref[rows, :] = _rmsnorm_rows(x_ref[rows, :], g_ref[...]).astype(BF16)
        _for_row_chunks(tm, chunk)

    a = a_ref[...]
    z = _dot(a, wc_ref[...]) * _dot(a, wu_ref[...])
    first = (i % tiles_per_seq) == 0

    @pl.when(first)
    def _():
        z_ref[:HALO, :] = jnp.zeros((HALO, z_ref.shape[1]), F32)

    @pl.when(jnp.logical_not(first))
    def _():
        z_ref[:HALO, :] = carry_ref[j]

    z_ref[HALO:, :] = z
    carry_ref[j] = z[tm - HALO:, :]
    conv = z_ref[HALO - 2:HALO - 2 + tm, :] * ck_ref[0:1, :]
    conv = conv + z_ref[HALO - 1:HALO - 1 + tm, :] * ck_ref[1:2, :]
    conv = conv + z * ck_ref[2:3, :]
    y = _dot(a, wb_ref[...]) * conv
    o_ref[...] = (y * _silu(_dot(a, wg_ref[...]))).astype(o_ref.dtype)


def _conv_mixer(x, g, w_in, layer, conv_kernel, seq, tm=1024, tn=512):
    m, k = x.shape
    width = w_in.shape[2] // 4
    tm, tn = min(tm, seq), min(tn, width)
    nt = width // tn
    w_spec = lambda q: pl.BlockSpec((None, k, tn), lambda i, j: (q * nt + j, 0, 0))
    w_in = _tile_cols(w_in, layer, tn)
    return pl.pallas_call(
        functools.partial(_conv_kernel, tiles_per_seq=seq // tm),
        grid=(m // tm, nt),
        in_specs=[
            pl.BlockSpec((tm, k), lambda i, j: (i, 0)),
            pl.BlockSpec((1, k), lambda i, j: (0, 0)),
            w_spec(0), w_spec(1), w_spec(2), w_spec(3),
            pl.BlockSpec((CONV_TAPS, tn), lambda i, j: (0, j)),
        ],
        out_specs=pl.BlockSpec((tm, tn), lambda i, j: (i, j)),
        out_shape=jax.ShapeDtypeStruct((m, width), BF16),
        scratch_shapes=[
            pltpu.VMEM((tm, k), BF16),
            pltpu.VMEM((HALO + tm, tn), F32),
            pltpu.VMEM((nt, HALO, tn), F32),
        ],
        compiler_params=_params(2, VMEM_LIMIT_BIG),
        name="conv_mixer",
    )(x, g.reshape(1, k), w_in, w_in, w_in, w_in, conv_kernel.astype(F32))


def _ssm_operators(lam_re, lam_im, log_dt, b_re, b_im, c_re, c_im, d_skip):
    n_groups = lam_re.shape[0]
    n_oct = n_groups // GROUPS_PER_TILE
    L, C, N = SSM_CHUNK, SSM_GROUP, SSM_STATE
    dt = jnp.exp(log_dt.astype(F32))[None, :]
    lr, li = lam_re.astype(F32).T, lam_im.astype(F32).T
    mag = jnp.exp(lr * dt)
    ab_re, ab_im = mag * jnp.cos(li * dt), mag * jnp.sin(li * dt)
    den = lr * lr + li * li
    nr = ab_re - 1.0
    coef_re = (nr * lr + ab_im * li) / den
    coef_im = (ab_im * lr - nr * li) / den
    br, bi = b_re.astype(F32).transpose(2, 1, 0), b_im.astype(F32).transpose(2, 1, 0)
    bb_re = coef_re * br - coef_im * bi
    bb_im = coef_re * bi + coef_im * br
    cr, ci = c_re.astype(F32).transpose(1, 2, 0), c_im.astype(F32).transpose(1, 2, 0)
    pw_re, pw_im = [jnp.ones_like(ab_re)], [jnp.zeros_like(ab_im)]
    for _ in range(L):
        pw_re, pw_im = (pw_re + [pw_re[-1] * ab_re - pw_im[-1] * ab_im], pw_im + [pw_re[-1] * ab_im + pw_im[-1] * ab_re])
    p_re, p_im = jnp.stack(pw_re), jnp.stack(pw_im)
    w_re = cr[None] * p_re[:L, None] - ci[None] * p_im[:L, None]
    w_im = cr[None] * p_im[:L, None] + ci[None] * p_re[:L, None]
    lag_k = jnp.sum(w_re[:, None] * bb_re[None, :, None] - w_im[:, None] * bb_im[None, :, None], axis=3)
    q_re, q_im = jnp.stack(pw_re[L - 1::-1]), jnp.stack(pw_im[L - 1::-1])
    bc_re = q_re[:, None] * bb_re[None] - q_im[:, None] * bb_im[None]
    bc_im = q_re[:, None] * bb_im[None] + q_im[:, None] * bb_re[None]
    o_re = cr[None] * p_re[1:, None] - ci[None] * p_im[1:, None]
    o_im = cr[None] * p_im[1:, None] + ci[None] * p_re[1:, None]
    octs = lambda t: t.reshape(t.shape[:-1] + (n_oct, GROUPS_PER_TILE))
    to_lag = lambda t: octs(t).transpose(3, 0, 1, 4, 2).reshape(n_oct, L, C, LANES)
    to_rows = lambda t: octs(t).transpose(3, 0, 1, 4, 2).reshape(n_oct, L * C, GROUPS_PER_TILE * N)
    to_cols = lambda t: octs(t).transpose(3, 4, 2, 0, 1).reshape(n_oct, GROUPS_PER_TILE * N, L * C)
    to_vec = lambda t: octs(t).transpose(1, 2, 0).reshape(n_oct, 1, GROUPS_PER_TILE * N)
    return dict(
        lag=to_lag(lag_k).astype(BF16),
        bc_re=to_rows(bc_re).astype(BF16), bc_im=to_rows(bc_im).astype(BF16),
        oc_re=to_cols(o_re).astype(BF16), oc_im_neg=to_cols(-o_im).astype(BF16),
        al_re=to_vec(p_re[L]), al_im=to_vec(p_im[L]), d=d_skip.astype(F32).reshape(n_oct, 1, LANES),
    )


def _ssm_expanders():
    L, C, N, G8 = SSM_CHUNK, SSM_GROUP, SSM_STATE, GROUPS_PER_TILE
    wide = np.arange(L * LANES)
    w_step, w_group, w_chan = wide // LANES, (wide // C) % G8, wide % C
    small = np.arange(L * C)
    s_step, s_chan = small // C, small % C
    state_group = np.arange(G8 * N) // N
    spread = (w_step[:, None] == s_step[None, :]) & (w_chan[:, None] == s_chan[None, :])
    lane = np.arange(LANES)
    as_bf16 = lambda a: jnp.asarray(a.astype(np.float32), dtype=BF16)
    return dict(
        spread_cols=as_bf16(spread.T),
        keep_cols=as_bf16(state_group[:, None] == w_group[None, :]),
        spread_lag=as_bf16((lane % C)[:, None] == np.arange(C)[None, :]),
        keep_lag=jnp.asarray(((lane // C)[:, None] == (lane // C)[None, :]).astype(np.float32)),
    )


def _ssm_kernel(u_ref, lag_ref, bxr_ref, bxi_ref, cxr_ref, cxi_ref, alr_ref, ali_ref, d_ref,
                sc_ref, kc_ref, sl_ref, kl_ref, o_ref,
                ub_ref, bcr_ref, bci_ref, ocr_ref, oci_ref, zr_ref, zi_ref, hr_ref, hi_ref, *, bsz):
    L = SSM_CHUNK
    rows = u_ref.shape[0] // L
    chunks = rows // bsz
    tile = 2 * LANES
    step_rows = lambda r: pl.ds(r, rows, stride=L)
    for r in range(L):
        ub_ref[:, r * LANES:(r + 1) * LANES] = u_ref[step_rows(r), :].astype(BF16)
    state_group = lax.broadcasted_iota(jnp.int32, (1, bxr_ref.shape[1]), 1) // SSM_STATE
    for compact_ref, full_ref in ((bxr_ref, bcr_ref), (bxi_ref, bci_ref)):
        for r in range(L):
            blk = compact_ref[r * SSM_GROUP:(r + 1) * SSM_GROUP, :]
            for grp in range(GROUPS_PER_TILE):
                first = r * LANES + grp * SSM_GROUP
                full_ref[first:first + SSM_GROUP, :] = jnp.where(state_group == grp, blk, jnp.zeros_like(blk))
    zr_ref[...] = _dot(ub_ref[...], bcr_ref[...])
    zi_ref[...] = _dot(ub_ref[...], bci_ref[...])
    a_re, a_im = alr_ref[...], ali_ref[...]

    def step(k, carry):
        new = []
        for b in range(bsz):
            h_re, h_im = carry[2 * b], carry[2 * b + 1]
            row = pl.ds(b * chunks + k, 1)
            hr_ref[row, :] = h_re
            hi_ref[row, :] = h_im
            new.append(a_re * h_re - a_im * h_im + zr_ref[row, :])
            new.append(a_re * h_im + a_im * h_re + zi_ref[row, :])
        return tuple(new)

    zero = jnp.zeros((1, a_re.shape[1]), F32)
    lax.fori_loop(0, chunks, step, (zero,) * (2 * bsz))

    ocr_ref[...] = (_dot(cxr_ref[...], sc_ref[...]) * kc_ref[...]).astype(BF16)
    oci_ref[...] = (_dot(cxi_ref[...], sc_ref[...]) * kc_ref[...]).astype(BF16)
    lag = [(_dot(sl_ref[...], lag_ref[d]) * kl_ref[...]).astype(BF16) for d in range(L)]
    lag_tile = []
    for dd in range(L // 2):
        below = lag[2 * dd - 1] if dd > 0 else jnp.zeros((LANES, LANES), BF16)
        lag_tile.append(jnp.concatenate([jnp.concatenate([lag[2 * dd], lag[2 * dd + 1]], axis=1),
                                         jnp.concatenate([below, lag[2 * dd]], axis=1)], axis=0))
    hb_re, hb_im = hr_ref[...].astype(BF16), hi_ref[...].astype(BF16)
    for t2 in range(L // 2):
        cols = slice(t2 * tile, (t2 + 1) * tile)
        y = _dot(hb_re, ocr_ref[:, cols]) + _dot(hb_im, oci_ref[:, cols])
        for r2 in range(t2 + 1):
            y = y + _dot(ub_ref[:, r2 * tile:(r2 + 1) * tile], lag_tile[t2 - r2])
        for half in range(2):
            t = 2 * t2 + half
            o_ref[step_rows(t), :] = jax.nn.gelu(y[:, half * LANES:(half + 1) * LANES] + d_ref[...] * u_ref[step_rows(t), :])


def _ssm_core(proj, ops, bsz, seq):
    m = proj.shape[0]
    width = proj.shape[1] // 2
    L, C = SSM_CHUNK, SSM_GROUP
    n_oct = width // LANES
    rows = m // L
    ow = L * LANES
    n_state = GROUPS_PER_TILE * SSM_STATE
    ex = _ssm_expanders()
    per_oct = lambda *shape: pl.BlockSpec((None,) + shape, lambda p: (p,) + (0,) * len(shape))
    const = lambda a: pl.BlockSpec(a.shape, lambda p: (0,) * a.ndim, pipeline_mode=pl.Buffered(1))
    consts = [ex['spread_cols'], ex['keep_cols'], ex['spread_lag'], ex['keep_lag']]
    return pl.pallas_call(
        functools.partial(_ssm_kernel, bsz=bsz),
        grid=(n_oct,),
        in_specs=[
            pl.BlockSpec((m, LANES), lambda p: (0, p)),
            per_oct(L, C, LANES), per_oct(L * C, n_state), per_oct(L * C, n_state),
            per_oct(n_state, L * C), per_oct(n_state, L * C), per_oct(1, n_state), per_oct(1, n_state), per_oct(1, LANES),
        ] + [const(a) for a in consts],
        out_specs=pl.BlockSpec((m, LANES), lambda p: (0, p)),
        out_shape=jax.ShapeDtypeStruct((m, width), F32),
        scratch_shapes=[pltpu.VMEM((rows, ow), BF16)] + [pltpu.VMEM((ow, n_state), BF16)] * 2
        + [pltpu.VMEM((n_state, ow), BF16)] * 2 + [pltpu.VMEM((rows, n_state), F32)] * 4,
        compiler_params=_params(1),
        name="ssm_core",
    )(proj, ops['lag'], ops['bc_re'], ops['bc_im'], ops['oc_re'], ops['oc_im_neg'], ops['al_re'], ops['al_im'], ops['d'], *consts)


def _glu_kernel(y_ref, wa_ref, wb_ref, ba_ref, bb_ref, gate_ref, o_ref, yb_ref):
    @pl.when(pl.program_id(1) == 0)
    def _():
        def chunk(rows):
            yb_ref[rows, :] = y_ref[rows, :].astype(BF16)
        _for_row_chunks(yb_ref.shape[0], chunk)

    y = yb_ref[...]
    ga = _dot(y, wa_ref[...]) + ba_ref[...]
    gb = _dot(y, wb_ref[...]) + bb_ref[...]
    o_ref[...] = ((ga * jax.nn.sigmoid(gb)) * _silu(gate_ref[...])).astype(o_ref.dtype)


def _glu(y, w_glu, layer, b_glu, proj, tm=1024, tn=512):
    m, k = y.shape
    width = w_glu.shape[2] // 2
    tm, tn = min(tm, m), min(tn, width)
    nt = width // tn
    w_tiles = _tile_cols(w_glu, layer, tn)
    return pl.pallas_call(
        _glu_kernel,
        grid=(m // tm, nt),
        in_specs=[
            pl.BlockSpec((tm, k), lambda i, j: (i, 0)),
            pl.BlockSpec((None, k, tn), lambda i, j: (j, 0, 0)),
            pl.BlockSpec((None, k, tn), lambda i, j: (nt + j, 0, 0)),
            pl.BlockSpec((1, tn), lambda i, j: (0, j)),
            pl.BlockSpec((1, tn), lambda i, j: (0, nt + j)),
            pl.BlockSpec((tm, tn), lambda i, j: (i, nt + j)),
        ],
        out_specs=pl.BlockSpec((tm, tn), lambda i, j: (i, j)),
        out_shape=jax.ShapeDtypeStruct((m, width), BF16),
        scratch_shapes=[pltpu.VMEM((tm, k), BF16)],
        compiler_params=_params(2),
        name="ssm_glu",
    )(y, w_tiles, w_tiles, b_glu, b_glu, proj)


def _ssm_mixer(x, g, w_in, layer, lam_re, lam_im, log_dt, b_re, b_im, c_re, c_im, d_skip, w_glu, b_glu, bsz, seq):
    proj = _norm_matmul(x, g, _tile_cols(w_in, layer, MATMUL_TN), F32)
    ops = _ssm_operators(lam_re, lam_im, log_dt, b_re, b_im, c_re, c_im, d_skip)
    y = _ssm_core(proj, ops, bsz, seq)
    return _glu(y, w_glu, layer, b_glu.astype(F32).reshape(1, -1), proj)


CUM_BLOCK = 128
ONES_ROWS = 16


def _split3(x):
    x1 = x.astype(BF16)
    r1 = x - x1.astype(F32)
    x2 = r1.astype(BF16)
    x3 = (r1 - x2.astype(F32)).astype(BF16)
    return x1, x2, x3


def _forget_cumsum_kernel(z_ref, b_ref, o_ref):
    n_blocks = z_ref.shape[0] // CUM_BLOCK
    ri = lax.broadcasted_iota(jnp.int32, (CUM_BLOCK, CUM_BLOCK), 0)
    ci = lax.broadcasted_iota(jnp.int32, (CUM_BLOCK, CUM_BLOCK), 1)
    tri = jnp.where(ci <= ri, 1.0, 0.0).astype(BF16)

    def body(i, carry):
        rows = pl.ds(pl.multiple_of(i * CUM_BLOCK, CUM_BLOCK), CUM_BLOCK)
        z = z_ref[rows, :] + b_ref[...]
        log_f = jnp.minimum(z, 0.0) - jnp.log1p(jnp.exp(-jnp.abs(z)))
        x1, x2, x3 = _split3(log_f)
        c = _dot(tri, x1) + _dot(tri, x2) + _dot(tri, x3) + carry
        o_ref[rows, :] = c
        return c[CUM_BLOCK - 1:, :]

    lax.fori_loop(0, n_blocks, body, jnp.zeros((1, z_ref.shape[1]), F32))


def _forget_cumsum(z, b, bsz, seq):
    lanes = z.shape[1]
    return pl.pallas_call(
        _forget_cumsum_kernel,
        grid=(bsz,),
        in_specs=[pl.BlockSpec((seq, lanes), lambda i: (i, 0)), pl.BlockSpec((1, lanes), lambda i: (0, 0))],
        out_specs=pl.BlockSpec((seq, lanes), lambda i: (i, 0)),
        out_shape=jax.ShapeDtypeStruct(z.shape, F32),
        compiler_params=_params(1),
        name="forget_cumsum",
    )(z, b)


def _fox_kernel(q_ref, k_ref, v_ref, sg_ref, cq_ref, ck_ref, o_ref, ka_ref, vt_ref, *head_refs, tk):
    hp, qi = pl.program_id(1), pl.program_id(2)
    tq = q_ref.shape[0]
    qa_ref, s_ref, p_ref, acc_ref, pv_ref = (head_refs[i::5] for i in range(5))
    lane = lax.broadcasted_iota(jnp.int32, (1, LANES), 1)
    free = (HEAD_DIM, 0)
    n_pieces = 3

    def bias_lanes(c, hh, c_offset, one_offset):
        rr = lax.broadcasted_iota(jnp.int32, (n_pieces * LANES, LANES), 0)
        cc = lax.broadcasted_iota(jnp.int32, (n_pieces * LANES, LANES), 1)
        place = jnp.logical_and(rr % LANES == 2 * hp + hh, cc == free[hh] + c_offset + rr // LANES)
        placed = _dot(jnp.concatenate(_split3(c), axis=1), jnp.where(place, 1.0, 0.0).astype(BF16))
        first = free[hh] + one_offset
        return (placed + jnp.where(jnp.logical_and(lane >= first, lane < first + n_pieces), 1.0, 0.0)).astype(BF16)

    @pl.when(qi == 0)
    def _():
        def chunk(j, carry):
            rows = pl.ds(pl.multiple_of(j * tk, tk), tk)
            keys = k_ref[rows, :]
            neg_c = ck_ref[rows, :] * -LOG2E
            for hh in range(2):
                ka_ref[hh, j] = jnp.where((lane // HEAD_DIM) == hh, keys, bias_lanes(neg_c, hh, 0, n_pieces))
            v_t = v_ref[rows, :].astype(F32).T.astype(BF16)
            for hh in range(2):
                vt_ref[j, hh, :HEAD_DIM, :] = v_t[hh * HEAD_DIM:(hh + 1) * HEAD_DIM]
                vt_ref[j, hh, HEAD_DIM:, :] = jnp.ones((ONES_ROWS, tk), BF16)
            return carry
        lax.fori_loop(0, k_ref.shape[0] // tk, chunk, 0)

    cq = cq_ref[...] * LOG2E
    for hh in range(2):
        qa = jnp.where((lane // HEAD_DIM) == hh, q_ref[...], bias_lanes(cq, hh, n_pieces, 0))
        qa_ref[hh][...] = qa.astype(F32).T.astype(BF16)

    def scores(hh, j):
        s_ref[hh][...] = _dot(ka_ref[hh, j], qa_ref[hh][...])

    rows8 = 8
    rows16 = 16

    def absorb(hh, j, m, diagonal):
        def strip(r, n):
            blk = s_ref[hh][r:r + n, :]
            if diagonal:
                kpos = lax.broadcasted_iota(jnp.int32, (n, tq), 0)
                qpos = lax.broadcasted_iota(jnp.int32, (n, tq), 1)
                blk = jnp.where(kpos + r <= qpos, blk, NEG)
            return blk
        top = strip(0, rows8)
        for r in range(rows8, tk, rows8):
            top = jnp.maximum(top, strip(r, rows8))
        m_new = jnp.maximum(m, top.max(axis=0, keepdims=True))
        alpha = jnp.exp2(m - m_new)
        m_rows = jnp.broadcast_to(m_new, (rows16, tq))
        for r in range(0, tk, rows16):
            p_ref[hh][r:r + rows16, :] = jnp.exp2(strip(r, rows16) - m_rows).astype(BF16)
        acc_ref[hh][...] = alpha * (acc_ref[hh][...] + pv_ref[hh][...])
        return m_new

    def values(hh, j):
        pv_ref[hh][...] = _dot(vt_ref[j, hh], p_ref[hh][...])

    def step(j, carry, diagonal):
        m0, m1 = carry
        values(1, jnp.maximum(j - 1, 0))
        scores(1, j)
        m0 = absorb(0, j, m0, diagonal)
        values(0, j)
        if not diagonal:
            scores(0, j + 1)
        m1 = absorb(1, j, m1, diagonal)
        return m0, m1

    for hh in range(2):
        acc_ref[hh][...] = jnp.zeros(acc_ref[hh].shape, F32)
        pv_ref[hh][...] = jnp.zeros(pv_ref[hh].shape, F32)
    p_ref[1][...] = jnp.zeros(p_ref[1].shape, BF16)
    scores(0, 0)
    init = (jnp.full((1, tq), NEG, F32),) * 2
    carry = lax.fori_loop(0, qi, lambda j, c: step(j, c, False), init)
    step(qi, carry, True)
    values(1, qi)
    halves = []
    for hh in range(2):
        total = acc_ref[hh][...] + pv_ref[hh][...]
        halves.append(total[:HEAD_DIM] / total[HEAD_DIM:HEAD_DIM + 1])
    out = jnp.concatenate(halves, axis=0)
    o_ref[...] = (out.T * sg_ref[...].astype(F32)).astype(o_ref.dtype)


def _fox_attention(proj, csum, bsz, seq, t=512):
    width = FOX_HEADS * HEAD_DIM
    n_pairs = width // LANES
    t = min(t, seq)
    nq = seq // t
    return pl.pallas_call(
        functools.partial(_fox_kernel, tk=t),
        grid=(bsz, n_pairs, nq),
        in_specs=[
            pl.BlockSpec((t, LANES), lambda b, h, i: (b * nq + i, h)),
            pl.BlockSpec((seq, LANES), lambda b, h, i: (b, n_pairs + h)),
            pl.BlockSpec((seq, LANES), lambda b, h, i: (b, 2 * n_pairs + h)),
            pl.BlockSpec((t, LANES), lambda b, h, i: (b * nq + i, 3 * n_pairs + h)),
            pl.BlockSpec((t, LANES), lambda b, h, i: (b * nq + i, 0)),
            pl.BlockSpec((seq, LANES), lambda b, h, i: (b, 0)),
        ],
        out_specs=pl.BlockSpec((t, LANES), lambda b, h, i: (b * nq + i, h)),
        out_shape=jax.ShapeDtypeStruct((bsz * seq, width), BF16),
        scratch_shapes=[
            pltpu.VMEM((2, nq, t, LANES), BF16), pltpu.VMEM((nq, 2, HEAD_DIM + ONES_ROWS, t), BF16),
        ] + 2 * [pltpu.VMEM((LANES, t), BF16), pltpu.VMEM((t, t), F32), pltpu.VMEM((t, t), BF16),
                 pltpu.VMEM((HEAD_DIM + ONES_ROWS, t), F32), pltpu.VMEM((HEAD_DIM + ONES_ROWS, t), F32)],
        compiler_params=_params(3),
        name="fox_attention",
    )(proj, proj, proj, proj, csum, csum)


def _fox_mixer(x, g, w_in, layer, w_fg, b_fg, bsz, seq, t=512):
    width = FOX_HEADS * HEAD_DIM
    col_scale = jnp.where(jnp.arange(4 * width) < width, LOG2E * HEAD_DIM ** -0.5, 1.0).astype(F32)
    proj = _norm_matmul(x, g, _tile_cols(w_in, layer, MATMUL_TN, col_scale), BF16, silu_cols=(3 * width, 4 * width))
    pad = LANES - FOX_HEADS
    z = _norm_matmul(x, g, _tile_cols_xla(jnp.pad(w_fg, ((0, 0), (0, pad))), LANES), F32)
    csum = _forget_cumsum(z, jnp.pad(b_fg.astype(F32), (0, pad)).reshape(1, LANES), bsz, seq)
    return _fox_attention(proj, csum, bsz, seq, t)


def kernel(x, p, norm_g, final_g, rel_bias, swa_w_in, swa_w_out, swa_sinks, conv_w_in, conv_kernel, conv_w_out, ssm_w_in, ssm_lam_re, ssm_lam_im, ssm_log_dt, ssm_b_re, ssm_b_im, ssm_c_re, ssm_c_im, ssm_d, ssm_w_glu, ssm_b_glu, ssm_w_out, fox_w_in, fox_w_fg, fox_b_fg, fox_w_out, ple_proj, ple_norm, ple_gate):
    bsz, seq, d_model = x.shape
    depth = p.shape[0]
    h = x.astype(F32).reshape(bsz * seq, d_model)
    for i in range(depth):
        mixer, j = i % N_MIXERS, i // N_MIXERS
        if mixer == 0:
            a, w_out = _swa_mixer(h, norm_g[i], swa_w_in[j], swa_sinks[j], rel_bias, bsz, seq), swa_w_out[j]
        elif mixer == 1:
            a, w_out = _conv_mixer(h, norm_g[i], conv_w_in, j, conv_kernel[j], seq), conv_w_out[j]
        elif mixer == 2:
            a = _ssm_mixer(h, norm_g[i], ssm_w_in, j, ssm_lam_re[j], ssm_lam_im[j], ssm_log_dt[j], ssm_b_re[j], ssm_b_im[j],
                           ssm_c_re[j], ssm_c_im[j], ssm_d[j], ssm_w_glu, ssm_b_glu[j], bsz, seq)
            w_out = ssm_w_out[j]
        else:
            a, w_out = _fox_mixer(h, norm_g[i], fox_w_in, j, fox_w_fg[j], fox_b_fg[j], bsz, seq), fox_w_out[j]
        h = _out_ple(a, w_out, h, ple_norm[i], ple_gate, p.reshape(depth, bsz * seq, -1), ple_proj, i,
                     final_g=final_g if i == depth - 1 else None)
    return h.reshape(bsz, seq, d_model).astype(x.dtype)
```

```python
import functools
import math

import numpy as np
import jax
import jax.numpy as jnp
from jax import lax
from jax.experimental import pallas as pl
from jax.experimental.pallas import tpu as pltpu

F32 = jnp.float32
BF16 = jnp.bfloat16

EPS = 1e-6
N_MIXERS = 4
PLE_DIM = 256

SWA_HEADS = 32
SWA_KV_HEADS = 4
SWA_GROUP = SWA_HEADS // SWA_KV_HEADS
HEAD_DIM = 64
SWA_BLOCK = 128
WINDOW = 128
REL_BUCKETS = 32
REL_MAX_DIST = 128

CONV_TAPS = 3

SSM_GROUP = 16
SSM_STATE = 64
SSM_CHUNK = 16

FOX_HEADS = 32

LANES = 128
GROUPS_PER_TILE = LANES // SSM_GROUP
VMEM_LIMIT = 48 * 1024 * 1024
VMEM_LIMIT_BIG = 60 * 1024 * 1024

NEG = float(jnp.finfo(jnp.float32).min)
LOG2E = math.log2(math.e)


def _params(n_axes, vmem_limit=VMEM_LIMIT):
    return pltpu.CompilerParams(dimension_semantics=("arbitrary",) * n_axes, vmem_limit_bytes=vmem_limit)


def _dot(a, b):
    return jnp.dot(a, b, preferred_element_type=F32)


def _dot_nt(a, b):
    return lax.dot_general(a, b, (((1,), (1,)), ((), ())), preferred_element_type=F32)


def _rmsnorm_rows(x, g):
    return x * lax.rsqrt(jnp.mean(x * x, axis=-1, keepdims=True) + EPS) * g


def _silu(x):
    return x * jax.nn.sigmoid(x)


ROW_CHUNK = 64


def _for_row_chunks(n_rows, fn):
    def body(c, carry):
        fn(pl.ds(pl.multiple_of(c * ROW_CHUNK, ROW_CHUNK), ROW_CHUNK))
        return carry
    lax.fori_loop(0, n_rows // ROW_CHUNK, body, 0)


def _tile_cols_xla(w, tn):
    k, n = w.shape
    return w.astype(BF16).reshape(k, n // tn, tn).transpose(1, 0, 2)


def _cast_tiles_kernel(w_ref, o_ref):
    def chunk(rows):
        o_ref[rows, :] = w_ref[rows, :].astype(BF16)
    _for_row_chunks(o_ref.shape[0], chunk)


def _scale_cast_tiles_kernel(w_ref, s_ref, o_ref):
    def chunk(rows):
        o_ref[rows, :] = (w_ref[rows, :] * s_ref[...]).astype(BF16)
    _for_row_chunks(o_ref.shape[0], chunk)


def _tile_cols(w, layer, tn, col_scale=None):
    _, k, n = w.shape
    tn = min(tn, n)
    w_spec = pl.BlockSpec((None, k, tn), lambda j: (layer, 0, j))
    common = dict(
        grid=(n // tn,),
        out_specs=pl.BlockSpec((None, k, tn), lambda j: (j, 0, 0)),
        out_shape=jax.ShapeDtypeStruct((n // tn, k, tn), BF16),
        compiler_params=_params(1),
        name="weight_tiles",
    )
    if col_scale is None:
        return pl.pallas_call(_cast_tiles_kernel, in_specs=[w_spec], **common)(w)
    s_spec = pl.BlockSpec((1, tn), lambda j: (0, j))
    return pl.pallas_call(_scale_cast_tiles_kernel, in_specs=[w_spec, s_spec], **common)(w, col_scale.reshape(1, n))


def _norm_matmul_kernel(x_ref, g_ref, w_ref, o_ref, a_ref, *, silu_tiles):
    j = pl.program_id(1)

    @pl.when(j == 0)
    def _():
        def chunk(rows):
            a_ref[rows, :] = _rmsnorm_rows(x_ref[rows, :], g_ref[...]).astype(BF16)
        _for_row_chunks(a_ref.shape[0], chunk)

    acc = _dot(a_ref[...], w_ref[...])
    if silu_tiles is None:
        o_ref[...] = acc.astype(o_ref.dtype)
    else:
        gated = jnp.logical_and(j >= silu_tiles[0], j < silu_tiles[1])

        @pl.when(gated)
        def _():
            o_ref[...] = _silu(acc).astype(o_ref.dtype)

        @pl.when(jnp.logical_not(gated))
        def _():
            o_ref[...] = acc.astype(o_ref.dtype)


MATMUL_TN = 1024


def _norm_matmul(x, g, w_tiles, out_dtype, silu_cols=None, tm=1024):
    m, k = x.shape
    tn = w_tiles.shape[2]
    n = w_tiles.shape[0] * tn
    tm = min(tm, m)
    silu_tiles = None if silu_cols is None else (silu_cols[0] // tn, silu_cols[1] // tn)
    return pl.pallas_call(
        functools.partial(_norm_matmul_kernel, silu_tiles=silu_tiles),
        grid=(m // tm, n // tn),
        in_specs=[
            pl.BlockSpec((tm, k), lambda i, j: (i, 0)),
            pl.BlockSpec((1, k), lambda i, j: (0, 0)),
            pl.BlockSpec((None, k, tn), lambda i, j: (j, 0, 0)),
        ],
        out_specs=pl.BlockSpec((tm, tn), lambda i, j: (i, j)),
        out_shape=jax.ShapeDtypeStruct((m, n), out_dtype),
        scratch_shapes=[pltpu.VMEM((tm, k), BF16)],
        compiler_params=_params(2),
        name="norm_matmul",
    )(x, g.reshape(1, k), w_tiles)


def _out_ple_kernel(a_ref, wo_ref, x_ref, g_ref, wg_ref, p_ref, wp_ref, gf_ref, o_ref, x1_ref, hn_ref, ss_ref, *, nt, final):
    j = pl.program_id(1)
    tn = x_ref.shape[1]
    n = nt * tn

    @pl.when(j == 0)
    def _():
        ss_ref[...] = jnp.zeros(ss_ref.shape, F32)

    @pl.when(j < nt)
    def _():
        x1 = x_ref[...] + _dot(a_ref[...], wo_ref[j])
        x1_ref[j] = x1
        hn_ref[j] = (x1 * g_ref[j]).astype(BF16)
        ss_ref[0] += jnp.sum(x1 * x1, axis=-1, keepdims=True)

    @pl.when(jnp.logical_and(j >= nt, j < 2 * nt))
    def _():
        t = j - nt
        emb = _dot(p_ref[...].astype(BF16), wp_ref[t])
        acc = _dot(hn_ref[0], wg_ref[t, :tn, :])
        for kt in range(1, nt):
            acc = acc + _dot(hn_ref[kt], wg_ref[t, kt * tn:(kt + 1) * tn, :])
        x2 = x1_ref[t] + emb * jax.nn.sigmoid(acc * lax.rsqrt(ss_ref[0] / n + EPS))
        if final:
            x1_ref[t] = x2
            ss_ref[1] += jnp.sum(x2 * x2, axis=-1, keepdims=True)
        else:
            o_ref[...] = x2

    if final:
        @pl.when(j >= 2 * nt)
        def _():
            t = j - 2 * nt
            o_ref[...] = x1_ref[t] * lax.rsqrt(ss_ref[1] / n + EPS) * gf_ref[t]


def _out_ple(a, w_out, x, g, w_gate, p, w_proj, layer, final_g=None, tm=1024, tn=512):
    m, k = a.shape
    n = w_out.shape[1]
    pd = p.shape[2]
    final = final_g is not None
    tm, tn = min(tm, m), min(tn, n)
    nt = n // tn
    passes = 3 if final else 2
    out_tile = lambda j: jnp.maximum(j - (passes - 1) * nt, 0)
    resident = lambda rows: pl.BlockSpec((nt, rows, tn), lambda i, j: (0, 0, 0), pipeline_mode=pl.Buffered(1))
    row_vec = pl.BlockSpec((nt, 1, tn), lambda i, j: (0, 0, 0))
    return pl.pallas_call(
        functools.partial(_out_ple_kernel, nt=nt, final=final),
        grid=(m // tm, passes * nt),
        in_specs=[
            pl.BlockSpec((tm, k), lambda i, j: (i, 0)),
            resident(k),
            pl.BlockSpec((tm, tn), lambda i, j: (i, jnp.minimum(j, nt - 1))),
            row_vec,
            resident(n),
            pl.BlockSpec((None, tm, pd), lambda i, j: (layer, i, 0)),
            resident(pd),
            row_vec,
        ],
        out_specs=pl.BlockSpec((tm, tn), lambda i, j: (i, out_tile(j))),
        out_shape=jax.ShapeDtypeStruct((m, n), F32),
        scratch_shapes=[pltpu.VMEM((nt, tm, tn), F32), pltpu.VMEM((nt, tm, tn), BF16), pltpu.VMEM((2, tm, 1), F32)],
        compiler_params=_params(2, VMEM_LIMIT_BIG),
        name="out_ple_final" if final else "out_ple",
    )(a, _tile_cols(w_out[None], 0, tn), x, g.reshape(nt, 1, tn), _tile_cols(w_gate, layer, tn), p,
      _tile_cols(w_proj, layer, tn), (final_g if final else g).reshape(nt, 1, tn))


def _t5_bucket(dist):
    max_exact = REL_BUCKETS // 2
    d = np.maximum(dist, 1).astype(np.float32)
    large = max_exact + (np.log(d / max_exact) / np.log(REL_MAX_DIST / max_exact) * (REL_BUCKETS - max_exact)).astype(np.int32)
    large = np.minimum(large, REL_BUCKETS - 1)
    return np.where(dist < max_exact, dist, large).astype(np.int32)


def _swa_bucket_table():
    qi = np.arange(SWA_BLOCK)[None, :]
    kj = np.arange(2 * SWA_BLOCK)[:, None]
    return _t5_bucket(np.clip(qi + SWA_BLOCK - kj, 0, None))


def _swa_kernel(sink_ref, rel_ref, bucket_ref, q_ref, sg_ref, kp_ref, kc_ref, vp_ref, vc_ref, o_ref, bias_ref, qs_ref):
    n = pl.program_id(1)
    blk = SWA_BLOCK
    lane = lax.broadcasted_iota(jnp.int32, (1, LANES), 1)

    @pl.when(jnp.logical_and(pl.program_id(0) == 0, n == 0))
    def _():
        bucket = bucket_ref[...]
        kj = lax.broadcasted_iota(jnp.int32, (2 * blk, blk), 0)
        qi = lax.broadcasted_iota(jnp.int32, (2 * blk, blk), 1)
        band = jnp.logical_or(jnp.logical_and(kj < blk, kj > qi), jnp.logical_and(kj >= blk, kj - blk <= qi))

        def per_head(head, carry):
            acc = jnp.zeros((2 * blk, blk), F32)
            for b in range(REL_BUCKETS):
                acc = jnp.where(bucket == b, rel_ref[b, head], acc)
            bias_ref[head] = jnp.where(band, acc, NEG)
            return carry
        lax.fori_loop(0, SWA_HEADS, per_head, 0)

    for kvh in range(SWA_KV_HEADS):
        cols = slice(kvh * LANES, (kvh + 1) * LANES)
        heads = range(kvh * SWA_GROUP, (kvh + 1) * SWA_GROUP)
        for g, head in enumerate(heads):
            q_pair = q_ref[:, head // 2 * LANES:(head // 2 + 1) * LANES]
            qs_ref[g * blk:(g + 1) * blk, :] = jnp.where((lane // HEAD_DIM) == head % 2, q_pair, jnp.zeros_like(q_pair))
        keys = jnp.concatenate([kp_ref[:, cols], kc_ref[:, cols]], axis=0)
        vals = jnp.concatenate([vp_ref[:, cols], vc_ref[:, cols]], axis=0)
        s = _dot_nt(keys, qs_ref[...]) + jnp.concatenate([bias_ref[head] for head in heads], axis=1)
        s_prev = jnp.where(n > 0, s[:blk], NEG)
        s_cur = s[blk:]
        sink = jnp.concatenate([jnp.full((1, blk), sink_ref[head], F32) for head in heads], axis=1)
        m = jnp.maximum(jnp.maximum(s_prev, s_cur).max(axis=0, keepdims=True), sink)
        e_prev, e_cur = jnp.exp(s_prev - m), jnp.exp(s_cur - m)
        den = e_prev.sum(axis=0, keepdims=True) + e_cur.sum(axis=0, keepdims=True) + jnp.exp(sink - m)
        e = jnp.concatenate([e_prev, e_cur], axis=0).astype(BF16)
        out = _dot(vals.astype(F32).T.astype(BF16), e) / den
        for pair in range(SWA_GROUP // 2):
            head = kvh * SWA_GROUP + 2 * pair
            even = out[:HEAD_DIM, 2 * pair * blk:(2 * pair + 1) * blk]
            odd = out[HEAD_DIM:, (2 * pair + 1) * blk:(2 * pair + 2) * blk]
            cols = slice(head // 2 * LANES, (head // 2 + 1) * LANES)
            o_ref[:, cols] = (jnp.concatenate([even, odd], axis=0).T * sg_ref[:, cols].astype(F32)).astype(o_ref.dtype)


def _swa_attention(proj, sinks, rel_bias, bsz, seq):
    nb = seq // SWA_BLOCK
    width = SWA_HEADS * HEAD_DIM
    kvw = SWA_KV_HEADS * LANES
    k_blk = 2 * width // kvw
    cur = lambda b, n: b * nb + n
    prev = lambda b, n: b * nb + jnp.maximum(n - 1, 0)
    smem = pl.BlockSpec(memory_space=pltpu.SMEM)
    return pl.pallas_call(
        _swa_kernel,
        grid=(bsz, nb),
        in_specs=[
            smem, smem,
            pl.BlockSpec((2 * SWA_BLOCK, SWA_BLOCK), lambda b, n: (0, 0)),
            pl.BlockSpec((SWA_BLOCK, width), lambda b, n: (cur(b, n), 0)),
            pl.BlockSpec((SWA_BLOCK, width), lambda b, n: (cur(b, n), 1)),
            pl.BlockSpec((SWA_BLOCK, kvw), lambda b, n: (prev(b, n), k_blk)),
            pl.BlockSpec((SWA_BLOCK, kvw), lambda b, n: (cur(b, n), k_blk)),
            pl.BlockSpec((SWA_BLOCK, kvw), lambda b, n: (prev(b, n), k_blk + 1)),
            pl.BlockSpec((SWA_BLOCK, kvw), lambda b, n: (cur(b, n), k_blk + 1)),
        ],
        out_specs=pl.BlockSpec((SWA_BLOCK, width), lambda b, n: (cur(b, n), 0)),
        out_shape=jax.ShapeDtypeStruct((bsz * seq, width), BF16),
        scratch_shapes=[
            pltpu.VMEM((SWA_HEADS, 2 * SWA_BLOCK, SWA_BLOCK), F32),
            pltpu.VMEM((SWA_GROUP * SWA_BLOCK, LANES), BF16),
        ],
        compiler_params=_params(2),
        name="swa_attention",
    )(sinks.astype(F32), rel_bias.astype(F32), jnp.asarray(_swa_bucket_table()), proj, proj, proj, proj, proj, proj)


def _dup_heads(w, n_heads):
    k = w.shape[0]
    w = w.reshape(k, n_heads, 1, HEAD_DIM)
    return jnp.broadcast_to(w, (k, n_heads, 2, HEAD_DIM)).reshape(k, n_heads * 2 * HEAD_DIM)


def _swa_mixer(x, g, w_in, sinks, rel_bias, bsz, seq):
    width = SWA_HEADS * HEAD_DIM
    kvw = SWA_KV_HEADS * HEAD_DIM
    w_q, w_k, w_v, w_g = (w_in[:, :width], w_in[:, width:width + kvw],
                          w_in[:, width + kvw:width + 2 * kvw], w_in[:, width + 2 * kvw:])
    w = jnp.concatenate([w_q * (HEAD_DIM ** -0.5), w_g, _dup_heads(w_k, SWA_KV_HEADS), _dup_heads(w_v, SWA_KV_HEADS)], axis=1)
    proj = _norm_matmul(x, g, _tile_cols_xla(w, MATMUL_TN), BF16, silu_cols=(width, 2 * width))
    return _swa_attention(proj, sinks, rel_bias, bsz, seq)


HALO = 8


def _conv_kernel(x_ref, g_ref, wb_ref, wc_ref, wu_ref, wg_ref, ck_ref, o_ref, a_ref, z_ref, carry_ref, *, tiles_per_seq):
    i, j = pl.program_id(0), pl.program_id(1)
    tm = a_ref.shape[0]

    @pl.when(j == 0)
    def _():
        def chunk(rows):
            a_ref[rows, :] = _rmsnorm_rows(x_ref[rows, :], g_ref[...]).astype(BF16)
        _for_row_chunks(tm, chunk)

    a = a_ref[...]
    z = _dot(a, wc_ref[...]) * _dot(a, wu_ref[...])
    first = (i % tiles_per_seq) == 0

    @pl.when(first)
    def _():
        z_ref[:HALO, :] = jnp.zeros((HALO, z_ref.shape[1]), F32)

    @pl.when(jnp.logical_not(first))
    def _():
        z_ref[:HALO, :] = carry_ref[j]

    z_ref[HALO:, :] = z
    carry_ref[j] = z[tm - HALO:, :]
    conv = z_ref[HALO - 2:HALO - 2 + tm, :] * ck_ref[0:1, :]
    conv = conv + z_ref[HALO - 1:HALO - 1 + tm, :] * ck_ref[1:2, :]
    conv = conv + z * ck_ref[2:3, :]
    y = _dot(a, wb_ref[...]) * conv
    o_ref[...] = (y * _silu(_dot(a, wg_ref[...]))).astype(o_ref.dtype)


def _conv_mixer(x, g, w_in, layer, conv_kernel, seq, tm=1024, tn=512):
    m, k = x.shape
    width = w_in.shape[2] // 4
    tm, tn = min(tm, seq), min(tn, width)
    nt = width // tn
    w_spec = lambda q: pl.BlockSpec((None, k, tn), lambda i, j: (q * nt + j, 0, 0))
    w_in = _tile_cols(w_in, layer, tn)
    return pl.pallas_call(
        functools.partial(_conv_kernel, tiles_per_seq=seq // tm),
        grid=(m // tm, nt),
        in_specs=[
            pl.BlockSpec((tm, k), lambda i, j: (i, 0)),
            pl.BlockSpec((1, k), lambda i, j: (0, 0)),
            w_spec(0), w_spec(1), w_spec(2), w_spec(3),
            pl.BlockSpec((CONV_TAPS, tn), lambda i, j: (0, j)),
        ],
        out_specs=pl.BlockSpec((tm, tn), lambda i, j: (i, j)),
        out_shape=jax.ShapeDtypeStruct((m, width), BF16),
        scratch_shapes=[
            pltpu.VMEM((tm, k), BF16),
            pltpu.VMEM((HALO + tm, tn), F32),
            pltpu.VMEM((nt, HALO, tn), F32),
        ],
        compiler_params=_params(2, VMEM_LIMIT_BIG),
        name="conv_mixer",
    )(x, g.reshape(1, k), w_in, w_in, w_in, w_in, conv_kernel.astype(F32))


def _ssm_operators(lam_re, lam_im, log_dt, b_re, b_im, c_re, c_im, d_skip):
    n_groups = lam_re.shape[0]
    n_oct = n_groups // GROUPS_PER_TILE
    L, C, N = SSM_CHUNK, SSM_GROUP, SSM_STATE
    dt = jnp.exp(log_dt.astype(F32))[None, :]
    lr, li = lam_re.astype(F32).T, lam_im.astype(F32).T
    mag = jnp.exp(lr * dt)
    ab_re, ab_im = mag * jnp.cos(li * dt), mag * jnp.sin(li * dt)
    den = lr * lr + li * li
    nr = ab_re - 1.0
    coef_re = (nr * lr + ab_im * li) / den
    coef_im = (ab_im * lr - nr * li) / den
    br, bi = b_re.astype(F32).transpose(2, 1, 0), b_im.astype(F32).transpose(2, 1, 0)
    bb_re = coef_re * br - coef_im * bi
    bb_im = coef_re * bi + coef_im * br
    cr, ci = c_re.astype(F32).transpose(1, 2, 0), c_im.astype(F32).transpose(1, 2, 0)
    pw_re, pw_im = [jnp.ones_like(ab_re)], [jnp.zeros_like(ab_im)]
    for _ in range(L):
        pw_re, pw_im = (pw_re + [pw_re[-1] * ab_re - pw_im[-1] * ab_im], pw_im + [pw_re[-1] * ab_im + pw_im[-1] * ab_re])
    p_re, p_im = jnp.stack(pw_re), jnp.stack(pw_im)
    w_re = cr[None] * p_re[:L, None] - ci[None] * p_im[:L, None]
    w_im = cr[None] * p_im[:L, None] + ci[None] * p_re[:L, None]
    lag_k = jnp.sum(w_re[:, None] * bb_re[None, :, None] - w_im[:, None] * bb_im[None, :, None], axis=3)
    q_re, q_im = jnp.stack(pw_re[L - 1::-1]), jnp.stack(pw_im[L - 1::-1])
    bc_re = q_re[:, None] * bb_re[None] - q_im[:, None] * bb_im[None]
    bc_im = q_re[:, None] * bb_im[None] + q_im[:, None] * bb_re[None]
    o_re = cr[None] * p_re[1:, None] - ci[None] * p_im[1:, None]
    o_im = cr[None] * p_im[1:, None] + ci[None] * p_re[1:, None]
    octs = lambda t: t.reshape(t.shape[:-1] + (n_oct, GROUPS_PER_TILE))
    to_lag = lambda t: octs(t).transpose(3, 0, 1, 4, 2).reshape(n_oct, L, C, LANES)
    to_rows = lambda t: octs(t).transpose(3, 0, 1, 4, 2).reshape(n_oct, L * C, GROUPS_PER_TILE * N)
    to_cols = lambda t: octs(t).transpose(3, 4, 2, 0, 1).reshape(n_oct, GROUPS_PER_TILE * N, L * C)
    to_vec = lambda t: octs(t).transpose(1, 2, 0).reshape(n_oct, 1, GROUPS_PER_TILE * N)
    return dict(
        lag=to_lag(lag_k).astype(BF16),
        bc_re=to_rows(bc_re).astype(BF16), bc_im=to_rows(bc_im).astype(BF16),
        oc_re=to_cols(o_re).astype(BF16), oc_im_neg=to_cols(-o_im).astype(BF16),
        al_re=to_vec(p_re[L]), al_im=to_vec(p_im[L]), d=d_skip.astype(F32).reshape(n_oct, 1, LANES),
    )


def _ssm_expanders():
    L, C, N, G8 = SSM_CHUNK, SSM_GROUP, SSM_STATE, GROUPS_PER_TILE
    wide = np.arange(L * LANES)
    w_step, w_group, w_chan = wide // LANES, (wide // C) % G8, wide % C
    small = np.arange(L * C)
    s_step, s_chan = small // C, small % C
    state_group = np.arange(G8 * N) // N
    spread = (w_step[:, None] == s_step[None, :]) & (w_chan[:, None] == s_chan[None, :])
    lane = np.arange(LANES)
    as_bf16 = lambda a: jnp.asarray(a.astype(np.float32), dtype=BF16)
    return dict(
        spread_cols=as_bf16(spread.T),
        keep_cols=as_bf16(state_group[:, None] == w_group[None, :]),
        spread_lag=as_bf16((lane % C)[:, None] == np.arange(C)[None, :]),
        keep_lag=jnp.asarray(((lane // C)[:, None] == (lane // C)[None, :]).astype(np.float32)),
    )


def _ssm_kernel(u_ref, lag_ref, bxr_ref, bxi_ref, cxr_ref, cxi_ref, alr_ref, ali_ref, d_ref,
                sc_ref, kc_ref, sl_ref, kl_ref, o_ref,
                ub_ref, bcr_ref, bci_ref, ocr_ref, oci_ref, zr_ref, zi_ref, hr_ref, hi_ref, *, bsz):
    L = SSM_CHUNK
    rows = u_ref.shape[0] // L
    chunks = rows // bsz
    tile = 2 * LANES
    step_rows = lambda r: pl.ds(r, rows, stride=L)
    for r in range(L):
        ub_ref[:, r * LANES:(r + 1) * LANES] = u_ref[step_rows(r), :].astype(BF16)
    state_group = lax.broadcasted_iota(jnp.int32, (1, bxr_ref.shape[1]), 1) // SSM_STATE
    for compact_ref, full_ref in ((bxr_ref, bcr_ref), (bxi_ref, bci_ref)):
        for r in range(L):
            blk = compact_ref[r * SSM_GROUP:(r + 1) * SSM_GROUP, :]
            for grp in range(GROUPS_PER_TILE):
                first = r * LANES + grp * SSM_GROUP
                full_ref[first:first + SSM_GROUP, :] = jnp.where(state_group == grp, blk, jnp.zeros_like(blk))
    zr_ref[...] = _dot(ub_ref[...], bcr_ref[...])
    zi_ref[...] = _dot(ub_ref[...], bci_ref[...])
    a_re, a_im = alr_ref[...], ali_ref[...]

    def step(k, carry):
        new = []
        for b in range(bsz):
            h_re, h_im = carry[2 * b], carry[2 * b + 1]
            row = pl.ds(b * chunks + k, 1)
            hr_ref[row, :] = h_re
            hi_ref[row, :] = h_im
            new.append(a_re * h_re - a_im * h_im + zr_ref[row, :])
            new.append(a_re * h_im + a_im * h_re + zi_ref[row, :])
        return tuple(new)

    zero = jnp.zeros((1, a_re.shape[1]), F32)
    lax.fori_loop(0, chunks, step, (zero,) * (2 * bsz))

    ocr_ref[...] = (_dot(cxr_ref[...], sc_ref[...]) * kc_ref[...]).astype(BF16)
    oci_ref[...] = (_dot(cxi_ref[...], sc_ref[...]) * kc_ref[...]).astype(BF16)
    lag = [(_dot(sl_ref[...], lag_ref[d]) * kl_ref[...]).astype(BF16) for d in range(L)]
    lag_tile = []
    for dd in range(L // 2):
        below = lag[2 * dd - 1] if dd > 0 else jnp.zeros((LANES, LANES), BF16)
        lag_tile.append(jnp.concatenate([jnp.concatenate([lag[2 * dd], lag[2 * dd + 1]], axis=1),
                                         jnp.concatenate([below, lag[2 * dd]], axis=1)], axis=0))
    hb_re, hb_im = hr_ref[...].astype(BF16), hi_ref[...].astype(BF16)
    for t2 in range(L // 2):
        cols = slice(t2 * tile, (t2 + 1) * tile)
        y = _dot(hb_re, ocr_ref[:, cols]) + _dot(hb_im, oci_ref[:, cols])
        for r2 in range(t2 + 1):
            y = y + _dot(ub_ref[:, r2 * tile:(r2 + 1) * tile], lag_tile[t2 - r2])
        for half in range(2):
            t = 2 * t2 + half
            o_ref[step_rows(t), :] = jax.nn.gelu(y[:, half * LANES:(half + 1) * LANES] + d_ref[...] * u_ref[step_rows(t), :])


def _ssm_core(proj, ops, bsz, seq):
    m = proj.shape[0]
    width = proj.shape[1] // 2
    L, C = SSM_CHUNK, SSM_GROUP
    n_oct = width // LANES
    rows = m // L
    ow = L * LANES
    n_state = GROUPS_PER_TILE * SSM_STATE
    ex = _ssm_expanders()
    per_oct = lambda *shape: pl.BlockSpec((None,) + shape, lambda p: (p,) + (0,) * len(shape))
    const = lambda a: pl.BlockSpec(a.shape, lambda p: (0,) * a.ndim, pipeline_mode=pl.Buffered(1))
    consts = [ex['spread_cols'], ex['keep_cols'], ex['spread_lag'], ex['keep_lag']]
    return pl.pallas_call(
        functools.partial(_ssm_kernel, bsz=bsz),
        grid=(n_oct,),
        in_specs=[
            pl.BlockSpec((m, LANES), lambda p: (0, p)),
            per_oct(L, C, LANES), per_oct(L * C, n_state), per_oct(L * C, n_state),
            per_oct(n_state, L * C), per_oct(n_state, L * C), per_oct(1, n_state), per_oct(1, n_state), per_oct(1, LANES),
        ] + [const(a) for a in consts],
        out_specs=pl.BlockSpec((m, LANES), lambda p: (0, p)),
        out_shape=jax.ShapeDtypeStruct((m, width), F32),
        scratch_shapes=[pltpu.VMEM((rows, ow), BF16)] + [pltpu.VMEM((ow, n_state), BF16)] * 2
        + [pltpu.VMEM((n_state, ow), BF16)] * 2 + [pltpu.VMEM((rows, n_state), F32)] * 4,
        compiler_params=_params(1),
        name="ssm_core",
    )(proj, ops['lag'], ops['bc_re'], ops['bc_im'], ops['oc_re'], ops['oc_im_neg'], ops['al_re'], ops['al_im'], ops['d'], *consts)


def _glu_kernel(y_ref, wa_ref, wb_ref, ba_ref, bb_ref, gate_ref, o_ref, yb_ref):
    @pl.when(pl.program_id(1) == 0)
    def _():
        def chunk(rows):
            yb_ref[rows, :] = y_ref[rows, :].astype(BF16)
        _for_row_chunks(yb_ref.shape[0], chunk)

    y = yb_ref[...]
    ga = _dot(y, wa_ref[...]) + ba_ref[...]
    gb = _dot(y, wb_ref[...]) + bb_ref[...]
    o_ref[...] = ((ga * jax.nn.sigmoid(gb)) * _silu(gate_ref[...])).astype(o_ref.dtype)


def _glu(y, w_glu, layer, b_glu, proj, tm=1024, tn=512):
    m, k = y.shape
    width = w_glu.shape[2] // 2
    tm, tn = min(tm, m), min(tn, width)
    nt = width // tn
    w_tiles = _tile_cols(w_glu, layer, tn)
    return pl.pallas_call(
        _glu_kernel,
        grid=(m // tm, nt),
        in_specs=[
            pl.BlockSpec((tm, k), lambda i, j: (i, 0)),
            pl.BlockSpec((None, k, tn), lambda i, j: (j, 0, 0)),
            pl.BlockSpec((None, k, tn), lambda i, j: (nt + j, 0, 0)),
            pl.BlockSpec((1, tn), lambda i, j: (0, j)),
            pl.BlockSpec((1, tn), lambda i, j: (0, nt + j)),
            pl.BlockSpec((tm, tn), lambda i, j: (i, nt + j)),
        ],
        out_specs=pl.BlockSpec((tm, tn), lambda i, j: (i, j)),
        out_shape=jax.ShapeDtypeStruct((m, width), BF16),
        scratch_shapes=[pltpu.VMEM((tm, k), BF16)],
        compiler_params=_params(2),
        name="ssm_glu",
    )(y, w_tiles, w_tiles, b_glu, b_glu, proj)


def _ssm_mixer(x, g, w_in, layer, lam_re, lam_im, log_dt, b_re, b_im, c_re, c_im, d_skip, w_glu, b_glu, bsz, seq):
    proj = _norm_matmul(x, g, _tile_cols(w_in, layer, MATMUL_TN), F32)
    ops = _ssm_operators(lam_re, lam_im, log_dt, b_re, b_im, c_re, c_im, d_skip)
    y = _ssm_core(proj, ops, bsz, seq)
    return _glu(y, w_glu, layer, b_glu.astype(F32).reshape(1, -1), proj)


CUM_BLOCK = 128
ONES_ROWS = 16


def _split3(x):
    x1 = x.astype(BF16)
    r1 = x - x1.astype(F32)
    x2 = r1.astype(BF16)
    x3 = (r1 - x2.astype(F32)).astype(BF16)
    return x1, x2, x3


def _forget_cumsum_kernel(z_ref, b_ref, o_ref):
    n_blocks = z_ref.shape[0] // CUM_BLOCK
    ri = lax.broadcasted_iota(jnp.int32, (CUM_BLOCK, CUM_BLOCK), 0)
    ci = lax.broadcasted_iota(jnp.int32, (CUM_BLOCK, CUM_BLOCK), 1)
    tri = jnp.where(ci <= ri, 1.0, 0.0).astype(BF16)

    def body(i, carry):
        rows = pl.ds(pl.multiple_of(i * CUM_BLOCK, CUM_BLOCK), CUM_BLOCK)
        z = z_ref[rows, :] + b_ref[...]
        log_f = jnp.minimum(z, 0.0) - jnp.log1p(jnp.exp(-jnp.abs(z)))
        x1, x2, x3 = _split3(log_f)
        c = _dot(tri, x1) + _dot(tri, x2) + _dot(tri, x3) + carry
        o_ref[rows, :] = c
        return c[CUM_BLOCK - 1:, :]

    lax.fori_loop(0, n_blocks, body, jnp.zeros((1, z_ref.shape[1]), F32))


def _forget_cumsum(z, b, bsz, seq):
    lanes = z.shape[1]
    return pl.pallas_call(
        _forget_cumsum_kernel,
        grid=(bsz,),
        in_specs=[pl.BlockSpec((seq, lanes), lambda i: (i, 0)), pl.BlockSpec((1, lanes), lambda i: (0, 0))],
        out_specs=pl.BlockSpec((seq, lanes), lambda i: (i, 0)),
        out_shape=jax.ShapeDtypeStruct(z.shape, F32),
        compiler_params=_params(1),
        name="forget_cumsum",
    )(z, b)


def _fox_kernel(q_ref, k_ref, v_ref, sg_ref, cq_ref, ck_ref, o_ref, ka_ref, vt_ref, *head_refs, tk):
    hp, qi = pl.program_id(1), pl.program_id(2)
    tq = q_ref.shape[0]
    qa_ref, s_ref, p_ref, acc_ref, pv_ref, m_ref = (head_refs[i::6] for i in range(6))
    lane = lax.broadcasted_iota(jnp.int32, (1, LANES), 1)
    free = (HEAD_DIM, 0)
    n_pieces = 3

    def bias_lanes(c, hh, c_offset, one_offset):
        rr = lax.broadcasted_iota(jnp.int32, (n_pieces * LANES, LANES), 0)
        cc = lax.broadcasted_iota(jnp.int32, (n_pieces * LANES, LANES), 1)
        place = jnp.logical_and(rr % LANES == 2 * hp + hh, cc == free[hh] + c_offset + rr // LANES)
        placed = _dot(jnp.concatenate(_split3(c), axis=1), jnp.where(place, 1.0, 0.0).astype(BF16))
        first = free[hh] + one_offset
        return (placed + jnp.where(jnp.logical_and(lane >= first, lane < first + n_pieces), 1.0, 0.0)).astype(BF16)

    @pl.when(qi == 0)
    def _():
        def chunk(j, carry):
            rows = pl.ds(pl.multiple_of(j * tk, tk), tk)
            keys = k_ref[rows, :]
            neg_c = ck_ref[rows, :] * -LOG2E
            for hh in range(2):
                ka_ref[hh, j] = jnp.where((lane // HEAD_DIM) == hh, keys, bias_lanes(neg_c, hh, 0, n_pieces))
            v_t = v_ref[rows, :].astype(F32).T.astype(BF16)
            for hh in range(2):
                vt_ref[j, hh, :HEAD_DIM, :] = v_t[hh * HEAD_DIM:(hh + 1) * HEAD_DIM]
                vt_ref[j, hh, HEAD_DIM:, :] = jnp.ones((ONES_ROWS, tk), BF16)
            return carry
        lax.fori_loop(0, k_ref.shape[0] // tk, chunk, 0)

    cq = cq_ref[...] * LOG2E
    for hh in range(2):
        qa = jnp.where((lane // HEAD_DIM) == hh, q_ref[...], bias_lanes(cq, hh, n_pieces, 0))
        qa_ref[hh][...] = qa.astype(F32).T.astype(BF16)

    def scores(hh, j):
        s_ref[hh][...] = _dot(ka_ref[hh, j], qa_ref[hh][...])

    rows8 = 8
    rows16 = 16

    def absorb(hh, j, diagonal):
        def strip(r, n):
            blk = s_ref[hh][r:r + n, :]
            if diagonal:
                kpos = lax.broadcasted_iota(jnp.int32, (n, tq), 0)
                qpos = lax.broadcasted_iota(jnp.int32, (n, tq), 1)
                blk = jnp.where(kpos + r <= qpos, blk, NEG)
            return blk
        top = strip(0, rows8)
        for r in range(rows8, tk, rows8):
            top = jnp.maximum(top, strip(r, rows8))
        m = m_ref[hh][...]
        m_new = jnp.maximum(m, top.max(axis=0, keepdims=True))
        m_ref[hh][...] = m_new
        alpha = jnp.exp2(m - m_new)
        m_rows = jnp.broadcast_to(m_new, (rows16, tq))
        for r in range(0, tk, rows16):
            p_ref[hh][r:r + rows16, :] = jnp.exp2(strip(r, rows16) - m_rows).astype(BF16)
        acc_ref[hh][...] = alpha * (acc_ref[hh][...] + pv_ref[hh][...])

    def values(hh, j):
        pv_ref[hh][...] = _dot(vt_ref[j, hh], p_ref[hh][...])

    def step(j, diagonal):
        values(1, jnp.maximum(j - 1, 0))
        scores(1, j)
        absorb(0, j, diagonal)
        values(0, j)
        if not diagonal:
            scores(0, j + 1)
        absorb(1, j, diagonal)

    for hh in range(2):
        acc_ref[hh][...] = jnp.zeros(acc_ref[hh].shape, F32)
        pv_ref[hh][...] = jnp.zeros(pv_ref[hh].shape, F32)
        m_ref[hh][...] = jnp.full(m_ref[hh].shape, NEG, F32)
    p_ref[1][...] = jnp.zeros(p_ref[1].shape, BF16)
    scores(0, 0)

    def two_blocks(i, carry):
        step(2 * i, False)
        step(2 * i + 1, False)
        return carry
    lax.fori_loop(0, qi // 2, two_blocks, 0)

    @pl.when(qi % 2 == 0)
    def _():
        step(qi, True)

    @pl.when(qi % 2 == 1)
    def _():
        step(qi - 1, False)
        step(qi, True)

    values(1, qi)
    halves = []
    for hh in range(2):
        total = acc_ref[hh][...] + pv_ref[hh][...]
        halves.append(total[:HEAD_DIM] / total[HEAD_DIM:HEAD_DIM + 1])
    out = jnp.concatenate(halves, axis=0)
    o_ref[...] = (out.T * sg_ref[...].astype(F32)).astype(o_ref.dtype)


def _fox_attention(proj, csum, bsz, seq, t=512):
    width = FOX_HEADS * HEAD_DIM
    n_pairs = width // LANES
    t = min(t, seq)
    nq = seq // t
    return pl.pallas_call(
        functools.partial(_fox_kernel, tk=t),
        grid=(bsz, n_pairs, nq),
        in_specs=[
            pl.BlockSpec((t, LANES), lambda b, h, i: (b * nq + i, h)),
            pl.BlockSpec((seq, LANES), lambda b, h, i: (b, n_pairs + h)),
            pl.BlockSpec((seq, LANES), lambda b, h, i: (b, 2 * n_pairs + h)),
            pl.BlockSpec((t, LANES), lambda b, h, i: (b * nq + i, 3 * n_pairs + h)),
            pl.BlockSpec((t, LANES), lambda b, h, i: (b * nq + i, 0)),
            pl.BlockSpec((seq, LANES), lambda b, h, i: (b, 0)),
        ],
        out_specs=pl.BlockSpec((t, LANES), lambda b, h, i: (b * nq + i, h)),
        out_shape=jax.ShapeDtypeStruct((bsz * seq, width), BF16),
        scratch_shapes=[
            pltpu.VMEM((2, nq, t, LANES), BF16), pltpu.VMEM((nq, 2, HEAD_DIM + ONES_ROWS, t), BF16),
        ] + 2 * [pltpu.VMEM((LANES, t), BF16), pltpu.VMEM((t, t), F32), pltpu.VMEM((t, t), BF16),
                 pltpu.VMEM((HEAD_DIM + ONES_ROWS, t), F32), pltpu.VMEM((HEAD_DIM + ONES_ROWS, t), F32),
                 pltpu.VMEM((1, t), F32)],
        compiler_params=_params(3),
        name="fox_attention",
    )(proj, proj, proj, proj, csum, csum)


def _fox_mixer(x, g, w_in, layer, w_fg, b_fg, bsz, seq, t=512):
    width = FOX_HEADS * HEAD_DIM
    col_scale = jnp.where(jnp.arange(4 * width) < width, LOG2E * HEAD_DIM ** -0.5, 1.0).astype(F32)
    proj = _norm_matmul(x, g, _tile_cols(w_in, layer, MATMUL_TN, col_scale), BF16, silu_cols=(3 * width, 4 * width))
    pad = LANES - FOX_HEADS
    z = _norm_matmul(x, g, _tile_cols_xla(jnp.pad(w_fg, ((0, 0), (0, pad))), LANES), F32)
    csum = _forget_cumsum(z, jnp.pad(b_fg.astype(F32), (0, pad)).reshape(1, LANES), bsz, seq)
    return _fox_attention(proj, csum, bsz, seq, t)


def kernel(x, p, norm_g, final_g, rel_bias, swa_w_in, swa_w_out, swa_sinks, conv_w_in, conv_kernel, conv_w_out, ssm_w_in, ssm_lam_re, ssm_lam_im, ssm_log_dt, ssm_b_re, ssm_b_im, ssm_c_re, ssm_c_im, ssm_d, ssm_w_glu, ssm_b_glu, ssm_w_out, fox_w_in, fox_w_fg, fox_b_fg, fox_w_out, ple_proj, ple_norm, ple_gate):
    bsz, seq, d_model = x.shape
    depth = p.shape[0]
    h = x.astype(F32).reshape(bsz * seq, d_model)
    for i in range(depth):
        mixer, j = i % N_MIXERS, i // N_MIXERS
        if mixer == 0:
            a, w_out = _swa_mixer(h, norm_g[i], swa_w_in[j], swa_sinks[j], rel_bias, bsz, seq), swa_w_out[j]
        elif mixer == 1:
            a, w_out = _conv_mixer(h, norm_g[i], conv_w_in, j, conv_kernel[j], seq), conv_w_out[j]
        elif mixer == 2:
            a = _ssm_mixer(h, norm_g[i], ssm_w_in, j, ssm_lam_re[j], ssm_lam_im[j], ssm_log_dt[j], ssm_b_re[j], ssm_b_im[j],
                           ssm_c_re[j], ssm_c_im[j], ssm_d[j], ssm_w_glu, ssm_b_glu[j], bsz, seq)
            w_out = ssm_w_out[j]
        else:
            a, w_out = _fox_mixer(h, norm_g[i], fox_w_in, j, fox_w_fg[j], fox_b_fg[j], bsz, seq), fox_w_out[j]
        h = _out_ple(a, w_out, h, ple_norm[i], ple_gate, p.reshape(depth, bsz * seq, -1), ple_proj, i,
                     final_g=final_g if i == depth - 1 else None)
    return h.reshape(bsz, seq, d_model).astype(x.dtype)
```

```python
import functools
import math

import numpy as np
import jax
import jax.numpy as jnp
from jax import lax
from jax.experimental import pallas as pl
from jax.experimental.pallas import tpu as pltpu

F32 = jnp.float32
BF16 = jnp.bfloat16

EPS = 1e-6
N_MIXERS = 4
PLE_DIM = 256

SWA_HEADS = 32
SWA_KV_HEADS = 4
SWA_GROUP = SWA_HEADS // SWA_KV_HEADS
HEAD_DIM = 64
SWA_BLOCK = 128
WINDOW = 128
REL_BUCKETS = 32
REL_MAX_DIST = 128

CONV_TAPS = 3

SSM_GROUP = 16
SSM_STATE = 64
SSM_CHUNK = 16

FOX_HEADS = 32

LANES = 128
GROUPS_PER_TILE = LANES // SSM_GROUP
VMEM_LIMIT = 48 * 1024 * 1024
VMEM_LIMIT_BIG = 60 * 1024 * 1024

NEG = float(jnp.finfo(jnp.float32).min)
LOG2E = math.log2(math.e)


def _params(n_axes, vmem_limit=VMEM_LIMIT):
    return pltpu.CompilerParams(dimension_semantics=("arbitrary",) * n_axes, vmem_limit_bytes=vmem_limit)


def _dot(a, b):
    return jnp.dot(a, b, preferred_element_type=F32)


def _dot_nt(a, b):
    return lax.dot_general(a, b, (((1,), (1,)), ((), ())), preferred_element_type=F32)


def _rmsnorm_rows(x, g):
    return x * lax.rsqrt(jnp.mean(x * x, axis=-1, keepdims=True) + EPS) * g


def _silu(x):
    return x * jax.nn.sigmoid(x)


ROW_CHUNK = 64


def _for_row_chunks(n_rows, fn):
    def body(c, carry):
        fn(pl.ds(pl.multiple_of(c * ROW_CHUNK, ROW_CHUNK), ROW_CHUNK))
        return carry
    lax.fori_loop(0, n_rows // ROW_CHUNK, body, 0)


def _tile_cols_xla(w, tn):
    k, n = w.shape
    return w.astype(BF16).reshape(k, n // tn, tn).transpose(1, 0, 2)


def _cast_tiles_kernel(w_ref, o_ref):
    def chunk(rows):
        o_ref[rows, :] = w_ref[rows, :].astype(BF16)
    _for_row_chunks(o_ref.shape[0], chunk)


def _scale_cast_tiles_kernel(w_ref, s_ref, o_ref):
    def chunk(rows):
        o_ref[rows, :] = (w_ref[rows, :] * s_ref[...]).astype(BF16)
    _for_row_chunks(o_ref.shape[0], chunk)


def _tile_cols(w, layer, tn, col_scale=None):
    _, k, n = w.shape
    tn = min(tn, n)
    w_spec = pl.BlockSpec((None, k, tn), lambda j: (layer, 0, j))
    common = dict(
        grid=(n // tn,),
        out_specs=pl.BlockSpec((None, k, tn), lambda j: (j, 0, 0)),
        out_shape=jax.ShapeDtypeStruct((n // tn, k, tn), BF16),
        compiler_params=_params(1),
        name="weight_tiles",
    )
    if col_scale is None:
        return pl.pallas_call(_cast_tiles_kernel, in_specs=[w_spec], **common)(w)
    s_spec = pl.BlockSpec((1, tn), lambda j: (0, j))
    return pl.pallas_call(_scale_cast_tiles_kernel, in_specs=[w_spec, s_spec], **common)(w, col_scale.reshape(1, n))


def _norm_matmul_kernel(x_ref, g_ref, w_ref, o_ref, a_ref, *, silu_tiles):
    j = pl.program_id(1)

    @pl.when(j == 0)
    def _():
        def chunk(rows):
            a_ref[rows, :] = _rmsnorm_rows(x_ref[rows, :], g_ref[...]).astype(BF16)
        _for_row_chunks(a_ref.shape[0], chunk)

    acc = _dot(a_ref[...], w_ref[...])
    if silu_tiles is None:
        o_ref[...] = acc.astype(o_ref.dtype)
    else:
        gated = jnp.logical_and(j >= silu_tiles[0], j < silu_tiles[1])

        @pl.when(gated)
        def _():
            o_ref[...] = _silu(acc).astype(o_ref.dtype)

        @pl.when(jnp.logical_not(gated))
        def _():
            o_ref[...] = acc.astype(o_ref.dtype)


MATMUL_TN = 1024


def _norm_matmul(x, g, w_tiles, out_dtype, silu_cols=None, tm=1024):
    m, k = x.shape
    tn = w_tiles.shape[2]
    n = w_tiles.shape[0] * tn
    tm = min(tm, m)
    silu_tiles = None if silu_cols is None else (silu_cols[0] // tn, silu_cols[1] // tn)
    return pl.pallas_call(
        functools.partial(_norm_matmul_kernel, silu_tiles=silu_tiles),
        grid=(m // tm, n // tn),
        in_specs=[
            pl.BlockSpec((tm, k), lambda i, j: (i, 0)),
            pl.BlockSpec((1, k), lambda i, j: (0, 0)),
            pl.BlockSpec((None, k, tn), lambda i, j: (j, 0, 0)),
        ],
        out_specs=pl.BlockSpec((tm, tn), lambda i, j: (i, j)),
        out_shape=jax.ShapeDtypeStruct((m, n), out_dtype),
        scratch_shapes=[pltpu.VMEM((tm, k), BF16)],
        compiler_params=_params(2),
        name="norm_matmul",
    )(x, g.reshape(1, k), w_tiles)


def _out_ple_kernel(a_ref, wo_ref, x_ref, g_ref, wg_ref, p_ref, wp_ref, gf_ref, o_ref, x1_ref, hn_ref, ss_ref, *, nt, final):
    j = pl.program_id(1)
    tn = x_ref.shape[1]
    n = nt * tn

    @pl.when(j == 0)
    def _():
        ss_ref[...] = jnp.zeros(ss_ref.shape, F32)

    @pl.when(j < nt)
    def _():
        x1 = x_ref[...] + _dot(a_ref[...], wo_ref[j])
        x1_ref[j] = x1
        hn_ref[j] = (x1 * g_ref[j]).astype(BF16)
        ss_ref[0] += jnp.sum(x1 * x1, axis=-1, keepdims=True)

    @pl.when(jnp.logical_and(j >= nt, j < 2 * nt))
    def _():
        t = j - nt
        emb = _dot(p_ref[...].astype(BF16), wp_ref[t])
        acc = _dot(hn_ref[0], wg_ref[t, :tn, :])
        for kt in range(1, nt):
            acc = acc + _dot(hn_ref[kt], wg_ref[t, kt * tn:(kt + 1) * tn, :])
        x2 = x1_ref[t] + emb * jax.nn.sigmoid(acc * lax.rsqrt(ss_ref[0] / n + EPS))
        if final:
            x1_ref[t] = x2
            ss_ref[1] += jnp.sum(x2 * x2, axis=-1, keepdims=True)
        else:
            o_ref[...] = x2

    if final:
        @pl.when(j >= 2 * nt)
        def _():
            t = j - 2 * nt
            o_ref[...] = x1_ref[t] * lax.rsqrt(ss_ref[1] / n + EPS) * gf_ref[t]


def _out_ple(a, w_out, x, g, w_gate, p, w_proj, layer, final_g=None, tm=1024, tn=512):
    m, k = a.shape
    n = w_out.shape[1]
    pd = p.shape[2]
    final = final_g is not None
    tm, tn = min(tm, m), min(tn, n)
    nt = n // tn
    passes = 3 if final else 2
    out_tile = lambda j: jnp.maximum(j - (passes - 1) * nt, 0)
    resident = lambda rows: pl.BlockSpec((nt, rows, tn), lambda i, j: (0, 0, 0), pipeline_mode=pl.Buffered(1))
    row_vec = pl.BlockSpec((nt, 1, tn), lambda i, j: (0, 0, 0))
    return pl.pallas_call(
        functools.partial(_out_ple_kernel, nt=nt, final=final),
        grid=(m // tm, passes * nt),
        in_specs=[
            pl.BlockSpec((tm, k), lambda i, j: (i, 0)),
            resident(k),
            pl.BlockSpec((tm, tn), lambda i, j: (i, jnp.minimum(j, nt - 1))),
            row_vec,
            resident(n),
            pl.BlockSpec((None, tm, pd), lambda i, j: (layer, i, 0)),
            resident(pd),
            row_vec,
        ],
        out_specs=pl.BlockSpec((tm, tn), lambda i, j: (i, out_tile(j))),
        out_shape=jax.ShapeDtypeStruct((m, n), F32),
        scratch_shapes=[pltpu.VMEM((nt, tm, tn), F32), pltpu.VMEM((nt, tm, tn), BF16), pltpu.VMEM((2, tm, 1), F32)],
        compiler_params=_params(2, VMEM_LIMIT_BIG),
        name="out_ple_final" if final else "out_ple",
    )(a, _tile_cols(w_out[None], 0, tn), x, g.reshape(nt, 1, tn), _tile_cols(w_gate, layer, tn), p,
      _tile_cols(w_proj, layer, tn), (final_g if final else g).reshape(nt, 1, tn))


def _t5_bucket(dist):
    max_exact = REL_BUCKETS // 2
    d = np.maximum(dist, 1).astype(np.float32)
    large = max_exact + (np.log(d / max_exact) / np.log(REL_MAX_DIST / max_exact) * (REL_BUCKETS - max_exact)).astype(np.int32)
    large = np.minimum(large, REL_BUCKETS - 1)
    return np.where(dist < max_exact, dist, large).astype(np.int32)


def _swa_bucket_table():
    qi = np.arange(SWA_BLOCK)[None, :]
    kj = np.arange(2 * SWA_BLOCK)[:, None]
    return _t5_bucket(np.clip(qi + SWA_BLOCK - kj, 0, None))


def _swa_kernel(sink_ref, rel_ref, bucket_ref, q_ref, sg_ref, kp_ref, kc_ref, vp_ref, vc_ref, o_ref, bias_ref, qs_ref):
    n = pl.program_id(1)
    blk = SWA_BLOCK
    lane = lax.broadcasted_iota(jnp.int32, (1, LANES), 1)

    @pl.when(jnp.logical_and(pl.program_id(0) == 0, n == 0))
    def _():
        bucket = bucket_ref[...]
        kj = lax.broadcasted_iota(jnp.int32, (2 * blk, blk), 0)
        qi = lax.broadcasted_iota(jnp.int32, (2 * blk, blk), 1)
        band = jnp.logical_or(jnp.logical_and(kj < blk, kj > qi), jnp.logical_and(kj >= blk, kj - blk <= qi))

        def per_head(head, carry):
            acc = jnp.zeros((2 * blk, blk), F32)
            for b in range(REL_BUCKETS):
                acc = jnp.where(bucket == b, rel_ref[b, head], acc)
            bias_ref[head] = jnp.where(band, acc, NEG)
            return carry
        lax.fori_loop(0, SWA_HEADS, per_head, 0)

    for kvh in range(SWA_KV_HEADS):
        cols = slice(kvh * LANES, (kvh + 1) * LANES)
        heads = range(kvh * SWA_GROUP, (kvh + 1) * SWA_GROUP)
        for g, head in enumerate(heads):
            q_pair = q_ref[:, head // 2 * LANES:(head // 2 + 1) * LANES]
            qs_ref[g * blk:(g + 1) * blk, :] = jnp.where((lane // HEAD_DIM) == head % 2, q_pair, jnp.zeros_like(q_pair))
        keys = jnp.concatenate([kp_ref[:, cols], kc_ref[:, cols]], axis=0)
        vals = jnp.concatenate([vp_ref[:, cols], vc_ref[:, cols]], axis=0)
        s = _dot_nt(keys, qs_ref[...]) + jnp.concatenate([bias_ref[head] for head in heads], axis=1)
        s_prev = jnp.where(n > 0, s[:blk], NEG)
        s_cur = s[blk:]
        sink = jnp.concatenate([jnp.full((1, blk), sink_ref[head], F32) for head in heads], axis=1)
        m = jnp.maximum(jnp.maximum(s_prev, s_cur).max(axis=0, keepdims=True), sink)
        e_prev, e_cur = jnp.exp(s_prev - m), jnp.exp(s_cur - m)
        den = e_prev.sum(axis=0, keepdims=True) + e_cur.sum(axis=0, keepdims=True) + jnp.exp(sink - m)
        e = jnp.concatenate([e_prev, e_cur], axis=0).astype(BF16)
        out = _dot(vals.astype(F32).T.astype(BF16), e) / den
        for pair in range(SWA_GROUP // 2):
            head = kvh * SWA_GROUP + 2 * pair
            even = out[:HEAD_DIM, 2 * pair * blk:(2 * pair + 1) * blk]
            odd = out[HEAD_DIM:, (2 * pair + 1) * blk:(2 * pair + 2) * blk]
            cols = slice(head // 2 * LANES, (head // 2 + 1) * LANES)
            o_ref[:, cols] = (jnp.concatenate([even, odd], axis=0).T * sg_ref[:, cols].astype(F32)).astype(o_ref.dtype)


def _swa_attention(proj, sinks, rel_bias, bsz, seq):
    nb = seq // SWA_BLOCK
    width = SWA_HEADS * HEAD_DIM
    kvw = SWA_KV_HEADS * LANES
    k_blk = 2 * width // kvw
    cur = lambda b, n: b * nb + n
    prev = lambda b, n: b * nb + jnp.maximum(n - 1, 0)
    smem = pl.BlockSpec(memory_space=pltpu.SMEM)
    return pl.pallas_call(
        _swa_kernel,
        grid=(bsz, nb),
        in_specs=[
            smem, smem,
            pl.BlockSpec((2 * SWA_BLOCK, SWA_BLOCK), lambda b, n: (0, 0)),
            pl.BlockSpec((SWA_BLOCK, width), lambda b, n: (cur(b, n), 0)),
            pl.BlockSpec((SWA_BLOCK, width), lambda b, n: (cur(b, n), 1)),
            pl.BlockSpec((SWA_BLOCK, kvw), lambda b, n: (prev(b, n), k_blk)),
            pl.BlockSpec((SWA_BLOCK, kvw), lambda b, n: (cur(b, n), k_blk)),
            pl.BlockSpec((SWA_BLOCK, kvw), lambda b, n: (prev(b, n), k_blk + 1)),
            pl.BlockSpec((SWA_BLOCK, kvw), lambda b, n: (cur(b, n), k_blk + 1)),
        ],
        out_specs=pl.BlockSpec((SWA_BLOCK, width), lambda b, n: (cur(b, n), 0)),
        out_shape=jax.ShapeDtypeStruct((bsz * seq, width), BF16),
        scratch_shapes=[
            pltpu.VMEM((SWA_HEADS, 2 * SWA_BLOCK, SWA_BLOCK), F32),
            pltpu.VMEM((SWA_GROUP * SWA_BLOCK, LANES), BF16),
        ],
        compiler_params=_params(2),
        name="swa_attention",
    )(sinks.astype(F32), rel_bias.astype(F32), jnp.asarray(_swa_bucket_table()), proj, proj, proj, proj, proj, proj)


def _dup_heads(w, n_heads):
    k = w.shape[0]
    w = w.reshape(k, n_heads, 1, HEAD_DIM)
    return jnp.broadcast_to(w, (k, n_heads, 2, HEAD_DIM)).reshape(k, n_heads * 2 * HEAD_DIM)


def _swa_mixer(x, g, w_in, sinks, rel_bias, bsz, seq):
    width = SWA_HEADS * HEAD_DIM
    kvw = SWA_KV_HEADS * HEAD_DIM
    w_q, w_k, w_v, w_g = (w_in[:, :width], w_in[:, width:width + kvw],
                          w_in[:, width + kvw:width + 2 * kvw], w_in[:, width + 2 * kvw:])
    w = jnp.concatenate([w_q * (HEAD_DIM ** -0.5), w_g, _dup_heads(w_k, SWA_KV_HEADS), _dup_heads(w_v, SWA_KV_HEADS)], axis=1)
    proj = _norm_matmul(x, g, _tile_cols_xla(w, MATMUL_TN), BF16, silu_cols=(width, 2 * width))
    return _swa_attention(proj, sinks, rel_bias, bsz, seq)


HALO = 8


def _conv_kernel(x_ref, g_ref, wb_ref, wc_ref, wu_ref, wg_ref, ck_ref, o_ref, a_ref, z_ref, carry_ref, *, tiles_per_seq):
    i, j = pl.program_id(0), pl.program_id(1)
    tm = a_ref.shape[0]

    @pl.when(j == 0)
    def _():
        def chunk(rows):
            a_ref[rows, :] = _rmsnorm_rows(x_ref[rows, :], g_ref[...]).astype(BF16)
        _for_row_chunks(tm, chunk)

    a = a_ref[...]
    z = _dot(a, wc_ref[...]) * _dot(a, wu_ref[...])
    first = (i % tiles_per_seq) == 0

    @pl.when(first)
    def _():
        z_ref[:HALO, :] = jnp.zeros((HALO, z_ref.shape[1]), F32)

    @pl.when(jnp.logical_not(first))
    def _():
        z_ref[:HALO, :] = carry_ref[j]

    z_ref[HALO:, :] = z
    carry_ref[j] = z[tm - HALO:, :]
    conv = z_ref[HALO - 2:HALO - 2 + tm, :] * ck_ref[0:1, :]
    conv = conv + z_ref[HALO - 1:HALO - 1 + tm, :] * ck_ref[1:2, :]
    conv = conv + z * ck_ref[2:3, :]
    y = _dot(a, wb_ref[...]) * conv
    o_ref[...] = (y * _silu(_dot(a, wg_ref[...]))).astype(o_ref.dtype)


def _conv_mixer(x, g, w_in, layer, conv_kernel, seq, tm=1024, tn=512):
    m, k = x.shape
    width = w_in.shape[2] // 4
    tm, tn = min(tm, seq), min(tn, width)
    nt = width // tn
    w_spec = lambda q: pl.BlockSpec((None, k, tn), lambda i, j: (q * nt + j, 0, 0))
    w_in = _tile_cols(w_in, layer, tn)
    return pl.pallas_call(
        functools.partial(_conv_kernel, tiles_per_seq=seq // tm),
        grid=(m // tm, nt),
        in_specs=[
            pl.BlockSpec((tm, k), lambda i, j: (i, 0)),
            pl.BlockSpec((1, k), lambda i, j: (0, 0)),
            w_spec(0), w_spec(1), w_spec(2), w_spec(3),
            pl.BlockSpec((CONV_TAPS, tn), lambda i, j: (0, j)),
        ],
        out_specs=pl.BlockSpec((tm, tn), lambda i, j: (i, j)),
        out_shape=jax.ShapeDtypeStruct((m, width), BF16),
        scratch_shapes=[
            pltpu.VMEM((tm, k), BF16),
            pltpu.VMEM((HALO + tm, tn), F32),
            pltpu.VMEM((nt, HALO, tn), F32),
        ],
        compiler_params=_params(2, VMEM_LIMIT_BIG),
        name="conv_mixer",
    )(x, g.reshape(1, k), w_in, w_in, w_in, w_in, conv_kernel.astype(F32))


def _ssm_operators(lam_re, lam_im, log_dt, b_re, b_im, c_re, c_im, d_skip):
    n_groups = lam_re.shape[0]
    n_oct = n_groups // GROUPS_PER_TILE
    L, C, N = SSM_CHUNK, SSM_GROUP, SSM_STATE
    dt = jnp.exp(log_dt.astype(F32))[None, :]
    lr, li = lam_re.astype(F32).T, lam_im.astype(F32).T
    mag = jnp.exp(lr * dt)
    ab_re, ab_im = mag * jnp.cos(li * dt), mag * jnp.sin(li * dt)
    den = lr * lr + li * li
    nr = ab_re - 1.0
    coef_re = (nr * lr + ab_im * li) / den
    coef_im = (ab_im * lr - nr * li) / den
    br, bi = b_re.astype(F32).transpose(2, 1, 0), b_im.astype(F32).transpose(2, 1, 0)
    bb_re = coef_re * br - coef_im * bi
    bb_im = coef_re * bi + coef_im * br
    cr, ci = c_re.astype(F32).transpose(1, 2, 0), c_im.astype(F32).transpose(1, 2, 0)
    pw_re, pw_im = [jnp.ones_like(ab_re)], [jnp.zeros_like(ab_im)]
    for _ in range(L):
        pw_re, pw_im = (pw_re + [pw_re[-1] * ab_re - pw_im[-1] * ab_im], pw_im + [pw_re[-1] * ab_im + pw_im[-1] * ab_re])
    p_re, p_im = jnp.stack(pw_re), jnp.stack(pw_im)
    w_re = cr[None] * p_re[:L, None] - ci[None] * p_im[:L, None]
    w_im = cr[None] * p_im[:L, None] + ci[None] * p_re[:L, None]
    lag_k = jnp.sum(w_re[:, None] * bb_re[None, :, None] - w_im[:, None] * bb_im[None, :, None], axis=3)
    q_re, q_im = jnp.stack(pw_re[L - 1::-1]), jnp.stack(pw_im[L - 1::-1])
    bc_re = q_re[:, None] * bb_re[None] - q_im[:, None] * bb_im[None]
    bc_im = q_re[:, None] * bb_im[None] + q_im[:, None] * bb_re[None]
    o_re = cr[None] * p_re[1:, None] - ci[None] * p_im[1:, None]
    o_im = cr[None] * p_im[1:, None] + ci[None] * p_re[1:, None]
    octs = lambda t: t.reshape(t.shape[:-1] + (n_oct, GROUPS_PER_TILE))
    to_lag = lambda t: octs(t).transpose(3, 0, 1, 4, 2).reshape(n_oct, L, C, LANES)
    to_rows = lambda t: octs(t).transpose(3, 0, 1, 4, 2).reshape(n_oct, L * C, GROUPS_PER_TILE * N)
    to_cols = lambda t: octs(t).transpose(3, 4, 2, 0, 1).reshape(n_oct, GROUPS_PER_TILE * N, L * C)
    to_vec = lambda t: octs(t).transpose(1, 2, 0).reshape(n_oct, 1, GROUPS_PER_TILE * N)
    return dict(
        lag=to_lag(lag_k).astype(BF16),
        bc_re=to_rows(bc_re).astype(BF16), bc_im=to_rows(bc_im).astype(BF16),
        oc_re=to_cols(o_re).astype(BF16), oc_im_neg=to_cols(-o_im).astype(BF16),
        al_re=to_vec(p_re[L]), al_im=to_vec(p_im[L]), d=d_skip.astype(F32).reshape(n_oct, 1, LANES),
    )


def _ssm_expanders():
    L, C, N, G8 = SSM_CHUNK, SSM_GROUP, SSM_STATE, GROUPS_PER_TILE
    wide = np.arange(L * LANES)
    w_step, w_group, w_chan = wide // LANES, (wide // C) % G8, wide % C
    small = np.arange(L * C)
    s_step, s_chan = small // C, small % C
    state_group = np.arange(G8 * N) // N
    spread = (w_step[:, None] == s_step[None, :]) & (w_chan[:, None] == s_chan[None, :])
    lane = np.arange(LANES)
    as_bf16 = lambda a: jnp.asarray(a.astype(np.float32), dtype=BF16)
    return dict(
        spread_cols=as_bf16(spread.T),
        keep_cols=as_bf16(state_group[:, None] == w_group[None, :]),
        spread_lag=as_bf16((lane % C)[:, None] == np.arange(C)[None, :]),
        keep_lag=jnp.asarray(((lane // C)[:, None] == (lane // C)[None, :]).astype(np.float32)),
    )


def _ssm_kernel(u_ref, lag_ref, bxr_ref, bxi_ref, cxr_ref, cxi_ref, alr_ref, ali_ref, d_ref,
                sc_ref, kc_ref, sl_ref, kl_ref, o_ref,
                ub_ref, bcr_ref, bci_ref, ocr_ref, oci_ref, zr_ref, zi_ref, hr_ref, hi_ref, *, bsz):
    L = SSM_CHUNK
    rows = u_ref.shape[0] // L
    chunks = rows // bsz
    tile = 2 * LANES
    step_rows = lambda r: pl.ds(r, rows, stride=L)
    for r in range(L):
        ub_ref[:, r * LANES:(r + 1) * LANES] = u_ref[step_rows(r), :].astype(BF16)
    state_group = lax.broadcasted_iota(jnp.int32, (1, bxr_ref.shape[1]), 1) // SSM_STATE
    for compact_ref, full_ref in ((bxr_ref, bcr_ref), (bxi_ref, bci_ref)):
        for r in range(L):
            blk = compact_ref[r * SSM_GROUP:(r + 1) * SSM_GROUP, :]
            for grp in range(GROUPS_PER_TILE):
                first = r * LANES + grp * SSM_GROUP
                full_ref[first:first + SSM_GROUP, :] = jnp.where(state_group == grp, blk, jnp.zeros_like(blk))
    zr_ref[...] = _dot(ub_ref[...], bcr_ref[...])
    zi_ref[...] = _dot(ub_ref[...], bci_ref[...])
    a_re, a_im = alr_ref[...], ali_ref[...]

    def step(k, carry):
        new = []
        for b in range(bsz):
            h_re, h_im = carry[2 * b], carry[2 * b + 1]
            row = pl.ds(b * chunks + k, 1)
            hr_ref[row, :] = h_re
            hi_ref[row, :] = h_im
            new.append(a_re * h_re - a_im * h_im + zr_ref[row, :])
            new.append(a_re * h_im + a_im * h_re + zi_ref[row, :])
        return tuple(new)

    zero = jnp.zeros((1, a_re.shape[1]), F32)
    lax.fori_loop(0, chunks, step, (zero,) * (2 * bsz))

    ocr_ref[...] = (_dot(cxr_ref[...], sc_ref[...]) * kc_ref[...]).astype(BF16)
    oci_ref[...] = (_dot(cxi_ref[...], sc_ref[...]) * kc_ref[...]).astype(BF16)
    lag = [(_dot(sl_ref[...], lag_ref[d]) * kl_ref[...]).astype(BF16) for d in range(L)]
    lag_tile = []
    for dd in range(L // 2):
        below = lag[2 * dd - 1] if dd > 0 else jnp.zeros((LANES, LANES), BF16)
        lag_tile.append(jnp.concatenate([jnp.concatenate([lag[2 * dd], lag[2 * dd + 1]], axis=1),
                                         jnp.concatenate([below, lag[2 * dd]], axis=1)], axis=0))
    hb_re, hb_im = hr_ref[...].astype(BF16), hi_ref[...].astype(BF16)
    for t2 in range(L // 2):
        cols = slice(t2 * tile, (t2 + 1) * tile)
        y = _dot(hb_re, ocr_ref[:, cols]) + _dot(hb_im, oci_ref[:, cols])
        for r2 in range(t2 + 1):
            y = y + _dot(ub_ref[:, r2 * tile:(r2 + 1) * tile], lag_tile[t2 - r2])
        for half in range(2):
            t = 2 * t2 + half
            o_ref[step_rows(t), :] = jax.nn.gelu(y[:, half * LANES:(half + 1) * LANES] + d_ref[...] * u_ref[step_rows(t), :])


def _ssm_core(proj, ops, bsz, seq):
    m = proj.shape[0]
    width = proj.shape[1] // 2
    L, C = SSM_CHUNK, SSM_GROUP
    n_oct = width // LANES
    rows = m // L
    ow = L * LANES
    n_state = GROUPS_PER_TILE * SSM_STATE
    ex = _ssm_expanders()
    per_oct = lambda *shape: pl.BlockSpec((None,) + shape, lambda p: (p,) + (0,) * len(shape))
    const = lambda a: pl.BlockSpec(a.shape, lambda p: (0,) * a.ndim, pipeline_mode=pl.Buffered(1))
    consts = [ex['spread_cols'], ex['keep_cols'], ex['spread_lag'], ex['keep_lag']]
    return pl.pallas_call(
        functools.partial(_ssm_kernel, bsz=bsz),
        grid=(n_oct,),
        in_specs=[
            pl.BlockSpec((m, LANES), lambda p: (0, p)),
            per_oct(L, C, LANES), per_oct(L * C, n_state), per_oct(L * C, n_state),
            per_oct(n_state, L * C), per_oct(n_state, L * C), per_oct(1, n_state), per_oct(1, n_state), per_oct(1, LANES),
        ] + [const(a) for a in consts],
        out_specs=pl.BlockSpec((m, LANES), lambda p: (0, p)),
        out_shape=jax.ShapeDtypeStruct((m, width), F32),
        scratch_shapes=[pltpu.VMEM((rows, ow), BF16)] + [pltpu.VMEM((ow, n_state), BF16)] * 2
        + [pltpu.VMEM((n_state, ow), BF16)] * 2 + [pltpu.VMEM((rows, n_state), F32)] * 4,
        compiler_params=_params(1),
        name="ssm_core",
    )(proj, ops['lag'], ops['bc_re'], ops['bc_im'], ops['oc_re'], ops['oc_im_neg'], ops['al_re'], ops['al_im'], ops['d'], *consts)


def _glu_kernel(y_ref, wa_ref, wb_ref, ba_ref, bb_ref, gate_ref, o_ref, yb_ref):
    @pl.when(pl.program_id(1) == 0)
    def _():
        def chunk(rows):
            yb_ref[rows, :] = y_ref[rows, :].astype(BF16)
        _for_row_chunks(yb_ref.shape[0], chunk)

    y = yb_ref[...]
    ga = _dot(y, wa_ref[...]) + ba_ref[...]
    gb = _dot(y, wb_ref[...]) + bb_ref[...]
    o_ref[...] = ((ga * jax.nn.sigmoid(gb)) * _silu(gate_ref[...])).astype(o_ref.dtype)


def _glu(y, w_glu, layer, b_glu, proj, tm=1024, tn=512):
    m, k = y.shape
    width = w_glu.shape[2] // 2
    tm, tn = min(tm, m), min(tn, width)
    nt = width // tn
    w_tiles = _tile_cols(w_glu, layer, tn)
    return pl.pallas_call(
        _glu_kernel,
        grid=(m // tm, nt),
        in_specs=[
            pl.BlockSpec((tm, k), lambda i, j: (i, 0)),
            pl.BlockSpec((None, k, tn), lambda i, j: (j, 0, 0)),
            pl.BlockSpec((None, k, tn), lambda i, j: (nt + j, 0, 0)),
            pl.BlockSpec((1, tn), lambda i, j: (0, j)),
            pl.BlockSpec((1, tn), lambda i, j: (0, nt + j)),
            pl.BlockSpec((tm, tn), lambda i, j: (i, nt + j)),
        ],
        out_specs=pl.BlockSpec((tm, tn), lambda i, j: (i, j)),
        out_shape=jax.ShapeDtypeStruct((m, width), BF16),
        scratch_shapes=[pltpu.VMEM((tm, k), BF16)],
        compiler_params=_params(2),
        name="ssm_glu",
    )(y, w_tiles, w_tiles, b_glu, b_glu, proj)


def _ssm_mixer(x, g, w_in, layer, lam_re, lam_im, log_dt, b_re, b_im, c_re, c_im, d_skip, w_glu, b_glu, bsz, seq):
    proj = _norm_matmul(x, g, _tile_cols(w_in, layer, MATMUL_TN), F32)
    ops = _ssm_operators(lam_re, lam_im, log_dt, b_re, b_im, c_re, c_im, d_skip)
    y = _ssm_core(proj, ops, bsz, seq)
    return _glu(y, w_glu, layer, b_glu.astype(F32).reshape(1, -1), proj)


CUM_BLOCK = 128
ONES_ROWS = 16


def _split3(x):
    x1 = x.astype(BF16)
    r1 = x - x1.astype(F32)
    x2 = r1.astype(BF16)
    x3 = (r1 - x2.astype(F32)).astype(BF16)
    return x1, x2, x3


def _forget_cumsum_kernel(z_ref, b_ref, o_ref):
    n_blocks = z_ref.shape[0] // CUM_BLOCK
    ri = lax.broadcasted_iota(jnp.int32, (CUM_BLOCK, CUM_BLOCK), 0)
    ci = lax.broadcasted_iota(jnp.int32, (CUM_BLOCK, CUM_BLOCK), 1)
    tri = jnp.where(ci <= ri, 1.0, 0.0).astype(BF16)

    def body(i, carry):
        rows = pl.ds(pl.multiple_of(i * CUM_BLOCK, CUM_BLOCK), CUM_BLOCK)
        z = z_ref[rows, :] + b_ref[...]
        log_f = jnp.minimum(z, 0.0) - jnp.log1p(jnp.exp(-jnp.abs(z)))
        x1, x2, x3 = _split3(log_f)
        c = _dot(tri, x1) + _dot(tri, x2) + _dot(tri, x3) + carry
        o_ref[rows, :] = c
        return c[CUM_BLOCK - 1:, :]

    lax.fori_loop(0, n_blocks, body, jnp.zeros((1, z_ref.shape[1]), F32))


def _forget_cumsum(z, b, bsz, seq):
    lanes = z.shape[1]
    return pl.pallas_call(
        _forget_cumsum_kernel,
        grid=(bsz,),
        in_specs=[pl.BlockSpec((seq, lanes), lambda i: (i, 0)), pl.BlockSpec((1, lanes), lambda i: (0, 0))],
        out_specs=pl.BlockSpec((seq, lanes), lambda i: (i, 0)),
        out_shape=jax.ShapeDtypeStruct(z.shape, F32),
        compiler_params=_params(1),
        name="forget_cumsum",
    )(z, b)


FOX_STEP_HEADS = 4


def _fox_kernel(q_ref, k_ref, v_ref, sg_ref, cq_ref, ck_ref, o_ref, ka_ref, vt_ref, *head_refs, tk):
    nh = FOX_STEP_HEADS
    hg, qi = pl.program_id(1), pl.program_id(2)
    tq = q_ref.shape[0]
    qa_ref, s_ref, p_ref, acc_ref, pv_ref, m_ref = (head_refs[i::6] for i in range(6))
    lane = lax.broadcasted_iota(jnp.int32, (1, LANES), 1)
    free = (HEAD_DIM, 0)
    n_pieces = 3
    pair_lanes = lambda h: slice(h // 2 * LANES, (h // 2 + 1) * LANES)

    def bias_lanes(c, h, c_offset, one_offset):
        rr = lax.broadcasted_iota(jnp.int32, (n_pieces * LANES, LANES), 0)
        cc = lax.broadcasted_iota(jnp.int32, (n_pieces * LANES, LANES), 1)
        place = jnp.logical_and(rr % LANES == nh * hg + h, cc == free[h % 2] + c_offset + rr // LANES)
        placed = _dot(jnp.concatenate(_split3(c), axis=1), jnp.where(place, 1.0, 0.0).astype(BF16))
        first = free[h % 2] + one_offset
        return (placed + jnp.where(jnp.logical_and(lane >= first, lane < first + n_pieces), 1.0, 0.0)).astype(BF16)

    @pl.when(qi == 0)
    def _():
        def chunk(j, carry):
            rows = pl.ds(pl.multiple_of(j * tk, tk), tk)
            neg_c = ck_ref[rows, :] * -LOG2E
            for h in range(nh):
                keys = k_ref[rows, pair_lanes(h)]
                ka_ref[h, j] = jnp.where((lane // HEAD_DIM) == h % 2, keys, bias_lanes(neg_c, h, 0, n_pieces))
            for pair in range(nh // 2):
                v_t = v_ref[rows, pair_lanes(2 * pair)].astype(F32).T.astype(BF16)
                for hh in range(2):
                    vt_ref[j, 2 * pair + hh, :HEAD_DIM, :] = v_t[hh * HEAD_DIM:(hh + 1) * HEAD_DIM]
                    vt_ref[j, 2 * pair + hh, HEAD_DIM:, :] = jnp.ones((ONES_ROWS, tk), BF16)
            return carry
        lax.fori_loop(0, k_ref.shape[0] // tk, chunk, 0)

    cq = cq_ref[...] * LOG2E
    for h in range(nh):
        qa = jnp.where((lane // HEAD_DIM) == h % 2, q_ref[:, pair_lanes(h)], bias_lanes(cq, h, n_pieces, 0))
        qa_ref[h][...] = qa.astype(F32).T.astype(BF16)

    def scores(h, j):
        s_ref[h][...] = _dot(ka_ref[h, j], qa_ref[h][...])

    rows8 = 8
    rows16 = 16

    def absorb(h, j, diagonal):
        def strip(r, n):
            blk = s_ref[h][r:r + n, :]
            if diagonal:
                kpos = lax.broadcasted_iota(jnp.int32, (n, tq), 0)
                qpos = lax.broadcasted_iota(jnp.int32, (n, tq), 1)
                blk = jnp.where(kpos + r <= qpos, blk, NEG)
            return blk
        top = strip(0, rows8)
        for r in range(rows8, tk, rows8):
            top = jnp.maximum(top, strip(r, rows8))
        m = m_ref[h][...]
        m_new = jnp.maximum(m, top.max(axis=0, keepdims=True))
        m_ref[h][...] = m_new
        alpha = jnp.exp2(m - m_new)
        m_rows = jnp.broadcast_to(m_new, (rows16, tq))
        for r in range(0, tk, rows16):
            p_ref[h][r:r + rows16, :] = jnp.exp2(strip(r, rows16) - m_rows).astype(BF16)
        acc_ref[h][...] = alpha * (acc_ref[h][...] + pv_ref[h][...])

    def values(h, j):
        pv_ref[h][...] = _dot(vt_ref[j, h], p_ref[h][...])

    def step(j, diagonal):
        values(nh - 1, jnp.maximum(j - 1, 0))
        scores(1, j)
        for h in range(nh):
            absorb(h, j, diagonal)
            if h < nh - 1:
                values(h, j)
            if h + 2 < nh:
                scores(h + 2, j)
            elif h + 2 == nh and not diagonal:
                scores(0, j + 1)

    for h in range(nh):
        acc_ref[h][...] = jnp.zeros(acc_ref[h].shape, F32)
        pv_ref[h][...] = jnp.zeros(pv_ref[h].shape, F32)
        m_ref[h][...] = jnp.full(m_ref[h].shape, NEG, F32)
    p_ref[nh - 1][...] = jnp.zeros(p_ref[nh - 1].shape, BF16)
    scores(0, 0)

    def two_blocks(i, carry):
        step(2 * i, False)
        step(2 * i + 1, False)
        return carry
    lax.fori_loop(0, qi // 2, two_blocks, 0)

    @pl.when(qi % 2 == 0)
    def _():
        step(qi, True)

    @pl.when(qi % 2 == 1)
    def _():
        step(qi - 1, False)
        step(qi, True)

    values(nh - 1, qi)
    outs = []
    for h in range(nh):
        total = acc_ref[h][...] + pv_ref[h][...]
        outs.append(total[:HEAD_DIM] / total[HEAD_DIM:HEAD_DIM + 1])
    out = jnp.concatenate(outs, axis=0)
    o_ref[...] = (out.T * sg_ref[...].astype(F32)).astype(o_ref.dtype)


def _fox_attention(proj, csum, bsz, seq, t=512):
    width = FOX_HEADS * HEAD_DIM
    nh = FOX_STEP_HEADS
    gw = nh * HEAD_DIM
    n_groups = width // gw
    t = min(t, seq)
    nq = seq // t
    return pl.pallas_call(
        functools.partial(_fox_kernel, tk=t),
        grid=(bsz, n_groups, nq),
        in_specs=[
            pl.BlockSpec((t, gw), lambda b, h, i: (b * nq + i, h)),
            pl.BlockSpec((seq, gw), lambda b, h, i: (b, n_groups + h)),
            pl.BlockSpec((seq, gw), lambda b, h, i: (b, 2 * n_groups + h)),
            pl.BlockSpec((t, gw), lambda b, h, i: (b * nq + i, 3 * n_groups + h)),
            pl.BlockSpec((t, LANES), lambda b, h, i: (b * nq + i, 0)),
            pl.BlockSpec((seq, LANES), lambda b, h, i: (b, 0)),
        ],
        out_specs=pl.BlockSpec((t, gw), lambda b, h, i: (b * nq + i, h)),
        out_shape=jax.ShapeDtypeStruct((bsz * seq, width), BF16),
        scratch_shapes=[
            pltpu.VMEM((nh, nq, t, LANES), BF16), pltpu.VMEM((nq, nh, HEAD_DIM + ONES_ROWS, t), BF16),
        ] + nh * [pltpu.VMEM((LANES, t), BF16), pltpu.VMEM((t, t), F32), pltpu.VMEM((t, t), BF16),
                  pltpu.VMEM((HEAD_DIM + ONES_ROWS, t), F32), pltpu.VMEM((HEAD_DIM + ONES_ROWS, t), F32),
                  pltpu.VMEM((1, t), F32)],
        compiler_params=_params(3),
        name="fox_attention",
    )(proj, proj, proj, proj, csum, csum)


def _fox_mixer(x, g, w_in, layer, w_fg, b_fg, bsz, seq, t=512):
    width = FOX_HEADS * HEAD_DIM
    col_scale = jnp.where(jnp.arange(4 * width) < width, LOG2E * HEAD_DIM ** -0.5, 1.0).astype(F32)
    proj = _norm_matmul(x, g, _tile_cols(w_in, layer, MATMUL_TN, col_scale), BF16, silu_cols=(3 * width, 4 * width))
    pad = LANES - FOX_HEADS
    z = _norm_matmul(x, g, _tile_cols_xla(jnp.pad(w_fg, ((0, 0), (0, pad))), LANES), F32)
    csum = _forget_cumsum(z, jnp.pad(b_fg.astype(F32), (0, pad)).reshape(1, LANES), bsz, seq)
    return _fox_attention(proj, csum, bsz, seq, t)


def kernel(x, p, norm_g, final_g, rel_bias, swa_w_in, swa_w_out, swa_sinks, conv_w_in, conv_kernel, conv_w_out, ssm_w_in, ssm_lam_re, ssm_lam_im, ssm_log_dt, ssm_b_re, ssm_b_im, ssm_c_re, ssm_c_im, ssm_d, ssm_w_glu, ssm_b_glu, ssm_w_out, fox_w_in, fox_w_fg, fox_b_fg, fox_w_out, ple_proj, ple_norm, ple_gate):
    bsz, seq, d_model = x.shape
    depth = p.shape[0]
    h = x.astype(F32).reshape(bsz * seq, d_model)
    for i in range(depth):
        mixer, j = i % N_MIXERS, i // N_MIXERS
        if mixer == 0:
            a, w_out = _swa_mixer(h, norm_g[i], swa_w_in[j], swa_sinks[j], rel_bias, bsz, seq), swa_w_out[j]
        elif mixer == 1:
            a, w_out = _conv_mixer(h, norm_g[i], conv_w_in, j, conv_kernel[j], seq), conv_w_out[j]
        elif mixer == 2:
            a = _ssm_mixer(h, norm_g[i], ssm_w_in, j, ssm_lam_re[j], ssm_lam_im[j], ssm_log_dt[j], ssm_b_re[j], ssm_b_im[j],
                           ssm_c_re[j], ssm_c_im[j], ssm_d[j], ssm_w_glu, ssm_b_glu[j], bsz, seq)
            w_out = ssm_w_out[j]
        else:
            a, w_out = _fox_mixer(h, norm_g[i], fox_w_in, j, fox_w_fg[j], fox_b_fg[j], bsz, seq), fox_w_out[j]
        h = _out_ple(a, w_out, h, ple_norm[i], ple_gate, p.reshape(depth, bsz * seq, -1), ple_proj, i,
                     final_g=final_g if i == depth - 1 else None)
    return h.reshape(bsz, seq, d_model).astype(x.dtype)
```

```python
import functools
import math

import numpy as np
import jax
import jax.numpy as jnp
from jax import lax
from jax.experimental import pallas as pl
from jax.experimental.pallas import tpu as pltpu

F32 = jnp.float32
BF16 = jnp.bfloat16

EPS = 1e-6
N_MIXERS = 4
PLE_DIM = 256

SWA_HEADS = 32
SWA_KV_HEADS = 4
SWA_GROUP = SWA_HEADS // SWA_KV_HEADS
HEAD_DIM = 64
SWA_BLOCK = 128
WINDOW = 128
REL_BUCKETS = 32
REL_MAX_DIST = 128

CONV_TAPS = 3

SSM_GROUP = 16
SSM_STATE = 64
SSM_CHUNK = 16

FOX_HEADS = 32

LANES = 128
GROUPS_PER_TILE = LANES // SSM_GROUP
VMEM_LIMIT = 48 * 1024 * 1024
VMEM_LIMIT_BIG = 60 * 1024 * 1024

NEG = float(jnp.finfo(jnp.float32).min)
LOG2E = math.log2(math.e)


def _params(n_axes, vmem_limit=VMEM_LIMIT):
    return pltpu.CompilerParams(dimension_semantics=("arbitrary",) * n_axes, vmem_limit_bytes=vmem_limit)


def _dot(a, b):
    return jnp.dot(a, b, preferred_element_type=F32)


def _dot_nt(a, b):
    return lax.dot_general(a, b, (((1,), (1,)), ((), ())), preferred_element_type=F32)


def _rmsnorm_rows(x, g):
    return x * lax.rsqrt(jnp.mean(x * x, axis=-1, keepdims=True) + EPS) * g


def _silu(x):
    return x * jax.nn.sigmoid(x)


ROW_CHUNK = 64


def _for_row_chunks(n_rows, fn):
    def body(c, carry):
        fn(pl.ds(pl.multiple_of(c * ROW_CHUNK, ROW_CHUNK), ROW_CHUNK))
        return carry
    lax.fori_loop(0, n_rows // ROW_CHUNK, body, 0)


def _tile_cols_xla(w, tn):
    k, n = w.shape
    return w.astype(BF16).reshape(k, n // tn, tn).transpose(1, 0, 2)


def _cast_tiles_kernel(w_ref, o_ref):
    def chunk(rows):
        o_ref[rows, :] = w_ref[rows, :].astype(BF16)
    _for_row_chunks(o_ref.shape[0], chunk)


def _scale_cast_tiles_kernel(w_ref, s_ref, o_ref):
    def chunk(rows):
        o_ref[rows, :] = (w_ref[rows, :] * s_ref[...]).astype(BF16)
    _for_row_chunks(o_ref.shape[0], chunk)


def _tile_cols(w, layer, tn, col_scale=None):
    _, k, n = w.shape
    tn = min(tn, n)
    w_spec = pl.BlockSpec((None, k, tn), lambda j: (layer, 0, j))
    common = dict(
        grid=(n // tn,),
        out_specs=pl.BlockSpec((None, k, tn), lambda j: (j, 0, 0)),
        out_shape=jax.ShapeDtypeStruct((n // tn, k, tn), BF16),
        compiler_params=_params(1),
        name="weight_tiles",
    )
    if col_scale is None:
        return pl.pallas_call(_cast_tiles_kernel, in_specs=[w_spec], **common)(w)
    s_spec = pl.BlockSpec((1, tn), lambda j: (0, j))
    return pl.pallas_call(_scale_cast_tiles_kernel, in_specs=[w_spec, s_spec], **common)(w, col_scale.reshape(1, n))


def _norm_matmul_kernel(x_ref, g_ref, w_ref, o_ref, a_ref, *, silu_tiles):
    j = pl.program_id(1)

    @pl.when(j == 0)
    def _():
        def chunk(rows):
            a_ref[rows, :] = _rmsnorm_rows(x_ref[rows, :], g_ref[...]).astype(BF16)
        _for_row_chunks(a_ref.shape[0], chunk)

    acc = _dot(a_ref[...], w_ref[...])
    if silu_tiles is None:
        o_ref[...] = acc.astype(o_ref.dtype)
    else:
        gated = jnp.logical_and(j >= silu_tiles[0], j < silu_tiles[1])

        @pl.when(gated)
        def _():
            o_ref[...] = _silu(acc).astype(o_ref.dtype)

        @pl.when(jnp.logical_not(gated))
        def _():
            o_ref[...] = acc.astype(o_ref.dtype)


MATMUL_TN = 1024


def _norm_matmul(x, g, w_tiles, out_dtype, silu_cols=None, tm=1024):
    m, k = x.shape
    tn = w_tiles.shape[2]
    n = w_tiles.shape[0] * tn
    tm = min(tm, m)
    silu_tiles = None if silu_cols is None else (silu_cols[0] // tn, silu_cols[1] // tn)
    return pl.pallas_call(
        functools.partial(_norm_matmul_kernel, silu_tiles=silu_tiles),
        grid=(m // tm, n // tn),
        in_specs=[
            pl.BlockSpec((tm, k), lambda i, j: (i, 0)),
            pl.BlockSpec((1, k), lambda i, j: (0, 0)),
            pl.BlockSpec((None, k, tn), lambda i, j: (j, 0, 0)),
        ],
        out_specs=pl.BlockSpec((tm, tn), lambda i, j: (i, j)),
        out_shape=jax.ShapeDtypeStruct((m, n), out_dtype),
        scratch_shapes=[pltpu.VMEM((tm, k), BF16)],
        compiler_params=_params(2),
        name="norm_matmul",
    )(x, g.reshape(1, k), w_tiles)


def _out_ple_kernel(a_ref, wo_ref, x_ref, g_ref, wg_ref, p_ref, wp_ref, gf_ref, o_ref, x1_ref, hn_ref, ss_ref, *, nt, final):
    j = pl.program_id(1)
    tn = x_ref.shape[1]
    n = nt * tn

    @pl.when(j == 0)
    def _():
        ss_ref[...] = jnp.zeros(ss_ref.shape, F32)

    @pl.when(j < nt)
    def _():
        x1 = x_ref[...] + _dot(a_ref[...], wo_ref[j])
        x1_ref[j] = x1
        hn_ref[j] = (x1 * g_ref[j]).astype(BF16)
        ss_ref[0] += jnp.sum(x1 * x1, axis=-1, keepdims=True)

    @pl.when(jnp.logical_and(j >= nt, j < 2 * nt))
    def _():
        t = j - nt
        emb = _dot(p_ref[...].astype(BF16), wp_ref[t])
        acc = _dot(hn_ref[0], wg_ref[t, :tn, :])
        for kt in range(1, nt):
            acc = acc + _dot(hn_ref[kt], wg_ref[t, kt * tn:(kt + 1) * tn, :])
        x2 = x1_ref[t] + emb * jax.nn.sigmoid(acc * lax.rsqrt(ss_ref[0] / n + EPS))
        if final:
            x1_ref[t] = x2
            ss_ref[1] += jnp.sum(x2 * x2, axis=-1, keepdims=True)
        else:
            o_ref[...] = x2

    if final:
        @pl.when(j >= 2 * nt)
        def _():
            t = j - 2 * nt
            o_ref[...] = x1_ref[t] * lax.rsqrt(ss_ref[1] / n + EPS) * gf_ref[t]


def _out_ple(a, w_out, x, g, w_gate, p, w_proj, layer, final_g=None, tm=1024, tn=512):
    m, k = a.shape
    n = w_out.shape[1]
    pd = p.shape[2]
    final = final_g is not None
    tm, tn = min(tm, m), min(tn, n)
    nt = n // tn
    passes = 3 if final else 2
    out_tile = lambda j: jnp.maximum(j - (passes - 1) * nt, 0)
    resident = lambda rows: pl.BlockSpec((nt, rows, tn), lambda i, j: (0, 0, 0), pipeline_mode=pl.Buffered(1))
    row_vec = pl.BlockSpec((nt, 1, tn), lambda i, j: (0, 0, 0))
    return pl.pallas_call(
        functools.partial(_out_ple_kernel, nt=nt, final=final),
        grid=(m // tm, passes * nt),
        in_specs=[
            pl.BlockSpec((tm, k), lambda i, j: (i, 0)),
            resident(k),
            pl.BlockSpec((tm, tn), lambda i, j: (i, jnp.minimum(j, nt - 1))),
            row_vec,
            resident(n),
            pl.BlockSpec((None, tm, pd), lambda i, j: (layer, i, 0)),
            resident(pd),
            row_vec,
        ],
        out_specs=pl.BlockSpec((tm, tn), lambda i, j: (i, out_tile(j))),
        out_shape=jax.ShapeDtypeStruct((m, n), F32),
        scratch_shapes=[pltpu.VMEM((nt, tm, tn), F32), pltpu.VMEM((nt, tm, tn), BF16), pltpu.VMEM((2, tm, 1), F32)],
        compiler_params=_params(2, VMEM_LIMIT_BIG),
        name="out_ple_final" if final else "out_ple",
    )(a, _tile_cols(w_out[None], 0, tn), x, g.reshape(nt, 1, tn), _tile_cols(w_gate, layer, tn), p,
      _tile_cols(w_proj, layer, tn), (final_g if final else g).reshape(nt, 1, tn))


def _t5_bucket(dist):
    max_exact = REL_BUCKETS // 2
    d = np.maximum(dist, 1).astype(np.float32)
    large = max_exact + (np.log(d / max_exact) / np.log(REL_MAX_DIST / max_exact) * (REL_BUCKETS - max_exact)).astype(np.int32)
    large = np.minimum(large, REL_BUCKETS - 1)
    return np.where(dist < max_exact, dist, large).astype(np.int32)


def _swa_bucket_table():
    qi = np.arange(SWA_BLOCK)[None, :]
    kj = np.arange(2 * SWA_BLOCK)[:, None]
    return _t5_bucket(np.clip(qi + SWA_BLOCK - kj, 0, None))


def _swa_kernel(sink_ref, rel_ref, bucket_ref, q_ref, sg_ref, kp_ref, kc_ref, vp_ref, vc_ref, o_ref, bias_ref, qs_ref):
    n = pl.program_id(1)
    blk = SWA_BLOCK
    lane = lax.broadcasted_iota(jnp.int32, (1, LANES), 1)

    @pl.when(jnp.logical_and(pl.program_id(0) == 0, n == 0))
    def _():
        bucket = bucket_ref[...]
        kj = lax.broadcasted_iota(jnp.int32, (2 * blk, blk), 0)
        qi = lax.broadcasted_iota(jnp.int32, (2 * blk, blk), 1)
        band = jnp.logical_or(jnp.logical_and(kj < blk, kj > qi), jnp.logical_and(kj >= blk, kj - blk <= qi))

        def per_head(head, carry):
            acc = jnp.zeros((2 * blk, blk), F32)
            for b in range(REL_BUCKETS):
                acc = jnp.where(bucket == b, rel_ref[b, head], acc)
            bias_ref[head] = jnp.where(band, acc, NEG)
            return carry
        lax.fori_loop(0, SWA_HEADS, per_head, 0)

    for kvh in range(SWA_KV_HEADS):
        cols = slice(kvh * LANES, (kvh + 1) * LANES)
        heads = range(kvh * SWA_GROUP, (kvh + 1) * SWA_GROUP)
        for g, head in enumerate(heads):
            q_pair = q_ref[:, head // 2 * LANES:(head // 2 + 1) * LANES]
            qs_ref[g * blk:(g + 1) * blk, :] = jnp.where((lane // HEAD_DIM) == head % 2, q_pair, jnp.zeros_like(q_pair))
        keys = jnp.concatenate([kp_ref[:, cols], kc_ref[:, cols]], axis=0)
        vals = jnp.concatenate([vp_ref[:, cols], vc_ref[:, cols]], axis=0)
        s = _dot_nt(keys, qs_ref[...]) + jnp.concatenate([bias_ref[head] for head in heads], axis=1)
        s_prev = jnp.where(n > 0, s[:blk], NEG)
        s_cur = s[blk:]
        sink = jnp.concatenate([jnp.full((1, blk), sink_ref[head], F32) for head in heads], axis=1)
        m = jnp.maximum(jnp.maximum(s_prev, s_cur).max(axis=0, keepdims=True), sink)
        e_prev, e_cur = jnp.exp(s_prev - m), jnp.exp(s_cur - m)
        den = e_prev.sum(axis=0, keepdims=True) + e_cur.sum(axis=0, keepdims=True) + jnp.exp(sink - m)
        e = jnp.concatenate([e_prev, e_cur], axis=0).astype(BF16)
        out = _dot(vals.astype(F32).T.astype(BF16), e) / den
        for pair in range(SWA_GROUP // 2):
            head = kvh * SWA_GROUP + 2 * pair
            even = out[:HEAD_DIM, 2 * pair * blk:(2 * pair + 1) * blk]
            odd = out[HEAD_DIM:, (2 * pair + 1) * blk:(2 * pair + 2) * blk]
            cols = slice(head // 2 * LANES, (head // 2 + 1) * LANES)
            o_ref[:, cols] = (jnp.concatenate([even, odd], axis=0).T * sg_ref[:, cols].astype(F32)).astype(o_ref.dtype)


def _swa_attention(proj, sinks, rel_bias, bsz, seq):
    nb = seq // SWA_BLOCK
    width = SWA_HEADS * HEAD_DIM
    kvw = SWA_KV_HEADS * LANES
    k_blk = 2 * width // kvw
    cur = lambda b, n: b * nb + n
    prev = lambda b, n: b * nb + jnp.maximum(n - 1, 0)
    smem = pl.BlockSpec(memory_space=pltpu.SMEM)
    return pl.pallas_call(
        _swa_kernel,
        grid=(bsz, nb),
        in_specs=[
            smem, smem,
            pl.BlockSpec((2 * SWA_BLOCK, SWA_BLOCK), lambda b, n: (0, 0)),
            pl.BlockSpec((SWA_BLOCK, width), lambda b, n: (cur(b, n), 0)),
            pl.BlockSpec((SWA_BLOCK, width), lambda b, n: (cur(b, n), 1)),
            pl.BlockSpec((SWA_BLOCK, kvw), lambda b, n: (prev(b, n), k_blk)),
            pl.BlockSpec((SWA_BLOCK, kvw), lambda b, n: (cur(b, n), k_blk)),
            pl.BlockSpec((SWA_BLOCK, kvw), lambda b, n: (prev(b, n), k_blk + 1)),
            pl.BlockSpec((SWA_BLOCK, kvw), lambda b, n: (cur(b, n), k_blk + 1)),
        ],
        out_specs=pl.BlockSpec((SWA_BLOCK, width), lambda b, n: (cur(b, n), 0)),
        out_shape=jax.ShapeDtypeStruct((bsz * seq, width), BF16),
        scratch_shapes=[
            pltpu.VMEM((SWA_HEADS, 2 * SWA_BLOCK, SWA_BLOCK), F32),
            pltpu.VMEM((SWA_GROUP * SWA_BLOCK, LANES), BF16),
        ],
        compiler_params=_params(2),
        name="swa_attention",
    )(sinks.astype(F32), rel_bias.astype(F32), jnp.asarray(_swa_bucket_table()), proj, proj, proj, proj, proj, proj)


def _dup_heads(w, n_heads):
    k = w.shape[0]
    w = w.reshape(k, n_heads, 1, HEAD_DIM)
    return jnp.broadcast_to(w, (k, n_heads, 2, HEAD_DIM)).reshape(k, n_heads * 2 * HEAD_DIM)


def _swa_mixer(x, g, w_in, sinks, rel_bias, bsz, seq):
    width = SWA_HEADS * HEAD_DIM
    kvw = SWA_KV_HEADS * HEAD_DIM
    w_q, w_k, w_v, w_g = (w_in[:, :width], w_in[:, width:width + kvw],
                          w_in[:, width + kvw:width + 2 * kvw], w_in[:, width + 2 * kvw:])
    w = jnp.concatenate([w_q * (HEAD_DIM ** -0.5), w_g, _dup_heads(w_k, SWA_KV_HEADS), _dup_heads(w_v, SWA_KV_HEADS)], axis=1)
    proj = _norm_matmul(x, g, _tile_cols_xla(w, MATMUL_TN), BF16, silu_cols=(width, 2 * width))
    return _swa_attention(proj, sinks, rel_bias, bsz, seq)


HALO = 8


def _conv_kernel(x_ref, g_ref, wb_ref, wc_ref, wu_ref, wg_ref, ck_ref, o_ref, a_ref, z_ref, carry_ref, *, tiles_per_seq):
    i, j = pl.program_id(0), pl.program_id(1)
    tm = a_ref.shape[0]

    @pl.when(j == 0)
    def _():
        def chunk(rows):
            a_ref[rows, :] = _rmsnorm_rows(x_ref[rows, :], g_ref[...]).astype(BF16)
        _for_row_chunks(tm, chunk)

    a = a_ref[...]
    z = _dot(a, wc_ref[...]) * _dot(a, wu_ref[...])
    first = (i % tiles_per_seq) == 0

    @pl.when(first)
    def _():
        z_ref[:HALO, :] = jnp.zeros((HALO, z_ref.shape[1]), F32)

    @pl.when(jnp.logical_not(first))
    def _():
        z_ref[:HALO, :] = carry_ref[j]

    z_ref[HALO:, :] = z
    carry_ref[j] = z[tm - HALO:, :]
    conv = z_ref[HALO - 2:HALO - 2 + tm, :] * ck_ref[0:1, :]
    conv = conv + z_ref[HALO - 1:HALO - 1 + tm, :] * ck_ref[1:2, :]
    conv = conv + z * ck_ref[2:3, :]
    y = _dot(a, wb_ref[...]) * conv
    o_ref[...] = (y * _silu(_dot(a, wg_ref[...]))).astype(o_ref.dtype)


def _conv_mixer(x, g, w_in, layer, conv_kernel, seq, tm=1024, tn=512):
    m, k = x.shape
    width = w_in.shape[2] // 4
    tm, tn = min(tm, seq), min(tn, width)
    nt = width // tn
    w_spec = lambda q: pl.BlockSpec((None, k, tn), lambda i, j: (q * nt + j, 0, 0))
    w_in = _tile_cols(w_in, layer, tn)
    return pl.pallas_call(
        functools.partial(_conv_kernel, tiles_per_seq=seq // tm),
        grid=(m // tm, nt),
        in_specs=[
            pl.BlockSpec((tm, k), lambda i, j: (i, 0)),
            pl.BlockSpec((1, k), lambda i, j: (0, 0)),
            w_spec(0), w_spec(1), w_spec(2), w_spec(3),
            pl.BlockSpec((CONV_TAPS, tn), lambda i, j: (0, j)),
        ],
        out_specs=pl.BlockSpec((tm, tn), lambda i, j: (i, j)),
        out_shape=jax.ShapeDtypeStruct((m, width), BF16),
        scratch_shapes=[
            pltpu.VMEM((tm, k), BF16),
            pltpu.VMEM((HALO + tm, tn), F32),
            pltpu.VMEM((nt, HALO, tn), F32),
        ],
        compiler_params=_params(2, VMEM_LIMIT_BIG),
        name="conv_mixer",
    )(x, g.reshape(1, k), w_in, w_in, w_in, w_in, conv_kernel.astype(F32))


def _ssm_operators(lam_re, lam_im, log_dt, b_re, b_im, c_re, c_im, d_skip):
    n_groups = lam_re.shape[0]
    n_oct = n_groups // GROUPS_PER_TILE
    L, C, N = SSM_CHUNK, SSM_GROUP, SSM_STATE
    dt = jnp.exp(log_dt.astype(F32))[None, :]
    lr, li = lam_re.astype(F32).T, lam_im.astype(F32).T
    mag = jnp.exp(lr * dt)
    ab_re, ab_im = mag * jnp.cos(li * dt), mag * jnp.sin(li * dt)
    den = lr * lr + li * li
    nr = ab_re - 1.0
    coef_re = (nr * lr + ab_im * li) / den
    coef_im = (ab_im * lr - nr * li) / den
    br, bi = b_re.astype(F32).transpose(2, 1, 0), b_im.astype(F32).transpose(2, 1, 0)
    bb_re = coef_re * br - coef_im * bi
    bb_im = coef_re * bi + coef_im * br
    cr, ci = c_re.astype(F32).transpose(1, 2, 0), c_im.astype(F32).transpose(1, 2, 0)
    pw_re, pw_im = [jnp.ones_like(ab_re)], [jnp.zeros_like(ab_im)]
    for _ in range(L):
        pw_re, pw_im = (pw_re + [pw_re[-1] * ab_re - pw_im[-1] * ab_im], pw_im + [pw_re[-1] * ab_im + pw_im[-1] * ab_re])
    p_re, p_im = jnp.stack(pw_re), jnp.stack(pw_im)
    w_re = cr[None] * p_re[:L, None] - ci[None] * p_im[:L, None]
    w_im = cr[None] * p_im[:L, None] + ci[None] * p_re[:L, None]
    lag_k = jnp.sum(w_re[:, None] * bb_re[None, :, None] - w_im[:, None] * bb_im[None, :, None], axis=3)
    q_re, q_im = jnp.stack(pw_re[L - 1::-1]), jnp.stack(pw_im[L - 1::-1])
    bc_re = q_re[:, None] * bb_re[None] - q_im[:, None] * bb_im[None]
    bc_im = q_re[:, None] * bb_im[None] + q_im[:, None] * bb_re[None]
    o_re = cr[None] * p_re[1:, None] - ci[None] * p_im[1:, None]
    o_im = cr[None] * p_im[1:, None] + ci[None] * p_re[1:, None]
    octs = lambda t: t.reshape(t.shape[:-1] + (n_oct, GROUPS_PER_TILE))
    to_lag = lambda t: octs(t).transpose(3, 0, 1, 4, 2).reshape(n_oct, L, C, LANES)
    to_rows = lambda t: octs(t).transpose(3, 0, 1, 4, 2).reshape(n_oct, L * C, GROUPS_PER_TILE * N)
    to_cols = lambda t: octs(t).transpose(3, 4, 2, 0, 1).reshape(n_oct, GROUPS_PER_TILE * N, L * C)
    to_vec = lambda t: octs(t).transpose(1, 2, 0).reshape(n_oct, 1, GROUPS_PER_TILE * N)
    return dict(
        lag=to_lag(lag_k).astype(BF16),
        bc_re=to_rows(bc_re).astype(BF16), bc_im=to_rows(bc_im).astype(BF16),
        oc_re=to_cols(o_re).astype(BF16), oc_im_neg=to_cols(-o_im).astype(BF16),
        al_re=to_vec(p_re[L]), al_im=to_vec(p_im[L]), d=d_skip.astype(F32).reshape(n_oct, 1, LANES),
    )


def _ssm_expanders():
    L, C, N, G8 = SSM_CHUNK, SSM_GROUP, SSM_STATE, GROUPS_PER_TILE
    wide = np.arange(L * LANES)
    w_step, w_group, w_chan = wide // LANES, (wide // C) % G8, wide % C
    small = np.arange(L * C)
    s_step, s_chan = small // C, small % C
    state_group = np.arange(G8 * N) // N
    spread = (w_step[:, None] == s_step[None, :]) & (w_chan[:, None] == s_chan[None, :])
    lane = np.arange(LANES)
    as_bf16 = lambda a: jnp.asarray(a.astype(np.float32), dtype=BF16)
    return dict(
        spread_cols=as_bf16(spread.T),
        keep_cols=as_bf16(state_group[:, None] == w_group[None, :]),
        spread_lag=as_bf16((lane % C)[:, None] == np.arange(C)[None, :]),
        keep_lag=jnp.asarray(((lane // C)[:, None] == (lane // C)[None, :]).astype(np.float32)),
    )


def _ssm_kernel(u_ref, lag_ref, bxr_ref, bxi_ref, cxr_ref, cxi_ref, alr_ref, ali_ref, d_ref,
                sc_ref, kc_ref, sl_ref, kl_ref, o_ref,
                ub_ref, bcr_ref, bci_ref, ocr_ref, oci_ref, zr_ref, zi_ref, hr_ref, hi_ref, *, bsz):
    L = SSM_CHUNK
    rows = u_ref.shape[0] // L
    chunks = rows // bsz
    tile = 2 * LANES
    step_rows = lambda r: pl.ds(r, rows, stride=L)
    for r in range(L):
        ub_ref[:, r * LANES:(r + 1) * LANES] = u_ref[step_rows(r), :].astype(BF16)
    state_group = lax.broadcasted_iota(jnp.int32, (1, bxr_ref.shape[1]), 1) // SSM_STATE
    for compact_ref, full_ref in ((bxr_ref, bcr_ref), (bxi_ref, bci_ref)):
        for r in range(L):
            blk = compact_ref[r * SSM_GROUP:(r + 1) * SSM_GROUP, :]
            for grp in range(GROUPS_PER_TILE):
                first = r * LANES + grp * SSM_GROUP
                full_ref[first:first + SSM_GROUP, :] = jnp.where(state_group == grp, blk, jnp.zeros_like(blk))
    zr_ref[...] = _dot(ub_ref[...], bcr_ref[...])
    zi_ref[...] = _dot(ub_ref[...], bci_ref[...])
    a_re, a_im = alr_ref[...], ali_ref[...]

    def step(k, carry):
        new = []
        for b in range(bsz):
            h_re, h_im = carry[2 * b], carry[2 * b + 1]
            row = pl.ds(b * chunks + k, 1)
            hr_ref[row, :] = h_re
            hi_ref[row, :] = h_im
            new.append(a_re * h_re - a_im * h_im + zr_ref[row, :])
            new.append(a_re * h_im + a_im * h_re + zi_ref[row, :])
        return tuple(new)

    zero = jnp.zeros((1, a_re.shape[1]), F32)
    lax.fori_loop(0, chunks, step, (zero,) * (2 * bsz))

    ocr_ref[...] = (_dot(cxr_ref[...], sc_ref[...]) * kc_ref[...]).astype(BF16)
    oci_ref[...] = (_dot(cxi_ref[...], sc_ref[...]) * kc_ref[...]).astype(BF16)
    lag = [(_dot(sl_ref[...], lag_ref[d]) * kl_ref[...]).astype(BF16) for d in range(L)]
    lag_tile = []
    for dd in range(L // 2):
        below = lag[2 * dd - 1] if dd > 0 else jnp.zeros((LANES, LANES), BF16)
        lag_tile.append(jnp.concatenate([jnp.concatenate([lag[2 * dd], lag[2 * dd + 1]], axis=1),
                                         jnp.concatenate([below, lag[2 * dd]], axis=1)], axis=0))
    hb_re, hb_im = hr_ref[...].astype(BF16), hi_ref[...].astype(BF16)
    for t2 in range(L // 2):
        cols = slice(t2 * tile, (t2 + 1) * tile)
        y = _dot(hb_re, ocr_ref[:, cols]) + _dot(hb_im, oci_ref[:, cols])
        for r2 in range(t2 + 1):
            y = y + _dot(ub_ref[:, r2 * tile:(r2 + 1) * tile], lag_tile[t2 - r2])
        for half in range(2):
            t = 2 * t2 + half
            o_ref[step_rows(t), :] = jax.nn.gelu(y[:, half * LANES:(half + 1) * LANES] + d_ref[...] * u_ref[step_rows(t), :])


def _ssm_core(proj, ops, bsz, seq):
    m = proj.shape[0]
    width = proj.shape[1] // 2
    L, C = SSM_CHUNK, SSM_GROUP
    n_oct = width // LANES
    rows = m // L
    ow = L * LANES
    n_state = GROUPS_PER_TILE * SSM_STATE
    ex = _ssm_expanders()
    per_oct = lambda *shape: pl.BlockSpec((None,) + shape, lambda p: (p,) + (0,) * len(shape))
    const = lambda a: pl.BlockSpec(a.shape, lambda p: (0,) * a.ndim, pipeline_mode=pl.Buffered(1))
    consts = [ex['spread_cols'], ex['keep_cols'], ex['spread_lag'], ex['keep_lag']]
    return pl.pallas_call(
        functools.partial(_ssm_kernel, bsz=bsz),
        grid=(n_oct,),
        in_specs=[
            pl.BlockSpec((m, LANES), lambda p: (0, p)),
            per_oct(L, C, LANES), per_oct(L * C, n_state), per_oct(L * C, n_state),
            per_oct(n_state, L * C), per_oct(n_state, L * C), per_oct(1, n_state), per_oct(1, n_state), per_oct(1, LANES),
        ] + [const(a) for a in consts],
        out_specs=pl.BlockSpec((m, LANES), lambda p: (0, p)),
        out_shape=jax.ShapeDtypeStruct((m, width), F32),
        scratch_shapes=[pltpu.VMEM((rows, ow), BF16)] + [pltpu.VMEM((ow, n_state), BF16)] * 2
        + [pltpu.VMEM((n_state, ow), BF16)] * 2 + [pltpu.VMEM((rows, n_state), F32)] * 4,
        compiler_params=_params(1),
        name="ssm_core",
    )(proj, ops['lag'], ops['bc_re'], ops['bc_im'], ops['oc_re'], ops['oc_im_neg'], ops['al_re'], ops['al_im'], ops['d'], *consts)


def _glu_kernel(y_ref, wa_ref, wb_ref, ba_ref, bb_ref, gate_ref, o_ref, yb_ref):
    @pl.when(pl.program_id(1) == 0)
    def _():
        def chunk(rows):
            yb_ref[rows, :] = y_ref[rows, :].astype(BF16)
        _for_row_chunks(yb_ref.shape[0], chunk)

    y = yb_ref[...]
    ga = _dot(y, wa_ref[...]) + ba_ref[...]
    gb = _dot(y, wb_ref[...]) + bb_ref[...]
    o_ref[...] = ((ga * jax.nn.sigmoid(gb)) * _silu(gate_ref[...])).astype(o_ref.dtype)


def _glu(y, w_glu, layer, b_glu, proj, tm=1024, tn=512):
    m, k = y.shape
    width = w_glu.shape[2] // 2
    tm, tn = min(tm, m), min(tn, width)
    nt = width // tn
    w_tiles = _tile_cols(w_glu, layer, tn)
    return pl.pallas_call(
        _glu_kernel,
        grid=(m // tm, nt),
        in_specs=[
            pl.BlockSpec((tm, k), lambda i, j: (i, 0)),
            pl.BlockSpec((None, k, tn), lambda i, j: (j, 0, 0)),
            pl.BlockSpec((None, k, tn), lambda i, j: (nt + j, 0, 0)),
            pl.BlockSpec((1, tn), lambda i, j: (0, j)),
            pl.BlockSpec((1, tn), lambda i, j: (0, nt + j)),
            pl.BlockSpec((tm, tn), lambda i, j: (i, nt + j)),
        ],
        out_specs=pl.BlockSpec((tm, tn), lambda i, j: (i, j)),
        out_shape=jax.ShapeDtypeStruct((m, width), BF16),
        scratch_shapes=[pltpu.VMEM((tm, k), BF16)],
        compiler_params=_params(2),
        name="ssm_glu",
    )(y, w_tiles, w_tiles, b_glu, b_glu, proj)


def _ssm_mixer(x, g, w_in, layer, lam_re, lam_im, log_dt, b_re, b_im, c_re, c_im, d_skip, w_glu, b_glu, bsz, seq):
    proj = _norm_matmul(x, g, _tile_cols(w_in, layer, MATMUL_TN), F32)
    ops = _ssm_operators(lam_re, lam_im, log_dt, b_re, b_im, c_re, c_im, d_skip)
    y = _ssm_core(proj, ops, bsz, seq)
    return _glu(y, w_glu, layer, b_glu.astype(F32).reshape(1, -1), proj)


CUM_BLOCK = 128
ONES_ROWS = 16


def _split3(x):
    x1 = x.astype(BF16)
    r1 = x - x1.astype(F32)
    x2 = r1.astype(BF16)
    x3 = (r1 - x2.astype(F32)).astype(BF16)
    return x1, x2, x3


def _forget_cumsum_kernel(z_ref, b_ref, o_ref):
    n_blocks = z_ref.shape[0] // CUM_BLOCK
    ri = lax.broadcasted_iota(jnp.int32, (CUM_BLOCK, CUM_BLOCK), 0)
    ci = lax.broadcasted_iota(jnp.int32, (CUM_BLOCK, CUM_BLOCK), 1)
    tri = jnp.where(ci <= ri, 1.0, 0.0).astype(BF16)

    def body(i, carry):
        rows = pl.ds(pl.multiple_of(i * CUM_BLOCK, CUM_BLOCK), CUM_BLOCK)
        z = z_ref[rows, :] + b_ref[...]
        log_f = jnp.minimum(z, 0.0) - jnp.log1p(jnp.exp(-jnp.abs(z)))
        x1, x2, x3 = _split3(log_f)
        c = _dot(tri, x1) + _dot(tri, x2) + _dot(tri, x3) + carry
        o_ref[rows, :] = c
        return c[CUM_BLOCK - 1:, :]

    lax.fori_loop(0, n_blocks, body, jnp.zeros((1, z_ref.shape[1]), F32))


def _forget_cumsum(z, b, bsz, seq):
    lanes = z.shape[1]
    return pl.pallas_call(
        _forget_cumsum_kernel,
        grid=(bsz,),
        in_specs=[pl.BlockSpec((seq, lanes), lambda i: (i, 0)), pl.BlockSpec((1, lanes), lambda i: (0, 0))],
        out_specs=pl.BlockSpec((seq, lanes), lambda i: (i, 0)),
        out_shape=jax.ShapeDtypeStruct(z.shape, F32),
        compiler_params=_params(1),
        name="forget_cumsum",
    )(z, b)


FOX_STEP_HEADS = 8


def _fox_kernel(q_ref, k_ref, v_ref, sg_ref, cq_ref, ck_ref, o_ref, ka_ref, vt_ref, *head_refs, tk):
    nh = FOX_STEP_HEADS
    hg, qi = pl.program_id(1), pl.program_id(2)
    tq = q_ref.shape[0]
    qa_ref, s_ref, p_ref, acc_ref, pv_ref, m_ref = (head_refs[i::6] for i in range(6))
    lane = lax.broadcasted_iota(jnp.int32, (1, LANES), 1)
    free = (HEAD_DIM, 0)
    n_pieces = 3
    pair_lanes = lambda h: slice(h // 2 * LANES, (h // 2 + 1) * LANES)

    def bias_lanes(c, h, c_offset, one_offset):
        rr = lax.broadcasted_iota(jnp.int32, (n_pieces * LANES, LANES), 0)
        cc = lax.broadcasted_iota(jnp.int32, (n_pieces * LANES, LANES), 1)
        place = jnp.logical_and(rr % LANES == nh * hg + h, cc == free[h % 2] + c_offset + rr // LANES)
        placed = _dot(jnp.concatenate(_split3(c), axis=1), jnp.where(place, 1.0, 0.0).astype(BF16))
        first = free[h % 2] + one_offset
        return (placed + jnp.where(jnp.logical_and(lane >= first, lane < first + n_pieces), 1.0, 0.0)).astype(BF16)

    @pl.when(qi == 0)
    def _():
        def chunk(j, carry):
            rows = pl.ds(pl.multiple_of(j * tk, tk), tk)
            neg_c = ck_ref[rows, :] * -LOG2E
            for h in range(nh):
                keys = k_ref[rows, pair_lanes(h)]
                ka_ref[h, j] = jnp.where((lane // HEAD_DIM) == h % 2, keys, bias_lanes(neg_c, h, 0, n_pieces))
            for pair in range(nh // 2):
                v_t = v_ref[rows, pair_lanes(2 * pair)].astype(F32).T.astype(BF16)
                for hh in range(2):
                    vt_ref[j, 2 * pair + hh, :HEAD_DIM, :] = v_t[hh * HEAD_DIM:(hh + 1) * HEAD_DIM]
                    vt_ref[j, 2 * pair + hh, HEAD_DIM:, :] = jnp.ones((ONES_ROWS, tk), BF16)
            return carry
        lax.fori_loop(0, k_ref.shape[0] // tk, chunk, 0)

    cq = cq_ref[...] * LOG2E
    for h in range(nh):
        qa = jnp.where((lane // HEAD_DIM) == h % 2, q_ref[:, pair_lanes(h)], bias_lanes(cq, h, n_pieces, 0))
        qa_ref[h][...] = qa.astype(F32).T.astype(BF16)

    def scores(h, j):
        s_ref[h][...] = _dot(ka_ref[h, j], qa_ref[h][...])

    rows8 = 8
    rows16 = 16

    def absorb(h, j, diagonal):
        def strip(r, n):
            blk = s_ref[h][r:r + n, :]
            if diagonal:
                kpos = lax.broadcasted_iota(jnp.int32, (n, tq), 0)
                qpos = lax.broadcasted_iota(jnp.int32, (n, tq), 1)
                blk = jnp.where(kpos + r <= qpos, blk, NEG)
            return blk
        top = strip(0, rows8)
        for r in range(rows8, tk, rows8):
            top = jnp.maximum(top, strip(r, rows8))
        m = m_ref[h][...]
        m_new = jnp.maximum(m, top.max(axis=0, keepdims=True))
        m_ref[h][...] = m_new
        alpha = jnp.exp2(m - m_new)
        m_rows = jnp.broadcast_to(m_new, (rows16, tq))
        for r in range(0, tk, rows16):
            p_ref[h][r:r + rows16, :] = jnp.exp2(strip(r, rows16) - m_rows).astype(BF16)
        acc_ref[h][...] = alpha * (acc_ref[h][...] + pv_ref[h][...])

    def values(h, j):
        pv_ref[h][...] = _dot(vt_ref[j, h], p_ref[h][...])

    def step(j, diagonal):
        values(nh - 1, jnp.maximum(j - 1, 0))
        scores(1, j)
        for h in range(nh):
            absorb(h, j, diagonal)
            if h < nh - 1:
                values(h, j)
            if h + 2 < nh:
                scores(h + 2, j)
            elif h + 2 == nh and not diagonal:
                scores(0, j + 1)

    for h in range(nh):
        acc_ref[h][...] = jnp.zeros(acc_ref[h].shape, F32)
        pv_ref[h][...] = jnp.zeros(pv_ref[h].shape, F32)
        m_ref[h][...] = jnp.full(m_ref[h].shape, NEG, F32)
    p_ref[nh - 1][...] = jnp.zeros(p_ref[nh - 1].shape, BF16)
    scores(0, 0)

    def two_blocks(i, carry):
        step(2 * i, False)
        step(2 * i + 1, False)
        return carry
    lax.fori_loop(0, qi // 2, two_blocks, 0)

    @pl.when(qi % 2 == 0)
    def _():
        step(qi, True)

    @pl.when(qi % 2 == 1)
    def _():
        step(qi - 1, False)
        step(qi, True)

    values(nh - 1, qi)
    outs = []
    for h in range(nh):
        total = acc_ref[h][...] + pv_ref[h][...]
        outs.append(total[:HEAD_DIM] / total[HEAD_DIM:HEAD_DIM + 1])
    out = jnp.concatenate(outs, axis=0)
    o_ref[...] = (out.T * sg_ref[...].astype(F32)).astype(o_ref.dtype)


def _fox_attention(proj, csum, bsz, seq, t=512):
    width = FOX_HEADS * HEAD_DIM
    nh = FOX_STEP_HEADS
    gw = nh * HEAD_DIM
    n_groups = width // gw
    t = min(t, seq)
    nq = seq // t
    return pl.pallas_call(
        functools.partial(_fox_kernel, tk=t),
        grid=(bsz, n_groups, nq),
        in_specs=[
            pl.BlockSpec((t, gw), lambda b, h, i: (b * nq + i, h)),
            pl.BlockSpec((seq, gw), lambda b, h, i: (b, n_groups + h), pipeline_mode=pl.Buffered(1)),
            pl.BlockSpec((seq, gw), lambda b, h, i: (b, 2 * n_groups + h), pipeline_mode=pl.Buffered(1)),
            pl.BlockSpec((t, gw), lambda b, h, i: (b * nq + i, 3 * n_groups + h)),
            pl.BlockSpec((t, LANES), lambda b, h, i: (b * nq + i, 0)),
            pl.BlockSpec((seq, LANES), lambda b, h, i: (b, 0), pipeline_mode=pl.Buffered(1)),
        ],
        out_specs=pl.BlockSpec((t, gw), lambda b, h, i: (b * nq + i, h)),
        out_shape=jax.ShapeDtypeStruct((bsz * seq, width), BF16),
        scratch_shapes=[
            pltpu.VMEM((nh, nq, t, LANES), BF16), pltpu.VMEM((nq, nh, HEAD_DIM + ONES_ROWS, t), BF16),
        ] + nh * [pltpu.VMEM((LANES, t), BF16), pltpu.VMEM((t, t), F32), pltpu.VMEM((t, t), BF16),
                  pltpu.VMEM((HEAD_DIM + ONES_ROWS, t), F32), pltpu.VMEM((HEAD_DIM + ONES_ROWS, t), F32),
                  pltpu.VMEM((1, t), F32)],
        compiler_params=_params(3, VMEM_LIMIT_BIG),
        name="fox_attention",
    )(proj, proj, proj, proj, csum, csum)


def _fox_mixer(x, g, w_in, layer, w_fg, b_fg, bsz, seq, t=512):
    width = FOX_HEADS * HEAD_DIM
    col_scale = jnp.where(jnp.arange(4 * width) < width, LOG2E * HEAD_DIM ** -0.5, 1.0).astype(F32)
    proj = _norm_matmul(x, g, _tile_cols(w_in, layer, MATMUL_TN, col_scale), BF16, silu_cols=(3 * width, 4 * width))
    pad = LANES - FOX_HEADS
    z = _norm_matmul(x, g, _tile_cols_xla(jnp.pad(w_fg, ((0, 0), (0, pad))), LANES), F32)
    csum = _forget_cumsum(z, jnp.pad(b_fg.astype(F32), (0, pad)).reshape(1, LANES), bsz, seq)
    return _fox_attention(proj, csum, bsz, seq, t)


def kernel(x, p, norm_g, final_g, rel_bias, swa_w_in, swa_w_out, swa_sinks, conv_w_in, conv_kernel, conv_w_out, ssm_w_in, ssm_lam_re, ssm_lam_im, ssm_log_dt, ssm_b_re, ssm_b_im, ssm_c_re, ssm_c_im, ssm_d, ssm_w_glu, ssm_b_glu, ssm_w_out, fox_w_in, fox_w_fg, fox_b_fg, fox_w_out, ple_proj, ple_norm, ple_gate):
    bsz, seq, d_model = x.shape
    depth = p.shape[0]
    h = x.astype(F32).reshape(bsz * seq, d_model)
    for i in range(depth):
        mixer, j = i % N_MIXERS, i // N_MIXERS
        if mixer == 0:
            a, w_out = _swa_mixer(h, norm_g[i], swa_w_in[j], swa_sinks[j], rel_bias, bsz, seq), swa_w_out[j]
        elif mixer == 1:
            a, w_out = _conv_mixer(h, norm_g[i], conv_w_in, j, conv_kernel[j], seq), conv_w_out[j]
        elif mixer == 2:
            a = _ssm_mixer(h, norm_g[i], ssm_w_in, j, ssm_lam_re[j], ssm_lam_im[j], ssm_log_dt[j], ssm_b_re[j], ssm_b_im[j],
                           ssm_c_re[j], ssm_c_im[j], ssm_d[j], ssm_w_glu, ssm_b_glu[j], bsz, seq)
            w_out = ssm_w_out[j]
        else:
            a, w_out = _fox_mixer(h, norm_g[i], fox_w_in, j, fox_w_fg[j], fox_b_fg[j], bsz, seq), fox_w_out[j]
        h = _out_ple(a, w_out, h, ple_norm[i], ple_gate, p.reshape(depth, bsz * seq, -1), ple_proj, i,
                     final_g=final_g if i == depth - 1 else None)
    return h.reshape(bsz, seq, d_model).astype(x.dtype)
```

```python
import functools
import math

import numpy as np
import jax
import jax.numpy as jnp
from jax import lax
from jax.experimental import pallas as pl
from jax.experimental.pallas import tpu as pltpu

F32 = jnp.float32
BF16 = jnp.bfloat16

EPS = 1e-6
N_MIXERS = 4
PLE_DIM = 256

SWA_HEADS = 32
SWA_KV_HEADS = 4
SWA_GROUP = SWA_HEADS // SWA_KV_HEADS
HEAD_DIM = 64
SWA_BLOCK = 128
WINDOW = 128
REL_BUCKETS = 32
REL_MAX_DIST = 128

CONV_TAPS = 3

SSM_GROUP = 16
SSM_STATE = 64
SSM_CHUNK = 16

FOX_HEADS = 32

LANES = 128
ONES_ROWS = 16
GROUPS_PER_TILE = LANES // SSM_GROUP
VMEM_LIMIT = 48 * 1024 * 1024
VMEM_LIMIT_BIG = 60 * 1024 * 1024

NEG = float(jnp.finfo(jnp.float32).min)
LOG2E = math.log2(math.e)


def _params(n_axes, vmem_limit=VMEM_LIMIT):
    return pltpu.CompilerParams(dimension_semantics=("arbitrary",) * n_axes, vmem_limit_bytes=vmem_limit)


def _dot(a, b):
    return jnp.dot(a, b, preferred_element_type=F32)


def _dot_nt(a, b):
    return lax.dot_general(a, b, (((1,), (1,)), ((), ())), preferred_element_type=F32)


def _rmsnorm_rows(x, g):
    return x * lax.rsqrt(jnp.mean(x * x, axis=-1, keepdims=True) + EPS) * g


def _silu(x):
    return x * jax.nn.sigmoid(x)


ROW_CHUNK = 64


def _for_row_chunks(n_rows, fn):
    def body(c, carry):
        fn(pl.ds(pl.multiple_of(c * ROW_CHUNK, ROW_CHUNK), ROW_CHUNK))
        return carry
    lax.fori_loop(0, n_rows // ROW_CHUNK, body, 0)


def _tile_cols_xla(w, tn):
    k, n = w.shape
    return w.astype(BF16).reshape(k, n // tn, tn).transpose(1, 0, 2)


def _cast_tiles_kernel(w_ref, o_ref):
    def chunk(rows):
        o_ref[rows, :] = w_ref[rows, :].astype(BF16)
    _for_row_chunks(o_ref.shape[0], chunk)


def _scale_cast_tiles_kernel(w_ref, s_ref, o_ref):
    def chunk(rows):
        o_ref[rows, :] = (w_ref[rows, :] * s_ref[...]).astype(BF16)
    _for_row_chunks(o_ref.shape[0], chunk)


def _tile_cols(w, layer, tn, col_scale=None):
    _, k, n = w.shape
    tn = min(tn, n)
    w_spec = pl.BlockSpec((None, k, tn), lambda j: (layer, 0, j))
    common = dict(
        grid=(n // tn,),
        out_specs=pl.BlockSpec((None, k, tn), lambda j: (j, 0, 0)),
        out_shape=jax.ShapeDtypeStruct((n // tn, k, tn), BF16),
        compiler_params=_params(1),
        name="weight_tiles",
    )
    if col_scale is None:
        return pl.pallas_call(_cast_tiles_kernel, in_specs=[w_spec], **common)(w)
    s_spec = pl.BlockSpec((1, tn), lambda j: (0, j))
    return pl.pallas_call(_scale_cast_tiles_kernel, in_specs=[w_spec, s_spec], **common)(w, col_scale.reshape(1, n))


def _norm_matmul_kernel(x_ref, g_ref, w_ref, o_ref, a_ref, *, silu_tiles):
    j = pl.program_id(1)

    @pl.when(j == 0)
    def _():
        def chunk(rows):
            a_ref[rows, :] = _rmsnorm_rows(x_ref[rows, :], g_ref[...]).astype(BF16)
        _for_row_chunks(a_ref.shape[0], chunk)

    acc = _dot(a_ref[...], w_ref[...])
    if silu_tiles is None:
        o_ref[...] = acc.astype(o_ref.dtype)
    else:
        gated = jnp.logical_and(j >= silu_tiles[0], j < silu_tiles[1])

        @pl.when(gated)
        def _():
            o_ref[...] = _silu(acc).astype(o_ref.dtype)

        @pl.when(jnp.logical_not(gated))
        def _():
            o_ref[...] = acc.astype(o_ref.dtype)


MATMUL_TN = 1024


def _norm_matmul(x, g, w_tiles, out_dtype, silu_cols=None, tm=1024):
    m, k = x.shape
    tn = w_tiles.shape[2]
    n = w_tiles.shape[0] * tn
    tm = min(tm, m)
    silu_tiles = None if silu_cols is None else (silu_cols[0] // tn, silu_cols[1] // tn)
    return pl.pallas_call(
        functools.partial(_norm_matmul_kernel, silu_tiles=silu_tiles),
        grid=(m // tm, n // tn),
        in_specs=[
            pl.BlockSpec((tm, k), lambda i, j: (i, 0)),
            pl.BlockSpec((1, k), lambda i, j: (0, 0)),
            pl.BlockSpec((None, k, tn), lambda i, j: (j, 0, 0)),
        ],
        out_specs=pl.BlockSpec((tm, tn), lambda i, j: (i, j)),
        out_shape=jax.ShapeDtypeStruct((m, n), out_dtype),
        scratch_shapes=[pltpu.VMEM((tm, k), BF16)],
        compiler_params=_params(2),
        name="norm_matmul",
    )(x, g.reshape(1, k), w_tiles)


def _out_ple_kernel(a_ref, wo_ref, x_ref, g_ref, wg_ref, p_ref, wp_ref, gf_ref, o_ref, x1_ref, hn_ref, ss_ref, *, nt, final):
    j = pl.program_id(1)
    tn = x_ref.shape[1]
    n = nt * tn

    @pl.when(j == 0)
    def _():
        ss_ref[...] = jnp.zeros(ss_ref.shape, F32)

    @pl.when(j < nt)
    def _():
        x1 = x_ref[...] + _dot(a_ref[...], wo_ref[j])
        x1_ref[j] = x1
        hn_ref[j] = (x1 * g_ref[j]).astype(BF16)
        ss_ref[0] += jnp.sum(x1 * x1, axis=-1, keepdims=True)

    @pl.when(jnp.logical_and(j >= nt, j < 2 * nt))
    def _():
        t = j - nt
        emb = _dot(p_ref[...].astype(BF16), wp_ref[t])
        acc = _dot(hn_ref[0], wg_ref[t, :tn, :])
        for kt in range(1, nt):
            acc = acc + _dot(hn_ref[kt], wg_ref[t, kt * tn:(kt + 1) * tn, :])
        x2 = x1_ref[t] + emb * jax.nn.sigmoid(acc * lax.rsqrt(ss_ref[0] / n + EPS))
        if final:
            x1_ref[t] = x2
            ss_ref[1] += jnp.sum(x2 * x2, axis=-1, keepdims=True)
        else:
            o_ref[...] = x2

    if final:
        @pl.when(j >= 2 * nt)
        def _():
            t = j - 2 * nt
            o_ref[...] = x1_ref[t] * lax.rsqrt(ss_ref[1] / n + EPS) * gf_ref[t]


def _out_ple(a, w_out, x, g, w_gate, p, w_proj, layer, final_g=None, tm=1024, tn=512):
    m, k = a.shape
    n = w_out.shape[1]
    pd = p.shape[2]
    final = final_g is not None
    tm, tn = min(tm, m), min(tn, n)
    nt = n // tn
    passes = 3 if final else 2
    out_tile = lambda j: jnp.maximum(j - (passes - 1) * nt, 0)
    resident = lambda rows: pl.BlockSpec((nt, rows, tn), lambda i, j: (0, 0, 0), pipeline_mode=pl.Buffered(1))
    row_vec = pl.BlockSpec((nt, 1, tn), lambda i, j: (0, 0, 0))
    return pl.pallas_call(
        functools.partial(_out_ple_kernel, nt=nt, final=final),
        grid=(m // tm, passes * nt),
        in_specs=[
            pl.BlockSpec((tm, k), lambda i, j: (i, 0)),
            resident(k),
            pl.BlockSpec((tm, tn), lambda i, j: (i, jnp.minimum(j, nt - 1))),
            row_vec,
            resident(n),
            pl.BlockSpec((None, tm, pd), lambda i, j: (layer, i, 0)),
            resident(pd),
            row_vec,
        ],
        out_specs=pl.BlockSpec((tm, tn), lambda i, j: (i, out_tile(j))),
        out_shape=jax.ShapeDtypeStruct((m, n), F32),
        scratch_shapes=[pltpu.VMEM((nt, tm, tn), F32), pltpu.VMEM((nt, tm, tn), BF16), pltpu.VMEM((2, tm, 1), F32)],
        compiler_params=_params(2, VMEM_LIMIT_BIG),
        name="out_ple_final" if final else "out_ple",
    )(a, _tile_cols(w_out[None], 0, tn), x, g.reshape(nt, 1, tn), _tile_cols(w_gate, layer, tn), p,
      _tile_cols(w_proj, layer, tn), (final_g if final else g).reshape(nt, 1, tn))


def _t5_bucket(dist):
    max_exact = REL_BUCKETS // 2
    d = np.maximum(dist, 1).astype(np.float32)
    large = max_exact + (np.log(d / max_exact) / np.log(REL_MAX_DIST / max_exact) * (REL_BUCKETS - max_exact)).astype(np.int32)
    large = np.minimum(large, REL_BUCKETS - 1)
    return np.where(dist < max_exact, dist, large).astype(np.int32)


def _swa_bucket_table():
    qi = np.arange(SWA_BLOCK)[None, :]
    kj = np.arange(2 * SWA_BLOCK)[:, None]
    return _t5_bucket(np.clip(qi + SWA_BLOCK - kj, 0, None))


def _swa_kernel(sink_ref, rel_ref, bucket_ref, q_ref, sg_ref, kp_ref, kc_ref, vp_ref, vc_ref, o_ref, bias_ref, qs_ref):
    n = pl.program_id(1)
    blk = SWA_BLOCK
    lane = lax.broadcasted_iota(jnp.int32, (1, LANES), 1)

    @pl.when(jnp.logical_and(pl.program_id(0) == 0, n == 0))
    def _():
        bucket = bucket_ref[...]
        kj = lax.broadcasted_iota(jnp.int32, (2 * blk, blk), 0)
        qi = lax.broadcasted_iota(jnp.int32, (2 * blk, blk), 1)
        band = jnp.logical_or(jnp.logical_and(kj < blk, kj > qi), jnp.logical_and(kj >= blk, kj - blk <= qi))

        def per_head(head, carry):
            acc = jnp.zeros((2 * blk, blk), F32)
            for b in range(REL_BUCKETS):
                acc = jnp.where(bucket == b, rel_ref[b, head] * LOG2E, acc)
            bias_ref[head] = jnp.where(band, acc, NEG)
            return carry
        lax.fori_loop(0, SWA_HEADS, per_head, 0)

    for kvh in range(SWA_KV_HEADS):
        cols = slice(kvh * LANES, (kvh + 1) * LANES)
        heads = range(kvh * SWA_GROUP, (kvh + 1) * SWA_GROUP)
        for g, head in enumerate(heads):
            q_pair = q_ref[:, head // 2 * LANES:(head // 2 + 1) * LANES]
            qs_ref[g * blk:(g + 1) * blk, :] = jnp.where((lane // HEAD_DIM) == head % 2, q_pair, jnp.zeros_like(q_pair))
        keys = jnp.concatenate([kp_ref[:, cols], kc_ref[:, cols]], axis=0)
        vals = jnp.concatenate([vp_ref[:, cols], vc_ref[:, cols]], axis=0)
        s = _dot_nt(keys, qs_ref[...]) + jnp.concatenate([bias_ref[head] for head in heads], axis=1)
        s_prev = jnp.where(n > 0, s[:blk], NEG)
        s_cur = s[blk:]
        sink = jnp.concatenate([jnp.full((1, blk), sink_ref[head] * LOG2E, F32) for head in heads], axis=1)
        m = jnp.maximum(jnp.maximum(s_prev, s_cur).max(axis=0, keepdims=True), sink)
        e = jnp.concatenate([jnp.exp2(s_prev - m).astype(BF16), jnp.exp2(s_cur - m).astype(BF16)], axis=0)
        v_t = jnp.concatenate([vals.astype(F32).T.astype(BF16)[:HEAD_DIM], jnp.ones((ONES_ROWS, 2 * blk), BF16)], axis=0)
        out = _dot(v_t, e)
        out = out[:HEAD_DIM] / (out[HEAD_DIM:HEAD_DIM + 1] + jnp.exp2(sink - m))
        for pair in range(SWA_GROUP // 2):
            head = kvh * SWA_GROUP + 2 * pair
            even = out[:, 2 * pair * blk:(2 * pair + 1) * blk]
            odd = out[:, (2 * pair + 1) * blk:(2 * pair + 2) * blk]
            cols = slice(head // 2 * LANES, (head // 2 + 1) * LANES)
            o_ref[:, cols] = (jnp.concatenate([even, odd], axis=0).T * sg_ref[:, cols].astype(F32)).astype(o_ref.dtype)


def _swa_attention(proj, sinks, rel_bias, bsz, seq):
    nb = seq // SWA_BLOCK
    width = SWA_HEADS * HEAD_DIM
    kvw = SWA_KV_HEADS * LANES
    k_blk = 2 * width // kvw
    cur = lambda b, n: b * nb + n
    prev = lambda b, n: b * nb + jnp.maximum(n - 1, 0)
    smem = pl.BlockSpec(memory_space=pltpu.SMEM)
    return pl.pallas_call(
        _swa_kernel,
        grid=(bsz, nb),
        in_specs=[
            smem, smem,
            pl.BlockSpec((2 * SWA_BLOCK, SWA_BLOCK), lambda b, n: (0, 0)),
            pl.BlockSpec((SWA_BLOCK, width), lambda b, n: (cur(b, n), 0)),
            pl.BlockSpec((SWA_BLOCK, width), lambda b, n: (cur(b, n), 1)),
            pl.BlockSpec((SWA_BLOCK, kvw), lambda b, n: (prev(b, n), k_blk)),
            pl.BlockSpec((SWA_BLOCK, kvw), lambda b, n: (cur(b, n), k_blk)),
            pl.BlockSpec((SWA_BLOCK, kvw), lambda b, n: (prev(b, n), k_blk + 1)),
            pl.BlockSpec((SWA_BLOCK, kvw), lambda b, n: (cur(b, n), k_blk + 1)),
        ],
        out_specs=pl.BlockSpec((SWA_BLOCK, width), lambda b, n: (cur(b, n), 0)),
        out_shape=jax.ShapeDtypeStruct((bsz * seq, width), BF16),
        scratch_shapes=[
            pltpu.VMEM((SWA_HEADS, 2 * SWA_BLOCK, SWA_BLOCK), F32),
            pltpu.VMEM((SWA_GROUP * SWA_BLOCK, LANES), BF16),
        ],
        compiler_params=_params(2),
        name="swa_attention",
    )(sinks.astype(F32), rel_bias.astype(F32), jnp.asarray(_swa_bucket_table()), proj, proj, proj, proj, proj, proj)


def _dup_heads(w, n_heads):
    k = w.shape[0]
    w = w.reshape(k, n_heads, 1, HEAD_DIM)
    return jnp.broadcast_to(w, (k, n_heads, 2, HEAD_DIM)).reshape(k, n_heads * 2 * HEAD_DIM)


def _swa_mixer(x, g, w_in, sinks, rel_bias, bsz, seq):
    width = SWA_HEADS * HEAD_DIM
    kvw = SWA_KV_HEADS * HEAD_DIM
    w_q, w_k, w_v, w_g = (w_in[:, :width], w_in[:, width:width + kvw],
                          w_in[:, width + kvw:width + 2 * kvw], w_in[:, width + 2 * kvw:])
    w = jnp.concatenate([w_q * (LOG2E * HEAD_DIM ** -0.5), w_g, _dup_heads(w_k, SWA_KV_HEADS), _dup_heads(w_v, SWA_KV_HEADS)], axis=1)
    proj = _norm_matmul(x, g, _tile_cols_xla(w, MATMUL_TN), BF16, silu_cols=(width, 2 * width))
    return _swa_attention(proj, sinks, rel_bias, bsz, seq)


HALO = 8


def _conv_kernel(x_ref, g_ref, wb_ref, wc_ref, wu_ref, wg_ref, ck_ref, o_ref, a_ref, z_ref, carry_ref, *, tiles_per_seq):
    i, j = pl.program_id(0), pl.program_id(1)
    tm = a_ref.shape[0]

    @pl.when(j == 0)
    def _():
        def chunk(rows):
            a_ref[rows, :] = _rmsnorm_rows(x_ref[rows, :], g_ref[...]).astype(BF16)
        _for_row_chunks(tm, chunk)

    a = a_ref[...]
    z = _dot(a, wc_ref[...]) * _dot(a, wu_ref[...])
    first = (i % tiles_per_seq) == 0

    @pl.when(first)
    def _():
        z_ref[:HALO, :] = jnp.zeros((HALO, z_ref.shape[1]), F32)

    @pl.when(jnp.logical_not(first))
    def _():
        z_ref[:HALO, :] = carry_ref[j]

    z_ref[HALO:, :] = z
    carry_ref[j] = z[tm - HALO:, :]
    conv = z_ref[HALO - 2:HALO - 2 + tm, :] * ck_ref[0:1, :]
    conv = conv + z_ref[HALO - 1:HALO - 1 + tm, :] * ck_ref[1:2, :]
    conv = conv + z * ck_ref[2:3, :]
    y = _dot(a, wb_ref[...]) * conv
    o_ref[...] = (y * _silu(_dot(a, wg_ref[...]))).astype(o_ref.dtype)


def _conv_mixer(x, g, w_in, layer, conv_kernel, seq, tm=1024, tn=512):
    m, k = x.shape
    width = w_in.shape[2] // 4
    tm, tn = min(tm, seq), min(tn, width)
    nt = width // tn
    w_spec = lambda q: pl.BlockSpec((None, k, tn), lambda i, j: (q * nt + j, 0, 0))
    w_in = _tile_cols(w_in, layer, tn)
    return pl.pallas_call(
        functools.partial(_conv_kernel, tiles_per_seq=seq // tm),
        grid=(m // tm, nt),
        in_specs=[
            pl.BlockSpec((tm, k), lambda i, j: (i, 0)),
            pl.BlockSpec((1, k), lambda i, j: (0, 0)),
            w_spec(0), w_spec(1), w_spec(2), w_spec(3),
            pl.BlockSpec((CONV_TAPS, tn), lambda i, j: (0, j)),
        ],
        out_specs=pl.BlockSpec((tm, tn), lambda i, j: (i, j)),
        out_shape=jax.ShapeDtypeStruct((m, width), BF16),
        scratch_shapes=[
            pltpu.VMEM((tm, k), BF16),
            pltpu.VMEM((HALO + tm, tn), F32),
            pltpu.VMEM((nt, HALO, tn), F32),
        ],
        compiler_params=_params(2, VMEM_LIMIT_BIG),
        name="conv_mixer",
    )(x, g.reshape(1, k), w_in, w_in, w_in, w_in, conv_kernel.astype(F32))


def _ssm_operators(lam_re, lam_im, log_dt, b_re, b_im, c_re, c_im, d_skip):
    n_groups = lam_re.shape[0]
    n_oct = n_groups // GROUPS_PER_TILE
    L, C, N = SSM_CHUNK, SSM_GROUP, SSM_STATE
    dt = jnp.exp(log_dt.astype(F32))[None, :]
    lr, li = lam_re.astype(F32).T, lam_im.astype(F32).T
    mag = jnp.exp(lr * dt)
    ab_re, ab_im = mag * jnp.cos(li * dt), mag * jnp.sin(li * dt)
    den = lr * lr + li * li
    nr = ab_re - 1.0
    coef_re = (nr * lr + ab_im * li) / den
    coef_im = (ab_im * lr - nr * li) / den
    br, bi = b_re.astype(F32).transpose(2, 1, 0), b_im.astype(F32).transpose(2, 1, 0)
    bb_re = coef_re * br - coef_im * bi
    bb_im = coef_re * bi + coef_im * br
    cr, ci = c_re.astype(F32).transpose(1, 2, 0), c_im.astype(F32).transpose(1, 2, 0)
    pw_re, pw_im = [jnp.ones_like(ab_re)], [jnp.zeros_like(ab_im)]
    for _ in range(L):
        pw_re, pw_im = (pw_re + [pw_re[-1] * ab_re - pw_im[-1] * ab_im], pw_im + [pw_re[-1] * ab_im + pw_im[-1] * ab_re])
    p_re, p_im = jnp.stack(pw_re), jnp.stack(pw_im)
    w_re = cr[None] * p_re[:L, None] - ci[None] * p_im[:L, None]
    w_im = cr[None] * p_im[:L, None] + ci[None] * p_re[:L, None]
    lag_k = jnp.sum(w_re[:, None] * bb_re[None, :, None] - w_im[:, None] * bb_im[None, :, None], axis=3)
    q_re, q_im = jnp.stack(pw_re[L - 1::-1]), jnp.stack(pw_im[L - 1::-1])
    bc_re = q_re[:, None] * bb_re[None] - q_im[:, None] * bb_im[None]
    bc_im = q_re[:, None] * bb_im[None] + q_im[:, None] * bb_re[None]
    o_re = cr[None] * p_re[1:, None] - ci[None] * p_im[1:, None]
    o_im = cr[None] * p_im[1:, None] + ci[None] * p_re[1:, None]
    octs = lambda t: t.reshape(t.shape[:-1] + (n_oct, GROUPS_PER_TILE))
    to_lag = lambda t: octs(t).transpose(3, 0, 1, 4, 2).reshape(n_oct, L, C, LANES)
    to_rows = lambda t: octs(t).transpose(3, 0, 1, 4, 2).reshape(n_oct, L * C, GROUPS_PER_TILE * N)
    to_cols = lambda t: octs(t).transpose(3, 4, 2, 0, 1).reshape(n_oct, GROUPS_PER_TILE * N, L * C)
    to_vec = lambda t: octs(t).transpose(1, 2, 0).reshape(n_oct, 1, GROUPS_PER_TILE * N)
    return dict(
        lag=to_lag(lag_k).astype(BF16),
        bc_re=to_rows(bc_re).astype(BF16), bc_im=to_rows(bc_im).astype(BF16),
        oc_re=to_cols(o_re).astype(BF16), oc_im_neg=to_cols(-o_im).astype(BF16),
        al_re=to_vec(p_re[L]), al_im=to_vec(p_im[L]), d=d_skip.astype(F32).reshape(n_oct, 1, LANES),
    )


def _ssm_expanders():
    L, C, N, G8 = SSM_CHUNK, SSM_GROUP, SSM_STATE, GROUPS_PER_TILE
    wide = np.arange(L * LANES)
    w_step, w_group, w_chan = wide // LANES, (wide // C) % G8, wide % C
    small = np.arange(L * C)
    s_step, s_chan = small // C, small % C
    state_group = np.arange(G8 * N) // N
    spread = (w_step[:, None] == s_step[None, :]) & (w_chan[:, None] == s_chan[None, :])
    lane = np.arange(LANES)
    as_bf16 = lambda a: jnp.asarray(a.astype(np.float32), dtype=BF16)
    return dict(
        spread_cols=as_bf16(spread.T),
        keep_cols=as_bf16(state_group[:, None] == w_group[None, :]),
        spread_lag=as_bf16((lane % C)[:, None] == np.arange(C)[None, :]),
        keep_lag=jnp.asarray(((lane // C)[:, None] == (lane // C)[None, :]).astype(np.float32)),
    )


def _ssm_kernel(u_ref, lag_ref, bxr_ref, bxi_ref, cxr_ref, cxi_ref, alr_ref, ali_ref, d_ref,
                sc_ref, kc_ref, sl_ref, kl_ref, o_ref,
                ub_ref, bcr_ref, bci_ref, ocr_ref, oci_ref, zr_ref, zi_ref, hr_ref, hi_ref, *, bsz):
    L = SSM_CHUNK
    rows = u_ref.shape[0] // L
    chunks = rows // bsz
    tile = 2 * LANES
    step_rows = lambda r: pl.ds(r, rows, stride=L)
    for r in range(L):
        ub_ref[:, r * LANES:(r + 1) * LANES] = u_ref[step_rows(r), :].astype(BF16)
    state_group = lax.broadcasted_iota(jnp.int32, (1, bxr_ref.shape[1]), 1) // SSM_STATE
    for compact_ref, full_ref in ((bxr_ref, bcr_ref), (bxi_ref, bci_ref)):
        for r in range(L):
            blk = compact_ref[r * SSM_GROUP:(r + 1) * SSM_GROUP, :]
            for grp in range(GROUPS_PER_TILE):
                first = r * LANES + grp * SSM_GROUP
                full_ref[first:first + SSM_GROUP, :] = jnp.where(state_group == grp, blk, jnp.zeros_like(blk))
    zr_ref[...] = _dot(ub_ref[...], bcr_ref[...])
    zi_ref[...] = _dot(ub_ref[...], bci_ref[...])
    a_re, a_im = alr_ref[...], ali_ref[...]

    def step(k, carry):
        new = []
        for b in range(bsz):
            h_re, h_im = carry[2 * b], carry[2 * b + 1]
            row = pl.ds(b * chunks + k, 1)
            hr_ref[row, :] = h_re
            hi_ref[row, :] = h_im
            new.append(a_re * h_re - a_im * h_im + zr_ref[row, :])
            new.append(a_re * h_im + a_im * h_re + zi_ref[row, :])
        return tuple(new)

    zero = jnp.zeros((1, a_re.shape[1]), F32)
    lax.fori_loop(0, chunks, step, (zero,) * (2 * bsz))

    ocr_ref[...] = (_dot(cxr_ref[...], sc_ref[...]) * kc_ref[...]).astype(BF16)
    oci_ref[...] = (_dot(cxi_ref[...], sc_ref[...]) * kc_ref[...]).astype(BF16)
    lag = [(_dot(sl_ref[...], lag_ref[d]) * kl_ref[...]).astype(BF16) for d in range(L)]
    lag_tile = []
    for dd in range(L // 2):
        below = lag[2 * dd - 1] if dd > 0 else jnp.zeros((LANES, LANES), BF16)
        lag_tile.append(jnp.concatenate([jnp.concatenate([lag[2 * dd], lag[2 * dd + 1]], axis=1),
                                         jnp.concatenate([below, lag[2 * dd]], axis=1)], axis=0))
    hb_re, hb_im = hr_ref[...].astype(BF16), hi_ref[...].astype(BF16)
    for t2 in range(L // 2):
        cols = slice(t2 * tile, (t2 + 1) * tile)
        y = _dot(hb_re, ocr_ref[:, cols]) + _dot(hb_im, oci_ref[:, cols])
        for r2 in range(t2 + 1):
            y = y + _dot(ub_ref[:, r2 * tile:(r2 + 1) * tile], lag_tile[t2 - r2])
        for half in range(2):
            t = 2 * t2 + half
            o_ref[step_rows(t), :] = jax.nn.gelu(y[:, half * LANES:(half + 1) * LANES] + d_ref[...] * u_ref[step_rows(t), :])


def _ssm_core(proj, ops, bsz, seq):
    m = proj.shape[0]
    width = proj.shape[1] // 2
    L, C = SSM_CHUNK, SSM_GROUP
    n_oct = width // LANES
    rows = m // L
    ow = L * LANES
    n_state = GROUPS_PER_TILE * SSM_STATE
    ex = _ssm_expanders()
    per_oct = lambda *shape: pl.BlockSpec((None,) + shape, lambda p: (p,) + (0,) * len(shape))
    const = lambda a: pl.BlockSpec(a.shape, lambda p: (0,) * a.ndim, pipeline_mode=pl.Buffered(1))
    consts = [ex['spread_cols'], ex['keep_cols'], ex['spread_lag'], ex['keep_lag']]
    return pl.pallas_call(
        functools.partial(_ssm_kernel, bsz=bsz),
        grid=(n_oct,),
        in_specs=[
            pl.BlockSpec((m, LANES), lambda p: (0, p)),
            per_oct(L, C, LANES), per_oct(L * C, n_state), per_oct(L * C, n_state),
            per_oct(n_state, L * C), per_oct(n_state, L * C), per_oct(1, n_state), per_oct(1, n_state), per_oct(1, LANES),
        ] + [const(a) for a in consts],
        out_specs=pl.BlockSpec((m, LANES), lambda p: (0, p)),
        out_shape=jax.ShapeDtypeStruct((m, width), F32),
        scratch_shapes=[pltpu.VMEM((rows, ow), BF16)] + [pltpu.VMEM((ow, n_state), BF16)] * 2
        + [pltpu.VMEM((n_state, ow), BF16)] * 2 + [pltpu.VMEM((rows, n_state), F32)] * 4,
        compiler_params=_params(1),
        name="ssm_core",
    )(proj, ops['lag'], ops['bc_re'], ops['bc_im'], ops['oc_re'], ops['oc_im_neg'], ops['al_re'], ops['al_im'], ops['d'], *consts)


def _glu_kernel(y_ref, wa_ref, wb_ref, ba_ref, bb_ref, gate_ref, o_ref, yb_ref):
    @pl.when(pl.program_id(1) == 0)
    def _():
        def chunk(rows):
            yb_ref[rows, :] = y_ref[rows, :].astype(BF16)
        _for_row_chunks(yb_ref.shape[0], chunk)

    y = yb_ref[...]
    ga = _dot(y, wa_ref[...]) + ba_ref[...]
    gb = _dot(y, wb_ref[...]) + bb_ref[...]
    o_ref[...] = ((ga * jax.nn.sigmoid(gb)) * _silu(gate_ref[...])).astype(o_ref.dtype)


def _glu(y, w_glu, layer, b_glu, proj, tm=1024, tn=512):
    m, k = y.shape
    width = w_glu.shape[2] // 2
    tm, tn = min(tm, m), min(tn, width)
    nt = width // tn
    w_tiles = _tile_cols(w_glu, layer, tn)
    return pl.pallas_call(
        _glu_kernel,
        grid=(m // tm, nt),
        in_specs=[
            pl.BlockSpec((tm, k), lambda i, j: (i, 0)),
            pl.BlockSpec((None, k, tn), lambda i, j: (j, 0, 0)),
            pl.BlockSpec((None, k, tn), lambda i, j: (nt + j, 0, 0)),
            pl.BlockSpec((1, tn), lambda i, j: (0, j)),
            pl.BlockSpec((1, tn), lambda i, j: (0, nt + j)),
            pl.BlockSpec((tm, tn), lambda i, j: (i, nt + j)),
        ],
        out_specs=pl.BlockSpec((tm, tn), lambda i, j: (i, j)),
        out_shape=jax.ShapeDtypeStruct((m, width), BF16),
        scratch_shapes=[pltpu.VMEM((tm, k), BF16)],
        compiler_params=_params(2),
        name="ssm_glu",
    )(y, w_tiles, w_tiles, b_glu, b_glu, proj)


def _ssm_mixer(x, g, w_in, layer, lam_re, lam_im, log_dt, b_re, b_im, c_re, c_im, d_skip, w_glu, b_glu, bsz, seq):
    proj = _norm_matmul(x, g, _tile_cols(w_in, layer, MATMUL_TN), F32)
    ops = _ssm_operators(lam_re, lam_im, log_dt, b_re, b_im, c_re, c_im, d_skip)
    y = _ssm_core(proj, ops, bsz, seq)
    return _glu(y, w_glu, layer, b_glu.astype(F32).reshape(1, -1), proj)


CUM_BLOCK = 128


def _split3(x):
    x1 = x.astype(BF16)
    r1 = x - x1.astype(F32)
    x2 = r1.astype(BF16)
    x3 = (r1 - x2.astype(F32)).astype(BF16)
    return x1, x2, x3


def _forget_cumsum_kernel(z_ref, b_ref, o_ref):
    n_blocks = z_ref.shape[0] // CUM_BLOCK
    ri = lax.broadcasted_iota(jnp.int32, (CUM_BLOCK, CUM_BLOCK), 0)
    ci = lax.broadcasted_iota(jnp.int32, (CUM_BLOCK, CUM_BLOCK), 1)
    tri = jnp.where(ci <= ri, 1.0, 0.0).astype(BF16)

    def body(i, carry):
        rows = pl.ds(pl.multiple_of(i * CUM_BLOCK, CUM_BLOCK), CUM_BLOCK)
        z = z_ref[rows, :] + b_ref[...]
        log_f = jnp.minimum(z, 0.0) - jnp.log1p(jnp.exp(-jnp.abs(z)))
        x1, x2, x3 = _split3(log_f)
        c = _dot(tri, x1) + _dot(tri, x2) + _dot(tri, x3) + carry
        o_ref[rows, :] = c
        return c[CUM_BLOCK - 1:, :]

    lax.fori_loop(0, n_blocks, body, jnp.zeros((1, z_ref.shape[1]), F32))


def _forget_cumsum(z, b, bsz, seq):
    lanes = z.shape[1]
    return pl.pallas_call(
        _forget_cumsum_kernel,
        grid=(bsz,),
        in_specs=[pl.BlockSpec((seq, lanes), lambda i: (i, 0)), pl.BlockSpec((1, lanes), lambda i: (0, 0))],
        out_specs=pl.BlockSpec((seq, lanes), lambda i: (i, 0)),
        out_shape=jax.ShapeDtypeStruct(z.shape, F32),
        compiler_params=_params(1),
        name="forget_cumsum",
    )(z, b)


FOX_STEP_HEADS = 8


def _fox_kernel(q_ref, k_ref, v_ref, sg_ref, cq_ref, ck_ref, o_ref, ka_ref, vt_ref, *head_refs, tk):
    nh = FOX_STEP_HEADS
    hg, qi = pl.program_id(1), pl.program_id(2)
    tq = q_ref.shape[0]
    qa_ref, s_ref, p_ref, acc_ref, pv_ref, m_ref = (head_refs[i::6] for i in range(6))
    lane = lax.broadcasted_iota(jnp.int32, (1, LANES), 1)
    free = (HEAD_DIM, 0)
    n_pieces = 3
    pair_lanes = lambda h: slice(h // 2 * LANES, (h // 2 + 1) * LANES)

    def bias_lanes(c, h, c_offset, one_offset):
        rr = lax.broadcasted_iota(jnp.int32, (n_pieces * LANES, LANES), 0)
        cc = lax.broadcasted_iota(jnp.int32, (n_pieces * LANES, LANES), 1)
        place = jnp.logical_and(rr % LANES == nh * hg + h, cc == free[h % 2] + c_offset + rr // LANES)
        placed = _dot(jnp.concatenate(_split3(c), axis=1), jnp.where(place, 1.0, 0.0).astype(BF16))
        first = free[h % 2] + one_offset
        return (placed + jnp.where(jnp.logical_and(lane >= first, lane < first + n_pieces), 1.0, 0.0)).astype(BF16)

    @pl.when(qi == 0)
    def _():
        def chunk(j, carry):
            rows = pl.ds(pl.multiple_of(j * tk, tk), tk)
            neg_c = ck_ref[rows, :] * -LOG2E
            for h in range(nh):
                keys = k_ref[rows, pair_lanes(h)]
                ka_ref[h, j] = jnp.where((lane // HEAD_DIM) == h % 2, keys, bias_lanes(neg_c, h, 0, n_pieces))
            for pair in range(nh // 2):
                v_t = v_ref[rows, pair_lanes(2 * pair)].astype(F32).T.astype(BF16)
                for hh in range(2):
                    vt_ref[j, 2 * pair + hh, :HEAD_DIM, :] = v_t[hh * HEAD_DIM:(hh + 1) * HEAD_DIM]
                    vt_ref[j, 2 * pair + hh, HEAD_DIM:, :] = jnp.ones((ONES_ROWS, tk), BF16)
            return carry
        lax.fori_loop(0, k_ref.shape[0] // tk, chunk, 0)

    cq = cq_ref[...] * LOG2E
    for h in range(nh):
        qa = jnp.where((lane // HEAD_DIM) == h % 2, q_ref[:, pair_lanes(h)], bias_lanes(cq, h, n_pieces, 0))
        qa_ref[h][...] = qa.astype(F32).T.astype(BF16)

    def scores(h, j):
        s_ref[h][...] = _dot(ka_ref[h, j], qa_ref[h][...])

    rows8 = 8
    rows16 = 16

    def absorb(h, j, diagonal):
        def strip(r, n):
            blk = s_ref[h][r:r + n, :]
            if diagonal:
                kpos = lax.broadcasted_iota(jnp.int32, (n, tq), 0)
                qpos = lax.broadcasted_iota(jnp.int32, (n, tq), 1)
                blk = jnp.where(kpos + r <= qpos, blk, NEG)
            return blk
        top = strip(0, rows8)
        for r in range(rows8, tk, rows8):
            top = jnp.maximum(top, strip(r, rows8))
        m = m_ref[h][...]
        m_new = jnp.maximum(m, top.max(axis=0, keepdims=True))
        m_ref[h][...] = m_new
        alpha = jnp.exp2(m - m_new)
        m_rows = jnp.broadcast_to(m_new, (rows16, tq))
        for r in range(0, tk, rows16):
            p_ref[h][r:r + rows16, :] = jnp.exp2(strip(r, rows16) - m_rows).astype(BF16)
        acc_ref[h][...] = alpha * (acc_ref[h][...] + pv_ref[h][...])

    def values(h, j):
        pv_ref[h][...] = _dot(vt_ref[j, h], p_ref[h][...])

    def step(j, diagonal):
        values(nh - 1, jnp.maximum(j - 1, 0))
        scores(1, j)
        for h in range(nh):
            absorb(h, j, diagonal)
            if h < nh - 1:
                values(h, j)
            if h + 2 < nh:
                scores(h + 2, j)
            elif h + 2 == nh and not diagonal:
                scores(0, j + 1)

    for h in range(nh):
        acc_ref[h][...] = jnp.zeros(acc_ref[h].shape, F32)
        pv_ref[h][...] = jnp.zeros(pv_ref[h].shape, F32)
        m_ref[h][...] = jnp.full(m_ref[h].shape, NEG, F32)
    p_ref[nh - 1][...] = jnp.zeros(p_ref[nh - 1].shape, BF16)
    scores(0, 0)

    def two_blocks(i, carry):
        step(2 * i, False)
        step(2 * i + 1, False)
        return carry
    lax.fori_loop(0, qi // 2, two_blocks, 0)

    @pl.when(qi % 2 == 0)
    def _():
        step(qi, True)

    @pl.when(qi % 2 == 1)
    def _():
        step(qi - 1, False)
        step(qi, True)

    values(nh - 1, qi)
    outs = []
    for h in range(nh):
        total = acc_ref[h][...] + pv_ref[h][...]
        outs.append(total[:HEAD_DIM] / total[HEAD_DIM:HEAD_DIM + 1])
    out = jnp.concatenate(outs, axis=0)
    o_ref[...] = (out.T * sg_ref[...].astype(F32)).astype(o_ref.dtype)


def _fox_attention(proj, csum, bsz, seq, t=512):
    width = FOX_HEADS * HEAD_DIM
    nh = FOX_STEP_HEADS
    gw = nh * HEAD_DIM
    n_groups = width // gw
    t = min(t, seq)
    nq = seq // t
    return pl.pallas_call(
        functools.partial(_fox_kernel, tk=t),
        grid=(bsz, n_groups, nq),
        in_specs=[
            pl.BlockSpec((t, gw), lambda b, h, i: (b * nq + i, h)),
            pl.BlockSpec((seq, gw), lambda b, h, i: (b, n_groups + h), pipeline_mode=pl.Buffered(1)),
            pl.BlockSpec((seq, gw), lambda b, h, i: (b, 2 * n_groups + h), pipeline_mode=pl.Buffered(1)),
            pl.BlockSpec((t, gw), lambda b, h, i: (b * nq + i, 3 * n_groups + h)),
            pl.BlockSpec((t, LANES), lambda b, h, i: (b * nq + i, 0)),
            pl.BlockSpec((seq, LANES), lambda b, h, i: (b, 0), pipeline_mode=pl.Buffered(1)),
        ],
        out_specs=pl.BlockSpec((t, gw), lambda b, h, i: (b * nq + i, h)),
        out_shape=jax.ShapeDtypeStruct((bsz * seq, width), BF16),
        scratch_shapes=[
            pltpu.VMEM((nh, nq, t, LANES), BF16), pltpu.VMEM((nq, nh, HEAD_DIM + ONES_ROWS, t), BF16),
        ] + nh * [pltpu.VMEM((LANES, t), BF16), pltpu.VMEM((t, t), F32), pltpu.VMEM((t, t), BF16),
                  pltpu.VMEM((HEAD_DIM + ONES_ROWS, t), F32), pltpu.VMEM((HEAD_DIM + ONES_ROWS, t), F32),
                  pltpu.VMEM((1, t), F32)],
        compiler_params=_params(3, VMEM_LIMIT_BIG),
        name="fox_attention",
    )(proj, proj, proj, proj, csum, csum)


def _fox_mixer(x, g, w_in, layer, w_fg, b_fg, bsz, seq, t=512):
    width = FOX_HEADS * HEAD_DIM
    col_scale = jnp.where(jnp.arange(4 * width) < width, LOG2E * HEAD_DIM ** -0.5, 1.0).astype(F32)
    proj = _norm_matmul(x, g, _tile_cols(w_in, layer, MATMUL_TN, col_scale), BF16, silu_cols=(3 * width, 4 * width))
    pad = LANES - FOX_HEADS
    z = _norm_matmul(x, g, _tile_cols_xla(jnp.pad(w_fg, ((0, 0), (0, pad))), LANES), F32)
    csum = _forget_cumsum(z, jnp.pad(b_fg.astype(F32), (0, pad)).reshape(1, LANES), bsz, seq)
    return _fox_attention(proj, csum, bsz, seq, t)


def kernel(x, p, norm_g, final_g, rel_bias, swa_w_in, swa_w_out, swa_sinks, conv_w_in, conv_kernel, conv_w_out, ssm_w_in, ssm_lam_re, ssm_lam_im, ssm_log_dt, ssm_b_re, ssm_b_im, ssm_c_re, ssm_c_im, ssm_d, ssm_w_glu, ssm_b_glu, ssm_w_out, fox_w_in, fox_w_fg, fox_b_fg, fox_w_out, ple_proj, ple_norm, ple_gate):
    bsz, seq, d_model = x.shape
    depth = p.shape[0]
    h = x.astype(F32).reshape(bsz * seq, d_model)
    for i in range(depth):
        mixer, j = i % N_MIXERS, i // N_MIXERS
        if mixer == 0:
            a, w_out = _swa_mixer(h, norm_g[i], swa_w_in[j], swa_sinks[j], rel_bias, bsz, seq), swa_w_out[j]
        elif mixer == 1:
            a, w_out = _conv_mixer(h, norm_g[i], conv_w_in, j, conv_kernel[j], seq), conv_w_out[j]
        elif mixer == 2:
            a = _ssm_mixer(h, norm_g[i], ssm_w_in, j, ssm_lam_re[j], ssm_lam_im[j], ssm_log_dt[j], ssm_b_re[j], ssm_b_im[j],
                           ssm_c_re[j], ssm_c_im[j], ssm_d[j], ssm_w_glu, ssm_b_glu[j], bsz, seq)
            w_out = ssm_w_out[j]
        else:
            a, w_out = _fox_mixer(h, norm_g[i], fox_w_in, j, fox_w_fg[j], fox_b_fg[j], bsz, seq), fox_w_out[j]
        h = _out_ple(a, w_out, h, ple_norm[i], ple_gate, p.reshape(depth, bsz * seq, -1), ple_proj, i,
                     final_g=final_g if i == depth - 1 else None)
    return h.reshape(bsz, seq, d_model).astype(x.dtype)
```

```python
import functools
import math

import numpy as np
import jax
import jax.numpy as jnp
from jax import lax
from jax.experimental import pallas as pl
from jax.experimental.pallas import tpu as pltpu

F32 = jnp.float32
BF16 = jnp.bfloat16

EPS = 1e-6
N_MIXERS = 4
PLE_DIM = 256

SWA_HEADS = 32
SWA_KV_HEADS = 4
SWA_GROUP = SWA_HEADS // SWA_KV_HEADS
HEAD_DIM = 64
SWA_BLOCK = 128
WINDOW = 128
REL_BUCKETS = 32
REL_MAX_DIST = 128

CONV_TAPS = 3

SSM_GROUP = 16
SSM_STATE = 64
SSM_CHUNK = 16

FOX_HEADS = 32

LANES = 128
ONES_ROWS = 16
GROUPS_PER_TILE = LANES // SSM_GROUP
VMEM_LIMIT = 48 * 1024 * 1024
VMEM_LIMIT_BIG = 60 * 1024 * 1024

NEG = float(jnp.finfo(jnp.float32).min)
LOG2E = math.log2(math.e)


def _params(n_axes, vmem_limit=VMEM_LIMIT):
    return pltpu.CompilerParams(dimension_semantics=("arbitrary",) * n_axes, vmem_limit_bytes=vmem_limit)


def _dot(a, b):
    return jnp.dot(a, b, preferred_element_type=F32)


def _dot_nt(a, b):
    return lax.dot_general(a, b, (((1,), (1,)), ((), ())), preferred_element_type=F32)


def _rmsnorm_rows(x, g):
    return x * lax.rsqrt(jnp.mean(x * x, axis=-1, keepdims=True) + EPS) * g


def _silu(x):
    return x * jax.nn.sigmoid(x)


ROW_CHUNK = 64


def _for_row_chunks(n_rows, fn):
    def body(c, carry):
        fn(pl.ds(pl.multiple_of(c * ROW_CHUNK, ROW_CHUNK), ROW_CHUNK))
        return carry
    lax.fori_loop(0, n_rows // ROW_CHUNK, body, 0)


def _tile_cols_xla(w, tn):
    k, n = w.shape
    return w.astype(BF16).reshape(k, n // tn, tn).transpose(1, 0, 2)


def _cast_tiles_kernel(w_ref, o_ref):
    def chunk(rows):
        o_ref[rows, :] = w_ref[rows, :].astype(BF16)
    _for_row_chunks(o_ref.shape[0], chunk)


def _scale_cast_tiles_kernel(w_ref, s_ref, o_ref):
    def chunk(rows):
        o_ref[rows, :] = (w_ref[rows, :] * s_ref[...]).astype(BF16)
    _for_row_chunks(o_ref.shape[0], chunk)


def _tile_cols(w, layer, tn, col_scale=None):
    _, k, n = w.shape
    tn = min(tn, n)
    w_spec = pl.BlockSpec((None, k, tn), lambda j: (layer, 0, j))
    common = dict(
        grid=(n // tn,),
        out_specs=pl.BlockSpec((None, k, tn), lambda j: (j, 0, 0)),
        out_shape=jax.ShapeDtypeStruct((n // tn, k, tn), BF16),
        compiler_params=_params(1),
        name="weight_tiles",
    )
    if col_scale is None:
        return pl.pallas_call(_cast_tiles_kernel, in_specs=[w_spec], **common)(w)
    s_spec = pl.BlockSpec((1, tn), lambda j: (0, j))
    return pl.pallas_call(_scale_cast_tiles_kernel, in_specs=[w_spec, s_spec], **common)(w, col_scale.reshape(1, n))


def _norm_matmul_kernel(x_ref, g_ref, w_ref, o_ref, a_ref, *, silu_tiles):
    j = pl.program_id(1)

    @pl.when(j == 0)
    def _():
        def chunk(rows):
            a_ref[rows, :] = _rmsnorm_rows(x_ref[rows, :], g_ref[...]).astype(BF16)
        _for_row_chunks(a_ref.shape[0], chunk)

    acc = _dot(a_ref[...], w_ref[...])
    if silu_tiles is None:
        o_ref[...] = acc.astype(o_ref.dtype)
    else:
        gated = jnp.logical_and(j >= silu_tiles[0], j < silu_tiles[1])

        @pl.when(gated)
        def _():
            o_ref[...] = _silu(acc).astype(o_ref.dtype)

        @pl.when(jnp.logical_not(gated))
        def _():
            o_ref[...] = acc.astype(o_ref.dtype)


MATMUL_TN = 1024


def _norm_matmul(x, g, w_tiles, out_dtype, silu_cols=None, tm=1024):
    m, k = x.shape
    tn = w_tiles.shape[2]
    n = w_tiles.shape[0] * tn
    tm = min(tm, m)
    silu_tiles = None if silu_cols is None else (silu_cols[0] // tn, silu_cols[1] // tn)
    return pl.pallas_call(
        functools.partial(_norm_matmul_kernel, silu_tiles=silu_tiles),
        grid=(m // tm, n // tn),
        in_specs=[
            pl.BlockSpec((tm, k), lambda i, j: (i, 0)),
            pl.BlockSpec((1, k), lambda i, j: (0, 0)),
            pl.BlockSpec((None, k, tn), lambda i, j: (j, 0, 0)),
        ],
        out_specs=pl.BlockSpec((tm, tn), lambda i, j: (i, j)),
        out_shape=jax.ShapeDtypeStruct((m, n), out_dtype),
        scratch_shapes=[pltpu.VMEM((tm, k), BF16)],
        compiler_params=_params(2),
        name="norm_matmul",
    )(x, g.reshape(1, k), w_tiles)


def _out_ple_kernel(a_ref, wo_ref, x_ref, g_ref, wg_ref, p_ref, wp_ref, gf_ref, o_ref, x1_ref, hn_ref, ss_ref, *, nt, final):
    j = pl.program_id(1)
    tn = x_ref.shape[1]
    n = nt * tn

    @pl.when(j == 0)
    def _():
        ss_ref[...] = jnp.zeros(ss_ref.shape, F32)

    @pl.when(j < nt)
    def _():
        x1 = x_ref[...] + _dot(a_ref[...], wo_ref[j])
        x1_ref[j] = x1
        hn_ref[j] = (x1 * g_ref[j]).astype(BF16)
        ss_ref[0] += jnp.sum(x1 * x1, axis=-1, keepdims=True)

    @pl.when(jnp.logical_and(j >= nt, j < 2 * nt))
    def _():
        t = j - nt
        emb = _dot(p_ref[...].astype(BF16), wp_ref[t])
        acc = _dot(hn_ref[0], wg_ref[t, :tn, :])
        for kt in range(1, nt):
            acc = acc + _dot(hn_ref[kt], wg_ref[t, kt * tn:(kt + 1) * tn, :])
        x2 = x1_ref[t] + emb * jax.nn.sigmoid(acc * lax.rsqrt(ss_ref[0] / n + EPS))
        if final:
            x1_ref[t] = x2
            ss_ref[1] += jnp.sum(x2 * x2, axis=-1, keepdims=True)
        else:
            o_ref[...] = x2

    if final:
        @pl.when(j >= 2 * nt)
        def _():
            t = j - 2 * nt
            o_ref[...] = x1_ref[t] * lax.rsqrt(ss_ref[1] / n + EPS) * gf_ref[t]


def _out_ple(a, w_out, x, g, w_gate, p, w_proj, layer, final_g=None, tm=1024, tn=512):
    m, k = a.shape
    n = w_out.shape[1]
    pd = p.shape[2]
    final = final_g is not None
    tm, tn = min(tm, m), min(tn, n)
    nt = n // tn
    passes = 3 if final else 2
    out_tile = lambda j: jnp.maximum(j - (passes - 1) * nt, 0)
    resident = lambda rows: pl.BlockSpec((nt, rows, tn), lambda i, j: (0, 0, 0), pipeline_mode=pl.Buffered(1))
    row_vec = pl.BlockSpec((nt, 1, tn), lambda i, j: (0, 0, 0))
    return pl.pallas_call(
        functools.partial(_out_ple_kernel, nt=nt, final=final),
        grid=(m // tm, passes * nt),
        in_specs=[
            pl.BlockSpec((tm, k), lambda i, j: (i, 0)),
            resident(k),
            pl.BlockSpec((tm, tn), lambda i, j: (i, jnp.minimum(j, nt - 1))),
            row_vec,
            resident(n),
            pl.BlockSpec((None, tm, pd), lambda i, j: (layer, i, 0)),
            resident(pd),
            row_vec,
        ],
        out_specs=pl.BlockSpec((tm, tn), lambda i, j: (i, out_tile(j))),
        out_shape=jax.ShapeDtypeStruct((m, n), F32),
        scratch_shapes=[pltpu.VMEM((nt, tm, tn), F32), pltpu.VMEM((nt, tm, tn), BF16), pltpu.VMEM((2, tm, 1), F32)],
        compiler_params=_params(2, VMEM_LIMIT_BIG),
        name="out_ple_final" if final else "out_ple",
    )(a, _tile_cols(w_out[None], 0, tn), x, g.reshape(nt, 1, tn), _tile_cols(w_gate, layer, tn), p,
      _tile_cols(w_proj, layer, tn), (final_g if final else g).reshape(nt, 1, tn))


def _t5_bucket(dist):
    max_exact = REL_BUCKETS // 2
    d = np.maximum(dist, 1).astype(np.float32)
    large = max_exact + (np.log(d / max_exact) / np.log(REL_MAX_DIST / max_exact) * (REL_BUCKETS - max_exact)).astype(np.int32)
    large = np.minimum(large, REL_BUCKETS - 1)
    return np.where(dist < max_exact, dist, large).astype(np.int32)


def _swa_bucket_table():
    qi = np.arange(SWA_BLOCK)[None, :]
    kj = np.arange(2 * SWA_BLOCK)[:, None]
    return _t5_bucket(np.clip(qi + SWA_BLOCK - kj, 0, None))


def _swa_kernel(sink_ref, rel_ref, bucket_ref, q_ref, sg_ref, kp_ref, kc_ref, vp_ref, vc_ref, o_ref, bias_ref, qs_ref):
    n = pl.program_id(1)
    blk = SWA_BLOCK
    lane = lax.broadcasted_iota(jnp.int32, (1, LANES), 1)

    @pl.when(jnp.logical_and(pl.program_id(0) == 0, n == 0))
    def _():
        bucket = bucket_ref[...]
        kj = lax.broadcasted_iota(jnp.int32, (2 * blk, blk), 0)
        qi = lax.broadcasted_iota(jnp.int32, (2 * blk, blk), 1)
        band = jnp.logical_or(jnp.logical_and(kj < blk, kj > qi), jnp.logical_and(kj >= blk, kj - blk <= qi))

        def per_head(head, carry):
            acc = jnp.zeros((2 * blk, blk), F32)
            for b in range(REL_BUCKETS):
                acc = jnp.where(bucket == b, rel_ref[b, head] * LOG2E, acc)
            bias_ref[head] = jnp.where(band, acc, NEG)
            return carry
        lax.fori_loop(0, SWA_HEADS, per_head, 0)

    for kvh in range(SWA_KV_HEADS):
        cols = slice(kvh * LANES, (kvh + 1) * LANES)
        heads = range(kvh * SWA_GROUP, (kvh + 1) * SWA_GROUP)
        for g, head in enumerate(heads):
            q_pair = q_ref[:, head // 2 * LANES:(head // 2 + 1) * LANES]
            qs_ref[g * blk:(g + 1) * blk, :] = jnp.where((lane // HEAD_DIM) == head % 2, q_pair, jnp.zeros_like(q_pair))
        keys = jnp.concatenate([kp_ref[:, cols], kc_ref[:, cols]], axis=0)
        vals = jnp.concatenate([vp_ref[:, cols], vc_ref[:, cols]], axis=0)
        s = _dot_nt(keys, qs_ref[...]) + jnp.concatenate([bias_ref[head] for head in heads], axis=1)
        s_prev = jnp.where(n > 0, s[:blk], NEG)
        s_cur = s[blk:]
        sink = jnp.concatenate([jnp.full((1, blk), sink_ref[head] * LOG2E, F32) for head in heads], axis=1)
        m = jnp.maximum(jnp.maximum(s_prev, s_cur).max(axis=0, keepdims=True), sink)
        e = jnp.concatenate([jnp.exp2(s_prev - m).astype(BF16), jnp.exp2(s_cur - m).astype(BF16)], axis=0)
        v_t = jnp.concatenate([vals.astype(F32).T.astype(BF16)[:HEAD_DIM], jnp.ones((ONES_ROWS, 2 * blk), BF16)], axis=0)
        out = _dot(v_t, e)
        out = out[:HEAD_DIM] / (out[HEAD_DIM:HEAD_DIM + 1] + jnp.exp2(sink - m))
        for pair in range(SWA_GROUP // 2):
            head = kvh * SWA_GROUP + 2 * pair
            even = out[:, 2 * pair * blk:(2 * pair + 1) * blk]
            odd = out[:, (2 * pair + 1) * blk:(2 * pair + 2) * blk]
            cols = slice(head // 2 * LANES, (head // 2 + 1) * LANES)
            o_ref[:, cols] = (jnp.concatenate([even, odd], axis=0).T * sg_ref[:, cols].astype(F32)).astype(o_ref.dtype)


def _swa_attention(proj, sinks, rel_bias, bsz, seq):
    nb = seq // SWA_BLOCK
    width = SWA_HEADS * HEAD_DIM
    kvw = SWA_KV_HEADS * LANES
    k_blk = 2 * width // kvw
    cur = lambda b, n: b * nb + n
    prev = lambda b, n: b * nb + jnp.maximum(n - 1, 0)
    smem = pl.BlockSpec(memory_space=pltpu.SMEM)
    return pl.pallas_call(
        _swa_kernel,
        grid=(bsz, nb),
        in_specs=[
            smem, smem,
            pl.BlockSpec((2 * SWA_BLOCK, SWA_BLOCK), lambda b, n: (0, 0)),
            pl.BlockSpec((SWA_BLOCK, width), lambda b, n: (cur(b, n), 0)),
            pl.BlockSpec((SWA_BLOCK, width), lambda b, n: (cur(b, n), 1)),
            pl.BlockSpec((SWA_BLOCK, kvw), lambda b, n: (prev(b, n), k_blk)),
            pl.BlockSpec((SWA_BLOCK, kvw), lambda b, n: (cur(b, n), k_blk)),
            pl.BlockSpec((SWA_BLOCK, kvw), lambda b, n: (prev(b, n), k_blk + 1)),
            pl.BlockSpec((SWA_BLOCK, kvw), lambda b, n: (cur(b, n), k_blk + 1)),
        ],
        out_specs=pl.BlockSpec((SWA_BLOCK, width), lambda b, n: (cur(b, n), 0)),
        out_shape=jax.ShapeDtypeStruct((bsz * seq, width), BF16),
        scratch_shapes=[
            pltpu.VMEM((SWA_HEADS, 2 * SWA_BLOCK, SWA_BLOCK), F32),
            pltpu.VMEM((SWA_GROUP * SWA_BLOCK, LANES), BF16),
        ],
        compiler_params=_params(2),
        name="swa_attention",
    )(sinks.astype(F32), rel_bias.astype(F32), jnp.asarray(_swa_bucket_table()), proj, proj, proj, proj, proj, proj)


SWA_TN = 512


def _swa_weight_tiles_kernel(w_ref, o_ref, *, q_tiles, gate_tiles, q_scale):
    j = pl.program_id(0)
    kvw = SWA_KV_HEADS * HEAD_DIM

    @pl.when(j < q_tiles)
    def _():
        def chunk(rows):
            o_ref[rows, :] = (w_ref[rows, :] * q_scale).astype(BF16)
        _for_row_chunks(o_ref.shape[0], chunk)

    @pl.when(jnp.logical_and(j >= q_tiles, j < q_tiles + gate_tiles))
    def _():
        def chunk(rows):
            o_ref[rows, :] = w_ref[rows, :].astype(BF16)
        _for_row_chunks(o_ref.shape[0], chunk)

    for part in range(2):
        @pl.when(j == q_tiles + gate_tiles + part)
        def _():
            def chunk(rows):
                w = w_ref[rows, part * kvw:(part + 1) * kvw].astype(BF16)
                heads = [w[:, h * HEAD_DIM:(h + 1) * HEAD_DIM] for h in range(SWA_KV_HEADS)]
                o_ref[rows, :] = jnp.concatenate([piece for head in heads for piece in (head, head)], axis=1)
            _for_row_chunks(o_ref.shape[0], chunk)


def _swa_weight_tiles(w_in, layer, q_scale):
    _, k, n = w_in.shape
    tn = SWA_TN
    width = SWA_HEADS * HEAD_DIM
    q_tiles = gate_tiles = width // tn
    kv_block = width // tn
    n_out = q_tiles + gate_tiles + 2

    def src(j):
        return jnp.where(j < q_tiles, j, jnp.where(j < q_tiles + gate_tiles, j + 1, kv_block))
    return pl.pallas_call(
        functools.partial(_swa_weight_tiles_kernel, q_tiles=q_tiles, gate_tiles=gate_tiles, q_scale=q_scale),
        grid=(n_out,),
        in_specs=[pl.BlockSpec((None, k, tn), lambda j: (layer, 0, src(j)))],
        out_specs=pl.BlockSpec((None, k, tn), lambda j: (j, 0, 0)),
        out_shape=jax.ShapeDtypeStruct((n_out, k, tn), BF16),
        compiler_params=_params(1),
        name="swa_weight_tiles",
    )(w_in)


def _swa_mixer(x, g, w_in, layer, sinks, rel_bias, bsz, seq):
    width = SWA_HEADS * HEAD_DIM
    w_tiles = _swa_weight_tiles(w_in, layer, LOG2E * HEAD_DIM ** -0.5)
    proj = _norm_matmul(x, g, w_tiles, BF16, silu_cols=(width, 2 * width))
    return _swa_attention(proj, sinks, rel_bias, bsz, seq)


HALO = 8


def _conv_kernel(x_ref, g_ref, wb_ref, wc_ref, wu_ref, wg_ref, ck_ref, o_ref, a_ref, z_ref, carry_ref, *, tiles_per_seq):
    i, j = pl.program_id(0), pl.program_id(1)
    tm = a_ref.shape[0]

    @pl.when(j == 0)
    def _():
        def chunk(rows):
            a_ref[rows, :] = _rmsnorm_rows(x_ref[rows, :], g_ref[...]).astype(BF16)
        _for_row_chunks(tm, chunk)

    a = a_ref[...]
    z = _dot(a, wc_ref[...]) * _dot(a, wu_ref[...])
    first = (i % tiles_per_seq) == 0

    @pl.when(first)
    def _():
        z_ref[:HALO, :] = jnp.zeros((HALO, z_ref.shape[1]), F32)

    @pl.when(jnp.logical_not(first))
    def _():
        z_ref[:HALO, :] = carry_ref[j]

    z_ref[HALO:, :] = z
    carry_ref[j] = z[tm - HALO:, :]
    conv = z_ref[HALO - 2:HALO - 2 + tm, :] * ck_ref[0:1, :]
    conv = conv + z_ref[HALO - 1:HALO - 1 + tm, :] * ck_ref[1:2, :]
    conv = conv + z * ck_ref[2:3, :]
    y = _dot(a, wb_ref[...]) * conv
    o_ref[...] = (y * _silu(_dot(a, wg_ref[...]))).astype(o_ref.dtype)


def _conv_mixer(x, g, w_in, layer, conv_kernel, seq, tm=1024, tn=512):
    m, k = x.shape
    width = w_in.shape[2] // 4
    tm, tn = min(tm, seq), min(tn, width)
    nt = width // tn
    w_spec = lambda q: pl.BlockSpec((None, k, tn), lambda i, j: (q * nt + j, 0, 0))
    w_in = _tile_cols(w_in, layer, tn)
    return pl.pallas_call(
        functools.partial(_conv_kernel, tiles_per_seq=seq // tm),
        grid=(m // tm, nt),
        in_specs=[
            pl.BlockSpec((tm, k), lambda i, j: (i, 0)),
            pl.BlockSpec((1, k), lambda i, j: (0, 0)),
            w_spec(0), w_spec(1), w_spec(2), w_spec(3),
            pl.BlockSpec((CONV_TAPS, tn), lambda i, j: (0, j)),
        ],
        out_specs=pl.BlockSpec((tm, tn), lambda i, j: (i, j)),
        out_shape=jax.ShapeDtypeStruct((m, width), BF16),
        scratch_shapes=[
            pltpu.VMEM((tm, k), BF16),
            pltpu.VMEM((HALO + tm, tn), F32),
            pltpu.VMEM((nt, HALO, tn), F32),
        ],
        compiler_params=_params(2, VMEM_LIMIT_BIG),
        name="conv_mixer",
    )(x, g.reshape(1, k), w_in, w_in, w_in, w_in, conv_kernel.astype(F32))


def _ssm_operators(lam_re, lam_im, log_dt, b_re, b_im, c_re, c_im, d_skip):
    n_groups = lam_re.shape[0]
    n_oct = n_groups // GROUPS_PER_TILE
    L, C, N = SSM_CHUNK, SSM_GROUP, SSM_STATE
    dt = jnp.exp(log_dt.astype(F32))[None, :]
    lr, li = lam_re.astype(F32).T, lam_im.astype(F32).T
    mag = jnp.exp(lr * dt)
    ab_re, ab_im = mag * jnp.cos(li * dt), mag * jnp.sin(li * dt)
    den = lr * lr + li * li
    nr = ab_re - 1.0
    coef_re = (nr * lr + ab_im * li) / den
    coef_im = (ab_im * lr - nr * li) / den
    br, bi = b_re.astype(F32).transpose(2, 1, 0), b_im.astype(F32).transpose(2, 1, 0)
    bb_re = coef_re * br - coef_im * bi
    bb_im = coef_re * bi + coef_im * br
    cr, ci = c_re.astype(F32).transpose(1, 2, 0), c_im.astype(F32).transpose(1, 2, 0)
    pw_re, pw_im = [jnp.ones_like(ab_re)], [jnp.zeros_like(ab_im)]
    for _ in range(L):
        pw_re, pw_im = (pw_re + [pw_re[-1] * ab_re - pw_im[-1] * ab_im], pw_im + [pw_re[-1] * ab_im + pw_im[-1] * ab_re])
    p_re, p_im = jnp.stack(pw_re), jnp.stack(pw_im)
    w_re = cr[None] * p_re[:L, None] - ci[None] * p_im[:L, None]
    w_im = cr[None] * p_im[:L, None] + ci[None] * p_re[:L, None]
    lag_k = jnp.sum(w_re[:, None] * bb_re[None, :, None] - w_im[:, None] * bb_im[None, :, None], axis=3)
    q_re, q_im = jnp.stack(pw_re[L - 1::-1]), jnp.stack(pw_im[L - 1::-1])
    bc_re = q_re[:, None] * bb_re[None] - q_im[:, None] * bb_im[None]
    bc_im = q_re[:, None] * bb_im[None] + q_im[:, None] * bb_re[None]
    o_re = cr[None] * p_re[1:, None] - ci[None] * p_im[1:, None]
    o_im = cr[None] * p_im[1:, None] + ci[None] * p_re[1:, None]
    octs = lambda t: t.reshape(t.shape[:-1] + (n_oct, GROUPS_PER_TILE))
    to_lag = lambda t: octs(t).transpose(3, 0, 1, 4, 2).reshape(n_oct, L, C, LANES)
    to_rows = lambda t: octs(t).transpose(3, 0, 1, 4, 2).reshape(n_oct, L * C, GROUPS_PER_TILE * N)
    to_cols = lambda t: octs(t).transpose(3, 4, 2, 0, 1).reshape(n_oct, GROUPS_PER_TILE * N, L * C)
    to_vec = lambda t: octs(t).transpose(1, 2, 0).reshape(n_oct, 1, GROUPS_PER_TILE * N)
    return dict(
        lag=to_lag(lag_k).astype(BF16),
        bc_re=to_rows(bc_re).astype(BF16), bc_im=to_rows(bc_im).astype(BF16),
        oc_re=to_cols(o_re).astype(BF16), oc_im_neg=to_cols(-o_im).astype(BF16),
        al_re=to_vec(p_re[L]), al_im=to_vec(p_im[L]), d=d_skip.astype(F32).reshape(n_oct, 1, LANES),
    )


def _ssm_expanders():
    L, C, N, G8 = SSM_CHUNK, SSM_GROUP, SSM_STATE, GROUPS_PER_TILE
    wide = np.arange(L * LANES)
    w_step, w_group, w_chan = wide // LANES, (wide // C) % G8, wide % C
    small = np.arange(L * C)
    s_step, s_chan = small // C, small % C
    state_group = np.arange(G8 * N) // N
    spread = (w_step[:, None] == s_step[None, :]) & (w_chan[:, None] == s_chan[None, :])
    lane = np.arange(LANES)
    as_bf16 = lambda a: jnp.asarray(a.astype(np.float32), dtype=BF16)
    return dict(
        spread_cols=as_bf16(spread.T),
        keep_cols=as_bf16(state_group[:, None] == w_group[None, :]),
        spread_lag=as_bf16((lane % C)[:, None] == np.arange(C)[None, :]),
        keep_lag=jnp.asarray(((lane // C)[:, None] == (lane // C)[None, :]).astype(np.float32)),
    )


def _ssm_kernel(u_ref, lag_ref, bxr_ref, bxi_ref, cxr_ref, cxi_ref, alr_ref, ali_ref, d_ref,
                sc_ref, kc_ref, sl_ref, kl_ref, o_ref,
                ub_ref, bcr_ref, bci_ref, ocr_ref, oci_ref, zr_ref, zi_ref, hr_ref, hi_ref, *, bsz):
    L = SSM_CHUNK
    rows = u_ref.shape[0] // L
    chunks = rows // bsz
    tile = 2 * LANES
    step_rows = lambda r: pl.ds(r, rows, stride=L)
    for r in range(L):
        ub_ref[:, r * LANES:(r + 1) * LANES] = u_ref[step_rows(r), :].astype(BF16)
    state_group = lax.broadcasted_iota(jnp.int32, (1, bxr_ref.shape[1]), 1) // SSM_STATE
    for compact_ref, full_ref in ((bxr_ref, bcr_ref), (bxi_ref, bci_ref)):
        for r in range(L):
            blk = compact_ref[r * SSM_GROUP:(r + 1) * SSM_GROUP, :]
            for grp in range(GROUPS_PER_TILE):
                first = r * LANES + grp * SSM_GROUP
                full_ref[first:first + SSM_GROUP, :] = jnp.where(state_group == grp, blk, jnp.zeros_like(blk))
    zr_ref[...] = _dot(ub_ref[...], bcr_ref[...])
    zi_ref[...] = _dot(ub_ref[...], bci_ref[...])
    a_re, a_im = alr_ref[...], ali_ref[...]

    def step(k, carry):
        new = []
        for b in range(bsz):
            h_re, h_im = carry[2 * b], carry[2 * b + 1]
            row = pl.ds(b * chunks + k, 1)
            hr_ref[row, :] = h_re
            hi_ref[row, :] = h_im
            new.append(a_re * h_re - a_im * h_im + zr_ref[row, :])
            new.append(a_re * h_im + a_im * h_re + zi_ref[row, :])
        return tuple(new)

    zero = jnp.zeros((1, a_re.shape[1]), F32)
    lax.fori_loop(0, chunks, step, (zero,) * (2 * bsz))

    ocr_ref[...] = (_dot(cxr_ref[...], sc_ref[...]) * kc_ref[...]).astype(BF16)
    oci_ref[...] = (_dot(cxi_ref[...], sc_ref[...]) * kc_ref[...]).astype(BF16)
    lag = [(_dot(sl_ref[...], lag_ref[d]) * kl_ref[...]).astype(BF16) for d in range(L)]
    lag_tile = []
    for dd in range(L // 2):
        below = lag[2 * dd - 1] if dd > 0 else jnp.zeros((LANES, LANES), BF16)
        lag_tile.append(jnp.concatenate([jnp.concatenate([lag[2 * dd], lag[2 * dd + 1]], axis=1),
                                         jnp.concatenate([below, lag[2 * dd]], axis=1)], axis=0))
    hb_re, hb_im = hr_ref[...].astype(BF16), hi_ref[...].astype(BF16)
    for t2 in range(L // 2):
        cols = slice(t2 * tile, (t2 + 1) * tile)
        y = _dot(hb_re, ocr_ref[:, cols]) + _dot(hb_im, oci_ref[:, cols])
        for r2 in range(t2 + 1):
            y = y + _dot(ub_ref[:, r2 * tile:(r2 + 1) * tile], lag_tile[t2 - r2])
        for half in range(2):
            t = 2 * t2 + half
            o_ref[step_rows(t), :] = jax.nn.gelu(y[:, half * LANES:(half + 1) * LANES] + d_ref[...] * u_ref[step_rows(t), :])


def _ssm_core(proj, ops, bsz, seq):
    m = proj.shape[0]
    width = proj.shape[1] // 2
    L, C = SSM_CHUNK, SSM_GROUP
    n_oct = width // LANES
    rows = m // L
    ow = L * LANES
    n_state = GROUPS_PER_TILE * SSM_STATE
    ex = _ssm_expanders()
    per_oct = lambda *shape: pl.BlockSpec((None,) + shape, lambda p: (p,) + (0,) * len(shape))
    const = lambda a: pl.BlockSpec(a.shape, lambda p: (0,) * a.ndim, pipeline_mode=pl.Buffered(1))
    consts = [ex['spread_cols'], ex['keep_cols'], ex['spread_lag'], ex['keep_lag']]
    return pl.pallas_call(
        functools.partial(_ssm_kernel, bsz=bsz),
        grid=(n_oct,),
        in_specs=[
            pl.BlockSpec((m, LANES), lambda p: (0, p)),
            per_oct(L, C, LANES), per_oct(L * C, n_state), per_oct(L * C, n_state),
            per_oct(n_state, L * C), per_oct(n_state, L * C), per_oct(1, n_state), per_oct(1, n_state), per_oct(1, LANES),
        ] + [const(a) for a in consts],
        out_specs=pl.BlockSpec((m, LANES), lambda p: (0, p)),
        out_shape=jax.ShapeDtypeStruct((m, width), F32),
        scratch_shapes=[pltpu.VMEM((rows, ow), BF16)] + [pltpu.VMEM((ow, n_state), BF16)] * 2
        + [pltpu.VMEM((n_state, ow), BF16)] * 2 + [pltpu.VMEM((rows, n_state), F32)] * 4,
        compiler_params=_params(1),
        name="ssm_core",
    )(proj, ops['lag'], ops['bc_re'], ops['bc_im'], ops['oc_re'], ops['oc_im_neg'], ops['al_re'], ops['al_im'], ops['d'], *consts)


def _glu_kernel(y_ref, wa_ref, wb_ref, ba_ref, bb_ref, gate_ref, o_ref, yb_ref):
    @pl.when(pl.program_id(1) == 0)
    def _():
        def chunk(rows):
            yb_ref[rows, :] = y_ref[rows, :].astype(BF16)
        _for_row_chunks(yb_ref.shape[0], chunk)

    y = yb_ref[...]
    ga = _dot(y, wa_ref[...]) + ba_ref[...]
    gb = _dot(y, wb_ref[...]) + bb_ref[...]
    o_ref[...] = ((ga * jax.nn.sigmoid(gb)) * _silu(gate_ref[...])).astype(o_ref.dtype)


def _glu(y, w_glu, layer, b_glu, proj, tm=1024, tn=512):
    m, k = y.shape
    width = w_glu.shape[2] // 2
    tm, tn = min(tm, m), min(tn, width)
    nt = width // tn
    w_tiles = _tile_cols(w_glu, layer, tn)
    return pl.pallas_call(
        _glu_kernel,
        grid=(m // tm, nt),
        in_specs=[
            pl.BlockSpec((tm, k), lambda i, j: (i, 0)),
            pl.BlockSpec((None, k, tn), lambda i, j: (j, 0, 0)),
            pl.BlockSpec((None, k, tn), lambda i, j: (nt + j, 0, 0)),
            pl.BlockSpec((1, tn), lambda i, j: (0, j)),
            pl.BlockSpec((1, tn), lambda i, j: (0, nt + j)),
            pl.BlockSpec((tm, tn), lambda i, j: (i, nt + j)),
        ],
        out_specs=pl.BlockSpec((tm, tn), lambda i, j: (i, j)),
        out_shape=jax.ShapeDtypeStruct((m, width), BF16),
        scratch_shapes=[pltpu.VMEM((tm, k), BF16)],
        compiler_params=_params(2),
        name="ssm_glu",
    )(y, w_tiles, w_tiles, b_glu, b_glu, proj)


def _ssm_mixer(x, g, w_in, layer, lam_re, lam_im, log_dt, b_re, b_im, c_re, c_im, d_skip, w_glu, b_glu, bsz, seq):
    proj = _norm_matmul(x, g, _tile_cols(w_in, layer, MATMUL_TN), F32)
    ops = _ssm_operators(lam_re, lam_im, log_dt, b_re, b_im, c_re, c_im, d_skip)
    y = _ssm_core(proj, ops, bsz, seq)
    return _glu(y, w_glu, layer, b_glu.astype(F32).reshape(1, -1), proj)


CUM_BLOCK = 128


def _split3(x):
    x1 = x.astype(BF16)
    r1 = x - x1.astype(F32)
    x2 = r1.astype(BF16)
    x3 = (r1 - x2.astype(F32)).astype(BF16)
    return x1, x2, x3


def _forget_cumsum_kernel(z_ref, b_ref, o_ref):
    n_blocks = z_ref.shape[0] // CUM_BLOCK
    ri = lax.broadcasted_iota(jnp.int32, (CUM_BLOCK, CUM_BLOCK), 0)
    ci = lax.broadcasted_iota(jnp.int32, (CUM_BLOCK, CUM_BLOCK), 1)
    tri = jnp.where(ci <= ri, 1.0, 0.0).astype(BF16)

    def body(i, carry):
        rows = pl.ds(pl.multiple_of(i * CUM_BLOCK, CUM_BLOCK), CUM_BLOCK)
        z = z_ref[rows, :] + b_ref[...]
        log_f = jnp.minimum(z, 0.0) - jnp.log1p(jnp.exp(-jnp.abs(z)))
        x1, x2, x3 = _split3(log_f)
        c = _dot(tri, x1) + _dot(tri, x2) + _dot(tri, x3) + carry
        o_ref[rows, :] = c
        return c[CUM_BLOCK - 1:, :]

    lax.fori_loop(0, n_blocks, body, jnp.zeros((1, z_ref.shape[1]), F32))


def _forget_cumsum(z, b, bsz, seq):
    lanes = z.shape[1]
    return pl.pallas_call(
        _forget_cumsum_kernel,
        grid=(bsz,),
        in_specs=[pl.BlockSpec((seq, lanes), lambda i: (i, 0)), pl.BlockSpec((1, lanes), lambda i: (0, 0))],
        out_specs=pl.BlockSpec((seq, lanes), lambda i: (i, 0)),
        out_shape=jax.ShapeDtypeStruct(z.shape, F32),
        compiler_params=_params(1),
        name="forget_cumsum",
    )(z, b)


FOX_STEP_HEADS = 8


def _fox_kernel(q_ref, k_ref, v_ref, sg_ref, cq_ref, ck_ref, o_ref, ka_ref, vt_ref, *head_refs, tk):
    nh = FOX_STEP_HEADS
    hg, qi = pl.program_id(1), pl.program_id(2)
    tq = q_ref.shape[0]
    qa_ref, s_ref, p_ref, acc_ref, pv_ref, m_ref = (head_refs[i::6] for i in range(6))
    lane = lax.broadcasted_iota(jnp.int32, (1, LANES), 1)
    free = (HEAD_DIM, 0)
    n_pieces = 3
    pair_lanes = lambda h: slice(h // 2 * LANES, (h // 2 + 1) * LANES)

    def bias_lanes(c, h, c_offset, one_offset):
        rr = lax.broadcasted_iota(jnp.int32, (n_pieces * LANES, LANES), 0)
        cc = lax.broadcasted_iota(jnp.int32, (n_pieces * LANES, LANES), 1)
        place = jnp.logical_and(rr % LANES == nh * hg + h, cc == free[h % 2] + c_offset + rr // LANES)
        placed = _dot(jnp.concatenate(_split3(c), axis=1), jnp.where(place, 1.0, 0.0).astype(BF16))
        first = free[h % 2] + one_offset
        return (placed + jnp.where(jnp.logical_and(lane >= first, lane < first + n_pieces), 1.0, 0.0)).astype(BF16)

    @pl.when(qi == 0)
    def _():
        def chunk(j, carry):
            rows = pl.ds(pl.multiple_of(j * tk, tk), tk)
            neg_c = ck_ref[rows, :] * -LOG2E
            for h in range(nh):
                keys = k_ref[rows, pair_lanes(h)]
                ka_ref[h, j] = jnp.where((lane // HEAD_DIM) == h % 2, keys, bias_lanes(neg_c, h, 0, n_pieces))
            for pair in range(nh // 2):
                v_t = v_ref[rows, pair_lanes(2 * pair)].astype(F32).T.astype(BF16)
                for hh in range(2):
                    vt_ref[j, 2 * pair + hh, :HEAD_DIM, :] = v_t[hh * HEAD_DIM:(hh + 1) * HEAD_DIM]
                    vt_ref[j, 2 * pair + hh, HEAD_DIM:, :] = jnp.ones((ONES_ROWS, tk), BF16)
            return carry
        lax.fori_loop(0, k_ref.shape[0] // tk, chunk, 0)

    cq = cq_ref[...] * LOG2E
    for h in range(nh):
        qa = jnp.where((lane // HEAD_DIM) == h % 2, q_ref[:, pair_lanes(h)], bias_lanes(cq, h, n_pieces, 0))
        qa_ref[h][...] = qa.astype(F32).T.astype(BF16)

    def scores(h, j):
        s_ref[h][...] = _dot(ka_ref[h, j], qa_ref[h][...])

    rows8 = 8
    rows16 = 16

    def absorb(h, j, diagonal):
        def strip(r, n):
            blk = s_ref[h][r:r + n, :]
            if diagonal:
                kpos = lax.broadcasted_iota(jnp.int32, (n, tq), 0)
                qpos = lax.broadcasted_iota(jnp.int32, (n, tq), 1)
                blk = jnp.where(kpos + r <= qpos, blk, NEG)
            return blk
        top = strip(0, rows8)
        for r in range(rows8, tk, rows8):
            top = jnp.maximum(top, strip(r, rows8))
        m = m_ref[h][...]
        m_new = jnp.maximum(m, top.max(axis=0, keepdims=True))
        m_ref[h][...] = m_new
        alpha = jnp.exp2(m - m_new)
        m_rows = jnp.broadcast_to(m_new, (rows16, tq))
        for r in range(0, tk, rows16):
            p_ref[h][r:r + rows16, :] = jnp.exp2(strip(r, rows16) - m_rows).astype(BF16)
        acc_ref[h][...] = alpha * (acc_ref[h][...] + pv_ref[h][...])

    def values(h, j):
        pv_ref[h][...] = _dot(vt_ref[j, h], p_ref[h][...])

    def step(j, diagonal):
        values(nh - 1, jnp.maximum(j - 1, 0))
        scores(1, j)
        for h in range(nh):
            absorb(h, j, diagonal)
            if h < nh - 1:
                values(h, j)
            if h + 2 < nh:
                scores(h + 2, j)
            elif h + 2 == nh and not diagonal:
                scores(0, j + 1)

    for h in range(nh):
        acc_ref[h][...] = jnp.zeros(acc_ref[h].shape, F32)
        pv_ref[h][...] = jnp.zeros(pv_ref[h].shape, F32)
        m_ref[h][...] = jnp.full(m_ref[h].shape, NEG, F32)
    p_ref[nh - 1][...] = jnp.zeros(p_ref[nh - 1].shape, BF16)
    scores(0, 0)

    def two_blocks(i, carry):
        step(2 * i, False)
        step(2 * i + 1, False)
        return carry
    lax.fori_loop(0, qi // 2, two_blocks, 0)

    @pl.when(qi % 2 == 0)
    def _():
        step(qi, True)

    @pl.when(qi % 2 == 1)
    def _():
        step(qi - 1, False)
        step(qi, True)

    values(nh - 1, qi)
    outs = []
    for h in range(nh):
        total = acc_ref[h][...] + pv_ref[h][...]
        outs.append(total[:HEAD_DIM] / total[HEAD_DIM:HEAD_DIM + 1])
    out = jnp.concatenate(outs, axis=0)
    o_ref[...] = (out.T * sg_ref[...].astype(F32)).astype(o_ref.dtype)


def _fox_attention(proj, csum, bsz, seq, t=512):
    width = FOX_HEADS * HEAD_DIM
    nh = FOX_STEP_HEADS
    gw = nh * HEAD_DIM
    n_groups = width // gw
    t = min(t, seq)
    nq = seq // t
    return pl.pallas_call(
        functools.partial(_fox_kernel, tk=t),
        grid=(bsz, n_groups, nq),
        in_specs=[
            pl.BlockSpec((t, gw), lambda b, h, i: (b * nq + i, h)),
            pl.BlockSpec((seq, gw), lambda b, h, i: (b, n_groups + h), pipeline_mode=pl.Buffered(1)),
            pl.BlockSpec((seq, gw), lambda b, h, i: (b, 2 * n_groups + h), pipeline_mode=pl.Buffered(1)),
            pl.BlockSpec((t, gw), lambda b, h, i: (b * nq + i, 3 * n_groups + h)),
            pl.BlockSpec((t, LANES), lambda b, h, i: (b * nq + i, 0)),
            pl.BlockSpec((seq, LANES), lambda b, h, i: (b, 0), pipeline_mode=pl.Buffered(1)),
        ],
        out_specs=pl.BlockSpec((t, gw), lambda b, h, i: (b * nq + i, h)),
        out_shape=jax.ShapeDtypeStruct((bsz * seq, width), BF16),
        scratch_shapes=[
            pltpu.VMEM((nh, nq, t, LANES), BF16), pltpu.VMEM((nq, nh, HEAD_DIM + ONES_ROWS, t), BF16),
        ] + nh * [pltpu.VMEM((LANES, t), BF16), pltpu.VMEM((t, t), F32), pltpu.VMEM((t, t), BF16),
                  pltpu.VMEM((HEAD_DIM + ONES_ROWS, t), F32), pltpu.VMEM((HEAD_DIM + ONES_ROWS, t), F32),
                  pltpu.VMEM((1, t), F32)],
        compiler_params=_params(3, VMEM_LIMIT_BIG),
        name="fox_attention",
    )(proj, proj, proj, proj, csum, csum)


def _fox_mixer(x, g, w_in, layer, w_fg, b_fg, bsz, seq, t=512):
    width = FOX_HEADS * HEAD_DIM
    col_scale = jnp.where(jnp.arange(4 * width) < width, LOG2E * HEAD_DIM ** -0.5, 1.0).astype(F32)
    proj = _norm_matmul(x, g, _tile_cols(w_in, layer, MATMUL_TN, col_scale), BF16, silu_cols=(3 * width, 4 * width))
    pad = LANES - FOX_HEADS
    z = _norm_matmul(x, g, _tile_cols_xla(jnp.pad(w_fg, ((0, 0), (0, pad))), LANES), F32)
    csum = _forget_cumsum(z, jnp.pad(b_fg.astype(F32), (0, pad)).reshape(1, LANES), bsz, seq)
    return _fox_attention(proj, csum, bsz, seq, t)


def kernel(x, p, norm_g, final_g, rel_bias, swa_w_in, swa_w_out, swa_sinks, conv_w_in, conv_kernel, conv_w_out, ssm_w_in, ssm_lam_re, ssm_lam_im, ssm_log_dt, ssm_b_re, ssm_b_im, ssm_c_re, ssm_c_im, ssm_d, ssm_w_glu, ssm_b_glu, ssm_w_out, fox_w_in, fox_w_fg, fox_b_fg, fox_w_out, ple_proj, ple_norm, ple_gate):
    bsz, seq, d_model = x.shape
    depth = p.shape[0]
    h = x.astype(F32).reshape(bsz * seq, d_model)
    for i in range(depth):
        mixer, j = i % N_MIXERS, i // N_MIXERS
        if mixer == 0:
            a, w_out = _swa_mixer(h, norm_g[i], swa_w_in, j, swa_sinks[j], rel_bias, bsz, seq), swa_w_out[j]
        elif mixer == 1:
            a, w_out = _conv_mixer(h, norm_g[i], conv_w_in, j, conv_kernel[j], seq), conv_w_out[j]
        elif mixer == 2:
            a = _ssm_mixer(h, norm_g[i], ssm_w_in, j, ssm_lam_re[j], ssm_lam_im[j], ssm_log_dt[j], ssm_b_re[j], ssm_b_im[j],
                           ssm_c_re[j], ssm_c_im[j], ssm_d[j], ssm_w_glu, ssm_b_glu[j], bsz, seq)
            w_out = ssm_w_out[j]
        else:
            a, w_out = _fox_mixer(h, norm_g[i], fox_w_in, j, fox_w_fg[j], fox_b_fg[j], bsz, seq), fox_w_out[j]
        h = _out_ple(a, w_out, h, ple_norm[i], ple_gate, p.reshape(depth, bsz * seq, -1), ple_proj, i,
                     final_g=final_g if i == depth - 1 else None)
    return h.reshape(bsz, seq, d_model).astype(x.dtype)
```

```python
import functools
import math

import numpy as np
import jax
import jax.numpy as jnp
from jax import lax
from jax.experimental import pallas as pl
from jax.experimental.pallas import tpu as pltpu

F32 = jnp.float32
BF16 = jnp.bfloat16

EPS = 1e-6
N_MIXERS = 4
PLE_DIM = 256

SWA_HEADS = 32
SWA_KV_HEADS = 4
SWA_GROUP = SWA_HEADS // SWA_KV_HEADS
HEAD_DIM = 64
SWA_BLOCK = 128
WINDOW = 128
REL_BUCKETS = 32
REL_MAX_DIST = 128

CONV_TAPS = 3

SSM_GROUP = 16
SSM_STATE = 64
SSM_CHUNK = 16

FOX_HEADS = 32

LANES = 128
ONES_ROWS = 16
GROUPS_PER_TILE = LANES // SSM_GROUP
VMEM_LIMIT = 48 * 1024 * 1024
VMEM_LIMIT_BIG = 60 * 1024 * 1024

NEG = float(jnp.finfo(jnp.float32).min)
LOG2E = math.log2(math.e)


def _params(n_axes, vmem_limit=VMEM_LIMIT):
    return pltpu.CompilerParams(dimension_semantics=("arbitrary",) * n_axes, vmem_limit_bytes=vmem_limit)


def _dot(a, b):
    return jnp.dot(a, b, preferred_element_type=F32)


def _dot_nt(a, b):
    return lax.dot_general(a, b, (((1,), (1,)), ((), ())), preferred_element_type=F32)


def _rmsnorm_rows(x, g):
    return x * lax.rsqrt(jnp.mean(x * x, axis=-1, keepdims=True) + EPS) * g


def _silu(x):
    return x * jax.nn.sigmoid(x)


ROW_CHUNK = 64


def _for_row_chunks(n_rows, fn):
    def body(c, carry):
        fn(pl.ds(pl.multiple_of(c * ROW_CHUNK, ROW_CHUNK), ROW_CHUNK))
        return carry
    lax.fori_loop(0, n_rows // ROW_CHUNK, body, 0)


def _tile_cols_xla(w, tn):
    k, n = w.shape
    return w.astype(BF16).reshape(k, n // tn, tn).transpose(1, 0, 2)


def _cast_tiles_kernel(w_ref, o_ref):
    def chunk(rows):
        o_ref[rows, :] = w_ref[rows, :].astype(BF16)
    _for_row_chunks(o_ref.shape[0], chunk)


def _scale_cast_tiles_kernel(w_ref, s_ref, o_ref):
    def chunk(rows):
        o_ref[rows, :] = (w_ref[rows, :] * s_ref[...]).astype(BF16)
    _for_row_chunks(o_ref.shape[0], chunk)


def _tile_cols(w, layer, tn, col_scale=None):
    _, k, n = w.shape
    tn = min(tn, n)
    w_spec = pl.BlockSpec((None, k, tn), lambda j: (layer, 0, j))
    common = dict(
        grid=(n // tn,),
        out_specs=pl.BlockSpec((None, k, tn), lambda j: (j, 0, 0)),
        out_shape=jax.ShapeDtypeStruct((n // tn, k, tn), BF16),
        compiler_params=_params(1),
        name="weight_tiles",
    )
    if col_scale is None:
        return pl.pallas_call(_cast_tiles_kernel, in_specs=[w_spec], **common)(w)
    s_spec = pl.BlockSpec((1, tn), lambda j: (0, j))
    return pl.pallas_call(_scale_cast_tiles_kernel, in_specs=[w_spec, s_spec], **common)(w, col_scale.reshape(1, n))


def _norm_matmul_kernel(x_ref, g_ref, w_ref, o_ref, a_ref, *, silu_tiles):
    j = pl.program_id(1)

    @pl.when(j == 0)
    def _():
        def chunk(rows):
            a_ref[rows, :] = _rmsnorm_rows(x_ref[rows, :], g_ref[...]).astype(BF16)
        _for_row_chunks(a_ref.shape[0], chunk)

    acc = _dot(a_ref[...], w_ref[...])
    if silu_tiles is None:
        o_ref[...] = acc.astype(o_ref.dtype)
    else:
        gated = jnp.logical_and(j >= silu_tiles[0], j < silu_tiles[1])

        @pl.when(gated)
        def _():
            o_ref[...] = _silu(acc).astype(o_ref.dtype)

        @pl.when(jnp.logical_not(gated))
        def _():
            o_ref[...] = acc.astype(o_ref.dtype)


MATMUL_TN = 1024


def _norm_matmul(x, g, w_tiles, out_dtype, silu_cols=None, tm=1024):
    m, k = x.shape
    tn = w_tiles.shape[2]
    n = w_tiles.shape[0] * tn
    tm = min(tm, m)
    silu_tiles = None if silu_cols is None else (silu_cols[0] // tn, silu_cols[1] // tn)
    return pl.pallas_call(
        functools.partial(_norm_matmul_kernel, silu_tiles=silu_tiles),
        grid=(m // tm, n // tn),
        in_specs=[
            pl.BlockSpec((tm, k), lambda i, j: (i, 0)),
            pl.BlockSpec((1, k), lambda i, j: (0, 0)),
            pl.BlockSpec((None, k, tn), lambda i, j: (j, 0, 0)),
        ],
        out_specs=pl.BlockSpec((tm, tn), lambda i, j: (i, j)),
        out_shape=jax.ShapeDtypeStruct((m, n), out_dtype),
        scratch_shapes=[pltpu.VMEM((tm, k), BF16)],
        compiler_params=_params(2),
        name="norm_matmul",
    )(x, g.reshape(1, k), w_tiles)


def _out_ple_kernel(a_ref, wo_ref, x_ref, g_ref, wg_ref, p_ref, wp_ref, gf_ref, o_ref, x1_ref, hn_ref, ss_ref, *, nt, final):
    j = pl.program_id(1)
    tn = x_ref.shape[1]
    n = nt * tn

    @pl.when(j == 0)
    def _():
        ss_ref[...] = jnp.zeros(ss_ref.shape, F32)

    @pl.when(j < nt)
    def _():
        x1 = x_ref[...] + _dot(a_ref[...], wo_ref[j])
        x1_ref[j] = x1
        hn_ref[j] = (x1 * g_ref[j]).astype(BF16)
        ss_ref[0] += jnp.sum(x1 * x1, axis=-1, keepdims=True)

    @pl.when(jnp.logical_and(j >= nt, j < 2 * nt))
    def _():
        t = j - nt
        emb = _dot(p_ref[...].astype(BF16), wp_ref[t])
        acc = _dot(hn_ref[0], wg_ref[t, :tn, :])
        for kt in range(1, nt):
            acc = acc + _dot(hn_ref[kt], wg_ref[t, kt * tn:(kt + 1) * tn, :])
        x2 = x1_ref[t] + emb * jax.nn.sigmoid(acc * lax.rsqrt(ss_ref[0] / n + EPS))
        if final:
            x1_ref[t] = x2
            ss_ref[1] += jnp.sum(x2 * x2, axis=-1, keepdims=True)
        else:
            o_ref[...] = x2

    if final:
        @pl.when(j >= 2 * nt)
        def _():
            t = j - 2 * nt
            o_ref[...] = x1_ref[t] * lax.rsqrt(ss_ref[1] / n + EPS) * gf_ref[t]


def _out_ple(a, w_out, x, g, w_gate, p, w_proj, layer, final_g=None, tm=1024, tn=512):
    m, k = a.shape
    n = w_out.shape[1]
    pd = p.shape[2]
    final = final_g is not None
    tm, tn = min(tm, m), min(tn, n)
    nt = n // tn
    passes = 3 if final else 2
    out_tile = lambda j: jnp.maximum(j - (passes - 1) * nt, 0)
    resident = lambda rows: pl.BlockSpec((nt, rows, tn), lambda i, j: (0, 0, 0), pipeline_mode=pl.Buffered(1))
    row_vec = pl.BlockSpec((nt, 1, tn), lambda i, j: (0, 0, 0))
    return pl.pallas_call(
        functools.partial(_out_ple_kernel, nt=nt, final=final),
        grid=(m // tm, passes * nt),
        in_specs=[
            pl.BlockSpec((tm, k), lambda i, j: (i, 0)),
            resident(k),
            pl.BlockSpec((tm, tn), lambda i, j: (i, jnp.minimum(j, nt - 1))),
            row_vec,
            resident(n),
            pl.BlockSpec((None, tm, pd), lambda i, j: (layer, i, 0)),
            resident(pd),
            row_vec,
        ],
        out_specs=pl.BlockSpec((tm, tn), lambda i, j: (i, out_tile(j))),
        out_shape=jax.ShapeDtypeStruct((m, n), F32),
        scratch_shapes=[pltpu.VMEM((nt, tm, tn), F32), pltpu.VMEM((nt, tm, tn), BF16), pltpu.VMEM((2, tm, 1), F32)],
        compiler_params=_params(2, VMEM_LIMIT_BIG),
        name="out_ple_final" if final else "out_ple",
    )(a, _tile_cols(w_out[None], 0, tn), x, g.reshape(nt, 1, tn), _tile_cols(w_gate, layer, tn), p,
      _tile_cols(w_proj, layer, tn), (final_g if final else g).reshape(nt, 1, tn))


def _t5_bucket(dist):
    max_exact = REL_BUCKETS // 2
    d = np.maximum(dist, 1).astype(np.float32)
    large = max_exact + (np.log(d / max_exact) / np.log(REL_MAX_DIST / max_exact) * (REL_BUCKETS - max_exact)).astype(np.int32)
    large = np.minimum(large, REL_BUCKETS - 1)
    return np.where(dist < max_exact, dist, large).astype(np.int32)


def _swa_bucket_table():
    qi = np.arange(SWA_BLOCK)[None, :]
    kj = np.arange(2 * SWA_BLOCK)[:, None]
    return _t5_bucket(np.clip(qi + SWA_BLOCK - kj, 0, None))


def _swa_kernel(sink_ref, rel_ref, bucket_ref, q_ref, sg_ref, kp_ref, kc_ref, vp_ref, vc_ref, o_ref, bias_ref, qs_ref):
    n = pl.program_id(1)
    blk = SWA_BLOCK
    lane = lax.broadcasted_iota(jnp.int32, (1, LANES), 1)

    @pl.when(jnp.logical_and(pl.program_id(0) == 0, n == 0))
    def _():
        bucket = bucket_ref[...]
        kj = lax.broadcasted_iota(jnp.int32, (2 * blk, blk), 0)
        qi = lax.broadcasted_iota(jnp.int32, (2 * blk, blk), 1)
        band = jnp.logical_or(jnp.logical_and(kj < blk, kj > qi), jnp.logical_and(kj >= blk, kj - blk <= qi))

        def per_head(head, carry):
            acc = jnp.zeros((2 * blk, blk), F32)
            for b in range(REL_BUCKETS):
                acc = jnp.where(bucket == b, rel_ref[b, head] * LOG2E, acc)
            bias_ref[head] = jnp.where(band, acc, NEG)
            return carry
        lax.fori_loop(0, SWA_HEADS, per_head, 0)

    for kvh in range(SWA_KV_HEADS):
        cols = slice(kvh * LANES, (kvh + 1) * LANES)
        heads = range(kvh * SWA_GROUP, (kvh + 1) * SWA_GROUP)
        for g, head in enumerate(heads):
            q_pair = q_ref[:, head // 2 * LANES:(head // 2 + 1) * LANES]
            qs_ref[g * blk:(g + 1) * blk, :] = jnp.where((lane // HEAD_DIM) == head % 2, q_pair, jnp.zeros_like(q_pair))
        keys = jnp.concatenate([kp_ref[:, cols], kc_ref[:, cols]], axis=0)
        vals = jnp.concatenate([vp_ref[:, cols], vc_ref[:, cols]], axis=0)
        s = _dot_nt(keys, qs_ref[...]) + jnp.concatenate([bias_ref[head] for head in heads], axis=1)
        s_prev = jnp.where(n > 0, s[:blk], NEG)
        s_cur = s[blk:]
        sink = jnp.concatenate([jnp.full((1, blk), sink_ref[head] * LOG2E, F32) for head in heads], axis=1)
        m = jnp.maximum(jnp.maximum(s_prev, s_cur).max(axis=0, keepdims=True), sink)
        e = jnp.concatenate([jnp.exp2(s_prev - m).astype(BF16), jnp.exp2(s_cur - m).astype(BF16)], axis=0)
        v_t = jnp.concatenate([vals.astype(F32).T.astype(BF16)[:HEAD_DIM], jnp.ones((ONES_ROWS, 2 * blk), BF16)], axis=0)
        out = _dot(v_t, e)
        out = out[:HEAD_DIM] / (out[HEAD_DIM:HEAD_DIM + 1] + jnp.exp2(sink - m))
        for pair in range(SWA_GROUP // 2):
            head = kvh * SWA_GROUP + 2 * pair
            even = out[:, 2 * pair * blk:(2 * pair + 1) * blk]
            odd = out[:, (2 * pair + 1) * blk:(2 * pair + 2) * blk]
            cols = slice(head // 2 * LANES, (head // 2 + 1) * LANES)
            o_ref[:, cols] = (jnp.concatenate([even, odd], axis=0).T * sg_ref[:, cols].astype(F32)).astype(o_ref.dtype)


def _swa_attention(proj, sinks, rel_bias, bsz, seq):
    nb = seq // SWA_BLOCK
    width = SWA_HEADS * HEAD_DIM
    kvw = SWA_KV_HEADS * LANES
    k_blk = 2 * width // kvw
    cur = lambda b, n: b * nb + n
    prev = lambda b, n: b * nb + jnp.maximum(n - 1, 0)
    smem = pl.BlockSpec(memory_space=pltpu.SMEM)
    return pl.pallas_call(
        _swa_kernel,
        grid=(bsz, nb),
        in_specs=[
            smem, smem,
            pl.BlockSpec((2 * SWA_BLOCK, SWA_BLOCK), lambda b, n: (0, 0)),
            pl.BlockSpec((SWA_BLOCK, width), lambda b, n: (cur(b, n), 0)),
            pl.BlockSpec((SWA_BLOCK, width), lambda b, n: (cur(b, n), 1)),
            pl.BlockSpec((SWA_BLOCK, kvw), lambda b, n: (prev(b, n), k_blk)),
            pl.BlockSpec((SWA_BLOCK, kvw), lambda b, n: (cur(b, n), k_blk)),
            pl.BlockSpec((SWA_BLOCK, kvw), lambda b, n: (prev(b, n), k_blk + 1)),
            pl.BlockSpec((SWA_BLOCK, kvw), lambda b, n: (cur(b, n), k_blk + 1)),
        ],
        out_specs=pl.BlockSpec((SWA_BLOCK, width), lambda b, n: (cur(b, n), 0)),
        out_shape=jax.ShapeDtypeStruct((bsz * seq, width), BF16),
        scratch_shapes=[
            pltpu.VMEM((SWA_HEADS, 2 * SWA_BLOCK, SWA_BLOCK), F32),
            pltpu.VMEM((SWA_GROUP * SWA_BLOCK, LANES), BF16),
        ],
        compiler_params=_params(2),
        name="swa_attention",
    )(sinks.astype(F32), rel_bias.astype(F32), jnp.asarray(_swa_bucket_table()), proj, proj, proj, proj, proj, proj)


SWA_TN = 512


def _swa_weight_tiles_kernel(w_ref, o_ref, *, q_tiles, gate_tiles, q_scale):
    j = pl.program_id(0)
    kvw = SWA_KV_HEADS * HEAD_DIM

    @pl.when(j < q_tiles)
    def _():
        def chunk(rows):
            o_ref[rows, :] = (w_ref[rows, :] * q_scale).astype(BF16)
        _for_row_chunks(o_ref.shape[0], chunk)

    @pl.when(jnp.logical_and(j >= q_tiles, j < q_tiles + gate_tiles))
    def _():
        def chunk(rows):
            o_ref[rows, :] = w_ref[rows, :].astype(BF16)
        _for_row_chunks(o_ref.shape[0], chunk)

    for part in range(2):
        @pl.when(j == q_tiles + gate_tiles + part)
        def _():
            def chunk(rows):
                w = w_ref[rows, part * kvw:(part + 1) * kvw].astype(BF16)
                heads = [w[:, h * HEAD_DIM:(h + 1) * HEAD_DIM] for h in range(SWA_KV_HEADS)]
                o_ref[rows, :] = jnp.concatenate([piece for head in heads for piece in (head, head)], axis=1)
            _for_row_chunks(o_ref.shape[0], chunk)


def _swa_weight_tiles(w_in, layer, q_scale):
    _, k, n = w_in.shape
    tn = SWA_TN
    width = SWA_HEADS * HEAD_DIM
    q_tiles = gate_tiles = width // tn
    kv_block = width // tn
    n_out = q_tiles + gate_tiles + 2

    def src(j):
        return jnp.where(j < q_tiles, j, jnp.where(j < q_tiles + gate_tiles, j + 1, kv_block))
    return pl.pallas_call(
        functools.partial(_swa_weight_tiles_kernel, q_tiles=q_tiles, gate_tiles=gate_tiles, q_scale=q_scale),
        grid=(n_out,),
        in_specs=[pl.BlockSpec((None, k, tn), lambda j: (layer, 0, src(j)))],
        out_specs=pl.BlockSpec((None, k, tn), lambda j: (j, 0, 0)),
        out_shape=jax.ShapeDtypeStruct((n_out, k, tn), BF16),
        compiler_params=_params(1),
        name="swa_weight_tiles",
    )(w_in)


def _swa_mixer(x, g, w_in, layer, sinks, rel_bias, bsz, seq):
    width = SWA_HEADS * HEAD_DIM
    w_tiles = _swa_weight_tiles(w_in, layer, LOG2E * HEAD_DIM ** -0.5)
    proj = _norm_matmul(x, g, w_tiles, BF16, silu_cols=(width, 2 * width))
    return _swa_attention(proj, sinks, rel_bias, bsz, seq)


HALO = 8


def _conv_kernel(x_ref, g_ref, wb_ref, wc_ref, wu_ref, wg_ref, ck_ref, o_ref, a_ref, z_ref, carry_ref, *, tiles_per_seq):
    i, j = pl.program_id(0), pl.program_id(1)
    tm = a_ref.shape[0]

    @pl.when(j == 0)
    def _():
        def chunk(rows):
            a_ref[rows, :] = _rmsnorm_rows(x_ref[rows, :], g_ref[...]).astype(BF16)
        _for_row_chunks(tm, chunk)

    a = a_ref[...]
    z = _dot(a, wc_ref[...]) * _dot(a, wu_ref[...])
    first = (i % tiles_per_seq) == 0

    @pl.when(first)
    def _():
        z_ref[:HALO, :] = jnp.zeros((HALO, z_ref.shape[1]), F32)

    @pl.when(jnp.logical_not(first))
    def _():
        z_ref[:HALO, :] = carry_ref[j]

    z_ref[HALO:, :] = z
    carry_ref[j] = z[tm - HALO:, :]
    conv = z_ref[HALO - 2:HALO - 2 + tm, :] * ck_ref[0:1, :]
    conv = conv + z_ref[HALO - 1:HALO - 1 + tm, :] * ck_ref[1:2, :]
    conv = conv + z * ck_ref[2:3, :]
    y = _dot(a, wb_ref[...]) * conv
    o_ref[...] = (y * _silu(_dot(a, wg_ref[...]))).astype(o_ref.dtype)


def _conv_mixer(x, g, w_in, layer, conv_kernel, seq, tm=1024, tn=512):
    m, k = x.shape
    width = w_in.shape[2] // 4
    tm, tn = min(tm, seq), min(tn, width)
    nt = width // tn
    w_spec = lambda q: pl.BlockSpec((None, k, tn), lambda i, j: (q * nt + j, 0, 0))
    w_in = _tile_cols(w_in, layer, tn)
    return pl.pallas_call(
        functools.partial(_conv_kernel, tiles_per_seq=seq // tm),
        grid=(m // tm, nt),
        in_specs=[
            pl.BlockSpec((tm, k), lambda i, j: (i, 0)),
            pl.BlockSpec((1, k), lambda i, j: (0, 0)),
            w_spec(0), w_spec(1), w_spec(2), w_spec(3),
            pl.BlockSpec((CONV_TAPS, tn), lambda i, j: (0, j)),
        ],
        out_specs=pl.BlockSpec((tm, tn), lambda i, j: (i, j)),
        out_shape=jax.ShapeDtypeStruct((m, width), BF16),
        scratch_shapes=[
            pltpu.VMEM((tm, k), BF16),
            pltpu.VMEM((HALO + tm, tn), F32),
            pltpu.VMEM((nt, HALO, tn), F32),
        ],
        compiler_params=_params(2, VMEM_LIMIT_BIG),
        name="conv_mixer",
    )(x, g.reshape(1, k), w_in, w_in, w_in, w_in, conv_kernel.astype(F32))


def _ssm_operators(lam_re, lam_im, log_dt, b_re, b_im, c_re, c_im, d_skip):
    n_groups = lam_re.shape[0]
    n_oct = n_groups // GROUPS_PER_TILE
    L, C, N = SSM_CHUNK, SSM_GROUP, SSM_STATE
    dt = jnp.exp(log_dt.astype(F32))[None, :]
    lr, li = lam_re.astype(F32).T, lam_im.astype(F32).T
    mag = jnp.exp(lr * dt)
    ab_re, ab_im = mag * jnp.cos(li * dt), mag * jnp.sin(li * dt)
    den = lr * lr + li * li
    nr = ab_re - 1.0
    coef_re = (nr * lr + ab_im * li) / den
    coef_im = (ab_im * lr - nr * li) / den
    br, bi = b_re.astype(F32).transpose(2, 1, 0), b_im.astype(F32).transpose(2, 1, 0)
    bb_re = coef_re * br - coef_im * bi
    bb_im = coef_re * bi + coef_im * br
    cr, ci = c_re.astype(F32).transpose(1, 2, 0), c_im.astype(F32).transpose(1, 2, 0)
    pw_re, pw_im = [jnp.ones_like(ab_re)], [jnp.zeros_like(ab_im)]
    for _ in range(L):
        pw_re, pw_im = (pw_re + [pw_re[-1] * ab_re - pw_im[-1] * ab_im], pw_im + [pw_re[-1] * ab_im + pw_im[-1] * ab_re])
    p_re, p_im = jnp.stack(pw_re), jnp.stack(pw_im)
    w_re = cr[None] * p_re[:L, None] - ci[None] * p_im[:L, None]
    w_im = cr[None] * p_im[:L, None] + ci[None] * p_re[:L, None]
    lag_k = jnp.sum(w_re[:, None] * bb_re[None, :, None] - w_im[:, None] * bb_im[None, :, None], axis=3)
    q_re, q_im = jnp.stack(pw_re[L - 1::-1]), jnp.stack(pw_im[L - 1::-1])
    bc_re = q_re[:, None] * bb_re[None] - q_im[:, None] * bb_im[None]
    bc_im = q_re[:, None] * bb_im[None] + q_im[:, None] * bb_re[None]
    o_re = cr[None] * p_re[1:, None] - ci[None] * p_im[1:, None]
    o_im = cr[None] * p_im[1:, None] + ci[None] * p_re[1:, None]
    octs = lambda t: t.reshape(t.shape[:-1] + (n_oct, GROUPS_PER_TILE))
    to_lag = lambda t: octs(t).transpose(3, 0, 1, 4, 2).reshape(n_oct, L, C, LANES)
    to_rows = lambda t: octs(t).transpose(3, 0, 1, 4, 2).reshape(n_oct, L * C, GROUPS_PER_TILE * N)
    to_cols = lambda t: octs(t).transpose(3, 4, 2, 0, 1).reshape(n_oct, GROUPS_PER_TILE * N, L * C)
    to_vec = lambda t: octs(t).transpose(1, 2, 0).reshape(n_oct, 1, GROUPS_PER_TILE * N)
    return dict(
        lag=to_lag(lag_k).astype(BF16),
        bc_re=to_rows(bc_re).astype(BF16), bc_im=to_rows(bc_im).astype(BF16),
        oc_re=to_cols(o_re).astype(BF16), oc_im_neg=to_cols(-o_im).astype(BF16),
        al_re=to_vec(p_re[L]), al_im=to_vec(p_im[L]), d=d_skip.astype(F32).reshape(n_oct, 1, LANES),
    )


def _ssm_expanders():
    L, C, N, G8 = SSM_CHUNK, SSM_GROUP, SSM_STATE, GROUPS_PER_TILE
    wide = np.arange(L * LANES)
    w_step, w_group, w_chan = wide // LANES, (wide // C) % G8, wide % C
    small = np.arange(L * C)
    s_step, s_chan = small // C, small % C
    state_group = np.arange(G8 * N) // N
    spread = (w_step[:, None] == s_step[None, :]) & (w_chan[:, None] == s_chan[None, :])
    lane = np.arange(LANES)
    as_bf16 = lambda a: jnp.asarray(a.astype(np.float32), dtype=BF16)
    return dict(
        spread_cols=as_bf16(spread.T),
        keep_cols=as_bf16(state_group[:, None] == w_group[None, :]),
        spread_lag=as_bf16((lane % C)[:, None] == np.arange(C)[None, :]),
        keep_lag=jnp.asarray(((lane // C)[:, None] == (lane // C)[None, :]).astype(np.float32)),
    )


def _ssm_kernel(u_ref, lag_ref, bxr_ref, bxi_ref, cxr_ref, cxi_ref, alr_ref, ali_ref, d_ref,
                sc_ref, kc_ref, sl_ref, kl_ref, o_ref,
                ub_ref, bcr_ref, bci_ref, ocr_ref, oci_ref, zr_ref, zi_ref, hr_ref, hi_ref, y_ref, *, bsz):
    L = SSM_CHUNK
    rows = u_ref.shape[0] // L
    chunks = rows // bsz
    tile = 2 * LANES
    step_rows = lambda r: pl.ds(r, rows, stride=L)
    for r in range(L):
        ub_ref[:, r * LANES:(r + 1) * LANES] = u_ref[step_rows(r), :].astype(BF16)
    state_group = lax.broadcasted_iota(jnp.int32, (1, bxr_ref.shape[1]), 1) // SSM_STATE
    for compact_ref, full_ref in ((bxr_ref, bcr_ref), (bxi_ref, bci_ref)):
        for r in range(L):
            blk = compact_ref[r * SSM_GROUP:(r + 1) * SSM_GROUP, :]
            for grp in range(GROUPS_PER_TILE):
                first = r * LANES + grp * SSM_GROUP
                full_ref[first:first + SSM_GROUP, :] = jnp.where(state_group == grp, blk, jnp.zeros_like(blk))
    zr_ref[...] = _dot(ub_ref[...], bcr_ref[...])
    zi_ref[...] = _dot(ub_ref[...], bci_ref[...])
    a_re, a_im = alr_ref[...], ali_ref[...]

    def step(k, carry):
        new = []
        for b in range(bsz):
            h_re, h_im = carry[2 * b], carry[2 * b + 1]
            row = pl.ds(b * chunks + k, 1)
            hr_ref[row, :] = h_re
            hi_ref[row, :] = h_im
            new.append(a_re * h_re - a_im * h_im + zr_ref[row, :])
            new.append(a_re * h_im + a_im * h_re + zi_ref[row, :])
        return tuple(new)

    zero = jnp.zeros((1, a_re.shape[1]), F32)
    lax.fori_loop(0, chunks, step, (zero,) * (2 * bsz))

    ocr_ref[...] = (_dot(cxr_ref[...], sc_ref[...]) * kc_ref[...]).astype(BF16)
    oci_ref[...] = (_dot(cxi_ref[...], sc_ref[...]) * kc_ref[...]).astype(BF16)
    lag = [(_dot(sl_ref[...], lag_ref[d]) * kl_ref[...]).astype(BF16) for d in range(L)]
    lag_tile = []
    for dd in range(L // 2):
        below = lag[2 * dd - 1] if dd > 0 else jnp.zeros((LANES, LANES), BF16)
        lag_tile.append(jnp.concatenate([jnp.concatenate([lag[2 * dd], lag[2 * dd + 1]], axis=1),
                                         jnp.concatenate([below, lag[2 * dd]], axis=1)], axis=0))
    hb_re, hb_im = hr_ref[...].astype(BF16), hi_ref[...].astype(BF16)
    for t2 in range(L // 2):
        cols = slice(t2 * tile, (t2 + 1) * tile)
        y = _dot(hb_re, ocr_ref[:, cols]) + _dot(hb_im, oci_ref[:, cols])
        for r2 in range(t2 + 1):
            y = y + _dot(ub_ref[:, r2 * tile:(r2 + 1) * tile], lag_tile[t2 - r2])
        for half in range(2):
            t = 2 * t2 + half
            y_ref[step_rows(t), :] = jax.nn.gelu(y[:, half * LANES:(half + 1) * LANES] + d_ref[...] * u_ref[step_rows(t), :])

    def chunk(rows):
        o_ref[rows, :] = y_ref[rows, :].astype(o_ref.dtype)
    _for_row_chunks(o_ref.shape[0], chunk)


def _ssm_core(proj, ops, bsz, seq):
    m = proj.shape[0]
    width = proj.shape[1] // 2
    L, C = SSM_CHUNK, SSM_GROUP
    n_oct = width // LANES
    rows = m // L
    ow = L * LANES
    n_state = GROUPS_PER_TILE * SSM_STATE
    ex = _ssm_expanders()
    per_oct = lambda *shape: pl.BlockSpec((None,) + shape, lambda p: (p,) + (0,) * len(shape))
    const = lambda a: pl.BlockSpec(a.shape, lambda p: (0,) * a.ndim, pipeline_mode=pl.Buffered(1))
    consts = [ex['spread_cols'], ex['keep_cols'], ex['spread_lag'], ex['keep_lag']]
    return pl.pallas_call(
        functools.partial(_ssm_kernel, bsz=bsz),
        grid=(n_oct,),
        in_specs=[
            pl.BlockSpec((m, LANES), lambda p: (0, p)),
            per_oct(L, C, LANES), per_oct(L * C, n_state), per_oct(L * C, n_state),
            per_oct(n_state, L * C), per_oct(n_state, L * C), per_oct(1, n_state), per_oct(1, n_state), per_oct(1, LANES),
        ] + [const(a) for a in consts],
        out_specs=pl.BlockSpec((m, LANES), lambda p: (0, p)),
        out_shape=jax.ShapeDtypeStruct((m, width), BF16),
        scratch_shapes=[pltpu.VMEM((rows, ow), BF16)] + [pltpu.VMEM((ow, n_state), BF16)] * 2
        + [pltpu.VMEM((n_state, ow), BF16)] * 2 + [pltpu.VMEM((rows, n_state), F32)] * 4 + [pltpu.VMEM((m, LANES), F32)],
        compiler_params=_params(1),
        name="ssm_core",
    )(proj, ops['lag'], ops['bc_re'], ops['bc_im'], ops['oc_re'], ops['oc_im_neg'], ops['al_re'], ops['al_im'], ops['d'], *consts)


def _glu_kernel(y_ref, wa_ref, wb_ref, ba_ref, bb_ref, gate_ref, o_ref):
    y = y_ref[...]
    ga = _dot(y, wa_ref[...]) + ba_ref[...]
    gb = _dot(y, wb_ref[...]) + bb_ref[...]
    o_ref[...] = ((ga * jax.nn.sigmoid(gb)) * _silu(gate_ref[...])).astype(o_ref.dtype)


def _glu(y, w_glu, layer, b_glu, proj, tm=1024, tn=512):
    m, k = y.shape
    width = w_glu.shape[2] // 2
    tm, tn = min(tm, m), min(tn, width)
    nt = width // tn
    w_tiles = _tile_cols(w_glu, layer, tn)
    return pl.pallas_call(
        _glu_kernel,
        grid=(m // tm, nt),
        in_specs=[
            pl.BlockSpec((tm, k), lambda i, j: (i, 0)),
            pl.BlockSpec((None, k, tn), lambda i, j: (j, 0, 0)),
            pl.BlockSpec((None, k, tn), lambda i, j: (nt + j, 0, 0)),
            pl.BlockSpec((1, tn), lambda i, j: (0, j)),
            pl.BlockSpec((1, tn), lambda i, j: (0, nt + j)),
            pl.BlockSpec((tm, tn), lambda i, j: (i, nt + j)),
        ],
        out_specs=pl.BlockSpec((tm, tn), lambda i, j: (i, j)),
        out_shape=jax.ShapeDtypeStruct((m, width), BF16),
        compiler_params=_params(2),
        name="ssm_glu",
    )(y, w_tiles, w_tiles, b_glu, b_glu, proj)


def _ssm_mixer(x, g, w_in, layer, lam_re, lam_im, log_dt, b_re, b_im, c_re, c_im, d_skip, w_glu, b_glu, bsz, seq):
    proj = _norm_matmul(x, g, _tile_cols(w_in, layer, MATMUL_TN), F32)
    ops = _ssm_operators(lam_re, lam_im, log_dt, b_re, b_im, c_re, c_im, d_skip)
    y = _ssm_core(proj, ops, bsz, seq)
    return _glu(y, w_glu, layer, b_glu.astype(F32).reshape(1, -1), proj)


CUM_BLOCK = 128


def _split3(x):
    x1 = x.astype(BF16)
    r1 = x - x1.astype(F32)
    x2 = r1.astype(BF16)
    x3 = (r1 - x2.astype(F32)).astype(BF16)
    return x1, x2, x3


def _forget_cumsum_kernel(z_ref, b_ref, o_ref):
    n_blocks = z_ref.shape[0] // CUM_BLOCK
    ri = lax.broadcasted_iota(jnp.int32, (CUM_BLOCK, CUM_BLOCK), 0)
    ci = lax.broadcasted_iota(jnp.int32, (CUM_BLOCK, CUM_BLOCK), 1)
    tri = jnp.where(ci <= ri, 1.0, 0.0).astype(BF16)

    def body(i, carry):
        rows = pl.ds(pl.multiple_of(i * CUM_BLOCK, CUM_BLOCK), CUM_BLOCK)
        z = z_ref[rows, :] + b_ref[...]
        log_f = jnp.minimum(z, 0.0) - jnp.log1p(jnp.exp(-jnp.abs(z)))
        x1, x2, x3 = _split3(log_f)
        c = _dot(tri, x1) + _dot(tri, x2) + _dot(tri, x3) + carry
        o_ref[rows, :] = c
        return c[CUM_BLOCK - 1:, :]

    lax.fori_loop(0, n_blocks, body, jnp.zeros((1, z_ref.shape[1]), F32))


def _forget_cumsum(z, b, bsz, seq):
    lanes = z.shape[1]
    return pl.pallas_call(
        _forget_cumsum_kernel,
        grid=(bsz,),
        in_specs=[pl.BlockSpec((seq, lanes), lambda i: (i, 0)), pl.BlockSpec((1, lanes), lambda i: (0, 0))],
        out_specs=pl.BlockSpec((seq, lanes), lambda i: (i, 0)),
        out_shape=jax.ShapeDtypeStruct(z.shape, F32),
        compiler_params=_params(1),
        name="forget_cumsum",
    )(z, b)


FOX_STEP_HEADS = 8


def _fox_kernel(q_ref, k_ref, v_ref, sg_ref, cq_ref, ck_ref, o_ref, ka_ref, vt_ref, *head_refs, tk):
    nh = FOX_STEP_HEADS
    hg, qi = pl.program_id(1), pl.program_id(2)
    tq = q_ref.shape[0]
    qa_ref, s_ref, p_ref, acc_ref, pv_ref, m_ref = (head_refs[i::6] for i in range(6))
    lane = lax.broadcasted_iota(jnp.int32, (1, LANES), 1)
    free = (HEAD_DIM, 0)
    n_pieces = 3
    pair_lanes = lambda h: slice(h // 2 * LANES, (h // 2 + 1) * LANES)

    def bias_lanes(c, h, c_offset, one_offset):
        rr = lax.broadcasted_iota(jnp.int32, (n_pieces * LANES, LANES), 0)
        cc = lax.broadcasted_iota(jnp.int32, (n_pieces * LANES, LANES), 1)
        place = jnp.logical_and(rr % LANES == nh * hg + h, cc == free[h % 2] + c_offset + rr // LANES)
        placed = _dot(jnp.concatenate(_split3(c), axis=1), jnp.where(place, 1.0, 0.0).astype(BF16))
        first = free[h % 2] + one_offset
        return (placed + jnp.where(jnp.logical_and(lane >= first, lane < first + n_pieces), 1.0, 0.0)).astype(BF16)

    @pl.when(qi == 0)
    def _():
        def chunk(j, carry):
            rows = pl.ds(pl.multiple_of(j * tk, tk), tk)
            neg_c = ck_ref[rows, :] * -LOG2E
            for h in range(nh):
                keys = k_ref[rows, pair_lanes(h)]
                ka_ref[h, j] = jnp.where((lane // HEAD_DIM) == h % 2, keys, bias_lanes(neg_c, h, 0, n_pieces))
            for pair in range(nh // 2):
                v_t = v_ref[rows, pair_lanes(2 * pair)].astype(F32).T.astype(BF16)
                for hh in range(2):
                    vt_ref[j, 2 * pair + hh, :HEAD_DIM, :] = v_t[hh * HEAD_DIM:(hh + 1) * HEAD_DIM]
                    vt_ref[j, 2 * pair + hh, HEAD_DIM:, :] = jnp.ones((ONES_ROWS, tk), BF16)
            return carry
        lax.fori_loop(0, k_ref.shape[0] // tk, chunk, 0)

    cq = cq_ref[...] * LOG2E
    for h in range(nh):
        qa = jnp.where((lane // HEAD_DIM) == h % 2, q_ref[:, pair_lanes(h)], bias_lanes(cq, h, n_pieces, 0))
        qa_ref[h][...] = qa.astype(F32).T.astype(BF16)

    def scores(h, j):
        s_ref[h][...] = _dot(ka_ref[h, j], qa_ref[h][...])

    rows8 = 8
    rows16 = 16

    def absorb(h, j, diagonal):
        def strip(r, n):
            blk = s_ref[h][r:r + n, :]
            if diagonal:
                kpos = lax.broadcasted_iota(jnp.int32, (n, tq), 0)
                qpos = lax.broadcasted_iota(jnp.int32, (n, tq), 1)
                blk = jnp.where(kpos + r <= qpos, blk, NEG)
            return blk
        top = strip(0, rows8)
        for r in range(rows8, tk, rows8):
            top = jnp.maximum(top, strip(r, rows8))
        m = m_ref[h][...]
        m_new = jnp.maximum(m, top.max(axis=0, keepdims=True))
        m_ref[h][...] = m_new
        alpha = jnp.exp2(m - m_new)
        m_rows = jnp.broadcast_to(m_new, (rows16, tq))
        for r in range(0, tk, rows16):
            p_ref[h][r:r + rows16, :] = jnp.exp2(strip(r, rows16) - m_rows).astype(BF16)
        acc_ref[h][...] = alpha * (acc_ref[h][...] + pv_ref[h][...])

    def values(h, j):
        pv_ref[h][...] = _dot(vt_ref[j, h], p_ref[h][...])

    def step(j, diagonal):
        values(nh - 1, jnp.maximum(j - 1, 0))
        scores(1, j)
        for h in range(nh):
            absorb(h, j, diagonal)
            if h < nh - 1:
                values(h, j)
            if h + 2 < nh:
                scores(h + 2, j)
            elif h + 2 == nh and not diagonal:
                scores(0, j + 1)

    for h in range(nh):
        acc_ref[h][...] = jnp.zeros(acc_ref[h].shape, F32)
        pv_ref[h][...] = jnp.zeros(pv_ref[h].shape, F32)
        m_ref[h][...] = jnp.full(m_ref[h].shape, NEG, F32)
    p_ref[nh - 1][...] = jnp.zeros(p_ref[nh - 1].shape, BF16)
    scores(0, 0)

    def two_blocks(i, carry):
        step(2 * i, False)
        step(2 * i + 1, False)
        return carry
    lax.fori_loop(0, qi // 2, two_blocks, 0)

    @pl.when(qi % 2 == 0)
    def _():
        step(qi, True)

    @pl.when(qi % 2 == 1)
    def _():
        step(qi - 1, False)
        step(qi, True)

    values(nh - 1, qi)
    outs = []
    for h in range(nh):
        total = acc_ref[h][...] + pv_ref[h][...]
        outs.append(total[:HEAD_DIM] / total[HEAD_DIM:HEAD_DIM + 1])
    out = jnp.concatenate(outs, axis=0)
    o_ref[...] = (out.T * sg_ref[...].astype(F32)).astype(o_ref.dtype)


def _fox_attention(proj, csum, bsz, seq, t=512):
    width = FOX_HEADS * HEAD_DIM
    nh = FOX_STEP_HEADS
    gw = nh * HEAD_DIM
    n_groups = width // gw
    t = min(t, seq)
    nq = seq // t
    return pl.pallas_call(
        functools.partial(_fox_kernel, tk=t),
        grid=(bsz, n_groups, nq),
        in_specs=[
            pl.BlockSpec((t, gw), lambda b, h, i: (b * nq + i, h)),
            pl.BlockSpec((seq, gw), lambda b, h, i: (b, n_groups + h), pipeline_mode=pl.Buffered(1)),
            pl.BlockSpec((seq, gw), lambda b, h, i: (b, 2 * n_groups + h), pipeline_mode=pl.Buffered(1)),
            pl.BlockSpec((t, gw), lambda b, h, i: (b * nq + i, 3 * n_groups + h)),
            pl.BlockSpec((t, LANES), lambda b, h, i: (b * nq + i, 0)),
            pl.BlockSpec((seq, LANES), lambda b, h, i: (b, 0), pipeline_mode=pl.Buffered(1)),
        ],
        out_specs=pl.BlockSpec((t, gw), lambda b, h, i: (b * nq + i, h)),
        out_shape=jax.ShapeDtypeStruct((bsz * seq, width), BF16),
        scratch_shapes=[
            pltpu.VMEM((nh, nq, t, LANES), BF16), pltpu.VMEM((nq, nh, HEAD_DIM + ONES_ROWS, t), BF16),
        ] + nh * [pltpu.VMEM((LANES, t), BF16), pltpu.VMEM((t, t), F32), pltpu.VMEM((t, t), BF16),
                  pltpu.VMEM((HEAD_DIM + ONES_ROWS, t), F32), pltpu.VMEM((HEAD_DIM + ONES_ROWS, t), F32),
                  pltpu.VMEM((1, t), F32)],
        compiler_params=_params(3, VMEM_LIMIT_BIG),
        name="fox_attention",
    )(proj, proj, proj, proj, csum, csum)


def _fox_mixer(x, g, w_in, layer, w_fg, b_fg, bsz, seq, t=512):
    width = FOX_HEADS * HEAD_DIM
    col_scale = jnp.where(jnp.arange(4 * width) < width, LOG2E * HEAD_DIM ** -0.5, 1.0).astype(F32)
    proj = _norm_matmul(x, g, _tile_cols(w_in, layer, MATMUL_TN, col_scale), BF16, silu_cols=(3 * width, 4 * width))
    pad = LANES - FOX_HEADS
    z = _norm_matmul(x, g, _tile_cols_xla(jnp.pad(w_fg, ((0, 0), (0, pad))), LANES), F32)
    csum = _forget_cumsum(z, jnp.pad(b_fg.astype(F32), (0, pad)).reshape(1, LANES), bsz, seq)
    return _fox_attention(proj, csum, bsz, seq, t)


def kernel(x, p, norm_g, final_g, rel_bias, swa_w_in, swa_w_out, swa_sinks, conv_w_in, conv_kernel, conv_w_out, ssm_w_in, ssm_lam_re, ssm_lam_im, ssm_log_dt, ssm_b_re, ssm_b_im, ssm_c_re, ssm_c_im, ssm_d, ssm_w_glu, ssm_b_glu, ssm_w_out, fox_w_in, fox_w_fg, fox_b_fg, fox_w_out, ple_proj, ple_norm, ple_gate):
    bsz, seq, d_model = x.shape
    depth = p.shape[0]
    h = x.astype(F32).reshape(bsz * seq, d_model)
    for i in range(depth):
        mixer, j = i % N_MIXERS, i // N_MIXERS
        if mixer == 0:
            a, w_out = _swa_mixer(h, norm_g[i], swa_w_in, j, swa_sinks[j], rel_bias, bsz, seq), swa_w_out[j]
        elif mixer == 1:
            a, w_out = _conv_mixer(h, norm_g[i], conv_w_in, j, conv_kernel[j], seq), conv_w_out[j]
        elif mixer == 2:
            a = _ssm_mixer(h, norm_g[i], ssm_w_in, j, ssm_lam_re[j], ssm_lam_im[j], ssm_log_dt[j], ssm_b_re[j], ssm_b_im[j],
                           ssm_c_re[j], ssm_c_im[j], ssm_d[j], ssm_w_glu, ssm_b_glu[j], bsz, seq)
            w_out = ssm_w_out[j]
        else:
            a, w_out = _fox_mixer(h, norm_g[i], fox_w_in, j, fox_w_fg[j], fox_b_fg[j], bsz, seq), fox_w_out[j]
        h = _out_ple(a, w_out, h, ple_norm[i], ple_gate, p.reshape(depth, bsz * seq, -1), ple_proj, i,
                     final_g=final_g if i == depth - 1 else None)
    return h.reshape(bsz, seq, d_model).astype(x.dtype)
```

```python
import functools
import math

import numpy as np
import jax
import jax.numpy as jnp
from jax import lax
from jax.experimental import pallas as pl
from jax.experimental.pallas import tpu as pltpu

F32 = jnp.float32
BF16 = jnp.bfloat16

EPS = 1e-6
N_MIXERS = 4
PLE_DIM = 256

SWA_HEADS = 32
SWA_KV_HEADS = 4
SWA_GROUP = SWA_HEADS // SWA_KV_HEADS
HEAD_DIM = 64
SWA_BLOCK = 128
WINDOW = 128
REL_BUCKETS = 32
REL_MAX_DIST = 128

CONV_TAPS = 3

SSM_GROUP = 16
SSM_STATE = 64
SSM_CHUNK = 16

FOX_HEADS = 32

LANES = 128
ONES_ROWS = 16
GROUPS_PER_TILE = LANES // SSM_GROUP
VMEM_LIMIT = 48 * 1024 * 1024
VMEM_LIMIT_BIG = 60 * 1024 * 1024

NEG = float(jnp.finfo(jnp.float32).min)
LOG2E = math.log2(math.e)


def _params(n_axes, vmem_limit=VMEM_LIMIT):
    return pltpu.CompilerParams(dimension_semantics=("arbitrary",) * n_axes, vmem_limit_bytes=vmem_limit)


def _dot(a, b):
    return jnp.dot(a, b, preferred_element_type=F32)


def _dot_nt(a, b):
    return lax.dot_general(a, b, (((1,), (1,)), ((), ())), preferred_element_type=F32)


def _rmsnorm_rows(x, g):
    return x * lax.rsqrt(jnp.mean(x * x, axis=-1, keepdims=True) + EPS) * g


def _silu(x):
    return x * jax.nn.sigmoid(x)


ROW_CHUNK = 64


def _for_row_chunks(n_rows, fn):
    def body(c, carry):
        fn(pl.ds(pl.multiple_of(c * ROW_CHUNK, ROW_CHUNK), ROW_CHUNK))
        return carry
    lax.fori_loop(0, n_rows // ROW_CHUNK, body, 0)


def _tile_cols_xla(w, tn):
    k, n = w.shape
    return w.astype(BF16).reshape(k, n // tn, tn).transpose(1, 0, 2)


def _cast_tiles_kernel(w_ref, o_ref):
    def chunk(rows):
        o_ref[rows, :] = w_ref[rows, :].astype(BF16)
    _for_row_chunks(o_ref.shape[0], chunk)


def _scale_cast_tiles_kernel(w_ref, s_ref, o_ref):
    def chunk(rows):
        o_ref[rows, :] = (w_ref[rows, :] * s_ref[...]).astype(BF16)
    _for_row_chunks(o_ref.shape[0], chunk)


def _tile_cols(w, layer, tn, col_scale=None):
    _, k, n = w.shape
    tn = min(tn, n)
    w_spec = pl.BlockSpec((None, k, tn), lambda j: (layer, 0, j))
    common = dict(
        grid=(n // tn,),
        out_specs=pl.BlockSpec((None, k, tn), lambda j: (j, 0, 0)),
        out_shape=jax.ShapeDtypeStruct((n // tn, k, tn), BF16),
        compiler_params=_params(1),
        name="weight_tiles",
    )
    if col_scale is None:
        return pl.pallas_call(_cast_tiles_kernel, in_specs=[w_spec], **common)(w)
    s_spec = pl.BlockSpec((1, tn), lambda j: (0, j))
    return pl.pallas_call(_scale_cast_tiles_kernel, in_specs=[w_spec, s_spec], **common)(w, col_scale.reshape(1, n))


def _norm_matmul_kernel(x_ref, g_ref, w_ref, o_ref, a_ref, *, silu_tiles):
    j = pl.program_id(1)

    @pl.when(j == 0)
    def _():
        def chunk(rows):
            a_ref[rows, :] = _rmsnorm_rows(x_ref[rows, :], g_ref[...]).astype(BF16)
        _for_row_chunks(a_ref.shape[0], chunk)

    acc = _dot(a_ref[...], w_ref[...])
    if silu_tiles is None:
        o_ref[...] = acc.astype(o_ref.dtype)
    else:
        gated = jnp.logical_and(j >= silu_tiles[0], j < silu_tiles[1])

        @pl.when(gated)
        def _():
            o_ref[...] = _silu(acc).astype(o_ref.dtype)

        @pl.when(jnp.logical_not(gated))
        def _():
            o_ref[...] = acc.astype(o_ref.dtype)


MATMUL_TN = 1024


def _norm_matmul(x, g, w_tiles, out_dtype, silu_cols=None, tm=1024):
    m, k = x.shape
    tn = w_tiles.shape[2]
    n = w_tiles.shape[0] * tn
    tm = min(tm, m)
    silu_tiles = None if silu_cols is None else (silu_cols[0] // tn, silu_cols[1] // tn)
    return pl.pallas_call(
        functools.partial(_norm_matmul_kernel, silu_tiles=silu_tiles),
        grid=(m // tm, n // tn),
        in_specs=[
            pl.BlockSpec((tm, k), lambda i, j: (i, 0)),
            pl.BlockSpec((1, k), lambda i, j: (0, 0)),
            pl.BlockSpec((None, k, tn), lambda i, j: (j, 0, 0)),
        ],
        out_specs=pl.BlockSpec((tm, tn), lambda i, j: (i, j)),
        out_shape=jax.ShapeDtypeStruct((m, n), out_dtype),
        scratch_shapes=[pltpu.VMEM((tm, k), BF16)],
        compiler_params=_params(2),
        name="norm_matmul",
    )(x, g.reshape(1, k), w_tiles)


def _out_ple_kernel(a_ref, wo_ref, x_ref, g_ref, wg_ref, p_ref, wp_ref, gf_ref, o_ref, x1_ref, hn_ref, ss_ref, *, nt, final):
    j = pl.program_id(1)
    tn = x_ref.shape[1]
    n = nt * tn

    @pl.when(j == 0)
    def _():
        ss_ref[...] = jnp.zeros(ss_ref.shape, F32)

    @pl.when(j < nt)
    def _():
        x1 = x_ref[...] + _dot(a_ref[...], wo_ref[j])
        x1_ref[j] = x1
        hn_ref[j] = (x1 * g_ref[j]).astype(BF16)
        ss_ref[0] += jnp.sum(x1 * x1, axis=-1, keepdims=True)

    @pl.when(jnp.logical_and(j >= nt, j < 2 * nt))
    def _():
        t = j - nt
        emb = _dot(p_ref[...].astype(BF16), wp_ref[t])
        acc = _dot(hn_ref[0], wg_ref[t, :tn, :])
        for kt in range(1, nt):
            acc = acc + _dot(hn_ref[kt], wg_ref[t, kt * tn:(kt + 1) * tn, :])
        x2 = x1_ref[t] + emb * jax.nn.sigmoid(acc * lax.rsqrt(ss_ref[0] / n + EPS))
        if final:
            x1_ref[t] = x2
            ss_ref[1] += jnp.sum(x2 * x2, axis=-1, keepdims=True)
        else:
            o_ref[...] = x2

    if final:
        @pl.when(j >= 2 * nt)
        def _():
            t = j - 2 * nt
            o_ref[...] = x1_ref[t] * lax.rsqrt(ss_ref[1] / n + EPS) * gf_ref[t]


def _out_ple(a, w_out, x, g, w_gate, p, w_proj, layer, final_g=None, tm=1024, tn=512):
    m, k = a.shape
    n = w_out.shape[1]
    pd = p.shape[2]
    final = final_g is not None
    tm, tn = min(tm, m), min(tn, n)
    nt = n // tn
    passes = 3 if final else 2
    out_tile = lambda j: jnp.maximum(j - (passes - 1) * nt, 0)
    resident = lambda rows: pl.BlockSpec((nt, rows, tn), lambda i, j: (0, 0, 0), pipeline_mode=pl.Buffered(1))
    row_vec = pl.BlockSpec((nt, 1, tn), lambda i, j: (0, 0, 0))
    return pl.pallas_call(
        functools.partial(_out_ple_kernel, nt=nt, final=final),
        grid=(m // tm, passes * nt),
        in_specs=[
            pl.BlockSpec((tm, k), lambda i, j: (i, 0)),
            resident(k),
            pl.BlockSpec((tm, tn), lambda i, j: (i, jnp.minimum(j, nt - 1))),
            row_vec,
            resident(n),
            pl.BlockSpec((None, tm, pd), lambda i, j: (layer, i, 0)),
            resident(pd),
            row_vec,
        ],
        out_specs=pl.BlockSpec((tm, tn), lambda i, j: (i, out_tile(j))),
        out_shape=jax.ShapeDtypeStruct((m, n), F32),
        scratch_shapes=[pltpu.VMEM((nt, tm, tn), F32), pltpu.VMEM((nt, tm, tn), BF16), pltpu.VMEM((2, tm, 1), F32)],
        compiler_params=_params(2, VMEM_LIMIT_BIG),
        name="out_ple_final" if final else "out_ple",
    )(a, _tile_cols(w_out[None], 0, tn), x, g.reshape(nt, 1, tn), _tile_cols(w_gate, layer, tn), p,
      _tile_cols(w_proj, layer, tn), (final_g if final else g).reshape(nt, 1, tn))


def _t5_bucket(dist):
    max_exact = REL_BUCKETS // 2
    d = np.maximum(dist, 1).astype(np.float32)
    large = max_exact + (np.log(d / max_exact) / np.log(REL_MAX_DIST / max_exact) * (REL_BUCKETS - max_exact)).astype(np.int32)
    large = np.minimum(large, REL_BUCKETS - 1)
    return np.where(dist < max_exact, dist, large).astype(np.int32)


def _swa_bucket_table():
    qi = np.arange(SWA_BLOCK)[None, :]
    kj = np.arange(2 * SWA_BLOCK)[:, None]
    return _t5_bucket(np.clip(qi + SWA_BLOCK - kj, 0, None))


def _swa_kernel(sink_ref, rel_ref, bucket_ref, q_ref, sg_ref, kp_ref, kc_ref, vp_ref, vc_ref, o_ref, bias_ref, qs_ref):
    n = pl.program_id(1)
    blk = SWA_BLOCK
    lane = lax.broadcasted_iota(jnp.int32, (1, LANES), 1)

    @pl.when(jnp.logical_and(pl.program_id(0) == 0, n == 0))
    def _():
        bucket = bucket_ref[...]
        kj = lax.broadcasted_iota(jnp.int32, (2 * blk, blk), 0)
        qi = lax.broadcasted_iota(jnp.int32, (2 * blk, blk), 1)
        band = jnp.logical_or(jnp.logical_and(kj < blk, kj > qi), jnp.logical_and(kj >= blk, kj - blk <= qi))

        def per_head(head, carry):
            acc = jnp.zeros((2 * blk, blk), F32)
            for b in range(REL_BUCKETS):
                acc = jnp.where(bucket == b, rel_ref[b, head] * LOG2E, acc)
            bias_ref[head] = jnp.where(band, acc, NEG)
            return carry
        lax.fori_loop(0, SWA_HEADS, per_head, 0)

    for kvh in range(SWA_KV_HEADS):
        cols = slice(kvh * LANES, (kvh + 1) * LANES)
        heads = range(kvh * SWA_GROUP, (kvh + 1) * SWA_GROUP)
        for g, head in enumerate(heads):
            q_pair = q_ref[:, head // 2 * LANES:(head // 2 + 1) * LANES]
            qs_ref[g * blk:(g + 1) * blk, :] = jnp.where((lane // HEAD_DIM) == head % 2, q_pair, jnp.zeros_like(q_pair))
        keys = jnp.concatenate([kp_ref[:, cols], kc_ref[:, cols]], axis=0)
        vals = jnp.concatenate([vp_ref[:, cols], vc_ref[:, cols]], axis=0)
        s = _dot_nt(keys, qs_ref[...]) + jnp.concatenate([bias_ref[head] for head in heads], axis=1)
        s_prev = jnp.where(n > 0, s[:blk], NEG)
        s_cur = s[blk:]
        sink = jnp.concatenate([jnp.full((1, blk), sink_ref[head] * LOG2E, F32) for head in heads], axis=1)
        m = jnp.maximum(jnp.maximum(s_prev, s_cur).max(axis=0, keepdims=True), sink)
        e = jnp.concatenate([jnp.exp2(s_prev - m).astype(BF16), jnp.exp2(s_cur - m).astype(BF16)], axis=0)
        v_t = jnp.concatenate([vals.astype(F32).T.astype(BF16)[:HEAD_DIM], jnp.ones((ONES_ROWS, 2 * blk), BF16)], axis=0)
        out = _dot(v_t, e)
        out = out[:HEAD_DIM] / (out[HEAD_DIM:HEAD_DIM + 1] + jnp.exp2(sink - m))
        for pair in range(SWA_GROUP // 2):
            head = kvh * SWA_GROUP + 2 * pair
            even = out[:, 2 * pair * blk:(2 * pair + 1) * blk]
            odd = out[:, (2 * pair + 1) * blk:(2 * pair + 2) * blk]
            cols = slice(head // 2 * LANES, (head // 2 + 1) * LANES)
            o_ref[:, cols] = (jnp.concatenate([even, odd], axis=0).T * sg_ref[:, cols].astype(F32)).astype(o_ref.dtype)


def _swa_attention(proj, sinks, rel_bias, bsz, seq):
    nb = seq // SWA_BLOCK
    width = SWA_HEADS * HEAD_DIM
    kvw = SWA_KV_HEADS * LANES
    k_blk = 2 * width // kvw
    cur = lambda b, n: b * nb + n
    prev = lambda b, n: b * nb + jnp.maximum(n - 1, 0)
    smem = pl.BlockSpec(memory_space=pltpu.SMEM)
    return pl.pallas_call(
        _swa_kernel,
        grid=(bsz, nb),
        in_specs=[
            smem, smem,
            pl.BlockSpec((2 * SWA_BLOCK, SWA_BLOCK), lambda b, n: (0, 0)),
            pl.BlockSpec((SWA_BLOCK, width), lambda b, n: (cur(b, n), 0)),
            pl.BlockSpec((SWA_BLOCK, width), lambda b, n: (cur(b, n), 1)),
            pl.BlockSpec((SWA_BLOCK, kvw), lambda b, n: (prev(b, n), k_blk)),
            pl.BlockSpec((SWA_BLOCK, kvw), lambda b, n: (cur(b, n), k_blk)),
            pl.BlockSpec((SWA_BLOCK, kvw), lambda b, n: (prev(b, n), k_blk + 1)),
            pl.BlockSpec((SWA_BLOCK, kvw), lambda b, n: (cur(b, n), k_blk + 1)),
        ],
        out_specs=pl.BlockSpec((SWA_BLOCK, width), lambda b, n: (cur(b, n), 0)),
        out_shape=jax.ShapeDtypeStruct((bsz * seq, width), BF16),
        scratch_shapes=[
            pltpu.VMEM((SWA_HEADS, 2 * SWA_BLOCK, SWA_BLOCK), F32),
            pltpu.VMEM((SWA_GROUP * SWA_BLOCK, LANES), BF16),
        ],
        compiler_params=_params(2),
        name="swa_attention",
    )(sinks.astype(F32), rel_bias.astype(F32), jnp.asarray(_swa_bucket_table()), proj, proj, proj, proj, proj, proj)


SWA_HALF_TILE = 512


def _swa_weight_tiles_kernel(wl_ref, wr_ref, o_ref, *, q_tiles, gate_tiles, q_scale):
    j = pl.program_id(0)
    half = SWA_HALF_TILE
    kvw = SWA_KV_HEADS * HEAD_DIM

    def rows_do(fn):
        def chunk(rows):
            for side, w_ref in enumerate((wl_ref, wr_ref)):
                o_ref[rows, side * half:(side + 1) * half] = fn(side, w_ref, rows)
        _for_row_chunks(o_ref.shape[0], chunk)

    @pl.when(j < q_tiles)
    def _():
        rows_do(lambda side, w_ref, rows: (w_ref[rows, :] * q_scale).astype(BF16))

    @pl.when(jnp.logical_and(j >= q_tiles, j < q_tiles + gate_tiles))
    def _():
        rows_do(lambda side, w_ref, rows: w_ref[rows, :].astype(BF16))

    @pl.when(j == q_tiles + gate_tiles)
    def _():
        def twice(side, w_ref, rows):
            w = w_ref[rows, side * kvw:(side + 1) * kvw].astype(BF16)
            heads = [w[:, h * HEAD_DIM:(h + 1) * HEAD_DIM] for h in range(SWA_KV_HEADS)]
            return jnp.concatenate([piece for head in heads for piece in (head, head)], axis=1)
        rows_do(twice)


def _swa_weight_tiles(w_in, layer, q_scale):
    _, k, n = w_in.shape
    half = SWA_HALF_TILE
    width = SWA_HEADS * HEAD_DIM
    q_tiles = gate_tiles = width // (2 * half)
    kv_block = width // half
    n_out = q_tiles + gate_tiles + 1

    def src(j, side):
        in_q, in_gate = j < q_tiles, j < q_tiles + gate_tiles
        return jnp.where(in_q, 2 * j + side, jnp.where(in_gate, 2 * j + side + 1, kv_block))
    return pl.pallas_call(
        functools.partial(_swa_weight_tiles_kernel, q_tiles=q_tiles, gate_tiles=gate_tiles, q_scale=q_scale),
        grid=(n_out,),
        in_specs=[pl.BlockSpec((None, k, half), lambda j: (layer, 0, src(j, 0))),
                  pl.BlockSpec((None, k, half), lambda j: (layer, 0, src(j, 1)))],
        out_specs=pl.BlockSpec((None, k, 2 * half), lambda j: (j, 0, 0)),
        out_shape=jax.ShapeDtypeStruct((n_out, k, 2 * half), BF16),
        compiler_params=_params(1),
        name="swa_weight_tiles",
    )(w_in, w_in)


def _swa_mixer(x, g, w_in, layer, sinks, rel_bias, bsz, seq):
    width = SWA_HEADS * HEAD_DIM
    w_tiles = _swa_weight_tiles(w_in, layer, LOG2E * HEAD_DIM ** -0.5)
    proj = _norm_matmul(x, g, w_tiles, BF16, silu_cols=(width, 2 * width))
    return _swa_attention(proj, sinks, rel_bias, bsz, seq)


HALO = 8


def _conv_kernel(x_ref, g_ref, wb_ref, wc_ref, wu_ref, wg_ref, ck_ref, o_ref, a_ref, z_ref, carry_ref, *, tiles_per_seq):
    i, j = pl.program_id(0), pl.program_id(1)
    tm = a_ref.shape[0]

    @pl.when(j == 0)
    def _():
        def chunk(rows):
            a_ref[rows, :] = _rmsnorm_rows(x_ref[rows, :], g_ref[...]).astype(BF16)
        _for_row_chunks(tm, chunk)

    a = a_ref[...]
    z = _dot(a, wc_ref[...]) * _dot(a, wu_ref[...])
    first = (i % tiles_per_seq) == 0

    @pl.when(first)
    def _():
        z_ref[:HALO, :] = jnp.zeros((HALO, z_ref.shape[1]), F32)

    @pl.when(jnp.logical_not(first))
    def _():
        z_ref[:HALO, :] = carry_ref[j]

    z_ref[HALO:, :] = z
    carry_ref[j] = z[tm - HALO:, :]
    conv = z_ref[HALO - 2:HALO - 2 + tm, :] * ck_ref[0:1, :]
    conv = conv + z_ref[HALO - 1:HALO - 1 + tm, :] * ck_ref[1:2, :]
    conv = conv + z * ck_ref[2:3, :]
    y = _dot(a, wb_ref[...]) * conv
    o_ref[...] = (y * _silu(_dot(a, wg_ref[...]))).astype(o_ref.dtype)


def _conv_mixer(x, g, w_in, layer, conv_kernel, seq, tm=1024, tn=512):
    m, k = x.shape
    width = w_in.shape[2] // 4
    tm, tn = min(tm, seq), min(tn, width)
    nt = width // tn
    w_spec = lambda q: pl.BlockSpec((None, k, tn), lambda i, j: (q * nt + j, 0, 0))
    w_in = _tile_cols(w_in, layer, tn)
    return pl.pallas_call(
        functools.partial(_conv_kernel, tiles_per_seq=seq // tm),
        grid=(m // tm, nt),
        in_specs=[
            pl.BlockSpec((tm, k), lambda i, j: (i, 0)),
            pl.BlockSpec((1, k), lambda i, j: (0, 0)),
            w_spec(0), w_spec(1), w_spec(2), w_spec(3),
            pl.BlockSpec((CONV_TAPS, tn), lambda i, j: (0, j)),
        ],
        out_specs=pl.BlockSpec((tm, tn), lambda i, j: (i, j)),
        out_shape=jax.ShapeDtypeStruct((m, width), BF16),
        scratch_shapes=[
            pltpu.VMEM((tm, k), BF16),
            pltpu.VMEM((HALO + tm, tn), F32),
            pltpu.VMEM((nt, HALO, tn), F32),
        ],
        compiler_params=_params(2, VMEM_LIMIT_BIG),
        name="conv_mixer",
    )(x, g.reshape(1, k), w_in, w_in, w_in, w_in, conv_kernel.astype(F32))


def _ssm_operators(lam_re, lam_im, log_dt, b_re, b_im, c_re, c_im, d_skip):
    n_groups = lam_re.shape[0]
    n_oct = n_groups // GROUPS_PER_TILE
    L, C, N = SSM_CHUNK, SSM_GROUP, SSM_STATE
    dt = jnp.exp(log_dt.astype(F32))[None, :]
    lr, li = lam_re.astype(F32).T, lam_im.astype(F32).T
    mag = jnp.exp(lr * dt)
    ab_re, ab_im = mag * jnp.cos(li * dt), mag * jnp.sin(li * dt)
    den = lr * lr + li * li
    nr = ab_re - 1.0
    coef_re = (nr * lr + ab_im * li) / den
    coef_im = (ab_im * lr - nr * li) / den
    br, bi = b_re.astype(F32).transpose(2, 1, 0), b_im.astype(F32).transpose(2, 1, 0)
    bb_re = coef_re * br - coef_im * bi
    bb_im = coef_re * bi + coef_im * br
    cr, ci = c_re.astype(F32).transpose(1, 2, 0), c_im.astype(F32).transpose(1, 2, 0)
    pw_re, pw_im = [jnp.ones_like(ab_re)], [jnp.zeros_like(ab_im)]
    for _ in range(L):
        pw_re, pw_im = (pw_re + [pw_re[-1] * ab_re - pw_im[-1] * ab_im], pw_im + [pw_re[-1] * ab_im + pw_im[-1] * ab_re])
    p_re, p_im = jnp.stack(pw_re), jnp.stack(pw_im)
    w_re = cr[None] * p_re[:L, None] - ci[None] * p_im[:L, None]
    w_im = cr[None] * p_im[:L, None] + ci[None] * p_re[:L, None]
    lag_k = jnp.sum(w_re[:, None] * bb_re[None, :, None] - w_im[:, None] * bb_im[None, :, None], axis=3)
    q_re, q_im = jnp.stack(pw_re[L - 1::-1]), jnp.stack(pw_im[L - 1::-1])
    bc_re = q_re[:, None] * bb_re[None] - q_im[:, None] * bb_im[None]
    bc_im = q_re[:, None] * bb_im[None] + q_im[:, None] * bb_re[None]
    o_re = cr[None] * p_re[1:, None] - ci[None] * p_im[1:, None]
    o_im = cr[None] * p_im[1:, None] + ci[None] * p_re[1:, None]
    octs = lambda t: t.reshape(t.shape[:-1] + (n_oct, GROUPS_PER_TILE))
    to_lag = lambda t: octs(t).transpose(3, 0, 1, 4, 2).reshape(n_oct, L, C, LANES)
    to_rows = lambda t: octs(t).transpose(3, 0, 1, 4, 2).reshape(n_oct, L * C, GROUPS_PER_TILE * N)
    to_cols = lambda t: octs(t).transpose(3, 4, 2, 0, 1).reshape(n_oct, GROUPS_PER_TILE * N, L * C)
    to_vec = lambda t: octs(t).transpose(1, 2, 0).reshape(n_oct, 1, GROUPS_PER_TILE * N)
    return dict(
        lag=to_lag(lag_k).astype(BF16),
        bc_re=to_rows(bc_re).astype(BF16), bc_im=to_rows(bc_im).astype(BF16),
        oc_re=to_cols(o_re).astype(BF16), oc_im_neg=to_cols(-o_im).astype(BF16),
        al_re=to_vec(p_re[L]), al_im=to_vec(p_im[L]), d=d_skip.astype(F32).reshape(n_oct, 1, LANES),
    )


def _ssm_expanders():
    L, C, N, G8 = SSM_CHUNK, SSM_GROUP, SSM_STATE, GROUPS_PER_TILE
    wide = np.arange(L * LANES)
    w_step, w_group, w_chan = wide // LANES, (wide // C) % G8, wide % C
    small = np.arange(L * C)
    s_step, s_chan = small // C, small % C
    state_group = np.arange(G8 * N) // N
    spread = (w_step[:, None] == s_step[None, :]) & (w_chan[:, None] == s_chan[None, :])
    lane = np.arange(LANES)
    as_bf16 = lambda a: jnp.asarray(a.astype(np.float32), dtype=BF16)
    return dict(
        spread_cols=as_bf16(spread.T),
        keep_cols=as_bf16(state_group[:, None] == w_group[None, :]),
        spread_lag=as_bf16((lane % C)[:, None] == np.arange(C)[None, :]),
        keep_lag=jnp.asarray(((lane // C)[:, None] == (lane // C)[None, :]).astype(np.float32)),
    )


def _ssm_kernel(u_ref, lag_ref, bxr_ref, bxi_ref, cxr_ref, cxi_ref, alr_ref, ali_ref, d_ref,
                sc_ref, kc_ref, sl_ref, kl_ref, o_ref,
                ub_ref, bcr_ref, bci_ref, ocr_ref, oci_ref, zr_ref, zi_ref, hr_ref, hi_ref, y_ref, *, bsz):
    L = SSM_CHUNK
    rows = u_ref.shape[0] // L
    chunks = rows // bsz
    tile = 2 * LANES
    step_rows = lambda r: pl.ds(r, rows, stride=L)
    for r in range(L):
        ub_ref[:, r * LANES:(r + 1) * LANES] = u_ref[step_rows(r), :].astype(BF16)
    state_group = lax.broadcasted_iota(jnp.int32, (1, bxr_ref.shape[1]), 1) // SSM_STATE
    for compact_ref, full_ref in ((bxr_ref, bcr_ref), (bxi_ref, bci_ref)):
        for r in range(L):
            blk = compact_ref[r * SSM_GROUP:(r + 1) * SSM_GROUP, :]
            for grp in range(GROUPS_PER_TILE):
                first = r * LANES + grp * SSM_GROUP
                full_ref[first:first + SSM_GROUP, :] = jnp.where(state_group == grp, blk, jnp.zeros_like(blk))
    zr_ref[...] = _dot(ub_ref[...], bcr_ref[...])
    zi_ref[...] = _dot(ub_ref[...], bci_ref[...])
    a_re, a_im = alr_ref[...], ali_ref[...]

    def step(k, carry):
        new = []
        for b in range(bsz):
            h_re, h_im = carry[2 * b], carry[2 * b + 1]
            row = pl.ds(b * chunks + k, 1)
            hr_ref[row, :] = h_re
            hi_ref[row, :] = h_im
            new.append(a_re * h_re - a_im * h_im + zr_ref[row, :])
            new.append(a_re * h_im + a_im * h_re + zi_ref[row, :])
        return tuple(new)

    zero = jnp.zeros((1, a_re.shape[1]), F32)
    lax.fori_loop(0, chunks, step, (zero,) * (2 * bsz))

    ocr_ref[...] = (_dot(cxr_ref[...], sc_ref[...]) * kc_ref[...]).astype(BF16)
    oci_ref[...] = (_dot(cxi_ref[...], sc_ref[...]) * kc_ref[...]).astype(BF16)
    lag = [(_dot(sl_ref[...], lag_ref[d]) * kl_ref[...]).astype(BF16) for d in range(L)]
    lag_tile = []
    for dd in range(L // 2):
        below = lag[2 * dd - 1] if dd > 0 else jnp.zeros((LANES, LANES), BF16)
        lag_tile.append(jnp.concatenate([jnp.concatenate([lag[2 * dd], lag[2 * dd + 1]], axis=1),
                                         jnp.concatenate([below, lag[2 * dd]], axis=1)], axis=0))
    hb_re, hb_im = hr_ref[...].astype(BF16), hi_ref[...].astype(BF16)
    for t2 in range(L // 2):
        cols = slice(t2 * tile, (t2 + 1) * tile)
        y = _dot(hb_re, ocr_ref[:, cols]) + _dot(hb_im, oci_ref[:, cols])
        for r2 in range(t2 + 1):
            y = y + _dot(ub_ref[:, r2 * tile:(r2 + 1) * tile], lag_tile[t2 - r2])
        for half in range(2):
            t = 2 * t2 + half
            y_ref[step_rows(t), :] = jax.nn.gelu(y[:, half * LANES:(half + 1) * LANES] + d_ref[...] * u_ref[step_rows(t), :])

    def chunk(rows):
        o_ref[rows, :] = y_ref[rows, :].astype(o_ref.dtype)
    _for_row_chunks(o_ref.shape[0], chunk)


def _ssm_core(proj, ops, bsz, seq):
    m = proj.shape[0]
    width = proj.shape[1] // 2
    L, C = SSM_CHUNK, SSM_GROUP
    n_oct = width // LANES
    rows = m // L
    ow = L * LANES
    n_state = GROUPS_PER_TILE * SSM_STATE
    ex = _ssm_expanders()
    per_oct = lambda *shape: pl.BlockSpec((None,) + shape, lambda p: (p,) + (0,) * len(shape))
    const = lambda a: pl.BlockSpec(a.shape, lambda p: (0,) * a.ndim, pipeline_mode=pl.Buffered(1))
    consts = [ex['spread_cols'], ex['keep_cols'], ex['spread_lag'], ex['keep_lag']]
    return pl.pallas_call(
        functools.partial(_ssm_kernel, bsz=bsz),
        grid=(n_oct,),
        in_specs=[
            pl.BlockSpec((m, LANES), lambda p: (0, p)),
            per_oct(L, C, LANES), per_oct(L * C, n_state), per_oct(L * C, n_state),
            per_oct(n_state, L * C), per_oct(n_state, L * C), per_oct(1, n_state), per_oct(1, n_state), per_oct(1, LANES),
        ] + [const(a) for a in consts],
        out_specs=pl.BlockSpec((m, LANES), lambda p: (0, p)),
        out_shape=jax.ShapeDtypeStruct((m, width), BF16),
        scratch_shapes=[pltpu.VMEM((rows, ow), BF16)] + [pltpu.VMEM((ow, n_state), BF16)] * 2
        + [pltpu.VMEM((n_state, ow), BF16)] * 2 + [pltpu.VMEM((rows, n_state), F32)] * 4 + [pltpu.VMEM((m, LANES), F32)],
        compiler_params=_params(1),
        name="ssm_core",
    )(proj, ops['lag'], ops['bc_re'], ops['bc_im'], ops['oc_re'], ops['oc_im_neg'], ops['al_re'], ops['al_im'], ops['d'], *consts)


def _glu_kernel(y_ref, wa_ref, wb_ref, ba_ref, bb_ref, gate_ref, o_ref):
    y = y_ref[...]
    ga = _dot(y, wa_ref[...]) + ba_ref[...]
    gb = _dot(y, wb_ref[...]) + bb_ref[...]
    o_ref[...] = ((ga * jax.nn.sigmoid(gb)) * _silu(gate_ref[...])).astype(o_ref.dtype)


def _glu(y, w_glu, layer, b_glu, proj, tm=1024, tn=512):
    m, k = y.shape
    width = w_glu.shape[2] // 2
    tm, tn = min(tm, m), min(tn, width)
    nt = width // tn
    w_tiles = _tile_cols(w_glu, layer, tn)
    return pl.pallas_call(
        _glu_kernel,
        grid=(m // tm, nt),
        in_specs=[
            pl.BlockSpec((tm, k), lambda i, j: (i, 0)),
            pl.BlockSpec((None, k, tn), lambda i, j: (j, 0, 0)),
            pl.BlockSpec((None, k, tn), lambda i, j: (nt + j, 0, 0)),
            pl.BlockSpec((1, tn), lambda i, j: (0, j)),
            pl.BlockSpec((1, tn), lambda i, j: (0, nt + j)),
            pl.BlockSpec((tm, tn), lambda i, j: (i, nt + j)),
        ],
        out_specs=pl.BlockSpec((tm, tn), lambda i, j: (i, j)),
        out_shape=jax.ShapeDtypeStruct((m, width), BF16),
        compiler_params=_params(2),
        name="ssm_glu",
    )(y, w_tiles, w_tiles, b_glu, b_glu, proj)


def _ssm_mixer(x, g, w_in, layer, lam_re, lam_im, log_dt, b_re, b_im, c_re, c_im, d_skip, w_glu, b_glu, bsz, seq):
    proj = _norm_matmul(x, g, _tile_cols(w_in, layer, MATMUL_TN), F32)
    ops = _ssm_operators(lam_re, lam_im, log_dt, b_re, b_im, c_re, c_im, d_skip)
    y = _ssm_core(proj, ops, bsz, seq)
    return _glu(y, w_glu, layer, b_glu.astype(F32).reshape(1, -1), proj)


CUM_BLOCK = 128


def _split3(x):
    x1 = x.astype(BF16)
    r1 = x - x1.astype(F32)
    x2 = r1.astype(BF16)
    x3 = (r1 - x2.astype(F32)).astype(BF16)
    return x1, x2, x3


def _forget_cumsum_kernel(z_ref, b_ref, o_ref):
    n_blocks = z_ref.shape[0] // CUM_BLOCK
    ri = lax.broadcasted_iota(jnp.int32, (CUM_BLOCK, CUM_BLOCK), 0)
    ci = lax.broadcasted_iota(jnp.int32, (CUM_BLOCK, CUM_BLOCK), 1)
    tri = jnp.where(ci <= ri, 1.0, 0.0).astype(BF16)

    def body(i, carry):
        rows = pl.ds(pl.multiple_of(i * CUM_BLOCK, CUM_BLOCK), CUM_BLOCK)
        z = z_ref[rows, :] + b_ref[...]
        log_f = jnp.minimum(z, 0.0) - jnp.log1p(jnp.exp(-jnp.abs(z)))
        x1, x2, x3 = _split3(log_f)
        c = _dot(tri, x1) + _dot(tri, x2) + _dot(tri, x3) + carry
        o_ref[rows, :] = c
        return c[CUM_BLOCK - 1:, :]

    lax.fori_loop(0, n_blocks, body, jnp.zeros((1, z_ref.shape[1]), F32))


def _forget_cumsum(z, b, bsz, seq):
    lanes = z.shape[1]
    return pl.pallas_call(
        _forget_cumsum_kernel,
        grid=(bsz,),
        in_specs=[pl.BlockSpec((seq, lanes), lambda i: (i, 0)), pl.BlockSpec((1, lanes), lambda i: (0, 0))],
        out_specs=pl.BlockSpec((seq, lanes), lambda i: (i, 0)),
        out_shape=jax.ShapeDtypeStruct(z.shape, F32),
        compiler_params=_params(1),
        name="forget_cumsum",
    )(z, b)


FOX_STEP_HEADS = 8


def _fox_kernel(q_ref, k_ref, v_ref, sg_ref, cq_ref, ck_ref, o_ref, ka_ref, vt_ref, *head_refs, tk):
    nh = FOX_STEP_HEADS
    hg, qi = pl.program_id(1), pl.program_id(2)
    tq = q_ref.shape[0]
    qa_ref, s_ref, p_ref, acc_ref, pv_ref, m_ref = (head_refs[i::6] for i in range(6))
    lane = lax.broadcasted_iota(jnp.int32, (1, LANES), 1)
    free = (HEAD_DIM, 0)
    n_pieces = 3
    pair_lanes = lambda h: slice(h // 2 * LANES, (h // 2 + 1) * LANES)

    def bias_lanes(c, h, c_offset, one_offset):
        rr = lax.broadcasted_iota(jnp.int32, (n_pieces * LANES, LANES), 0)
        cc = lax.broadcasted_iota(jnp.int32, (n_pieces * LANES, LANES), 1)
        place = jnp.logical_and(rr % LANES == nh * hg + h, cc == free[h % 2] + c_offset + rr // LANES)
        placed = _dot(jnp.concatenate(_split3(c), axis=1), jnp.where(place, 1.0, 0.0).astype(BF16))
        first = free[h % 2] + one_offset
        return (placed + jnp.where(jnp.logical_and(lane >= first, lane < first + n_pieces), 1.0, 0.0)).astype(BF16)

    @pl.when(qi == 0)
    def _():
        def chunk(j, carry):
            rows = pl.ds(pl.multiple_of(j * tk, tk), tk)
            neg_c = ck_ref[rows, :] * -LOG2E
            for h in range(nh):
                keys = k_ref[rows, pair_lanes(h)]
                ka_ref[h, j] = jnp.where((lane // HEAD_DIM) == h % 2, keys, bias_lanes(neg_c, h, 0, n_pieces))
            for pair in range(nh // 2):
                v_t = v_ref[rows, pair_lanes(2 * pair)].astype(F32).T.astype(BF16)
                for hh in range(2):
                    vt_ref[j, 2 * pair + hh, :HEAD_DIM, :] = v_t[hh * HEAD_DIM:(hh + 1) * HEAD_DIM]
                    vt_ref[j, 2 * pair + hh, HEAD_DIM:, :] = jnp.ones((ONES_ROWS, tk), BF16)
            return carry
        lax.fori_loop(0, k_ref.shape[0] // tk, chunk, 0)

    cq = cq_ref[...] * LOG2E
    for h in range(nh):
        qa = jnp.where((lane // HEAD_DIM) == h % 2, q_ref[:, pair_lanes(h)], bias_lanes(cq, h, n_pieces, 0))
        qa_ref[h][...] = qa.astype(F32).T.astype(BF16)

    def scores(h, j):
        s_ref[h][...] = _dot(ka_ref[h, j], qa_ref[h][...])

    rows8 = 8
    rows16 = 16

    def absorb(h, j, diagonal):
        def strip(r, n):
            blk = s_ref[h][r:r + n, :]
            if diagonal:
                kpos = lax.broadcasted_iota(jnp.int32, (n, tq), 0)
                qpos = lax.broadcasted_iota(jnp.int32, (n, tq), 1)
                blk = jnp.where(kpos + r <= qpos, blk, NEG)
            return blk
        top = strip(0, rows8)
        for r in range(rows8, tk, rows8):
            top = jnp.maximum(top, strip(r, rows8))
        m = m_ref[h][...]
        m_new = jnp.maximum(m, top.max(axis=0, keepdims=True))
        m_ref[h][...] = m_new
        alpha = jnp.exp2(m - m_new)
        m_rows = jnp.broadcast_to(m_new, (rows16, tq))
        for r in range(0, tk, rows16):
            p_ref[h][r:r + rows16, :] = jnp.exp2(strip(r, rows16) - m_rows).astype(BF16)
        acc_ref[h][...] = alpha * (acc_ref[h][...] + pv_ref[h][...])

    def values(h, j):
        pv_ref[h][...] = _dot(vt_ref[j, h], p_ref[h][...])

    def step(j, diagonal):
        values(nh - 1, jnp.maximum(j - 1, 0))
        scores(1, j)
        for h in range(nh):
            absorb(h, j, diagonal)
            if h < nh - 1:
                values(h, j)
            if h + 2 < nh:
                scores(h + 2, j)
            elif h + 2 == nh and not diagonal:
                scores(0, j + 1)

    for h in range(nh):
        acc_ref[h][...] = jnp.zeros(acc_ref[h].shape, F32)
        pv_ref[h][...] = jnp.zeros(pv_ref[h].shape, F32)
        m_ref[h][...] = jnp.full(m_ref[h].shape, NEG, F32)
    p_ref[nh - 1][...] = jnp.zeros(p_ref[nh - 1].shape, BF16)
    scores(0, 0)

    def two_blocks(i, carry):
        step(2 * i, False)
        step(2 * i + 1, False)
        return carry
    lax.fori_loop(0, qi // 2, two_blocks, 0)

    @pl.when(qi % 2 == 0)
    def _():
        step(qi, True)

    @pl.when(qi % 2 == 1)
    def _():
        step(qi - 1, False)
        step(qi, True)

    values(nh - 1, qi)
    outs = []
    for h in range(nh):
        total = acc_ref[h][...] + pv_ref[h][...]
        outs.append(total[:HEAD_DIM] / total[HEAD_DIM:HEAD_DIM + 1])
    out = jnp.concatenate(outs, axis=0)
    o_ref[...] = (out.T * sg_ref[...].astype(F32)).astype(o_ref.dtype)


def _fox_attention(proj, csum, bsz, seq, t=512):
    width = FOX_HEADS * HEAD_DIM
    nh = FOX_STEP_HEADS
    gw = nh * HEAD_DIM
    n_groups = width // gw
    t = min(t, seq)
    nq = seq // t
    return pl.pallas_call(
        functools.partial(_fox_kernel, tk=t),
        grid=(bsz, n_groups, nq),
        in_specs=[
            pl.BlockSpec((t, gw), lambda b, h, i: (b * nq + i, h)),
            pl.BlockSpec((seq, gw), lambda b, h, i: (b, n_groups + h), pipeline_mode=pl.Buffered(1)),
            pl.BlockSpec((seq, gw), lambda b, h, i: (b, 2 * n_groups + h), pipeline_mode=pl.Buffered(1)),
            pl.BlockSpec((t, gw), lambda b, h, i: (b * nq + i, 3 * n_groups + h)),
            pl.BlockSpec((t, LANES), lambda b, h, i: (b * nq + i, 0)),
            pl.BlockSpec((seq, LANES), lambda b, h, i: (b, 0), pipeline_mode=pl.Buffered(1)),
        ],
        out_specs=pl.BlockSpec((t, gw), lambda b, h, i: (b * nq + i, h)),
        out_shape=jax.ShapeDtypeStruct((bsz * seq, width), BF16),
        scratch_shapes=[
            pltpu.VMEM((nh, nq, t, LANES), BF16), pltpu.VMEM((nq, nh, HEAD_DIM + ONES_ROWS, t), BF16),
        ] + nh * [pltpu.VMEM((LANES, t), BF16), pltpu.VMEM((t, t), F32), pltpu.VMEM((t, t), BF16),
                  pltpu.VMEM((HEAD_DIM + ONES_ROWS, t), F32), pltpu.VMEM((HEAD_DIM + ONES_ROWS, t), F32),
                  pltpu.VMEM((1, t), F32)],
        compiler_params=_params(3, VMEM_LIMIT_BIG),
        name="fox_attention",
    )(proj, proj, proj, proj, csum, csum)


def _fox_mixer(x, g, w_in, layer, w_fg, b_fg, bsz, seq, t=512):
    width = FOX_HEADS * HEAD_DIM
    col_scale = jnp.where(jnp.arange(4 * width) < width, LOG2E * HEAD_DIM ** -0.5, 1.0).astype(F32)
    proj = _norm_matmul(x, g, _tile_cols(w_in, layer, MATMUL_TN, col_scale), BF16, silu_cols=(3 * width, 4 * width))
    pad = LANES - FOX_HEADS
    z = _norm_matmul(x, g, _tile_cols_xla(jnp.pad(w_fg, ((0, 0), (0, pad))), LANES), F32)
    csum = _forget_cumsum(z, jnp.pad(b_fg.astype(F32), (0, pad)).reshape(1, LANES), bsz, seq)
    return _fox_attention(proj, csum, bsz, seq, t)


def kernel(x, p, norm_g, final_g, rel_bias, swa_w_in, swa_w_out, swa_sinks, conv_w_in, conv_kernel, conv_w_out, ssm_w_in, ssm_lam_re, ssm_lam_im, ssm_log_dt, ssm_b_re, ssm_b_im, ssm_c_re, ssm_c_im, ssm_d, ssm_w_glu, ssm_b_glu, ssm_w_out, fox_w_in, fox_w_fg, fox_b_fg, fox_w_out, ple_proj, ple_norm, ple_gate):
    bsz, seq, d_model = x.shape
    depth = p.shape[0]
    h = x.astype(F32).reshape(bsz * seq, d_model)
    for i in range(depth):
        mixer, j = i % N_MIXERS, i // N_MIXERS
        if mixer == 0:
            a, w_out = _swa_mixer(h, norm_g[i], swa_w_in, j, swa_sinks[j], rel_bias, bsz, seq), swa_w_out[j]
        elif mixer == 1:
            a, w_out = _conv_mixer(h, norm_g[i], conv_w_in, j, conv_kernel[j], seq), conv_w_out[j]
        elif mixer == 2:
            a = _ssm_mixer(h, norm_g[i], ssm_w_in, j, ssm_lam_re[j], ssm_lam_im[j], ssm_log_dt[j], ssm_b_re[j], ssm_b_im[j],
                           ssm_c_re[j], ssm_c_im[j], ssm_d[j], ssm_w_glu, ssm_b_glu[j], bsz, seq)
            w_out = ssm_w_out[j]
        else:
            a, w_out = _fox_mixer(h, norm_g[i], fox_w_in, j, fox_w_fg[j], fox_b_fg[j], bsz, seq), fox_w_out[j]
        h = _out_ple(a, w_out, h, ple_norm[i], ple_gate, p.reshape(depth, bsz * seq, -1), ple_proj, i,
                     final_g=final_g if i == depth - 1 else None)
    return h.reshape(bsz, seq, d_model).astype(x.dtype)
```

```python
import functools
import math

import numpy as np
import jax
import jax.numpy as jnp
from jax import lax
from jax.experimental import pallas as pl
from jax.experimental.pallas import tpu as pltpu

F32 = jnp.float32
BF16 = jnp.bfloat16

EPS = 1e-6
N_MIXERS = 4
PLE_DIM = 256

SWA_HEADS = 32
SWA_KV_HEADS = 4
SWA_GROUP = SWA_HEADS // SWA_KV_HEADS
HEAD_DIM = 64
SWA_BLOCK = 128
WINDOW = 128
REL_BUCKETS = 32
REL_MAX_DIST = 128

CONV_TAPS = 3

SSM_GROUP = 16
SSM_STATE = 64
SSM_CHUNK = 16

FOX_HEADS = 32

LANES = 128
ONES_ROWS = 16
GROUPS_PER_TILE = LANES // SSM_GROUP
VMEM_LIMIT = 48 * 1024 * 1024
VMEM_LIMIT_BIG = 60 * 1024 * 1024

NEG = float(jnp.finfo(jnp.float32).min)
LOG2E = math.log2(math.e)


def _params(n_axes, vmem_limit=VMEM_LIMIT):
    return pltpu.CompilerParams(dimension_semantics=("arbitrary",) * n_axes, vmem_limit_bytes=vmem_limit)


def _dot(a, b):
    return jnp.dot(a, b, preferred_element_type=F32)


def _dot_nt(a, b):
    return lax.dot_general(a, b, (((1,), (1,)), ((), ())), preferred_element_type=F32)


def _rmsnorm_rows(x, g):
    return x * lax.rsqrt(jnp.mean(x * x, axis=-1, keepdims=True) + EPS) * g


def _silu(x):
    return x * jax.nn.sigmoid(x)


ROW_CHUNK = 64


def _for_row_chunks(n_rows, fn):
    def body(c, carry):
        fn(pl.ds(pl.multiple_of(c * ROW_CHUNK, ROW_CHUNK), ROW_CHUNK))
        return carry
    lax.fori_loop(0, n_rows // ROW_CHUNK, body, 0)


def _tile_cols_xla(w, tn):
    k, n = w.shape
    return w.astype(BF16).reshape(k, n // tn, tn).transpose(1, 0, 2)


def _cast_tiles_kernel(w_ref, o_ref):
    def chunk(rows):
        o_ref[rows, :] = w_ref[rows, :].astype(BF16)
    _for_row_chunks(o_ref.shape[0], chunk)


def _scale_cast_tiles_kernel(w_ref, s_ref, o_ref):
    def chunk(rows):
        o_ref[rows, :] = (w_ref[rows, :] * s_ref[...]).astype(BF16)
    _for_row_chunks(o_ref.shape[0], chunk)


def _tile_cols(w, layer, tn, col_scale=None):
    _, k, n = w.shape
    tn = min(tn, n)
    w_spec = pl.BlockSpec((None, k, tn), lambda j: (layer, 0, j))
    common = dict(
        grid=(n // tn,),
        out_specs=pl.BlockSpec((None, k, tn), lambda j: (j, 0, 0)),
        out_shape=jax.ShapeDtypeStruct((n // tn, k, tn), BF16),
        compiler_params=_params(1),
        name="weight_tiles",
    )
    if col_scale is None:
        return pl.pallas_call(_cast_tiles_kernel, in_specs=[w_spec], **common)(w)
    s_spec = pl.BlockSpec((1, tn), lambda j: (0, j))
    return pl.pallas_call(_scale_cast_tiles_kernel, in_specs=[w_spec, s_spec], **common)(w, col_scale.reshape(1, n))


def _norm_matmul_kernel(x0_ref, x_ref, g_ref, w_ref, o_ref, a_ref, *, silu_tiles, rows_per_step):
    i, j = pl.program_id(0), pl.program_id(1)
    tm = a_ref.shape[1]

    def normalize(src_ref, slot, rows):
        a_ref[slot, rows, :] = _rmsnorm_rows(src_ref[rows, :], g_ref[...]).astype(BF16)

    @pl.when(jnp.logical_and(i == 0, j == 0))
    def _():
        _for_row_chunks(tm, lambda rows: normalize(x0_ref, 0, rows))

    acc = _dot(a_ref[i % 2], w_ref[...])
    start = jnp.minimum(j * rows_per_step, tm - rows_per_step)
    for c in range(rows_per_step // ROW_CHUNK):
        normalize(x_ref, (i + 1) % 2, pl.ds(pl.multiple_of(start + c * ROW_CHUNK, ROW_CHUNK), ROW_CHUNK))
    if silu_tiles is None:
        o_ref[...] = acc.astype(o_ref.dtype)
    else:
        gated = jnp.logical_and(j >= silu_tiles[0], j < silu_tiles[1])

        @pl.when(gated)
        def _():
            o_ref[...] = _silu(acc).astype(o_ref.dtype)

        @pl.when(jnp.logical_not(gated))
        def _():
            o_ref[...] = acc.astype(o_ref.dtype)


MATMUL_TN = 1024


def _norm_matmul(x, g, w_tiles, out_dtype, silu_cols=None, tm=1024):
    m, k = x.shape
    nt, _, tn = w_tiles.shape
    n = nt * tn
    tm = min(tm, m)
    n_row_tiles = m // tm
    silu_tiles = None if silu_cols is None else (silu_cols[0] // tn, silu_cols[1] // tn)
    rows_per_step = -(-tm // (nt * ROW_CHUNK)) * ROW_CHUNK
    return pl.pallas_call(
        functools.partial(_norm_matmul_kernel, silu_tiles=silu_tiles, rows_per_step=rows_per_step),
        grid=(n_row_tiles, nt),
        in_specs=[
            pl.BlockSpec((tm, k), lambda i, j: (0, 0), pipeline_mode=pl.Buffered(1)),
            pl.BlockSpec((tm, k), lambda i, j: (jnp.minimum(i + 1, n_row_tiles - 1), 0)),
            pl.BlockSpec((1, k), lambda i, j: (0, 0)),
            pl.BlockSpec((None, k, tn), lambda i, j: (j, 0, 0)),
        ],
        out_specs=pl.BlockSpec((tm, tn), lambda i, j: (i, j)),
        out_shape=jax.ShapeDtypeStruct((m, n), out_dtype),
        scratch_shapes=[pltpu.VMEM((2, tm, k), BF16)],
        compiler_params=_params(2, VMEM_LIMIT_BIG),
        name="norm_matmul",
    )(x, x, g.reshape(1, k), w_tiles)


def _out_ple_kernel(a_ref, wo_ref, x_ref, g_ref, wg_ref, p_ref, wp_ref, gf_ref, o_ref, x1_ref, hn_ref, ss_ref, *, nt, final):
    j = pl.program_id(1)
    tn = x_ref.shape[1]
    n = nt * tn

    @pl.when(j == 0)
    def _():
        ss_ref[...] = jnp.zeros(ss_ref.shape, F32)

    @pl.when(j < nt)
    def _():
        x1 = x_ref[...] + _dot(a_ref[...], wo_ref[j])
        x1_ref[j] = x1
        hn_ref[j] = (x1 * g_ref[j]).astype(BF16)
        ss_ref[0] += jnp.sum(x1 * x1, axis=-1, keepdims=True)

    @pl.when(jnp.logical_and(j >= nt, j < 2 * nt))
    def _():
        t = j - nt
        emb = _dot(p_ref[...].astype(BF16), wp_ref[t])
        acc = _dot(hn_ref[0], wg_ref[t, :tn, :])
        for kt in range(1, nt):
            acc = acc + _dot(hn_ref[kt], wg_ref[t, kt * tn:(kt + 1) * tn, :])
        x2 = x1_ref[t] + emb * jax.nn.sigmoid(acc * lax.rsqrt(ss_ref[0] / n + EPS))
        if final:
            x1_ref[t] = x2
            ss_ref[1] += jnp.sum(x2 * x2, axis=-1, keepdims=True)
        else:
            o_ref[...] = x2

    if final:
        @pl.when(j >= 2 * nt)
        def _():
            t = j - 2 * nt
            o_ref[...] = x1_ref[t] * lax.rsqrt(ss_ref[1] / n + EPS) * gf_ref[t]


def _out_ple(a, w_out, x, g, w_gate, p, w_proj, layer, final_g=None, tm=1024, tn=512):
    m, k = a.shape
    n = w_out.shape[1]
    pd = p.shape[2]
    final = final_g is not None
    tm, tn = min(tm, m), min(tn, n)
    nt = n // tn
    passes = 3 if final else 2
    out_tile = lambda j: jnp.maximum(j - (passes - 1) * nt, 0)
    resident = lambda rows: pl.BlockSpec((nt, rows, tn), lambda i, j: (0, 0, 0), pipeline_mode=pl.Buffered(1))
    row_vec = pl.BlockSpec((nt, 1, tn), lambda i, j: (0, 0, 0))
    return pl.pallas_call(
        functools.partial(_out_ple_kernel, nt=nt, final=final),
        grid=(m // tm, passes * nt),
        in_specs=[
            pl.BlockSpec((tm, k), lambda i, j: (i, 0)),
            resident(k),
            pl.BlockSpec((tm, tn), lambda i, j: (i, jnp.minimum(j, nt - 1))),
            row_vec,
            resident(n),
            pl.BlockSpec((None, tm, pd), lambda i, j: (layer, i, 0)),
            resident(pd),
            row_vec,
        ],
        out_specs=pl.BlockSpec((tm, tn), lambda i, j: (i, out_tile(j))),
        out_shape=jax.ShapeDtypeStruct((m, n), F32),
        scratch_shapes=[pltpu.VMEM((nt, tm, tn), F32), pltpu.VMEM((nt, tm, tn), BF16), pltpu.VMEM((2, tm, 1), F32)],
        compiler_params=_params(2, VMEM_LIMIT_BIG),
        name="out_ple_final" if final else "out_ple",
    )(a, _tile_cols(w_out[None], 0, tn), x, g.reshape(nt, 1, tn), _tile_cols(w_gate, layer, tn), p,
      _tile_cols(w_proj, layer, tn), (final_g if final else g).reshape(nt, 1, tn))


def _t5_bucket(dist):
    max_exact = REL_BUCKETS // 2
    d = np.maximum(dist, 1).astype(np.float32)
    large = max_exact + (np.log(d / max_exact) / np.log(REL_MAX_DIST / max_exact) * (REL_BUCKETS - max_exact)).astype(np.int32)
    large = np.minimum(large, REL_BUCKETS - 1)
    return np.where(dist < max_exact, dist, large).astype(np.int32)


def _swa_bucket_table():
    qi = np.arange(SWA_BLOCK)[None, :]
    kj = np.arange(2 * SWA_BLOCK)[:, None]
    return _t5_bucket(np.clip(qi + SWA_BLOCK - kj, 0, None))


def _swa_kernel(sink_ref, rel_ref, bucket_ref, q_ref, sg_ref, kp_ref, kc_ref, vp_ref, vc_ref, o_ref, bias_ref, qs_ref):
    n = pl.program_id(1)
    blk = SWA_BLOCK
    lane = lax.broadcasted_iota(jnp.int32, (1, LANES), 1)

    @pl.when(jnp.logical_and(pl.program_id(0) == 0, n == 0))
    def _():
        bucket = bucket_ref[...]
        kj = lax.broadcasted_iota(jnp.int32, (2 * blk, blk), 0)
        qi = lax.broadcasted_iota(jnp.int32, (2 * blk, blk), 1)
        band = jnp.logical_or(jnp.logical_and(kj < blk, kj > qi), jnp.logical_and(kj >= blk, kj - blk <= qi))

        def per_head(head, carry):
            acc = jnp.zeros((2 * blk, blk), F32)
            for b in range(REL_BUCKETS):
                acc = jnp.where(bucket == b, rel_ref[b, head] * LOG2E, acc)
            bias_ref[head] = jnp.where(band, acc, NEG)
            return carry
        lax.fori_loop(0, SWA_HEADS, per_head, 0)

    for kvh in range(SWA_KV_HEADS):
        cols = slice(kvh * LANES, (kvh + 1) * LANES)
        heads = range(kvh * SWA_GROUP, (kvh + 1) * SWA_GROUP)
        for g, head in enumerate(heads):
            q_pair = q_ref[:, head // 2 * LANES:(head // 2 + 1) * LANES]
            qs_ref[g * blk:(g + 1) * blk, :] = jnp.where((lane // HEAD_DIM) == head % 2, q_pair, jnp.zeros_like(q_pair))
        keys = jnp.concatenate([kp_ref[:, cols], kc_ref[:, cols]], axis=0)
        vals = jnp.concatenate([vp_ref[:, cols], vc_ref[:, cols]], axis=0)
        s = _dot_nt(keys, qs_ref[...]) + jnp.concatenate([bias_ref[head] for head in heads], axis=1)
        s_prev = jnp.where(n > 0, s[:blk], NEG)
        s_cur = s[blk:]
        sink = jnp.concatenate([jnp.full((1, blk), sink_ref[head] * LOG2E, F32) for head in heads], axis=1)
        m = jnp.maximum(jnp.maximum(s_prev, s_cur).max(axis=0, keepdims=True), sink)
        e = jnp.concatenate([jnp.exp2(s_prev - m).astype(BF16), jnp.exp2(s_cur - m).astype(BF16)], axis=0)
        v_t = jnp.concatenate([vals.astype(F32).T.astype(BF16)[:HEAD_DIM], jnp.ones((ONES_ROWS, 2 * blk), BF16)], axis=0)
        out = _dot(v_t, e)
        out = out[:HEAD_DIM] / (out[HEAD_DIM:HEAD_DIM + 1] + jnp.exp2(sink - m))
        for pair in range(SWA_GROUP // 2):
            head = kvh * SWA_GROUP + 2 * pair
            even = out[:, 2 * pair * blk:(2 * pair + 1) * blk]
            odd = out[:, (2 * pair + 1) * blk:(2 * pair + 2) * blk]
            cols = slice(head // 2 * LANES, (head // 2 + 1) * LANES)
            o_ref[:, cols] = (jnp.concatenate([even, odd], axis=0).T * sg_ref[:, cols].astype(F32)).astype(o_ref.dtype)


def _swa_attention(proj, sinks, rel_bias, bsz, seq):
    nb = seq // SWA_BLOCK
    width = SWA_HEADS * HEAD_DIM
    kvw = SWA_KV_HEADS * LANES
    k_blk = 2 * width // kvw
    cur = lambda b, n: b * nb + n
    prev = lambda b, n: b * nb + jnp.maximum(n - 1, 0)
    smem = pl.BlockSpec(memory_space=pltpu.SMEM)
    return pl.pallas_call(
        _swa_kernel,
        grid=(bsz, nb),
        in_specs=[
            smem, smem,
            pl.BlockSpec((2 * SWA_BLOCK, SWA_BLOCK), lambda b, n: (0, 0)),
            pl.BlockSpec((SWA_BLOCK, width), lambda b, n: (cur(b, n), 0)),
            pl.BlockSpec((SWA_BLOCK, width), lambda b, n: (cur(b, n), 1)),
            pl.BlockSpec((SWA_BLOCK, kvw), lambda b, n: (prev(b, n), k_blk)),
            pl.BlockSpec((SWA_BLOCK, kvw), lambda b, n: (cur(b, n), k_blk)),
            pl.BlockSpec((SWA_BLOCK, kvw), lambda b, n: (prev(b, n), k_blk + 1)),
            pl.BlockSpec((SWA_BLOCK, kvw), lambda b, n: (cur(b, n), k_blk + 1)),
        ],
        out_specs=pl.BlockSpec((SWA_BLOCK, width), lambda b, n: (cur(b, n), 0)),
        out_shape=jax.ShapeDtypeStruct((bsz * seq, width), BF16),
        scratch_shapes=[
            pltpu.VMEM((SWA_HEADS, 2 * SWA_BLOCK, SWA_BLOCK), F32),
            pltpu.VMEM((SWA_GROUP * SWA_BLOCK, LANES), BF16),
        ],
        compiler_params=_params(2),
        name="swa_attention",
    )(sinks.astype(F32), rel_bias.astype(F32), jnp.asarray(_swa_bucket_table()), proj, proj, proj, proj, proj, proj)


SWA_HALF_TILE = 512


def _swa_weight_tiles_kernel(wl_ref, wr_ref, o_ref, *, q_tiles, gate_tiles, q_scale):
    j = pl.program_id(0)
    half = SWA_HALF_TILE
    kvw = SWA_KV_HEADS * HEAD_DIM

    def rows_do(fn):
        def chunk(rows):
            for side, w_ref in enumerate((wl_ref, wr_ref)):
                o_ref[rows, side * half:(side + 1) * half] = fn(side, w_ref, rows)
        _for_row_chunks(o_ref.shape[0], chunk)

    @pl.when(j < q_tiles)
    def _():
        rows_do(lambda side, w_ref, rows: (w_ref[rows, :] * q_scale).astype(BF16))

    @pl.when(jnp.logical_and(j >= q_tiles, j < q_tiles + gate_tiles))
    def _():
        rows_do(lambda side, w_ref, rows: w_ref[rows, :].astype(BF16))

    @pl.when(j == q_tiles + gate_tiles)
    def _():
        def twice(side, w_ref, rows):
            w = w_ref[rows, side * kvw:(side + 1) * kvw].astype(BF16)
            heads = [w[:, h * HEAD_DIM:(h + 1) * HEAD_DIM] for h in range(SWA_KV_HEADS)]
            return jnp.concatenate([piece for head in heads for piece in (head, head)], axis=1)
        rows_do(twice)


def _swa_weight_tiles(w_in, layer, q_scale):
    _, k, n = w_in.shape
    half = SWA_HALF_TILE
    width = SWA_HEADS * HEAD_DIM
    q_tiles = gate_tiles = width // (2 * half)
    kv_block = width // half
    n_out = q_tiles + gate_tiles + 1

    def src(j, side):
        in_q, in_gate = j < q_tiles, j < q_tiles + gate_tiles
        return jnp.where(in_q, 2 * j + side, jnp.where(in_gate, 2 * j + side + 1, kv_block))
    return pl.pallas_call(
        functools.partial(_swa_weight_tiles_kernel, q_tiles=q_tiles, gate_tiles=gate_tiles, q_scale=q_scale),
        grid=(n_out,),
        in_specs=[pl.BlockSpec((None, k, half), lambda j: (layer, 0, src(j, 0))),
                  pl.BlockSpec((None, k, half), lambda j: (layer, 0, src(j, 1)))],
        out_specs=pl.BlockSpec((None, k, 2 * half), lambda j: (j, 0, 0)),
        out_shape=jax.ShapeDtypeStruct((n_out, k, 2 * half), BF16),
        compiler_params=_params(1),
        name="swa_weight_tiles",
    )(w_in, w_in)


def _swa_mixer(x, g, w_in, layer, sinks, rel_bias, bsz, seq):
    width = SWA_HEADS * HEAD_DIM
    w_tiles = _swa_weight_tiles(w_in, layer, LOG2E * HEAD_DIM ** -0.5)
    proj = _norm_matmul(x, g, w_tiles, BF16, silu_cols=(width, 2 * width))
    return _swa_attention(proj, sinks, rel_bias, bsz, seq)


HALO = 8


def _conv_kernel(x_ref, g_ref, wb_ref, wc_ref, wu_ref, wg_ref, ck_ref, o_ref, a_ref, z_ref, carry_ref, *, tiles_per_seq):
    i, j = pl.program_id(0), pl.program_id(1)
    tm = a_ref.shape[0]

    @pl.when(j == 0)
    def _():
        def chunk(rows):
            a_ref[rows, :] = _rmsnorm_rows(x_ref[rows, :], g_ref[...]).astype(BF16)
        _for_row_chunks(tm, chunk)

    a = a_ref[...]
    z = _dot(a, wc_ref[...]) * _dot(a, wu_ref[...])
    first = (i % tiles_per_seq) == 0

    @pl.when(first)
    def _():
        z_ref[:HALO, :] = jnp.zeros((HALO, z_ref.shape[1]), F32)

    @pl.when(jnp.logical_not(first))
    def _():
        z_ref[:HALO, :] = carry_ref[j]

    z_ref[HALO:, :] = z
    carry_ref[j] = z[tm - HALO:, :]
    conv = z_ref[HALO - 2:HALO - 2 + tm, :] * ck_ref[0:1, :]
    conv = conv + z_ref[HALO - 1:HALO - 1 + tm, :] * ck_ref[1:2, :]
    conv = conv + z * ck_ref[2:3, :]
    y = _dot(a, wb_ref[...]) * conv
    o_ref[...] = (y * _silu(_dot(a, wg_ref[...]))).astype(o_ref.dtype)


def _conv_mixer(x, g, w_in, layer, conv_kernel, seq, tm=1024, tn=512):
    m, k = x.shape
    width = w_in.shape[2] // 4
    tm, tn = min(tm, seq), min(tn, width)
    nt = width // tn
    w_spec = lambda q: pl.BlockSpec((None, k, tn), lambda i, j: (q * nt + j, 0, 0))
    w_in = _tile_cols(w_in, layer, tn)
    return pl.pallas_call(
        functools.partial(_conv_kernel, tiles_per_seq=seq // tm),
        grid=(m // tm, nt),
        in_specs=[
            pl.BlockSpec((tm, k), lambda i, j: (i, 0)),
            pl.BlockSpec((1, k), lambda i, j: (0, 0)),
            w_spec(0), w_spec(1), w_spec(2), w_spec(3),
            pl.BlockSpec((CONV_TAPS, tn), lambda i, j: (0, j)),
        ],
        out_specs=pl.BlockSpec((tm, tn), lambda i, j: (i, j)),
        out_shape=jax.ShapeDtypeStruct((m, width), BF16),
        scratch_shapes=[
            pltpu.VMEM((tm, k), BF16),
            pltpu.VMEM((HALO + tm, tn), F32),
            pltpu.VMEM((nt, HALO, tn), F32),
        ],
        compiler_params=_params(2, VMEM_LIMIT_BIG),
        name="conv_mixer",
    )(x, g.reshape(1, k), w_in, w_in, w_in, w_in, conv_kernel.astype(F32))


def _ssm_operators(lam_re, lam_im, log_dt, b_re, b_im, c_re, c_im, d_skip):
    n_groups = lam_re.shape[0]
    n_oct = n_groups // GROUPS_PER_TILE
    L, C, N = SSM_CHUNK, SSM_GROUP, SSM_STATE
    dt = jnp.exp(log_dt.astype(F32))[None, :]
    lr, li = lam_re.astype(F32).T, lam_im.astype(F32).T
    mag = jnp.exp(lr * dt)
    ab_re, ab_im = mag * jnp.cos(li * dt), mag * jnp.sin(li * dt)
    den = lr * lr + li * li
    nr = ab_re - 1.0
    coef_re = (nr * lr + ab_im * li) / den
    coef_im = (ab_im * lr - nr * li) / den
    br, bi = b_re.astype(F32).transpose(2, 1, 0), b_im.astype(F32).transpose(2, 1, 0)
    bb_re = coef_re * br - coef_im * bi
    bb_im = coef_re * bi + coef_im * br
    cr, ci = c_re.astype(F32).transpose(1, 2, 0), c_im.astype(F32).transpose(1, 2, 0)
    pw_re, pw_im = [jnp.ones_like(ab_re)], [jnp.zeros_like(ab_im)]
    for _ in range(L):
        pw_re, pw_im = (pw_re + [pw_re[-1] * ab_re - pw_im[-1] * ab_im], pw_im + [pw_re[-1] * ab_im + pw_im[-1] * ab_re])
    p_re, p_im = jnp.stack(pw_re), jnp.stack(pw_im)
    w_re = cr[None] * p_re[:L, None] - ci[None] * p_im[:L, None]
    w_im = cr[None] * p_im[:L, None] + ci[None] * p_re[:L, None]
    lag_k = jnp.sum(w_re[:, None] * bb_re[None, :, None] - w_im[:, None] * bb_im[None, :, None], axis=3)
    q_re, q_im = jnp.stack(pw_re[L - 1::-1]), jnp.stack(pw_im[L - 1::-1])
    bc_re = q_re[:, None] * bb_re[None] - q_im[:, None] * bb_im[None]
    bc_im = q_re[:, None] * bb_im[None] + q_im[:, None] * bb_re[None]
    o_re = cr[None] * p_re[1:, None] - ci[None] * p_im[1:, None]
    o_im = cr[None] * p_im[1:, None] + ci[None] * p_re[1:, None]
    octs = lambda t: t.reshape(t.shape[:-1] + (n_oct, GROUPS_PER_TILE))
    to_lag = lambda t: octs(t).transpose(3, 0, 1, 4, 2).reshape(n_oct, L, C, LANES)
    to_rows = lambda t: octs(t).transpose(3, 0, 1, 4, 2).reshape(n_oct, L * C, GROUPS_PER_TILE * N)
    to_cols = lambda t: octs(t).transpose(3, 4, 2, 0, 1).reshape(n_oct, GROUPS_PER_TILE * N, L * C)
    to_vec = lambda t: octs(t).transpose(1, 2, 0).reshape(n_oct, 1, GROUPS_PER_TILE * N)
    return dict(
        lag=to_lag(lag_k).astype(BF16),
        bc_re=to_rows(bc_re).astype(BF16), bc_im=to_rows(bc_im).astype(BF16),
        oc_re=to_cols(o_re).astype(BF16), oc_im_neg=to_cols(-o_im).astype(BF16),
        al_re=to_vec(p_re[L]), al_im=to_vec(p_im[L]), d=d_skip.astype(F32).reshape(n_oct, 1, LANES),
    )


def _ssm_expanders():
    L, C, N, G8 = SSM_CHUNK, SSM_GROUP, SSM_STATE, GROUPS_PER_TILE
    wide = np.arange(L * LANES)
    w_step, w_group, w_chan = wide // LANES, (wide // C) % G8, wide % C
    small = np.arange(L * C)
    s_step, s_chan = small // C, small % C
    state_group = np.arange(G8 * N) // N
    spread = (w_step[:, None] == s_step[None, :]) & (w_chan[:, None] == s_chan[None, :])
    lane = np.arange(LANES)
    as_bf16 = lambda a: jnp.asarray(a.astype(np.float32), dtype=BF16)
    return dict(
        spread_cols=as_bf16(spread.T),
        keep_cols=as_bf16(state_group[:, None] == w_group[None, :]),
        spread_lag=as_bf16((lane % C)[:, None] == np.arange(C)[None, :]),
        keep_lag=jnp.asarray(((lane // C)[:, None] == (lane // C)[None, :]).astype(np.float32)),
    )


def _ssm_kernel(u_ref, lag_ref, bxr_ref, bxi_ref, cxr_ref, cxi_ref, alr_ref, ali_ref, d_ref,
                sc_ref, kc_ref, sl_ref, kl_ref, o_ref,
                ub_ref, bcr_ref, bci_ref, ocr_ref, oci_ref, zr_ref, zi_ref, hr_ref, hi_ref, y_ref, *, bsz):
    L = SSM_CHUNK
    rows = u_ref.shape[0] // L
    chunks = rows // bsz
    tile = 2 * LANES
    step_rows = lambda r: pl.ds(r, rows, stride=L)
    for r in range(L):
        ub_ref[:, r * LANES:(r + 1) * LANES] = u_ref[step_rows(r), :].astype(BF16)
    state_group = lax.broadcasted_iota(jnp.int32, (1, bxr_ref.shape[1]), 1) // SSM_STATE
    for compact_ref, full_ref in ((bxr_ref, bcr_ref), (bxi_ref, bci_ref)):
        for r in range(L):
            blk = compact_ref[r * SSM_GROUP:(r + 1) * SSM_GROUP, :]
            for grp in range(GROUPS_PER_TILE):
                first = r * LANES + grp * SSM_GROUP
                full_ref[first:first + SSM_GROUP, :] = jnp.where(state_group == grp, blk, jnp.zeros_like(blk))
    zr_ref[...] = _dot(ub_ref[...], bcr_ref[...])
    zi_ref[...] = _dot(ub_ref[...], bci_ref[...])
    a_re, a_im = alr_ref[...], ali_ref[...]

    def step(k, carry):
        new = []
        for b in range(bsz):
            h_re, h_im = carry[2 * b], carry[2 * b + 1]
            row = pl.ds(b * chunks + k, 1)
            hr_ref[row, :] = h_re
            hi_ref[row, :] = h_im
            new.append(a_re * h_re - a_im * h_im + zr_ref[row, :])
            new.append(a_re * h_im + a_im * h_re + zi_ref[row, :])
        return tuple(new)

    zero = jnp.zeros((1, a_re.shape[1]), F32)
    lax.fori_loop(0, chunks, step, (zero,) * (2 * bsz))

    ocr_ref[...] = (_dot(cxr_ref[...], sc_ref[...]) * kc_ref[...]).astype(BF16)
    oci_ref[...] = (_dot(cxi_ref[...], sc_ref[...]) * kc_ref[...]).astype(BF16)
    lag = [(_dot(sl_ref[...], lag_ref[d]) * kl_ref[...]).astype(BF16) for d in range(L)]
    lag_tile = []
    for dd in range(L // 2):
        below = lag[2 * dd - 1] if dd > 0 else jnp.zeros((LANES, LANES), BF16)
        lag_tile.append(jnp.concatenate([jnp.concatenate([lag[2 * dd], lag[2 * dd + 1]], axis=1),
                                         jnp.concatenate([below, lag[2 * dd]], axis=1)], axis=0))
    hb_re, hb_im = hr_ref[...].astype(BF16), hi_ref[...].astype(BF16)
    for t2 in range(L // 2):
        cols = slice(t2 * tile, (t2 + 1) * tile)
        y = _dot(hb_re, ocr_ref[:, cols]) + _dot(hb_im, oci_ref[:, cols])
        for r2 in range(t2 + 1):
            y = y + _dot(ub_ref[:, r2 * tile:(r2 + 1) * tile], lag_tile[t2 - r2])
        for half in range(2):
            t = 2 * t2 + half
            y_ref[step_rows(t), :] = jax.nn.gelu(y[:, half * LANES:(half + 1) * LANES] + d_ref[...] * u_ref[step_rows(t), :])

    def chunk(rows):
        o_ref[rows, :] = y_ref[rows, :].astype(o_ref.dtype)
    _for_row_chunks(o_ref.shape[0], chunk)


def _ssm_core(proj, ops, bsz, seq):
    m = proj.shape[0]
    width = proj.shape[1] // 2
    L, C = SSM_CHUNK, SSM_GROUP
    n_oct = width // LANES
    rows = m // L
    ow = L * LANES
    n_state = GROUPS_PER_TILE * SSM_STATE
    ex = _ssm_expanders()
    per_oct = lambda *shape: pl.BlockSpec((None,) + shape, lambda p: (p,) + (0,) * len(shape))
    const = lambda a: pl.BlockSpec(a.shape, lambda p: (0,) * a.ndim, pipeline_mode=pl.Buffered(1))
    consts = [ex['spread_cols'], ex['keep_cols'], ex['spread_lag'], ex['keep_lag']]
    return pl.pallas_call(
        functools.partial(_ssm_kernel, bsz=bsz),
        grid=(n_oct,),
        in_specs=[
            pl.BlockSpec((m, LANES), lambda p: (0, p)),
            per_oct(L, C, LANES), per_oct(L * C, n_state), per_oct(L * C, n_state),
            per_oct(n_state, L * C), per_oct(n_state, L * C), per_oct(1, n_state), per_oct(1, n_state), per_oct(1, LANES),
        ] + [const(a) for a in consts],
        out_specs=pl.BlockSpec((m, LANES), lambda p: (0, p)),
        out_shape=jax.ShapeDtypeStruct((m, width), BF16),
        scratch_shapes=[pltpu.VMEM((rows, ow), BF16)] + [pltpu.VMEM((ow, n_state), BF16)] * 2
        + [pltpu.VMEM((n_state, ow), BF16)] * 2 + [pltpu.VMEM((rows, n_state), F32)] * 4 + [pltpu.VMEM((m, LANES), F32)],
        compiler_params=_params(1),
        name="ssm_core",
    )(proj, ops['lag'], ops['bc_re'], ops['bc_im'], ops['oc_re'], ops['oc_im_neg'], ops['al_re'], ops['al_im'], ops['d'], *consts)


def _glu_kernel(y_ref, wa_ref, wb_ref, ba_ref, bb_ref, gate_ref, o_ref):
    y = y_ref[...]
    ga = _dot(y, wa_ref[...]) + ba_ref[...]
    gb = _dot(y, wb_ref[...]) + bb_ref[...]
    o_ref[...] = ((ga * jax.nn.sigmoid(gb)) * _silu(gate_ref[...])).astype(o_ref.dtype)


def _glu(y, w_glu, layer, b_glu, proj, tm=1024, tn=512):
    m, k = y.shape
    width = w_glu.shape[2] // 2
    tm, tn = min(tm, m), min(tn, width)
    nt = width // tn
    w_tiles = _tile_cols(w_glu, layer, tn)
    return pl.pallas_call(
        _glu_kernel,
        grid=(m // tm, nt),
        in_specs=[
            pl.BlockSpec((tm, k), lambda i, j: (i, 0)),
            pl.BlockSpec((None, k, tn), lambda i, j: (j, 0, 0)),
            pl.BlockSpec((None, k, tn), lambda i, j: (nt + j, 0, 0)),
            pl.BlockSpec((1, tn), lambda i, j: (0, j)),
            pl.BlockSpec((1, tn), lambda i, j: (0, nt + j)),
            pl.BlockSpec((tm, tn), lambda i, j: (i, nt + j)),
        ],
        out_specs=pl.BlockSpec((tm, tn), lambda i, j: (i, j)),
        out_shape=jax.ShapeDtypeStruct((m, width), BF16),
        compiler_params=_params(2),
        name="ssm_glu",
    )(y, w_tiles, w_tiles, b_glu, b_glu, proj)


def _ssm_mixer(x, g, w_in, layer, lam_re, lam_im, log_dt, b_re, b_im, c_re, c_im, d_skip, w_glu, b_glu, bsz, seq):
    proj = _norm_matmul(x, g, _tile_cols(w_in, layer, MATMUL_TN), F32)
    ops = _ssm_operators(lam_re, lam_im, log_dt, b_re, b_im, c_re, c_im, d_skip)
    y = _ssm_core(proj, ops, bsz, seq)
    return _glu(y, w_glu, layer, b_glu.astype(F32).reshape(1, -1), proj)


CUM_BLOCK = 128


def _split3(x):
    x1 = x.astype(BF16)
    r1 = x - x1.astype(F32)
    x2 = r1.astype(BF16)
    x3 = (r1 - x2.astype(F32)).astype(BF16)
    return x1, x2, x3


def _forget_cumsum_kernel(z_ref, b_ref, o_ref):
    n_blocks = z_ref.shape[0] // CUM_BLOCK
    ri = lax.broadcasted_iota(jnp.int32, (CUM_BLOCK, CUM_BLOCK), 0)
    ci = lax.broadcasted_iota(jnp.int32, (CUM_BLOCK, CUM_BLOCK), 1)
    tri = jnp.where(ci <= ri, 1.0, 0.0).astype(BF16)

    def body(i, carry):
        rows = pl.ds(pl.multiple_of(i * CUM_BLOCK, CUM_BLOCK), CUM_BLOCK)
        z = z_ref[rows, :] + b_ref[...]
        log_f = jnp.minimum(z, 0.0) - jnp.log1p(jnp.exp(-jnp.abs(z)))
        x1, x2, x3 = _split3(log_f)
        c = _dot(tri, x1) + _dot(tri, x2) + _dot(tri, x3) + carry
        o_ref[rows, :] = c
        return c[CUM_BLOCK - 1:, :]

    lax.fori_loop(0, n_blocks, body, jnp.zeros((1, z_ref.shape[1]), F32))


def _forget_cumsum(z, b, bsz, seq):
    lanes = z.shape[1]
    return pl.pallas_call(
        _forget_cumsum_kernel,
        grid=(bsz,),
        in_specs=[pl.BlockSpec((seq, lanes), lambda i: (i, 0)), pl.BlockSpec((1, lanes), lambda i: (0, 0))],
        out_specs=pl.BlockSpec((seq, lanes), lambda i: (i, 0)),
        out_shape=jax.ShapeDtypeStruct(z.shape, F32),
        compiler_params=_params(1),
        name="forget_cumsum",
    )(z, b)


FOX_STEP_HEADS = 8


def _fox_kernel(q_ref, k_ref, v_ref, sg_ref, cq_ref, ck_ref, o_ref, ka_ref, vt_ref, *head_refs, tk):
    nh = FOX_STEP_HEADS
    hg, qi = pl.program_id(1), pl.program_id(2)
    tq = q_ref.shape[0]
    qa_ref, s_ref, p_ref, acc_ref, pv_ref, m_ref = (head_refs[i::6] for i in range(6))
    lane = lax.broadcasted_iota(jnp.int32, (1, LANES), 1)
    free = (HEAD_DIM, 0)
    n_pieces = 3
    pair_lanes = lambda h: slice(h // 2 * LANES, (h // 2 + 1) * LANES)

    def bias_lanes(c, h, c_offset, one_offset):
        rr = lax.broadcasted_iota(jnp.int32, (n_pieces * LANES, LANES), 0)
        cc = lax.broadcasted_iota(jnp.int32, (n_pieces * LANES, LANES), 1)
        place = jnp.logical_and(rr % LANES == nh * hg + h, cc == free[h % 2] + c_offset + rr // LANES)
        placed = _dot(jnp.concatenate(_split3(c), axis=1), jnp.where(place, 1.0, 0.0).astype(BF16))
        first = free[h % 2] + one_offset
        return (placed + jnp.where(jnp.logical_and(lane >= first, lane < first + n_pieces), 1.0, 0.0)).astype(BF16)

    @pl.when(qi == 0)
    def _():
        def chunk(j, carry):
            rows = pl.ds(pl.multiple_of(j * tk, tk), tk)
            neg_c = ck_ref[rows, :] * -LOG2E
            for h in range(nh):
                keys = k_ref[rows, pair_lanes(h)]
                ka_ref[h, j] = jnp.where((lane // HEAD_DIM) == h % 2, keys, bias_lanes(neg_c, h, 0, n_pieces))
            for pair in range(nh // 2):
                v_t = v_ref[rows, pair_lanes(2 * pair)].astype(F32).T.astype(BF16)
                for hh in range(2):
                    vt_ref[j, 2 * pair + hh, :HEAD_DIM, :] = v_t[hh * HEAD_DIM:(hh + 1) * HEAD_DIM]
                    vt_ref[j, 2 * pair + hh, HEAD_DIM:, :] = jnp.ones((ONES_ROWS, tk), BF16)
            return carry
        lax.fori_loop(0, k_ref.shape[0] // tk, chunk, 0)

    cq = cq_ref[...] * LOG2E
    for h in range(nh):
        qa = jnp.where((lane // HEAD_DIM) == h % 2, q_ref[:, pair_lanes(h)], bias_lanes(cq, h, n_pieces, 0))
        qa_ref[h][...] = qa.astype(F32).T.astype(BF16)

    def scores(h, j):
        s_ref[h][...] = _dot(ka_ref[h, j], qa_ref[h][...])

    rows8 = 8
    rows16 = 16

    def absorb(h, j, diagonal):
        def strip(r, n):
            blk = s_ref[h][r:r + n, :]
            if diagonal:
                kpos = lax.broadcasted_iota(jnp.int32, (n, tq), 0)
                qpos = lax.broadcasted_iota(jnp.int32, (n, tq), 1)
                blk = jnp.where(kpos + r <= qpos, blk, NEG)
            return blk
        top = strip(0, rows8)
        for r in range(rows8, tk, rows8):
            top = jnp.maximum(top, strip(r, rows8))
        m = m_ref[h][...]
        m_new = jnp.maximum(m, top.max(axis=0, keepdims=True))
        m_ref[h][...] = m_new
        alpha = jnp.exp2(m - m_new)
        m_rows = jnp.broadcast_to(m_new, (rows16, tq))
        for r in range(0, tk, rows16):
            p_ref[h][r:r + rows16, :] = jnp.exp2(strip(r, rows16) - m_rows).astype(BF16)
        acc_ref[h][...] = alpha * (acc_ref[h][...] + pv_ref[h][...])

    def values(h, j):
        pv_ref[h][...] = _dot(vt_ref[j, h], p_ref[h][...])

    def step(j, diagonal):
        values(nh - 1, jnp.maximum(j - 1, 0))
        scores(1, j)
        for h in range(nh):
            absorb(h, j, diagonal)
            if h < nh - 1:
                values(h, j)
            if h + 2 < nh:
                scores(h + 2, j)
            elif h + 2 == nh and not diagonal:
                scores(0, j + 1)

    for h in range(nh):
        acc_ref[h][...] = jnp.zeros(acc_ref[h].shape, F32)
        pv_ref[h][...] = jnp.zeros(pv_ref[h].shape, F32)
        m_ref[h][...] = jnp.full(m_ref[h].shape, NEG, F32)
    p_ref[nh - 1][...] = jnp.zeros(p_ref[nh - 1].shape, BF16)
    scores(0, 0)

    def two_blocks(i, carry):
        step(2 * i, False)
        step(2 * i + 1, False)
        return carry
    lax.fori_loop(0, qi // 2, two_blocks, 0)

    @pl.when(qi % 2 == 0)
    def _():
        step(qi, True)

    @pl.when(qi % 2 == 1)
    def _():
        step(qi - 1, False)
        step(qi, True)

    values(nh - 1, qi)
    outs = []
    for h in range(nh):
        total = acc_ref[h][...] + pv_ref[h][...]
        outs.append(total[:HEAD_DIM] / total[HEAD_DIM:HEAD_DIM + 1])
    out = jnp.concatenate(outs, axis=0)
    o_ref[...] = (out.T * sg_ref[...].astype(F32)).astype(o_ref.dtype)


def _fox_attention(proj, csum, bsz, seq, t=512):
    width = FOX_HEADS * HEAD_DIM
    nh = FOX_STEP_HEADS
    gw = nh * HEAD_DIM
    n_groups = width // gw
    t = min(t, seq)
    nq = seq // t
    return pl.pallas_call(
        functools.partial(_fox_kernel, tk=t),
        grid=(bsz, n_groups, nq),
        in_specs=[
            pl.BlockSpec((t, gw), lambda b, h, i: (b * nq + i, h)),
            pl.BlockSpec((seq, gw), lambda b, h, i: (b, n_groups + h), pipeline_mode=pl.Buffered(1)),
            pl.BlockSpec((seq, gw), lambda b, h, i: (b, 2 * n_groups + h), pipeline_mode=pl.Buffered(1)),
            pl.BlockSpec((t, gw), lambda b, h, i: (b * nq + i, 3 * n_groups + h)),
            pl.BlockSpec((t, LANES), lambda b, h, i: (b * nq + i, 0)),
            pl.BlockSpec((seq, LANES), lambda b, h, i: (b, 0), pipeline_mode=pl.Buffered(1)),
        ],
        out_specs=pl.BlockSpec((t, gw), lambda b, h, i: (b * nq + i, h)),
        out_shape=jax.ShapeDtypeStruct((bsz * seq, width), BF16),
        scratch_shapes=[
            pltpu.VMEM((nh, nq, t, LANES), BF16), pltpu.VMEM((nq, nh, HEAD_DIM + ONES_ROWS, t), BF16),
        ] + nh * [pltpu.VMEM((LANES, t), BF16), pltpu.VMEM((t, t), F32), pltpu.VMEM((t, t), BF16),
                  pltpu.VMEM((HEAD_DIM + ONES_ROWS, t), F32), pltpu.VMEM((HEAD_DIM + ONES_ROWS, t), F32),
                  pltpu.VMEM((1, t), F32)],
        compiler_params=_params(3, VMEM_LIMIT_BIG),
        name="fox_attention",
    )(proj, proj, proj, proj, csum, csum)


def _fox_mixer(x, g, w_in, layer, w_fg, b_fg, bsz, seq, t=512):
    width = FOX_HEADS * HEAD_DIM
    col_scale = jnp.where(jnp.arange(4 * width) < width, LOG2E * HEAD_DIM ** -0.5, 1.0).astype(F32)
    proj = _norm_matmul(x, g, _tile_cols(w_in, layer, MATMUL_TN, col_scale), BF16, silu_cols=(3 * width, 4 * width))
    pad = LANES - FOX_HEADS
    z = _norm_matmul(x, g, _tile_cols_xla(jnp.pad(w_fg, ((0, 0), (0, pad))), LANES), F32)
    csum = _forget_cumsum(z, jnp.pad(b_fg.astype(F32), (0, pad)).reshape(1, LANES), bsz, seq)
    return _fox_attention(proj, csum, bsz, seq, t)


def kernel(x, p, norm_g, final_g, rel_bias, swa_w_in, swa_w_out, swa_sinks, conv_w_in, conv_kernel, conv_w_out, ssm_w_in, ssm_lam_re, ssm_lam_im, ssm_log_dt, ssm_b_re, ssm_b_im, ssm_c_re, ssm_c_im, ssm_d, ssm_w_glu, ssm_b_glu, ssm_w_out, fox_w_in, fox_w_fg, fox_b_fg, fox_w_out, ple_proj, ple_norm, ple_gate):
    bsz, seq, d_model = x.shape
    depth = p.shape[0]
    h = x.astype(F32).reshape(bsz * seq, d_model)
    for i in range(depth):
        mixer, j = i % N_MIXERS, i // N_MIXERS
        if mixer == 0:
            a, w_out = _swa_mixer(h, norm_g[i], swa_w_in, j, swa_sinks[j], rel_bias, bsz, seq), swa_w_out[j]
        elif mixer == 1:
            a, w_out = _conv_mixer(h, norm_g[i], conv_w_in, j, conv_kernel[j], seq), conv_w_out[j]
        elif mixer == 2:
            a = _ssm_mixer(h, norm_g[i], ssm_w_in, j, ssm_lam_re[j], ssm_lam_im[j], ssm_log_dt[j], ssm_b_re[j], ssm_b_im[j],
                           ssm_c_re[j], ssm_c_im[j], ssm_d[j], ssm_w_glu, ssm_b_glu[j], bsz, seq)
            w_out = ssm_w_out[j]
        else:
            a, w_out = _fox_mixer(h, norm_g[i], fox_w_in, j, fox_w_fg[j], fox_b_fg[j], bsz, seq), fox_w_out[j]
        h = _out_ple(a, w_out, h, ple_norm[i], ple_gate, p.reshape(depth, bsz * seq, -1), ple_proj, i,
                     final_g=final_g if i == depth - 1 else None)
    return h.reshape(bsz, seq, d_model).astype(x.dtype)
```

```python
import functools
import math

import numpy as np
import jax
import jax.numpy as jnp
from jax import lax
from jax.experimental import pallas as pl
from jax.experimental.pallas import tpu as pltpu

F32 = jnp.float32
BF16 = jnp.bfloat16

EPS = 1e-6
N_MIXERS = 4

SWA_HEADS = 32
SWA_KV_HEADS = 4
SWA_GROUP = SWA_HEADS // SWA_KV_HEADS
HEAD_DIM = 64
SWA_BLOCK = 128
WINDOW = 128
REL_BUCKETS = 32
REL_MAX_DIST = 128

CONV_TAPS = 3

SSM_GROUP = 16
SSM_STATE = 64
SSM_CHUNK = 16

FOX_HEADS = 32

LANES = 128
ONES_ROWS = 16
GROUPS_PER_TILE = LANES // SSM_GROUP
VMEM_LIMIT = 48 * 1024 * 1024
VMEM_LIMIT_BIG = 60 * 1024 * 1024

NEG = float(jnp.finfo(jnp.float32).min)
LOG2E = math.log2(math.e)


def _params(n_axes, vmem_limit=VMEM_LIMIT):
    return pltpu.CompilerParams(dimension_semantics=("arbitrary",) * n_axes, vmem_limit_bytes=vmem_limit)


def _dot(a, b):
    return jnp.dot(a, b, preferred_element_type=F32)


def _dot_nt(a, b):
    return lax.dot_general(a, b, (((1,), (1,)), ((), ())), preferred_element_type=F32)


def _rmsnorm_rows(x, g):
    return x * lax.rsqrt(jnp.mean(x * x, axis=-1, keepdims=True) + EPS) * g


def _silu(x):
    return x * jax.nn.sigmoid(x)


ROW_CHUNK = 64


def _for_row_chunks(n_rows, fn):
    def body(c, carry):
        fn(pl.ds(pl.multiple_of(c * ROW_CHUNK, ROW_CHUNK), ROW_CHUNK))
        return carry
    lax.fori_loop(0, n_rows // ROW_CHUNK, body, 0)


def _tile_cols_xla(w, tn):
    k, n = w.shape
    return w.astype(BF16).reshape(k, n // tn, tn).transpose(1, 0, 2)


def _cast_tiles_kernel(w_ref, o_ref):
    def chunk(rows):
        o_ref[rows, :] = w_ref[rows, :].astype(BF16)
    _for_row_chunks(o_ref.shape[0], chunk)


def _scale_cast_tiles_kernel(w_ref, s_ref, o_ref):
    def chunk(rows):
        o_ref[rows, :] = (w_ref[rows, :] * s_ref[...]).astype(BF16)
    _for_row_chunks(o_ref.shape[0], chunk)


def _tile_cols(w, layer, tn, col_scale=None):
    _, k, n = w.shape
    tn = min(tn, n)
    w_spec = pl.BlockSpec((None, k, tn), lambda j: (layer, 0, j))
    common = dict(
        grid=(n // tn,),
        out_specs=pl.BlockSpec((None, k, tn), lambda j: (j, 0, 0)),
        out_shape=jax.ShapeDtypeStruct((n // tn, k, tn), BF16),
        compiler_params=_params(1),
        name="weight_tiles",
    )
    if col_scale is None:
        return pl.pallas_call(_cast_tiles_kernel, in_specs=[w_spec], **common)(w)
    s_spec = pl.BlockSpec((1, tn), lambda j: (0, j))
    return pl.pallas_call(_scale_cast_tiles_kernel, in_specs=[w_spec, s_spec], **common)(w, col_scale.reshape(1, n))


def _lookahead_x_spec(tm, k, n_row_tiles):
    def tile(i, j):
        first_step = jnp.logical_and(i == 0, j == 0)
        return jnp.where(first_step, 0, jnp.minimum(i + 1, n_row_tiles - 1))
    return pl.BlockSpec((tm, k), lambda i, j: (tile(i, j), 0))


def _lookahead_rows(tm, nt):
    return -(-tm // ((nt - 1) * ROW_CHUNK)) * ROW_CHUNK


def _lookahead_rmsnorm(x_ref, g_ref, a_ref, rows_per_step):
    i, j = pl.program_id(0), pl.program_id(1)
    tm = a_ref.shape[1]

    def normalize(slot, rows):
        a_ref[slot, rows, :] = _rmsnorm_rows(x_ref[rows, :], g_ref[...]).astype(BF16)

    @pl.when(jnp.logical_and(i == 0, j == 0))
    def _():
        _for_row_chunks(tm, lambda rows: normalize(0, rows))

    start = jnp.clip((j - 1) * rows_per_step, 0, tm - rows_per_step)
    for c in range(rows_per_step // ROW_CHUNK):
        normalize((i + 1) % 2, pl.ds(pl.multiple_of(start + c * ROW_CHUNK, ROW_CHUNK), ROW_CHUNK))


def _norm_matmul_kernel(x_ref, g_ref, w_ref, o_ref, a_ref, *, silu_tiles, rows_per_step):
    i, j = pl.program_id(0), pl.program_id(1)
    if rows_per_step is None:
        def chunk(rows):
            a_ref[0, rows, :] = _rmsnorm_rows(x_ref[rows, :], g_ref[...]).astype(BF16)
        _for_row_chunks(a_ref.shape[1], chunk)
        acc = _dot(a_ref[0], w_ref[...])
    else:
        _lookahead_rmsnorm(x_ref, g_ref, a_ref, rows_per_step)
        acc = _dot(a_ref[i % 2], w_ref[...])
    if silu_tiles is None:
        o_ref[...] = acc.astype(o_ref.dtype)
    else:
        gated = jnp.logical_and(j >= silu_tiles[0], j < silu_tiles[1])

        @pl.when(gated)
        def _():
            o_ref[...] = _silu(acc).astype(o_ref.dtype)

        @pl.when(jnp.logical_not(gated))
        def _():
            o_ref[...] = acc.astype(o_ref.dtype)


MATMUL_TN = 1024


def _norm_matmul(x, g, w_tiles, out_dtype, silu_cols=None, tm=1024):
    m, k = x.shape
    nt, _, tn = w_tiles.shape
    n = nt * tn
    tm = min(tm, m)
    n_row_tiles = m // tm
    silu_tiles = None if silu_cols is None else (silu_cols[0] // tn, silu_cols[1] // tn)
    lookahead = nt > 1
    return pl.pallas_call(
        functools.partial(_norm_matmul_kernel, silu_tiles=silu_tiles,
                          rows_per_step=_lookahead_rows(tm, nt) if lookahead else None),
        grid=(n_row_tiles, nt),
        in_specs=[
            _lookahead_x_spec(tm, k, n_row_tiles) if lookahead else pl.BlockSpec((tm, k), lambda i, j: (i, 0)),
            pl.BlockSpec((1, k), lambda i, j: (0, 0)),
            pl.BlockSpec((None, k, tn), lambda i, j: (j, 0, 0)),
        ],
        out_specs=pl.BlockSpec((tm, tn), lambda i, j: (i, j)),
        out_shape=jax.ShapeDtypeStruct((m, n), out_dtype),
        scratch_shapes=[pltpu.VMEM((2, tm, k), BF16)],
        compiler_params=_params(2, VMEM_LIMIT_BIG),
        name="norm_matmul",
    )(x, g.reshape(1, k), w_tiles)


def _out_ple_kernel(a_ref, wo_ref, x_ref, g_ref, wg_ref, p_ref, wp_ref, gf_ref, o_ref, x1_ref, hn_ref, ss_ref, *, nt, final):
    j = pl.program_id(1)
    tn = x_ref.shape[1]
    n = nt * tn

    @pl.when(j == 0)
    def _():
        ss_ref[...] = jnp.zeros(ss_ref.shape, F32)

    @pl.when(j < nt)
    def _():
        x1 = x_ref[...] + _dot(a_ref[...], wo_ref[j])
        x1_ref[j] = x1
        hn_ref[j] = (x1 * g_ref[j]).astype(BF16)
        ss_ref[0] += jnp.sum(x1 * x1, axis=-1, keepdims=True)

    @pl.when(jnp.logical_and(j >= nt, j < 2 * nt))
    def _():
        t = j - nt
        emb = _dot(p_ref[...].astype(BF16), wp_ref[t])
        acc = _dot(hn_ref[0], wg_ref[t, :tn, :])
        for kt in range(1, nt):
            acc = acc + _dot(hn_ref[kt], wg_ref[t, kt * tn:(kt + 1) * tn, :])
        x2 = x1_ref[t] + emb * jax.nn.sigmoid(acc * lax.rsqrt(ss_ref[0] / n + EPS))
        if final:
            x1_ref[t] = x2
            ss_ref[1] += jnp.sum(x2 * x2, axis=-1, keepdims=True)
        else:
            o_ref[...] = x2

    if final:
        @pl.when(j >= 2 * nt)
        def _():
            t = j - 2 * nt
            o_ref[...] = x1_ref[t] * lax.rsqrt(ss_ref[1] / n + EPS) * gf_ref[t]


def _out_ple(a, w_out, x, g, w_gate, p, w_proj, layer, final_g=None, tm=1024, tn=512):
    m, k = a.shape
    n = w_out.shape[1]
    pd = p.shape[2]
    final = final_g is not None
    tm, tn = min(tm, m), min(tn, n)
    nt = n // tn
    passes = 3 if final else 2
    out_tile = lambda j: jnp.maximum(j - (passes - 1) * nt, 0)
    resident = lambda rows: pl.BlockSpec((nt, rows, tn), lambda i, j: (0, 0, 0), pipeline_mode=pl.Buffered(1))
    row_vec = pl.BlockSpec((nt, 1, tn), lambda i, j: (0, 0, 0))
    return pl.pallas_call(
        functools.partial(_out_ple_kernel, nt=nt, final=final),
        grid=(m // tm, passes * nt),
        in_specs=[
            pl.BlockSpec((tm, k), lambda i, j: (i, 0)),
            resident(k),
            pl.BlockSpec((tm, tn), lambda i, j: (i, jnp.minimum(j, nt - 1))),
            row_vec,
            resident(n),
            pl.BlockSpec((None, tm, pd), lambda i, j: (layer, i, 0)),
            resident(pd),
            row_vec,
        ],
        out_specs=pl.BlockSpec((tm, tn), lambda i, j: (i, out_tile(j))),
        out_shape=jax.ShapeDtypeStruct((m, n), F32),
        scratch_shapes=[pltpu.VMEM((nt, tm, tn), F32), pltpu.VMEM((nt, tm, tn), BF16), pltpu.VMEM((2, tm, 1), F32)],
        compiler_params=_params(2, VMEM_LIMIT_BIG),
        name="out_ple_final" if final else "out_ple",
    )(a, _tile_cols(w_out[None], 0, tn), x, g.reshape(nt, 1, tn), _tile_cols(w_gate, layer, tn), p,
      _tile_cols(w_proj, layer, tn), (final_g if final else g).reshape(nt, 1, tn))


def _t5_bucket(dist):
    max_exact = REL_BUCKETS // 2
    d = np.maximum(dist, 1).astype(np.float32)
    large = max_exact + (np.log(d / max_exact) / np.log(REL_MAX_DIST / max_exact) * (REL_BUCKETS - max_exact)).astype(np.int32)
    large = np.minimum(large, REL_BUCKETS - 1)
    return np.where(dist < max_exact, dist, large).astype(np.int32)


assert WINDOW == SWA_BLOCK


def _swa_bucket_table():
    qi = np.arange(SWA_BLOCK)[None, :]
    kj = np.arange(2 * SWA_BLOCK)[:, None]
    return _t5_bucket(np.clip(qi + SWA_BLOCK - kj, 0, None))


def _swa_kernel(sink_ref, rel_ref, bucket_ref, q_ref, sg_ref, kp_ref, kc_ref, vp_ref, vc_ref, o_ref, bias_ref, qs_ref):
    n = pl.program_id(1)
    blk = SWA_BLOCK
    lane = lax.broadcasted_iota(jnp.int32, (1, LANES), 1)

    @pl.when(jnp.logical_and(pl.program_id(0) == 0, n == 0))
    def _():
        bucket = bucket_ref[...]
        kj = lax.broadcasted_iota(jnp.int32, (2 * blk, blk), 0)
        qi = lax.broadcasted_iota(jnp.int32, (2 * blk, blk), 1)
        band = jnp.logical_or(jnp.logical_and(kj < blk, kj > qi), jnp.logical_and(kj >= blk, kj - blk <= qi))

        def per_head(head, carry):
            acc = jnp.zeros((2 * blk, blk), F32)
            for b in range(REL_BUCKETS):
                acc = jnp.where(bucket == b, rel_ref[b, head] * LOG2E, acc)
            bias_ref[head] = jnp.where(band, acc, NEG)
            return carry
        lax.fori_loop(0, SWA_HEADS, per_head, 0)

    for kvh in range(SWA_KV_HEADS):
        cols = slice(kvh * LANES, (kvh + 1) * LANES)
        heads = range(kvh * SWA_GROUP, (kvh + 1) * SWA_GROUP)
        for g, head in enumerate(heads):
            q_pair = q_ref[:, head // 2 * LANES:(head // 2 + 1) * LANES]
            qs_ref[g * blk:(g + 1) * blk, :] = jnp.where((lane // HEAD_DIM) == head % 2, q_pair, jnp.zeros_like(q_pair))
        keys = jnp.concatenate([kp_ref[:, cols], kc_ref[:, cols]], axis=0)
        vals = jnp.concatenate([vp_ref[:, cols], vc_ref[:, cols]], axis=0)
        s = _dot_nt(keys, qs_ref[...]) + jnp.concatenate([bias_ref[head] for head in heads], axis=1)
        s_prev = jnp.where(n > 0, s[:blk], NEG)
        s_cur = s[blk:]
        sink = jnp.concatenate([jnp.full((1, blk), sink_ref[head] * LOG2E, F32) for head in heads], axis=1)
        m = jnp.maximum(jnp.maximum(s_prev, s_cur).max(axis=0, keepdims=True), sink)
        e = jnp.concatenate([jnp.exp2(s_prev - m).astype(BF16), jnp.exp2(s_cur - m).astype(BF16)], axis=0)
        v_t = jnp.concatenate([vals.astype(F32).T.astype(BF16)[:HEAD_DIM], jnp.ones((ONES_ROWS, 2 * blk), BF16)], axis=0)
        out = _dot(v_t, e)
        out = out[:HEAD_DIM] / (out[HEAD_DIM:HEAD_DIM + 1] + jnp.exp2(sink - m))
        for pair in range(SWA_GROUP // 2):
            head = kvh * SWA_GROUP + 2 * pair
            even = out[:, 2 * pair * blk:(2 * pair + 1) * blk]
            odd = out[:, (2 * pair + 1) * blk:(2 * pair + 2) * blk]
            cols = slice(head // 2 * LANES, (head // 2 + 1) * LANES)
            o_ref[:, cols] = (jnp.concatenate([even, odd], axis=0).T * sg_ref[:, cols].astype(F32)).astype(o_ref.dtype)


def _swa_attention(proj, sinks, rel_bias, bsz, seq):
    nb = seq // SWA_BLOCK
    width = SWA_HEADS * HEAD_DIM
    kvw = SWA_KV_HEADS * LANES
    k_blk = 2 * width // kvw
    cur = lambda b, n: b * nb + n
    prev = lambda b, n: b * nb + jnp.maximum(n - 1, 0)
    smem = pl.BlockSpec(memory_space=pltpu.SMEM)
    return pl.pallas_call(
        _swa_kernel,
        grid=(bsz, nb),
        in_specs=[
            smem, smem,
            pl.BlockSpec((2 * SWA_BLOCK, SWA_BLOCK), lambda b, n: (0, 0)),
            pl.BlockSpec((SWA_BLOCK, width), lambda b, n: (cur(b, n), 0)),
            pl.BlockSpec((SWA_BLOCK, width), lambda b, n: (cur(b, n), 1)),
            pl.BlockSpec((SWA_BLOCK, kvw), lambda b, n: (prev(b, n), k_blk)),
            pl.BlockSpec((SWA_BLOCK, kvw), lambda b, n: (cur(b, n), k_blk)),
            pl.BlockSpec((SWA_BLOCK, kvw), lambda b, n: (prev(b, n), k_blk + 1)),
            pl.BlockSpec((SWA_BLOCK, kvw), lambda b, n: (cur(b, n), k_blk + 1)),
        ],
        out_specs=pl.BlockSpec((SWA_BLOCK, width), lambda b, n: (cur(b, n), 0)),
        out_shape=jax.ShapeDtypeStruct((bsz * seq, width), BF16),
        scratch_shapes=[
            pltpu.VMEM((SWA_HEADS, 2 * SWA_BLOCK, SWA_BLOCK), F32),
            pltpu.VMEM((SWA_GROUP * SWA_BLOCK, LANES), BF16),
        ],
        compiler_params=_params(2),
        name="swa_attention",
    )(sinks.astype(F32), rel_bias.astype(F32), jnp.asarray(_swa_bucket_table()), proj, proj, proj, proj, proj, proj)


SWA_HALF_TILE = 512


def _swa_weight_tiles_kernel(wl_ref, wr_ref, o_ref, *, q_tiles, gate_tiles, q_scale):
    j = pl.program_id(0)
    half = SWA_HALF_TILE
    kvw = SWA_KV_HEADS * HEAD_DIM

    def rows_do(fn):
        def chunk(rows):
            for side, w_ref in enumerate((wl_ref, wr_ref)):
                o_ref[rows, side * half:(side + 1) * half] = fn(side, w_ref, rows)
        _for_row_chunks(o_ref.shape[0], chunk)

    @pl.when(j < q_tiles)
    def _():
        rows_do(lambda side, w_ref, rows: (w_ref[rows, :] * q_scale).astype(BF16))

    @pl.when(jnp.logical_and(j >= q_tiles, j < q_tiles + gate_tiles))
    def _():
        rows_do(lambda side, w_ref, rows: w_ref[rows, :].astype(BF16))

    @pl.when(j == q_tiles + gate_tiles)
    def _():
        def twice(side, w_ref, rows):
            w = w_ref[rows, side * kvw:(side + 1) * kvw].astype(BF16)
            heads = [w[:, h * HEAD_DIM:(h + 1) * HEAD_DIM] for h in range(SWA_KV_HEADS)]
            return jnp.concatenate([piece for head in heads for piece in (head, head)], axis=1)
        rows_do(twice)


def _swa_weight_tiles(w_in, layer, q_scale):
    _, k, n = w_in.shape
    half = SWA_HALF_TILE
    width = SWA_HEADS * HEAD_DIM
    q_tiles = gate_tiles = width // (2 * half)
    kv_block = width // half
    n_out = q_tiles + gate_tiles + 1

    def src(j, side):
        in_q, in_gate = j < q_tiles, j < q_tiles + gate_tiles
        return jnp.where(in_q, 2 * j + side, jnp.where(in_gate, 2 * j + side + 1, kv_block))
    return pl.pallas_call(
        functools.partial(_swa_weight_tiles_kernel, q_tiles=q_tiles, gate_tiles=gate_tiles, q_scale=q_scale),
        grid=(n_out,),
        in_specs=[pl.BlockSpec((None, k, half), lambda j: (layer, 0, src(j, 0))),
                  pl.BlockSpec((None, k, half), lambda j: (layer, 0, src(j, 1)))],
        out_specs=pl.BlockSpec((None, k, 2 * half), lambda j: (j, 0, 0)),
        out_shape=jax.ShapeDtypeStruct((n_out, k, 2 * half), BF16),
        compiler_params=_params(1),
        name="swa_weight_tiles",
    )(w_in, w_in)


def _swa_mixer(x, g, w_in, layer, sinks, rel_bias, bsz, seq):
    width = SWA_HEADS * HEAD_DIM
    w_tiles = _swa_weight_tiles(w_in, layer, LOG2E * HEAD_DIM ** -0.5)
    proj = _norm_matmul(x, g, w_tiles, BF16, silu_cols=(width, 2 * width))
    return _swa_attention(proj, sinks, rel_bias, bsz, seq)


HALO = 8


def _conv_kernel(x_ref, g_ref, wb_ref, wc_ref, wu_ref, wg_ref, ck_ref, o_ref, a_ref, z_ref, carry_ref, *,
                 tiles_per_seq, rows_per_step):
    i, j = pl.program_id(0), pl.program_id(1)
    tm = a_ref.shape[1]
    _lookahead_rmsnorm(x_ref, g_ref, a_ref, rows_per_step)
    a = a_ref[i % 2]
    z = _dot(a, wc_ref[...]) * _dot(a, wu_ref[...])
    first = (i % tiles_per_seq) == 0

    @pl.when(first)
    def _():
        z_ref[:HALO, :] = jnp.zeros((HALO, z_ref.shape[1]), F32)

    @pl.when(jnp.logical_not(first))
    def _():
        z_ref[:HALO, :] = carry_ref[j]

    z_ref[HALO:, :] = z
    carry_ref[j] = z[tm - HALO:, :]
    conv = z_ref[HALO - 2:HALO - 2 + tm, :] * ck_ref[0:1, :]
    conv = conv + z_ref[HALO - 1:HALO - 1 + tm, :] * ck_ref[1:2, :]
    conv = conv + z * ck_ref[2:3, :]
    y = _dot(a, wb_ref[...]) * conv
    o_ref[...] = (y * _silu(_dot(a, wg_ref[...]))).astype(o_ref.dtype)


def _conv_mixer(x, g, w_in, layer, conv_kernel, seq, tm=1024, tn=512):
    m, k = x.shape
    width = w_in.shape[2] // 4
    tm, tn = min(tm, seq), min(tn, width)
    nt = width // tn
    w_spec = lambda q: pl.BlockSpec((None, k, tn), lambda i, j: (q * nt + j, 0, 0))
    w_in = _tile_cols(w_in, layer, tn)
    return pl.pallas_call(
        functools.partial(_conv_kernel, tiles_per_seq=seq // tm, rows_per_step=_lookahead_rows(tm, nt)),
        grid=(m // tm, nt),
        in_specs=[
            _lookahead_x_spec(tm, k, m // tm),
            pl.BlockSpec((1, k), lambda i, j: (0, 0)),
            w_spec(0), w_spec(1), w_spec(2), w_spec(3),
            pl.BlockSpec((CONV_TAPS, tn), lambda i, j: (0, j)),
        ],
        out_specs=pl.BlockSpec((tm, tn), lambda i, j: (i, j)),
        out_shape=jax.ShapeDtypeStruct((m, width), BF16),
        scratch_shapes=[
            pltpu.VMEM((2, tm, k), BF16),
            pltpu.VMEM((HALO + tm, tn), F32),
            pltpu.VMEM((nt, HALO, tn), F32),
        ],
        compiler_params=_params(2, VMEM_LIMIT_BIG),
        name="conv_mixer",
    )(x, g.reshape(1, k), w_in, w_in, w_in, w_in, conv_kernel.astype(F32))


def _ssm_operators(lam_re, lam_im, log_dt, b_re, b_im, c_re, c_im, d_skip):
    n_groups = lam_re.shape[0]
    n_oct = n_groups // GROUPS_PER_TILE
    L, C, N = SSM_CHUNK, SSM_GROUP, SSM_STATE
    dt = jnp.exp(log_dt.astype(F32))[None, :]
    lr, li = lam_re.astype(F32).T, lam_im.astype(F32).T
    mag = jnp.exp(lr * dt)
    ab_re, ab_im = mag * jnp.cos(li * dt), mag * jnp.sin(li * dt)
    den = lr * lr + li * li
    nr = ab_re - 1.0
    coef_re = (nr * lr + ab_im * li) / den
    coef_im = (ab_im * lr - nr * li) / den
    br, bi = b_re.astype(F32).transpose(2, 1, 0), b_im.astype(F32).transpose(2, 1, 0)
    bb_re = coef_re * br - coef_im * bi
    bb_im = coef_re * bi + coef_im * br
    cr, ci = c_re.astype(F32).transpose(1, 2, 0), c_im.astype(F32).transpose(1, 2, 0)
    pw_re, pw_im = [jnp.ones_like(ab_re)], [jnp.zeros_like(ab_im)]
    for _ in range(L):
        pw_re, pw_im = (pw_re + [pw_re[-1] * ab_re - pw_im[-1] * ab_im], pw_im + [pw_re[-1] * ab_im + pw_im[-1] * ab_re])
    p_re, p_im = jnp.stack(pw_re), jnp.stack(pw_im)
    w_re = cr[None] * p_re[:L, None] - ci[None] * p_im[:L, None]
    w_im = cr[None] * p_im[:L, None] + ci[None] * p_re[:L, None]
    lag_k = jnp.sum(w_re[:, None] * bb_re[None, :, None] - w_im[:, None] * bb_im[None, :, None], axis=3)
    q_re, q_im = jnp.stack(pw_re[L - 1::-1]), jnp.stack(pw_im[L - 1::-1])
    bc_re = q_re[:, None] * bb_re[None] - q_im[:, None] * bb_im[None]
    bc_im = q_re[:, None] * bb_im[None] + q_im[:, None] * bb_re[None]
    o_re = cr[None] * p_re[1:, None] - ci[None] * p_im[1:, None]
    o_im = cr[None] * p_im[1:, None] + ci[None] * p_re[1:, None]
    octs = lambda t: t.reshape(t.shape[:-1] + (n_oct, GROUPS_PER_TILE))
    to_lag = lambda t: octs(t).transpose(3, 0, 1, 4, 2).reshape(n_oct, L, C, LANES)
    to_rows = lambda t: octs(t).transpose(3, 0, 1, 4, 2).reshape(n_oct, L * C, GROUPS_PER_TILE * N)
    to_cols = lambda t: octs(t).transpose(3, 4, 2, 0, 1).reshape(n_oct, GROUPS_PER_TILE * N, L * C)
    to_vec = lambda t: octs(t).transpose(1, 2, 0).reshape(n_oct, 1, GROUPS_PER_TILE * N)
    return dict(
        lag=to_lag(lag_k).astype(BF16),
        bc_re=to_rows(bc_re).astype(BF16), bc_im=to_rows(bc_im).astype(BF16),
        oc_re=to_cols(o_re).astype(BF16), oc_im_neg=to_cols(-o_im).astype(BF16),
        al_re=to_vec(p_re[L]), al_im=to_vec(p_im[L]), d=d_skip.astype(F32).reshape(n_oct, 1, LANES),
    )


def _ssm_expanders():
    L, C, N, G8 = SSM_CHUNK, SSM_GROUP, SSM_STATE, GROUPS_PER_TILE
    wide = np.arange(L * LANES)
    w_step, w_group, w_chan = wide // LANES, (wide // C) % G8, wide % C
    small = np.arange(L * C)
    s_step, s_chan = small // C, small % C
    state_group = np.arange(G8 * N) // N
    spread = (w_step[:, None] == s_step[None, :]) & (w_chan[:, None] == s_chan[None, :])
    lane = np.arange(LANES)
    as_bf16 = lambda a: jnp.asarray(a.astype(np.float32), dtype=BF16)
    return dict(
        spread_cols=as_bf16(spread.T),
        keep_cols=as_bf16(state_group[:, None] == w_group[None, :]),
        spread_lag=as_bf16((lane % C)[:, None] == np.arange(C)[None, :]),
        keep_lag=jnp.asarray(((lane // C)[:, None] == (lane // C)[None, :]).astype(np.float32)),
    )


def _ssm_kernel(u_ref, lag_ref, bxr_ref, bxi_ref, cxr_ref, cxi_ref, alr_ref, ali_ref, d_ref,
                sc_ref, kc_ref, sl_ref, kl_ref, o_ref,
                ub_ref, bcr_ref, bci_ref, ocr_ref, oci_ref, zr_ref, zi_ref, hr_ref, hi_ref, y_ref, *, bsz):
    L = SSM_CHUNK
    rows = u_ref.shape[0] // L
    chunks = rows // bsz
    tile = 2 * LANES
    step_rows = lambda r: pl.ds(r, rows, stride=L)
    for r in range(L):
        ub_ref[:, r * LANES:(r + 1) * LANES] = u_ref[step_rows(r), :].astype(BF16)
    state_group = lax.broadcasted_iota(jnp.int32, (1, bxr_ref.shape[1]), 1) // SSM_STATE
    for compact_ref, full_ref in ((bxr_ref, bcr_ref), (bxi_ref, bci_ref)):
        for r in range(L):
            blk = compact_ref[r * SSM_GROUP:(r + 1) * SSM_GROUP, :]
            for grp in range(GROUPS_PER_TILE):
                first = r * LANES + grp * SSM_GROUP
                full_ref[first:first + SSM_GROUP, :] = jnp.where(state_group == grp, blk, jnp.zeros_like(blk))
    zr_ref[...] = _dot(ub_ref[...], bcr_ref[...])
    zi_ref[...] = _dot(ub_ref[...], bci_ref[...])
    a_re, a_im = alr_ref[...], ali_ref[...]

    def step(k, carry):
        new = []
        for b in range(bsz):
            h_re, h_im = carry[2 * b], carry[2 * b + 1]
            row = pl.ds(b * chunks + k, 1)
            hr_ref[row, :] = h_re
            hi_ref[row, :] = h_im
            new.append(a_re * h_re - a_im * h_im + zr_ref[row, :])
            new.append(a_re * h_im + a_im * h_re + zi_ref[row, :])
        return tuple(new)

    zero = jnp.zeros((1, a_re.shape[1]), F32)
    lax.fori_loop(0, chunks, step, (zero,) * (2 * bsz))

    ocr_ref[...] = (_dot(cxr_ref[...], sc_ref[...]) * kc_ref[...]).astype(BF16)
    oci_ref[...] = (_dot(cxi_ref[...], sc_ref[...]) * kc_ref[...]).astype(BF16)
    lag = [(_dot(sl_ref[...], lag_ref[d]) * kl_ref[...]).astype(BF16) for d in range(L)]
    lag_tile = []
    for dd in range(L // 2):
        below = lag[2 * dd - 1] if dd > 0 else jnp.zeros((LANES, LANES), BF16)
        lag_tile.append(jnp.concatenate([jnp.concatenate([lag[2 * dd], lag[2 * dd + 1]], axis=1),
                                         jnp.concatenate([below, lag[2 * dd]], axis=1)], axis=0))
    hb_re, hb_im = hr_ref[...].astype(BF16), hi_ref[...].astype(BF16)
    for t2 in range(L // 2):
        cols = slice(t2 * tile, (t2 + 1) * tile)
        y = _dot(hb_re, ocr_ref[:, cols]) + _dot(hb_im, oci_ref[:, cols])
        for r2 in range(t2 + 1):
            y = y + _dot(ub_ref[:, r2 * tile:(r2 + 1) * tile], lag_tile[t2 - r2])
        for half in range(2):
            t = 2 * t2 + half
            y_ref[step_rows(t), :] = jax.nn.gelu(y[:, half * LANES:(half + 1) * LANES] + d_ref[...] * u_ref[step_rows(t), :])

    def chunk(rows):
        o_ref[rows, :] = y_ref[rows, :].astype(o_ref.dtype)
    _for_row_chunks(o_ref.shape[0], chunk)


def _ssm_core(proj, ops, bsz, seq):
    m = proj.shape[0]
    width = proj.shape[1] // 2
    L, C = SSM_CHUNK, SSM_GROUP
    n_oct = width // LANES
    rows = m // L
    ow = L * LANES
    n_state = GROUPS_PER_TILE * SSM_STATE
    ex = _ssm_expanders()
    per_oct = lambda *shape: pl.BlockSpec((None,) + shape, lambda p: (p,) + (0,) * len(shape))
    const = lambda a: pl.BlockSpec(a.shape, lambda p: (0,) * a.ndim, pipeline_mode=pl.Buffered(1))
    consts = [ex['spread_cols'], ex['keep_cols'], ex['spread_lag'], ex['keep_lag']]
    return pl.pallas_call(
        functools.partial(_ssm_kernel, bsz=bsz),
        grid=(n_oct,),
        in_specs=[
            pl.BlockSpec((m, LANES), lambda p: (0, p)),
            per_oct(L, C, LANES), per_oct(L * C, n_state), per_oct(L * C, n_state),
            per_oct(n_state, L * C), per_oct(n_state, L * C), per_oct(1, n_state), per_oct(1, n_state), per_oct(1, LANES),
        ] + [const(a) for a in consts],
        out_specs=pl.BlockSpec((m, LANES), lambda p: (0, p)),
        out_shape=jax.ShapeDtypeStruct((m, width), BF16),
        scratch_shapes=[pltpu.VMEM((rows, ow), BF16)] + [pltpu.VMEM((ow, n_state), BF16)] * 2
        + [pltpu.VMEM((n_state, ow), BF16)] * 2 + [pltpu.VMEM((rows, n_state), F32)] * 4 + [pltpu.VMEM((m, LANES), F32)],
        compiler_params=_params(1),
        name="ssm_core",
    )(proj, ops['lag'], ops['bc_re'], ops['bc_im'], ops['oc_re'], ops['oc_im_neg'], ops['al_re'], ops['al_im'], ops['d'], *consts)


def _glu_kernel(y_ref, wa_ref, wb_ref, ba_ref, bb_ref, gate_ref, o_ref):
    y = y_ref[...]
    ga = _dot(y, wa_ref[...]) + ba_ref[...]
    gb = _dot(y, wb_ref[...]) + bb_ref[...]
    o_ref[...] = ((ga * jax.nn.sigmoid(gb)) * _silu(gate_ref[...])).astype(o_ref.dtype)


def _glu(y, w_glu, layer, b_glu, proj, tm=1024, tn=512):
    m, k = y.shape
    width = w_glu.shape[2] // 2
    tm, tn = min(tm, m), min(tn, width)
    nt = width // tn
    w_tiles = _tile_cols(w_glu, layer, tn)
    return pl.pallas_call(
        _glu_kernel,
        grid=(m // tm, nt),
        in_specs=[
            pl.BlockSpec((tm, k), lambda i, j: (i, 0)),
            pl.BlockSpec((None, k, tn), lambda i, j: (j, 0, 0)),
            pl.BlockSpec((None, k, tn), lambda i, j: (nt + j, 0, 0)),
            pl.BlockSpec((1, tn), lambda i, j: (0, j)),
            pl.BlockSpec((1, tn), lambda i, j: (0, nt + j)),
            pl.BlockSpec((tm, tn), lambda i, j: (i, nt + j)),
        ],
        out_specs=pl.BlockSpec((tm, tn), lambda i, j: (i, j)),
        out_shape=jax.ShapeDtypeStruct((m, width), BF16),
        compiler_params=_params(2),
        name="ssm_glu",
    )(y, w_tiles, w_tiles, b_glu, b_glu, proj)


def _ssm_mixer(x, g, w_in, layer, lam_re, lam_im, log_dt, b_re, b_im, c_re, c_im, d_skip, w_glu, b_glu, bsz, seq):
    proj = _norm_matmul(x, g, _tile_cols(w_in, layer, MATMUL_TN), F32)
    ops = _ssm_operators(lam_re, lam_im, log_dt, b_re, b_im, c_re, c_im, d_skip)
    y = _ssm_core(proj, ops, bsz, seq)
    return _glu(y, w_glu, layer, b_glu.astype(F32).reshape(1, -1), proj)


CUM_BLOCK = 128


def _split3(x):
    x1 = x.astype(BF16)
    r1 = x - x1.astype(F32)
    x2 = r1.astype(BF16)
    x3 = (r1 - x2.astype(F32)).astype(BF16)
    return x1, x2, x3


def _forget_cumsum_kernel(z_ref, b_ref, o_ref):
    n_blocks = z_ref.shape[0] // CUM_BLOCK
    ri = lax.broadcasted_iota(jnp.int32, (CUM_BLOCK, CUM_BLOCK), 0)
    ci = lax.broadcasted_iota(jnp.int32, (CUM_BLOCK, CUM_BLOCK), 1)
    tri = jnp.where(ci <= ri, 1.0, 0.0).astype(BF16)

    def body(i, carry):
        rows = pl.ds(pl.multiple_of(i * CUM_BLOCK, CUM_BLOCK), CUM_BLOCK)
        z = z_ref[rows, :] + b_ref[...]
        log_f = jnp.minimum(z, 0.0) - jnp.log1p(jnp.exp(-jnp.abs(z)))
        x1, x2, x3 = _split3(log_f)
        c = _dot(tri, x1) + _dot(tri, x2) + _dot(tri, x3) + carry
        o_ref[rows, :] = c
        return c[CUM_BLOCK - 1:, :]

    lax.fori_loop(0, n_blocks, body, jnp.zeros((1, z_ref.shape[1]), F32))


def _forget_cumsum(z, b, bsz, seq):
    lanes = z.shape[1]
    return pl.pallas_call(
        _forget_cumsum_kernel,
        grid=(bsz,),
        in_specs=[pl.BlockSpec((seq, lanes), lambda i: (i, 0)), pl.BlockSpec((1, lanes), lambda i: (0, 0))],
        out_specs=pl.BlockSpec((seq, lanes), lambda i: (i, 0)),
        out_shape=jax.ShapeDtypeStruct(z.shape, F32),
        compiler_params=_params(1),
        name="forget_cumsum",
    )(z, b)


FOX_STEP_HEADS = 8


def _fox_kernel(q_ref, k_ref, v_ref, sg_ref, cq_ref, ck_ref, o_ref, ka_ref, vt_ref, *head_refs, tk):
    nh = FOX_STEP_HEADS
    hg, qi = pl.program_id(1), pl.program_id(2)
    tq = q_ref.shape[0]
    qa_ref, s_ref, p_ref, acc_ref, pv_ref, m_ref = (head_refs[i::6] for i in range(6))
    lane = lax.broadcasted_iota(jnp.int32, (1, LANES), 1)
    free = (HEAD_DIM, 0)
    n_pieces = 3
    pair_lanes = lambda h: slice(h // 2 * LANES, (h // 2 + 1) * LANES)

    def bias_lanes(c, h, c_offset, one_offset):
        rr = lax.broadcasted_iota(jnp.int32, (n_pieces * LANES, LANES), 0)
        cc = lax.broadcasted_iota(jnp.int32, (n_pieces * LANES, LANES), 1)
        place = jnp.logical_and(rr % LANES == nh * hg + h, cc == free[h % 2] + c_offset + rr // LANES)
        placed = _dot(jnp.concatenate(_split3(c), axis=1), jnp.where(place, 1.0, 0.0).astype(BF16))
        first = free[h % 2] + one_offset
        return (placed + jnp.where(jnp.logical_and(lane >= first, lane < first + n_pieces), 1.0, 0.0)).astype(BF16)

    @pl.when(qi == 0)
    def _():
        def chunk(j, carry):
            rows = pl.ds(pl.multiple_of(j * tk, tk), tk)
            neg_c = ck_ref[rows, :] * -LOG2E
            for h in range(nh):
                keys = k_ref[rows, pair_lanes(h)]
                ka_ref[h, j] = jnp.where((lane // HEAD_DIM) == h % 2, keys, bias_lanes(neg_c, h, 0, n_pieces))
            for pair in range(nh // 2):
                v_t = v_ref[rows, pair_lanes(2 * pair)].astype(F32).T.astype(BF16)
                for hh in range(2):
                    vt_ref[j, 2 * pair + hh, :HEAD_DIM, :] = v_t[hh * HEAD_DIM:(hh + 1) * HEAD_DIM]
                    vt_ref[j, 2 * pair + hh, HEAD_DIM:, :] = jnp.ones((ONES_ROWS, tk), BF16)
            return carry
        lax.fori_loop(0, k_ref.shape[0] // tk, chunk, 0)

    cq = cq_ref[...] * LOG2E
    for h in range(nh):
        qa = jnp.where((lane // HEAD_DIM) == h % 2, q_ref[:, pair_lanes(h)], bias_lanes(cq, h, n_pieces, 0))
        qa_ref[h][...] = qa.astype(F32).T.astype(BF16)

    def scores(h, j):
        s_ref[h][...] = _dot(ka_ref[h, j], qa_ref[h][...])

    rows8 = 8
    rows16 = 16

    def absorb(h, j, diagonal):
        def strip(r, n):
            blk = s_ref[h][r:r + n, :]
            if diagonal:
                kpos = lax.broadcasted_iota(jnp.int32, (n, tq), 0)
                qpos = lax.broadcasted_iota(jnp.int32, (n, tq), 1)
                blk = jnp.where(kpos + r <= qpos, blk, NEG)
            return blk
        top = strip(0, rows8)
        for r in range(rows8, tk, rows8):
            top = jnp.maximum(top, strip(r, rows8))
        m = m_ref[h][...]
        m_new = jnp.maximum(m, top.max(axis=0, keepdims=True))
        m_ref[h][...] = m_new
        alpha = jnp.exp2(m - m_new)
        m_rows = jnp.broadcast_to(m_new, (rows16, tq))
        for r in range(0, tk, rows16):
            p_ref[h][r:r + rows16, :] = jnp.exp2(strip(r, rows16) - m_rows).astype(BF16)
        acc_ref[h][...] = alpha * (acc_ref[h][...] + pv_ref[h][...])

    def values(h, j):
        pv_ref[h][...] = _dot(vt_ref[j, h], p_ref[h][...])

    def step(j, diagonal):
        values(nh - 1, jnp.maximum(j - 1, 0))
        scores(1, j)
        for h in range(nh):
            absorb(h, j, diagonal)
            if h < nh - 1:
                values(h, j)
            if h + 2 < nh:
                scores(h + 2, j)
            elif h + 2 == nh and not diagonal:
                scores(0, j + 1)

    for h in range(nh):
        acc_ref[h][...] = jnp.zeros(acc_ref[h].shape, F32)
        pv_ref[h][...] = jnp.zeros(pv_ref[h].shape, F32)
        m_ref[h][...] = jnp.full(m_ref[h].shape, NEG, F32)
    p_ref[nh - 1][...] = jnp.zeros(p_ref[nh - 1].shape, BF16)
    scores(0, 0)

    def two_blocks(i, carry):
        step(2 * i, False)
        step(2 * i + 1, False)
        return carry
    lax.fori_loop(0, qi // 2, two_blocks, 0)

    @pl.when(qi % 2 == 0)
    def _():
        step(qi, True)

    @pl.when(qi % 2 == 1)
    def _():
        step(qi - 1, False)
        step(qi, True)

    values(nh - 1, qi)
    outs = []
    for h in range(nh):
        total = acc_ref[h][...] + pv_ref[h][...]
        outs.append(total[:HEAD_DIM] / total[HEAD_DIM:HEAD_DIM + 1])
    out = jnp.concatenate(outs, axis=0)
    o_ref[...] = (out.T * sg_ref[...].astype(F32)).astype(o_ref.dtype)


def _fox_attention(proj, csum, bsz, seq, t=512):
    width = FOX_HEADS * HEAD_DIM
    nh = FOX_STEP_HEADS
    gw = nh * HEAD_DIM
    n_groups = width // gw
    t = min(t, seq)
    nq = seq // t
    return pl.pallas_call(
        functools.partial(_fox_kernel, tk=t),
        grid=(bsz, n_groups, nq),
        in_specs=[
            pl.BlockSpec((t, gw), lambda b, h, i: (b * nq + i, h)),
            pl.BlockSpec((seq, gw), lambda b, h, i: (b, n_groups + h), pipeline_mode=pl.Buffered(1)),
            pl.BlockSpec((seq, gw), lambda b, h, i: (b, 2 * n_groups + h), pipeline_mode=pl.Buffered(1)),
            pl.BlockSpec((t, gw), lambda b, h, i: (b * nq + i, 3 * n_groups + h)),
            pl.BlockSpec((t, LANES), lambda b, h, i: (b * nq + i, 0)),
            pl.BlockSpec((seq, LANES), lambda b, h, i: (b, 0), pipeline_mode=pl.Buffered(1)),
        ],
        out_specs=pl.BlockSpec((t, gw), lambda b, h, i: (b * nq + i, h)),
        out_shape=jax.ShapeDtypeStruct((bsz * seq, width), BF16),
        scratch_shapes=[
            pltpu.VMEM((nh, nq, t, LANES), BF16), pltpu.VMEM((nq, nh, HEAD_DIM + ONES_ROWS, t), BF16),
        ] + nh * [pltpu.VMEM((LANES, t), BF16), pltpu.VMEM((t, t), F32), pltpu.VMEM((t, t), BF16),
                  pltpu.VMEM((HEAD_DIM + ONES_ROWS, t), F32), pltpu.VMEM((HEAD_DIM + ONES_ROWS, t), F32),
                  pltpu.VMEM((1, t), F32)],
        compiler_params=_params(3, VMEM_LIMIT_BIG),
        name="fox_attention",
    )(proj, proj, proj, proj, csum, csum)


def _fox_mixer(x, g, w_in, layer, w_fg, b_fg, bsz, seq, t=512):
    width = FOX_HEADS * HEAD_DIM
    col_scale = jnp.where(jnp.arange(4 * width) < width, LOG2E * HEAD_DIM ** -0.5, 1.0).astype(F32)
    proj = _norm_matmul(x, g, _tile_cols(w_in, layer, MATMUL_TN, col_scale), BF16, silu_cols=(3 * width, 4 * width))
    pad = LANES - FOX_HEADS
    z = _norm_matmul(x, g, _tile_cols_xla(jnp.pad(w_fg, ((0, 0), (0, pad))), LANES), F32)
    csum = _forget_cumsum(z, jnp.pad(b_fg.astype(F32), (0, pad)).reshape(1, LANES), bsz, seq)
    return _fox_attention(proj, csum, bsz, seq, t)


def kernel(x, p, norm_g, final_g, rel_bias, swa_w_in, swa_w_out, swa_sinks, conv_w_in, conv_kernel, conv_w_out, ssm_w_in, ssm_lam_re, ssm_lam_im, ssm_log_dt, ssm_b_re, ssm_b_im, ssm_c_re, ssm_c_im, ssm_d, ssm_w_glu, ssm_b_glu, ssm_w_out, fox_w_in, fox_w_fg, fox_b_fg, fox_w_out, ple_proj, ple_norm, ple_gate):
    bsz, seq, d_model = x.shape
    depth = p.shape[0]
    h = x.astype(F32).reshape(bsz * seq, d_model)
    for i in range(depth):
        mixer, j = i % N_MIXERS, i // N_MIXERS
        if mixer == 0:
            a, w_out = _swa_mixer(h, norm_g[i], swa_w_in, j, swa_sinks[j], rel_bias, bsz, seq), swa_w_out[j]
        elif mixer == 1:
            a, w_out = _conv_mixer(h, norm_g[i], conv_w_in, j, conv_kernel[j], seq), conv_w_out[j]
        elif mixer == 2:
            a = _ssm_mixer(h, norm_g[i], ssm_w_in, j, ssm_lam_re[j], ssm_lam_im[j], ssm_log_dt[j], ssm_b_re[j], ssm_b_im[j],
                           ssm_c_re[j], ssm_c_im[j], ssm_d[j], ssm_w_glu, ssm_b_glu[j], bsz, seq)
            w_out = ssm_w_out[j]
        else:
            a, w_out = _fox_mixer(h, norm_g[i], fox_w_in, j, fox_w_fg[j], fox_b_fg[j], bsz, seq), fox_w_out[j]
        h = _out_ple(a, w_out, h, ple_norm[i], ple_gate, p.reshape(depth, bsz * seq, -1), ple_proj, i,
                     final_g=final_g if i == depth - 1 else None)
    return h.reshape(bsz, seq, d_model).astype(x.dtype)
```

```python
import functools
import math

import numpy as np
import jax
import jax.numpy as jnp
from jax import lax
from jax.experimental import pallas as pl
from jax.experimental.pallas import tpu as pltpu

F32 = jnp.float32
BF16 = jnp.bfloat16

EPS = 1e-6
N_MIXERS = 4

SWA_HEADS = 32
SWA_KV_HEADS = 4
SWA_GROUP = SWA_HEADS // SWA_KV_HEADS
HEAD_DIM = 64
SWA_BLOCK = 128
WINDOW = 128
REL_BUCKETS = 32
REL_MAX_DIST = 128

CONV_TAPS = 3

SSM_GROUP = 16
SSM_STATE = 64
SSM_CHUNK = 16

FOX_HEADS = 32

LANES = 128
ONES_ROWS = 16
GROUPS_PER_TILE = LANES // SSM_GROUP
VMEM_LIMIT = 48 * 1024 * 1024
VMEM_LIMIT_BIG = 60 * 1024 * 1024

NEG = float(jnp.finfo(jnp.float32).min)
LOG2E = math.log2(math.e)


def _params(n_axes, vmem_limit=VMEM_LIMIT):
    return pltpu.CompilerParams(dimension_semantics=("arbitrary",) * n_axes, vmem_limit_bytes=vmem_limit)


def _dot(a, b):
    return jnp.dot(a, b, preferred_element_type=F32)


def _dot_nt(a, b):
    return lax.dot_general(a, b, (((1,), (1,)), ((), ())), preferred_element_type=F32)


def _rmsnorm_rows(x, g):
    return x * lax.rsqrt(jnp.mean(x * x, axis=-1, keepdims=True) + EPS) * g


def _silu(x):
    return x * jax.nn.sigmoid(x)


ROW_CHUNK = 64


def _for_row_chunks(n_rows, fn):
    def body(c, carry):
        fn(pl.ds(pl.multiple_of(c * ROW_CHUNK, ROW_CHUNK), ROW_CHUNK))
        return carry
    lax.fori_loop(0, n_rows // ROW_CHUNK, body, 0)


def _tile_cols_xla(w, tn):
    k, n = w.shape
    return w.astype(BF16).reshape(k, n // tn, tn).transpose(1, 0, 2)


def _cast_tiles_kernel(w_ref, o_ref):
    def chunk(rows):
        o_ref[rows, :] = w_ref[rows, :].astype(BF16)
    _for_row_chunks(o_ref.shape[0], chunk)


def _scale_cast_tiles_kernel(w_ref, s_ref, o_ref):
    def chunk(rows):
        o_ref[rows, :] = (w_ref[rows, :] * s_ref[...]).astype(BF16)
    _for_row_chunks(o_ref.shape[0], chunk)


def _tile_cols(w, layer, tn, col_scale=None):
    _, k, n = w.shape
    tn = min(tn, n)
    w_spec = pl.BlockSpec((None, k, tn), lambda j: (layer, 0, j))
    common = dict(
        grid=(n // tn,),
        out_specs=pl.BlockSpec((None, k, tn), lambda j: (j, 0, 0)),
        out_shape=jax.ShapeDtypeStruct((n // tn, k, tn), BF16),
        compiler_params=_params(1),
        name="weight_tiles",
    )
    if col_scale is None:
        return pl.pallas_call(_cast_tiles_kernel, in_specs=[w_spec], **common)(w)
    s_spec = pl.BlockSpec((1, tn), lambda j: (0, j))
    return pl.pallas_call(_scale_cast_tiles_kernel, in_specs=[w_spec, s_spec], **common)(w, col_scale.reshape(1, n))


def _lookahead_x_spec(tm, k, n_row_tiles):
    def tile(i, j):
        first_step = jnp.logical_and(i == 0, j == 0)
        return jnp.where(first_step, 0, jnp.minimum(i + 1, n_row_tiles - 1))
    return pl.BlockSpec((tm, k), lambda i, j: (tile(i, j), 0))


def _lookahead_rows(tm, nt):
    return -(-tm // ((nt - 1) * ROW_CHUNK)) * ROW_CHUNK


def _lookahead_first(x_ref, g_ref, a_ref):
    @pl.when(jnp.logical_and(pl.program_id(0) == 0, pl.program_id(1) == 0))
    def _():
        def chunk(rows):
            a_ref[0, rows, :] = _rmsnorm_rows(x_ref[rows, :], g_ref[...]).astype(BF16)
        _for_row_chunks(a_ref.shape[1], chunk)


def _lookahead_next(x_ref, g_ref, a_ref, rows_per_step):
    i, j = pl.program_id(0), pl.program_id(1)
    tm = a_ref.shape[1]
    start = jnp.clip((j - 1) * rows_per_step, 0, tm - rows_per_step)
    for c in range(rows_per_step // ROW_CHUNK):
        rows = pl.ds(pl.multiple_of(start + c * ROW_CHUNK, ROW_CHUNK), ROW_CHUNK)
        a_ref[(i + 1) % 2, rows, :] = _rmsnorm_rows(x_ref[rows, :], g_ref[...]).astype(BF16)


def _norm_matmul_kernel(x_ref, g_ref, w_ref, o_ref, a_ref, *, silu_tiles, rows_per_step):
    i, j = pl.program_id(0), pl.program_id(1)
    if rows_per_step is None:
        def chunk(rows):
            a_ref[0, rows, :] = _rmsnorm_rows(x_ref[rows, :], g_ref[...]).astype(BF16)
        _for_row_chunks(a_ref.shape[1], chunk)
        acc = _dot(a_ref[0], w_ref[...])
    else:
        _lookahead_first(x_ref, g_ref, a_ref)
        acc = _dot(a_ref[i % 2], w_ref[...])
        _lookahead_next(x_ref, g_ref, a_ref, rows_per_step)
    if silu_tiles is None:
        o_ref[...] = acc.astype(o_ref.dtype)
    else:
        gated = jnp.logical_and(j >= silu_tiles[0], j < silu_tiles[1])

        @pl.when(gated)
        def _():
            o_ref[...] = _silu(acc).astype(o_ref.dtype)

        @pl.when(jnp.logical_not(gated))
        def _():
            o_ref[...] = acc.astype(o_ref.dtype)


MATMUL_TN = 1024


def _norm_matmul(x, g, w_tiles, out_dtype, silu_cols=None, tm=1024):
    m, k = x.shape
    nt, _, tn = w_tiles.shape
    n = nt * tn
    tm = min(tm, m)
    n_row_tiles = m // tm
    silu_tiles = None if silu_cols is None else (silu_cols[0] // tn, silu_cols[1] // tn)
    lookahead = nt > 1
    return pl.pallas_call(
        functools.partial(_norm_matmul_kernel, silu_tiles=silu_tiles,
                          rows_per_step=_lookahead_rows(tm, nt) if lookahead else None),
        grid=(n_row_tiles, nt),
        in_specs=[
            _lookahead_x_spec(tm, k, n_row_tiles) if lookahead else pl.BlockSpec((tm, k), lambda i, j: (i, 0)),
            pl.BlockSpec((1, k), lambda i, j: (0, 0)),
            pl.BlockSpec((None, k, tn), lambda i, j: (j, 0, 0)),
        ],
        out_specs=pl.BlockSpec((tm, tn), lambda i, j: (i, j)),
        out_shape=jax.ShapeDtypeStruct((m, n), out_dtype),
        scratch_shapes=[pltpu.VMEM((2, tm, k), BF16)],
        compiler_params=_params(2, VMEM_LIMIT_BIG),
        name="norm_matmul",
    )(x, g.reshape(1, k), w_tiles)


def _out_ple_kernel(a_ref, wo_ref, x_ref, g_ref, wg_ref, p_ref, wp_ref, gf_ref, o_ref, x1_ref, hn_ref, ss_ref, *, nt, final):
    j = pl.program_id(1)
    tn = x_ref.shape[1]
    n = nt * tn

    @pl.when(j == 0)
    def _():
        ss_ref[...] = jnp.zeros(ss_ref.shape, F32)

    @pl.when(j < nt)
    def _():
        x1 = x_ref[...] + _dot(a_ref[...], wo_ref[j])
        x1_ref[j] = x1
        hn_ref[j] = (x1 * g_ref[j]).astype(BF16)
        ss_ref[0] += jnp.sum(x1 * x1, axis=-1, keepdims=True)

    @pl.when(jnp.logical_and(j >= nt, j < 2 * nt))
    def _():
        t = j - nt
        emb = _dot(p_ref[...].astype(BF16), wp_ref[t])
        acc = _dot(hn_ref[0], wg_ref[t, :tn, :])
        for kt in range(1, nt):
            acc = acc + _dot(hn_ref[kt], wg_ref[t, kt * tn:(kt + 1) * tn, :])
        x2 = x1_ref[t] + emb * jax.nn.sigmoid(acc * lax.rsqrt(ss_ref[0] / n + EPS))
        if final:
            x1_ref[t] = x2
            ss_ref[1] += jnp.sum(x2 * x2, axis=-1, keepdims=True)
        else:
            o_ref[...] = x2

    if final:
        @pl.when(j >= 2 * nt)
        def _():
            t = j - 2 * nt
            o_ref[...] = x1_ref[t] * lax.rsqrt(ss_ref[1] / n + EPS) * gf_ref[t]


def _out_ple(a, w_out, x, g, w_gate, p, w_proj, layer, final_g=None, tm=1024, tn=512):
    m, k = a.shape
    n = w_out.shape[1]
    pd = p.shape[2]
    final = final_g is not None
    tm, tn = min(tm, m), min(tn, n)
    nt = n // tn
    passes = 3 if final else 2
    out_tile = lambda j: jnp.maximum(j - (passes - 1) * nt, 0)
    resident = lambda rows: pl.BlockSpec((nt, rows, tn), lambda i, j: (0, 0, 0), pipeline_mode=pl.Buffered(1))
    row_vec = pl.BlockSpec((nt, 1, tn), lambda i, j: (0, 0, 0))
    return pl.pallas_call(
        functools.partial(_out_ple_kernel, nt=nt, final=final),
        grid=(m // tm, passes * nt),
        in_specs=[
            pl.BlockSpec((tm, k), lambda i, j: (i, 0)),
            resident(k),
            pl.BlockSpec((tm, tn), lambda i, j: (i, jnp.minimum(j, nt - 1))),
            row_vec,
            resident(n),
            pl.BlockSpec((None, tm, pd), lambda i, j: (layer, i, 0)),
            resident(pd),
            row_vec,
        ],
        out_specs=pl.BlockSpec((tm, tn), lambda i, j: (i, out_tile(j))),
        out_shape=jax.ShapeDtypeStruct((m, n), F32),
        scratch_shapes=[pltpu.VMEM((nt, tm, tn), F32), pltpu.VMEM((nt, tm, tn), BF16), pltpu.VMEM((2, tm, 1), F32)],
        compiler_params=_params(2, VMEM_LIMIT_BIG),
        name="out_ple_final" if final else "out_ple",
    )(a, _tile_cols(w_out[None], 0, tn), x, g.reshape(nt, 1, tn), _tile_cols(w_gate, layer, tn), p,
      _tile_cols(w_proj, layer, tn), (final_g if final else g).reshape(nt, 1, tn))


def _t5_bucket(dist):
    max_exact = REL_BUCKETS // 2
    d = np.maximum(dist, 1).astype(np.float32)
    large = max_exact + (np.log(d / max_exact) / np.log(REL_MAX_DIST / max_exact) * (REL_BUCKETS - max_exact)).astype(np.int32)
    large = np.minimum(large, REL_BUCKETS - 1)
    return np.where(dist < max_exact, dist, large).astype(np.int32)


assert WINDOW == SWA_BLOCK


def _swa_bucket_table():
    qi = np.arange(SWA_BLOCK)[None, :]
    kj = np.arange(2 * SWA_BLOCK)[:, None]
    return _t5_bucket(np.clip(qi + SWA_BLOCK - kj, 0, None))


def _swa_kernel(sink_ref, rel_ref, bucket_ref, q_ref, sg_ref, kp_ref, kc_ref, vp_ref, vc_ref, o_ref, bias_ref, qs_ref):
    n = pl.program_id(1)
    blk = SWA_BLOCK
    lane = lax.broadcasted_iota(jnp.int32, (1, LANES), 1)

    @pl.when(jnp.logical_and(pl.program_id(0) == 0, n == 0))
    def _():
        bucket = bucket_ref[...]
        kj = lax.broadcasted_iota(jnp.int32, (2 * blk, blk), 0)
        qi = lax.broadcasted_iota(jnp.int32, (2 * blk, blk), 1)
        band = jnp.logical_or(jnp.logical_and(kj < blk, kj > qi), jnp.logical_and(kj >= blk, kj - blk <= qi))

        def per_head(head, carry):
            acc = jnp.zeros((2 * blk, blk), F32)
            for b in range(REL_BUCKETS):
                acc = jnp.where(bucket == b, rel_ref[b, head] * LOG2E, acc)
            bias_ref[head] = jnp.where(band, acc, NEG)
            return carry
        lax.fori_loop(0, SWA_HEADS, per_head, 0)

    for kvh in range(SWA_KV_HEADS):
        cols = slice(kvh * LANES, (kvh + 1) * LANES)
        heads = range(kvh * SWA_GROUP, (kvh + 1) * SWA_GROUP)
        for g, head in enumerate(heads):
            q_pair = q_ref[:, head // 2 * LANES:(head // 2 + 1) * LANES]
            qs_ref[g * blk:(g + 1) * blk, :] = jnp.where((lane // HEAD_DIM) == head % 2, q_pair, jnp.zeros_like(q_pair))
        keys = jnp.concatenate([kp_ref[:, cols], kc_ref[:, cols]], axis=0)
        vals = jnp.concatenate([vp_ref[:, cols], vc_ref[:, cols]], axis=0)
        s = _dot_nt(keys, qs_ref[...]) + jnp.concatenate([bias_ref[head] for head in heads], axis=1)
        s_prev = jnp.where(n > 0, s[:blk], NEG)
        s_cur = s[blk:]
        sink = jnp.concatenate([jnp.full((1, blk), sink_ref[head] * LOG2E, F32) for head in heads], axis=1)
        m = jnp.maximum(jnp.maximum(s_prev, s_cur).max(axis=0, keepdims=True), sink)
        e = jnp.concatenate([jnp.exp2(s_prev - m).astype(BF16), jnp.exp2(s_cur - m).astype(BF16)], axis=0)
        v_t = jnp.concatenate([vals.astype(F32).T.astype(BF16)[:HEAD_DIM], jnp.ones((ONES_ROWS, 2 * blk), BF16)], axis=0)
        out = _dot(v_t, e)
        out = out[:HEAD_DIM] / (out[HEAD_DIM:HEAD_DIM + 1] + jnp.exp2(sink - m))
        for pair in range(SWA_GROUP // 2):
            head = kvh * SWA_GROUP + 2 * pair
            even = out[:, 2 * pair * blk:(2 * pair + 1) * blk]
            odd = out[:, (2 * pair + 1) * blk:(2 * pair + 2) * blk]
            cols = slice(head // 2 * LANES, (head // 2 + 1) * LANES)
            o_ref[:, cols] = (jnp.concatenate([even, odd], axis=0).T * sg_ref[:, cols].astype(F32)).astype(o_ref.dtype)


def _swa_attention(proj, sinks, rel_bias, bsz, seq):
    nb = seq // SWA_BLOCK
    width = SWA_HEADS * HEAD_DIM
    kvw = SWA_KV_HEADS * LANES
    k_blk = 2 * width // kvw
    cur = lambda b, n: b * nb + n
    prev = lambda b, n: b * nb + jnp.maximum(n - 1, 0)
    smem = pl.BlockSpec(memory_space=pltpu.SMEM)
    return pl.pallas_call(
        _swa_kernel,
        grid=(bsz, nb),
        in_specs=[
            smem, smem,
            pl.BlockSpec((2 * SWA_BLOCK, SWA_BLOCK), lambda b, n: (0, 0)),
            pl.BlockSpec((SWA_BLOCK, width), lambda b, n: (cur(b, n), 0)),
            pl.BlockSpec((SWA_BLOCK, width), lambda b, n: (cur(b, n), 1)),
            pl.BlockSpec((SWA_BLOCK, kvw), lambda b, n: (prev(b, n), k_blk)),
            pl.BlockSpec((SWA_BLOCK, kvw), lambda b, n: (cur(b, n), k_blk)),
            pl.BlockSpec((SWA_BLOCK, kvw), lambda b, n: (prev(b, n), k_blk + 1)),
            pl.BlockSpec((SWA_BLOCK, kvw), lambda b, n: (cur(b, n), k_blk + 1)),
        ],
        out_specs=pl.BlockSpec((SWA_BLOCK, width), lambda b, n: (cur(b, n), 0)),
        out_shape=jax.ShapeDtypeStruct((bsz * seq, width), BF16),
        scratch_shapes=[
            pltpu.VMEM((SWA_HEADS, 2 * SWA_BLOCK, SWA_BLOCK), F32),
            pltpu.VMEM((SWA_GROUP * SWA_BLOCK, LANES), BF16),
        ],
        compiler_params=_params(2),
        name="swa_attention",
    )(sinks.astype(F32), rel_bias.astype(F32), jnp.asarray(_swa_bucket_table()), proj, proj, proj, proj, proj, proj)


SWA_HALF_TILE = 512


def _swa_weight_tiles_kernel(wl_ref, wr_ref, o_ref, *, q_tiles, gate_tiles, q_scale):
    j = pl.program_id(0)
    half = SWA_HALF_TILE
    kvw = SWA_KV_HEADS * HEAD_DIM

    def rows_do(fn):
        def chunk(rows):
            for side, w_ref in enumerate((wl_ref, wr_ref)):
                o_ref[rows, side * half:(side + 1) * half] = fn(side, w_ref, rows)
        _for_row_chunks(o_ref.shape[0], chunk)

    @pl.when(j < q_tiles)
    def _():
        rows_do(lambda side, w_ref, rows: (w_ref[rows, :] * q_scale).astype(BF16))

    @pl.when(jnp.logical_and(j >= q_tiles, j < q_tiles + gate_tiles))
    def _():
        rows_do(lambda side, w_ref, rows: w_ref[rows, :].astype(BF16))

    @pl.when(j == q_tiles + gate_tiles)
    def _():
        def twice(side, w_ref, rows):
            w = w_ref[rows, side * kvw:(side + 1) * kvw].astype(BF16)
            heads = [w[:, h * HEAD_DIM:(h + 1) * HEAD_DIM] for h in range(SWA_KV_HEADS)]
            return jnp.concatenate([piece for head in heads for piece in (head, head)], axis=1)
        rows_do(twice)


def _swa_weight_tiles(w_in, layer, q_scale):
    _, k, n = w_in.shape
    half = SWA_HALF_TILE
    width = SWA_HEADS * HEAD_DIM
    q_tiles = gate_tiles = width // (2 * half)
    kv_block = width // half
    n_out = q_tiles + gate_tiles + 1

    def src(j, side):
        in_q, in_gate = j < q_tiles, j < q_tiles + gate_tiles
        return jnp.where(in_q, 2 * j + side, jnp.where(in_gate, 2 * j + side + 1, kv_block))
    return pl.pallas_call(
        functools.partial(_swa_weight_tiles_kernel, q_tiles=q_tiles, gate_tiles=gate_tiles, q_scale=q_scale),
        grid=(n_out,),
        in_specs=[pl.BlockSpec((None, k, half), lambda j: (layer, 0, src(j, 0))),
                  pl.BlockSpec((None, k, half), lambda j: (layer, 0, src(j, 1)))],
        out_specs=pl.BlockSpec((None, k, 2 * half), lambda j: (j, 0, 0)),
        out_shape=jax.ShapeDtypeStruct((n_out, k, 2 * half), BF16),
        compiler_params=_params(1),
        name="swa_weight_tiles",
    )(w_in, w_in)


def _swa_mixer(x, g, w_in, layer, sinks, rel_bias, bsz, seq):
    width = SWA_HEADS * HEAD_DIM
    w_tiles = _swa_weight_tiles(w_in, layer, LOG2E * HEAD_DIM ** -0.5)
    proj = _norm_matmul(x, g, w_tiles, BF16, silu_cols=(width, 2 * width))
    return _swa_attention(proj, sinks, rel_bias, bsz, seq)


HALO = 8


def _conv_kernel(x_ref, g_ref, wb_ref, wc_ref, wu_ref, wg_ref, ck_ref, o_ref, a_ref, z_ref, carry_ref, *,
                 tiles_per_seq, rows_per_step):
    i, j = pl.program_id(0), pl.program_id(1)
    tm = a_ref.shape[1]
    _lookahead_first(x_ref, g_ref, a_ref)
    a = a_ref[i % 2]
    z = _dot(a, wc_ref[...]) * _dot(a, wu_ref[...])
    first = (i % tiles_per_seq) == 0

    @pl.when(first)
    def _():
        z_ref[:HALO, :] = jnp.zeros((HALO, z_ref.shape[1]), F32)

    @pl.when(jnp.logical_not(first))
    def _():
        z_ref[:HALO, :] = carry_ref[j]

    z_ref[HALO:, :] = z
    carry_ref[j] = z[tm - HALO:, :]
    conv = z_ref[HALO - 2:HALO - 2 + tm, :] * ck_ref[0:1, :]
    conv = conv + z_ref[HALO - 1:HALO - 1 + tm, :] * ck_ref[1:2, :]
    conv = conv + z * ck_ref[2:3, :]
    y = _dot(a, wb_ref[...]) * conv
    o_ref[...] = (y * _silu(_dot(a, wg_ref[...]))).astype(o_ref.dtype)
    _lookahead_next(x_ref, g_ref, a_ref, rows_per_step)


def _conv_mixer(x, g, w_in, layer, conv_kernel, seq, tm=1024, tn=512):
    m, k = x.shape
    width = w_in.shape[2] // 4
    tm, tn = min(tm, seq), min(tn, width)
    nt = width // tn
    w_spec = lambda q: pl.BlockSpec((None, k, tn), lambda i, j: (q * nt + j, 0, 0))
    w_in = _tile_cols(w_in, layer, tn)
    return pl.pallas_call(
        functools.partial(_conv_kernel, tiles_per_seq=seq // tm, rows_per_step=_lookahead_rows(tm, nt)),
        grid=(m // tm, nt),
        in_specs=[
            _lookahead_x_spec(tm, k, m // tm),
            pl.BlockSpec((1, k), lambda i, j: (0, 0)),
            w_spec(0), w_spec(1), w_spec(2), w_spec(3),
            pl.BlockSpec((CONV_TAPS, tn), lambda i, j: (0, j)),
        ],
        out_specs=pl.BlockSpec((tm, tn), lambda i, j: (i, j)),
        out_shape=jax.ShapeDtypeStruct((m, width), BF16),
        scratch_shapes=[
            pltpu.VMEM((2, tm, k), BF16),
            pltpu.VMEM((HALO + tm, tn), F32),
            pltpu.VMEM((nt, HALO, tn), F32),
        ],
        compiler_params=_params(2, VMEM_LIMIT_BIG),
        name="conv_mixer",
    )(x, g.reshape(1, k), w_in, w_in, w_in, w_in, conv_kernel.astype(F32))


def _ssm_operators(lam_re, lam_im, log_dt, b_re, b_im, c_re, c_im, d_skip):
    n_groups = lam_re.shape[0]
    n_oct = n_groups // GROUPS_PER_TILE
    L, C, N = SSM_CHUNK, SSM_GROUP, SSM_STATE
    dt = jnp.exp(log_dt.astype(F32))[None, :]
    lr, li = lam_re.astype(F32).T, lam_im.astype(F32).T
    mag = jnp.exp(lr * dt)
    ab_re, ab_im = mag * jnp.cos(li * dt), mag * jnp.sin(li * dt)
    den = lr * lr + li * li
    nr = ab_re - 1.0
    coef_re = (nr * lr + ab_im * li) / den
    coef_im = (ab_im * lr - nr * li) / den
    br, bi = b_re.astype(F32).transpose(2, 1, 0), b_im.astype(F32).transpose(2, 1, 0)
    bb_re = coef_re * br - coef_im * bi
    bb_im = coef_re * bi + coef_im * br
    cr, ci = c_re.astype(F32).transpose(1, 2, 0), c_im.astype(F32).transpose(1, 2, 0)
    pw_re, pw_im = [jnp.ones_like(ab_re)], [jnp.zeros_like(ab_im)]
    for _ in range(L):
        pw_re, pw_im = (pw_re + [pw_re[-1] * ab_re - pw_im[-1] * ab_im], pw_im + [pw_re[-1] * ab_im + pw_im[-1] * ab_re])
    p_re, p_im = jnp.stack(pw_re), jnp.stack(pw_im)
    w_re = cr[None] * p_re[:L, None] - ci[None] * p_im[:L, None]
    w_im = cr[None] * p_im[:L, None] + ci[None] * p_re[:L, None]
    lag_k = jnp.sum(w_re[:, None] * bb_re[None, :, None] - w_im[:, None] * bb_im[None, :, None], axis=3)
    q_re, q_im = jnp.stack(pw_re[L - 1::-1]), jnp.stack(pw_im[L - 1::-1])
    bc_re = q_re[:, None] * bb_re[None] - q_im[:, None] * bb_im[None]
    bc_im = q_re[:, None] * bb_im[None] + q_im[:, None] * bb_re[None]
    o_re = cr[None] * p_re[1:, None] - ci[None] * p_im[1:, None]
    o_im = cr[None] * p_im[1:, None] + ci[None] * p_re[1:, None]
    octs = lambda t: t.reshape(t.shape[:-1] + (n_oct, GROUPS_PER_TILE))
    to_lag = lambda t: octs(t).transpose(3, 0, 1, 4, 2).reshape(n_oct, L, C, LANES)
    to_rows = lambda t: octs(t).transpose(3, 0, 1, 4, 2).reshape(n_oct, L * C, GROUPS_PER_TILE * N)
    to_cols = lambda t: octs(t).transpose(3, 4, 2, 0, 1).reshape(n_oct, GROUPS_PER_TILE * N, L * C)
    to_vec = lambda t: octs(t).transpose(1, 2, 0).reshape(n_oct, 1, GROUPS_PER_TILE * N)
    return dict(
        lag=to_lag(lag_k).astype(BF16),
        bc_re=to_rows(bc_re).astype(BF16), bc_im=to_rows(bc_im).astype(BF16),
        oc_re=to_cols(o_re).astype(BF16), oc_im_neg=to_cols(-o_im).astype(BF16),
        al_re=to_vec(p_re[L]), al_im=to_vec(p_im[L]), d=d_skip.astype(F32).reshape(n_oct, 1, LANES),
    )


def _ssm_expanders():
    L, C, N, G8 = SSM_CHUNK, SSM_GROUP, SSM_STATE, GROUPS_PER_TILE
    wide = np.arange(L * LANES)
    w_step, w_group, w_chan = wide // LANES, (wide // C) % G8, wide % C
    small = np.arange(L * C)
    s_step, s_chan = small // C, small % C
    state_group = np.arange(G8 * N) // N
    spread = (w_step[:, None] == s_step[None, :]) & (w_chan[:, None] == s_chan[None, :])
    lane = np.arange(LANES)
    as_bf16 = lambda a: jnp.asarray(a.astype(np.float32), dtype=BF16)
    return dict(
        spread_cols=as_bf16(spread.T),
        keep_cols=as_bf16(state_group[:, None] == w_group[None, :]),
        spread_lag=as_bf16((lane % C)[:, None] == np.arange(C)[None, :]),
        keep_lag=jnp.asarray(((lane // C)[:, None] == (lane // C)[None, :]).astype(np.float32)),
    )


def _ssm_kernel(u_ref, lag_ref, bxr_ref, bxi_ref, cxr_ref, cxi_ref, alr_ref, ali_ref, d_ref,
                sc_ref, kc_ref, sl_ref, kl_ref, o_ref,
                ub_ref, bcr_ref, bci_ref, ocr_ref, oci_ref, zr_ref, zi_ref, hr_ref, hi_ref, y_ref, *, bsz):
    L = SSM_CHUNK
    rows = u_ref.shape[0] // L
    chunks = rows // bsz
    tile = 2 * LANES
    step_rows = lambda r: pl.ds(r, rows, stride=L)
    for r in range(L):
        ub_ref[:, r * LANES:(r + 1) * LANES] = u_ref[step_rows(r), :].astype(BF16)
    state_group = lax.broadcasted_iota(jnp.int32, (1, bxr_ref.shape[1]), 1) // SSM_STATE
    for compact_ref, full_ref in ((bxr_ref, bcr_ref), (bxi_ref, bci_ref)):
        for r in range(L):
            blk = compact_ref[r * SSM_GROUP:(r + 1) * SSM_GROUP, :]
            for grp in range(GROUPS_PER_TILE):
                first = r * LANES + grp * SSM_GROUP
                full_ref[first:first + SSM_GROUP, :] = jnp.where(state_group == grp, blk, jnp.zeros_like(blk))
    zr_ref[...] = _dot(ub_ref[...], bcr_ref[...])
    zi_ref[...] = _dot(ub_ref[...], bci_ref[...])
    a_re, a_im = alr_ref[...], ali_ref[...]

    def step(k, carry):
        new = []
        for b in range(bsz):
            h_re, h_im = carry[2 * b], carry[2 * b + 1]
            row = pl.ds(b * chunks + k, 1)
            hr_ref[row, :] = h_re
            hi_ref[row, :] = h_im
            new.append(a_re * h_re - a_im * h_im + zr_ref[row, :])
            new.append(a_re * h_im + a_im * h_re + zi_ref[row, :])
        return tuple(new)

    zero = jnp.zeros((1, a_re.shape[1]), F32)
    lax.fori_loop(0, chunks, step, (zero,) * (2 * bsz))

    ocr_ref[...] = (_dot(cxr_ref[...], sc_ref[...]) * kc_ref[...]).astype(BF16)
    oci_ref[...] = (_dot(cxi_ref[...], sc_ref[...]) * kc_ref[...]).astype(BF16)
    lag = [(_dot(sl_ref[...], lag_ref[d]) * kl_ref[...]).astype(BF16) for d in range(L)]
    lag_tile = []
    for dd in range(L // 2):
        below = lag[2 * dd - 1] if dd > 0 else jnp.zeros((LANES, LANES), BF16)
        lag_tile.append(jnp.concatenate([jnp.concatenate([lag[2 * dd], lag[2 * dd + 1]], axis=1),
                                         jnp.concatenate([below, lag[2 * dd]], axis=1)], axis=0))
    hb_re, hb_im = hr_ref[...].astype(BF16), hi_ref[...].astype(BF16)
    for t2 in range(L // 2):
        cols = slice(t2 * tile, (t2 + 1) * tile)
        y = _dot(hb_re, ocr_ref[:, cols]) + _dot(hb_im, oci_ref[:, cols])
        for r2 in range(t2 + 1):
            y = y + _dot(ub_ref[:, r2 * tile:(r2 + 1) * tile], lag_tile[t2 - r2])
        for half in range(2):
            t = 2 * t2 + half
            y_ref[step_rows(t), :] = jax.nn.gelu(y[:, half * LANES:(half + 1) * LANES] + d_ref[...] * u_ref[step_rows(t), :])

    def chunk(rows):
        o_ref[rows, :] = y_ref[rows, :].astype(o_ref.dtype)
    _for_row_chunks(o_ref.shape[0], chunk)


def _ssm_core(proj, ops, bsz, seq):
    m = proj.shape[0]
    width = proj.shape[1] // 2
    L, C = SSM_CHUNK, SSM_GROUP
    n_oct = width // LANES
    rows = m // L
    ow = L * LANES
    n_state = GROUPS_PER_TILE * SSM_STATE
    ex = _ssm_expanders()
    per_oct = lambda *shape: pl.BlockSpec((None,) + shape, lambda p: (p,) + (0,) * len(shape))
    const = lambda a: pl.BlockSpec(a.shape, lambda p: (0,) * a.ndim, pipeline_mode=pl.Buffered(1))
    consts = [ex['spread_cols'], ex['keep_cols'], ex['spread_lag'], ex['keep_lag']]
    return pl.pallas_call(
        functools.partial(_ssm_kernel, bsz=bsz),
        grid=(n_oct,),
        in_specs=[
            pl.BlockSpec((m, LANES), lambda p: (0, p)),
            per_oct(L, C, LANES), per_oct(L * C, n_state), per_oct(L * C, n_state),
            per_oct(n_state, L * C), per_oct(n_state, L * C), per_oct(1, n_state), per_oct(1, n_state), per_oct(1, LANES),
        ] + [const(a) for a in consts],
        out_specs=pl.BlockSpec((m, LANES), lambda p: (0, p)),
        out_shape=jax.ShapeDtypeStruct((m, width), BF16),
        scratch_shapes=[pltpu.VMEM((rows, ow), BF16)] + [pltpu.VMEM((ow, n_state), BF16)] * 2
        + [pltpu.VMEM((n_state, ow), BF16)] * 2 + [pltpu.VMEM((rows, n_state), F32)] * 4 + [pltpu.VMEM((m, LANES), F32)],
        compiler_params=_params(1),
        name="ssm_core",
    )(proj, ops['lag'], ops['bc_re'], ops['bc_im'], ops['oc_re'], ops['oc_im_neg'], ops['al_re'], ops['al_im'], ops['d'], *consts)


def _glu_kernel(y_ref, wa_ref, wb_ref, ba_ref, bb_ref, gate_ref, o_ref):
    y = y_ref[...]
    ga = _dot(y, wa_ref[...]) + ba_ref[...]
    gb = _dot(y, wb_ref[...]) + bb_ref[...]
    o_ref[...] = ((ga * jax.nn.sigmoid(gb)) * _silu(gate_ref[...])).astype(o_ref.dtype)


def _glu(y, w_glu, layer, b_glu, proj, tm=1024, tn=512):
    m, k = y.shape
    width = w_glu.shape[2] // 2
    tm, tn = min(tm, m), min(tn, width)
    nt = width // tn
    w_tiles = _tile_cols(w_glu, layer, tn)
    return pl.pallas_call(
        _glu_kernel,
        grid=(m // tm, nt),
        in_specs=[
            pl.BlockSpec((tm, k), lambda i, j: (i, 0)),
            pl.BlockSpec((None, k, tn), lambda i, j: (j, 0, 0)),
            pl.BlockSpec((None, k, tn), lambda i, j: (nt + j, 0, 0)),
            pl.BlockSpec((1, tn), lambda i, j: (0, j)),
            pl.BlockSpec((1, tn), lambda i, j: (0, nt + j)),
            pl.BlockSpec((tm, tn), lambda i, j: (i, nt + j)),
        ],
        out_specs=pl.BlockSpec((tm, tn), lambda i, j: (i, j)),
        out_shape=jax.ShapeDtypeStruct((m, width), BF16),
        compiler_params=_params(2),
        name="ssm_glu",
    )(y, w_tiles, w_tiles, b_glu, b_glu, proj)


def _ssm_mixer(x, g, w_in, layer, lam_re, lam_im, log_dt, b_re, b_im, c_re, c_im, d_skip, w_glu, b_glu, bsz, seq):
    proj = _norm_matmul(x, g, _tile_cols(w_in, layer, MATMUL_TN), F32)
    ops = _ssm_operators(lam_re, lam_im, log_dt, b_re, b_im, c_re, c_im, d_skip)
    y = _ssm_core(proj, ops, bsz, seq)
    return _glu(y, w_glu, layer, b_glu.astype(F32).reshape(1, -1), proj)


CUM_BLOCK = 128


def _split3(x):
    x1 = x.astype(BF16)
    r1 = x - x1.astype(F32)
    x2 = r1.astype(BF16)
    x3 = (r1 - x2.astype(F32)).astype(BF16)
    return x1, x2, x3


def _forget_cumsum_kernel(z_ref, b_ref, o_ref):
    n_blocks = z_ref.shape[0] // CUM_BLOCK
    ri = lax.broadcasted_iota(jnp.int32, (CUM_BLOCK, CUM_BLOCK), 0)
    ci = lax.broadcasted_iota(jnp.int32, (CUM_BLOCK, CUM_BLOCK), 1)
    tri = jnp.where(ci <= ri, 1.0, 0.0).astype(BF16)

    def body(i, carry):
        rows = pl.ds(pl.multiple_of(i * CUM_BLOCK, CUM_BLOCK), CUM_BLOCK)
        z = z_ref[rows, :] + b_ref[...]
        log_f = jnp.minimum(z, 0.0) - jnp.log1p(jnp.exp(-jnp.abs(z)))
        x1, x2, x3 = _split3(log_f)
        c = _dot(tri, x1) + _dot(tri, x2) + _dot(tri, x3) + carry
        o_ref[rows, :] = c
        return c[CUM_BLOCK - 1:, :]

    lax.fori_loop(0, n_blocks, body, jnp.zeros((1, z_ref.shape[1]), F32))


def _forget_cumsum(z, b, bsz, seq):
    lanes = z.shape[1]
    return pl.pallas_call(
        _forget_cumsum_kernel,
        grid=(bsz,),
        in_specs=[pl.BlockSpec((seq, lanes), lambda i: (i, 0)), pl.BlockSpec((1, lanes), lambda i: (0, 0))],
        out_specs=pl.BlockSpec((seq, lanes), lambda i: (i, 0)),
        out_shape=jax.ShapeDtypeStruct(z.shape, F32),
        compiler_params=_params(1),
        name="forget_cumsum",
    )(z, b)


FOX_STEP_HEADS = 8


def _fox_kernel(q_ref, k_ref, v_ref, sg_ref, cq_ref, ck_ref, o_ref, ka_ref, vt_ref, *head_refs, tk):
    nh = FOX_STEP_HEADS
    hg, qi = pl.program_id(1), pl.program_id(2)
    tq = q_ref.shape[0]
    qa_ref, s_ref, p_ref, acc_ref, pv_ref, m_ref = (head_refs[i::6] for i in range(6))
    lane = lax.broadcasted_iota(jnp.int32, (1, LANES), 1)
    free = (HEAD_DIM, 0)
    n_pieces = 3
    pair_lanes = lambda h: slice(h // 2 * LANES, (h // 2 + 1) * LANES)

    def bias_lanes(c, h, c_offset, one_offset):
        rr = lax.broadcasted_iota(jnp.int32, (n_pieces * LANES, LANES), 0)
        cc = lax.broadcasted_iota(jnp.int32, (n_pieces * LANES, LANES), 1)
        place = jnp.logical_and(rr % LANES == nh * hg + h, cc == free[h % 2] + c_offset + rr // LANES)
        placed = _dot(jnp.concatenate(_split3(c), axis=1), jnp.where(place, 1.0, 0.0).astype(BF16))
        first = free[h % 2] + one_offset
        return (placed + jnp.where(jnp.logical_and(lane >= first, lane < first + n_pieces), 1.0, 0.0)).astype(BF16)

    @pl.when(qi == 0)
    def _():
        def chunk(j, carry):
            rows = pl.ds(pl.multiple_of(j * tk, tk), tk)
            neg_c = ck_ref[rows, :] * -LOG2E
            for h in range(nh):
                keys = k_ref[rows, pair_lanes(h)]
                ka_ref[h, j] = jnp.where((lane // HEAD_DIM) == h % 2, keys, bias_lanes(neg_c, h, 0, n_pieces))
            for pair in range(nh // 2):
                v_t = v_ref[rows, pair_lanes(2 * pair)].astype(F32).T.astype(BF16)
                for hh in range(2):
                    vt_ref[j, 2 * pair + hh, :HEAD_DIM, :] = v_t[hh * HEAD_DIM:(hh + 1) * HEAD_DIM]
                    vt_ref[j, 2 * pair + hh, HEAD_DIM:, :] = jnp.ones((ONES_ROWS, tk), BF16)
            return carry
        lax.fori_loop(0, k_ref.shape[0] // tk, chunk, 0)

    cq = cq_ref[...] * LOG2E
    for h in range(nh):
        qa = jnp.where((lane // HEAD_DIM) == h % 2, q_ref[:, pair_lanes(h)], bias_lanes(cq, h, n_pieces, 0))
        qa_ref[h][...] = qa.astype(F32).T.astype(BF16)

    def scores(h, j):
        s_ref[h][...] = _dot(ka_ref[h, j], qa_ref[h][...])

    rows8 = 8
    rows16 = 16

    def absorb(h, j, diagonal):
        def strip(r, n):
            blk = s_ref[h][r:r + n, :]
            if diagonal:
                kpos = lax.broadcasted_iota(jnp.int32, (n, tq), 0)
                qpos = lax.broadcasted_iota(jnp.int32, (n, tq), 1)
                blk = jnp.where(kpos + r <= qpos, blk, NEG)
            return blk
        top = strip(0, rows8)
        for r in range(rows8, tk, rows8):
            top = jnp.maximum(top, strip(r, rows8))
        m = m_ref[h][...]
        m_new = jnp.maximum(m, top.max(axis=0, keepdims=True))
        m_ref[h][...] = m_new
        alpha = jnp.exp2(m - m_new)
        m_rows = jnp.broadcast_to(m_new, (rows16, tq))
        for r in range(0, tk, rows16):
            p_ref[h][r:r + rows16, :] = jnp.exp2(strip(r, rows16) - m_rows).astype(BF16)
        acc_ref[h][...] = alpha * (acc_ref[h][...] + pv_ref[h][...])

    def values(h, j):
        pv_ref[h][...] = _dot(vt_ref[j, h], p_ref[h][...])

    def step(j, diagonal):
        values(nh - 1, jnp.maximum(j - 1, 0))
        scores(1, j)
        for h in range(nh):
            absorb(h, j, diagonal)
            if h < nh - 1:
                values(h, j)
            if h + 2 < nh:
                scores(h + 2, j)
            elif h + 2 == nh and not diagonal:
                scores(0, j + 1)

    for h in range(nh):
        acc_ref[h][...] = jnp.zeros(acc_ref[h].shape, F32)
        pv_ref[h][...] = jnp.zeros(pv_ref[h].shape, F32)
        m_ref[h][...] = jnp.full(m_ref[h].shape, NEG, F32)
    p_ref[nh - 1][...] = jnp.zeros(p_ref[nh - 1].shape, BF16)
    scores(0, 0)

    def two_blocks(i, carry):
        step(2 * i, False)
        step(2 * i + 1, False)
        return carry
    lax.fori_loop(0, qi // 2, two_blocks, 0)

    @pl.when(qi % 2 == 0)
    def _():
        step(qi, True)

    @pl.when(qi % 2 == 1)
    def _():
        step(qi - 1, False)
        step(qi, True)

    values(nh - 1, qi)
    outs = []
    for h in range(nh):
        total = acc_ref[h][...] + pv_ref[h][...]
        outs.append(total[:HEAD_DIM] / total[HEAD_DIM:HEAD_DIM + 1])
    out = jnp.concatenate(outs, axis=0)
    o_ref[...] = (out.T * sg_ref[...].astype(F32)).astype(o_ref.dtype)


def _fox_attention(proj, csum, bsz, seq, t=512):
    width = FOX_HEADS * HEAD_DIM
    nh = FOX_STEP_HEADS
    gw = nh * HEAD_DIM
    n_groups = width // gw
    t = min(t, seq)
    nq = seq // t
    return pl.pallas_call(
        functools.partial(_fox_kernel, tk=t),
        grid=(bsz, n_groups, nq),
        in_specs=[
            pl.BlockSpec((t, gw), lambda b, h, i: (b * nq + i, h)),
            pl.BlockSpec((seq, gw), lambda b, h, i: (b, n_groups + h), pipeline_mode=pl.Buffered(1)),
            pl.BlockSpec((seq, gw), lambda b, h, i: (b, 2 * n_groups + h), pipeline_mode=pl.Buffered(1)),
            pl.BlockSpec((t, gw), lambda b, h, i: (b * nq + i, 3 * n_groups + h)),
            pl.BlockSpec((t, LANES), lambda b, h, i: (b * nq + i, 0)),
            pl.BlockSpec((seq, LANES), lambda b, h, i: (b, 0), pipeline_mode=pl.Buffered(1)),
        ],
        out_specs=pl.BlockSpec((t, gw), lambda b, h, i: (b * nq + i, h)),
        out_shape=jax.ShapeDtypeStruct((bsz * seq, width), BF16),
        scratch_shapes=[
            pltpu.VMEM((nh, nq, t, LANES), BF16), pltpu.VMEM((nq, nh, HEAD_DIM + ONES_ROWS, t), BF16),
        ] + nh * [pltpu.VMEM((LANES, t), BF16), pltpu.VMEM((t, t), F32), pltpu.VMEM((t, t), BF16),
                  pltpu.VMEM((HEAD_DIM + ONES_ROWS, t), F32), pltpu.VMEM((HEAD_DIM + ONES_ROWS, t), F32),
                  pltpu.VMEM((1, t), F32)],
        compiler_params=_params(3, VMEM_LIMIT_BIG),
        name="fox_attention",
    )(proj, proj, proj, proj, csum, csum)


def _fox_mixer(x, g, w_in, layer, w_fg, b_fg, bsz, seq, t=512):
    width = FOX_HEADS * HEAD_DIM
    col_scale = jnp.where(jnp.arange(4 * width) < width, LOG2E * HEAD_DIM ** -0.5, 1.0).astype(F32)
    proj = _norm_matmul(x, g, _tile_cols(w_in, layer, MATMUL_TN, col_scale), BF16, silu_cols=(3 * width, 4 * width))
    pad = LANES - FOX_HEADS
    z = _norm_matmul(x, g, _tile_cols_xla(jnp.pad(w_fg, ((0, 0), (0, pad))), LANES), F32)
    csum = _forget_cumsum(z, jnp.pad(b_fg.astype(F32), (0, pad)).reshape(1, LANES), bsz, seq)
    return _fox_attention(proj, csum, bsz, seq, t)


def kernel(x, p, norm_g, final_g, rel_bias, swa_w_in, swa_w_out, swa_sinks, conv_w_in, conv_kernel, conv_w_out, ssm_w_in, ssm_lam_re, ssm_lam_im, ssm_log_dt, ssm_b_re, ssm_b_im, ssm_c_re, ssm_c_im, ssm_d, ssm_w_glu, ssm_b_glu, ssm_w_out, fox_w_in, fox_w_fg, fox_b_fg, fox_w_out, ple_proj, ple_norm, ple_gate):
    bsz, seq, d_model = x.shape
    depth = p.shape[0]
    h = x.astype(F32).reshape(bsz * seq, d_model)
    for i in range(depth):
        mixer, j = i % N_MIXERS, i // N_MIXERS
        if mixer == 0:
            a, w_out = _swa_mixer(h, norm_g[i], swa_w_in, j, swa_sinks[j], rel_bias, bsz, seq), swa_w_out[j]
        elif mixer == 1:
            a, w_out = _conv_mixer(h, norm_g[i], conv_w_in, j, conv_kernel[j], seq), conv_w_out[j]
        elif mixer == 2:
            a = _ssm_mixer(h, norm_g[i], ssm_w_in, j, ssm_lam_re[j], ssm_lam_im[j], ssm_log_dt[j], ssm_b_re[j], ssm_b_im[j],
                           ssm_c_re[j], ssm_c_im[j], ssm_d[j], ssm_w_glu, ssm_b_glu[j], bsz, seq)
            w_out = ssm_w_out[j]
        else:
            a, w_out = _fox_mixer(h, norm_g[i], fox_w_in, j, fox_w_fg[j], fox_b_fg[j], bsz, seq), fox_w_out[j]
        h = _out_ple(a, w_out, h, ple_norm[i], ple_gate, p.reshape(depth, bsz * seq, -1), ple_proj, i,
                     final_g=final_g if i == depth - 1 else None)
    return h.reshape(bsz, seq, d_model).astype(x.dtype)
```

```python
import functools
import math

import numpy as np
import jax
import jax.numpy as jnp
from jax import lax
from jax.experimental import pallas as pl
from jax.experimental.pallas import tpu as pltpu

F32 = jnp.float32
BF16 = jnp.bfloat16

EPS = 1e-6
N_MIXERS = 4

SWA_HEADS = 32
SWA_KV_HEADS = 4
SWA_GROUP = SWA_HEADS // SWA_KV_HEADS
HEAD_DIM = 64
SWA_BLOCK = 128
WINDOW = 128
REL_BUCKETS = 32
REL_MAX_DIST = 128

CONV_TAPS = 3

SSM_GROUP = 16
SSM_STATE = 64
SSM_CHUNK = 16

FOX_HEADS = 32

LANES = 128
ONES_ROWS = 16
GROUPS_PER_TILE = LANES // SSM_GROUP
VMEM_LIMIT = 48 * 1024 * 1024
VMEM_LIMIT_BIG = 60 * 1024 * 1024

NEG = float(jnp.finfo(jnp.float32).min)
LOG2E = math.log2(math.e)


def _params(n_axes, vmem_limit=VMEM_LIMIT):
    return pltpu.CompilerParams(dimension_semantics=("arbitrary",) * n_axes, vmem_limit_bytes=vmem_limit)


def _dot(a, b):
    return jnp.dot(a, b, preferred_element_type=F32)


def _dot_nt(a, b):
    return lax.dot_general(a, b, (((1,), (1,)), ((), ())), preferred_element_type=F32)


def _rmsnorm_rows(x, g):
    return x * lax.rsqrt(jnp.mean(x * x, axis=-1, keepdims=True) + EPS) * g


def _silu(x):
    return x * jax.nn.sigmoid(x)


ROW_CHUNK = 64


def _for_row_chunks(n_rows, fn):
    def body(c, carry):
        fn(pl.ds(pl.multiple_of(c * ROW_CHUNK, ROW_CHUNK), ROW_CHUNK))
        return carry
    lax.fori_loop(0, n_rows // ROW_CHUNK, body, 0)


def _tile_cols_xla(w, tn):
    k, n = w.shape
    return w.astype(BF16).reshape(k, n // tn, tn).transpose(1, 0, 2)


def _cast_tiles_kernel(w_ref, o_ref):
    def chunk(rows):
        o_ref[rows, :] = w_ref[rows, :].astype(BF16)
    _for_row_chunks(o_ref.shape[0], chunk)


def _scale_cast_tiles_kernel(w_ref, s_ref, o_ref):
    def chunk(rows):
        o_ref[rows, :] = (w_ref[rows, :] * s_ref[...]).astype(BF16)
    _for_row_chunks(o_ref.shape[0], chunk)


def _tile_cols(w, layer, tn, col_scale=None):
    _, k, n = w.shape
    tn = min(tn, n)
    w_spec = pl.BlockSpec((None, k, tn), lambda j: (layer, 0, j))
    common = dict(
        grid=(n // tn,),
        out_specs=pl.BlockSpec((None, k, tn), lambda j: (j, 0, 0)),
        out_shape=jax.ShapeDtypeStruct((n // tn, k, tn), BF16),
        compiler_params=_params(1),
        name="weight_tiles",
    )
    if col_scale is None:
        return pl.pallas_call(_cast_tiles_kernel, in_specs=[w_spec], **common)(w)
    s_spec = pl.BlockSpec((1, tn), lambda j: (0, j))
    return pl.pallas_call(_scale_cast_tiles_kernel, in_specs=[w_spec, s_spec], **common)(w, col_scale.reshape(1, n))


def _lookahead_x_spec(tm, k, n_row_tiles):
    def tile(i, j):
        first_step = jnp.logical_and(i == 0, j == 0)
        return jnp.where(first_step, 0, jnp.minimum(i + 1, n_row_tiles - 1))
    return pl.BlockSpec((tm, k), lambda i, j: (tile(i, j), 0))


def _lookahead_rows(tm, nt):
    return -(-tm // ((nt - 1) * ROW_CHUNK)) * ROW_CHUNK


def _lookahead_first(x_ref, g_ref, a_ref):
    @pl.when(jnp.logical_and(pl.program_id(0) == 0, pl.program_id(1) == 0))
    def _():
        def chunk(rows):
            a_ref[0, rows, :] = _rmsnorm_rows(x_ref[rows, :], g_ref[...]).astype(BF16)
        _for_row_chunks(a_ref.shape[1], chunk)


def _lookahead_next(x_ref, g_ref, a_ref, rows_per_step):
    i, j = pl.program_id(0), pl.program_id(1)
    tm = a_ref.shape[1]
    start = jnp.clip((j - 1) * rows_per_step, 0, tm - rows_per_step)
    for c in range(rows_per_step // ROW_CHUNK):
        rows = pl.ds(pl.multiple_of(start + c * ROW_CHUNK, ROW_CHUNK), ROW_CHUNK)
        a_ref[(i + 1) % 2, rows, :] = _rmsnorm_rows(x_ref[rows, :], g_ref[...]).astype(BF16)


def _norm_matmul_kernel(x_ref, g_ref, w_ref, o_ref, a_ref, *, silu_tiles, rows_per_step):
    i, j = pl.program_id(0), pl.program_id(1)
    if rows_per_step is None:
        def chunk(rows):
            a_ref[0, rows, :] = _rmsnorm_rows(x_ref[rows, :], g_ref[...]).astype(BF16)
        _for_row_chunks(a_ref.shape[1], chunk)
        acc = _dot(a_ref[0], w_ref[...])
    else:
        _lookahead_first(x_ref, g_ref, a_ref)
        acc = _dot(a_ref[i % 2], w_ref[...])
        _lookahead_next(x_ref, g_ref, a_ref, rows_per_step)
    if silu_tiles is None:
        o_ref[...] = acc.astype(o_ref.dtype)
    else:
        gated = jnp.logical_and(j >= silu_tiles[0], j < silu_tiles[1])

        @pl.when(gated)
        def _():
            o_ref[...] = _silu(acc).astype(o_ref.dtype)

        @pl.when(jnp.logical_not(gated))
        def _():
            o_ref[...] = acc.astype(o_ref.dtype)


MATMUL_TN = 1024


def _norm_matmul(x, g, w_tiles, out_dtype, silu_cols=None, tm=1024):
    m, k = x.shape
    nt, _, tn = w_tiles.shape
    n = nt * tn
    tm = min(tm, m)
    n_row_tiles = m // tm
    silu_tiles = None if silu_cols is None else (silu_cols[0] // tn, silu_cols[1] // tn)
    lookahead = nt > 1
    return pl.pallas_call(
        functools.partial(_norm_matmul_kernel, silu_tiles=silu_tiles,
                          rows_per_step=_lookahead_rows(tm, nt) if lookahead else None),
        grid=(n_row_tiles, nt),
        in_specs=[
            _lookahead_x_spec(tm, k, n_row_tiles) if lookahead else pl.BlockSpec((tm, k), lambda i, j: (i, 0)),
            pl.BlockSpec((1, k), lambda i, j: (0, 0)),
            pl.BlockSpec((None, k, tn), lambda i, j: (j, 0, 0)),
        ],
        out_specs=pl.BlockSpec((tm, tn), lambda i, j: (i, j)),
        out_shape=jax.ShapeDtypeStruct((m, n), out_dtype),
        scratch_shapes=[pltpu.VMEM((2, tm, k), BF16)],
        compiler_params=_params(2, VMEM_LIMIT_BIG),
        name="norm_matmul",
    )(x, g.reshape(1, k), w_tiles)


def _out_ple_kernel(a_ref, wo_ref, x_ref, g_ref, wg_ref, p_ref, wp_ref, gf_ref, o_ref, x1_ref, hn_ref, ss_ref, *, nt, final):
    j = pl.program_id(1)
    tn = x_ref.shape[1]
    n = nt * tn

    @pl.when(j == 0)
    def _():
        ss_ref[...] = jnp.zeros(ss_ref.shape, F32)

    @pl.when(j < nt)
    def _():
        x1 = x_ref[...] + _dot(a_ref[...], wo_ref[j])
        x1_ref[j] = x1
        hn_ref[j] = (x1 * g_ref[j]).astype(BF16)
        ss_ref[0] += jnp.sum(x1 * x1, axis=-1, keepdims=True)

    @pl.when(jnp.logical_and(j >= nt, j < 2 * nt))
    def _():
        t = j - nt
        emb = _dot(p_ref[...].astype(BF16), wp_ref[t])
        acc = _dot(hn_ref[0], wg_ref[t, :tn, :])
        for kt in range(1, nt):
            acc = acc + _dot(hn_ref[kt], wg_ref[t, kt * tn:(kt + 1) * tn, :])
        x2 = x1_ref[t] + emb * jax.nn.sigmoid(acc * lax.rsqrt(ss_ref[0] / n + EPS))
        if final:
            x1_ref[t] = x2
            ss_ref[1] += jnp.sum(x2 * x2, axis=-1, keepdims=True)
        else:
            o_ref[...] = x2

    if final:
        @pl.when(j >= 2 * nt)
        def _():
            t = j - 2 * nt
            o_ref[...] = x1_ref[t] * lax.rsqrt(ss_ref[1] / n + EPS) * gf_ref[t]


def _out_ple(a, w_out, x, g, w_gate, p, w_proj, layer, final_g=None, tm=1024, tn=512):
    m, k = a.shape
    n = w_out.shape[1]
    pd = p.shape[2]
    final = final_g is not None
    tm, tn = min(tm, m), min(tn, n)
    nt = n // tn
    passes = 3 if final else 2
    out_tile = lambda j: jnp.maximum(j - (passes - 1) * nt, 0)
    resident = lambda rows: pl.BlockSpec((nt, rows, tn), lambda i, j: (0, 0, 0), pipeline_mode=pl.Buffered(1))
    row_vec = pl.BlockSpec((nt, 1, tn), lambda i, j: (0, 0, 0))
    return pl.pallas_call(
        functools.partial(_out_ple_kernel, nt=nt, final=final),
        grid=(m // tm, passes * nt),
        in_specs=[
            pl.BlockSpec((tm, k), lambda i, j: (i, 0)),
            resident(k),
            pl.BlockSpec((tm, tn), lambda i, j: (i, jnp.minimum(j, nt - 1))),
            row_vec,
            resident(n),
            pl.BlockSpec((None, tm, pd), lambda i, j: (layer, i, 0)),
            resident(pd),
            row_vec,
        ],
        out_specs=pl.BlockSpec((tm, tn), lambda i, j: (i, out_tile(j))),
        out_shape=jax.ShapeDtypeStruct((m, n), F32),
        scratch_shapes=[pltpu.VMEM((nt, tm, tn), F32), pltpu.VMEM((nt, tm, tn), BF16), pltpu.VMEM((2, tm, 1), F32)],
        compiler_params=_params(2, VMEM_LIMIT_BIG),
        name="out_ple_final" if final else "out_ple",
    )(a, _tile_cols(w_out[None], 0, tn), x, g.reshape(nt, 1, tn), _tile_cols(w_gate, layer, tn), p,
      _tile_cols(w_proj, layer, tn), (final_g if final else g).reshape(nt, 1, tn))


def _t5_bucket(dist):
    max_exact = REL_BUCKETS // 2
    d = np.maximum(dist, 1).astype(np.float32)
    large = max_exact + (np.log(d / max_exact) / np.log(REL_MAX_DIST / max_exact) * (REL_BUCKETS - max_exact)).astype(np.int32)
    large = np.minimum(large, REL_BUCKETS - 1)
    return np.where(dist < max_exact, dist, large).astype(np.int32)


assert WINDOW == SWA_BLOCK


def _swa_bucket_table():
    qi = np.arange(SWA_BLOCK)[None, :]
    kj = np.arange(2 * SWA_BLOCK)[:, None]
    return _t5_bucket(np.clip(qi + SWA_BLOCK - kj, 0, None))


def _swa_kernel(sink_ref, rel_ref, bucket_ref, q_ref, sg_ref, kp_ref, kc_ref, vp_ref, vc_ref, o_ref, bias_ref, qs_ref):
    n = pl.program_id(1)
    blk = SWA_BLOCK
    lane = lax.broadcasted_iota(jnp.int32, (1, LANES), 1)

    @pl.when(jnp.logical_and(pl.program_id(0) == 0, n == 0))
    def _():
        bucket = bucket_ref[...]
        kj = lax.broadcasted_iota(jnp.int32, (2 * blk, blk), 0)
        qi = lax.broadcasted_iota(jnp.int32, (2 * blk, blk), 1)
        band = jnp.logical_or(jnp.logical_and(kj < blk, kj > qi), jnp.logical_and(kj >= blk, kj - blk <= qi))

        def per_head(head, carry):
            acc = jnp.zeros((2 * blk, blk), F32)
            for b in range(REL_BUCKETS):
                acc = jnp.where(bucket == b, rel_ref[b, head] * LOG2E, acc)
            bias_ref[head] = jnp.where(band, acc, NEG)
            return carry
        lax.fori_loop(0, SWA_HEADS, per_head, 0)

    for kvh in range(SWA_KV_HEADS):
        cols = slice(kvh * LANES, (kvh + 1) * LANES)
        heads = range(kvh * SWA_GROUP, (kvh + 1) * SWA_GROUP)
        for g, head in enumerate(heads):
            q_pair = q_ref[:, head // 2 * LANES:(head // 2 + 1) * LANES]
            qs_ref[g * blk:(g + 1) * blk, :] = jnp.where((lane // HEAD_DIM) == head % 2, q_pair, jnp.zeros_like(q_pair))
        keys = jnp.concatenate([kp_ref[:, cols], kc_ref[:, cols]], axis=0)
        vals = jnp.concatenate([vp_ref[:, cols], vc_ref[:, cols]], axis=0)
        s = _dot_nt(keys, qs_ref[...]) + jnp.concatenate([bias_ref[head] for head in heads], axis=1)
        s_prev = jnp.where(n > 0, s[:blk], NEG)
        s_cur = s[blk:]
        sink = jnp.concatenate([jnp.full((1, blk), sink_ref[head] * LOG2E, F32) for head in heads], axis=1)
        m = jnp.maximum(jnp.maximum(s_prev, s_cur).max(axis=0, keepdims=True), sink)
        e = jnp.concatenate([jnp.exp2(s_prev - m).astype(BF16), jnp.exp2(s_cur - m).astype(BF16)], axis=0)
        v_t = jnp.concatenate([vals.astype(F32).T.astype(BF16)[:HEAD_DIM], jnp.ones((ONES_ROWS, 2 * blk), BF16)], axis=0)
        out = _dot(v_t, e)
        out = out[:HEAD_DIM] / (out[HEAD_DIM:HEAD_DIM + 1] + jnp.exp2(sink - m))
        for pair in range(SWA_GROUP // 2):
            head = kvh * SWA_GROUP + 2 * pair
            even = out[:, 2 * pair * blk:(2 * pair + 1) * blk]
            odd = out[:, (2 * pair + 1) * blk:(2 * pair + 2) * blk]
            cols = slice(head // 2 * LANES, (head // 2 + 1) * LANES)
            o_ref[:, cols] = (jnp.concatenate([even, odd], axis=0).T * sg_ref[:, cols].astype(F32)).astype(o_ref.dtype)


def _swa_attention(proj, sinks, rel_bias, bsz, seq):
    nb = seq // SWA_BLOCK
    width = SWA_HEADS * HEAD_DIM
    kvw = SWA_KV_HEADS * LANES
    k_blk = 2 * width // kvw
    cur = lambda b, n: b * nb + n
    prev = lambda b, n: b * nb + jnp.maximum(n - 1, 0)
    smem = pl.BlockSpec(memory_space=pltpu.SMEM)
    return pl.pallas_call(
        _swa_kernel,
        grid=(bsz, nb),
        in_specs=[
            smem, smem,
            pl.BlockSpec((2 * SWA_BLOCK, SWA_BLOCK), lambda b, n: (0, 0)),
            pl.BlockSpec((SWA_BLOCK, width), lambda b, n: (cur(b, n), 0)),
            pl.BlockSpec((SWA_BLOCK, width), lambda b, n: (cur(b, n), 1)),
            pl.BlockSpec((SWA_BLOCK, kvw), lambda b, n: (prev(b, n), k_blk)),
            pl.BlockSpec((SWA_BLOCK, kvw), lambda b, n: (cur(b, n), k_blk)),
            pl.BlockSpec((SWA_BLOCK, kvw), lambda b, n: (prev(b, n), k_blk + 1)),
            pl.BlockSpec((SWA_BLOCK, kvw), lambda b, n: (cur(b, n), k_blk + 1)),
        ],
        out_specs=pl.BlockSpec((SWA_BLOCK, width), lambda b, n: (cur(b, n), 0)),
        out_shape=jax.ShapeDtypeStruct((bsz * seq, width), BF16),
        scratch_shapes=[
            pltpu.VMEM((SWA_HEADS, 2 * SWA_BLOCK, SWA_BLOCK), F32),
            pltpu.VMEM((SWA_GROUP * SWA_BLOCK, LANES), BF16),
        ],
        compiler_params=_params(2),
        name="swa_attention",
    )(sinks.astype(F32), rel_bias.astype(F32), jnp.asarray(_swa_bucket_table()), proj, proj, proj, proj, proj, proj)


SWA_HALF_TILE = 512


def _swa_weight_tiles_kernel(wl_ref, wr_ref, o_ref, *, q_tiles, gate_tiles, q_scale):
    j = pl.program_id(0)
    half = SWA_HALF_TILE
    kvw = SWA_KV_HEADS * HEAD_DIM

    def rows_do(fn):
        def chunk(rows):
            for side, w_ref in enumerate((wl_ref, wr_ref)):
                o_ref[rows, side * half:(side + 1) * half] = fn(side, w_ref, rows)
        _for_row_chunks(o_ref.shape[0], chunk)

    @pl.when(j < q_tiles)
    def _():
        rows_do(lambda side, w_ref, rows: (w_ref[rows, :] * q_scale).astype(BF16))

    @pl.when(jnp.logical_and(j >= q_tiles, j < q_tiles + gate_tiles))
    def _():
        rows_do(lambda side, w_ref, rows: w_ref[rows, :].astype(BF16))

    @pl.when(j == q_tiles + gate_tiles)
    def _():
        def twice(side, w_ref, rows):
            w = w_ref[rows, side * kvw:(side + 1) * kvw].astype(BF16)
            heads = [w[:, h * HEAD_DIM:(h + 1) * HEAD_DIM] for h in range(SWA_KV_HEADS)]
            return jnp.concatenate([piece for head in heads for piece in (head, head)], axis=1)
        rows_do(twice)


def _swa_weight_tiles(w_in, layer, q_scale):
    _, k, n = w_in.shape
    half = SWA_HALF_TILE
    width = SWA_HEADS * HEAD_DIM
    q_tiles = gate_tiles = width // (2 * half)
    kv_block = width // half
    n_out = q_tiles + gate_tiles + 1

    def src(j, side):
        in_q, in_gate = j < q_tiles, j < q_tiles + gate_tiles
        return jnp.where(in_q, 2 * j + side, jnp.where(in_gate, 2 * j + side + 1, kv_block))
    return pl.pallas_call(
        functools.partial(_swa_weight_tiles_kernel, q_tiles=q_tiles, gate_tiles=gate_tiles, q_scale=q_scale),
        grid=(n_out,),
        in_specs=[pl.BlockSpec((None, k, half), lambda j: (layer, 0, src(j, 0))),
                  pl.BlockSpec((None, k, half), lambda j: (layer, 0, src(j, 1)))],
        out_specs=pl.BlockSpec((None, k, 2 * half), lambda j: (j, 0, 0)),
        out_shape=jax.ShapeDtypeStruct((n_out, k, 2 * half), BF16),
        compiler_params=_params(1),
        name="swa_weight_tiles",
    )(w_in, w_in)


def _swa_mixer(x, g, w_in, layer, sinks, rel_bias, bsz, seq):
    width = SWA_HEADS * HEAD_DIM
    w_tiles = _swa_weight_tiles(w_in, layer, LOG2E * HEAD_DIM ** -0.5)
    proj = _norm_matmul(x, g, w_tiles, BF16, silu_cols=(width, 2 * width))
    return _swa_attention(proj, sinks, rel_bias, bsz, seq)


HALO = 8


def _conv_kernel(x_ref, g_ref, wb_ref, wc_ref, wu_ref, wg_ref, ck_ref, o_ref, a_ref, z_ref, carry_ref, *,
                 tiles_per_seq, rows_per_step):
    i, j = pl.program_id(0), pl.program_id(1)
    tm = a_ref.shape[1]
    _lookahead_first(x_ref, g_ref, a_ref)
    a = a_ref[i % 2]
    z = _dot(a, wc_ref[...]) * _dot(a, wu_ref[...])
    first = (i % tiles_per_seq) == 0

    @pl.when(first)
    def _():
        z_ref[:HALO, :] = jnp.zeros((HALO, z_ref.shape[1]), F32)

    @pl.when(jnp.logical_not(first))
    def _():
        z_ref[:HALO, :] = carry_ref[j]

    z_ref[HALO:, :] = z
    carry_ref[j] = z[tm - HALO:, :]
    conv = z_ref[HALO - 2:HALO - 2 + tm, :] * ck_ref[0:1, :]
    conv = conv + z_ref[HALO - 1:HALO - 1 + tm, :] * ck_ref[1:2, :]
    conv = conv + z * ck_ref[2:3, :]
    y = _dot(a, wb_ref[...]) * conv
    o_ref[...] = (y * _silu(_dot(a, wg_ref[...]))).astype(o_ref.dtype)
    _lookahead_next(x_ref, g_ref, a_ref, rows_per_step)


def _conv_mixer(x, g, w_in, layer, conv_kernel, seq, tm=1024, tn=512):
    m, k = x.shape
    width = w_in.shape[2] // 4
    tm, tn = min(tm, seq), min(tn, width)
    nt = width // tn
    w_spec = lambda q: pl.BlockSpec((None, k, tn), lambda i, j: (q * nt + j, 0, 0))
    w_in = _tile_cols(w_in, layer, tn)
    return pl.pallas_call(
        functools.partial(_conv_kernel, tiles_per_seq=seq // tm, rows_per_step=_lookahead_rows(tm, nt)),
        grid=(m // tm, nt),
        in_specs=[
            _lookahead_x_spec(tm, k, m // tm),
            pl.BlockSpec((1, k), lambda i, j: (0, 0)),
            w_spec(0), w_spec(1), w_spec(2), w_spec(3),
            pl.BlockSpec((CONV_TAPS, tn), lambda i, j: (0, j)),
        ],
        out_specs=pl.BlockSpec((tm, tn), lambda i, j: (i, j)),
        out_shape=jax.ShapeDtypeStruct((m, width), BF16),
        scratch_shapes=[
            pltpu.VMEM((2, tm, k), BF16),
            pltpu.VMEM((HALO + tm, tn), F32),
            pltpu.VMEM((nt, HALO, tn), F32),
        ],
        compiler_params=_params(2, VMEM_LIMIT_BIG),
        name="conv_mixer",
    )(x, g.reshape(1, k), w_in, w_in, w_in, w_in, conv_kernel.astype(F32))


def _ssm_operators(lam_re, lam_im, log_dt, b_re, b_im, c_re, c_im, d_skip):
    n_groups = lam_re.shape[0]
    n_oct = n_groups // GROUPS_PER_TILE
    L, C, N = SSM_CHUNK, SSM_GROUP, SSM_STATE
    dt = jnp.exp(log_dt.astype(F32))[None, :]
    lr, li = lam_re.astype(F32).T, lam_im.astype(F32).T
    mag = jnp.exp(lr * dt)
    ab_re, ab_im = mag * jnp.cos(li * dt), mag * jnp.sin(li * dt)
    den = lr * lr + li * li
    nr = ab_re - 1.0
    coef_re = (nr * lr + ab_im * li) / den
    coef_im = (ab_im * lr - nr * li) / den
    br, bi = b_re.astype(F32).transpose(2, 1, 0), b_im.astype(F32).transpose(2, 1, 0)
    bb_re = coef_re * br - coef_im * bi
    bb_im = coef_re * bi + coef_im * br
    cr, ci = c_re.astype(F32).transpose(1, 2, 0), c_im.astype(F32).transpose(1, 2, 0)
    pw_re, pw_im = [jnp.ones_like(ab_re)], [jnp.zeros_like(ab_im)]
    for _ in range(L):
        pw_re, pw_im = (pw_re + [pw_re[-1] * ab_re - pw_im[-1] * ab_im], pw_im + [pw_re[-1] * ab_im + pw_im[-1] * ab_re])
    p_re, p_im = jnp.stack(pw_re), jnp.stack(pw_im)
    w_re = cr[None] * p_re[:L, None] - ci[None] * p_im[:L, None]
    w_im = cr[None] * p_im[:L, None] + ci[None] * p_re[:L, None]
    lag_k = jnp.sum(w_re[:, None] * bb_re[None, :, None] - w_im[:, None] * bb_im[None, :, None], axis=3)
    q_re, q_im = jnp.stack(pw_re[L - 1::-1]), jnp.stack(pw_im[L - 1::-1])
    bc_re = q_re[:, None] * bb_re[None] - q_im[:, None] * bb_im[None]
    bc_im = q_re[:, None] * bb_im[None] + q_im[:, None] * bb_re[None]
    o_re = cr[None] * p_re[1:, None] - ci[None] * p_im[1:, None]
    o_im = cr[None] * p_im[1:, None] + ci[None] * p_re[1:, None]
    octs = lambda t: t.reshape(t.shape[:-1] + (n_oct, GROUPS_PER_TILE))
    to_lag = lambda t: octs(t).transpose(3, 0, 1, 4, 2).reshape(n_oct, L, C, LANES)
    to_rows = lambda t: octs(t).transpose(3, 0, 1, 4, 2).reshape(n_oct, L * C, GROUPS_PER_TILE * N)
    to_cols = lambda t: octs(t).transpose(3, 4, 2, 0, 1).reshape(n_oct, GROUPS_PER_TILE * N, L * C)
    to_vec = lambda t: octs(t).transpose(1, 2, 0).reshape(n_oct, 1, GROUPS_PER_TILE * N)
    return dict(
        lag=to_lag(lag_k).astype(BF16),
        bc_re=to_rows(bc_re).astype(BF16), bc_im=to_rows(bc_im).astype(BF16),
        oc_re=to_cols(o_re).astype(BF16), oc_im_neg=to_cols(-o_im).astype(BF16),
        al_re=to_vec(p_re[L]), al_im=to_vec(p_im[L]), d=d_skip.astype(F32).reshape(n_oct, 1, LANES),
    )


def _ssm_expanders():
    L, C, N, G8 = SSM_CHUNK, SSM_GROUP, SSM_STATE, GROUPS_PER_TILE
    wide = np.arange(L * LANES)
    w_step, w_group, w_chan = wide // LANES, (wide // C) % G8, wide % C
    small = np.arange(L * C)
    s_step, s_chan = small // C, small % C
    state_group = np.arange(G8 * N) // N
    spread = (w_step[:, None] == s_step[None, :]) & (w_chan[:, None] == s_chan[None, :])
    lane = np.arange(LANES)
    as_bf16 = lambda a: jnp.asarray(a.astype(np.float32), dtype=BF16)
    return dict(
        spread_cols=as_bf16(spread.T),
        keep_cols=as_bf16(state_group[:, None] == w_group[None, :]),
        spread_lag=as_bf16((lane % C)[:, None] == np.arange(C)[None, :]),
        keep_lag=jnp.asarray(((lane // C)[:, None] == (lane // C)[None, :]).astype(np.float32)),
    )


def _ssm_kernel(u_ref, lag_ref, bxr_ref, bxi_ref, cxr_ref, cxi_ref, alr_ref, ali_ref, d_ref,
                sc_ref, kc_ref, sl_ref, kl_ref, o_ref,
                ub_ref, bcr_ref, bci_ref, ocr_ref, oci_ref, zr_ref, zi_ref, hr_ref, hi_ref, y_ref, *, bsz):
    L = SSM_CHUNK
    rows = u_ref.shape[0] // L
    chunks = rows // bsz
    tile = 2 * LANES
    step_rows = lambda r: pl.ds(r, rows, stride=L)
    for r in range(L):
        ub_ref[:, r * LANES:(r + 1) * LANES] = u_ref[step_rows(r), :].astype(BF16)
    state_group = lax.broadcasted_iota(jnp.int32, (1, bxr_ref.shape[1]), 1) // SSM_STATE
    for compact_ref, full_ref in ((bxr_ref, bcr_ref), (bxi_ref, bci_ref)):
        for r in range(L):
            blk = compact_ref[r * SSM_GROUP:(r + 1) * SSM_GROUP, :]
            for grp in range(GROUPS_PER_TILE):
                first = r * LANES + grp * SSM_GROUP
                full_ref[first:first + SSM_GROUP, :] = jnp.where(state_group == grp, blk, jnp.zeros_like(blk))
    zr_ref[...] = _dot(ub_ref[...], bcr_ref[...])
    zi_ref[...] = _dot(ub_ref[...], bci_ref[...])
    a_re, a_im = alr_ref[...], ali_ref[...]

    def step(k, carry):
        new = []
        for b in range(bsz):
            h_re, h_im = carry[2 * b], carry[2 * b + 1]
            row = pl.ds(b * chunks + k, 1)
            hr_ref[row, :] = h_re
            hi_ref[row, :] = h_im
            new.append(a_re * h_re - a_im * h_im + zr_ref[row, :])
            new.append(a_re * h_im + a_im * h_re + zi_ref[row, :])
        return tuple(new)

    zero = jnp.zeros((1, a_re.shape[1]), F32)
    lax.fori_loop(0, chunks, step, (zero,) * (2 * bsz), unroll=True)

    ocr_ref[...] = (_dot(cxr_ref[...], sc_ref[...]) * kc_ref[...]).astype(BF16)
    oci_ref[...] = (_dot(cxi_ref[...], sc_ref[...]) * kc_ref[...]).astype(BF16)
    lag = [(_dot(sl_ref[...], lag_ref[d]) * kl_ref[...]).astype(BF16) for d in range(L)]
    lag_tile = []
    for dd in range(L // 2):
        below = lag[2 * dd - 1] if dd > 0 else jnp.zeros((LANES, LANES), BF16)
        lag_tile.append(jnp.concatenate([jnp.concatenate([lag[2 * dd], lag[2 * dd + 1]], axis=1),
                                         jnp.concatenate([below, lag[2 * dd]], axis=1)], axis=0))
    hb_re, hb_im = hr_ref[...].astype(BF16), hi_ref[...].astype(BF16)
    for t2 in range(L // 2):
        cols = slice(t2 * tile, (t2 + 1) * tile)
        y = _dot(hb_re, ocr_ref[:, cols]) + _dot(hb_im, oci_ref[:, cols])
        for r2 in range(t2 + 1):
            y = y + _dot(ub_ref[:, r2 * tile:(r2 + 1) * tile], lag_tile[t2 - r2])
        for half in range(2):
            t = 2 * t2 + half
            y_ref[step_rows(t), :] = jax.nn.gelu(y[:, half * LANES:(half + 1) * LANES] + d_ref[...] * u_ref[step_rows(t), :])

    def chunk(rows):
        o_ref[rows, :] = y_ref[rows, :].astype(o_ref.dtype)
    _for_row_chunks(o_ref.shape[0], chunk)


def _ssm_core(proj, ops, bsz, seq):
    m = proj.shape[0]
    width = proj.shape[1] // 2
    L, C = SSM_CHUNK, SSM_GROUP
    n_oct = width // LANES
    rows = m // L
    ow = L * LANES
    n_state = GROUPS_PER_TILE * SSM_STATE
    ex = _ssm_expanders()
    per_oct = lambda *shape: pl.BlockSpec((None,) + shape, lambda p: (p,) + (0,) * len(shape))
    const = lambda a: pl.BlockSpec(a.shape, lambda p: (0,) * a.ndim, pipeline_mode=pl.Buffered(1))
    consts = [ex['spread_cols'], ex['keep_cols'], ex['spread_lag'], ex['keep_lag']]
    return pl.pallas_call(
        functools.partial(_ssm_kernel, bsz=bsz),
        grid=(n_oct,),
        in_specs=[
            pl.BlockSpec((m, LANES), lambda p: (0, p)),
            per_oct(L, C, LANES), per_oct(L * C, n_state), per_oct(L * C, n_state),
            per_oct(n_state, L * C), per_oct(n_state, L * C), per_oct(1, n_state), per_oct(1, n_state), per_oct(1, LANES),
        ] + [const(a) for a in consts],
        out_specs=pl.BlockSpec((m, LANES), lambda p: (0, p)),
        out_shape=jax.ShapeDtypeStruct((m, width), BF16),
        scratch_shapes=[pltpu.VMEM((rows, ow), BF16)] + [pltpu.VMEM((ow, n_state), BF16)] * 2
        + [pltpu.VMEM((n_state, ow), BF16)] * 2 + [pltpu.VMEM((rows, n_state), F32)] * 4 + [pltpu.VMEM((m, LANES), F32)],
        compiler_params=_params(1),
        name="ssm_core",
    )(proj, ops['lag'], ops['bc_re'], ops['bc_im'], ops['oc_re'], ops['oc_im_neg'], ops['al_re'], ops['al_im'], ops['d'], *consts)


def _glu_kernel(y_ref, wa_ref, wb_ref, ba_ref, bb_ref, gate_ref, o_ref):
    y = y_ref[...]
    ga = _dot(y, wa_ref[...]) + ba_ref[...]
    gb = _dot(y, wb_ref[...]) + bb_ref[...]
    o_ref[...] = ((ga * jax.nn.sigmoid(gb)) * _silu(gate_ref[...])).astype(o_ref.dtype)


def _glu(y, w_glu, layer, b_glu, proj, tm=1024, tn=512):
    m, k = y.shape
    width = w_glu.shape[2] // 2
    tm, tn = min(tm, m), min(tn, width)
    nt = width // tn
    w_tiles = _tile_cols(w_glu, layer, tn)
    return pl.pallas_call(
        _glu_kernel,
        grid=(m // tm, nt),
        in_specs=[
            pl.BlockSpec((tm, k), lambda i, j: (i, 0)),
            pl.BlockSpec((None, k, tn), lambda i, j: (j, 0, 0)),
            pl.BlockSpec((None, k, tn), lambda i, j: (nt + j, 0, 0)),
            pl.BlockSpec((1, tn), lambda i, j: (0, j)),
            pl.BlockSpec((1, tn), lambda i, j: (0, nt + j)),
            pl.BlockSpec((tm, tn), lambda i, j: (i, nt + j)),
        ],
        out_specs=pl.BlockSpec((tm, tn), lambda i, j: (i, j)),
        out_shape=jax.ShapeDtypeStruct((m, width), BF16),
        compiler_params=_params(2),
        name="ssm_glu",
    )(y, w_tiles, w_tiles, b_glu, b_glu, proj)


def _ssm_mixer(x, g, w_in, layer, lam_re, lam_im, log_dt, b_re, b_im, c_re, c_im, d_skip, w_glu, b_glu, bsz, seq):
    proj = _norm_matmul(x, g, _tile_cols(w_in, layer, MATMUL_TN), F32)
    ops = _ssm_operators(lam_re, lam_im, log_dt, b_re, b_im, c_re, c_im, d_skip)
    y = _ssm_core(proj, ops, bsz, seq)
    return _glu(y, w_glu, layer, b_glu.astype(F32).reshape(1, -1), proj)


CUM_BLOCK = 128


def _split3(x):
    x1 = x.astype(BF16)
    r1 = x - x1.astype(F32)
    x2 = r1.astype(BF16)
    x3 = (r1 - x2.astype(F32)).astype(BF16)
    return x1, x2, x3


def _forget_cumsum_kernel(z_ref, b_ref, o_ref):
    n_blocks = z_ref.shape[0] // CUM_BLOCK
    ri = lax.broadcasted_iota(jnp.int32, (CUM_BLOCK, CUM_BLOCK), 0)
    ci = lax.broadcasted_iota(jnp.int32, (CUM_BLOCK, CUM_BLOCK), 1)
    tri = jnp.where(ci <= ri, 1.0, 0.0).astype(BF16)

    def body(i, carry):
        rows = pl.ds(pl.multiple_of(i * CUM_BLOCK, CUM_BLOCK), CUM_BLOCK)
        z = z_ref[rows, :] + b_ref[...]
        log_f = jnp.minimum(z, 0.0) - jnp.log1p(jnp.exp(-jnp.abs(z)))
        x1, x2, x3 = _split3(log_f)
        c = _dot(tri, x1) + _dot(tri, x2) + _dot(tri, x3) + carry
        o_ref[rows, :] = c
        return c[CUM_BLOCK - 1:, :]

    lax.fori_loop(0, n_blocks, body, jnp.zeros((1, z_ref.shape[1]), F32))


def _forget_cumsum(z, b, bsz, seq):
    lanes = z.shape[1]
    return pl.pallas_call(
        _forget_cumsum_kernel,
        grid=(bsz,),
        in_specs=[pl.BlockSpec((seq, lanes), lambda i: (i, 0)), pl.BlockSpec((1, lanes), lambda i: (0, 0))],
        out_specs=pl.BlockSpec((seq, lanes), lambda i: (i, 0)),
        out_shape=jax.ShapeDtypeStruct(z.shape, F32),
        compiler_params=_params(1),
        name="forget_cumsum",
    )(z, b)


FOX_STEP_HEADS = 8


def _fox_kernel(q_ref, k_ref, v_ref, sg_ref, cq_ref, ck_ref, o_ref, ka_ref, vt_ref, *head_refs, tk):
    nh = FOX_STEP_HEADS
    hg, qi = pl.program_id(1), pl.program_id(2)
    tq = q_ref.shape[0]
    qa_ref, s_ref, p_ref, acc_ref, pv_ref, m_ref = (head_refs[i::6] for i in range(6))
    lane = lax.broadcasted_iota(jnp.int32, (1, LANES), 1)
    free = (HEAD_DIM, 0)
    n_pieces = 3
    pair_lanes = lambda h: slice(h // 2 * LANES, (h // 2 + 1) * LANES)

    def bias_lanes(c, h, c_offset, one_offset):
        rr = lax.broadcasted_iota(jnp.int32, (n_pieces * LANES, LANES), 0)
        cc = lax.broadcasted_iota(jnp.int32, (n_pieces * LANES, LANES), 1)
        place = jnp.logical_and(rr % LANES == nh * hg + h, cc == free[h % 2] + c_offset + rr // LANES)
        placed = _dot(jnp.concatenate(_split3(c), axis=1), jnp.where(place, 1.0, 0.0).astype(BF16))
        first = free[h % 2] + one_offset
        return (placed + jnp.where(jnp.logical_and(lane >= first, lane < first + n_pieces), 1.0, 0.0)).astype(BF16)

    @pl.when(qi == 0)
    def _():
        def chunk(j, carry):
            rows = pl.ds(pl.multiple_of(j * tk, tk), tk)
            neg_c = ck_ref[rows, :] * -LOG2E
            for h in range(nh):
                keys = k_ref[rows, pair_lanes(h)]
                ka_ref[h, j] = jnp.where((lane // HEAD_DIM) == h % 2, keys, bias_lanes(neg_c, h, 0, n_pieces))
            for pair in range(nh // 2):
                v_t = v_ref[rows, pair_lanes(2 * pair)].astype(F32).T.astype(BF16)
                for hh in range(2):
                    vt_ref[j, 2 * pair + hh, :HEAD_DIM, :] = v_t[hh * HEAD_DIM:(hh + 1) * HEAD_DIM]
                    vt_ref[j, 2 * pair + hh, HEAD_DIM:, :] = jnp.ones((ONES_ROWS, tk), BF16)
            return carry
        lax.fori_loop(0, k_ref.shape[0] // tk, chunk, 0)

    cq = cq_ref[...] * LOG2E
    for h in range(nh):
        qa = jnp.where((lane // HEAD_DIM) == h % 2, q_ref[:, pair_lanes(h)], bias_lanes(cq, h, n_pieces, 0))
        qa_ref[h][...] = qa.astype(F32).T.astype(BF16)

    def scores(h, j):
        s_ref[h][...] = _dot(ka_ref[h, j], qa_ref[h][...])

    rows8 = 8
    rows16 = 16

    def absorb(h, j, diagonal):
        def strip(r, n):
            blk = s_ref[h][r:r + n, :]
            if diagonal:
                kpos = lax.broadcasted_iota(jnp.int32, (n, tq), 0)
                qpos = lax.broadcasted_iota(jnp.int32, (n, tq), 1)
                blk = jnp.where(kpos + r <= qpos, blk, NEG)
            return blk
        top = strip(0, rows8)
        for r in range(rows8, tk, rows8):
            top = jnp.maximum(top, strip(r, rows8))
        m = m_ref[h][...]
        m_new = jnp.maximum(m, top.max(axis=0, keepdims=True))
        m_ref[h][...] = m_new
        alpha = jnp.exp2(m - m_new)
        m_rows = jnp.broadcast_to(m_new, (rows16, tq))
        for r in range(0, tk, rows16):
            p_ref[h][r:r + rows16, :] = jnp.exp2(strip(r, rows16) - m_rows).astype(BF16)
        acc_ref[h][...] = alpha * (acc_ref[h][...] + pv_ref[h][...])

    def values(h, j):
        pv_ref[h][...] = _dot(vt_ref[j, h], p_ref[h][...])

    def step(j, diagonal):
        values(nh - 1, jnp.maximum(j - 1, 0))
        scores(1, j)
        for h in range(nh):
            absorb(h, j, diagonal)
            if h < nh - 1:
                values(h, j)
            if h + 2 < nh:
                scores(h + 2, j)
            elif h + 2 == nh and not diagonal:
                scores(0, j + 1)

    for h in range(nh):
        acc_ref[h][...] = jnp.zeros(acc_ref[h].shape, F32)
        pv_ref[h][...] = jnp.zeros(pv_ref[h].shape, F32)
        m_ref[h][...] = jnp.full(m_ref[h].shape, NEG, F32)
    p_ref[nh - 1][...] = jnp.zeros(p_ref[nh - 1].shape, BF16)
    scores(0, 0)

    def two_blocks(i, carry):
        step(2 * i, False)
        step(2 * i + 1, False)
        return carry
    lax.fori_loop(0, qi // 2, two_blocks, 0)

    @pl.when(qi % 2 == 0)
    def _():
        step(qi, True)

    @pl.when(qi % 2 == 1)
    def _():
        step(qi - 1, False)
        step(qi, True)

    values(nh - 1, qi)
    outs = []
    for h in range(nh):
        total = acc_ref[h][...] + pv_ref[h][...]
        outs.append(total[:HEAD_DIM] / total[HEAD_DIM:HEAD_DIM + 1])
    out = jnp.concatenate(outs, axis=0)
    o_ref[...] = (out.T * sg_ref[...].astype(F32)).astype(o_ref.dtype)


def _fox_attention(proj, csum, bsz, seq, t=512):
    width = FOX_HEADS * HEAD_DIM
    nh = FOX_STEP_HEADS
    gw = nh * HEAD_DIM
    n_groups = width // gw
    t = min(t, seq)
    nq = seq // t
    return pl.pallas_call(
        functools.partial(_fox_kernel, tk=t),
        grid=(bsz, n_groups, nq),
        in_specs=[
            pl.BlockSpec((t, gw), lambda b, h, i: (b * nq + i, h)),
            pl.BlockSpec((seq, gw), lambda b, h, i: (b, n_groups + h), pipeline_mode=pl.Buffered(1)),
            pl.BlockSpec((seq, gw), lambda b, h, i: (b, 2 * n_groups + h), pipeline_mode=pl.Buffered(1)),
            pl.BlockSpec((t, gw), lambda b, h, i: (b * nq + i, 3 * n_groups + h)),
            pl.BlockSpec((t, LANES), lambda b, h, i: (b * nq + i, 0)),
            pl.BlockSpec((seq, LANES), lambda b, h, i: (b, 0), pipeline_mode=pl.Buffered(1)),
        ],
        out_specs=pl.BlockSpec((t, gw), lambda b, h, i: (b * nq + i, h)),
        out_shape=jax.ShapeDtypeStruct((bsz * seq, width), BF16),
        scratch_shapes=[
            pltpu.VMEM((nh, nq, t, LANES), BF16), pltpu.VMEM((nq, nh, HEAD_DIM + ONES_ROWS, t), BF16),
        ] + nh * [pltpu.VMEM((LANES, t), BF16), pltpu.VMEM((t, t), F32), pltpu.VMEM((t, t), BF16),
                  pltpu.VMEM((HEAD_DIM + ONES_ROWS, t), F32), pltpu.VMEM((HEAD_DIM + ONES_ROWS, t), F32),
                  pltpu.VMEM((1, t), F32)],
        compiler_params=_params(3, VMEM_LIMIT_BIG),
        name="fox_attention",
    )(proj, proj, proj, proj, csum, csum)


def _fox_mixer(x, g, w_in, layer, w_fg, b_fg, bsz, seq, t=512):
    width = FOX_HEADS * HEAD_DIM
    col_scale = jnp.where(jnp.arange(4 * width) < width, LOG2E * HEAD_DIM ** -0.5, 1.0).astype(F32)
    proj = _norm_matmul(x, g, _tile_cols(w_in, layer, MATMUL_TN, col_scale), BF16, silu_cols=(3 * width, 4 * width))
    pad = LANES - FOX_HEADS
    z = _norm_matmul(x, g, _tile_cols_xla(jnp.pad(w_fg, ((0, 0), (0, pad))), LANES), F32)
    csum = _forget_cumsum(z, jnp.pad(b_fg.astype(F32), (0, pad)).reshape(1, LANES), bsz, seq)
    return _fox_attention(proj, csum, bsz, seq, t)


def kernel(x, p, norm_g, final_g, rel_bias, swa_w_in, swa_w_out, swa_sinks, conv_w_in, conv_kernel, conv_w_out, ssm_w_in, ssm_lam_re, ssm_lam_im, ssm_log_dt, ssm_b_re, ssm_b_im, ssm_c_re, ssm_c_im, ssm_d, ssm_w_glu, ssm_b_glu, ssm_w_out, fox_w_in, fox_w_fg, fox_b_fg, fox_w_out, ple_proj, ple_norm, ple_gate):
    bsz, seq, d_model = x.shape
    depth = p.shape[0]
    h = x.astype(F32).reshape(bsz * seq, d_model)
    for i in range(depth):
        mixer, j = i % N_MIXERS, i // N_MIXERS
        if mixer == 0:
            a, w_out = _swa_mixer(h, norm_g[i], swa_w_in, j, swa_sinks[j], rel_bias, bsz, seq), swa_w_out[j]
        elif mixer == 1:
            a, w_out = _conv_mixer(h, norm_g[i], conv_w_in, j, conv_kernel[j], seq), conv_w_out[j]
        elif mixer == 2:
            a = _ssm_mixer(h, norm_g[i], ssm_w_in, j, ssm_lam_re[j], ssm_lam_im[j], ssm_log_dt[j], ssm_b_re[j], ssm_b_im[j],
                           ssm_c_re[j], ssm_c_im[j], ssm_d[j], ssm_w_glu, ssm_b_glu[j], bsz, seq)
            w_out = ssm_w_out[j]
        else:
            a, w_out = _fox_mixer(h, norm_g[i], fox_w_in, j, fox_w_fg[j], fox_b_fg[j], bsz, seq), fox_w_out[j]
        h = _out_ple(a, w_out, h, ple_norm[i], ple_gate, p.reshape(depth, bsz * seq, -1), ple_proj, i,
                     final_g=final_g if i == depth - 1 else None)
    return h.reshape(bsz, seq, d_model).astype(x.dtype)
```

```python
import functools
import math

import numpy as np
import jax
import jax.numpy as jnp
from jax import lax
from jax.experimental import pallas as pl
from jax.experimental.pallas import tpu as pltpu

F32 = jnp.float32
BF16 = jnp.bfloat16

EPS = 1e-6
N_MIXERS = 4

SWA_HEADS = 32
SWA_KV_HEADS = 4
SWA_GROUP = SWA_HEADS // SWA_KV_HEADS
HEAD_DIM = 64
SWA_BLOCK = 128
WINDOW = 128
REL_BUCKETS = 32
REL_MAX_DIST = 128

CONV_TAPS = 3

SSM_GROUP = 16
SSM_STATE = 64
SSM_CHUNK = 16

FOX_HEADS = 32

LANES = 128
ONES_ROWS = 16
GROUPS_PER_TILE = LANES // SSM_GROUP
VMEM_LIMIT = 48 * 1024 * 1024
VMEM_LIMIT_BIG = 60 * 1024 * 1024

NEG = float(jnp.finfo(jnp.float32).min)
LOG2E = math.log2(math.e)


def _params(n_axes, vmem_limit=VMEM_LIMIT):
    return pltpu.CompilerParams(dimension_semantics=("arbitrary",) * n_axes, vmem_limit_bytes=vmem_limit)


def _dot(a, b):
    return jnp.dot(a, b, preferred_element_type=F32)


def _dot_nt(a, b):
    return lax.dot_general(a, b, (((1,), (1,)), ((), ())), preferred_element_type=F32)


def _rmsnorm_rows(x, g):
    return x * lax.rsqrt(jnp.mean(x * x, axis=-1, keepdims=True) + EPS) * g


def _silu(x):
    return x * jax.nn.sigmoid(x)


ROW_CHUNK = 64


def _for_row_chunks(n_rows, fn):
    def body(c, carry):
        fn(pl.ds(pl.multiple_of(c * ROW_CHUNK, ROW_CHUNK), ROW_CHUNK))
        return carry
    lax.fori_loop(0, n_rows // ROW_CHUNK, body, 0)


def _tile_cols_xla(w, tn):
    k, n = w.shape
    return w.astype(BF16).reshape(k, n // tn, tn).transpose(1, 0, 2)


def _cast_tiles_kernel(w_ref, o_ref):
    def chunk(rows):
        o_ref[rows, :] = w_ref[rows, :].astype(BF16)
    _for_row_chunks(o_ref.shape[0], chunk)


def _scale_cast_tiles_kernel(w_ref, s_ref, o_ref):
    def chunk(rows):
        o_ref[rows, :] = (w_ref[rows, :] * s_ref[...]).astype(BF16)
    _for_row_chunks(o_ref.shape[0], chunk)


def _tile_cols(w, layer, tn, col_scale=None):
    _, k, n = w.shape
    tn = min(tn, n)
    w_spec = pl.BlockSpec((None, k, tn), lambda j: (layer, 0, j))
    common = dict(
        grid=(n // tn,),
        out_specs=pl.BlockSpec((None, k, tn), lambda j: (j, 0, 0)),
        out_shape=jax.ShapeDtypeStruct((n // tn, k, tn), BF16),
        compiler_params=_params(1),
        name="weight_tiles",
    )
    if col_scale is None:
        return pl.pallas_call(_cast_tiles_kernel, in_specs=[w_spec], **common)(w)
    s_spec = pl.BlockSpec((1, tn), lambda j: (0, j))
    return pl.pallas_call(_scale_cast_tiles_kernel, in_specs=[w_spec, s_spec], **common)(w, col_scale.reshape(1, n))


def _lookahead_x_spec(tm, k, n_row_tiles):
    def tile(i, j):
        first_step = jnp.logical_and(i == 0, j == 0)
        return jnp.where(first_step, 0, jnp.minimum(i + 1, n_row_tiles - 1))
    return pl.BlockSpec((tm, k), lambda i, j: (tile(i, j), 0))


def _lookahead_rows(tm, nt):
    return -(-tm // ((nt - 1) * ROW_CHUNK)) * ROW_CHUNK


def _lookahead_first(x_ref, g_ref, a_ref):
    @pl.when(jnp.logical_and(pl.program_id(0) == 0, pl.program_id(1) == 0))
    def _():
        def chunk(rows):
            a_ref[0, rows, :] = _rmsnorm_rows(x_ref[rows, :], g_ref[...]).astype(BF16)
        _for_row_chunks(a_ref.shape[1], chunk)


def _lookahead_next(x_ref, g_ref, a_ref, rows_per_step):
    i, j = pl.program_id(0), pl.program_id(1)
    tm = a_ref.shape[1]
    start = jnp.clip((j - 1) * rows_per_step, 0, tm - rows_per_step)
    for c in range(rows_per_step // ROW_CHUNK):
        rows = pl.ds(pl.multiple_of(start + c * ROW_CHUNK, ROW_CHUNK), ROW_CHUNK)
        a_ref[(i + 1) % 2, rows, :] = _rmsnorm_rows(x_ref[rows, :], g_ref[...]).astype(BF16)


def _norm_matmul_kernel(x_ref, g_ref, w_ref, o_ref, a_ref, *, silu_tiles, rows_per_step):
    i, j = pl.program_id(0), pl.program_id(1)
    if rows_per_step is None:
        def chunk(rows):
            a_ref[0, rows, :] = _rmsnorm_rows(x_ref[rows, :], g_ref[...]).astype(BF16)
        _for_row_chunks(a_ref.shape[1], chunk)
        acc = _dot(a_ref[0], w_ref[...])
    else:
        _lookahead_first(x_ref, g_ref, a_ref)
        acc = _dot(a_ref[i % 2], w_ref[...])
        _lookahead_next(x_ref, g_ref, a_ref, rows_per_step)
    if silu_tiles is None:
        o_ref[...] = acc.astype(o_ref.dtype)
    else:
        gated = jnp.logical_and(j >= silu_tiles[0], j < silu_tiles[1])

        @pl.when(gated)
        def _():
            o_ref[...] = _silu(acc).astype(o_ref.dtype)

        @pl.when(jnp.logical_not(gated))
        def _():
            o_ref[...] = acc.astype(o_ref.dtype)


MATMUL_TN = 1024


def _norm_matmul(x, g, w_tiles, out_dtype, silu_cols=None, tm=1024):
    m, k = x.shape
    nt, _, tn = w_tiles.shape
    n = nt * tn
    tm = min(tm, m)
    n_row_tiles = m // tm
    silu_tiles = None if silu_cols is None else (silu_cols[0] // tn, silu_cols[1] // tn)
    lookahead = nt > 1
    return pl.pallas_call(
        functools.partial(_norm_matmul_kernel, silu_tiles=silu_tiles,
                          rows_per_step=_lookahead_rows(tm, nt) if lookahead else None),
        grid=(n_row_tiles, nt),
        in_specs=[
            _lookahead_x_spec(tm, k, n_row_tiles) if lookahead else pl.BlockSpec((tm, k), lambda i, j: (i, 0)),
            pl.BlockSpec((1, k), lambda i, j: (0, 0)),
            pl.BlockSpec((None, k, tn), lambda i, j: (j, 0, 0)),
        ],
        out_specs=pl.BlockSpec((tm, tn), lambda i, j: (i, j)),
        out_shape=jax.ShapeDtypeStruct((m, n), out_dtype),
        scratch_shapes=[pltpu.VMEM((2, tm, k), BF16)],
        compiler_params=_params(2, VMEM_LIMIT_BIG),
        name="norm_matmul",
    )(x, g.reshape(1, k), w_tiles)


def _out_ple_kernel(a_ref, wo_ref, x_ref, g_ref, wg_ref, p_ref, wp_ref, gf_ref, o_ref, x1_ref, hn_ref, ss_ref, *, nt, final):
    j = pl.program_id(1)
    tn = x_ref.shape[1]
    n = nt * tn

    @pl.when(j == 0)
    def _():
        ss_ref[...] = jnp.zeros(ss_ref.shape, F32)

    @pl.when(j < nt)
    def _():
        x1 = x_ref[...] + _dot(a_ref[...], wo_ref[j])
        x1_ref[j] = x1
        hn_ref[j] = (x1 * g_ref[j]).astype(BF16)
        ss_ref[0] += jnp.sum(x1 * x1, axis=-1, keepdims=True)

    @pl.when(jnp.logical_and(j >= nt, j < 2 * nt))
    def _():
        t = j - nt
        emb = _dot(p_ref[...].astype(BF16), wp_ref[t])
        acc = _dot(hn_ref[0], wg_ref[t, :tn, :])
        for kt in range(1, nt):
            acc = acc + _dot(hn_ref[kt], wg_ref[t, kt * tn:(kt + 1) * tn, :])
        x2 = x1_ref[t] + emb * jax.nn.sigmoid(acc * lax.rsqrt(ss_ref[0] / n + EPS))
        if final:
            x1_ref[t] = x2
            ss_ref[1] += jnp.sum(x2 * x2, axis=-1, keepdims=True)
        else:
            o_ref[...] = x2

    if final:
        @pl.when(j >= 2 * nt)
        def _():
            t = j - 2 * nt
            o_ref[...] = x1_ref[t] * lax.rsqrt(ss_ref[1] / n + EPS) * gf_ref[t]


def _out_ple(a, w_out, x, g, w_gate, p, w_proj, layer, final_g=None, tm=1024, tn=512):
    m, k = a.shape
    n = w_out.shape[1]
    pd = p.shape[2]
    final = final_g is not None
    tm, tn = min(tm, m), min(tn, n)
    nt = n // tn
    passes = 3 if final else 2
    out_tile = lambda j: jnp.maximum(j - (passes - 1) * nt, 0)
    resident = lambda rows: pl.BlockSpec((nt, rows, tn), lambda i, j: (0, 0, 0), pipeline_mode=pl.Buffered(1))
    row_vec = pl.BlockSpec((nt, 1, tn), lambda i, j: (0, 0, 0))
    return pl.pallas_call(
        functools.partial(_out_ple_kernel, nt=nt, final=final),
        grid=(m // tm, passes * nt),
        in_specs=[
            pl.BlockSpec((tm, k), lambda i, j: (i, 0)),
            resident(k),
            pl.BlockSpec((tm, tn), lambda i, j: (i, jnp.minimum(j, nt - 1))),
            row_vec,
            resident(n),
            pl.BlockSpec((None, tm, pd), lambda i, j: (layer, i, 0)),
            resident(pd),
            row_vec,
        ],
        out_specs=pl.BlockSpec((tm, tn), lambda i, j: (i, out_tile(j))),
        out_shape=jax.ShapeDtypeStruct((m, n), F32),
        scratch_shapes=[pltpu.VMEM((nt, tm, tn), F32), pltpu.VMEM((nt, tm, tn), BF16), pltpu.VMEM((2, tm, 1), F32)],
        compiler_params=_params(2, VMEM_LIMIT_BIG),
        name="out_ple_final" if final else "out_ple",
    )(a, _tile_cols(w_out[None], 0, tn), x, g.reshape(nt, 1, tn), _tile_cols(w_gate, layer, tn), p,
      _tile_cols(w_proj, layer, tn), (final_g if final else g).reshape(nt, 1, tn))


def _t5_bucket(dist):
    max_exact = REL_BUCKETS // 2
    d = np.maximum(dist, 1).astype(np.float32)
    large = max_exact + (np.log(d / max_exact) / np.log(REL_MAX_DIST / max_exact) * (REL_BUCKETS - max_exact)).astype(np.int32)
    large = np.minimum(large, REL_BUCKETS - 1)
    return np.where(dist < max_exact, dist, large).astype(np.int32)


assert WINDOW == SWA_BLOCK


def _swa_bucket_table():
    qi = np.arange(SWA_BLOCK)[None, :]
    kj = np.arange(2 * SWA_BLOCK)[:, None]
    return _t5_bucket(np.clip(qi + SWA_BLOCK - kj, 0, None))


def _swa_kernel(sink_ref, rel_ref, bucket_ref, q_ref, sg_ref, kp_ref, kc_ref, vp_ref, vc_ref, o_ref, bias_ref, qs_ref):
    n = pl.program_id(1)
    blk = SWA_BLOCK
    lane = lax.broadcasted_iota(jnp.int32, (1, LANES), 1)

    @pl.when(jnp.logical_and(pl.program_id(0) == 0, n == 0))
    def _():
        bucket = bucket_ref[...]
        kj = lax.broadcasted_iota(jnp.int32, (2 * blk, blk), 0)
        qi = lax.broadcasted_iota(jnp.int32, (2 * blk, blk), 1)
        band = jnp.logical_or(jnp.logical_and(kj < blk, kj > qi), jnp.logical_and(kj >= blk, kj - blk <= qi))

        def per_head(head, carry):
            acc = jnp.zeros((2 * blk, blk), F32)
            for b in range(REL_BUCKETS):
                acc = jnp.where(bucket == b, rel_ref[b, head] * LOG2E, acc)
            bias_ref[head] = jnp.where(band, acc, NEG)
            return carry
        lax.fori_loop(0, SWA_HEADS, per_head, 0)

    for kvh in range(SWA_KV_HEADS):
        cols = slice(kvh * LANES, (kvh + 1) * LANES)
        heads = range(kvh * SWA_GROUP, (kvh + 1) * SWA_GROUP)
        for g, head in enumerate(heads):
            q_pair = q_ref[:, head // 2 * LANES:(head // 2 + 1) * LANES]
            qs_ref[g * blk:(g + 1) * blk, :] = jnp.where((lane // HEAD_DIM) == head % 2, q_pair, jnp.zeros_like(q_pair))
        keys = jnp.concatenate([kp_ref[:, cols], kc_ref[:, cols]], axis=0)
        vals = jnp.concatenate([vp_ref[:, cols], vc_ref[:, cols]], axis=0)
        s = _dot_nt(keys, qs_ref[...]) + jnp.concatenate([bias_ref[head] for head in heads], axis=1)
        s_prev = jnp.where(n > 0, s[:blk], NEG)
        s_cur = s[blk:]
        sink = jnp.concatenate([jnp.full((1, blk), sink_ref[head] * LOG2E, F32) for head in heads], axis=1)
        m = jnp.maximum(jnp.maximum(s_prev, s_cur).max(axis=0, keepdims=True), sink)
        e = jnp.concatenate([jnp.exp2(s_prev - m).astype(BF16), jnp.exp2(s_cur - m).astype(BF16)], axis=0)
        v_t = jnp.concatenate([vals.astype(F32).T.astype(BF16)[:HEAD_DIM], jnp.ones((ONES_ROWS, 2 * blk), BF16)], axis=0)
        out = _dot(v_t, e)
        out = out[:HEAD_DIM] / (out[HEAD_DIM:HEAD_DIM + 1] + jnp.exp2(sink - m))
        for pair in range(SWA_GROUP // 2):
            head = kvh * SWA_GROUP + 2 * pair
            even = out[:, 2 * pair * blk:(2 * pair + 1) * blk]
            odd = out[:, (2 * pair + 1) * blk:(2 * pair + 2) * blk]
            cols = slice(head // 2 * LANES, (head // 2 + 1) * LANES)
            o_ref[:, cols] = (jnp.concatenate([even, odd], axis=0).T * sg_ref[:, cols].astype(F32)).astype(o_ref.dtype)


def _swa_attention(proj, sinks, rel_bias, bsz, seq):
    nb = seq // SWA_BLOCK
    width = SWA_HEADS * HEAD_DIM
    kvw = SWA_KV_HEADS * LANES
    k_blk = 2 * width // kvw
    cur = lambda b, n: b * nb + n
    prev = lambda b, n: b * nb + jnp.maximum(n - 1, 0)
    smem = pl.BlockSpec(memory_space=pltpu.SMEM)
    return pl.pallas_call(
        _swa_kernel,
        grid=(bsz, nb),
        in_specs=[
            smem, smem,
            pl.BlockSpec((2 * SWA_BLOCK, SWA_BLOCK), lambda b, n: (0, 0)),
            pl.BlockSpec((SWA_BLOCK, width), lambda b, n: (cur(b, n), 0)),
            pl.BlockSpec((SWA_BLOCK, width), lambda b, n: (cur(b, n), 1)),
            pl.BlockSpec((SWA_BLOCK, kvw), lambda b, n: (prev(b, n), k_blk)),
            pl.BlockSpec((SWA_BLOCK, kvw), lambda b, n: (cur(b, n), k_blk)),
            pl.BlockSpec((SWA_BLOCK, kvw), lambda b, n: (prev(b, n), k_blk + 1)),
            pl.BlockSpec((SWA_BLOCK, kvw), lambda b, n: (cur(b, n), k_blk + 1)),
        ],
        out_specs=pl.BlockSpec((SWA_BLOCK, width), lambda b, n: (cur(b, n), 0)),
        out_shape=jax.ShapeDtypeStruct((bsz * seq, width), BF16),
        scratch_shapes=[
            pltpu.VMEM((SWA_HEADS, 2 * SWA_BLOCK, SWA_BLOCK), F32),
            pltpu.VMEM((SWA_GROUP * SWA_BLOCK, LANES), BF16),
        ],
        compiler_params=_params(2),
        name="swa_attention",
    )(sinks.astype(F32), rel_bias.astype(F32), jnp.asarray(_swa_bucket_table()), proj, proj, proj, proj, proj, proj)


SWA_HALF_TILE = 512


def _swa_weight_tiles_kernel(wl_ref, wr_ref, o_ref, *, q_tiles, gate_tiles, q_scale):
    j = pl.program_id(0)
    half = SWA_HALF_TILE
    kvw = SWA_KV_HEADS * HEAD_DIM

    def rows_do(fn):
        def chunk(rows):
            for side, w_ref in enumerate((wl_ref, wr_ref)):
                o_ref[rows, side * half:(side + 1) * half] = fn(side, w_ref, rows)
        _for_row_chunks(o_ref.shape[0], chunk)

    @pl.when(j < q_tiles)
    def _():
        rows_do(lambda side, w_ref, rows: (w_ref[rows, :] * q_scale).astype(BF16))

    @pl.when(jnp.logical_and(j >= q_tiles, j < q_tiles + gate_tiles))
    def _():
        rows_do(lambda side, w_ref, rows: w_ref[rows, :].astype(BF16))

    @pl.when(j == q_tiles + gate_tiles)
    def _():
        def twice(side, w_ref, rows):
            w = w_ref[rows, side * kvw:(side + 1) * kvw].astype(BF16)
            heads = [w[:, h * HEAD_DIM:(h + 1) * HEAD_DIM] for h in range(SWA_KV_HEADS)]
            return jnp.concatenate([piece for head in heads for piece in (head, head)], axis=1)
        rows_do(twice)


def _swa_weight_tiles(w_in, layer, q_scale):
    _, k, n = w_in.shape
    half = SWA_HALF_TILE
    width = SWA_HEADS * HEAD_DIM
    q_tiles = gate_tiles = width // (2 * half)
    kv_block = width // half
    n_out = q_tiles + gate_tiles + 1

    def src(j, side):
        in_q, in_gate = j < q_tiles, j < q_tiles + gate_tiles
        return jnp.where(in_q, 2 * j + side, jnp.where(in_gate, 2 * j + side + 1, kv_block))
    return pl.pallas_call(
        functools.partial(_swa_weight_tiles_kernel, q_tiles=q_tiles, gate_tiles=gate_tiles, q_scale=q_scale),
        grid=(n_out,),
        in_specs=[pl.BlockSpec((None, k, half), lambda j: (layer, 0, src(j, 0))),
                  pl.BlockSpec((None, k, half), lambda j: (layer, 0, src(j, 1)))],
        out_specs=pl.BlockSpec((None, k, 2 * half), lambda j: (j, 0, 0)),
        out_shape=jax.ShapeDtypeStruct((n_out, k, 2 * half), BF16),
        compiler_params=_params(1),
        name="swa_weight_tiles",
    )(w_in, w_in)


def _swa_mixer(x, g, w_in, layer, sinks, rel_bias, bsz, seq):
    width = SWA_HEADS * HEAD_DIM
    w_tiles = _swa_weight_tiles(w_in, layer, LOG2E * HEAD_DIM ** -0.5)
    proj = _norm_matmul(x, g, w_tiles, BF16, silu_cols=(width, 2 * width))
    return _swa_attention(proj, sinks, rel_bias, bsz, seq)


HALO = 8


def _conv_kernel(x_ref, g_ref, wb_ref, wc_ref, wu_ref, wg_ref, ck_ref, o_ref, a_ref, z_ref, carry_ref, *,
                 tiles_per_seq, rows_per_step):
    i, j = pl.program_id(0), pl.program_id(1)
    tm = a_ref.shape[1]
    _lookahead_first(x_ref, g_ref, a_ref)
    a = a_ref[i % 2]
    cg, u, bg, gate = (_dot(a, w_ref[...]) for w_ref in (wc_ref, wu_ref, wb_ref, wg_ref))
    z = cg * u
    first = (i % tiles_per_seq) == 0

    @pl.when(first)
    def _():
        z_ref[:HALO, :] = jnp.zeros((HALO, z_ref.shape[1]), F32)

    @pl.when(jnp.logical_not(first))
    def _():
        z_ref[:HALO, :] = carry_ref[j]

    z_ref[HALO:, :] = z
    carry_ref[j] = z[tm - HALO:, :]
    conv = z_ref[HALO - 2:HALO - 2 + tm, :] * ck_ref[0:1, :]
    conv = conv + z_ref[HALO - 1:HALO - 1 + tm, :] * ck_ref[1:2, :]
    conv = conv + z * ck_ref[2:3, :]
    o_ref[...] = (bg * conv * _silu(gate)).astype(o_ref.dtype)
    _lookahead_next(x_ref, g_ref, a_ref, rows_per_step)


def _conv_mixer(x, g, w_in, layer, conv_kernel, seq, tm=1024, tn=512):
    m, k = x.shape
    width = w_in.shape[2] // 4
    tm, tn = min(tm, seq), min(tn, width)
    nt = width // tn
    w_spec = lambda q: pl.BlockSpec((None, k, tn), lambda i, j: (q * nt + j, 0, 0))
    w_in = _tile_cols(w_in, layer, tn)
    return pl.pallas_call(
        functools.partial(_conv_kernel, tiles_per_seq=seq // tm, rows_per_step=_lookahead_rows(tm, nt)),
        grid=(m // tm, nt),
        in_specs=[
            _lookahead_x_spec(tm, k, m // tm),
            pl.BlockSpec((1, k), lambda i, j: (0, 0)),
            w_spec(0), w_spec(1), w_spec(2), w_spec(3),
            pl.BlockSpec((CONV_TAPS, tn), lambda i, j: (0, j)),
        ],
        out_specs=pl.BlockSpec((tm, tn), lambda i, j: (i, j)),
        out_shape=jax.ShapeDtypeStruct((m, width), BF16),
        scratch_shapes=[
            pltpu.VMEM((2, tm, k), BF16),
            pltpu.VMEM((HALO + tm, tn), F32),
            pltpu.VMEM((nt, HALO, tn), F32),
        ],
        compiler_params=_params(2, VMEM_LIMIT_BIG),
        name="conv_mixer",
    )(x, g.reshape(1, k), w_in, w_in, w_in, w_in, conv_kernel.astype(F32))


def _ssm_operators(lam_re, lam_im, log_dt, b_re, b_im, c_re, c_im, d_skip):
    n_groups = lam_re.shape[0]
    n_oct = n_groups // GROUPS_PER_TILE
    L, C, N = SSM_CHUNK, SSM_GROUP, SSM_STATE
    dt = jnp.exp(log_dt.astype(F32))[None, :]
    lr, li = lam_re.astype(F32).T, lam_im.astype(F32).T
    mag = jnp.exp(lr * dt)
    ab_re, ab_im = mag * jnp.cos(li * dt), mag * jnp.sin(li * dt)
    den = lr * lr + li * li
    nr = ab_re - 1.0
    coef_re = (nr * lr + ab_im * li) / den
    coef_im = (ab_im * lr - nr * li) / den
    br, bi = b_re.astype(F32).transpose(2, 1, 0), b_im.astype(F32).transpose(2, 1, 0)
    bb_re = coef_re * br - coef_im * bi
    bb_im = coef_re * bi + coef_im * br
    cr, ci = c_re.astype(F32).transpose(1, 2, 0), c_im.astype(F32).transpose(1, 2, 0)
    pw_re, pw_im = [jnp.ones_like(ab_re)], [jnp.zeros_like(ab_im)]
    for _ in range(L):
        pw_re, pw_im = (pw_re + [pw_re[-1] * ab_re - pw_im[-1] * ab_im], pw_im + [pw_re[-1] * ab_im + pw_im[-1] * ab_re])
    p_re, p_im = jnp.stack(pw_re), jnp.stack(pw_im)
    w_re = cr[None] * p_re[:L, None] - ci[None] * p_im[:L, None]
    w_im = cr[None] * p_im[:L, None] + ci[None] * p_re[:L, None]
    lag_k = jnp.sum(w_re[:, None] * bb_re[None, :, None] - w_im[:, None] * bb_im[None, :, None], axis=3)
    q_re, q_im = jnp.stack(pw_re[L - 1::-1]), jnp.stack(pw_im[L - 1::-1])
    bc_re = q_re[:, None] * bb_re[None] - q_im[:, None] * bb_im[None]
    bc_im = q_re[:, None] * bb_im[None] + q_im[:, None] * bb_re[None]
    o_re = cr[None] * p_re[1:, None] - ci[None] * p_im[1:, None]
    o_im = cr[None] * p_im[1:, None] + ci[None] * p_re[1:, None]
    octs = lambda t: t.reshape(t.shape[:-1] + (n_oct, GROUPS_PER_TILE))
    to_lag = lambda t: octs(t).transpose(3, 0, 1, 4, 2).reshape(n_oct, L, C, LANES)
    to_rows = lambda t: octs(t).transpose(3, 0, 1, 4, 2).reshape(n_oct, L * C, GROUPS_PER_TILE * N)
    to_cols = lambda t: octs(t).transpose(3, 4, 2, 0, 1).reshape(n_oct, GROUPS_PER_TILE * N, L * C)
    to_vec = lambda t: octs(t).transpose(1, 2, 0).reshape(n_oct, 1, GROUPS_PER_TILE * N)
    return dict(
        lag=to_lag(lag_k).astype(BF16),
        bc_re=to_rows(bc_re).astype(BF16), bc_im=to_rows(bc_im).astype(BF16),
        oc_re=to_cols(o_re).astype(BF16), oc_im_neg=to_cols(-o_im).astype(BF16),
        al_re=to_vec(p_re[L]), al_im=to_vec(p_im[L]), d=d_skip.astype(F32).reshape(n_oct, 1, LANES),
    )


def _ssm_expanders():
    L, C, N, G8 = SSM_CHUNK, SSM_GROUP, SSM_STATE, GROUPS_PER_TILE
    wide = np.arange(L * LANES)
    w_step, w_group, w_chan = wide // LANES, (wide // C) % G8, wide % C
    small = np.arange(L * C)
    s_step, s_chan = small // C, small % C
    state_group = np.arange(G8 * N) // N
    spread = (w_step[:, None] == s_step[None, :]) & (w_chan[:, None] == s_chan[None, :])
    lane = np.arange(LANES)
    as_bf16 = lambda a: jnp.asarray(a.astype(np.float32), dtype=BF16)
    return dict(
        spread_cols=as_bf16(spread.T),
        keep_cols=as_bf16(state_group[:, None] == w_group[None, :]),
        spread_lag=as_bf16((lane % C)[:, None] == np.arange(C)[None, :]),
        keep_lag=jnp.asarray(((lane // C)[:, None] == (lane // C)[None, :]).astype(np.float32)),
    )


def _ssm_kernel(u_ref, lag_ref, bxr_ref, bxi_ref, cxr_ref, cxi_ref, alr_ref, ali_ref, d_ref,
                sc_ref, kc_ref, sl_ref, kl_ref, o_ref,
                ub_ref, bcr_ref, bci_ref, ocr_ref, oci_ref, zr_ref, zi_ref, hr_ref, hi_ref, y_ref, *, bsz):
    L = SSM_CHUNK
    rows = u_ref.shape[0] // L
    chunks = rows // bsz
    tile = 2 * LANES
    step_rows = lambda r: pl.ds(r, rows, stride=L)
    for r in range(L):
        ub_ref[:, r * LANES:(r + 1) * LANES] = u_ref[step_rows(r), :].astype(BF16)
    state_group = lax.broadcasted_iota(jnp.int32, (1, bxr_ref.shape[1]), 1) // SSM_STATE
    for compact_ref, full_ref in ((bxr_ref, bcr_ref), (bxi_ref, bci_ref)):
        for r in range(L):
            blk = compact_ref[r * SSM_GROUP:(r + 1) * SSM_GROUP, :]
            for grp in range(GROUPS_PER_TILE):
                first = r * LANES + grp * SSM_GROUP
                full_ref[first:first + SSM_GROUP, :] = jnp.where(state_group == grp, blk, jnp.zeros_like(blk))
    zr_ref[...] = _dot(ub_ref[...], bcr_ref[...])
    zi_ref[...] = _dot(ub_ref[...], bci_ref[...])
    a_re, a_im = alr_ref[...], ali_ref[...]

    def step(k, carry):
        new = []
        for b in range(bsz):
            h_re, h_im = carry[2 * b], carry[2 * b + 1]
            row = pl.ds(b * chunks + k, 1)
            hr_ref[row, :] = h_re
            hi_ref[row, :] = h_im
            new.append(a_re * h_re - a_im * h_im + zr_ref[row, :])
            new.append(a_re * h_im + a_im * h_re + zi_ref[row, :])
        return tuple(new)

    zero = jnp.zeros((1, a_re.shape[1]), F32)
    lax.fori_loop(0, chunks, step, (zero,) * (2 * bsz), unroll=True)

    ocr_ref[...] = (_dot(cxr_ref[...], sc_ref[...]) * kc_ref[...]).astype(BF16)
    oci_ref[...] = (_dot(cxi_ref[...], sc_ref[...]) * kc_ref[...]).astype(BF16)
    lag = [(_dot(sl_ref[...], lag_ref[d]) * kl_ref[...]).astype(BF16) for d in range(L)]
    lag_tile = []
    for dd in range(L // 2):
        below = lag[2 * dd - 1] if dd > 0 else jnp.zeros((LANES, LANES), BF16)
        lag_tile.append(jnp.concatenate([jnp.concatenate([lag[2 * dd], lag[2 * dd + 1]], axis=1),
                                         jnp.concatenate([below, lag[2 * dd]], axis=1)], axis=0))
    hb_re, hb_im = hr_ref[...].astype(BF16), hi_ref[...].astype(BF16)
    for t2 in range(L // 2):
        cols = slice(t2 * tile, (t2 + 1) * tile)
        y = _dot(hb_re, ocr_ref[:, cols]) + _dot(hb_im, oci_ref[:, cols])
        for r2 in range(t2 + 1):
            y = y + _dot(ub_ref[:, r2 * tile:(r2 + 1) * tile], lag_tile[t2 - r2])
        for half in range(2):
            t = 2 * t2 + half
            y_ref[step_rows(t), :] = jax.nn.gelu(y[:, half * LANES:(half + 1) * LANES] + d_ref[...] * u_ref[step_rows(t), :])

    def chunk(rows):
        o_ref[rows, :] = y_ref[rows, :].astype(o_ref.dtype)
    _for_row_chunks(o_ref.shape[0], chunk)


def _ssm_core(proj, ops, bsz, seq):
    m = proj.shape[0]
    width = proj.shape[1] // 2
    L, C = SSM_CHUNK, SSM_GROUP
    n_oct = width // LANES
    rows = m // L
    ow = L * LANES
    n_state = GROUPS_PER_TILE * SSM_STATE
    ex = _ssm_expanders()
    per_oct = lambda *shape: pl.BlockSpec((None,) + shape, lambda p: (p,) + (0,) * len(shape))
    const = lambda a: pl.BlockSpec(a.shape, lambda p: (0,) * a.ndim, pipeline_mode=pl.Buffered(1))
    consts = [ex['spread_cols'], ex['keep_cols'], ex['spread_lag'], ex['keep_lag']]
    return pl.pallas_call(
        functools.partial(_ssm_kernel, bsz=bsz),
        grid=(n_oct,),
        in_specs=[
            pl.BlockSpec((m, LANES), lambda p: (0, p)),
            per_oct(L, C, LANES), per_oct(L * C, n_state), per_oct(L * C, n_state),
            per_oct(n_state, L * C), per_oct(n_state, L * C), per_oct(1, n_state), per_oct(1, n_state), per_oct(1, LANES),
        ] + [const(a) for a in consts],
        out_specs=pl.BlockSpec((m, LANES), lambda p: (0, p)),
        out_shape=jax.ShapeDtypeStruct((m, width), BF16),
        scratch_shapes=[pltpu.VMEM((rows, ow), BF16)] + [pltpu.VMEM((ow, n_state), BF16)] * 2
        + [pltpu.VMEM((n_state, ow), BF16)] * 2 + [pltpu.VMEM((rows, n_state), F32)] * 4 + [pltpu.VMEM((m, LANES), F32)],
        compiler_params=_params(1),
        name="ssm_core",
    )(proj, ops['lag'], ops['bc_re'], ops['bc_im'], ops['oc_re'], ops['oc_im_neg'], ops['al_re'], ops['al_im'], ops['d'], *consts)


def _glu_kernel(y_ref, wa_ref, wb_ref, ba_ref, bb_ref, gate_ref, o_ref):
    y = y_ref[...]
    ga = _dot(y, wa_ref[...]) + ba_ref[...]
    gb = _dot(y, wb_ref[...]) + bb_ref[...]
    o_ref[...] = ((ga * jax.nn.sigmoid(gb)) * _silu(gate_ref[...])).astype(o_ref.dtype)


def _glu(y, w_glu, layer, b_glu, proj, tm=1024, tn=512):
    m, k = y.shape
    width = w_glu.shape[2] // 2
    tm, tn = min(tm, m), min(tn, width)
    nt = width // tn
    w_tiles = _tile_cols(w_glu, layer, tn)
    return pl.pallas_call(
        _glu_kernel,
        grid=(m // tm, nt),
        in_specs=[
            pl.BlockSpec((tm, k), lambda i, j: (i, 0)),
            pl.BlockSpec((None, k, tn), lambda i, j: (j, 0, 0)),
            pl.BlockSpec((None, k, tn), lambda i, j: (nt + j, 0, 0)),
            pl.BlockSpec((1, tn), lambda i, j: (0, j)),
            pl.BlockSpec((1, tn), lambda i, j: (0, nt + j)),
            pl.BlockSpec((tm, tn), lambda i, j: (i, nt + j)),
        ],
        out_specs=pl.BlockSpec((tm, tn), lambda i, j: (i, j)),
        out_shape=jax.ShapeDtypeStruct((m, width), BF16),
        compiler_params=_params(2),
        name="ssm_glu",
    )(y, w_tiles, w_tiles, b_glu, b_glu, proj)


def _ssm_mixer(x, g, w_in, layer, lam_re, lam_im, log_dt, b_re, b_im, c_re, c_im, d_skip, w_glu, b_glu, bsz, seq):
    proj = _norm_matmul(x, g, _tile_cols(w_in, layer, MATMUL_TN), F32)
    ops = _ssm_operators(lam_re, lam_im, log_dt, b_re, b_im, c_re, c_im, d_skip)
    y = _ssm_core(proj, ops, bsz, seq)
    return _glu(y, w_glu, layer, b_glu.astype(F32).reshape(1, -1), proj)


CUM_BLOCK = 128


def _split3(x):
    x1 = x.astype(BF16)
    r1 = x - x1.astype(F32)
    x2 = r1.astype(BF16)
    x3 = (r1 - x2.astype(F32)).astype(BF16)
    return x1, x2, x3


def _forget_cumsum_kernel(z_ref, b_ref, o_ref):
    n_blocks = z_ref.shape[0] // CUM_BLOCK
    ri = lax.broadcasted_iota(jnp.int32, (CUM_BLOCK, CUM_BLOCK), 0)
    ci = lax.broadcasted_iota(jnp.int32, (CUM_BLOCK, CUM_BLOCK), 1)
    tri = jnp.where(ci <= ri, 1.0, 0.0).astype(BF16)

    def body(i, carry):
        rows = pl.ds(pl.multiple_of(i * CUM_BLOCK, CUM_BLOCK), CUM_BLOCK)
        z = z_ref[rows, :] + b_ref[...]
        log_f = jnp.minimum(z, 0.0) - jnp.log1p(jnp.exp(-jnp.abs(z)))
        x1, x2, x3 = _split3(log_f)
        c = _dot(tri, x1) + _dot(tri, x2) + _dot(tri, x3) + carry
        o_ref[rows, :] = c
        return c[CUM_BLOCK - 1:, :]

    lax.fori_loop(0, n_blocks, body, jnp.zeros((1, z_ref.shape[1]), F32))


def _forget_cumsum(z, b, bsz, seq):
    lanes = z.shape[1]
    return pl.pallas_call(
        _forget_cumsum_kernel,
        grid=(bsz,),
        in_specs=[pl.BlockSpec((seq, lanes), lambda i: (i, 0)), pl.BlockSpec((1, lanes), lambda i: (0, 0))],
        out_specs=pl.BlockSpec((seq, lanes), lambda i: (i, 0)),
        out_shape=jax.ShapeDtypeStruct(z.shape, F32),
        compiler_params=_params(1),
        name="forget_cumsum",
    )(z, b)


FOX_STEP_HEADS = 8


def _fox_kernel(q_ref, k_ref, v_ref, sg_ref, cq_ref, ck_ref, o_ref, ka_ref, vt_ref, *head_refs, tk):
    nh = FOX_STEP_HEADS
    hg, qi = pl.program_id(1), pl.program_id(2)
    tq = q_ref.shape[0]
    qa_ref, s_ref, p_ref, acc_ref, pv_ref, m_ref = (head_refs[i::6] for i in range(6))
    lane = lax.broadcasted_iota(jnp.int32, (1, LANES), 1)
    free = (HEAD_DIM, 0)
    n_pieces = 3
    pair_lanes = lambda h: slice(h // 2 * LANES, (h // 2 + 1) * LANES)

    def bias_lanes(c, h, c_offset, one_offset):
        rr = lax.broadcasted_iota(jnp.int32, (n_pieces * LANES, LANES), 0)
        cc = lax.broadcasted_iota(jnp.int32, (n_pieces * LANES, LANES), 1)
        place = jnp.logical_and(rr % LANES == nh * hg + h, cc == free[h % 2] + c_offset + rr // LANES)
        placed = _dot(jnp.concatenate(_split3(c), axis=1), jnp.where(place, 1.0, 0.0).astype(BF16))
        first = free[h % 2] + one_offset
        return (placed + jnp.where(jnp.logical_and(lane >= first, lane < first + n_pieces), 1.0, 0.0)).astype(BF16)

    @pl.when(qi == 0)
    def _():
        def chunk(j, carry):
            rows = pl.ds(pl.multiple_of(j * tk, tk), tk)
            neg_c = ck_ref[rows, :] * -LOG2E
            for h in range(nh):
                keys = k_ref[rows, pair_lanes(h)]
                ka_ref[h, j] = jnp.where((lane // HEAD_DIM) == h % 2, keys, bias_lanes(neg_c, h, 0, n_pieces))
            for pair in range(nh // 2):
                v_t = v_ref[rows, pair_lanes(2 * pair)].astype(F32).T.astype(BF16)
                for hh in range(2):
                    vt_ref[j, 2 * pair + hh, :HEAD_DIM, :] = v_t[hh * HEAD_DIM:(hh + 1) * HEAD_DIM]
                    vt_ref[j, 2 * pair + hh, HEAD_DIM:, :] = jnp.ones((ONES_ROWS, tk), BF16)
            return carry
        lax.fori_loop(0, k_ref.shape[0] // tk, chunk, 0, unroll=True)

    cq = cq_ref[...] * LOG2E
    for h in range(nh):
        qa = jnp.where((lane // HEAD_DIM) == h % 2, q_ref[:, pair_lanes(h)], bias_lanes(cq, h, n_pieces, 0))
        qa_ref[h][...] = qa.astype(F32).T.astype(BF16)

    def scores(h, j):
        s_ref[h][...] = _dot(ka_ref[h, j], qa_ref[h][...])

    rows8 = 8
    rows16 = 16

    def absorb(h, j, diagonal):
        def strip(r, n):
            blk = s_ref[h][r:r + n, :]
            if diagonal:
                kpos = lax.broadcasted_iota(jnp.int32, (n, tq), 0)
                qpos = lax.broadcasted_iota(jnp.int32, (n, tq), 1)
                blk = jnp.where(kpos + r <= qpos, blk, NEG)
            return blk
        top = strip(0, rows8)
        for r in range(rows8, tk, rows8):
            top = jnp.maximum(top, strip(r, rows8))
        m = m_ref[h][...]
        m_new = jnp.maximum(m, top.max(axis=0, keepdims=True))
        m_ref[h][...] = m_new
        alpha = jnp.exp2(m - m_new)
        m_rows = jnp.broadcast_to(m_new, (rows16, tq))
        for r in range(0, tk, rows16):
            p_ref[h][r:r + rows16, :] = jnp.exp2(strip(r, rows16) - m_rows).astype(BF16)
        acc_ref[h][...] = alpha * (acc_ref[h][...] + pv_ref[h][...])

    def values(h, j):
        pv_ref[h][...] = _dot(vt_ref[j, h], p_ref[h][...])

    def step(j, diagonal):
        values(nh - 1, jnp.maximum(j - 1, 0))
        scores(1, j)
        for h in range(nh):
            absorb(h, j, diagonal)
            if h < nh - 1:
                values(h, j)
            if h + 2 < nh:
                scores(h + 2, j)
            elif h + 2 == nh and not diagonal:
                scores(0, j + 1)

    for h in range(nh):
        acc_ref[h][...] = jnp.zeros(acc_ref[h].shape, F32)
        pv_ref[h][...] = jnp.zeros(pv_ref[h].shape, F32)
        m_ref[h][...] = jnp.full(m_ref[h].shape, NEG, F32)
    p_ref[nh - 1][...] = jnp.zeros(p_ref[nh - 1].shape, BF16)
    scores(0, 0)

    def two_blocks(i, carry):
        step(2 * i, False)
        step(2 * i + 1, False)
        return carry
    lax.fori_loop(0, qi // 2, two_blocks, 0)

    @pl.when(qi % 2 == 0)
    def _():
        step(qi, True)

    @pl.when(qi % 2 == 1)
    def _():
        step(qi - 1, False)
        step(qi, True)

    values(nh - 1, qi)
    outs = []
    for h in range(nh):
        total = acc_ref[h][...] + pv_ref[h][...]
        outs.append(total[:HEAD_DIM] / total[HEAD_DIM:HEAD_DIM + 1])
    out = jnp.concatenate(outs, axis=0)
    o_ref[...] = (out.T * sg_ref[...].astype(F32)).astype(o_ref.dtype)


def _fox_attention(proj, csum, bsz, seq, t=512):
    width = FOX_HEADS * HEAD_DIM
    nh = FOX_STEP_HEADS
    gw = nh * HEAD_DIM
    n_groups = width // gw
    t = min(t, seq)
    nq = seq // t
    return pl.pallas_call(
        functools.partial(_fox_kernel, tk=t),
        grid=(bsz, n_groups, nq),
        in_specs=[
            pl.BlockSpec((t, gw), lambda b, h, i: (b * nq + i, h)),
            pl.BlockSpec((seq, gw), lambda b, h, i: (b, n_groups + h), pipeline_mode=pl.Buffered(1)),
            pl.BlockSpec((seq, gw), lambda b, h, i: (b, 2 * n_groups + h), pipeline_mode=pl.Buffered(1)),
            pl.BlockSpec((t, gw), lambda b, h, i: (b * nq + i, 3 * n_groups + h)),
            pl.BlockSpec((t, LANES), lambda b, h, i: (b * nq + i, 0)),
            pl.BlockSpec((seq, LANES), lambda b, h, i: (b, 0), pipeline_mode=pl.Buffered(1)),
        ],
        out_specs=pl.BlockSpec((t, gw), lambda b, h, i: (b * nq + i, h)),
        out_shape=jax.ShapeDtypeStruct((bsz * seq, width), BF16),
        scratch_shapes=[
            pltpu.VMEM((nh, nq, t, LANES), BF16), pltpu.VMEM((nq, nh, HEAD_DIM + ONES_ROWS, t), BF16),
        ] + nh * [pltpu.VMEM((LANES, t), BF16), pltpu.VMEM((t, t), F32), pltpu.VMEM((t, t), BF16),
                  pltpu.VMEM((HEAD_DIM + ONES_ROWS, t), F32), pltpu.VMEM((HEAD_DIM + ONES_ROWS, t), F32),
                  pltpu.VMEM((1, t), F32)],
        compiler_params=_params(3, VMEM_LIMIT_BIG),
        name="fox_attention",
    )(proj, proj, proj, proj, csum, csum)


def _fox_mixer(x, g, w_in, layer, w_fg, b_fg, bsz, seq, t=512):
    width = FOX_HEADS * HEAD_DIM
    col_scale = jnp.where(jnp.arange(4 * width) < width, LOG2E * HEAD_DIM ** -0.5, 1.0).astype(F32)
    proj = _norm_matmul(x, g, _tile_cols(w_in, layer, MATMUL_TN, col_scale), BF16, silu_cols=(3 * width, 4 * width))
    pad = LANES - FOX_HEADS
    z = _norm_matmul(x, g, _tile_cols_xla(jnp.pad(w_fg, ((0, 0), (0, pad))), LANES), F32)
    csum = _forget_cumsum(z, jnp.pad(b_fg.astype(F32), (0, pad)).reshape(1, LANES), bsz, seq)
    return _fox_attention(proj, csum, bsz, seq, t)


def kernel(x, p, norm_g, final_g, rel_bias, swa_w_in, swa_w_out, swa_sinks, conv_w_in, conv_kernel, conv_w_out, ssm_w_in, ssm_lam_re, ssm_lam_im, ssm_log_dt, ssm_b_re, ssm_b_im, ssm_c_re, ssm_c_im, ssm_d, ssm_w_glu, ssm_b_glu, ssm_w_out, fox_w_in, fox_w_fg, fox_b_fg, fox_w_out, ple_proj, ple_norm, ple_gate):
    bsz, seq, d_model = x.shape
    depth = p.shape[0]
    h = x.astype(F32).reshape(bsz * seq, d_model)
    for i in range(depth):
        mixer, j = i % N_MIXERS, i // N_MIXERS
        if mixer == 0:
            a, w_out = _swa_mixer(h, norm_g[i], swa_w_in, j, swa_sinks[j], rel_bias, bsz, seq), swa_w_out[j]
        elif mixer == 1:
            a, w_out = _conv_mixer(h, norm_g[i], conv_w_in, j, conv_kernel[j], seq), conv_w_out[j]
        elif mixer == 2:
            a = _ssm_mixer(h, norm_g[i], ssm_w_in, j, ssm_lam_re[j], ssm_lam_im[j], ssm_log_dt[j], ssm_b_re[j], ssm_b_im[j],
                           ssm_c_re[j], ssm_c_im[j], ssm_d[j], ssm_w_glu, ssm_b_glu[j], bsz, seq)
            w_out = ssm_w_out[j]
        else:
            a, w_out = _fox_mixer(h, norm_g[i], fox_w_in, j, fox_w_fg[j], fox_b_fg[j], bsz, seq), fox_w_out[j]
        h = _out_ple(a, w_out, h, ple_norm[i], ple_gate, p.reshape(depth, bsz * seq, -1), ple_proj, i,
                     final_g=final_g if i == depth - 1 else None)
    return h.reshape(bsz, seq, d_model).astype(x.dtype)
```

```python
import functools
import math

import numpy as np
import jax
import jax.numpy as jnp
from jax import lax
from jax.experimental import pallas as pl
from jax.experimental.pallas import tpu as pltpu

F32 = jnp.float32
BF16 = jnp.bfloat16

EPS = 1e-6
N_MIXERS = 4

SWA_HEADS = 32
SWA_KV_HEADS = 4
SWA_GROUP = SWA_HEADS // SWA_KV_HEADS
HEAD_DIM = 64
SWA_BLOCK = 128
WINDOW = 128
REL_BUCKETS = 32
REL_MAX_DIST = 128

CONV_TAPS = 3

SSM_GROUP = 16
SSM_STATE = 64
SSM_CHUNK = 16

FOX_HEADS = 32

LANES = 128
ONES_ROWS = 16
GROUPS_PER_TILE = LANES // SSM_GROUP
VMEM_LIMIT = 48 * 1024 * 1024
VMEM_LIMIT_BIG = 60 * 1024 * 1024

NEG = float(jnp.finfo(jnp.float32).min)
LOG2E = math.log2(math.e)


def _params(n_axes, vmem_limit=VMEM_LIMIT):
    return pltpu.CompilerParams(dimension_semantics=("arbitrary",) * n_axes, vmem_limit_bytes=vmem_limit)


def _dot(a, b):
    return jnp.dot(a, b, preferred_element_type=F32)


def _dot_nt(a, b):
    return lax.dot_general(a, b, (((1,), (1,)), ((), ())), preferred_element_type=F32)


def _rmsnorm_rows(x, g):
    return x * lax.rsqrt(jnp.mean(x * x, axis=-1, keepdims=True) + EPS) * g


def _silu(x):
    return x * jax.nn.sigmoid(x)


ROW_CHUNK = 64


def _for_row_chunks(n_rows, fn):
    def body(c, carry):
        fn(pl.ds(pl.multiple_of(c * ROW_CHUNK, ROW_CHUNK), ROW_CHUNK))
        return carry
    lax.fori_loop(0, n_rows // ROW_CHUNK, body, 0)


def _tile_cols_xla(w, tn):
    k, n = w.shape
    return w.astype(BF16).reshape(k, n // tn, tn).transpose(1, 0, 2)


def _cast_tiles_kernel(w_ref, o_ref):
    def chunk(rows):
        o_ref[rows, :] = w_ref[rows, :].astype(BF16)
    _for_row_chunks(o_ref.shape[0], chunk)


def _scale_cast_tiles_kernel(w_ref, s_ref, o_ref):
    def chunk(rows):
        o_ref[rows, :] = (w_ref[rows, :] * s_ref[...]).astype(BF16)
    _for_row_chunks(o_ref.shape[0], chunk)


def _tile_cols(w, layer, tn, col_scale=None):
    _, k, n = w.shape
    tn = min(tn, n)
    w_spec = pl.BlockSpec((None, k, tn), lambda j: (layer, 0, j))
    common = dict(
        grid=(n // tn,),
        out_specs=pl.BlockSpec((None, k, tn), lambda j: (j, 0, 0)),
        out_shape=jax.ShapeDtypeStruct((n // tn, k, tn), BF16),
        compiler_params=_params(1),
        name="weight_tiles",
    )
    if col_scale is None:
        return pl.pallas_call(_cast_tiles_kernel, in_specs=[w_spec], **common)(w)
    s_spec = pl.BlockSpec((1, tn), lambda j: (0, j))
    return pl.pallas_call(_scale_cast_tiles_kernel, in_specs=[w_spec, s_spec], **common)(w, col_scale.reshape(1, n))


def _lookahead_x_spec(tm, k, n_row_tiles):
    def tile(i, j):
        first_step = jnp.logical_and(i == 0, j == 0)
        return jnp.where(first_step, 0, jnp.minimum(i + 1, n_row_tiles - 1))
    return pl.BlockSpec((tm, k), lambda i, j: (tile(i, j), 0))


def _lookahead_rows(tm, nt):
    return -(-tm // ((nt - 1) * ROW_CHUNK)) * ROW_CHUNK


def _lookahead_first(x_ref, g_ref, a_ref):
    @pl.when(jnp.logical_and(pl.program_id(0) == 0, pl.program_id(1) == 0))
    def _():
        def chunk(rows):
            a_ref[0, rows, :] = _rmsnorm_rows(x_ref[rows, :], g_ref[...]).astype(BF16)
        _for_row_chunks(a_ref.shape[1], chunk)


def _lookahead_next(x_ref, g_ref, a_ref, rows_per_step):
    i, j = pl.program_id(0), pl.program_id(1)
    tm = a_ref.shape[1]
    start = jnp.clip((j - 1) * rows_per_step, 0, tm - rows_per_step)
    for c in range(rows_per_step // ROW_CHUNK):
        rows = pl.ds(pl.multiple_of(start + c * ROW_CHUNK, ROW_CHUNK), ROW_CHUNK)
        a_ref[(i + 1) % 2, rows, :] = _rmsnorm_rows(x_ref[rows, :], g_ref[...]).astype(BF16)


def _norm_matmul_kernel(x_ref, g_ref, w_ref, o_ref, a_ref, *, silu_tiles, rows_per_step):
    i, j = pl.program_id(0), pl.program_id(1)
    if rows_per_step is None:
        def chunk(rows):
            a_ref[0, rows, :] = _rmsnorm_rows(x_ref[rows, :], g_ref[...]).astype(BF16)
        _for_row_chunks(a_ref.shape[1], chunk)
        acc = _dot(a_ref[0], w_ref[...])
    else:
        _lookahead_first(x_ref, g_ref, a_ref)
        acc = _dot(a_ref[i % 2], w_ref[...])
        _lookahead_next(x_ref, g_ref, a_ref, rows_per_step)
    if silu_tiles is None:
        o_ref[...] = acc.astype(o_ref.dtype)
    else:
        gated = jnp.logical_and(j >= silu_tiles[0], j < silu_tiles[1])

        @pl.when(gated)
        def _():
            o_ref[...] = _silu(acc).astype(o_ref.dtype)

        @pl.when(jnp.logical_not(gated))
        def _():
            o_ref[...] = acc.astype(o_ref.dtype)


MATMUL_TN = 1024


def _norm_matmul(x, g, w_tiles, out_dtype, silu_cols=None, tm=1024):
    m, k = x.shape
    nt, _, tn = w_tiles.shape
    n = nt * tn
    tm = min(tm, m)
    n_row_tiles = m // tm
    silu_tiles = None if silu_cols is None else (silu_cols[0] // tn, silu_cols[1] // tn)
    lookahead = nt > 1
    return pl.pallas_call(
        functools.partial(_norm_matmul_kernel, silu_tiles=silu_tiles,
                          rows_per_step=_lookahead_rows(tm, nt) if lookahead else None),
        grid=(n_row_tiles, nt),
        in_specs=[
            _lookahead_x_spec(tm, k, n_row_tiles) if lookahead else pl.BlockSpec((tm, k), lambda i, j: (i, 0)),
            pl.BlockSpec((1, k), lambda i, j: (0, 0)),
            pl.BlockSpec((None, k, tn), lambda i, j: (j, 0, 0)),
        ],
        out_specs=pl.BlockSpec((tm, tn), lambda i, j: (i, j)),
        out_shape=jax.ShapeDtypeStruct((m, n), out_dtype),
        scratch_shapes=[pltpu.VMEM((2, tm, k), BF16)],
        compiler_params=_params(2, VMEM_LIMIT_BIG),
        name="norm_matmul",
    )(x, g.reshape(1, k), w_tiles)


def _out_ple_kernel(a_ref, wo_ref, x_ref, g_ref, wg_ref, p_ref, wp_ref, gf_ref, o_ref, x1_ref, hn_ref, ss_ref, *, nt, final):
    j = pl.program_id(1)
    tn = x_ref.shape[1]
    n = nt * tn

    @pl.when(j == 0)
    def _():
        ss_ref[...] = jnp.zeros(ss_ref.shape, F32)

    @pl.when(j < nt)
    def _():
        x1 = x_ref[...] + _dot(a_ref[...], wo_ref[j])
        x1_ref[j] = x1
        hn_ref[j] = (x1 * g_ref[j]).astype(BF16)
        ss_ref[0] += jnp.sum(x1 * x1, axis=-1, keepdims=True)

    @pl.when(j >= nt)
    def _():
        t = j - nt
        emb = _dot(p_ref[...].astype(BF16), wp_ref[t])
        acc = _dot(hn_ref[0], wg_ref[t, :tn, :])
        for kt in range(1, nt):
            acc = acc + _dot(hn_ref[kt], wg_ref[t, kt * tn:(kt + 1) * tn, :])
        x2 = x1_ref[t] + emb * jax.nn.sigmoid(acc * lax.rsqrt(ss_ref[0] / n + EPS))
        if final:
            x1_ref[t] = x2
            ss_ref[1] += jnp.sum(x2 * x2, axis=-1, keepdims=True)
        else:
            o_ref[...] = x2

    if final:
        @pl.when(j == 2 * nt - 1)
        def _():
            inv = lax.rsqrt(ss_ref[1] / n + EPS)
            for t in range(nt):
                o_ref[:, t * tn:(t + 1) * tn] = x1_ref[t] * inv * gf_ref[t]


def _out_ple(a, w_out, x, g, w_gate, p, w_proj, layer, final_g=None, tm=1024, tn=512):
    m, k = a.shape
    n = w_out.shape[1]
    pd = p.shape[2]
    final = final_g is not None
    tm, tn = min(tm, m), min(tn, n)
    nt = n // tn
    if final:
        out_spec = pl.BlockSpec((tm, n), lambda i, j: (i, 0), pipeline_mode=pl.Buffered(1))
    else:
        out_spec = pl.BlockSpec((tm, tn), lambda i, j: (i, jnp.maximum(j - nt, 0)))
    resident = lambda rows: pl.BlockSpec((nt, rows, tn), lambda i, j: (0, 0, 0), pipeline_mode=pl.Buffered(1))
    row_vec = pl.BlockSpec((nt, 1, tn), lambda i, j: (0, 0, 0))
    return pl.pallas_call(
        functools.partial(_out_ple_kernel, nt=nt, final=final),
        grid=(m // tm, 2 * nt),
        in_specs=[
            pl.BlockSpec((tm, k), lambda i, j: (i, 0)),
            resident(k),
            pl.BlockSpec((tm, tn), lambda i, j: (i, jnp.minimum(j, nt - 1))),
            row_vec,
            resident(n),
            pl.BlockSpec((None, tm, pd), lambda i, j: (layer, i, 0)),
            resident(pd),
            row_vec,
        ],
        out_specs=out_spec,
        out_shape=jax.ShapeDtypeStruct((m, n), F32),
        scratch_shapes=[pltpu.VMEM((nt, tm, tn), F32), pltpu.VMEM((nt, tm, tn), BF16), pltpu.VMEM((2, tm, 1), F32)],
        compiler_params=_params(2, VMEM_LIMIT_BIG),
        name="out_ple_final" if final else "out_ple",
    )(a, _tile_cols(w_out[None], 0, tn), x, g.reshape(nt, 1, tn), _tile_cols(w_gate, layer, tn), p,
      _tile_cols(w_proj, layer, tn), (final_g if final else g).reshape(nt, 1, tn))


def _t5_bucket(dist):
    max_exact = REL_BUCKETS // 2
    d = np.maximum(dist, 1).astype(np.float32)
    large = max_exact + (np.log(d / max_exact) / np.log(REL_MAX_DIST / max_exact) * (REL_BUCKETS - max_exact)).astype(np.int32)
    large = np.minimum(large, REL_BUCKETS - 1)
    return np.where(dist < max_exact, dist, large).astype(np.int32)


assert WINDOW == SWA_BLOCK


def _swa_bucket_table():
    qi = np.arange(SWA_BLOCK)[None, :]
    kj = np.arange(2 * SWA_BLOCK)[:, None]
    return _t5_bucket(np.clip(qi + SWA_BLOCK - kj, 0, None))


def _swa_kernel(sink_ref, rel_ref, bucket_ref, q_ref, sg_ref, kp_ref, kc_ref, vp_ref, vc_ref, o_ref, bias_ref, qs_ref):
    n = pl.program_id(1)
    blk = SWA_BLOCK
    lane = lax.broadcasted_iota(jnp.int32, (1, LANES), 1)

    @pl.when(jnp.logical_and(pl.program_id(0) == 0, n == 0))
    def _():
        bucket = bucket_ref[...]
        kj = lax.broadcasted_iota(jnp.int32, (2 * blk, blk), 0)
        qi = lax.broadcasted_iota(jnp.int32, (2 * blk, blk), 1)
        band = jnp.logical_or(jnp.logical_and(kj < blk, kj > qi), jnp.logical_and(kj >= blk, kj - blk <= qi))

        def per_head(head, carry):
            acc = jnp.zeros((2 * blk, blk), F32)
            for b in range(REL_BUCKETS):
                acc = jnp.where(bucket == b, rel_ref[b, head] * LOG2E, acc)
            bias_ref[head] = jnp.where(band, acc, NEG)
            return carry
        lax.fori_loop(0, SWA_HEADS, per_head, 0)

    for kvh in range(SWA_KV_HEADS):
        cols = slice(kvh * LANES, (kvh + 1) * LANES)
        heads = range(kvh * SWA_GROUP, (kvh + 1) * SWA_GROUP)
        for g, head in enumerate(heads):
            q_pair = q_ref[:, head // 2 * LANES:(head // 2 + 1) * LANES]
            qs_ref[g * blk:(g + 1) * blk, :] = jnp.where((lane // HEAD_DIM) == head % 2, q_pair, jnp.zeros_like(q_pair))
        keys = jnp.concatenate([kp_ref[:, cols], kc_ref[:, cols]], axis=0)
        vals = jnp.concatenate([vp_ref[:, cols], vc_ref[:, cols]], axis=0)
        s = _dot_nt(keys, qs_ref[...]) + jnp.concatenate([bias_ref[head] for head in heads], axis=1)
        s_prev = jnp.where(n > 0, s[:blk], NEG)
        s_cur = s[blk:]
        sink = jnp.concatenate([jnp.full((1, blk), sink_ref[head] * LOG2E, F32) for head in heads], axis=1)
        m = jnp.maximum(jnp.maximum(s_prev, s_cur).max(axis=0, keepdims=True), sink)
        e = jnp.concatenate([jnp.exp2(s_prev - m).astype(BF16), jnp.exp2(s_cur - m).astype(BF16)], axis=0)
        v_t = jnp.concatenate([vals.astype(F32).T.astype(BF16)[:HEAD_DIM], jnp.ones((ONES_ROWS, 2 * blk), BF16)], axis=0)
        out = _dot(v_t, e)
        out = out[:HEAD_DIM] / (out[HEAD_DIM:HEAD_DIM + 1] + jnp.exp2(sink - m))
        for pair in range(SWA_GROUP // 2):
            head = kvh * SWA_GROUP + 2 * pair
            even = out[:, 2 * pair * blk:(2 * pair + 1) * blk]
            odd = out[:, (2 * pair + 1) * blk:(2 * pair + 2) * blk]
            cols = slice(head // 2 * LANES, (head // 2 + 1) * LANES)
            o_ref[:, cols] = (jnp.concatenate([even, odd], axis=0).T * sg_ref[:, cols].astype(F32)).astype(o_ref.dtype)


def _swa_attention(proj, sinks, rel_bias, bsz, seq):
    nb = seq // SWA_BLOCK
    width = SWA_HEADS * HEAD_DIM
    kvw = SWA_KV_HEADS * LANES
    k_blk = 2 * width // kvw
    cur = lambda b, n: b * nb + n
    prev = lambda b, n: b * nb + jnp.maximum(n - 1, 0)
    smem = pl.BlockSpec(memory_space=pltpu.SMEM)
    return pl.pallas_call(
        _swa_kernel,
        grid=(bsz, nb),
        in_specs=[
            smem, smem,
            pl.BlockSpec((2 * SWA_BLOCK, SWA_BLOCK), lambda b, n: (0, 0)),
            pl.BlockSpec((SWA_BLOCK, width), lambda b, n: (cur(b, n), 0)),
            pl.BlockSpec((SWA_BLOCK, width), lambda b, n: (cur(b, n), 1)),
            pl.BlockSpec((SWA_BLOCK, kvw), lambda b, n: (prev(b, n), k_blk)),
            pl.BlockSpec((SWA_BLOCK, kvw), lambda b, n: (cur(b, n), k_blk)),
            pl.BlockSpec((SWA_BLOCK, kvw), lambda b, n: (prev(b, n), k_blk + 1)),
            pl.BlockSpec((SWA_BLOCK, kvw), lambda b, n: (cur(b, n), k_blk + 1)),
        ],
        out_specs=pl.BlockSpec((SWA_BLOCK, width), lambda b, n: (cur(b, n), 0)),
        out_shape=jax.ShapeDtypeStruct((bsz * seq, width), BF16),
        scratch_shapes=[
            pltpu.VMEM((SWA_HEADS, 2 * SWA_BLOCK, SWA_BLOCK), F32),
            pltpu.VMEM((SWA_GROUP * SWA_BLOCK, LANES), BF16),
        ],
        compiler_params=_params(2),
        name="swa_attention",
    )(sinks.astype(F32), rel_bias.astype(F32), jnp.asarray(_swa_bucket_table()), proj, proj, proj, proj, proj, proj)


SWA_HALF_TILE = 512


def _swa_weight_tiles_kernel(wl_ref, wr_ref, o_ref, *, q_tiles, gate_tiles, q_scale):
    j = pl.program_id(0)
    half = SWA_HALF_TILE
    kvw = SWA_KV_HEADS * HEAD_DIM

    def rows_do(fn):
        def chunk(rows):
            for side, w_ref in enumerate((wl_ref, wr_ref)):
                o_ref[rows, side * half:(side + 1) * half] = fn(side, w_ref, rows)
        _for_row_chunks(o_ref.shape[0], chunk)

    @pl.when(j < q_tiles)
    def _():
        rows_do(lambda side, w_ref, rows: (w_ref[rows, :] * q_scale).astype(BF16))

    @pl.when(jnp.logical_and(j >= q_tiles, j < q_tiles + gate_tiles))
    def _():
        rows_do(lambda side, w_ref, rows: w_ref[rows, :].astype(BF16))

    @pl.when(j == q_tiles + gate_tiles)
    def _():
        def twice(side, w_ref, rows):
            w = w_ref[rows, side * kvw:(side + 1) * kvw].astype(BF16)
            heads = [w[:, h * HEAD_DIM:(h + 1) * HEAD_DIM] for h in range(SWA_KV_HEADS)]
            return jnp.concatenate([piece for head in heads for piece in (head, head)], axis=1)
        rows_do(twice)


def _swa_weight_tiles(w_in, layer, q_scale):
    _, k, n = w_in.shape
    half = SWA_HALF_TILE
    width = SWA_HEADS * HEAD_DIM
    q_tiles = gate_tiles = width // (2 * half)
    kv_block = width // half
    n_out = q_tiles + gate_tiles + 1

    def src(j, side):
        in_q, in_gate = j < q_tiles, j < q_tiles + gate_tiles
        return jnp.where(in_q, 2 * j + side, jnp.where(in_gate, 2 * j + side + 1, kv_block))
    return pl.pallas_call(
        functools.partial(_swa_weight_tiles_kernel, q_tiles=q_tiles, gate_tiles=gate_tiles, q_scale=q_scale),
        grid=(n_out,),
        in_specs=[pl.BlockSpec((None, k, half), lambda j: (layer, 0, src(j, 0))),
                  pl.BlockSpec((None, k, half), lambda j: (layer, 0, src(j, 1)))],
        out_specs=pl.BlockSpec((None, k, 2 * half), lambda j: (j, 0, 0)),
        out_shape=jax.ShapeDtypeStruct((n_out, k, 2 * half), BF16),
        compiler_params=_params(1),
        name="swa_weight_tiles",
    )(w_in, w_in)


def _swa_mixer(x, g, w_in, layer, sinks, rel_bias, bsz, seq):
    width = SWA_HEADS * HEAD_DIM
    w_tiles = _swa_weight_tiles(w_in, layer, LOG2E * HEAD_DIM ** -0.5)
    proj = _norm_matmul(x, g, w_tiles, BF16, silu_cols=(width, 2 * width))
    return _swa_attention(proj, sinks, rel_bias, bsz, seq)


HALO = 8


def _conv_kernel(x_ref, g_ref, wb_ref, wc_ref, wu_ref, wg_ref, ck_ref, o_ref, a_ref, z_ref, carry_ref, *,
                 tiles_per_seq, rows_per_step):
    i, j = pl.program_id(0), pl.program_id(1)
    tm = a_ref.shape[1]
    _lookahead_first(x_ref, g_ref, a_ref)
    a = a_ref[i % 2]
    z = _dot(a, wc_ref[...]) * _dot(a, wu_ref[...])
    first = (i % tiles_per_seq) == 0

    @pl.when(first)
    def _():
        z_ref[:HALO, :] = jnp.zeros((HALO, z_ref.shape[1]), F32)

    @pl.when(jnp.logical_not(first))
    def _():
        z_ref[:HALO, :] = carry_ref[j]

    z_ref[HALO:, :] = z
    carry_ref[j] = z[tm - HALO:, :]
    conv = z_ref[HALO - 2:HALO - 2 + tm, :] * ck_ref[0:1, :]
    conv = conv + z_ref[HALO - 1:HALO - 1 + tm, :] * ck_ref[1:2, :]
    conv = conv + z * ck_ref[2:3, :]
    y = _dot(a, wb_ref[...]) * conv
    o_ref[...] = (y * _silu(_dot(a, wg_ref[...]))).astype(o_ref.dtype)
    _lookahead_next(x_ref, g_ref, a_ref, rows_per_step)


def _conv_mixer(x, g, w_in, layer, conv_kernel, seq, tm=1024, tn=512):
    m, k = x.shape
    width = w_in.shape[2] // 4
    tm, tn = min(tm, seq), min(tn, width)
    nt = width // tn
    w_spec = lambda q: pl.BlockSpec((None, k, tn), lambda i, j: (q * nt + j, 0, 0))
    w_in = _tile_cols(w_in, layer, tn)
    return pl.pallas_call(
        functools.partial(_conv_kernel, tiles_per_seq=seq // tm, rows_per_step=_lookahead_rows(tm, nt)),
        grid=(m // tm, nt),
        in_specs=[
            _lookahead_x_spec(tm, k, m // tm),
            pl.BlockSpec((1, k), lambda i, j: (0, 0)),
            w_spec(0), w_spec(1), w_spec(2), w_spec(3),
            pl.BlockSpec((CONV_TAPS, tn), lambda i, j: (0, j)),
        ],
        out_specs=pl.BlockSpec((tm, tn), lambda i, j: (i, j)),
        out_shape=jax.ShapeDtypeStruct((m, width), BF16),
        scratch_shapes=[
            pltpu.VMEM((2, tm, k), BF16),
            pltpu.VMEM((HALO + tm, tn), F32),
            pltpu.VMEM((nt, HALO, tn), F32),
        ],
        compiler_params=_params(2, VMEM_LIMIT_BIG),
        name="conv_mixer",
    )(x, g.reshape(1, k), w_in, w_in, w_in, w_in, conv_kernel.astype(F32))


def _ssm_operators(lam_re, lam_im, log_dt, b_re, b_im, c_re, c_im, d_skip):
    n_groups = lam_re.shape[0]
    n_oct = n_groups // GROUPS_PER_TILE
    L, C, N = SSM_CHUNK, SSM_GROUP, SSM_STATE
    dt = jnp.exp(log_dt.astype(F32))[None, :]
    lr, li = lam_re.astype(F32).T, lam_im.astype(F32).T
    mag = jnp.exp(lr * dt)
    ab_re, ab_im = mag * jnp.cos(li * dt), mag * jnp.sin(li * dt)
    den = lr * lr + li * li
    nr = ab_re - 1.0
    coef_re = (nr * lr + ab_im * li) / den
    coef_im = (ab_im * lr - nr * li) / den
    br, bi = b_re.astype(F32).transpose(2, 1, 0), b_im.astype(F32).transpose(2, 1, 0)
    bb_re = coef_re * br - coef_im * bi
    bb_im = coef_re * bi + coef_im * br
    cr, ci = c_re.astype(F32).transpose(1, 2, 0), c_im.astype(F32).transpose(1, 2, 0)
    pw_re, pw_im = [jnp.ones_like(ab_re)], [jnp.zeros_like(ab_im)]
    for _ in range(L):
        pw_re, pw_im = (pw_re + [pw_re[-1] * ab_re - pw_im[-1] * ab_im], pw_im + [pw_re[-1] * ab_im + pw_im[-1] * ab_re])
    p_re, p_im = jnp.stack(pw_re), jnp.stack(pw_im)
    w_re = cr[None] * p_re[:L, None] - ci[None] * p_im[:L, None]
    w_im = cr[None] * p_im[:L, None] + ci[None] * p_re[:L, None]
    lag_k = jnp.sum(w_re[:, None] * bb_re[None, :, None] - w_im[:, None] * bb_im[None, :, None], axis=3)
    q_re, q_im = jnp.stack(pw_re[L - 1::-1]), jnp.stack(pw_im[L - 1::-1])
    bc_re = q_re[:, None] * bb_re[None] - q_im[:, None] * bb_im[None]
    bc_im = q_re[:, None] * bb_im[None] + q_im[:, None] * bb_re[None]
    o_re = cr[None] * p_re[1:, None] - ci[None] * p_im[1:, None]
    o_im = cr[None] * p_im[1:, None] + ci[None] * p_re[1:, None]
    octs = lambda t: t.reshape(t.shape[:-1] + (n_oct, GROUPS_PER_TILE))
    to_lag = lambda t: octs(t).transpose(3, 0, 1, 4, 2).reshape(n_oct, L, C, LANES)
    to_rows = lambda t: octs(t).transpose(3, 0, 1, 4, 2).reshape(n_oct, L * C, GROUPS_PER_TILE * N)
    to_cols = lambda t: octs(t).transpose(3, 4, 2, 0, 1).reshape(n_oct, GROUPS_PER_TILE * N, L * C)
    to_vec = lambda t: octs(t).transpose(1, 2, 0).reshape(n_oct, 1, GROUPS_PER_TILE * N)
    return dict(
        lag=to_lag(lag_k).astype(BF16),
        bc_re=to_rows(bc_re).astype(BF16), bc_im=to_rows(bc_im).astype(BF16),
        oc_re=to_cols(o_re).astype(BF16), oc_im_neg=to_cols(-o_im).astype(BF16),
        al_re=to_vec(p_re[L]), al_im=to_vec(p_im[L]), d=d_skip.astype(F32).reshape(n_oct, 1, LANES),
    )


def _ssm_expanders():
    L, C, N, G8 = SSM_CHUNK, SSM_GROUP, SSM_STATE, GROUPS_PER_TILE
    wide = np.arange(L * LANES)
    w_step, w_group, w_chan = wide // LANES, (wide // C) % G8, wide % C
    small = np.arange(L * C)
    s_step, s_chan = small // C, small % C
    state_group = np.arange(G8 * N) // N
    spread = (w_step[:, None] == s_step[None, :]) & (w_chan[:, None] == s_chan[None, :])
    lane = np.arange(LANES)
    as_bf16 = lambda a: jnp.asarray(a.astype(np.float32), dtype=BF16)
    return dict(
        spread_cols=as_bf16(spread.T),
        keep_cols=as_bf16(state_group[:, None] == w_group[None, :]),
        spread_lag=as_bf16((lane % C)[:, None] == np.arange(C)[None, :]),
        keep_lag=jnp.asarray(((lane // C)[:, None] == (lane // C)[None, :]).astype(np.float32)),
    )


def _ssm_kernel(u_ref, lag_ref, bxr_ref, bxi_ref, cxr_ref, cxi_ref, alr_ref, ali_ref, d_ref,
                sc_ref, kc_ref, sl_ref, kl_ref, o_ref,
                ub_ref, bcr_ref, bci_ref, ocr_ref, oci_ref, zr_ref, zi_ref, hr_ref, hi_ref, y_ref, *, bsz):
    L = SSM_CHUNK
    rows = u_ref.shape[0] // L
    chunks = rows // bsz
    tile = 2 * LANES
    step_rows = lambda r: pl.ds(r, rows, stride=L)
    for r in range(L):
        ub_ref[:, r * LANES:(r + 1) * LANES] = u_ref[step_rows(r), :].astype(BF16)
    state_group = lax.broadcasted_iota(jnp.int32, (1, bxr_ref.shape[1]), 1) // SSM_STATE
    for compact_ref, full_ref in ((bxr_ref, bcr_ref), (bxi_ref, bci_ref)):
        for r in range(L):
            blk = compact_ref[r * SSM_GROUP:(r + 1) * SSM_GROUP, :]
            for grp in range(GROUPS_PER_TILE):
                first = r * LANES + grp * SSM_GROUP
                full_ref[first:first + SSM_GROUP, :] = jnp.where(state_group == grp, blk, jnp.zeros_like(blk))
    zr_ref[...] = _dot(ub_ref[...], bcr_ref[...])
    zi_ref[...] = _dot(ub_ref[...], bci_ref[...])
    a_re, a_im = alr_ref[...], ali_ref[...]

    def step(k, carry):
        new = []
        for b in range(bsz):
            h_re, h_im = carry[2 * b], carry[2 * b + 1]
            row = pl.ds(b * chunks + k, 1)
            hr_ref[row, :] = h_re
            hi_ref[row, :] = h_im
            new.append(a_re * h_re - a_im * h_im + zr_ref[row, :])
            new.append(a_re * h_im + a_im * h_re + zi_ref[row, :])
        return tuple(new)

    zero = jnp.zeros((1, a_re.shape[1]), F32)
    lax.fori_loop(0, chunks, step, (zero,) * (2 * bsz), unroll=True)

    ocr_ref[...] = (_dot(cxr_ref[...], sc_ref[...]) * kc_ref[...]).astype(BF16)
    oci_ref[...] = (_dot(cxi_ref[...], sc_ref[...]) * kc_ref[...]).astype(BF16)
    lag = [(_dot(sl_ref[...], lag_ref[d]) * kl_ref[...]).astype(BF16) for d in range(L)]
    lag_tile = []
    for dd in range(L // 2):
        below = lag[2 * dd - 1] if dd > 0 else jnp.zeros((LANES, LANES), BF16)
        lag_tile.append(jnp.concatenate([jnp.concatenate([lag[2 * dd], lag[2 * dd + 1]], axis=1),
                                         jnp.concatenate([below, lag[2 * dd]], axis=1)], axis=0))
    hb_re, hb_im = hr_ref[...].astype(BF16), hi_ref[...].astype(BF16)
    for t2 in range(L // 2):
        cols = slice(t2 * tile, (t2 + 1) * tile)
        y = _dot(hb_re, ocr_ref[:, cols]) + _dot(hb_im, oci_ref[:, cols])
        for r2 in range(t2 + 1):
            y = y + _dot(ub_ref[:, r2 * tile:(r2 + 1) * tile], lag_tile[t2 - r2])
        for half in range(2):
            t = 2 * t2 + half
            y_ref[step_rows(t), :] = jax.nn.gelu(y[:, half * LANES:(half + 1) * LANES] + d_ref[...] * u_ref[step_rows(t), :])

    def chunk(rows):
        o_ref[rows, :] = y_ref[rows, :].astype(o_ref.dtype)
    _for_row_chunks(o_ref.shape[0], chunk)


def _ssm_core(proj, ops, bsz, seq):
    m = proj.shape[0]
    width = proj.shape[1] // 2
    L, C = SSM_CHUNK, SSM_GROUP
    n_oct = width // LANES
    rows = m // L
    ow = L * LANES
    n_state = GROUPS_PER_TILE * SSM_STATE
    ex = _ssm_expanders()
    per_oct = lambda *shape: pl.BlockSpec((None,) + shape, lambda p: (p,) + (0,) * len(shape))
    const = lambda a: pl.BlockSpec(a.shape, lambda p: (0,) * a.ndim, pipeline_mode=pl.Buffered(1))
    consts = [ex['spread_cols'], ex['keep_cols'], ex['spread_lag'], ex['keep_lag']]
    return pl.pallas_call(
        functools.partial(_ssm_kernel, bsz=bsz),
        grid=(n_oct,),
        in_specs=[
            pl.BlockSpec((m, LANES), lambda p: (0, p)),
            per_oct(L, C, LANES), per_oct(L * C, n_state), per_oct(L * C, n_state),
            per_oct(n_state, L * C), per_oct(n_state, L * C), per_oct(1, n_state), per_oct(1, n_state), per_oct(1, LANES),
        ] + [const(a) for a in consts],
        out_specs=pl.BlockSpec((m, LANES), lambda p: (0, p)),
        out_shape=jax.ShapeDtypeStruct((m, width), BF16),
        scratch_shapes=[pltpu.VMEM((rows, ow), BF16)] + [pltpu.VMEM((ow, n_state), BF16)] * 2
        + [pltpu.VMEM((n_state, ow), BF16)] * 2 + [pltpu.VMEM((rows, n_state), F32)] * 4 + [pltpu.VMEM((m, LANES), F32)],
        compiler_params=_params(1),
        name="ssm_core",
    )(proj, ops['lag'], ops['bc_re'], ops['bc_im'], ops['oc_re'], ops['oc_im_neg'], ops['al_re'], ops['al_im'], ops['d'], *consts)


def _glu_kernel(y_ref, wa_ref, wb_ref, ba_ref, bb_ref, gate_ref, o_ref):
    y = y_ref[...]
    ga = _dot(y, wa_ref[...]) + ba_ref[...]
    gb = _dot(y, wb_ref[...]) + bb_ref[...]
    o_ref[...] = ((ga * jax.nn.sigmoid(gb)) * _silu(gate_ref[...])).astype(o_ref.dtype)


def _glu(y, w_glu, layer, b_glu, proj, tm=1024, tn=512):
    m, k = y.shape
    width = w_glu.shape[2] // 2
    tm, tn = min(tm, m), min(tn, width)
    nt = width // tn
    w_tiles = _tile_cols(w_glu, layer, tn)
    return pl.pallas_call(
        _glu_kernel,
        grid=(m // tm, nt),
        in_specs=[
            pl.BlockSpec((tm, k), lambda i, j: (i, 0)),
            pl.BlockSpec((None, k, tn), lambda i, j: (j, 0, 0)),
            pl.BlockSpec((None, k, tn), lambda i, j: (nt + j, 0, 0)),
            pl.BlockSpec((1, tn), lambda i, j: (0, j)),
            pl.BlockSpec((1, tn), lambda i, j: (0, nt + j)),
            pl.BlockSpec((tm, tn), lambda i, j: (i, nt + j)),
        ],
        out_specs=pl.BlockSpec((tm, tn), lambda i, j: (i, j)),
        out_shape=jax.ShapeDtypeStruct((m, width), BF16),
        compiler_params=_params(2),
        name="ssm_glu",
    )(y, w_tiles, w_tiles, b_glu, b_glu, proj)


def _ssm_mixer(x, g, w_in, layer, lam_re, lam_im, log_dt, b_re, b_im, c_re, c_im, d_skip, w_glu, b_glu, bsz, seq):
    proj = _norm_matmul(x, g, _tile_cols(w_in, layer, MATMUL_TN), F32)
    ops = _ssm_operators(lam_re, lam_im, log_dt, b_re, b_im, c_re, c_im, d_skip)
    y = _ssm_core(proj, ops, bsz, seq)
    return _glu(y, w_glu, layer, b_glu.astype(F32).reshape(1, -1), proj)


CUM_BLOCK = 128


def _split3(x):
    x1 = x.astype(BF16)
    r1 = x - x1.astype(F32)
    x2 = r1.astype(BF16)
    x3 = (r1 - x2.astype(F32)).astype(BF16)
    return x1, x2, x3


def _forget_cumsum_kernel(z_ref, b_ref, o_ref):
    n_blocks = z_ref.shape[0] // CUM_BLOCK
    ri = lax.broadcasted_iota(jnp.int32, (CUM_BLOCK, CUM_BLOCK), 0)
    ci = lax.broadcasted_iota(jnp.int32, (CUM_BLOCK, CUM_BLOCK), 1)
    tri = jnp.where(ci <= ri, 1.0, 0.0).astype(BF16)

    def body(i, carry):
        rows = pl.ds(pl.multiple_of(i * CUM_BLOCK, CUM_BLOCK), CUM_BLOCK)
        z = z_ref[rows, :] + b_ref[...]
        log_f = jnp.minimum(z, 0.0) - jnp.log1p(jnp.exp(-jnp.abs(z)))
        x1, x2, x3 = _split3(log_f)
        c = _dot(tri, x1) + _dot(tri, x2) + _dot(tri, x3) + carry
        o_ref[rows, :] = c
        return c[CUM_BLOCK - 1:, :]

    lax.fori_loop(0, n_blocks, body, jnp.zeros((1, z_ref.shape[1]), F32))


def _forget_cumsum(z, b, bsz, seq):
    lanes = z.shape[1]
    return pl.pallas_call(
        _forget_cumsum_kernel,
        grid=(bsz,),
        in_specs=[pl.BlockSpec((seq, lanes), lambda i: (i, 0)), pl.BlockSpec((1, lanes), lambda i: (0, 0))],
        out_specs=pl.BlockSpec((seq, lanes), lambda i: (i, 0)),
        out_shape=jax.ShapeDtypeStruct(z.shape, F32),
        compiler_params=_params(1),
        name="forget_cumsum",
    )(z, b)


FOX_STEP_HEADS = 8


def _fox_kernel(q_ref, k_ref, v_ref, sg_ref, cq_ref, ck_ref, o_ref, ka_ref, vt_ref, *head_refs, tk):
    nh = FOX_STEP_HEADS
    hg, qi = pl.program_id(1), pl.program_id(2)
    tq = q_ref.shape[0]
    qa_ref, s_ref, p_ref, acc_ref, pv_ref, m_ref = (head_refs[i::6] for i in range(6))
    lane = lax.broadcasted_iota(jnp.int32, (1, LANES), 1)
    free = (HEAD_DIM, 0)
    n_pieces = 3
    pair_lanes = lambda h: slice(h // 2 * LANES, (h // 2 + 1) * LANES)

    def bias_lanes(c, h, c_offset, one_offset):
        rr = lax.broadcasted_iota(jnp.int32, (n_pieces * LANES, LANES), 0)
        cc = lax.broadcasted_iota(jnp.int32, (n_pieces * LANES, LANES), 1)
        place = jnp.logical_and(rr % LANES == nh * hg + h, cc == free[h % 2] + c_offset + rr // LANES)
        placed = _dot(jnp.concatenate(_split3(c), axis=1), jnp.where(place, 1.0, 0.0).astype(BF16))
        first = free[h % 2] + one_offset
        return (placed + jnp.where(jnp.logical_and(lane >= first, lane < first + n_pieces), 1.0, 0.0)).astype(BF16)

    @pl.when(qi == 0)
    def _():
        def chunk(j, carry):
            rows = pl.ds(pl.multiple_of(j * tk, tk), tk)
            neg_c = ck_ref[rows, :] * -LOG2E
            for h in range(nh):
                keys = k_ref[rows, pair_lanes(h)]
                ka_ref[h, j] = jnp.where((lane // HEAD_DIM) == h % 2, keys, bias_lanes(neg_c, h, 0, n_pieces))
            for pair in range(nh // 2):
                v_t = v_ref[rows, pair_lanes(2 * pair)].astype(F32).T.astype(BF16)
                for hh in range(2):
                    vt_ref[j, 2 * pair + hh, :HEAD_DIM, :] = v_t[hh * HEAD_DIM:(hh + 1) * HEAD_DIM]
                    vt_ref[j, 2 * pair + hh, HEAD_DIM:, :] = jnp.ones((ONES_ROWS, tk), BF16)
            return carry
        lax.fori_loop(0, k_ref.shape[0] // tk, chunk, 0)

    cq = cq_ref[...] * LOG2E
    for h in range(nh):
        qa = jnp.where((lane // HEAD_DIM) == h % 2, q_ref[:, pair_lanes(h)], bias_lanes(cq, h, n_pieces, 0))
        qa_ref[h][...] = qa.astype(F32).T.astype(BF16)

    def scores(h, j):
        s_ref[h][...] = _dot(ka_ref[h, j], qa_ref[h][...])

    rows8 = 8
    rows16 = 16

    def absorb(h, j, diagonal):
        def strip(r, n):
            blk = s_ref[h][r:r + n, :]
            if diagonal:
                kpos = lax.broadcasted_iota(jnp.int32, (n, tq), 0)
                qpos = lax.broadcasted_iota(jnp.int32, (n, tq), 1)
                blk = jnp.where(kpos + r <= qpos, blk, NEG)
            return blk
        top = strip(0, rows8)
        for r in range(rows8, tk, rows8):
            top = jnp.maximum(top, strip(r, rows8))
        m = m_ref[h][...]
        m_new = jnp.maximum(m, top.max(axis=0, keepdims=True))
        m_ref[h][...] = m_new
        alpha = jnp.exp2(m - m_new)
        m_rows = jnp.broadcast_to(m_new, (rows16, tq))
        for r in range(0, tk, rows16):
            p_ref[h][r:r + rows16, :] = jnp.exp2(strip(r, rows16) - m_rows).astype(BF16)
        acc_ref[h][...] = alpha * (acc_ref[h][...] + pv_ref[h][...])

    def values(h, j):
        pv_ref[h][...] = _dot(vt_ref[j, h], p_ref[h][...])

    def step(j, diagonal):
        values(nh - 1, jnp.maximum(j - 1, 0))
        scores(1, j)
        for h in range(nh):
            absorb(h, j, diagonal)
            if h < nh - 1:
                values(h, j)
            if h + 2 < nh:
                scores(h + 2, j)
            elif h + 2 == nh and not diagonal:
                scores(0, j + 1)

    for h in range(nh):
        acc_ref[h][...] = jnp.zeros(acc_ref[h].shape, F32)
        pv_ref[h][...] = jnp.zeros(pv_ref[h].shape, F32)
        m_ref[h][...] = jnp.full(m_ref[h].shape, NEG, F32)
    p_ref[nh - 1][...] = jnp.zeros(p_ref[nh - 1].shape, BF16)
    scores(0, 0)

    def two_blocks(i, carry):
        step(2 * i, False)
        step(2 * i + 1, False)
        return carry
    lax.fori_loop(0, qi // 2, two_blocks, 0)

    @pl.when(qi % 2 == 0)
    def _():
        step(qi, True)

    @pl.when(qi % 2 == 1)
    def _():
        step(qi - 1, False)
        step(qi, True)

    values(nh - 1, qi)
    outs = []
    for h in range(nh):
        total = acc_ref[h][...] + pv_ref[h][...]
        outs.append(total[:HEAD_DIM] / total[HEAD_DIM:HEAD_DIM + 1])
    out = jnp.concatenate(outs, axis=0)
    o_ref[...] = (out.T * sg_ref[...].astype(F32)).astype(o_ref.dtype)


def _fox_attention(proj, csum, bsz, seq, t=512):
    width = FOX_HEADS * HEAD_DIM
    nh = FOX_STEP_HEADS
    gw = nh * HEAD_DIM
    n_groups = width // gw
    t = min(t, seq)
    nq = seq // t
    return pl.pallas_call(
        functools.partial(_fox_kernel, tk=t),
        grid=(bsz, n_groups, nq),
        in_specs=[
            pl.BlockSpec((t, gw), lambda b, h, i: (b * nq + i, h)),
            pl.BlockSpec((seq, gw), lambda b, h, i: (b, n_groups + h), pipeline_mode=pl.Buffered(1)),
            pl.BlockSpec((seq, gw), lambda b, h, i: (b, 2 * n_groups + h), pipeline_mode=pl.Buffered(1)),
            pl.BlockSpec((t, gw), lambda b, h, i: (b * nq + i, 3 * n_groups + h)),
            pl.BlockSpec((t, LANES), lambda b, h, i: (b * nq + i, 0)),
            pl.BlockSpec((seq, LANES), lambda b, h, i: (b, 0), pipeline_mode=pl.Buffered(1)),
        ],
        out_specs=pl.BlockSpec((t, gw), lambda b, h, i: (b * nq + i, h)),
        out_shape=jax.ShapeDtypeStruct((bsz * seq, width), BF16),
        scratch_shapes=[
            pltpu.VMEM((nh, nq, t, LANES), BF16), pltpu.VMEM((nq, nh, HEAD_DIM + ONES_ROWS, t), BF16),
        ] + nh * [pltpu.VMEM((LANES, t), BF16), pltpu.VMEM((t, t), F32), pltpu.VMEM((t, t), BF16),
                  pltpu.VMEM((HEAD_DIM + ONES_ROWS, t), F32), pltpu.VMEM((HEAD_DIM + ONES_ROWS, t), F32),
                  pltpu.VMEM((1, t), F32)],
        compiler_params=_params(3, VMEM_LIMIT_BIG),
        name="fox_attention",
    )(proj, proj, proj, proj, csum, csum)


def _fox_mixer(x, g, w_in, layer, w_fg, b_fg, bsz, seq, t=512):
    width = FOX_HEADS * HEAD_DIM
    col_scale = jnp.where(jnp.arange(4 * width) < width, LOG2E * HEAD_DIM ** -0.5, 1.0).astype(F32)
    proj = _norm_matmul(x, g, _tile_cols(w_in, layer, MATMUL_TN, col_scale), BF16, silu_cols=(3 * width, 4 * width))
    pad = LANES - FOX_HEADS
    z = _norm_matmul(x, g, _tile_cols_xla(jnp.pad(w_fg, ((0, 0), (0, pad))), LANES), F32)
    csum = _forget_cumsum(z, jnp.pad(b_fg.astype(F32), (0, pad)).reshape(1, LANES), bsz, seq)
    return _fox_attention(proj, csum, bsz, seq, t)


def kernel(x, p, norm_g, final_g, rel_bias, swa_w_in, swa_w_out, swa_sinks, conv_w_in, conv_kernel, conv_w_out, ssm_w_in, ssm_lam_re, ssm_lam_im, ssm_log_dt, ssm_b_re, ssm_b_im, ssm_c_re, ssm_c_im, ssm_d, ssm_w_glu, ssm_b_glu, ssm_w_out, fox_w_in, fox_w_fg, fox_b_fg, fox_w_out, ple_proj, ple_norm, ple_gate):
    bsz, seq, d_model = x.shape
    depth = p.shape[0]
    h = x.astype(F32).reshape(bsz * seq, d_model)
    for i in range(depth):
        mixer, j = i % N_MIXERS, i // N_MIXERS
        if mixer == 0:
            a, w_out = _swa_mixer(h, norm_g[i], swa_w_in, j, swa_sinks[j], rel_bias, bsz, seq), swa_w_out[j]
        elif mixer == 1:
            a, w_out = _conv_mixer(h, norm_g[i], conv_w_in, j, conv_kernel[j], seq), conv_w_out[j]
        elif mixer == 2:
            a = _ssm_mixer(h, norm_g[i], ssm_w_in, j, ssm_lam_re[j], ssm_lam_im[j], ssm_log_dt[j], ssm_b_re[j], ssm_b_im[j],
                           ssm_c_re[j], ssm_c_im[j], ssm_d[j], ssm_w_glu, ssm_b_glu[j], bsz, seq)
            w_out = ssm_w_out[j]
        else:
            a, w_out = _fox_mixer(h, norm_g[i], fox_w_in, j, fox_w_fg[j], fox_b_fg[j], bsz, seq), fox_w_out[j]
        h = _out_ple(a, w_out, h, ple_norm[i], ple_gate, p.reshape(depth, bsz * seq, -1), ple_proj, i,
                     final_g=final_g if i == depth - 1 else None)
    return h.reshape(bsz, seq, d_model).astype(x.dtype)
```

```python
import functools
import math

import numpy as np
import jax
import jax.numpy as jnp
from jax import lax
from jax.experimental import pallas as pl
from jax.experimental.pallas import tpu as pltpu

F32 = jnp.float32
BF16 = jnp.bfloat16

EPS = 1e-6
N_MIXERS = 4

SWA_HEADS = 32
SWA_KV_HEADS = 4
SWA_GROUP = SWA_HEADS // SWA_KV_HEADS
HEAD_DIM = 64
SWA_BLOCK = 128
WINDOW = 128
REL_BUCKETS = 32
REL_MAX_DIST = 128

CONV_TAPS = 3

SSM_GROUP = 16
SSM_STATE = 64
SSM_CHUNK = 16

FOX_HEADS = 32

LANES = 128
ONES_ROWS = 16
GROUPS_PER_TILE = LANES // SSM_GROUP
VMEM_LIMIT = 48 * 1024 * 1024
VMEM_LIMIT_BIG = 60 * 1024 * 1024

NEG = float(jnp.finfo(jnp.float32).min)
LOG2E = math.log2(math.e)


def _params(n_axes, vmem_limit=VMEM_LIMIT):
    return pltpu.CompilerParams(dimension_semantics=("arbitrary",) * n_axes, vmem_limit_bytes=vmem_limit)


def _dot(a, b):
    return jnp.dot(a, b, preferred_element_type=F32)


def _dot_nt(a, b):
    return lax.dot_general(a, b, (((1,), (1,)), ((), ())), preferred_element_type=F32)


def _rmsnorm_rows(x, g):
    return x * lax.rsqrt(jnp.mean(x * x, axis=-1, keepdims=True) + EPS) * g


def _silu(x):
    return x * jax.nn.sigmoid(x)


ROW_CHUNK = 64


def _for_row_chunks(n_rows, fn):
    def body(c, carry):
        fn(pl.ds(pl.multiple_of(c * ROW_CHUNK, ROW_CHUNK), ROW_CHUNK))
        return carry
    lax.fori_loop(0, n_rows // ROW_CHUNK, body, 0)


def _tile_cols_xla(w, tn):
    k, n = w.shape
    return w.astype(BF16).reshape(k, n // tn, tn).transpose(1, 0, 2)


def _cast_tiles_kernel(w_ref, o_ref):
    def chunk(rows):
        o_ref[rows, :] = w_ref[rows, :].astype(BF16)
    _for_row_chunks(o_ref.shape[0], chunk)


def _scale_cast_tiles_kernel(w_ref, s_ref, o_ref):
    def chunk(rows):
        o_ref[rows, :] = (w_ref[rows, :] * s_ref[...]).astype(BF16)
    _for_row_chunks(o_ref.shape[0], chunk)


def _tile_cols(w, layer, tn, col_scale=None):
    _, k, n = w.shape
    tn = min(tn, n)
    w_spec = pl.BlockSpec((None, k, tn), lambda j: (layer, 0, j))
    common = dict(
        grid=(n // tn,),
        out_specs=pl.BlockSpec((None, k, tn), lambda j: (j, 0, 0)),
        out_shape=jax.ShapeDtypeStruct((n // tn, k, tn), BF16),
        compiler_params=_params(1),
        name="weight_tiles",
    )
    if col_scale is None:
        return pl.pallas_call(_cast_tiles_kernel, in_specs=[w_spec], **common)(w)
    s_spec = pl.BlockSpec((1, tn), lambda j: (0, j))
    return pl.pallas_call(_scale_cast_tiles_kernel, in_specs=[w_spec, s_spec], **common)(w, col_scale.reshape(1, n))


def _lookahead_x_spec(tm, k, n_row_tiles):
    def tile(i, j):
        first_step = jnp.logical_and(i == 0, j == 0)
        return jnp.where(first_step, 0, jnp.minimum(i + 1, n_row_tiles - 1))
    return pl.BlockSpec((tm, k), lambda i, j: (tile(i, j), 0))


def _lookahead_rows(tm, nt):
    return -(-tm // ((nt - 1) * ROW_CHUNK)) * ROW_CHUNK


def _lookahead_first(x_ref, g_ref, a_ref):
    @pl.when(jnp.logical_and(pl.program_id(0) == 0, pl.program_id(1) == 0))
    def _():
        def chunk(rows):
            a_ref[0, rows, :] = _rmsnorm_rows(x_ref[rows, :], g_ref[...]).astype(BF16)
        _for_row_chunks(a_ref.shape[1], chunk)


def _lookahead_next(x_ref, g_ref, a_ref, rows_per_step):
    i, j = pl.program_id(0), pl.program_id(1)
    tm = a_ref.shape[1]
    start = jnp.clip((j - 1) * rows_per_step, 0, tm - rows_per_step)
    for c in range(rows_per_step // ROW_CHUNK):
        rows = pl.ds(pl.multiple_of(start + c * ROW_CHUNK, ROW_CHUNK), ROW_CHUNK)
        a_ref[(i + 1) % 2, rows, :] = _rmsnorm_rows(x_ref[rows, :], g_ref[...]).astype(BF16)


def _norm_matmul_kernel(x_ref, g_ref, w_ref, o_ref, a_ref, *, silu_tiles, rows_per_step):
    i, j = pl.program_id(0), pl.program_id(1)
    if rows_per_step is None:
        def chunk(rows):
            a_ref[0, rows, :] = _rmsnorm_rows(x_ref[rows, :], g_ref[...]).astype(BF16)
        _for_row_chunks(a_ref.shape[1], chunk)
        acc = _dot(a_ref[0], w_ref[...])
    else:
        _lookahead_first(x_ref, g_ref, a_ref)
        acc = _dot(a_ref[i % 2], w_ref[...])
        _lookahead_next(x_ref, g_ref, a_ref, rows_per_step)
    if silu_tiles is None:
        o_ref[...] = acc.astype(o_ref.dtype)
    else:
        gated = jnp.logical_and(j >= silu_tiles[0], j < silu_tiles[1])

        @pl.when(gated)
        def _():
            o_ref[...] = _silu(acc).astype(o_ref.dtype)

        @pl.when(jnp.logical_not(gated))
        def _():
            o_ref[...] = acc.astype(o_ref.dtype)


MATMUL_TN = 1024


def _norm_matmul(x, g, w_tiles, out_dtype, silu_cols=None, tm=1024):
    m, k = x.shape
    nt, _, tn = w_tiles.shape
    n = nt * tn
    tm = min(tm, m)
    n_row_tiles = m // tm
    silu_tiles = None if silu_cols is None else (silu_cols[0] // tn, silu_cols[1] // tn)
    lookahead = nt > 1
    return pl.pallas_call(
        functools.partial(_norm_matmul_kernel, silu_tiles=silu_tiles,
                          rows_per_step=_lookahead_rows(tm, nt) if lookahead else None),
        grid=(n_row_tiles, nt),
        in_specs=[
            _lookahead_x_spec(tm, k, n_row_tiles) if lookahead else pl.BlockSpec((tm, k), lambda i, j: (i, 0)),
            pl.BlockSpec((1, k), lambda i, j: (0, 0)),
            pl.BlockSpec((None, k, tn), lambda i, j: (j, 0, 0)),
        ],
        out_specs=pl.BlockSpec((tm, tn), lambda i, j: (i, j)),
        out_shape=jax.ShapeDtypeStruct((m, n), out_dtype),
        scratch_shapes=[pltpu.VMEM((2, tm, k), BF16)],
        compiler_params=_params(2, VMEM_LIMIT_BIG),
        name="norm_matmul",
    )(x, g.reshape(1, k), w_tiles)


def _out_ple_kernel(a_ref, wo_ref, x_ref, g_ref, wg_ref, p_ref, wp_ref, gf_ref, o_ref, x1_ref, hn_ref, ss_ref, *, nt, final):
    j = pl.program_id(1)
    tn = x_ref.shape[1]
    n = nt * tn

    @pl.when(j == 0)
    def _():
        ss_ref[...] = jnp.zeros(ss_ref.shape, F32)

    @pl.when(j < nt)
    def _():
        x1 = x_ref[...] + _dot(a_ref[...], wo_ref[j])
        x1_ref[j] = x1
        hn_ref[j] = (x1 * g_ref[j]).astype(BF16)
        ss_ref[0] += jnp.sum(x1 * x1, axis=-1, keepdims=True)

    @pl.when(j >= nt)
    def _():
        t = j - nt
        emb = _dot(p_ref[...].astype(BF16), wp_ref[t])
        acc = _dot(hn_ref[0], wg_ref[t, :tn, :])
        for kt in range(1, nt):
            acc = acc + _dot(hn_ref[kt], wg_ref[t, kt * tn:(kt + 1) * tn, :])
        x2 = x1_ref[t] + emb * jax.nn.sigmoid(acc * lax.rsqrt(ss_ref[0] / n + EPS))
        if final:
            x1_ref[t] = x2
            ss_ref[1] += jnp.sum(x2 * x2, axis=-1, keepdims=True)
        else:
            o_ref[...] = x2

    if final:
        @pl.when(j == 2 * nt - 1)
        def _():
            inv = lax.rsqrt(ss_ref[1] / n + EPS)
            for t in range(nt):
                o_ref[:, t * tn:(t + 1) * tn] = x1_ref[t] * inv * gf_ref[t]


def _out_ple(a, w_out, x, g, w_gate, p, w_proj, layer, final_g=None, tm=1024, tn=512):
    m, k = a.shape
    n = w_out.shape[1]
    pd = p.shape[2]
    final = final_g is not None
    tm, tn = min(tm, m), min(tn, n)
    nt = n // tn
    if final:
        out_spec = pl.BlockSpec((tm, n), lambda i, j: (i, 0), pipeline_mode=pl.Buffered(1))
    else:
        out_spec = pl.BlockSpec((tm, tn), lambda i, j: (i, jnp.maximum(j - nt, 0)))
    resident = lambda rows: pl.BlockSpec((nt, rows, tn), lambda i, j: (0, 0, 0), pipeline_mode=pl.Buffered(1))
    row_vec = pl.BlockSpec((nt, 1, tn), lambda i, j: (0, 0, 0))
    return pl.pallas_call(
        functools.partial(_out_ple_kernel, nt=nt, final=final),
        grid=(m // tm, 2 * nt),
        in_specs=[
            pl.BlockSpec((tm, k), lambda i, j: (i, 0)),
            resident(k),
            pl.BlockSpec((tm, tn), lambda i, j: (i, jnp.minimum(j, nt - 1))),
            row_vec,
            resident(n),
            pl.BlockSpec((None, tm, pd), lambda i, j: (layer, i, 0)),
            resident(pd),
            row_vec,
        ],
        out_specs=out_spec,
        out_shape=jax.ShapeDtypeStruct((m, n), F32),
        scratch_shapes=[pltpu.VMEM((nt, tm, tn), F32), pltpu.VMEM((nt, tm, tn), BF16), pltpu.VMEM((2, tm, 1), F32)],
        compiler_params=_params(2, VMEM_LIMIT_BIG),
        name="out_ple_final" if final else "out_ple",
    )(a, _tile_cols(w_out[None], 0, tn), x, g.reshape(nt, 1, tn), _tile_cols(w_gate, layer, tn), p,
      _tile_cols(w_proj, layer, tn), (final_g if final else g).reshape(nt, 1, tn))


def _t5_bucket(dist):
    max_exact = REL_BUCKETS // 2
    d = np.maximum(dist, 1).astype(np.float32)
    large = max_exact + (np.log(d / max_exact) / np.log(REL_MAX_DIST / max_exact) * (REL_BUCKETS - max_exact)).astype(np.int32)
    large = np.minimum(large, REL_BUCKETS - 1)
    return np.where(dist < max_exact, dist, large).astype(np.int32)


assert WINDOW == SWA_BLOCK


def _swa_bucket_table():
    qi = np.arange(SWA_BLOCK)[None, :]
    kj = np.arange(2 * SWA_BLOCK)[:, None]
    return _t5_bucket(np.clip(qi + SWA_BLOCK - kj, 0, None))


def _swa_kernel(sink_ref, rel_ref, bucket_ref, q_ref, sg_ref, kp_ref, kc_ref, vp_ref, vc_ref, o_ref, bias_ref, qs_ref):
    n = pl.program_id(1)
    blk = SWA_BLOCK
    lane = lax.broadcasted_iota(jnp.int32, (1, LANES), 1)

    @pl.when(jnp.logical_and(pl.program_id(0) == 0, n == 0))
    def _():
        bucket = bucket_ref[...]
        kj = lax.broadcasted_iota(jnp.int32, (2 * blk, blk), 0)
        qi = lax.broadcasted_iota(jnp.int32, (2 * blk, blk), 1)
        band = jnp.logical_or(jnp.logical_and(kj < blk, kj > qi), jnp.logical_and(kj >= blk, kj - blk <= qi))

        def per_head(head, carry):
            acc = jnp.zeros((2 * blk, blk), F32)
            for b in range(REL_BUCKETS):
                acc = jnp.where(bucket == b, rel_ref[b, head] * LOG2E, acc)
            bias_ref[head] = jnp.where(band, acc, NEG)
            return carry
        lax.fori_loop(0, SWA_HEADS, per_head, 0)

    for kvh in range(SWA_KV_HEADS):
        cols = slice(kvh * LANES, (kvh + 1) * LANES)
        heads = range(kvh * SWA_GROUP, (kvh + 1) * SWA_GROUP)
        for g, head in enumerate(heads):
            q_pair = q_ref[:, head // 2 * LANES:(head // 2 + 1) * LANES]
            qs_ref[g * blk:(g + 1) * blk, :] = jnp.where((lane // HEAD_DIM) == head % 2, q_pair, jnp.zeros_like(q_pair))
        keys = jnp.concatenate([kp_ref[:, cols], kc_ref[:, cols]], axis=0)
        vals = jnp.concatenate([vp_ref[:, cols], vc_ref[:, cols]], axis=0)
        s = _dot_nt(keys, qs_ref[...]) + jnp.concatenate([bias_ref[head] for head in heads], axis=1)
        s_prev = jnp.where(n > 0, s[:blk], NEG)
        s_cur = s[blk:]
        sink = jnp.concatenate([jnp.full((1, blk), sink_ref[head] * LOG2E, F32) for head in heads], axis=1)
        m = jnp.maximum(jnp.maximum(s_prev, s_cur).max(axis=0, keepdims=True), sink)
        e = jnp.concatenate([jnp.exp2(s_prev - m).astype(BF16), jnp.exp2(s_cur - m).astype(BF16)], axis=0)
        v_t = jnp.concatenate([vals.astype(F32).T.astype(BF16)[:HEAD_DIM], jnp.ones((ONES_ROWS, 2 * blk), BF16)], axis=0)
        out = _dot(v_t, e)
        out = out[:HEAD_DIM] / (out[HEAD_DIM:HEAD_DIM + 1] + jnp.exp2(sink - m))
        for pair in range(SWA_GROUP // 2):
            head = kvh * SWA_GROUP + 2 * pair
            even = out[:, 2 * pair * blk:(2 * pair + 1) * blk]
            odd = out[:, (2 * pair + 1) * blk:(2 * pair + 2) * blk]
            cols = slice(head // 2 * LANES, (head // 2 + 1) * LANES)
            o_ref[:, cols] = (jnp.concatenate([even, odd], axis=0).T * sg_ref[:, cols].astype(F32)).astype(o_ref.dtype)


def _swa_attention(proj, sinks, rel_bias, bsz, seq):
    nb = seq // SWA_BLOCK
    width = SWA_HEADS * HEAD_DIM
    kvw = SWA_KV_HEADS * LANES
    k_blk = 2 * width // kvw
    cur = lambda b, n: b * nb + n
    prev = lambda b, n: b * nb + jnp.maximum(n - 1, 0)
    smem = pl.BlockSpec(memory_space=pltpu.SMEM)
    return pl.pallas_call(
        _swa_kernel,
        grid=(bsz, nb),
        in_specs=[
            smem, smem,
            pl.BlockSpec((2 * SWA_BLOCK, SWA_BLOCK), lambda b, n: (0, 0)),
            pl.BlockSpec((SWA_BLOCK, width), lambda b, n: (cur(b, n), 0)),
            pl.BlockSpec((SWA_BLOCK, width), lambda b, n: (cur(b, n), 1)),
            pl.BlockSpec((SWA_BLOCK, kvw), lambda b, n: (prev(b, n), k_blk)),
            pl.BlockSpec((SWA_BLOCK, kvw), lambda b, n: (cur(b, n), k_blk)),
            pl.BlockSpec((SWA_BLOCK, kvw), lambda b, n: (prev(b, n), k_blk + 1)),
            pl.BlockSpec((SWA_BLOCK, kvw), lambda b, n: (cur(b, n), k_blk + 1)),
        ],
        out_specs=pl.BlockSpec((SWA_BLOCK, width), lambda b, n: (cur(b, n), 0)),
        out_shape=jax.ShapeDtypeStruct((bsz * seq, width), BF16),
        scratch_shapes=[
            pltpu.VMEM((SWA_HEADS, 2 * SWA_BLOCK, SWA_BLOCK), F32),
            pltpu.VMEM((SWA_GROUP * SWA_BLOCK, LANES), BF16),
        ],
        compiler_params=_params(2),
        name="swa_attention",
    )(sinks.astype(F32), rel_bias.astype(F32), jnp.asarray(_swa_bucket_table()), proj, proj, proj, proj, proj, proj)


SWA_HALF_TILE = 512


def _swa_weight_tiles_kernel(wl_ref, wr_ref, o_ref, *, q_tiles, gate_tiles, q_scale):
    j = pl.program_id(0)
    half = SWA_HALF_TILE
    kvw = SWA_KV_HEADS * HEAD_DIM

    def rows_do(fn):
        def chunk(rows):
            for side, w_ref in enumerate((wl_ref, wr_ref)):
                o_ref[rows, side * half:(side + 1) * half] = fn(side, w_ref, rows)
        _for_row_chunks(o_ref.shape[0], chunk)

    @pl.when(j < q_tiles)
    def _():
        rows_do(lambda side, w_ref, rows: (w_ref[rows, :] * q_scale).astype(BF16))

    @pl.when(jnp.logical_and(j >= q_tiles, j < q_tiles + gate_tiles))
    def _():
        rows_do(lambda side, w_ref, rows: w_ref[rows, :].astype(BF16))

    @pl.when(j == q_tiles + gate_tiles)
    def _():
        def twice(side, w_ref, rows):
            w = w_ref[rows, side * kvw:(side + 1) * kvw].astype(BF16)
            heads = [w[:, h * HEAD_DIM:(h + 1) * HEAD_DIM] for h in range(SWA_KV_HEADS)]
            return jnp.concatenate([piece for head in heads for piece in (head, head)], axis=1)
        rows_do(twice)


def _swa_weight_tiles(w_in, layer, q_scale):
    _, k, n = w_in.shape
    half = SWA_HALF_TILE
    width = SWA_HEADS * HEAD_DIM
    q_tiles = gate_tiles = width // (2 * half)
    kv_block = width // half
    n_out = q_tiles + gate_tiles + 1

    def src(j, side):
        in_q, in_gate = j < q_tiles, j < q_tiles + gate_tiles
        return jnp.where(in_q, 2 * j + side, jnp.where(in_gate, 2 * j + side + 1, kv_block))
    return pl.pallas_call(
        functools.partial(_swa_weight_tiles_kernel, q_tiles=q_tiles, gate_tiles=gate_tiles, q_scale=q_scale),
        grid=(n_out,),
        in_specs=[pl.BlockSpec((None, k, half), lambda j: (layer, 0, src(j, 0))),
                  pl.BlockSpec((None, k, half), lambda j: (layer, 0, src(j, 1)))],
        out_specs=pl.BlockSpec((None, k, 2 * half), lambda j: (j, 0, 0)),
        out_shape=jax.ShapeDtypeStruct((n_out, k, 2 * half), BF16),
        compiler_params=_params(1),
        name="swa_weight_tiles",
    )(w_in, w_in)


def _swa_mixer(x, g, w_in, layer, sinks, rel_bias, bsz, seq):
    width = SWA_HEADS * HEAD_DIM
    w_tiles = _swa_weight_tiles(w_in, layer, LOG2E * HEAD_DIM ** -0.5)
    proj = _norm_matmul(x, g, w_tiles, BF16, silu_cols=(width, 2 * width))
    return _swa_attention(proj, sinks, rel_bias, bsz, seq)


HALO = 8


def _conv_kernel(x_ref, g_ref, wb_ref, wc_ref, wu_ref, wg_ref, ck_ref, o_ref, a_ref, z_ref, carry_ref, *,
                 tiles_per_seq, rows_per_step):
    i, j = pl.program_id(0), pl.program_id(1)
    tm = a_ref.shape[1]
    _lookahead_first(x_ref, g_ref, a_ref)
    a = a_ref[i % 2]
    z = _dot(a, wc_ref[...]) * _dot(a, wu_ref[...])
    first = (i % tiles_per_seq) == 0

    @pl.when(first)
    def _():
        z_ref[:HALO, :] = jnp.zeros((HALO, z_ref.shape[1]), F32)

    @pl.when(jnp.logical_not(first))
    def _():
        z_ref[:HALO, :] = carry_ref[j]

    z_ref[HALO:, :] = z
    carry_ref[j] = z[tm - HALO:, :]
    conv = z_ref[HALO - 2:HALO - 2 + tm, :] * ck_ref[0:1, :]
    conv = conv + z_ref[HALO - 1:HALO - 1 + tm, :] * ck_ref[1:2, :]
    conv = conv + z * ck_ref[2:3, :]
    y = _dot(a, wb_ref[...]) * conv
    o_ref[...] = (y * _silu(_dot(a, wg_ref[...]))).astype(o_ref.dtype)
    _lookahead_next(x_ref, g_ref, a_ref, rows_per_step)


def _conv_mixer(x, g, w_in, layer, conv_kernel, seq, tm=1024, tn=512):
    m, k = x.shape
    width = w_in.shape[2] // 4
    tm, tn = min(tm, seq), min(tn, width)
    nt = width // tn
    w_spec = lambda q: pl.BlockSpec((None, k, tn), lambda i, j: (q * nt + j, 0, 0))
    w_in = _tile_cols(w_in, layer, tn)
    return pl.pallas_call(
        functools.partial(_conv_kernel, tiles_per_seq=seq // tm, rows_per_step=_lookahead_rows(tm, nt)),
        grid=(m // tm, nt),
        in_specs=[
            _lookahead_x_spec(tm, k, m // tm),
            pl.BlockSpec((1, k), lambda i, j: (0, 0)),
            w_spec(0), w_spec(1), w_spec(2), w_spec(3),
            pl.BlockSpec((CONV_TAPS, tn), lambda i, j: (0, j)),
        ],
        out_specs=pl.BlockSpec((tm, tn), lambda i, j: (i, j)),
        out_shape=jax.ShapeDtypeStruct((m, width), BF16),
        scratch_shapes=[
            pltpu.VMEM((2, tm, k), BF16),
            pltpu.VMEM((HALO + tm, tn), F32),
            pltpu.VMEM((nt, HALO, tn), F32),
        ],
        compiler_params=_params(2, VMEM_LIMIT_BIG),
        name="conv_mixer",
    )(x, g.reshape(1, k), w_in, w_in, w_in, w_in, conv_kernel.astype(F32))


def _ssm_operators(lam_re, lam_im, log_dt, b_re, b_im, c_re, c_im, d_skip):
    n_groups = lam_re.shape[0]
    n_oct = n_groups // GROUPS_PER_TILE
    L, C, N = SSM_CHUNK, SSM_GROUP, SSM_STATE
    dt = jnp.exp(log_dt.astype(F32))[None, :]
    lr, li = lam_re.astype(F32).T, lam_im.astype(F32).T
    mag = jnp.exp(lr * dt)
    ab_re, ab_im = mag * jnp.cos(li * dt), mag * jnp.sin(li * dt)
    den = lr * lr + li * li
    nr = ab_re - 1.0
    coef_re = (nr * lr + ab_im * li) / den
    coef_im = (ab_im * lr - nr * li) / den
    br, bi = b_re.astype(F32).transpose(2, 1, 0), b_im.astype(F32).transpose(2, 1, 0)
    bb_re = coef_re * br - coef_im * bi
    bb_im = coef_re * bi + coef_im * br
    cr, ci = c_re.astype(F32).transpose(1, 2, 0), c_im.astype(F32).transpose(1, 2, 0)
    pw_re, pw_im = [jnp.ones_like(ab_re)], [jnp.zeros_like(ab_im)]
    for _ in range(L):
        pw_re, pw_im = (pw_re + [pw_re[-1] * ab_re - pw_im[-1] * ab_im], pw_im + [pw_re[-1] * ab_im + pw_im[-1] * ab_re])
    p_re, p_im = jnp.stack(pw_re), jnp.stack(pw_im)
    w_re = cr[None] * p_re[:L, None] - ci[None] * p_im[:L, None]
    w_im = cr[None] * p_im[:L, None] + ci[None] * p_re[:L, None]
    lag_k = jnp.sum(w_re[:, None] * bb_re[None, :, None] - w_im[:, None] * bb_im[None, :, None], axis=3)
    q_re, q_im = jnp.stack(pw_re[L - 1::-1]), jnp.stack(pw_im[L - 1::-1])
    bc_re = q_re[:, None] * bb_re[None] - q_im[:, None] * bb_im[None]
    bc_im = q_re[:, None] * bb_im[None] + q_im[:, None] * bb_re[None]
    o_re = cr[None] * p_re[1:, None] - ci[None] * p_im[1:, None]
    o_im = cr[None] * p_im[1:, None] + ci[None] * p_re[1:, None]
    octs = lambda t: t.reshape(t.shape[:-1] + (n_oct, GROUPS_PER_TILE))
    to_lag = lambda t: octs(t).transpose(3, 0, 1, 4, 2).reshape(n_oct, L, C, LANES)
    to_rows = lambda t: octs(t).transpose(3, 0, 1, 4, 2).reshape(n_oct, L * C, GROUPS_PER_TILE * N)
    to_cols = lambda t: octs(t).transpose(3, 4, 2, 0, 1).reshape(n_oct, GROUPS_PER_TILE * N, L * C)
    to_vec = lambda t: octs(t).transpose(1, 2, 0).reshape(n_oct, 1, GROUPS_PER_TILE * N)
    return dict(
        lag=to_lag(lag_k).astype(BF16),
        bc_re=to_rows(bc_re).astype(BF16), bc_im=to_rows(bc_im).astype(BF16),
        oc_re=to_cols(o_re).astype(BF16), oc_im_neg=to_cols(-o_im).astype(BF16),
        al_re=to_vec(p_re[L]), al_im=to_vec(p_im[L]), d=d_skip.astype(F32).reshape(n_oct, 1, LANES),
    )


def _ssm_expanders():
    L, C, N, G8 = SSM_CHUNK, SSM_GROUP, SSM_STATE, GROUPS_PER_TILE
    wide = np.arange(L * LANES)
    w_step, w_group, w_chan = wide // LANES, (wide // C) % G8, wide % C
    small = np.arange(L * C)
    s_step, s_chan = small // C, small % C
    state_group = np.arange(G8 * N) // N
    spread = (w_step[:, None] == s_step[None, :]) & (w_chan[:, None] == s_chan[None, :])
    lane = np.arange(LANES)
    as_bf16 = lambda a: jnp.asarray(a.astype(np.float32), dtype=BF16)
    return dict(
        spread_cols=as_bf16(spread.T),
        keep_cols=as_bf16(state_group[:, None] == w_group[None, :]),
        spread_lag=as_bf16((lane % C)[:, None] == np.arange(C)[None, :]),
        keep_lag=jnp.asarray(((lane // C)[:, None] == (lane // C)[None, :]).astype(np.float32)),
    )


def _ssm_kernel(u_ref, lag_ref, bxr_ref, bxi_ref, cxr_ref, cxi_ref, alr_ref, ali_ref, d_ref,
                sc_ref, kc_ref, sl_ref, kl_ref, o_ref,
                ub_ref, bcr_ref, bci_ref, ocr_ref, oci_ref, zr_ref, zi_ref, hr_ref, hi_ref, y_ref, *, bsz):
    L = SSM_CHUNK
    rows = u_ref.shape[0] // L
    chunks = rows // bsz
    tile = 2 * LANES
    step_rows = lambda r: pl.ds(r, rows, stride=L)
    for r in range(L):
        ub_ref[:, r * LANES:(r + 1) * LANES] = u_ref[step_rows(r), :].astype(BF16)
    state_group = lax.broadcasted_iota(jnp.int32, (1, bxr_ref.shape[1]), 1) // SSM_STATE
    for compact_ref, full_ref in ((bxr_ref, bcr_ref), (bxi_ref, bci_ref)):
        for r in range(L):
            blk = compact_ref[r * SSM_GROUP:(r + 1) * SSM_GROUP, :]
            for grp in range(GROUPS_PER_TILE):
                first = r * LANES + grp * SSM_GROUP
                full_ref[first:first + SSM_GROUP, :] = jnp.where(state_group == grp, blk, jnp.zeros_like(blk))
    zr_ref[...] = _dot(ub_ref[...], bcr_ref[...])
    zi_ref[...] = _dot(ub_ref[...], bci_ref[...])
    a_re, a_im = alr_ref[...], ali_ref[...]

    def step(k, carry):
        new = []
        for b in range(bsz):
            h_re, h_im = carry[2 * b], carry[2 * b + 1]
            row = pl.ds(b * chunks + k, 1)
            hr_ref[row, :] = h_re
            hi_ref[row, :] = h_im
            new.append(a_re * h_re - a_im * h_im + zr_ref[row, :])
            new.append(a_re * h_im + a_im * h_re + zi_ref[row, :])
        return tuple(new)

    zero = jnp.zeros((1, a_re.shape[1]), F32)
    lax.fori_loop(0, chunks, step, (zero,) * (2 * bsz), unroll=True)

    ocr_ref[...] = (_dot(cxr_ref[...], sc_ref[...]) * kc_ref[...]).astype(BF16)
    oci_ref[...] = (_dot(cxi_ref[...], sc_ref[...]) * kc_ref[...]).astype(BF16)
    lag = [(_dot(sl_ref[...], lag_ref[d]) * kl_ref[...]).astype(BF16) for d in range(L)]
    lag_tile = []
    for dd in range(L // 2):
        below = lag[2 * dd - 1] if dd > 0 else jnp.zeros((LANES, LANES), BF16)
        lag_tile.append(jnp.concatenate([jnp.concatenate([lag[2 * dd], lag[2 * dd + 1]], axis=1),
                                         jnp.concatenate([below, lag[2 * dd]], axis=1)], axis=0))
    hb_re, hb_im = hr_ref[...].astype(BF16), hi_ref[...].astype(BF16)
    for t2 in range(L // 2):
        cols = slice(t2 * tile, (t2 + 1) * tile)
        y = _dot(hb_re, ocr_ref[:, cols]) + _dot(hb_im, oci_ref[:, cols])
        for r2 in range(t2 + 1):
            y = y + _dot(ub_ref[:, r2 * tile:(r2 + 1) * tile], lag_tile[t2 - r2])
        for half in range(2):
            t = 2 * t2 + half
            y_ref[step_rows(t), :] = jax.nn.gelu(y[:, half * LANES:(half + 1) * LANES] + d_ref[...] * u_ref[step_rows(t), :])

    def chunk(rows):
        o_ref[rows, :] = y_ref[rows, :].astype(o_ref.dtype)
    _for_row_chunks(o_ref.shape[0], chunk)


def _ssm_core(proj, ops, bsz, seq):
    m = proj.shape[0]
    width = proj.shape[1] // 2
    L, C = SSM_CHUNK, SSM_GROUP
    n_oct = width // LANES
    rows = m // L
    ow = L * LANES
    n_state = GROUPS_PER_TILE * SSM_STATE
    ex = _ssm_expanders()
    per_oct = lambda *shape: pl.BlockSpec((None,) + shape, lambda p: (p,) + (0,) * len(shape))
    const = lambda a: pl.BlockSpec(a.shape, lambda p: (0,) * a.ndim, pipeline_mode=pl.Buffered(1))
    consts = [ex['spread_cols'], ex['keep_cols'], ex['spread_lag'], ex['keep_lag']]
    return pl.pallas_call(
        functools.partial(_ssm_kernel, bsz=bsz),
        grid=(n_oct,),
        in_specs=[
            pl.BlockSpec((m, LANES), lambda p: (0, p)),
            per_oct(L, C, LANES), per_oct(L * C, n_state), per_oct(L * C, n_state),
            per_oct(n_state, L * C), per_oct(n_state, L * C), per_oct(1, n_state), per_oct(1, n_state), per_oct(1, LANES),
        ] + [const(a) for a in consts],
        out_specs=pl.BlockSpec((m, LANES), lambda p: (0, p)),
        out_shape=jax.ShapeDtypeStruct((m, width), BF16),
        scratch_shapes=[pltpu.VMEM((rows, ow), BF16)] + [pltpu.VMEM((ow, n_state), BF16)] * 2
        + [pltpu.VMEM((n_state, ow), BF16)] * 2 + [pltpu.VMEM((rows, n_state), F32)] * 4 + [pltpu.VMEM((m, LANES), F32)],
        compiler_params=_params(1),
        name="ssm_core",
    )(proj, ops['lag'], ops['bc_re'], ops['bc_im'], ops['oc_re'], ops['oc_im_neg'], ops['al_re'], ops['al_im'], ops['d'], *consts)


def _glu_kernel(y_ref, wa_ref, wb_ref, ba_ref, bb_ref, gate_ref, o_ref):
    y = y_ref[...]
    ga = _dot(y, wa_ref[...]) + ba_ref[...]
    gb = _dot(y, wb_ref[...]) + bb_ref[...]
    o_ref[...] = ((ga * jax.nn.sigmoid(gb)) * _silu(gate_ref[...])).astype(o_ref.dtype)


def _glu(y, w_glu, layer, b_glu, proj, tm=1024, tn=512):
    m, k = y.shape
    width = w_glu.shape[2] // 2
    tm, tn = min(tm, m), min(tn, width)
    nt = width // tn
    w_tiles = _tile_cols(w_glu, layer, tn)
    return pl.pallas_call(
        _glu_kernel,
        grid=(m // tm, nt),
        in_specs=[
            pl.BlockSpec((tm, k), lambda i, j: (i, 0)),
            pl.BlockSpec((None, k, tn), lambda i, j: (j, 0, 0)),
            pl.BlockSpec((None, k, tn), lambda i, j: (nt + j, 0, 0)),
            pl.BlockSpec((1, tn), lambda i, j: (0, j)),
            pl.BlockSpec((1, tn), lambda i, j: (0, nt + j)),
            pl.BlockSpec((tm, tn), lambda i, j: (i, nt + j)),
        ],
        out_specs=pl.BlockSpec((tm, tn), lambda i, j: (i, j)),
        out_shape=jax.ShapeDtypeStruct((m, width), BF16),
        compiler_params=_params(2),
        name="ssm_glu",
    )(y, w_tiles, w_tiles, b_glu, b_glu, proj)


def _ssm_mixer(x, g, w_in, layer, lam_re, lam_im, log_dt, b_re, b_im, c_re, c_im, d_skip, w_glu, b_glu, bsz, seq):
    proj = _norm_matmul(x, g, _tile_cols(w_in, layer, MATMUL_TN), F32)
    ops = _ssm_operators(lam_re, lam_im, log_dt, b_re, b_im, c_re, c_im, d_skip)
    y = _ssm_core(proj, ops, bsz, seq)
    return _glu(y, w_glu, layer, b_glu.astype(F32).reshape(1, -1), proj)


CUM_BLOCK = 128


def _split3(x):
    x1 = x.astype(BF16)
    r1 = x - x1.astype(F32)
    x2 = r1.astype(BF16)
    x3 = (r1 - x2.astype(F32)).astype(BF16)
    return x1, x2, x3


def _forget_cumsum_kernel(z_ref, b_ref, o_ref):
    n_blocks = z_ref.shape[0] // CUM_BLOCK
    ri = lax.broadcasted_iota(jnp.int32, (CUM_BLOCK, CUM_BLOCK), 0)
    ci = lax.broadcasted_iota(jnp.int32, (CUM_BLOCK, CUM_BLOCK), 1)
    tri = jnp.where(ci <= ri, 1.0, 0.0).astype(BF16)

    def body(i, carry):
        rows = pl.ds(pl.multiple_of(i * CUM_BLOCK, CUM_BLOCK), CUM_BLOCK)
        z = z_ref[rows, :] + b_ref[...]
        log_f = jnp.minimum(z, 0.0) - jnp.log1p(jnp.exp(-jnp.abs(z)))
        x1, x2, x3 = _split3(log_f)
        c = _dot(tri, x1) + _dot(tri, x2) + _dot(tri, x3) + carry
        o_ref[rows, :] = c
        return c[CUM_BLOCK - 1:, :]

    lax.fori_loop(0, n_blocks, body, jnp.zeros((1, z_ref.shape[1]), F32))


def _forget_cumsum(z, b, bsz, seq):
    lanes = z.shape[1]
    return pl.pallas_call(
        _forget_cumsum_kernel,
        grid=(bsz,),
        in_specs=[pl.BlockSpec((seq, lanes), lambda i: (i, 0)), pl.BlockSpec((1, lanes), lambda i: (0, 0))],
        out_specs=pl.BlockSpec((seq, lanes), lambda i: (i, 0)),
        out_shape=jax.ShapeDtypeStruct(z.shape, F32),
        compiler_params=_params(1),
        name="forget_cumsum",
    )(z, b)


FOX_STEP_HEADS = 8


def _fox_kernel(q_ref, k_ref, v_ref, sg_ref, cq_ref, ck_ref, o_ref, ka_ref, vt_ref, *head_refs, tk):
    nh = FOX_STEP_HEADS
    hg, qi = pl.program_id(1), pl.program_id(2)
    tq = q_ref.shape[0]
    qa_ref, s_ref, p_ref, acc_ref, pv_ref, m_ref = (head_refs[i::6] for i in range(6))
    lane = lax.broadcasted_iota(jnp.int32, (1, LANES), 1)
    free = (HEAD_DIM, 0)
    n_pieces = 3
    pair_lanes = lambda h: slice(h // 2 * LANES, (h // 2 + 1) * LANES)

    def pieces_of(c):
        return jnp.concatenate(_split3(c), axis=1)

    def bias_lanes(pieces, h, c_offset, one_offset):
        rr = lax.broadcasted_iota(jnp.int32, (n_pieces * LANES, LANES), 0)
        cc = lax.broadcasted_iota(jnp.int32, (n_pieces * LANES, LANES), 1)
        place = jnp.logical_and(rr % LANES == nh * hg + h, cc == free[h % 2] + c_offset + rr // LANES)
        placed = _dot(pieces, jnp.where(place, 1.0, 0.0).astype(BF16))
        first = free[h % 2] + one_offset
        return (placed + jnp.where(jnp.logical_and(lane >= first, lane < first + n_pieces), 1.0, 0.0)).astype(BF16)

    @pl.when(qi == 0)
    def _():
        def chunk(j, carry):
            rows = pl.ds(pl.multiple_of(j * tk, tk), tk)
            neg_c = pieces_of(ck_ref[rows, :] * -LOG2E)
            for h in range(nh):
                keys = k_ref[rows, pair_lanes(h)]
                ka_ref[h, j] = jnp.where((lane // HEAD_DIM) == h % 2, keys, bias_lanes(neg_c, h, 0, n_pieces))
            for pair in range(nh // 2):
                v_t = v_ref[rows, pair_lanes(2 * pair)].astype(F32).T.astype(BF16)
                for hh in range(2):
                    vt_ref[j, 2 * pair + hh, :HEAD_DIM, :] = v_t[hh * HEAD_DIM:(hh + 1) * HEAD_DIM]
                    vt_ref[j, 2 * pair + hh, HEAD_DIM:, :] = jnp.ones((ONES_ROWS, tk), BF16)
            return carry
        lax.fori_loop(0, k_ref.shape[0] // tk, chunk, 0)

    cq = pieces_of(cq_ref[...] * LOG2E)
    for h in range(nh):
        qa = jnp.where((lane // HEAD_DIM) == h % 2, q_ref[:, pair_lanes(h)], bias_lanes(cq, h, n_pieces, 0))
        qa_ref[h][...] = qa.astype(F32).T.astype(BF16)

    def scores(h, j):
        s_ref[h][...] = _dot(ka_ref[h, j], qa_ref[h][...])

    rows8 = 8
    rows16 = 16

    def absorb(h, j, diagonal):
        def strip(r, n):
            blk = s_ref[h][r:r + n, :]
            if diagonal:
                kpos = lax.broadcasted_iota(jnp.int32, (n, tq), 0)
                qpos = lax.broadcasted_iota(jnp.int32, (n, tq), 1)
                blk = jnp.where(kpos + r <= qpos, blk, NEG)
            return blk
        top = strip(0, rows8)
        for r in range(rows8, tk, rows8):
            top = jnp.maximum(top, strip(r, rows8))
        m = m_ref[h][...]
        m_new = jnp.maximum(m, top.max(axis=0, keepdims=True))
        m_ref[h][...] = m_new
        alpha = jnp.exp2(m - m_new)
        m_rows = jnp.broadcast_to(m_new, (rows16, tq))
        for r in range(0, tk, rows16):
            p_ref[h][r:r + rows16, :] = jnp.exp2(strip(r, rows16) - m_rows).astype(BF16)
        acc_ref[h][...] = alpha * (acc_ref[h][...] + pv_ref[h][...])

    def values(h, j):
        pv_ref[h][...] = _dot(vt_ref[j, h], p_ref[h][...])

    def step(j, diagonal):
        values(nh - 1, jnp.maximum(j - 1, 0))
        scores(1, j)
        for h in range(nh):
            absorb(h, j, diagonal)
            if h < nh - 1:
                values(h, j)
            if h + 2 < nh:
                scores(h + 2, j)
            elif h + 2 == nh and not diagonal:
                scores(0, j + 1)

    for h in range(nh):
        acc_ref[h][...] = jnp.zeros(acc_ref[h].shape, F32)
        pv_ref[h][...] = jnp.zeros(pv_ref[h].shape, F32)
        m_ref[h][...] = jnp.full(m_ref[h].shape, NEG, F32)
    p_ref[nh - 1][...] = jnp.zeros(p_ref[nh - 1].shape, BF16)
    scores(0, 0)

    def two_blocks(i, carry):
        step(2 * i, False)
        step(2 * i + 1, False)
        return carry
    lax.fori_loop(0, qi // 2, two_blocks, 0)

    @pl.when(qi % 2 == 0)
    def _():
        step(qi, True)

    @pl.when(qi % 2 == 1)
    def _():
        step(qi - 1, False)
        step(qi, True)

    values(nh - 1, qi)
    outs = []
    for h in range(nh):
        total = acc_ref[h][...] + pv_ref[h][...]
        outs.append(total[:HEAD_DIM] / total[HEAD_DIM:HEAD_DIM + 1])
    out = jnp.concatenate(outs, axis=0)
    o_ref[...] = (out.T * sg_ref[...].astype(F32)).astype(o_ref.dtype)


def _fox_attention(proj, csum, bsz, seq, t=512):
    width = FOX_HEADS * HEAD_DIM
    nh = FOX_STEP_HEADS
    gw = nh * HEAD_DIM
    n_groups = width // gw
    t = min(t, seq)
    nq = seq // t
    return pl.pallas_call(
        functools.partial(_fox_kernel, tk=t),
        grid=(bsz, n_groups, nq),
        in_specs=[
            pl.BlockSpec((t, gw), lambda b, h, i: (b * nq + i, h)),
            pl.BlockSpec((seq, gw), lambda b, h, i: (b, n_groups + h), pipeline_mode=pl.Buffered(1)),
            pl.BlockSpec((seq, gw), lambda b, h, i: (b, 2 * n_groups + h), pipeline_mode=pl.Buffered(1)),
            pl.BlockSpec((t, gw), lambda b, h, i: (b * nq + i, 3 * n_groups + h)),
            pl.BlockSpec((t, LANES), lambda b, h, i: (b * nq + i, 0)),
            pl.BlockSpec((seq, LANES), lambda b, h, i: (b, 0), pipeline_mode=pl.Buffered(1)),
        ],
        out_specs=pl.BlockSpec((t, gw), lambda b, h, i: (b * nq + i, h)),
        out_shape=jax.ShapeDtypeStruct((bsz * seq, width), BF16),
        scratch_shapes=[
            pltpu.VMEM((nh, nq, t, LANES), BF16), pltpu.VMEM((nq, nh, HEAD_DIM + ONES_ROWS, t), BF16),
        ] + nh * [pltpu.VMEM((LANES, t), BF16), pltpu.VMEM((t, t), F32), pltpu.VMEM((t, t), BF16),
                  pltpu.VMEM((HEAD_DIM + ONES_ROWS, t), F32), pltpu.VMEM((HEAD_DIM + ONES_ROWS, t), F32),
                  pltpu.VMEM((1, t), F32)],
        compiler_params=_params(3, VMEM_LIMIT_BIG),
        name="fox_attention",
    )(proj, proj, proj, proj, csum, csum)


def _fox_mixer(x, g, w_in, layer, w_fg, b_fg, bsz, seq, t=512):
    width = FOX_HEADS * HEAD_DIM
    col_scale = jnp.where(jnp.arange(4 * width) < width, LOG2E * HEAD_DIM ** -0.5, 1.0).astype(F32)
    proj = _norm_matmul(x, g, _tile_cols(w_in, layer, MATMUL_TN, col_scale), BF16, silu_cols=(3 * width, 4 * width))
    pad = LANES - FOX_HEADS
    z = _norm_matmul(x, g, _tile_cols_xla(jnp.pad(w_fg, ((0, 0), (0, pad))), LANES), F32)
    csum = _forget_cumsum(z, jnp.pad(b_fg.astype(F32), (0, pad)).reshape(1, LANES), bsz, seq)
    return _fox_attention(proj, csum, bsz, seq, t)


def kernel(x, p, norm_g, final_g, rel_bias, swa_w_in, swa_w_out, swa_sinks, conv_w_in, conv_kernel, conv_w_out, ssm_w_in, ssm_lam_re, ssm_lam_im, ssm_log_dt, ssm_b_re, ssm_b_im, ssm_c_re, ssm_c_im, ssm_d, ssm_w_glu, ssm_b_glu, ssm_w_out, fox_w_in, fox_w_fg, fox_b_fg, fox_w_out, ple_proj, ple_norm, ple_gate):
    bsz, seq, d_model = x.shape
    depth = p.shape[0]
    h = x.astype(F32).reshape(bsz * seq, d_model)
    for i in range(depth):
        mixer, j = i % N_MIXERS, i // N_MIXERS
        if mixer == 0:
            a, w_out = _swa_mixer(h, norm_g[i], swa_w_in, j, swa_sinks[j], rel_bias, bsz, seq), swa_w_out[j]
        elif mixer == 1:
            a, w_out = _conv_mixer(h, norm_g[i], conv_w_in, j, conv_kernel[j], seq), conv_w_out[j]
        elif mixer == 2:
            a = _ssm_mixer(h, norm_g[i], ssm_w_in, j, ssm_lam_re[j], ssm_lam_im[j], ssm_log_dt[j], ssm_b_re[j], ssm_b_im[j],
                           ssm_c_re[j], ssm_c_im[j], ssm_d[j], ssm_w_glu, ssm_b_glu[j], bsz, seq)
            w_out = ssm_w_out[j]
        else:
            a, w_out = _fox_mixer(h, norm_g[i], fox_w_in, j, fox_w_fg[j], fox_b_fg[j], bsz, seq), fox_w_out[j]
        h = _out_ple(a, w_out, h, ple_norm[i], ple_gate, p.reshape(depth, bsz * seq, -1), ple_proj, i,
                     final_g=final_g if i == depth - 1 else None)
    return h.reshape(bsz, seq, d_model).astype(x.dtype)
```

```python
import functools
import math

import numpy as np
import jax
import jax.numpy as jnp
from jax import lax
from jax.experimental import pallas as pl
from jax.experimental.pallas import tpu as pltpu

F32 = jnp.float32
BF16 = jnp.bfloat16

EPS = 1e-6
N_MIXERS = 4

SWA_HEADS = 32
SWA_KV_HEADS = 4
SWA_GROUP = SWA_HEADS // SWA_KV_HEADS
HEAD_DIM = 64
SWA_BLOCK = 128
WINDOW = 128
REL_BUCKETS = 32
REL_MAX_DIST = 128

CONV_TAPS = 3

SSM_GROUP = 16
SSM_STATE = 64
SSM_CHUNK = 16

FOX_HEADS = 32

LANES = 128
ONES_ROWS = 16
GROUPS_PER_TILE = LANES // SSM_GROUP
VMEM_LIMIT = 48 * 1024 * 1024
VMEM_LIMIT_BIG = 60 * 1024 * 1024

NEG = float(jnp.finfo(jnp.float32).min)
LOG2E = math.log2(math.e)


def _params(n_axes, vmem_limit=VMEM_LIMIT):
    return pltpu.CompilerParams(dimension_semantics=("arbitrary",) * n_axes, vmem_limit_bytes=vmem_limit)


def _dot(a, b):
    return jnp.dot(a, b, preferred_element_type=F32)


def _dot_nt(a, b):
    return lax.dot_general(a, b, (((1,), (1,)), ((), ())), preferred_element_type=F32)


def _rmsnorm_rows(x, g):
    return x * lax.rsqrt(jnp.mean(x * x, axis=-1, keepdims=True) + EPS) * g


def _silu(x):
    return x * jax.nn.sigmoid(x)


ROW_CHUNK = 64


def _for_row_chunks(n_rows, fn):
    def body(c, carry):
        fn(pl.ds(pl.multiple_of(c * ROW_CHUNK, ROW_CHUNK), ROW_CHUNK))
        return carry
    lax.fori_loop(0, n_rows // ROW_CHUNK, body, 0)


def _tile_cols_xla(w, tn):
    k, n = w.shape
    return w.astype(BF16).reshape(k, n // tn, tn).transpose(1, 0, 2)


def _cast_tiles_kernel(w_ref, o_ref):
    def chunk(rows):
        o_ref[rows, :] = w_ref[rows, :].astype(BF16)
    _for_row_chunks(o_ref.shape[0], chunk)


def _scale_cast_tiles_kernel(w_ref, s_ref, o_ref):
    def chunk(rows):
        o_ref[rows, :] = (w_ref[rows, :] * s_ref[...]).astype(BF16)
    _for_row_chunks(o_ref.shape[0], chunk)


def _tile_cols(w, layer, tn, col_scale=None):
    _, k, n = w.shape
    tn = min(tn, n)
    w_spec = pl.BlockSpec((None, k, tn), lambda j: (layer, 0, j))
    common = dict(
        grid=(n // tn,),
        out_specs=pl.BlockSpec((None, k, tn), lambda j: (j, 0, 0)),
        out_shape=jax.ShapeDtypeStruct((n // tn, k, tn), BF16),
        compiler_params=_params(1),
        name="weight_tiles",
    )
    if col_scale is None:
        return pl.pallas_call(_cast_tiles_kernel, in_specs=[w_spec], **common)(w)
    s_spec = pl.BlockSpec((1, tn), lambda j: (0, j))
    return pl.pallas_call(_scale_cast_tiles_kernel, in_specs=[w_spec, s_spec], **common)(w, col_scale.reshape(1, n))


def _lookahead_x_spec(tm, k, n_row_tiles):
    def tile(i, j):
        first_step = jnp.logical_and(i == 0, j == 0)
        return jnp.where(first_step, 0, jnp.minimum(i + 1, n_row_tiles - 1))
    return pl.BlockSpec((tm, k), lambda i, j: (tile(i, j), 0))


def _lookahead_rows(tm, nt):
    return -(-tm // ((nt - 1) * ROW_CHUNK)) * ROW_CHUNK


def _lookahead_first(x_ref, g_ref, a_ref):
    @pl.when(jnp.logical_and(pl.program_id(0) == 0, pl.program_id(1) == 0))
    def _():
        def chunk(rows):
            a_ref[0, rows, :] = _rmsnorm_rows(x_ref[rows, :], g_ref[...]).astype(BF16)
        _for_row_chunks(a_ref.shape[1], chunk)


def _lookahead_next(x_ref, g_ref, a_ref, rows_per_step):
    i, j = pl.program_id(0), pl.program_id(1)
    tm = a_ref.shape[1]
    start = jnp.clip((j - 1) * rows_per_step, 0, tm - rows_per_step)
    for c in range(rows_per_step // ROW_CHUNK):
        rows = pl.ds(pl.multiple_of(start + c * ROW_CHUNK, ROW_CHUNK), ROW_CHUNK)
        a_ref[(i + 1) % 2, rows, :] = _rmsnorm_rows(x_ref[rows, :], g_ref[...]).astype(BF16)


def _norm_matmul_kernel(x_ref, g_ref, w_ref, o_ref, a_ref, *, silu_tiles, rows_per_step):
    i, j = pl.program_id(0), pl.program_id(1)
    if rows_per_step is None:
        def chunk(rows):
            a_ref[0, rows, :] = _rmsnorm_rows(x_ref[rows, :], g_ref[...]).astype(BF16)
        _for_row_chunks(a_ref.shape[1], chunk)
        acc = _dot(a_ref[0], w_ref[...])
    else:
        _lookahead_first(x_ref, g_ref, a_ref)
        acc = _dot(a_ref[i % 2], w_ref[...])
        _lookahead_next(x_ref, g_ref, a_ref, rows_per_step)
    if silu_tiles is None:
        o_ref[...] = acc.astype(o_ref.dtype)
    else:
        gated = jnp.logical_and(j >= silu_tiles[0], j < silu_tiles[1])

        @pl.when(gated)
        def _():
            o_ref[...] = _silu(acc).astype(o_ref.dtype)

        @pl.when(jnp.logical_not(gated))
        def _():
            o_ref[...] = acc.astype(o_ref.dtype)


MATMUL_TN = 1024


def _norm_matmul(x, g, w_tiles, out_dtype, silu_cols=None, tm=1024):
    m, k = x.shape
    nt, _, tn = w_tiles.shape
    n = nt * tn
    tm = min(tm, m)
    n_row_tiles = m // tm
    silu_tiles = None if silu_cols is None else (silu_cols[0] // tn, silu_cols[1] // tn)
    lookahead = nt > 1
    return pl.pallas_call(
        functools.partial(_norm_matmul_kernel, silu_tiles=silu_tiles,
                          rows_per_step=_lookahead_rows(tm, nt) if lookahead else None),
        grid=(n_row_tiles, nt),
        in_specs=[
            _lookahead_x_spec(tm, k, n_row_tiles) if lookahead else pl.BlockSpec((tm, k), lambda i, j: (i, 0)),
            pl.BlockSpec((1, k), lambda i, j: (0, 0)),
            pl.BlockSpec((None, k, tn), lambda i, j: (j, 0, 0)),
        ],
        out_specs=pl.BlockSpec((tm, tn), lambda i, j: (i, j)),
        out_shape=jax.ShapeDtypeStruct((m, n), out_dtype),
        scratch_shapes=[pltpu.VMEM((2, tm, k), BF16)],
        compiler_params=_params(2, VMEM_LIMIT_BIG),
        name="norm_matmul",
    )(x, g.reshape(1, k), w_tiles)


def _out_ple_kernel(a_ref, wo_ref, x_ref, g_ref, wg_ref, p_ref, wp_ref, gf_ref, o_ref, x1_ref, hn_ref, ss_ref, *, nt, final):
    j = pl.program_id(1)
    tn = x_ref.shape[1]
    n = nt * tn

    @pl.when(j == 0)
    def _():
        ss_ref[...] = jnp.zeros(ss_ref.shape, F32)

    @pl.when(j < nt)
    def _():
        x1 = x_ref[...] + _dot(a_ref[...], wo_ref[j])
        x1_ref[j] = x1
        hn_ref[j] = (x1 * g_ref[j]).astype(BF16)
        ss_ref[0] += jnp.sum(x1 * x1, axis=-1, keepdims=True)

    @pl.when(j >= nt)
    def _():
        t = j - nt
        emb = _dot(p_ref[...].astype(BF16), wp_ref[t])
        acc = _dot(hn_ref[0], wg_ref[t, :tn, :])
        for kt in range(1, nt):
            acc = acc + _dot(hn_ref[kt], wg_ref[t, kt * tn:(kt + 1) * tn, :])
        x2 = x1_ref[t] + emb * jax.nn.sigmoid(acc * lax.rsqrt(ss_ref[0] / n + EPS))
        if final:
            x1_ref[t] = x2
            ss_ref[1] += jnp.sum(x2 * x2, axis=-1, keepdims=True)
        else:
            o_ref[...] = x2

    if final:
        @pl.when(j == 2 * nt - 1)
        def _():
            inv = lax.rsqrt(ss_ref[1] / n + EPS)
            for t in range(nt):
                o_ref[:, t * tn:(t + 1) * tn] = x1_ref[t] * inv * gf_ref[t]


def _out_ple(a, w_out, x, g, w_gate, p, w_proj, layer, final_g=None, tm=1024, tn=512):
    m, k = a.shape
    n = w_out.shape[1]
    pd = p.shape[2]
    final = final_g is not None
    tm, tn = min(tm, m), min(tn, n)
    nt = n // tn
    if final:
        out_spec = pl.BlockSpec((tm, n), lambda i, j: (i, 0), pipeline_mode=pl.Buffered(1))
    else:
        out_spec = pl.BlockSpec((tm, tn), lambda i, j: (i, jnp.maximum(j - nt, 0)))
    resident = lambda rows: pl.BlockSpec((nt, rows, tn), lambda i, j: (0, 0, 0), pipeline_mode=pl.Buffered(1))
    row_vec = pl.BlockSpec((nt, 1, tn), lambda i, j: (0, 0, 0))
    return pl.pallas_call(
        functools.partial(_out_ple_kernel, nt=nt, final=final),
        grid=(m // tm, 2 * nt),
        in_specs=[
            pl.BlockSpec((tm, k), lambda i, j: (i, 0)),
            resident(k),
            pl.BlockSpec((tm, tn), lambda i, j: (i, jnp.minimum(j, nt - 1))),
            row_vec,
            resident(n),
            pl.BlockSpec((None, tm, pd), lambda i, j: (layer, i, 0)),
            resident(pd),
            row_vec,
        ],
        out_specs=out_spec,
        out_shape=jax.ShapeDtypeStruct((m, n), F32),
        scratch_shapes=[pltpu.VMEM((nt, tm, tn), F32), pltpu.VMEM((nt, tm, tn), BF16), pltpu.VMEM((2, tm, 1), F32)],
        compiler_params=_params(2, VMEM_LIMIT_BIG),
        name="out_ple_final" if final else "out_ple",
    )(a, _tile_cols(w_out[None], 0, tn), x, g.reshape(nt, 1, tn), _tile_cols(w_gate, layer, tn), p,
      _tile_cols(w_proj, layer, tn), (final_g if final else g).reshape(nt, 1, tn))


def _t5_bucket(dist):
    max_exact = REL_BUCKETS // 2
    d = np.maximum(dist, 1).astype(np.float32)
    large = max_exact + (np.log(d / max_exact) / np.log(REL_MAX_DIST / max_exact) * (REL_BUCKETS - max_exact)).astype(np.int32)
    large = np.minimum(large, REL_BUCKETS - 1)
    return np.where(dist < max_exact, dist, large).astype(np.int32)


assert WINDOW == SWA_BLOCK


def _swa_bucket_table():
    qi = np.arange(SWA_BLOCK)[None, :]
    kj = np.arange(2 * SWA_BLOCK)[:, None]
    return _t5_bucket(np.clip(qi + SWA_BLOCK - kj, 0, None))


def _swa_kernel(sink_ref, rel_ref, bucket_ref, q_ref, sg_ref, kp_ref, kc_ref, vp_ref, vc_ref, o_ref, bias_ref, qs_ref):
    n = pl.program_id(1)
    blk = SWA_BLOCK
    lane = lax.broadcasted_iota(jnp.int32, (1, LANES), 1)

    @pl.when(jnp.logical_and(pl.program_id(0) == 0, n == 0))
    def _():
        bucket = bucket_ref[...]
        kj = lax.broadcasted_iota(jnp.int32, (2 * blk, blk), 0)
        qi = lax.broadcasted_iota(jnp.int32, (2 * blk, blk), 1)
        band = jnp.logical_or(jnp.logical_and(kj < blk, kj > qi), jnp.logical_and(kj >= blk, kj - blk <= qi))

        def per_head(head, carry):
            acc = jnp.zeros((2 * blk, blk), F32)
            for b in range(REL_BUCKETS):
                acc = jnp.where(bucket == b, rel_ref[b, head] * LOG2E, acc)
            bias_ref[head] = jnp.where(band, acc, NEG)
            return carry
        lax.fori_loop(0, SWA_HEADS, per_head, 0)

    for kvh in range(SWA_KV_HEADS):
        cols = slice(kvh * LANES, (kvh + 1) * LANES)
        heads = range(kvh * SWA_GROUP, (kvh + 1) * SWA_GROUP)
        for g, head in enumerate(heads):
            q_pair = q_ref[:, head // 2 * LANES:(head // 2 + 1) * LANES]
            qs_ref[g * blk:(g + 1) * blk, :] = jnp.where((lane // HEAD_DIM) == head % 2, q_pair, jnp.zeros_like(q_pair))
        keys = jnp.concatenate([kp_ref[:, cols], kc_ref[:, cols]], axis=0)
        vals = jnp.concatenate([vp_ref[:, cols], vc_ref[:, cols]], axis=0)
        s = _dot_nt(keys, qs_ref[...]) + jnp.concatenate([bias_ref[head] for head in heads], axis=1)
        s_prev = jnp.where(n > 0, s[:blk], NEG)
        s_cur = s[blk:]
        sink = jnp.concatenate([jnp.full((1, blk), sink_ref[head] * LOG2E, F32) for head in heads], axis=1)
        m = jnp.maximum(jnp.maximum(s_prev, s_cur).max(axis=0, keepdims=True), sink)
        e = jnp.concatenate([jnp.exp2(s_prev - m).astype(BF16), jnp.exp2(s_cur - m).astype(BF16)], axis=0)
        v_t = jnp.concatenate([vals.astype(F32).T.astype(BF16)[:HEAD_DIM], jnp.ones((ONES_ROWS, 2 * blk), BF16)], axis=0)
        out = _dot(v_t, e)
        out = out[:HEAD_DIM] / (out[HEAD_DIM:HEAD_DIM + 1] + jnp.exp2(sink - m))
        for pair in range(SWA_GROUP // 2):
            head = kvh * SWA_GROUP + 2 * pair
            even = out[:, 2 * pair * blk:(2 * pair + 1) * blk]
            odd = out[:, (2 * pair + 1) * blk:(2 * pair + 2) * blk]
            cols = slice(head // 2 * LANES, (head // 2 + 1) * LANES)
            o_ref[:, cols] = (jnp.concatenate([even, odd], axis=0).T * sg_ref[:, cols].astype(F32)).astype(o_ref.dtype)


def _swa_attention(proj, sinks, rel_bias, bsz, seq):
    nb = seq // SWA_BLOCK
    width = SWA_HEADS * HEAD_DIM
    kvw = SWA_KV_HEADS * LANES
    k_blk = 2 * width // kvw
    cur = lambda b, n: b * nb + n
    prev = lambda b, n: b * nb + jnp.maximum(n - 1, 0)
    smem = pl.BlockSpec(memory_space=pltpu.SMEM)
    return pl.pallas_call(
        _swa_kernel,
        grid=(bsz, nb),
        in_specs=[
            smem, smem,
            pl.BlockSpec((2 * SWA_BLOCK, SWA_BLOCK), lambda b, n: (0, 0)),
            pl.BlockSpec((SWA_BLOCK, width), lambda b, n: (cur(b, n), 0)),
            pl.BlockSpec((SWA_BLOCK, width), lambda b, n: (cur(b, n), 1)),
            pl.BlockSpec((SWA_BLOCK, kvw), lambda b, n: (prev(b, n), k_blk)),
            pl.BlockSpec((SWA_BLOCK, kvw), lambda b, n: (cur(b, n), k_blk)),
            pl.BlockSpec((SWA_BLOCK, kvw), lambda b, n: (prev(b, n), k_blk + 1)),
            pl.BlockSpec((SWA_BLOCK, kvw), lambda b, n: (cur(b, n), k_blk + 1)),
        ],
        out_specs=pl.BlockSpec((SWA_BLOCK, width), lambda b, n: (cur(b, n), 0)),
        out_shape=jax.ShapeDtypeStruct((bsz * seq, width), BF16),
        scratch_shapes=[
            pltpu.VMEM((SWA_HEADS, 2 * SWA_BLOCK, SWA_BLOCK), F32),
            pltpu.VMEM((SWA_GROUP * SWA_BLOCK, LANES), BF16),
        ],
        compiler_params=_params(2),
        name="swa_attention",
    )(sinks.astype(F32), rel_bias.astype(F32), jnp.asarray(_swa_bucket_table()), proj, proj, proj, proj, proj, proj)


SWA_HALF_TILE = 512


def _swa_weight_tiles_kernel(wl_ref, wr_ref, o_ref, *, q_tiles, gate_tiles, q_scale):
    j = pl.program_id(0)
    half = SWA_HALF_TILE
    kvw = SWA_KV_HEADS * HEAD_DIM

    def rows_do(fn):
        def chunk(rows):
            for side, w_ref in enumerate((wl_ref, wr_ref)):
                o_ref[rows, side * half:(side + 1) * half] = fn(side, w_ref, rows)
        _for_row_chunks(o_ref.shape[0], chunk)

    @pl.when(j < q_tiles)
    def _():
        rows_do(lambda side, w_ref, rows: (w_ref[rows, :] * q_scale).astype(BF16))

    @pl.when(jnp.logical_and(j >= q_tiles, j < q_tiles + gate_tiles))
    def _():
        rows_do(lambda side, w_ref, rows: w_ref[rows, :].astype(BF16))

    @pl.when(j == q_tiles + gate_tiles)
    def _():
        def twice(side, w_ref, rows):
            w = w_ref[rows, side * kvw:(side + 1) * kvw].astype(BF16)
            heads = [w[:, h * HEAD_DIM:(h + 1) * HEAD_DIM] for h in range(SWA_KV_HEADS)]
            return jnp.concatenate([piece for head in heads for piece in (head, head)], axis=1)
        rows_do(twice)


def _swa_weight_tiles(w_in, layer, q_scale):
    _, k, n = w_in.shape
    half = SWA_HALF_TILE
    width = SWA_HEADS * HEAD_DIM
    q_tiles = gate_tiles = width // (2 * half)
    kv_block = width // half
    n_out = q_tiles + gate_tiles + 1

    def src(j, side):
        in_q, in_gate = j < q_tiles, j < q_tiles + gate_tiles
        return jnp.where(in_q, 2 * j + side, jnp.where(in_gate, 2 * j + side + 1, kv_block))
    return pl.pallas_call(
        functools.partial(_swa_weight_tiles_kernel, q_tiles=q_tiles, gate_tiles=gate_tiles, q_scale=q_scale),
        grid=(n_out,),
        in_specs=[pl.BlockSpec((None, k, half), lambda j: (layer, 0, src(j, 0))),
                  pl.BlockSpec((None, k, half), lambda j: (layer, 0, src(j, 1)))],
        out_specs=pl.BlockSpec((None, k, 2 * half), lambda j: (j, 0, 0)),
        out_shape=jax.ShapeDtypeStruct((n_out, k, 2 * half), BF16),
        compiler_params=_params(1),
        name="swa_weight_tiles",
    )(w_in, w_in)


def _swa_mixer(x, g, w_in, layer, sinks, rel_bias, bsz, seq):
    width = SWA_HEADS * HEAD_DIM
    w_tiles = _swa_weight_tiles(w_in, layer, LOG2E * HEAD_DIM ** -0.5)
    proj = _norm_matmul(x, g, w_tiles, BF16, silu_cols=(width, 2 * width))
    return _swa_attention(proj, sinks, rel_bias, bsz, seq)


HALO = 8


def _conv_kernel(x_ref, g_ref, wb_ref, wc_ref, wu_ref, wg_ref, ck_ref, o_ref, a_ref, z_ref, carry_ref, *,
                 tiles_per_seq, rows_per_step):
    i, j = pl.program_id(0), pl.program_id(1)
    tm = a_ref.shape[1]
    _lookahead_first(x_ref, g_ref, a_ref)
    a = a_ref[i % 2]
    z = _dot(a, wc_ref[...]) * _dot(a, wu_ref[...])
    first = (i % tiles_per_seq) == 0

    @pl.when(first)
    def _():
        z_ref[:HALO, :] = jnp.zeros((HALO, z_ref.shape[1]), F32)

    @pl.when(jnp.logical_not(first))
    def _():
        z_ref[:HALO, :] = carry_ref[j]

    z_ref[HALO:, :] = z
    carry_ref[j] = z[tm - HALO:, :]
    conv = z_ref[HALO - 2:HALO - 2 + tm, :] * ck_ref[0:1, :]
    conv = conv + z_ref[HALO - 1:HALO - 1 + tm, :] * ck_ref[1:2, :]
    conv = conv + z * ck_ref[2:3, :]
    y = _dot(a, wb_ref[...]) * conv
    o_ref[...] = (y * _silu(_dot(a, wg_ref[...]))).astype(o_ref.dtype)
    _lookahead_next(x_ref, g_ref, a_ref, rows_per_step)


def _conv_mixer(x, g, w_in, layer, conv_kernel, seq, tm=1024, tn=512):
    m, k = x.shape
    width = w_in.shape[2] // 4
    tm, tn = min(tm, seq), min(tn, width)
    nt = width // tn
    w_spec = lambda q: pl.BlockSpec((None, k, tn), lambda i, j: (q * nt + j, 0, 0))
    w_in = _tile_cols(w_in, layer, tn)
    return pl.pallas_call(
        functools.partial(_conv_kernel, tiles_per_seq=seq // tm, rows_per_step=_lookahead_rows(tm, nt)),
        grid=(m // tm, nt),
        in_specs=[
            _lookahead_x_spec(tm, k, m // tm),
            pl.BlockSpec((1, k), lambda i, j: (0, 0)),
            w_spec(0), w_spec(1), w_spec(2), w_spec(3),
            pl.BlockSpec((CONV_TAPS, tn), lambda i, j: (0, j)),
        ],
        out_specs=pl.BlockSpec((tm, tn), lambda i, j: (i, j)),
        out_shape=jax.ShapeDtypeStruct((m, width), BF16),
        scratch_shapes=[
            pltpu.VMEM((2, tm, k), BF16),
            pltpu.VMEM((HALO + tm, tn), F32),
            pltpu.VMEM((nt, HALO, tn), F32),
        ],
        compiler_params=_params(2, VMEM_LIMIT_BIG),
        name="conv_mixer",
    )(x, g.reshape(1, k), w_in, w_in, w_in, w_in, conv_kernel.astype(F32))


def _ssm_operators(lam_re, lam_im, log_dt, b_re, b_im, c_re, c_im, d_skip):
    n_groups = lam_re.shape[0]
    n_oct = n_groups // GROUPS_PER_TILE
    L, C, N = SSM_CHUNK, SSM_GROUP, SSM_STATE
    dt = jnp.exp(log_dt.astype(F32))[None, :]
    lr, li = lam_re.astype(F32).T, lam_im.astype(F32).T
    mag = jnp.exp(lr * dt)
    ab_re, ab_im = mag * jnp.cos(li * dt), mag * jnp.sin(li * dt)
    den = lr * lr + li * li
    nr = ab_re - 1.0
    coef_re = (nr * lr + ab_im * li) / den
    coef_im = (ab_im * lr - nr * li) / den
    br, bi = b_re.astype(F32).transpose(2, 1, 0), b_im.astype(F32).transpose(2, 1, 0)
    bb_re = coef_re * br - coef_im * bi
    bb_im = coef_re * bi + coef_im * br
    cr, ci = c_re.astype(F32).transpose(1, 2, 0), c_im.astype(F32).transpose(1, 2, 0)
    pw_re, pw_im = [jnp.ones_like(ab_re)], [jnp.zeros_like(ab_im)]
    for _ in range(L):
        pw_re, pw_im = (pw_re + [pw_re[-1] * ab_re - pw_im[-1] * ab_im], pw_im + [pw_re[-1] * ab_im + pw_im[-1] * ab_re])
    p_re, p_im = jnp.stack(pw_re), jnp.stack(pw_im)
    w_re = cr[None] * p_re[:L, None] - ci[None] * p_im[:L, None]
    w_im = cr[None] * p_im[:L, None] + ci[None] * p_re[:L, None]
    lag_k = jnp.sum(w_re[:, None] * bb_re[None, :, None] - w_im[:, None] * bb_im[None, :, None], axis=3)
    q_re, q_im = jnp.stack(pw_re[L - 1::-1]), jnp.stack(pw_im[L - 1::-1])
    bc_re = q_re[:, None] * bb_re[None] - q_im[:, None] * bb_im[None]
    bc_im = q_re[:, None] * bb_im[None] + q_im[:, None] * bb_re[None]
    o_re = cr[None] * p_re[1:, None] - ci[None] * p_im[1:, None]
    o_im = cr[None] * p_im[1:, None] + ci[None] * p_re[1:, None]
    octs = lambda t: t.reshape(t.shape[:-1] + (n_oct, GROUPS_PER_TILE))
    to_lag = lambda t: octs(t).transpose(3, 0, 1, 4, 2).reshape(n_oct, L, C, LANES)
    to_rows = lambda t: octs(t).transpose(3, 0, 1, 4, 2).reshape(n_oct, L * C, GROUPS_PER_TILE * N)
    to_cols = lambda t: octs(t).transpose(3, 4, 2, 0, 1).reshape(n_oct, GROUPS_PER_TILE * N, L * C)
    to_vec = lambda t: octs(t).transpose(1, 2, 0).reshape(n_oct, 1, GROUPS_PER_TILE * N)
    return dict(
        lag=to_lag(lag_k).astype(BF16),
        bc_re=to_rows(bc_re).astype(BF16), bc_im=to_rows(bc_im).astype(BF16),
        oc_re=to_cols(o_re).astype(BF16), oc_im_neg=to_cols(-o_im).astype(BF16),
        al_re=to_vec(p_re[L]), al_im=to_vec(p_im[L]), d=d_skip.astype(F32).reshape(n_oct, 1, LANES),
    )


def _ssm_expanders():
    L, C, N, G8 = SSM_CHUNK, SSM_GROUP, SSM_STATE, GROUPS_PER_TILE
    wide = np.arange(L * LANES)
    w_step, w_group, w_chan = wide // LANES, (wide // C) % G8, wide % C
    small = np.arange(L * C)
    s_step, s_chan = small // C, small % C
    state_group = np.arange(G8 * N) // N
    spread = (w_step[:, None] == s_step[None, :]) & (w_chan[:, None] == s_chan[None, :])
    lane = np.arange(LANES)
    as_bf16 = lambda a: jnp.asarray(a.astype(np.float32), dtype=BF16)
    return dict(
        spread_cols=as_bf16(spread.T),
        keep_cols=as_bf16(state_group[:, None] == w_group[None, :]),
        spread_lag=as_bf16((lane % C)[:, None] == np.arange(C)[None, :]),
        keep_lag=jnp.asarray(((lane // C)[:, None] == (lane // C)[None, :]).astype(np.float32)),
    )


def _ssm_kernel(u_ref, lag_ref, bxr_ref, bxi_ref, cxr_ref, cxi_ref, alr_ref, ali_ref, d_ref,
                sc_ref, kc_ref, sl_ref, kl_ref, o_ref,
                ub_ref, bcr_ref, bci_ref, ocr_ref, oci_ref, zr_ref, zi_ref, hr_ref, hi_ref, y_ref, *, bsz):
    L = SSM_CHUNK
    rows = u_ref.shape[0] // L
    chunks = rows // bsz
    tile = 2 * LANES
    step_rows = lambda r: pl.ds(r, rows, stride=L)
    for r in range(L):
        ub_ref[:, r * LANES:(r + 1) * LANES] = u_ref[step_rows(r), :].astype(BF16)
    state_group = lax.broadcasted_iota(jnp.int32, (1, bxr_ref.shape[1]), 1) // SSM_STATE
    for compact_ref, full_ref in ((bxr_ref, bcr_ref), (bxi_ref, bci_ref)):
        for r in range(L):
            blk = compact_ref[r * SSM_GROUP:(r + 1) * SSM_GROUP, :]
            for grp in range(GROUPS_PER_TILE):
                first = r * LANES + grp * SSM_GROUP
                full_ref[first:first + SSM_GROUP, :] = jnp.where(state_group == grp, blk, jnp.zeros_like(blk))
    zr_ref[...] = _dot(ub_ref[...], bcr_ref[...])
    zi_ref[...] = _dot(ub_ref[...], bci_ref[...])
    a_re, a_im = alr_ref[...], ali_ref[...]

    def step(k, carry):
        new = []
        for b in range(bsz):
            h_re, h_im = carry[2 * b], carry[2 * b + 1]
            row = pl.ds(b * chunks + k, 1)
            hr_ref[row, :] = h_re
            hi_ref[row, :] = h_im
            new.append(a_re * h_re - a_im * h_im + zr_ref[row, :])
            new.append(a_re * h_im + a_im * h_re + zi_ref[row, :])
        return tuple(new)

    zero = jnp.zeros((1, a_re.shape[1]), F32)
    lax.fori_loop(0, chunks, step, (zero,) * (2 * bsz), unroll=True)

    ocr_ref[...] = (_dot(cxr_ref[...], sc_ref[...]) * kc_ref[...]).astype(BF16)
    oci_ref[...] = (_dot(cxi_ref[...], sc_ref[...]) * kc_ref[...]).astype(BF16)
    lag = [(_dot(sl_ref[...], lag_ref[d]) * kl_ref[...]).astype(BF16) for d in range(L)]
    lag_tile = []
    for dd in range(L // 2):
        below = lag[2 * dd - 1] if dd > 0 else jnp.zeros((LANES, LANES), BF16)
        lag_tile.append(jnp.concatenate([jnp.concatenate([lag[2 * dd], lag[2 * dd + 1]], axis=1),
                                         jnp.concatenate([below, lag[2 * dd]], axis=1)], axis=0))
    hb_re, hb_im = hr_ref[...].astype(BF16), hi_ref[...].astype(BF16)
    for t2 in range(L // 2):
        cols = slice(t2 * tile, (t2 + 1) * tile)
        y = _dot(hb_re, ocr_ref[:, cols]) + _dot(hb_im, oci_ref[:, cols])
        for r2 in range(t2 + 1):
            y = y + _dot(ub_ref[:, r2 * tile:(r2 + 1) * tile], lag_tile[t2 - r2])
        for half in range(2):
            t = 2 * t2 + half
            y_ref[step_rows(t), :] = jax.nn.gelu(y[:, half * LANES:(half + 1) * LANES] + d_ref[...] * u_ref[step_rows(t), :])

    def chunk(rows):
        o_ref[rows, :] = y_ref[rows, :].astype(o_ref.dtype)
    _for_row_chunks(o_ref.shape[0], chunk)


def _ssm_core(proj, ops, bsz, seq):
    m = proj.shape[0]
    width = proj.shape[1] // 2
    L, C = SSM_CHUNK, SSM_GROUP
    n_oct = width // LANES
    rows = m // L
    ow = L * LANES
    n_state = GROUPS_PER_TILE * SSM_STATE
    ex = _ssm_expanders()
    per_oct = lambda *shape: pl.BlockSpec((None,) + shape, lambda p: (p,) + (0,) * len(shape))
    const = lambda a: pl.BlockSpec(a.shape, lambda p: (0,) * a.ndim, pipeline_mode=pl.Buffered(1))
    consts = [ex['spread_cols'], ex['keep_cols'], ex['spread_lag'], ex['keep_lag']]
    return pl.pallas_call(
        functools.partial(_ssm_kernel, bsz=bsz),
        grid=(n_oct,),
        in_specs=[
            pl.BlockSpec((m, LANES), lambda p: (0, p)),
            per_oct(L, C, LANES), per_oct(L * C, n_state), per_oct(L * C, n_state),
            per_oct(n_state, L * C), per_oct(n_state, L * C), per_oct(1, n_state), per_oct(1, n_state), per_oct(1, LANES),
        ] + [const(a) for a in consts],
        out_specs=pl.BlockSpec((m, LANES), lambda p: (0, p)),
        out_shape=jax.ShapeDtypeStruct((m, width), BF16),
        scratch_shapes=[pltpu.VMEM((rows, ow), BF16)] + [pltpu.VMEM((ow, n_state), BF16)] * 2
        + [pltpu.VMEM((n_state, ow), BF16)] * 2 + [pltpu.VMEM((rows, n_state), F32)] * 4 + [pltpu.VMEM((m, LANES), F32)],
        compiler_params=_params(1),
        name="ssm_core",
    )(proj, ops['lag'], ops['bc_re'], ops['bc_im'], ops['oc_re'], ops['oc_im_neg'], ops['al_re'], ops['al_im'], ops['d'], *consts)


def _glu_kernel(y_ref, wa_ref, wb_ref, ba_ref, bb_ref, gate_ref, o_ref):
    y = y_ref[...]
    ga = _dot(y, wa_ref[...]) + ba_ref[...]
    gb = _dot(y, wb_ref[...]) + bb_ref[...]
    o_ref[...] = ((ga * jax.nn.sigmoid(gb)) * _silu(gate_ref[...])).astype(o_ref.dtype)


def _glu(y, w_glu, layer, b_glu, proj, tm=1024, tn=512):
    m, k = y.shape
    width = w_glu.shape[2] // 2
    tm, tn = min(tm, m), min(tn, width)
    nt = width // tn
    w_tiles = _tile_cols(w_glu, layer, tn)
    return pl.pallas_call(
        _glu_kernel,
        grid=(m // tm, nt),
        in_specs=[
            pl.BlockSpec((tm, k), lambda i, j: (i, 0)),
            pl.BlockSpec((None, k, tn), lambda i, j: (j, 0, 0)),
            pl.BlockSpec((None, k, tn), lambda i, j: (nt + j, 0, 0)),
            pl.BlockSpec((1, tn), lambda i, j: (0, j)),
            pl.BlockSpec((1, tn), lambda i, j: (0, nt + j)),
            pl.BlockSpec((tm, tn), lambda i, j: (i, nt + j)),
        ],
        out_specs=pl.BlockSpec((tm, tn), lambda i, j: (i, j)),
        out_shape=jax.ShapeDtypeStruct((m, width), BF16),
        compiler_params=_params(2),
        name="ssm_glu",
    )(y, w_tiles, w_tiles, b_glu, b_glu, proj)


def _ssm_mixer(x, g, w_in, layer, lam_re, lam_im, log_dt, b_re, b_im, c_re, c_im, d_skip, w_glu, b_glu, bsz, seq):
    proj = _norm_matmul(x, g, _tile_cols(w_in, layer, MATMUL_TN), F32)
    ops = _ssm_operators(lam_re, lam_im, log_dt, b_re, b_im, c_re, c_im, d_skip)
    y = _ssm_core(proj, ops, bsz, seq)
    return _glu(y, w_glu, layer, b_glu.astype(F32).reshape(1, -1), proj)


CUM_BLOCK = 128


def _split3(x):
    x1 = x.astype(BF16)
    r1 = x - x1.astype(F32)
    x2 = r1.astype(BF16)
    x3 = (r1 - x2.astype(F32)).astype(BF16)
    return x1, x2, x3


def _forget_cumsum_kernel(z_ref, b_ref, o_ref):
    n_blocks = z_ref.shape[0] // CUM_BLOCK
    ri = lax.broadcasted_iota(jnp.int32, (CUM_BLOCK, CUM_BLOCK), 0)
    ci = lax.broadcasted_iota(jnp.int32, (CUM_BLOCK, CUM_BLOCK), 1)
    tri = jnp.where(ci <= ri, 1.0, 0.0).astype(BF16)

    def body(i, carry):
        rows = pl.ds(pl.multiple_of(i * CUM_BLOCK, CUM_BLOCK), CUM_BLOCK)
        z = z_ref[rows, :] + b_ref[...]
        log_f = jnp.minimum(z, 0.0) - jnp.log1p(jnp.exp(-jnp.abs(z)))
        x1, x2, x3 = _split3(log_f)
        c = _dot(tri, x1) + _dot(tri, x2) + _dot(tri, x3) + carry
        o_ref[rows, :] = c
        return c[CUM_BLOCK - 1:, :]

    lax.fori_loop(0, n_blocks, body, jnp.zeros((1, z_ref.shape[1]), F32))


def _forget_cumsum(z, b, bsz, seq):
    lanes = z.shape[1]
    return pl.pallas_call(
        _forget_cumsum_kernel,
        grid=(bsz,),
        in_specs=[pl.BlockSpec((seq, lanes), lambda i: (i, 0)), pl.BlockSpec((1, lanes), lambda i: (0, 0))],
        out_specs=pl.BlockSpec((seq, lanes), lambda i: (i, 0)),
        out_shape=jax.ShapeDtypeStruct(z.shape, F32),
        compiler_params=_params(1),
        name="forget_cumsum",
    )(z, b)


FOX_STEP_HEADS = 8


def _fox_kernel(q_ref, k_ref, v_ref, sg_ref, cq_ref, ck_ref, o_ref, ka_ref, vt_ref, *head_refs, tk):
    nh = FOX_STEP_HEADS
    hg, qi = pl.program_id(1), pl.program_id(2)
    tq = q_ref.shape[0]
    qa_ref, s_ref, p_ref, acc_ref, pv_ref, m_ref = (head_refs[i::6] for i in range(6))
    lane = lax.broadcasted_iota(jnp.int32, (1, LANES), 1)
    free = (HEAD_DIM, 0)
    n_pieces = 3
    pair_lanes = lambda h: slice(h // 2 * LANES, (h // 2 + 1) * LANES)

    def bias_lanes(c, h, c_offset, one_offset):
        rr = lax.broadcasted_iota(jnp.int32, (n_pieces * LANES, LANES), 0)
        cc = lax.broadcasted_iota(jnp.int32, (n_pieces * LANES, LANES), 1)
        place = jnp.logical_and(rr % LANES == nh * hg + h, cc == free[h % 2] + c_offset + rr // LANES)
        placed = _dot(jnp.concatenate(_split3(c), axis=1), jnp.where(place, 1.0, 0.0).astype(BF16))
        first = free[h % 2] + one_offset
        return (placed + jnp.where(jnp.logical_and(lane >= first, lane < first + n_pieces), 1.0, 0.0)).astype(BF16)

    @pl.when(qi == 0)
    def _():
        def chunk(j, carry):
            rows = pl.ds(pl.multiple_of(j * tk, tk), tk)
            neg_c = ck_ref[rows, :] * -LOG2E
            for h in range(nh):
                keys = k_ref[rows, pair_lanes(h)]
                ka_ref[h, j] = jnp.where((lane // HEAD_DIM) == h % 2, keys, bias_lanes(neg_c, h, 0, n_pieces))
            for pair in range(nh // 2):
                v_t = v_ref[rows, pair_lanes(2 * pair)].astype(F32).T.astype(BF16)
                for hh in range(2):
                    vt_ref[j, 2 * pair + hh, :HEAD_DIM, :] = v_t[hh * HEAD_DIM:(hh + 1) * HEAD_DIM]
                    vt_ref[j, 2 * pair + hh, HEAD_DIM:, :] = jnp.ones((ONES_ROWS, tk), BF16)
            return carry
        lax.fori_loop(0, k_ref.shape[0] // tk, chunk, 0)

    cq = cq_ref[...] * LOG2E
    for h in range(nh):
        qa = jnp.where((lane // HEAD_DIM) == h % 2, q_ref[:, pair_lanes(h)], bias_lanes(cq, h, n_pieces, 0))
        qa_ref[h][...] = qa.astype(F32).T.astype(BF16)

    def scores(h, j):
        s_ref[h][...] = _dot(ka_ref[h, j], qa_ref[h][...])

    rows8 = 8
    rows16 = 16

    def absorb(h, j, diagonal):
        def strip(r, n):
            blk = s_ref[h][r:r + n, :]
            if diagonal:
                kpos = lax.broadcasted_iota(jnp.int32, (n, tq), 0)
                qpos = lax.broadcasted_iota(jnp.int32, (n, tq), 1)
                blk = jnp.where(kpos + r <= qpos, blk, NEG)
            return blk
        top = strip(0, rows8)
        for r in range(rows8, tk, rows8):
            top = jnp.maximum(top, strip(r, rows8))
        m = m_ref[h][...]
        m_new = jnp.maximum(m, top.max(axis=0, keepdims=True))
        m_ref[h][...] = m_new
        alpha = jnp.exp2(m - m_new)
        m_rows = jnp.broadcast_to(m_new, (rows16, tq))
        for r in range(0, tk, rows16):
            p_ref[h][r:r + rows16, :] = jnp.exp2((strip(r, rows16) - m_rows).astype(BF16))
        acc_ref[h][...] = alpha * (acc_ref[h][...] + pv_ref[h][...])

    def values(h, j):
        pv_ref[h][...] = _dot(vt_ref[j, h], p_ref[h][...])

    def step(j, diagonal):
        values(nh - 1, jnp.maximum(j - 1, 0))
        scores(1, j)
        for h in range(nh):
            absorb(h, j, diagonal)
            if h < nh - 1:
                values(h, j)
            if h + 2 < nh:
                scores(h + 2, j)
            elif h + 2 == nh and not diagonal:
                scores(0, j + 1)

    for h in range(nh):
        acc_ref[h][...] = jnp.zeros(acc_ref[h].shape, F32)
        pv_ref[h][...] = jnp.zeros(pv_ref[h].shape, F32)
        m_ref[h][...] = jnp.full(m_ref[h].shape, NEG, F32)
    p_ref[nh - 1][...] = jnp.zeros(p_ref[nh - 1].shape, BF16)
    scores(0, 0)

    def two_blocks(i, carry):
        step(2 * i, False)
        step(2 * i + 1, False)
        return carry
    lax.fori_loop(0, qi // 2, two_blocks, 0)

    @pl.when(qi % 2 == 0)
    def _():
        step(qi, True)

    @pl.when(qi % 2 == 1)
    def _():
        step(qi - 1, False)
        step(qi, True)

    values(nh - 1, qi)
    outs = []
    for h in range(nh):
        total = acc_ref[h][...] + pv_ref[h][...]
        outs.append(total[:HEAD_DIM] / total[HEAD_DIM:HEAD_DIM + 1])
    out = jnp.concatenate(outs, axis=0)
    o_ref[...] = (out.T * sg_ref[...].astype(F32)).astype(o_ref.dtype)


def _fox_attention(proj, csum, bsz, seq, t=512):
    width = FOX_HEADS * HEAD_DIM
    nh = FOX_STEP_HEADS
    gw = nh * HEAD_DIM
    n_groups = width // gw
    t = min(t, seq)
    nq = seq // t
    return pl.pallas_call(
        functools.partial(_fox_kernel, tk=t),
        grid=(bsz, n_groups, nq),
        in_specs=[
            pl.BlockSpec((t, gw), lambda b, h, i: (b * nq + i, h)),
            pl.BlockSpec((seq, gw), lambda b, h, i: (b, n_groups + h), pipeline_mode=pl.Buffered(1)),
            pl.BlockSpec((seq, gw), lambda b, h, i: (b, 2 * n_groups + h), pipeline_mode=pl.Buffered(1)),
            pl.BlockSpec((t, gw), lambda b, h, i: (b * nq + i, 3 * n_groups + h)),
            pl.BlockSpec((t, LANES), lambda b, h, i: (b * nq + i, 0)),
            pl.BlockSpec((seq, LANES), lambda b, h, i: (b, 0), pipeline_mode=pl.Buffered(1)),
        ],
        out_specs=pl.BlockSpec((t, gw), lambda b, h, i: (b * nq + i, h)),
        out_shape=jax.ShapeDtypeStruct((bsz * seq, width), BF16),
        scratch_shapes=[
            pltpu.VMEM((nh, nq, t, LANES), BF16), pltpu.VMEM((nq, nh, HEAD_DIM + ONES_ROWS, t), BF16),
        ] + nh * [pltpu.VMEM((LANES, t), BF16), pltpu.VMEM((t, t), F32), pltpu.VMEM((t, t), BF16),
                  pltpu.VMEM((HEAD_DIM + ONES_ROWS, t), F32), pltpu.VMEM((HEAD_DIM + ONES_ROWS, t), F32),
                  pltpu.VMEM((1, t), F32)],
        compiler_params=_params(3, VMEM_LIMIT_BIG),
        name="fox_attention",
    )(proj, proj, proj, proj, csum, csum)


def _fox_mixer(x, g, w_in, layer, w_fg, b_fg, bsz, seq, t=512):
    width = FOX_HEADS * HEAD_DIM
    col_scale = jnp.where(jnp.arange(4 * width) < width, LOG2E * HEAD_DIM ** -0.5, 1.0).astype(F32)
    proj = _norm_matmul(x, g, _tile_cols(w_in, layer, MATMUL_TN, col_scale), BF16, silu_cols=(3 * width, 4 * width))
    pad = LANES - FOX_HEADS
    z = _norm_matmul(x, g, _tile_cols_xla(jnp.pad(w_fg, ((0, 0), (0, pad))), LANES), F32)
    csum = _forget_cumsum(z, jnp.pad(b_fg.astype(F32), (0, pad)).reshape(1, LANES), bsz, seq)
    return _fox_attention(proj, csum, bsz, seq, t)


def kernel(x, p, norm_g, final_g, rel_bias, swa_w_in, swa_w_out, swa_sinks, conv_w_in, conv_kernel, conv_w_out, ssm_w_in, ssm_lam_re, ssm_lam_im, ssm_log_dt, ssm_b_re, ssm_b_im, ssm_c_re, ssm_c_im, ssm_d, ssm_w_glu, ssm_b_glu, ssm_w_out, fox_w_in, fox_w_fg, fox_b_fg, fox_w_out, ple_proj, ple_norm, ple_gate):
    bsz, seq, d_model = x.shape
    depth = p.shape[0]
    h = x.astype(F32).reshape(bsz * seq, d_model)
    for i in range(depth):
        mixer, j = i % N_MIXERS, i // N_MIXERS
        if mixer == 0:
            a, w_out = _swa_mixer(h, norm_g[i], swa_w_in, j, swa_sinks[j], rel_bias, bsz, seq), swa_w_out[j]
        elif mixer == 1:
            a, w_out = _conv_mixer(h, norm_g[i], conv_w_in, j, conv_kernel[j], seq), conv_w_out[j]
        elif mixer == 2:
            a = _ssm_mixer(h, norm_g[i], ssm_w_in, j, ssm_lam_re[j], ssm_lam_im[j], ssm_log_dt[j], ssm_b_re[j], ssm_b_im[j],
                           ssm_c_re[j], ssm_c_im[j], ssm_d[j], ssm_w_glu, ssm_b_glu[j], bsz, seq)
            w_out = ssm_w_out[j]
        else:
            a, w_out = _fox_mixer(h, norm_g[i], fox_w_in, j, fox_w_fg[j], fox_b_fg[j], bsz, seq), fox_w_out[j]
        h = _out_ple(a, w_out, h, ple_norm[i], ple_gate, p.reshape(depth, bsz * seq, -1), ple_proj, i,
                     final_g=final_g if i == depth - 1 else None)
    return h.reshape(bsz, seq, d_model).astype(x.dtype)
```

```python
import functools
import math

import numpy as np
import jax
import jax.numpy as jnp
from jax import lax
from jax.experimental import pallas as pl
from jax.experimental.pallas import tpu as pltpu

F32 = jnp.float32
BF16 = jnp.bfloat16

EPS = 1e-6
N_MIXERS = 4

SWA_HEADS = 32
SWA_KV_HEADS = 4
SWA_GROUP = SWA_HEADS // SWA_KV_HEADS
HEAD_DIM = 64
SWA_BLOCK = 128
WINDOW = 128
REL_BUCKETS = 32
REL_MAX_DIST = 128

CONV_TAPS = 3

SSM_GROUP = 16
SSM_STATE = 64
SSM_CHUNK = 16

FOX_HEADS = 32

LANES = 128
ONES_ROWS = 16
GROUPS_PER_TILE = LANES // SSM_GROUP
VMEM_LIMIT = 48 * 1024 * 1024
VMEM_LIMIT_BIG = 60 * 1024 * 1024

NEG = float(jnp.finfo(jnp.float32).min)
LOG2E = math.log2(math.e)


def _params(n_axes, vmem_limit=VMEM_LIMIT):
    return pltpu.CompilerParams(dimension_semantics=("arbitrary",) * n_axes, vmem_limit_bytes=vmem_limit)


def _dot(a, b):
    return jnp.dot(a, b, preferred_element_type=F32)


def _dot_nt(a, b):
    return lax.dot_general(a, b, (((1,), (1,)), ((), ())), preferred_element_type=F32)


def _rmsnorm_rows(x, g):
    return x * lax.rsqrt(jnp.mean(x * x, axis=-1, keepdims=True) + EPS) * g


def _silu(x):
    return x * jax.nn.sigmoid(x)


ROW_CHUNK = 64


def _for_row_chunks(n_rows, fn):
    def body(c, carry):
        fn(pl.ds(pl.multiple_of(c * ROW_CHUNK, ROW_CHUNK), ROW_CHUNK))
        return carry
    lax.fori_loop(0, n_rows // ROW_CHUNK, body, 0)


def _tile_cols_xla(w, tn):
    k, n = w.shape
    return w.astype(BF16).reshape(k, n // tn, tn).transpose(1, 0, 2)


def _cast_tiles_kernel(w_ref, o_ref):
    def chunk(rows):
        o_ref[rows, :] = w_ref[rows, :].astype(BF16)
    _for_row_chunks(o_ref.shape[0], chunk)


def _scale_cast_tiles_kernel(w_ref, s_ref, o_ref):
    def chunk(rows):
        o_ref[rows, :] = (w_ref[rows, :] * s_ref[...]).astype(BF16)
    _for_row_chunks(o_ref.shape[0], chunk)


def _tile_cols(w, layer, tn, col_scale=None):
    _, k, n = w.shape
    tn = min(tn, n)
    w_spec = pl.BlockSpec((None, k, tn), lambda j: (layer, 0, j))
    common = dict(
        grid=(n // tn,),
        out_specs=pl.BlockSpec((None, k, tn), lambda j: (j, 0, 0)),
        out_shape=jax.ShapeDtypeStruct((n // tn, k, tn), BF16),
        compiler_params=_params(1),
        name="weight_tiles",
    )
    if col_scale is None:
        return pl.pallas_call(_cast_tiles_kernel, in_specs=[w_spec], **common)(w)
    s_spec = pl.BlockSpec((1, tn), lambda j: (0, j))
    return pl.pallas_call(_scale_cast_tiles_kernel, in_specs=[w_spec, s_spec], **common)(w, col_scale.reshape(1, n))


def _lookahead_x_spec(tm, k, n_row_tiles):
    def tile(i, j):
        first_step = jnp.logical_and(i == 0, j == 0)
        return jnp.where(first_step, 0, jnp.minimum(i + 1, n_row_tiles - 1))
    return pl.BlockSpec((tm, k), lambda i, j: (tile(i, j), 0))


def _lookahead_rows(tm, nt):
    return -(-tm // ((nt - 1) * ROW_CHUNK)) * ROW_CHUNK


def _lookahead_first(x_ref, g_ref, a_ref):
    @pl.when(jnp.logical_and(pl.program_id(0) == 0, pl.program_id(1) == 0))
    def _():
        def chunk(rows):
            a_ref[0, rows, :] = _rmsnorm_rows(x_ref[rows, :], g_ref[...]).astype(BF16)
        _for_row_chunks(a_ref.shape[1], chunk)


def _lookahead_next(x_ref, g_ref, a_ref, rows_per_step):
    i, j = pl.program_id(0), pl.program_id(1)
    tm = a_ref.shape[1]
    start = jnp.clip((j - 1) * rows_per_step, 0, tm - rows_per_step)
    for c in range(rows_per_step // ROW_CHUNK):
        rows = pl.ds(pl.multiple_of(start + c * ROW_CHUNK, ROW_CHUNK), ROW_CHUNK)
        a_ref[(i + 1) % 2, rows, :] = _rmsnorm_rows(x_ref[rows, :], g_ref[...]).astype(BF16)


def _norm_matmul_kernel(x_ref, g_ref, w_ref, o_ref, a_ref, *, silu_tiles, rows_per_step):
    i, j = pl.program_id(0), pl.program_id(1)
    if rows_per_step is None:
        def chunk(rows):
            a_ref[0, rows, :] = _rmsnorm_rows(x_ref[rows, :], g_ref[...]).astype(BF16)
        _for_row_chunks(a_ref.shape[1], chunk)
        acc = _dot(a_ref[0], w_ref[...])
    else:
        _lookahead_first(x_ref, g_ref, a_ref)
        acc = _dot(a_ref[i % 2], w_ref[...])
        _lookahead_next(x_ref, g_ref, a_ref, rows_per_step)
    if silu_tiles is None:
        o_ref[...] = acc.astype(o_ref.dtype)
    else:
        gated = jnp.logical_and(j >= silu_tiles[0], j < silu_tiles[1])

        @pl.when(gated)
        def _():
            o_ref[...] = _silu(acc).astype(o_ref.dtype)

        @pl.when(jnp.logical_not(gated))
        def _():
            o_ref[...] = acc.astype(o_ref.dtype)


MATMUL_TN = 1024


def _norm_matmul(x, g, w_tiles, out_dtype, silu_cols=None, tm=1024):
    m, k = x.shape
    nt, _, tn = w_tiles.shape
    n = nt * tn
    tm = min(tm, m)
    n_row_tiles = m // tm
    silu_tiles = None if silu_cols is None else (silu_cols[0] // tn, silu_cols[1] // tn)
    lookahead = nt > 1
    return pl.pallas_call(
        functools.partial(_norm_matmul_kernel, silu_tiles=silu_tiles,
                          rows_per_step=_lookahead_rows(tm, nt) if lookahead else None),
        grid=(n_row_tiles, nt),
        in_specs=[
            _lookahead_x_spec(tm, k, n_row_tiles) if lookahead else pl.BlockSpec((tm, k), lambda i, j: (i, 0)),
            pl.BlockSpec((1, k), lambda i, j: (0, 0)),
            pl.BlockSpec((None, k, tn), lambda i, j: (j, 0, 0)),
        ],
        out_specs=pl.BlockSpec((tm, tn), lambda i, j: (i, j)),
        out_shape=jax.ShapeDtypeStruct((m, n), out_dtype),
        scratch_shapes=[pltpu.VMEM((2, tm, k), BF16)],
        compiler_params=_params(2, VMEM_LIMIT_BIG),
        name="norm_matmul",
    )(x, g.reshape(1, k), w_tiles)


def _out_ple_kernel(a_ref, wo_ref, x_ref, g_ref, wg_ref, p_ref, wp_ref, gf_ref, o_ref, x1_ref, hn_ref, ss_ref, *, nt, final):
    j = pl.program_id(1)
    tn = x_ref.shape[1]
    n = nt * tn

    @pl.when(j == 0)
    def _():
        ss_ref[...] = jnp.zeros(ss_ref.shape, F32)

    @pl.when(j < nt)
    def _():
        x1 = x_ref[...] + _dot(a_ref[...], wo_ref[j])
        x1_ref[j] = x1
        hn_ref[j] = (x1 * g_ref[j]).astype(BF16)
        ss_ref[0] += jnp.sum(x1 * x1, axis=-1, keepdims=True)

    @pl.when(j >= nt)
    def _():
        t = j - nt
        emb = _dot(p_ref[...].astype(BF16), wp_ref[t])
        acc = _dot(hn_ref[0], wg_ref[t, :tn, :])
        for kt in range(1, nt):
            acc = acc + _dot(hn_ref[kt], wg_ref[t, kt * tn:(kt + 1) * tn, :])
        x2 = x1_ref[t] + emb * jax.nn.sigmoid(acc * lax.rsqrt(ss_ref[0] / n + EPS))
        if final:
            x1_ref[t] = x2
            ss_ref[1] += jnp.sum(x2 * x2, axis=-1, keepdims=True)
        else:
            o_ref[...] = x2

    if final:
        @pl.when(j == 2 * nt - 1)
        def _():
            inv = lax.rsqrt(ss_ref[1] / n + EPS)
            for t in range(nt):
                o_ref[:, t * tn:(t + 1) * tn] = x1_ref[t] * inv * gf_ref[t]


def _out_ple(a, w_out, x, g, w_gate, p, w_proj, layer, final_g=None, tm=1024, tn=512):
    m, k = a.shape
    n = w_out.shape[1]
    pd = p.shape[2]
    final = final_g is not None
    tm, tn = min(tm, m), min(tn, n)
    nt = n // tn
    if final:
        out_spec = pl.BlockSpec((tm, n), lambda i, j: (i, 0), pipeline_mode=pl.Buffered(1))
    else:
        out_spec = pl.BlockSpec((tm, tn), lambda i, j: (i, jnp.maximum(j - nt, 0)))
    resident = lambda rows: pl.BlockSpec((nt, rows, tn), lambda i, j: (0, 0, 0), pipeline_mode=pl.Buffered(1))
    row_vec = pl.BlockSpec((nt, 1, tn), lambda i, j: (0, 0, 0))
    return pl.pallas_call(
        functools.partial(_out_ple_kernel, nt=nt, final=final),
        grid=(m // tm, 2 * nt),
        in_specs=[
            pl.BlockSpec((tm, k), lambda i, j: (i, 0)),
            resident(k),
            pl.BlockSpec((tm, tn), lambda i, j: (i, jnp.minimum(j, nt - 1))),
            row_vec,
            resident(n),
            pl.BlockSpec((None, tm, pd), lambda i, j: (layer, i, 0)),
            resident(pd),
            row_vec,
        ],
        out_specs=out_spec,
        out_shape=jax.ShapeDtypeStruct((m, n), F32),
        scratch_shapes=[pltpu.VMEM((nt, tm, tn), F32), pltpu.VMEM((nt, tm, tn), BF16), pltpu.VMEM((2, tm, 1), F32)],
        compiler_params=_params(2, VMEM_LIMIT_BIG),
        name="out_ple_final" if final else "out_ple",
    )(a, _tile_cols(w_out[None], 0, tn), x, g.reshape(nt, 1, tn), _tile_cols(w_gate, layer, tn), p,
      _tile_cols(w_proj, layer, tn), (final_g if final else g).reshape(nt, 1, tn))


def _t5_bucket(dist):
    max_exact = REL_BUCKETS // 2
    d = np.maximum(dist, 1).astype(np.float32)
    large = max_exact + (np.log(d / max_exact) / np.log(REL_MAX_DIST / max_exact) * (REL_BUCKETS - max_exact)).astype(np.int32)
    large = np.minimum(large, REL_BUCKETS - 1)
    return np.where(dist < max_exact, dist, large).astype(np.int32)


assert WINDOW == SWA_BLOCK


def _swa_bucket_table():
    qi = np.arange(SWA_BLOCK)[None, :]
    kj = np.arange(2 * SWA_BLOCK)[:, None]
    return _t5_bucket(np.clip(qi + SWA_BLOCK - kj, 0, None))


def _swa_kernel(sink_ref, rel_ref, bucket_ref, q_ref, sg_ref, kp_ref, kc_ref, vp_ref, vc_ref, o_ref, bias_ref, qs_ref):
    n = pl.program_id(1)
    blk = SWA_BLOCK
    lane = lax.broadcasted_iota(jnp.int32, (1, LANES), 1)

    @pl.when(jnp.logical_and(pl.program_id(0) == 0, n == 0))
    def _():
        bucket = bucket_ref[...]
        kj = lax.broadcasted_iota(jnp.int32, (2 * blk, blk), 0)
        qi = lax.broadcasted_iota(jnp.int32, (2 * blk, blk), 1)
        band = jnp.logical_or(jnp.logical_and(kj < blk, kj > qi), jnp.logical_and(kj >= blk, kj - blk <= qi))

        def per_head(head, carry):
            acc = jnp.zeros((2 * blk, blk), F32)
            for b in range(REL_BUCKETS):
                acc = jnp.where(bucket == b, rel_ref[b, head] * LOG2E, acc)
            bias_ref[head] = jnp.where(band, acc, NEG)
            return carry
        lax.fori_loop(0, SWA_HEADS, per_head, 0)

    for kvh in range(SWA_KV_HEADS):
        cols = slice(kvh * LANES, (kvh + 1) * LANES)
        heads = range(kvh * SWA_GROUP, (kvh + 1) * SWA_GROUP)
        for g, head in enumerate(heads):
            q_pair = q_ref[:, head // 2 * LANES:(head // 2 + 1) * LANES]
            qs_ref[g * blk:(g + 1) * blk, :] = jnp.where((lane // HEAD_DIM) == head % 2, q_pair, jnp.zeros_like(q_pair))
        keys = jnp.concatenate([kp_ref[:, cols], kc_ref[:, cols]], axis=0)
        vals = jnp.concatenate([vp_ref[:, cols], vc_ref[:, cols]], axis=0)
        s = _dot_nt(keys, qs_ref[...]) + jnp.concatenate([bias_ref[head] for head in heads], axis=1)
        s_prev = jnp.where(n > 0, s[:blk], NEG)
        s_cur = s[blk:]
        sink = jnp.concatenate([jnp.full((1, blk), sink_ref[head] * LOG2E, F32) for head in heads], axis=1)
        m = jnp.maximum(jnp.maximum(s_prev, s_cur).max(axis=0, keepdims=True), sink)
        e = jnp.concatenate([jnp.exp2(s_prev - m).astype(BF16), jnp.exp2(s_cur - m).astype(BF16)], axis=0)
        v_t = jnp.concatenate([vals.astype(F32).T.astype(BF16)[:HEAD_DIM], jnp.ones((ONES_ROWS, 2 * blk), BF16)], axis=0)
        out = _dot(v_t, e)
        out = out[:HEAD_DIM] / (out[HEAD_DIM:HEAD_DIM + 1] + jnp.exp2(sink - m))
        for pair in range(SWA_GROUP // 2):
            head = kvh * SWA_GROUP + 2 * pair
            even = out[:, 2 * pair * blk:(2 * pair + 1) * blk]
            odd = out[:, (2 * pair + 1) * blk:(2 * pair + 2) * blk]
            cols = slice(head // 2 * LANES, (head // 2 + 1) * LANES)
            o_ref[:, cols] = (jnp.concatenate([even, odd], axis=0).T * sg_ref[:, cols].astype(F32)).astype(o_ref.dtype)


def _swa_attention(proj, sinks, rel_bias, bsz, seq):
    nb = seq // SWA_BLOCK
    width = SWA_HEADS * HEAD_DIM
    kvw = SWA_KV_HEADS * LANES
    k_blk = 2 * width // kvw
    cur = lambda b, n: b * nb + n
    prev = lambda b, n: b * nb + jnp.maximum(n - 1, 0)
    smem = pl.BlockSpec(memory_space=pltpu.SMEM)
    return pl.pallas_call(
        _swa_kernel,
        grid=(bsz, nb),
        in_specs=[
            smem, smem,
            pl.BlockSpec((2 * SWA_BLOCK, SWA_BLOCK), lambda b, n: (0, 0)),
            pl.BlockSpec((SWA_BLOCK, width), lambda b, n: (cur(b, n), 0)),
            pl.BlockSpec((SWA_BLOCK, width), lambda b, n: (cur(b, n), 1)),
            pl.BlockSpec((SWA_BLOCK, kvw), lambda b, n: (prev(b, n), k_blk)),
            pl.BlockSpec((SWA_BLOCK, kvw), lambda b, n: (cur(b, n), k_blk)),
            pl.BlockSpec((SWA_BLOCK, kvw), lambda b, n: (prev(b, n), k_blk + 1)),
            pl.BlockSpec((SWA_BLOCK, kvw), lambda b, n: (cur(b, n), k_blk + 1)),
        ],
        out_specs=pl.BlockSpec((SWA_BLOCK, width), lambda b, n: (cur(b, n), 0)),
        out_shape=jax.ShapeDtypeStruct((bsz * seq, width), BF16),
        scratch_shapes=[
            pltpu.VMEM((SWA_HEADS, 2 * SWA_BLOCK, SWA_BLOCK), F32),
            pltpu.VMEM((SWA_GROUP * SWA_BLOCK, LANES), BF16),
        ],
        compiler_params=_params(2),
        name="swa_attention",
    )(sinks.astype(F32), rel_bias.astype(F32), jnp.asarray(_swa_bucket_table()), proj, proj, proj, proj, proj, proj)


SWA_HALF_TILE = 512


def _swa_weight_tiles_kernel(wl_ref, wr_ref, o_ref, *, q_tiles, gate_tiles, q_scale):
    j = pl.program_id(0)
    half = SWA_HALF_TILE
    kvw = SWA_KV_HEADS * HEAD_DIM

    def rows_do(fn):
        def chunk(rows):
            for side, w_ref in enumerate((wl_ref, wr_ref)):
                o_ref[rows, side * half:(side + 1) * half] = fn(side, w_ref, rows)
        _for_row_chunks(o_ref.shape[0], chunk)

    @pl.when(j < q_tiles)
    def _():
        rows_do(lambda side, w_ref, rows: (w_ref[rows, :] * q_scale).astype(BF16))

    @pl.when(jnp.logical_and(j >= q_tiles, j < q_tiles + gate_tiles))
    def _():
        rows_do(lambda side, w_ref, rows: w_ref[rows, :].astype(BF16))

    @pl.when(j == q_tiles + gate_tiles)
    def _():
        def twice(side, w_ref, rows):
            w = w_ref[rows, side * kvw:(side + 1) * kvw].astype(BF16)
            heads = [w[:, h * HEAD_DIM:(h + 1) * HEAD_DIM] for h in range(SWA_KV_HEADS)]
            return jnp.concatenate([piece for head in heads for piece in (head, head)], axis=1)
        rows_do(twice)


def _swa_weight_tiles(w_in, layer, q_scale):
    _, k, n = w_in.shape
    half = SWA_HALF_TILE
    width = SWA_HEADS * HEAD_DIM
    q_tiles = gate_tiles = width // (2 * half)
    kv_block = width // half
    n_out = q_tiles + gate_tiles + 1

    def src(j, side):
        in_q, in_gate = j < q_tiles, j < q_tiles + gate_tiles
        return jnp.where(in_q, 2 * j + side, jnp.where(in_gate, 2 * j + side + 1, kv_block))
    return pl.pallas_call(
        functools.partial(_swa_weight_tiles_kernel, q_tiles=q_tiles, gate_tiles=gate_tiles, q_scale=q_scale),
        grid=(n_out,),
        in_specs=[pl.BlockSpec((None, k, half), lambda j: (layer, 0, src(j, 0))),
                  pl.BlockSpec((None, k, half), lambda j: (layer, 0, src(j, 1)))],
        out_specs=pl.BlockSpec((None, k, 2 * half), lambda j: (j, 0, 0)),
        out_shape=jax.ShapeDtypeStruct((n_out, k, 2 * half), BF16),
        compiler_params=_params(1),
        name="swa_weight_tiles",
    )(w_in, w_in)


def _swa_mixer(x, g, w_in, layer, sinks, rel_bias, bsz, seq):
    width = SWA_HEADS * HEAD_DIM
    w_tiles = _swa_weight_tiles(w_in, layer, LOG2E * HEAD_DIM ** -0.5)
    proj = _norm_matmul(x, g, w_tiles, BF16, silu_cols=(width, 2 * width))
    return _swa_attention(proj, sinks, rel_bias, bsz, seq)


HALO = 8


def _conv_kernel(x_ref, g_ref, wb_ref, wc_ref, wu_ref, wg_ref, ck_ref, o_ref, a_ref, z_ref, carry_ref, *,
                 tiles_per_seq, rows_per_step):
    i, j = pl.program_id(0), pl.program_id(1)
    tm = a_ref.shape[1]
    _lookahead_first(x_ref, g_ref, a_ref)
    a = a_ref[i % 2]
    z = _dot(a, wc_ref[...]) * _dot(a, wu_ref[...])
    first = (i % tiles_per_seq) == 0

    @pl.when(first)
    def _():
        z_ref[:HALO, :] = jnp.zeros((HALO, z_ref.shape[1]), F32)

    @pl.when(jnp.logical_not(first))
    def _():
        z_ref[:HALO, :] = carry_ref[j]

    z_ref[HALO:, :] = z
    carry_ref[j] = z[tm - HALO:, :]
    conv = z_ref[HALO - 2:HALO - 2 + tm, :] * ck_ref[0:1, :]
    conv = conv + z_ref[HALO - 1:HALO - 1 + tm, :] * ck_ref[1:2, :]
    conv = conv + z * ck_ref[2:3, :]
    y = _dot(a, wb_ref[...]) * conv
    o_ref[...] = (y * _silu(_dot(a, wg_ref[...]))).astype(o_ref.dtype)
    _lookahead_next(x_ref, g_ref, a_ref, rows_per_step)


def _conv_mixer(x, g, w_in, layer, conv_kernel, seq, tm=1024, tn=512):
    m, k = x.shape
    width = w_in.shape[2] // 4
    tm, tn = min(tm, seq), min(tn, width)
    nt = width // tn
    w_spec = lambda q: pl.BlockSpec((None, k, tn), lambda i, j: (q * nt + j, 0, 0))
    w_in = _tile_cols(w_in, layer, tn)
    return pl.pallas_call(
        functools.partial(_conv_kernel, tiles_per_seq=seq // tm, rows_per_step=_lookahead_rows(tm, nt)),
        grid=(m // tm, nt),
        in_specs=[
            _lookahead_x_spec(tm, k, m // tm),
            pl.BlockSpec((1, k), lambda i, j: (0, 0)),
            w_spec(0), w_spec(1), w_spec(2), w_spec(3),
            pl.BlockSpec((CONV_TAPS, tn), lambda i, j: (0, j)),
        ],
        out_specs=pl.BlockSpec((tm, tn), lambda i, j: (i, j)),
        out_shape=jax.ShapeDtypeStruct((m, width), BF16),
        scratch_shapes=[
            pltpu.VMEM((2, tm, k), BF16),
            pltpu.VMEM((HALO + tm, tn), F32),
            pltpu.VMEM((nt, HALO, tn), F32),
        ],
        compiler_params=_params(2, VMEM_LIMIT_BIG),
        name="conv_mixer",
    )(x, g.reshape(1, k), w_in, w_in, w_in, w_in, conv_kernel.astype(F32))


def _ssm_operators(lam_re, lam_im, log_dt, b_re, b_im, c_re, c_im, d_skip):
    n_groups = lam_re.shape[0]
    n_oct = n_groups // GROUPS_PER_TILE
    L, C, N = SSM_CHUNK, SSM_GROUP, SSM_STATE
    dt = jnp.exp(log_dt.astype(F32))[None, :]
    lr, li = lam_re.astype(F32).T, lam_im.astype(F32).T
    mag = jnp.exp(lr * dt)
    ab_re, ab_im = mag * jnp.cos(li * dt), mag * jnp.sin(li * dt)
    den = lr * lr + li * li
    nr = ab_re - 1.0
    coef_re = (nr * lr + ab_im * li) / den
    coef_im = (ab_im * lr - nr * li) / den
    br, bi = b_re.astype(F32).transpose(2, 1, 0), b_im.astype(F32).transpose(2, 1, 0)
    bb_re = coef_re * br - coef_im * bi
    bb_im = coef_re * bi + coef_im * br
    cr, ci = c_re.astype(F32).transpose(1, 2, 0), c_im.astype(F32).transpose(1, 2, 0)
    pw_re, pw_im = [jnp.ones_like(ab_re)], [jnp.zeros_like(ab_im)]
    for _ in range(L):
        pw_re, pw_im = (pw_re + [pw_re[-1] * ab_re - pw_im[-1] * ab_im], pw_im + [pw_re[-1] * ab_im + pw_im[-1] * ab_re])
    p_re, p_im = jnp.stack(pw_re), jnp.stack(pw_im)
    w_re = cr[None] * p_re[:L, None] - ci[None] * p_im[:L, None]
    w_im = cr[None] * p_im[:L, None] + ci[None] * p_re[:L, None]
    lag_k = jnp.sum(w_re[:, None] * bb_re[None, :, None] - w_im[:, None] * bb_im[None, :, None], axis=3)
    q_re, q_im = jnp.stack(pw_re[L - 1::-1]), jnp.stack(pw_im[L - 1::-1])
    bc_re = q_re[:, None] * bb_re[None] - q_im[:, None] * bb_im[None]
    bc_im = q_re[:, None] * bb_im[None] + q_im[:, None] * bb_re[None]
    o_re = cr[None] * p_re[1:, None] - ci[None] * p_im[1:, None]
    o_im = cr[None] * p_im[1:, None] + ci[None] * p_re[1:, None]
    octs = lambda t: t.reshape(t.shape[:-1] + (n_oct, GROUPS_PER_TILE))
    to_lag = lambda t: octs(t).transpose(3, 0, 1, 4, 2).reshape(n_oct, L, C, LANES)
    to_rows = lambda t: octs(t).transpose(3, 0, 1, 4, 2).reshape(n_oct, L * C, GROUPS_PER_TILE * N)
    to_cols = lambda t: octs(t).transpose(3, 4, 2, 0, 1).reshape(n_oct, GROUPS_PER_TILE * N, L * C)
    to_vec = lambda t: octs(t).transpose(1, 2, 0).reshape(n_oct, 1, GROUPS_PER_TILE * N)
    return dict(
        lag=to_lag(lag_k).astype(BF16),
        bc_re=to_rows(bc_re).astype(BF16), bc_im=to_rows(bc_im).astype(BF16),
        oc_re=to_cols(o_re).astype(BF16), oc_im_neg=to_cols(-o_im).astype(BF16),
        al_re=to_vec(p_re[L]), al_im=to_vec(p_im[L]), d=d_skip.astype(F32).reshape(n_oct, 1, LANES),
    )


def _ssm_expanders():
    L, C, N, G8 = SSM_CHUNK, SSM_GROUP, SSM_STATE, GROUPS_PER_TILE
    wide = np.arange(L * LANES)
    w_step, w_group, w_chan = wide // LANES, (wide // C) % G8, wide % C
    small = np.arange(L * C)
    s_step, s_chan = small // C, small % C
    state_group = np.arange(G8 * N) // N
    spread = (w_step[:, None] == s_step[None, :]) & (w_chan[:, None] == s_chan[None, :])
    lane = np.arange(LANES)
    as_bf16 = lambda a: jnp.asarray(a.astype(np.float32), dtype=BF16)
    return dict(
        spread_cols=as_bf16(spread.T),
        keep_cols=as_bf16(state_group[:, None] == w_group[None, :]),
        spread_lag=as_bf16((lane % C)[:, None] == np.arange(C)[None, :]),
        keep_lag=jnp.asarray(((lane // C)[:, None] == (lane // C)[None, :]).astype(np.float32)),
    )


def _ssm_kernel(u_ref, lag_ref, bxr_ref, bxi_ref, cxr_ref, cxi_ref, alr_ref, ali_ref, d_ref,
                sc_ref, kc_ref, sl_ref, kl_ref, o_ref,
                ub_ref, bcr_ref, bci_ref, ocr_ref, oci_ref, zr_ref, zi_ref, hr_ref, hi_ref, y_ref, *, bsz):
    L = SSM_CHUNK
    rows = u_ref.shape[0] // L
    chunks = rows // bsz
    tile = 2 * LANES
    step_rows = lambda r: pl.ds(r, rows, stride=L)
    for r in range(L):
        ub_ref[:, r * LANES:(r + 1) * LANES] = u_ref[step_rows(r), :].astype(BF16)
    state_group = lax.broadcasted_iota(jnp.int32, (1, bxr_ref.shape[1]), 1) // SSM_STATE
    for compact_ref, full_ref in ((bxr_ref, bcr_ref), (bxi_ref, bci_ref)):
        for r in range(L):
            blk = compact_ref[r * SSM_GROUP:(r + 1) * SSM_GROUP, :]
            for grp in range(GROUPS_PER_TILE):
                first = r * LANES + grp * SSM_GROUP
                full_ref[first:first + SSM_GROUP, :] = jnp.where(state_group == grp, blk, jnp.zeros_like(blk))
    zr_ref[...] = _dot(ub_ref[...], bcr_ref[...])
    zi_ref[...] = _dot(ub_ref[...], bci_ref[...])
    a_re, a_im = alr_ref[...], ali_ref[...]

    def step(k, carry):
        new = []
        for b in range(bsz):
            h_re, h_im = carry[2 * b], carry[2 * b + 1]
            row = pl.ds(b * chunks + k, 1)
            hr_ref[row, :] = h_re
            hi_ref[row, :] = h_im
            new.append(a_re * h_re - a_im * h_im + zr_ref[row, :])
            new.append(a_re * h_im + a_im * h_re + zi_ref[row, :])
        return tuple(new)

    zero = jnp.zeros((1, a_re.shape[1]), F32)
    lax.fori_loop(0, chunks, step, (zero,) * (2 * bsz), unroll=True)

    ocr_ref[...] = (_dot(cxr_ref[...], sc_ref[...]) * kc_ref[...]).astype(BF16)
    oci_ref[...] = (_dot(cxi_ref[...], sc_ref[...]) * kc_ref[...]).astype(BF16)
    lag = [(_dot(sl_ref[...], lag_ref[d]) * kl_ref[...]).astype(BF16) for d in range(L)]
    lag_tile = []
    for dd in range(L // 2):
        below = lag[2 * dd - 1] if dd > 0 else jnp.zeros((LANES, LANES), BF16)
        lag_tile.append(jnp.concatenate([jnp.concatenate([lag[2 * dd], lag[2 * dd + 1]], axis=1),
                                         jnp.concatenate([below, lag[2 * dd]], axis=1)], axis=0))
    hb_re, hb_im = hr_ref[...].astype(BF16), hi_ref[...].astype(BF16)
    for t2 in range(L // 2):
        cols = slice(t2 * tile, (t2 + 1) * tile)
        y = _dot(hb_re, ocr_ref[:, cols]) + _dot(hb_im, oci_ref[:, cols])
        for r2 in range(t2 + 1):
            y = y + _dot(ub_ref[:, r2 * tile:(r2 + 1) * tile], lag_tile[t2 - r2])
        for half in range(2):
            t = 2 * t2 + half
            y_ref[step_rows(t), :] = jax.nn.gelu(y[:, half * LANES:(half + 1) * LANES] + d_ref[...] * u_ref[step_rows(t), :])

    def chunk(rows):
        o_ref[rows, :] = y_ref[rows, :].astype(o_ref.dtype)
    _for_row_chunks(o_ref.shape[0], chunk)


def _ssm_core(proj, ops, bsz, seq):
    m = proj.shape[0]
    width = proj.shape[1] // 2
    L, C = SSM_CHUNK, SSM_GROUP
    n_oct = width // LANES
    rows = m // L
    ow = L * LANES
    n_state = GROUPS_PER_TILE * SSM_STATE
    ex = _ssm_expanders()
    per_oct = lambda *shape: pl.BlockSpec((None,) + shape, lambda p: (p,) + (0,) * len(shape))
    const = lambda a: pl.BlockSpec(a.shape, lambda p: (0,) * a.ndim, pipeline_mode=pl.Buffered(1))
    consts = [ex['spread_cols'], ex['keep_cols'], ex['spread_lag'], ex['keep_lag']]
    return pl.pallas_call(
        functools.partial(_ssm_kernel, bsz=bsz),
        grid=(n_oct,),
        in_specs=[
            pl.BlockSpec((m, LANES), lambda p: (0, p)),
            per_oct(L, C, LANES), per_oct(L * C, n_state), per_oct(L * C, n_state),
            per_oct(n_state, L * C), per_oct(n_state, L * C), per_oct(1, n_state), per_oct(1, n_state), per_oct(1, LANES),
        ] + [const(a) for a in consts],
        out_specs=pl.BlockSpec((m, LANES), lambda p: (0, p)),
        out_shape=jax.ShapeDtypeStruct((m, width), BF16),
        scratch_shapes=[pltpu.VMEM((rows, ow), BF16)] + [pltpu.VMEM((ow, n_state), BF16)] * 2
        + [pltpu.VMEM((n_state, ow), BF16)] * 2 + [pltpu.VMEM((rows, n_state), F32)] * 4 + [pltpu.VMEM((m, LANES), F32)],
        compiler_params=_params(1),
        name="ssm_core",
    )(proj, ops['lag'], ops['bc_re'], ops['bc_im'], ops['oc_re'], ops['oc_im_neg'], ops['al_re'], ops['al_im'], ops['d'], *consts)


def _glu_kernel(y_ref, wa_ref, wb_ref, ba_ref, bb_ref, gate_ref, o_ref):
    y = y_ref[...]
    ga = _dot(y, wa_ref[...]) + ba_ref[...]
    gb = _dot(y, wb_ref[...]) + bb_ref[...]
    o_ref[...] = ((ga * jax.nn.sigmoid(gb)) * _silu(gate_ref[...])).astype(o_ref.dtype)


def _glu(y, w_glu, layer, b_glu, proj, tm=1024, tn=512):
    m, k = y.shape
    width = w_glu.shape[2] // 2
    tm, tn = min(tm, m), min(tn, width)
    nt = width // tn
    w_tiles = _tile_cols(w_glu, layer, tn)
    return pl.pallas_call(
        _glu_kernel,
        grid=(m // tm, nt),
        in_specs=[
            pl.BlockSpec((tm, k), lambda i, j: (i, 0)),
            pl.BlockSpec((None, k, tn), lambda i, j: (j, 0, 0)),
            pl.BlockSpec((None, k, tn), lambda i, j: (nt + j, 0, 0)),
            pl.BlockSpec((1, tn), lambda i, j: (0, j)),
            pl.BlockSpec((1, tn), lambda i, j: (0, nt + j)),
            pl.BlockSpec((tm, tn), lambda i, j: (i, nt + j)),
        ],
        out_specs=pl.BlockSpec((tm, tn), lambda i, j: (i, j)),
        out_shape=jax.ShapeDtypeStruct((m, width), BF16),
        compiler_params=_params(2),
        name="ssm_glu",
    )(y, w_tiles, w_tiles, b_glu, b_glu, proj)


def _ssm_mixer(x, g, w_in, layer, lam_re, lam_im, log_dt, b_re, b_im, c_re, c_im, d_skip, w_glu, b_glu, bsz, seq):
    proj = _norm_matmul(x, g, _tile_cols(w_in, layer, MATMUL_TN), F32)
    ops = _ssm_operators(lam_re, lam_im, log_dt, b_re, b_im, c_re, c_im, d_skip)
    y = _ssm_core(proj, ops, bsz, seq)
    return _glu(y, w_glu, layer, b_glu.astype(F32).reshape(1, -1), proj)


CUM_BLOCK = 128


def _split3(x):
    x1 = x.astype(BF16)
    r1 = x - x1.astype(F32)
    x2 = r1.astype(BF16)
    x3 = (r1 - x2.astype(F32)).astype(BF16)
    return x1, x2, x3


def _forget_cumsum_kernel(z_ref, b_ref, o_ref):
    n_blocks = z_ref.shape[0] // CUM_BLOCK
    ri = lax.broadcasted_iota(jnp.int32, (CUM_BLOCK, CUM_BLOCK), 0)
    ci = lax.broadcasted_iota(jnp.int32, (CUM_BLOCK, CUM_BLOCK), 1)
    tri = jnp.where(ci <= ri, 1.0, 0.0).astype(BF16)

    def body(i, carry):
        rows = pl.ds(pl.multiple_of(i * CUM_BLOCK, CUM_BLOCK), CUM_BLOCK)
        z = z_ref[rows, :] + b_ref[...]
        log_f = jnp.minimum(z, 0.0) - jnp.log1p(jnp.exp(-jnp.abs(z)))
        x1, x2, x3 = _split3(log_f)
        c = _dot(tri, x1) + _dot(tri, x2) + _dot(tri, x3) + carry
        o_ref[rows, :] = c
        return c[CUM_BLOCK - 1:, :]

    lax.fori_loop(0, n_blocks, body, jnp.zeros((1, z_ref.shape[1]), F32))


def _forget_cumsum(z, b, bsz, seq):
    lanes = z.shape[1]
    return pl.pallas_call(
        _forget_cumsum_kernel,
        grid=(bsz,),
        in_specs=[pl.BlockSpec((seq, lanes), lambda i: (i, 0)), pl.BlockSpec((1, lanes), lambda i: (0, 0))],
        out_specs=pl.BlockSpec((seq, lanes), lambda i: (i, 0)),
        out_shape=jax.ShapeDtypeStruct(z.shape, F32),
        compiler_params=_params(1),
        name="forget_cumsum",
    )(z, b)


FOX_STEP_HEADS = 8


def _fox_kernel(q_ref, k_ref, v_ref, sg_ref, cq_ref, ck_ref, o_ref, ka_ref, vt_ref, *head_refs, tk):
    nh = FOX_STEP_HEADS
    hg, qi = pl.program_id(1), pl.program_id(2)
    tq = q_ref.shape[0]
    qa_ref, s_ref, p_ref, acc_ref, pv_ref, m_ref = (head_refs[i::6] for i in range(6))
    lane = lax.broadcasted_iota(jnp.int32, (1, LANES), 1)
    free = (HEAD_DIM, 0)
    n_pieces = 3
    pair_lanes = lambda h: slice(h // 2 * LANES, (h // 2 + 1) * LANES)

    def bias_lanes(c, h, c_offset, one_offset):
        rr = lax.broadcasted_iota(jnp.int32, (n_pieces * LANES, LANES), 0)
        cc = lax.broadcasted_iota(jnp.int32, (n_pieces * LANES, LANES), 1)
        place = jnp.logical_and(rr % LANES == nh * hg + h, cc == free[h % 2] + c_offset + rr // LANES)
        placed = _dot(jnp.concatenate(_split3(c), axis=1), jnp.where(place, 1.0, 0.0).astype(BF16))
        first = free[h % 2] + one_offset
        return (placed + jnp.where(jnp.logical_and(lane >= first, lane < first + n_pieces), 1.0, 0.0)).astype(BF16)

    @pl.when(qi == 0)
    def _():
        def chunk(j, carry):
            rows = pl.ds(pl.multiple_of(j * tk, tk), tk)
            neg_c = ck_ref[rows, :] * -LOG2E
            for h in range(nh):
                keys = k_ref[rows, pair_lanes(h)]
                ka_ref[h, j] = jnp.where((lane // HEAD_DIM) == h % 2, keys, bias_lanes(neg_c, h, 0, n_pieces))
            for pair in range(nh // 2):
                v_t = v_ref[rows, pair_lanes(2 * pair)].astype(F32).T.astype(BF16)
                for hh in range(2):
                    vt_ref[j, 2 * pair + hh, :HEAD_DIM, :] = v_t[hh * HEAD_DIM:(hh + 1) * HEAD_DIM]
                    vt_ref[j, 2 * pair + hh, HEAD_DIM:, :] = jnp.ones((ONES_ROWS, tk), BF16)
            return carry
        lax.fori_loop(0, k_ref.shape[0] // tk, chunk, 0)

    cq = cq_ref[...] * LOG2E
    for h in range(nh):
        qa = jnp.where((lane // HEAD_DIM) == h % 2, q_ref[:, pair_lanes(h)], bias_lanes(cq, h, n_pieces, 0))
        qa_ref[h][...] = qa.astype(F32).T.astype(BF16)

    def scores(h, j):
        s_ref[h][...] = _dot(ka_ref[h, j], qa_ref[h][...])

    rows8 = 8
    rows16 = 16

    def absorb(h, j, diagonal):
        def strip(r, n):
            blk = s_ref[h][r:r + n, :]
            if diagonal:
                kpos = lax.broadcasted_iota(jnp.int32, (n, tq), 0)
                qpos = lax.broadcasted_iota(jnp.int32, (n, tq), 1)
                blk = jnp.where(kpos + r <= qpos, blk, NEG)
            return blk
        top = strip(0, rows8)
        for r in range(rows8, tk, rows8):
            top = jnp.maximum(top, strip(r, rows8))
        m = m_ref[h][...]
        m_new = jnp.maximum(m, top.max(axis=0, keepdims=True))
        m_ref[h][...] = m_new
        alpha = jnp.exp2(m - m_new)
        m_rows = jnp.broadcast_to(m_new, (rows16, tq))
        for r in range(0, tk, rows16):
            p_ref[h][r:r + rows16, :] = jnp.exp2(strip(r, rows16) - m_rows).astype(BF16)
        acc_ref[h][...] = alpha * (acc_ref[h][...] + pv_ref[h][...])

    def values(h, j):
        pv_ref[h][...] = _dot(vt_ref[j, h], p_ref[h][...])

    def step(j, diagonal):
        values(nh - 1, jnp.maximum(j - 1, 0))
        scores(1, j)
        for h in range(nh):
            absorb(h, j, diagonal)
            if h < nh - 1:
                values(h, j)
            if h + 2 < nh:
                scores(h + 2, j)
            elif h + 2 == nh and not diagonal:
                scores(0, j + 1)

    for h in range(nh):
        acc_ref[h][...] = jnp.zeros(acc_ref[h].shape, F32)
        pv_ref[h][...] = jnp.zeros(pv_ref[h].shape, F32)
        m_ref[h][...] = jnp.full(m_ref[h].shape, NEG, F32)
    p_ref[nh - 1][...] = jnp.zeros(p_ref[nh - 1].shape, BF16)
    scores(0, 0)

    def two_blocks(i, carry):
        step(2 * i, False)
        step(2 * i + 1, False)
        return carry
    lax.fori_loop(0, qi // 2, two_blocks, 0)

    @pl.when(qi % 2 == 0)
    def _():
        step(qi, True)

    @pl.when(qi % 2 == 1)
    def _():
        step(qi - 1, False)
        step(qi, True)

    values(nh - 1, qi)
    outs = []
    for h in range(nh):
        total = acc_ref[h][...] + pv_ref[h][...]
        outs.append(total[:HEAD_DIM] / total[HEAD_DIM:HEAD_DIM + 1])
    out = jnp.concatenate(outs, axis=0)
    o_ref[...] = (out.T * sg_ref[...].astype(F32)).astype(o_ref.dtype)


def _fox_attention(proj, csum, bsz, seq, t=512):
    width = FOX_HEADS * HEAD_DIM
    nh = FOX_STEP_HEADS
    gw = nh * HEAD_DIM
    n_groups = width // gw
    t = min(t, seq)
    nq = seq // t
    return pl.pallas_call(
        functools.partial(_fox_kernel, tk=t),
        grid=(bsz, n_groups, nq),
        in_specs=[
            pl.BlockSpec((t, gw), lambda b, h, i: (b * nq + i, h)),
            pl.BlockSpec((seq, gw), lambda b, h, i: (b, n_groups + h), pipeline_mode=pl.Buffered(1)),
            pl.BlockSpec((seq, gw), lambda b, h, i: (b, 2 * n_groups + h), pipeline_mode=pl.Buffered(1)),
            pl.BlockSpec((t, gw), lambda b, h, i: (b * nq + i, 3 * n_groups + h)),
            pl.BlockSpec((t, LANES), lambda b, h, i: (b * nq + i, 0)),
            pl.BlockSpec((seq, LANES), lambda b, h, i: (b, 0), pipeline_mode=pl.Buffered(1)),
        ],
        out_specs=pl.BlockSpec((t, gw), lambda b, h, i: (b * nq + i, h)),
        out_shape=jax.ShapeDtypeStruct((bsz * seq, width), BF16),
        scratch_shapes=[
            pltpu.VMEM((nh, nq, t, LANES), BF16), pltpu.VMEM((nq, nh, HEAD_DIM + ONES_ROWS, t), BF16),
        ] + nh * [pltpu.VMEM((LANES, t), BF16), pltpu.VMEM((t, t), F32), pltpu.VMEM((t, t), BF16),
                  pltpu.VMEM((HEAD_DIM + ONES_ROWS, t), F32), pltpu.VMEM((HEAD_DIM + ONES_ROWS, t), F32),
                  pltpu.VMEM((1, t), F32)],
        compiler_params=_params(3, VMEM_LIMIT_BIG),
        name="fox_attention",
    )(proj, proj, proj, proj, csum, csum)


def _fox_mixer(x, g, w_in, layer, w_fg, b_fg, bsz, seq, t=512):
    width = FOX_HEADS * HEAD_DIM
    col_scale = jnp.where(jnp.arange(4 * width) < width, LOG2E * HEAD_DIM ** -0.5, 1.0).astype(F32)
    proj = _norm_matmul(x, g, _tile_cols(w_in, layer, 2 * MATMUL_TN, col_scale), BF16, silu_cols=(3 * width, 4 * width))
    pad = LANES - FOX_HEADS
    z = _norm_matmul(x, g, _tile_cols_xla(jnp.pad(w_fg, ((0, 0), (0, pad))), LANES), F32)
    csum = _forget_cumsum(z, jnp.pad(b_fg.astype(F32), (0, pad)).reshape(1, LANES), bsz, seq)
    return _fox_attention(proj, csum, bsz, seq, t)


def kernel(x, p, norm_g, final_g, rel_bias, swa_w_in, swa_w_out, swa_sinks, conv_w_in, conv_kernel, conv_w_out, ssm_w_in, ssm_lam_re, ssm_lam_im, ssm_log_dt, ssm_b_re, ssm_b_im, ssm_c_re, ssm_c_im, ssm_d, ssm_w_glu, ssm_b_glu, ssm_w_out, fox_w_in, fox_w_fg, fox_b_fg, fox_w_out, ple_proj, ple_norm, ple_gate):
    bsz, seq, d_model = x.shape
    depth = p.shape[0]
    h = x.astype(F32).reshape(bsz * seq, d_model)
    for i in range(depth):
        mixer, j = i % N_MIXERS, i // N_MIXERS
        if mixer == 0:
            a, w_out = _swa_mixer(h, norm_g[i], swa_w_in, j, swa_sinks[j], rel_bias, bsz, seq), swa_w_out[j]
        elif mixer == 1:
            a, w_out = _conv_mixer(h, norm_g[i], conv_w_in, j, conv_kernel[j], seq), conv_w_out[j]
        elif mixer == 2:
            a = _ssm_mixer(h, norm_g[i], ssm_w_in, j, ssm_lam_re[j], ssm_lam_im[j], ssm_log_dt[j], ssm_b_re[j], ssm_b_im[j],
                           ssm_c_re[j], ssm_c_im[j], ssm_d[j], ssm_w_glu, ssm_b_glu[j], bsz, seq)
            w_out = ssm_w_out[j]
        else:
            a, w_out = _fox_mixer(h, norm_g[i], fox_w_in, j, fox_w_fg[j], fox_b_fg[j], bsz, seq), fox_w_out[j]
        h = _out_ple(a, w_out, h, ple_norm[i], ple_gate, p.reshape(depth, bsz * seq, -1), ple_proj, i,
                     final_g=final_g if i == depth - 1 else None)
    return h.reshape(bsz, seq, d_model).astype(x.dtype)
```

```python
import functools
import math

import numpy as np
import jax
import jax.numpy as jnp
from jax import lax
from jax.experimental import pallas as pl
from jax.experimental.pallas import tpu as pltpu

F32 = jnp.float32
BF16 = jnp.bfloat16

EPS = 1e-6
N_MIXERS = 4

SWA_HEADS = 32
SWA_KV_HEADS = 4
SWA_GROUP = SWA_HEADS // SWA_KV_HEADS
HEAD_DIM = 64
SWA_BLOCK = 128
WINDOW = 128
REL_BUCKETS = 32
REL_MAX_DIST = 128

CONV_TAPS = 3

SSM_GROUP = 16
SSM_STATE = 64
SSM_CHUNK = 16

FOX_HEADS = 32

LANES = 128
ONES_ROWS = 16
GROUPS_PER_TILE = LANES // SSM_GROUP
VMEM_LIMIT = 48 * 1024 * 1024
VMEM_LIMIT_BIG = 60 * 1024 * 1024

NEG = float(jnp.finfo(jnp.float32).min)
LOG2E = math.log2(math.e)


def _params(n_axes, vmem_limit=VMEM_LIMIT):
    return pltpu.CompilerParams(dimension_semantics=("arbitrary",) * n_axes, vmem_limit_bytes=vmem_limit)


def _dot(a, b):
    return jnp.dot(a, b, preferred_element_type=F32)


def _dot_nt(a, b):
    return lax.dot_general(a, b, (((1,), (1,)), ((), ())), preferred_element_type=F32)


def _rmsnorm_rows(x, g):
    return x * lax.rsqrt(jnp.mean(x * x, axis=-1, keepdims=True) + EPS) * g


def _silu(x):
    return x * jax.nn.sigmoid(x)


ROW_CHUNK = 64


def _for_row_chunks(n_rows, fn):
    def body(c, carry):
        fn(pl.ds(pl.multiple_of(c * ROW_CHUNK, ROW_CHUNK), ROW_CHUNK))
        return carry
    lax.fori_loop(0, n_rows // ROW_CHUNK, body, 0)


def _cast_tiles_kernel(w_ref, o_ref):
    def chunk(rows):
        o_ref[rows, :] = w_ref[rows, :].astype(BF16)
    _for_row_chunks(o_ref.shape[0], chunk)


def _scale_cast_tiles_kernel(w_ref, s_ref, o_ref):
    def chunk(rows):
        o_ref[rows, :] = (w_ref[rows, :] * s_ref[...]).astype(BF16)
    _for_row_chunks(o_ref.shape[0], chunk)


def _tile_cols(w, layer, tn, col_scale=None):
    _, k, n = w.shape
    tn = min(tn, n)
    w_spec = pl.BlockSpec((None, k, tn), lambda j: (layer, 0, j))
    common = dict(
        grid=(n // tn,),
        out_specs=pl.BlockSpec((None, k, tn), lambda j: (j, 0, 0)),
        out_shape=jax.ShapeDtypeStruct((n // tn, k, tn), BF16),
        compiler_params=_params(1),
        name="weight_tiles",
    )
    if col_scale is None:
        return pl.pallas_call(_cast_tiles_kernel, in_specs=[w_spec], **common)(w)
    s_spec = pl.BlockSpec((1, tn), lambda j: (0, j))
    return pl.pallas_call(_scale_cast_tiles_kernel, in_specs=[w_spec, s_spec], **common)(w, col_scale.reshape(1, n))


def _lookahead_x_spec(tm, k, n_row_tiles):
    def tile(i, j):
        first_step = jnp.logical_and(i == 0, j == 0)
        return jnp.where(first_step, 0, jnp.minimum(i + 1, n_row_tiles - 1))
    return pl.BlockSpec((tm, k), lambda i, j: (tile(i, j), 0))


def _lookahead_rows(tm, nt):
    return -(-tm // ((nt - 1) * ROW_CHUNK)) * ROW_CHUNK


def _lookahead_first(x_ref, g_ref, a_ref):
    @pl.when(jnp.logical_and(pl.program_id(0) == 0, pl.program_id(1) == 0))
    def _():
        def chunk(rows):
            a_ref[0, rows, :] = _rmsnorm_rows(x_ref[rows, :], g_ref[...]).astype(BF16)
        _for_row_chunks(a_ref.shape[1], chunk)


def _lookahead_next(x_ref, g_ref, a_ref, rows_per_step):
    i, j = pl.program_id(0), pl.program_id(1)
    tm = a_ref.shape[1]
    start = jnp.clip((j - 1) * rows_per_step, 0, tm - rows_per_step)
    for c in range(rows_per_step // ROW_CHUNK):
        rows = pl.ds(pl.multiple_of(start + c * ROW_CHUNK, ROW_CHUNK), ROW_CHUNK)
        a_ref[(i + 1) % 2, rows, :] = _rmsnorm_rows(x_ref[rows, :], g_ref[...]).astype(BF16)


def _norm_matmul_kernel(x_ref, g_ref, w_ref, *refs, silu_tiles, rows_per_step, side):
    if side:
        ws_ref, o_ref, os_ref, a_ref = refs
    else:
        o_ref, a_ref = refs
    i, j = pl.program_id(0), pl.program_id(1)
    if rows_per_step is None:
        def chunk(rows):
            a_ref[0, rows, :] = _rmsnorm_rows(x_ref[rows, :], g_ref[...]).astype(BF16)
        _for_row_chunks(a_ref.shape[1], chunk)
        acc = _dot(a_ref[0], w_ref[...])
    else:
        _lookahead_first(x_ref, g_ref, a_ref)
        acc = _dot(a_ref[i % 2], w_ref[...])
        _lookahead_next(x_ref, g_ref, a_ref, rows_per_step)
    if side:
        @pl.when(j == 0)
        def _():
            os_ref[...] = _dot(a_ref[i % 2], ws_ref[...])
    if silu_tiles is None:
        o_ref[...] = acc.astype(o_ref.dtype)
    else:
        gated = jnp.logical_and(j >= silu_tiles[0], j < silu_tiles[1])

        @pl.when(gated)
        def _():
            o_ref[...] = _silu(acc).astype(o_ref.dtype)

        @pl.when(jnp.logical_not(gated))
        def _():
            o_ref[...] = acc.astype(o_ref.dtype)


MATMUL_TN = 1024


def _norm_matmul(x, g, w_tiles, out_dtype, silu_cols=None, w_side=None, tm=1024):
    m, k = x.shape
    nt, _, tn = w_tiles.shape
    n = nt * tn
    tm = min(tm, m)
    n_row_tiles = m // tm
    silu_tiles = None if silu_cols is None else (silu_cols[0] // tn, silu_cols[1] // tn)
    lookahead = nt > 1
    side = w_side is not None
    assert lookahead or not side
    in_specs = [
        _lookahead_x_spec(tm, k, n_row_tiles) if lookahead else pl.BlockSpec((tm, k), lambda i, j: (i, 0)),
        pl.BlockSpec((1, k), lambda i, j: (0, 0)),
        pl.BlockSpec((None, k, tn), lambda i, j: (j, 0, 0)),
    ]
    out_specs = pl.BlockSpec((tm, tn), lambda i, j: (i, j))
    out_shape = jax.ShapeDtypeStruct((m, n), out_dtype)
    operands = (x, g.reshape(1, k), w_tiles)
    if side:
        ns = w_side.shape[1]
        in_specs.append(pl.BlockSpec((k, ns), lambda i, j: (0, 0)))
        out_specs = [out_specs, pl.BlockSpec((tm, ns), lambda i, j: (i, 0))]
        out_shape = [out_shape, jax.ShapeDtypeStruct((m, ns), F32)]
        operands += (w_side,)
    return pl.pallas_call(
        functools.partial(_norm_matmul_kernel, silu_tiles=silu_tiles, side=side,
                          rows_per_step=_lookahead_rows(tm, nt) if lookahead else None),
        grid=(n_row_tiles, nt),
        in_specs=in_specs,
        out_specs=out_specs,
        out_shape=out_shape,
        scratch_shapes=[pltpu.VMEM((2, tm, k), BF16)],
        compiler_params=_params(2, VMEM_LIMIT_BIG),
        name="norm_matmul",
    )(*operands)


def _out_ple_kernel(a_ref, wo_ref, x_ref, g_ref, wg_ref, p_ref, wp_ref, gf_ref, o_ref, x1_ref, hn_ref, ss_ref, *, nt, final):
    j = pl.program_id(1)
    tn = x_ref.shape[1]
    n = nt * tn

    @pl.when(j == 0)
    def _():
        ss_ref[...] = jnp.zeros(ss_ref.shape, F32)

    @pl.when(j < nt)
    def _():
        x1 = x_ref[...] + _dot(a_ref[...], wo_ref[j])
        x1_ref[j] = x1
        hn_ref[j] = (x1 * g_ref[j]).astype(BF16)
        ss_ref[0] += jnp.sum(x1 * x1, axis=-1, keepdims=True)

    @pl.when(j >= nt)
    def _():
        t = j - nt
        emb = _dot(p_ref[...].astype(BF16), wp_ref[t])
        acc = _dot(hn_ref[0], wg_ref[t, :tn, :])
        for kt in range(1, nt):
            acc = acc + _dot(hn_ref[kt], wg_ref[t, kt * tn:(kt + 1) * tn, :])
        x2 = x1_ref[t] + emb * jax.nn.sigmoid(acc * lax.rsqrt(ss_ref[0] / n + EPS))
        if final:
            x1_ref[t] = x2
            ss_ref[1] += jnp.sum(x2 * x2, axis=-1, keepdims=True)
        else:
            o_ref[...] = x2

    if final:
        @pl.when(j == 2 * nt - 1)
        def _():
            inv = lax.rsqrt(ss_ref[1] / n + EPS)
            for t in range(nt):
                o_ref[:, t * tn:(t + 1) * tn] = x1_ref[t] * inv * gf_ref[t]


def _out_ple(a, w_out, x, g, w_gate, p, w_proj, layer, final_g=None, tm=1024, tn=512):
    m, k = a.shape
    n = w_out.shape[1]
    pd = p.shape[2]
    final = final_g is not None
    tm, tn = min(tm, m), min(tn, n)
    nt = n // tn
    if final:
        out_spec = pl.BlockSpec((tm, n), lambda i, j: (i, 0), pipeline_mode=pl.Buffered(1))
    else:
        out_spec = pl.BlockSpec((tm, tn), lambda i, j: (i, jnp.maximum(j - nt, 0)))
    resident = lambda rows: pl.BlockSpec((nt, rows, tn), lambda i, j: (0, 0, 0), pipeline_mode=pl.Buffered(1))
    row_vec = pl.BlockSpec((nt, 1, tn), lambda i, j: (0, 0, 0))
    return pl.pallas_call(
        functools.partial(_out_ple_kernel, nt=nt, final=final),
        grid=(m // tm, 2 * nt),
        in_specs=[
            pl.BlockSpec((tm, k), lambda i, j: (i, 0)),
            resident(k),
            pl.BlockSpec((tm, tn), lambda i, j: (i, jnp.minimum(j, nt - 1))),
            row_vec,
            resident(n),
            pl.BlockSpec((None, tm, pd), lambda i, j: (layer, i, 0)),
            resident(pd),
            row_vec,
        ],
        out_specs=out_spec,
        out_shape=jax.ShapeDtypeStruct((m, n), F32),
        scratch_shapes=[pltpu.VMEM((nt, tm, tn), F32), pltpu.VMEM((nt, tm, tn), BF16), pltpu.VMEM((2, tm, 1), F32)],
        compiler_params=_params(2, VMEM_LIMIT_BIG),
        name="out_ple_final" if final else "out_ple",
    )(a, _tile_cols(w_out[None], 0, tn), x, g.reshape(nt, 1, tn), _tile_cols(w_gate, layer, tn), p,
      _tile_cols(w_proj, layer, tn), (final_g if final else g).reshape(nt, 1, tn))


def _t5_bucket(dist):
    max_exact = REL_BUCKETS // 2
    d = np.maximum(dist, 1).astype(np.float32)
    large = max_exact + (np.log(d / max_exact) / np.log(REL_MAX_DIST / max_exact) * (REL_BUCKETS - max_exact)).astype(np.int32)
    large = np.minimum(large, REL_BUCKETS - 1)
    return np.where(dist < max_exact, dist, large).astype(np.int32)


assert WINDOW == SWA_BLOCK


def _swa_bucket_table():
    qi = np.arange(SWA_BLOCK)[None, :]
    kj = np.arange(2 * SWA_BLOCK)[:, None]
    return _t5_bucket(np.clip(qi + SWA_BLOCK - kj, 0, None))


def _swa_kernel(sink_ref, rel_ref, bucket_ref, q_ref, sg_ref, kp_ref, kc_ref, vp_ref, vc_ref, o_ref, bias_ref, qs_ref):
    n = pl.program_id(1)
    blk = SWA_BLOCK
    lane = lax.broadcasted_iota(jnp.int32, (1, LANES), 1)

    @pl.when(jnp.logical_and(pl.program_id(0) == 0, n == 0))
    def _():
        bucket = bucket_ref[...]
        kj = lax.broadcasted_iota(jnp.int32, (2 * blk, blk), 0)
        qi = lax.broadcasted_iota(jnp.int32, (2 * blk, blk), 1)
        band = jnp.logical_or(jnp.logical_and(kj < blk, kj > qi), jnp.logical_and(kj >= blk, kj - blk <= qi))

        def per_head(head, carry):
            acc = jnp.zeros((2 * blk, blk), F32)
            for b in range(REL_BUCKETS):
                acc = jnp.where(bucket == b, rel_ref[b, head] * LOG2E, acc)
            bias_ref[head] = jnp.where(band, acc, NEG)
            return carry
        lax.fori_loop(0, SWA_HEADS, per_head, 0)

    for kvh in range(SWA_KV_HEADS):
        cols = slice(kvh * LANES, (kvh + 1) * LANES)
        heads = range(kvh * SWA_GROUP, (kvh + 1) * SWA_GROUP)
        for g, head in enumerate(heads):
            q_pair = q_ref[:, head // 2 * LANES:(head // 2 + 1) * LANES]
            qs_ref[g * blk:(g + 1) * blk, :] = jnp.where((lane // HEAD_DIM) == head % 2, q_pair, jnp.zeros_like(q_pair))
        keys = jnp.concatenate([kp_ref[:, cols], kc_ref[:, cols]], axis=0)
        vals = jnp.concatenate([vp_ref[:, cols], vc_ref[:, cols]], axis=0)
        s = _dot_nt(keys, qs_ref[...]) + jnp.concatenate([bias_ref[head] for head in heads], axis=1)
        s_prev = jnp.where(n > 0, s[:blk], NEG)
        s_cur = s[blk:]
        sink = jnp.concatenate([jnp.full((1, blk), sink_ref[head] * LOG2E, F32) for head in heads], axis=1)
        m = jnp.maximum(jnp.maximum(s_prev, s_cur).max(axis=0, keepdims=True), sink)
        e = jnp.concatenate([jnp.exp2(s_prev - m).astype(BF16), jnp.exp2(s_cur - m).astype(BF16)], axis=0)
        v_t = jnp.concatenate([vals.astype(F32).T.astype(BF16)[:HEAD_DIM], jnp.ones((ONES_ROWS, 2 * blk), BF16)], axis=0)
        out = _dot(v_t, e)
        out = out[:HEAD_DIM] / (out[HEAD_DIM:HEAD_DIM + 1] + jnp.exp2(sink - m))
        for pair in range(SWA_GROUP // 2):
            head = kvh * SWA_GROUP + 2 * pair
            even = out[:, 2 * pair * blk:(2 * pair + 1) * blk]
            odd = out[:, (2 * pair + 1) * blk:(2 * pair + 2) * blk]
            cols = slice(head // 2 * LANES, (head // 2 + 1) * LANES)
            o_ref[:, cols] = (jnp.concatenate([even, odd], axis=0).T * sg_ref[:, cols].astype(F32)).astype(o_ref.dtype)


def _swa_attention(proj, sinks, rel_bias, bsz, seq):
    nb = seq // SWA_BLOCK
    width = SWA_HEADS * HEAD_DIM
    kvw = SWA_KV_HEADS * LANES
    k_blk = 2 * width // kvw
    cur = lambda b, n: b * nb + n
    prev = lambda b, n: b * nb + jnp.maximum(n - 1, 0)
    smem = pl.BlockSpec(memory_space=pltpu.SMEM)
    return pl.pallas_call(
        _swa_kernel,
        grid=(bsz, nb),
        in_specs=[
            smem, smem,
            pl.BlockSpec((2 * SWA_BLOCK, SWA_BLOCK), lambda b, n: (0, 0)),
            pl.BlockSpec((SWA_BLOCK, width), lambda b, n: (cur(b, n), 0)),
            pl.BlockSpec((SWA_BLOCK, width), lambda b, n: (cur(b, n), 1)),
            pl.BlockSpec((SWA_BLOCK, kvw), lambda b, n: (prev(b, n), k_blk)),
            pl.BlockSpec((SWA_BLOCK, kvw), lambda b, n: (cur(b, n), k_blk)),
            pl.BlockSpec((SWA_BLOCK, kvw), lambda b, n: (prev(b, n), k_blk + 1)),
            pl.BlockSpec((SWA_BLOCK, kvw), lambda b, n: (cur(b, n), k_blk + 1)),
        ],
        out_specs=pl.BlockSpec((SWA_BLOCK, width), lambda b, n: (cur(b, n), 0)),
        out_shape=jax.ShapeDtypeStruct((bsz * seq, width), BF16),
        scratch_shapes=[
            pltpu.VMEM((SWA_HEADS, 2 * SWA_BLOCK, SWA_BLOCK), F32),
            pltpu.VMEM((SWA_GROUP * SWA_BLOCK, LANES), BF16),
        ],
        compiler_params=_params(2),
        name="swa_attention",
    )(sinks.astype(F32), rel_bias.astype(F32), jnp.asarray(_swa_bucket_table()), proj, proj, proj, proj, proj, proj)


SWA_HALF_TILE = 512


def _swa_weight_tiles_kernel(wl_ref, wr_ref, o_ref, *, q_tiles, gate_tiles, q_scale):
    j = pl.program_id(0)
    half = SWA_HALF_TILE
    kvw = SWA_KV_HEADS * HEAD_DIM

    def rows_do(fn):
        def chunk(rows):
            for side, w_ref in enumerate((wl_ref, wr_ref)):
                o_ref[rows, side * half:(side + 1) * half] = fn(side, w_ref, rows)
        _for_row_chunks(o_ref.shape[0], chunk)

    @pl.when(j < q_tiles)
    def _():
        rows_do(lambda side, w_ref, rows: (w_ref[rows, :] * q_scale).astype(BF16))

    @pl.when(jnp.logical_and(j >= q_tiles, j < q_tiles + gate_tiles))
    def _():
        rows_do(lambda side, w_ref, rows: w_ref[rows, :].astype(BF16))

    @pl.when(j == q_tiles + gate_tiles)
    def _():
        def twice(side, w_ref, rows):
            w = w_ref[rows, side * kvw:(side + 1) * kvw].astype(BF16)
            heads = [w[:, h * HEAD_DIM:(h + 1) * HEAD_DIM] for h in range(SWA_KV_HEADS)]
            return jnp.concatenate([piece for head in heads for piece in (head, head)], axis=1)
        rows_do(twice)


def _swa_weight_tiles(w_in, layer, q_scale):
    _, k, n = w_in.shape
    half = SWA_HALF_TILE
    width = SWA_HEADS * HEAD_DIM
    q_tiles = gate_tiles = width // (2 * half)
    kv_block = width // half
    n_out = q_tiles + gate_tiles + 1

    def src(j, side):
        in_q, in_gate = j < q_tiles, j < q_tiles + gate_tiles
        return jnp.where(in_q, 2 * j + side, jnp.where(in_gate, 2 * j + side + 1, kv_block))
    return pl.pallas_call(
        functools.partial(_swa_weight_tiles_kernel, q_tiles=q_tiles, gate_tiles=gate_tiles, q_scale=q_scale),
        grid=(n_out,),
        in_specs=[pl.BlockSpec((None, k, half), lambda j: (layer, 0, src(j, 0))),
                  pl.BlockSpec((None, k, half), lambda j: (layer, 0, src(j, 1)))],
        out_specs=pl.BlockSpec((None, k, 2 * half), lambda j: (j, 0, 0)),
        out_shape=jax.ShapeDtypeStruct((n_out, k, 2 * half), BF16),
        compiler_params=_params(1),
        name="swa_weight_tiles",
    )(w_in, w_in)


def _swa_mixer(x, g, w_in, layer, sinks, rel_bias, bsz, seq):
    width = SWA_HEADS * HEAD_DIM
    w_tiles = _swa_weight_tiles(w_in, layer, LOG2E * HEAD_DIM ** -0.5)
    proj = _norm_matmul(x, g, w_tiles, BF16, silu_cols=(width, 2 * width))
    return _swa_attention(proj, sinks, rel_bias, bsz, seq)


HALO = 8


def _conv_kernel(x_ref, g_ref, wb_ref, wc_ref, wu_ref, wg_ref, ck_ref, o_ref, a_ref, z_ref, carry_ref, *,
                 tiles_per_seq, rows_per_step):
    i, j = pl.program_id(0), pl.program_id(1)
    tm = a_ref.shape[1]
    _lookahead_first(x_ref, g_ref, a_ref)
    a = a_ref[i % 2]
    z = _dot(a, wc_ref[...]) * _dot(a, wu_ref[...])
    first = (i % tiles_per_seq) == 0

    @pl.when(first)
    def _():
        z_ref[:HALO, :] = jnp.zeros((HALO, z_ref.shape[1]), F32)

    @pl.when(jnp.logical_not(first))
    def _():
        z_ref[:HALO, :] = carry_ref[j]

    z_ref[HALO:, :] = z
    carry_ref[j] = z[tm - HALO:, :]
    conv = z_ref[HALO - 2:HALO - 2 + tm, :] * ck_ref[0:1, :]
    conv = conv + z_ref[HALO - 1:HALO - 1 + tm, :] * ck_ref[1:2, :]
    conv = conv + z * ck_ref[2:3, :]
    y = _dot(a, wb_ref[...]) * conv
    o_ref[...] = (y * _silu(_dot(a, wg_ref[...]))).astype(o_ref.dtype)
    _lookahead_next(x_ref, g_ref, a_ref, rows_per_step)


def _conv_mixer(x, g, w_in, layer, conv_kernel, seq, tm=1024, tn=512):
    m, k = x.shape
    width = w_in.shape[2] // 4
    tm, tn = min(tm, seq), min(tn, width)
    nt = width // tn
    w_spec = lambda q: pl.BlockSpec((None, k, tn), lambda i, j: (q * nt + j, 0, 0))
    w_in = _tile_cols(w_in, layer, tn)
    return pl.pallas_call(
        functools.partial(_conv_kernel, tiles_per_seq=seq // tm, rows_per_step=_lookahead_rows(tm, nt)),
        grid=(m // tm, nt),
        in_specs=[
            _lookahead_x_spec(tm, k, m // tm),
            pl.BlockSpec((1, k), lambda i, j: (0, 0)),
            w_spec(0), w_spec(1), w_spec(2), w_spec(3),
            pl.BlockSpec((CONV_TAPS, tn), lambda i, j: (0, j)),
        ],
        out_specs=pl.BlockSpec((tm, tn), lambda i, j: (i, j)),
        out_shape=jax.ShapeDtypeStruct((m, width), BF16),
        scratch_shapes=[
            pltpu.VMEM((2, tm, k), BF16),
            pltpu.VMEM((HALO + tm, tn), F32),
            pltpu.VMEM((nt, HALO, tn), F32),
        ],
        compiler_params=_params(2, VMEM_LIMIT_BIG),
        name="conv_mixer",
    )(x, g.reshape(1, k), w_in, w_in, w_in, w_in, conv_kernel.astype(F32))


def _ssm_operators(lam_re, lam_im, log_dt, b_re, b_im, c_re, c_im, d_skip):
    n_groups = lam_re.shape[0]
    n_oct = n_groups // GROUPS_PER_TILE
    L, C, N = SSM_CHUNK, SSM_GROUP, SSM_STATE
    dt = jnp.exp(log_dt.astype(F32))[None, :]
    lr, li = lam_re.astype(F32).T, lam_im.astype(F32).T
    mag = jnp.exp(lr * dt)
    ab_re, ab_im = mag * jnp.cos(li * dt), mag * jnp.sin(li * dt)
    den = lr * lr + li * li
    nr = ab_re - 1.0
    coef_re = (nr * lr + ab_im * li) / den
    coef_im = (ab_im * lr - nr * li) / den
    br, bi = b_re.astype(F32).transpose(2, 1, 0), b_im.astype(F32).transpose(2, 1, 0)
    bb_re = coef_re * br - coef_im * bi
    bb_im = coef_re * bi + coef_im * br
    cr, ci = c_re.astype(F32).transpose(1, 2, 0), c_im.astype(F32).transpose(1, 2, 0)
    pw_re, pw_im = [jnp.ones_like(ab_re)], [jnp.zeros_like(ab_im)]
    for _ in range(L):
        pw_re, pw_im = (pw_re + [pw_re[-1] * ab_re - pw_im[-1] * ab_im], pw_im + [pw_re[-1] * ab_im + pw_im[-1] * ab_re])
    p_re, p_im = jnp.stack(pw_re), jnp.stack(pw_im)
    w_re = cr[None] * p_re[:L, None] - ci[None] * p_im[:L, None]
    w_im = cr[None] * p_im[:L, None] + ci[None] * p_re[:L, None]
    lag_k = jnp.sum(w_re[:, None] * bb_re[None, :, None] - w_im[:, None] * bb_im[None, :, None], axis=3)
    q_re, q_im = jnp.stack(pw_re[L - 1::-1]), jnp.stack(pw_im[L - 1::-1])
    bc_re = q_re[:, None] * bb_re[None] - q_im[:, None] * bb_im[None]
    bc_im = q_re[:, None] * bb_im[None] + q_im[:, None] * bb_re[None]
    o_re = cr[None] * p_re[1:, None] - ci[None] * p_im[1:, None]
    o_im = cr[None] * p_im[1:, None] + ci[None] * p_re[1:, None]
    octs = lambda t: t.reshape(t.shape[:-1] + (n_oct, GROUPS_PER_TILE))
    to_lag = lambda t: octs(t).transpose(3, 0, 1, 4, 2).reshape(n_oct, L, C, LANES)
    to_rows = lambda t: octs(t).transpose(3, 0, 1, 4, 2).reshape(n_oct, L * C, GROUPS_PER_TILE * N)
    to_cols = lambda t: octs(t).transpose(3, 4, 2, 0, 1).reshape(n_oct, GROUPS_PER_TILE * N, L * C)
    to_vec = lambda t: octs(t).transpose(1, 2, 0).reshape(n_oct, 1, GROUPS_PER_TILE * N)
    return dict(
        lag=to_lag(lag_k).astype(BF16),
        bc_re=to_rows(bc_re).astype(BF16), bc_im=to_rows(bc_im).astype(BF16),
        oc_re=to_cols(o_re).astype(BF16), oc_im_neg=to_cols(-o_im).astype(BF16),
        al_re=to_vec(p_re[L]), al_im=to_vec(p_im[L]), d=d_skip.astype(F32).reshape(n_oct, 1, LANES),
    )


def _ssm_expanders():
    L, C, N, G8 = SSM_CHUNK, SSM_GROUP, SSM_STATE, GROUPS_PER_TILE
    wide = np.arange(L * LANES)
    w_step, w_group, w_chan = wide // LANES, (wide // C) % G8, wide % C
    small = np.arange(L * C)
    s_step, s_chan = small // C, small % C
    state_group = np.arange(G8 * N) // N
    spread = (w_step[:, None] == s_step[None, :]) & (w_chan[:, None] == s_chan[None, :])
    lane = np.arange(LANES)
    as_bf16 = lambda a: jnp.asarray(a.astype(np.float32), dtype=BF16)
    return dict(
        spread_cols=as_bf16(spread.T),
        keep_cols=as_bf16(state_group[:, None] == w_group[None, :]),
        spread_lag=as_bf16((lane % C)[:, None] == np.arange(C)[None, :]),
        keep_lag=jnp.asarray(((lane // C)[:, None] == (lane // C)[None, :]).astype(np.float32)),
    )


def _ssm_kernel(u_ref, lag_ref, bxr_ref, bxi_ref, cxr_ref, cxi_ref, alr_ref, ali_ref, d_ref,
                sc_ref, kc_ref, sl_ref, kl_ref, o_ref,
                ub_ref, bcr_ref, bci_ref, ocr_ref, oci_ref, zr_ref, zi_ref, hr_ref, hi_ref, y_ref, *, bsz):
    L = SSM_CHUNK
    rows = u_ref.shape[0] // L
    chunks = rows // bsz
    tile = 2 * LANES
    step_rows = lambda r: pl.ds(r, rows, stride=L)
    for r in range(L):
        ub_ref[:, r * LANES:(r + 1) * LANES] = u_ref[step_rows(r), :].astype(BF16)
    state_group = lax.broadcasted_iota(jnp.int32, (1, bxr_ref.shape[1]), 1) // SSM_STATE
    for compact_ref, full_ref in ((bxr_ref, bcr_ref), (bxi_ref, bci_ref)):
        for r in range(L):
            blk = compact_ref[r * SSM_GROUP:(r + 1) * SSM_GROUP, :]
            for grp in range(GROUPS_PER_TILE):
                first = r * LANES + grp * SSM_GROUP
                full_ref[first:first + SSM_GROUP, :] = jnp.where(state_group == grp, blk, jnp.zeros_like(blk))
    zr_ref[...] = _dot(ub_ref[...], bcr_ref[...])
    zi_ref[...] = _dot(ub_ref[...], bci_ref[...])
    a_re, a_im = alr_ref[...], ali_ref[...]

    def step(k, carry):
        new = []
        for b in range(bsz):
            h_re, h_im = carry[2 * b], carry[2 * b + 1]
            row = pl.ds(b * chunks + k, 1)
            hr_ref[row, :] = h_re
            hi_ref[row, :] = h_im
            new.append(a_re * h_re - a_im * h_im + zr_ref[row, :])
            new.append(a_re * h_im + a_im * h_re + zi_ref[row, :])
        return tuple(new)

    zero = jnp.zeros((1, a_re.shape[1]), F32)
    lax.fori_loop(0, chunks, step, (zero,) * (2 * bsz), unroll=True)

    ocr_ref[...] = (_dot(cxr_ref[...], sc_ref[...]) * kc_ref[...]).astype(BF16)
    oci_ref[...] = (_dot(cxi_ref[...], sc_ref[...]) * kc_ref[...]).astype(BF16)
    lag = [(_dot(sl_ref[...], lag_ref[d]) * kl_ref[...]).astype(BF16) for d in range(L)]
    lag_tile = []
    for dd in range(L // 2):
        below = lag[2 * dd - 1] if dd > 0 else jnp.zeros((LANES, LANES), BF16)
        lag_tile.append(jnp.concatenate([jnp.concatenate([lag[2 * dd], lag[2 * dd + 1]], axis=1),
                                         jnp.concatenate([below, lag[2 * dd]], axis=1)], axis=0))
    hb_re, hb_im = hr_ref[...].astype(BF16), hi_ref[...].astype(BF16)
    for t2 in range(L // 2):
        cols = slice(t2 * tile, (t2 + 1) * tile)
        y = _dot(hb_re, ocr_ref[:, cols]) + _dot(hb_im, oci_ref[:, cols])
        for r2 in range(t2 + 1):
            y = y + _dot(ub_ref[:, r2 * tile:(r2 + 1) * tile], lag_tile[t2 - r2])
        for half in range(2):
            t = 2 * t2 + half
            y_ref[step_rows(t), :] = jax.nn.gelu(y[:, half * LANES:(half + 1) * LANES] + d_ref[...] * u_ref[step_rows(t), :])

    def chunk(rows):
        o_ref[rows, :] = y_ref[rows, :].astype(o_ref.dtype)
    _for_row_chunks(o_ref.shape[0], chunk)


def _ssm_core(proj, ops, bsz, seq):
    m = proj.shape[0]
    width = proj.shape[1] // 2
    L, C = SSM_CHUNK, SSM_GROUP
    n_oct = width // LANES
    rows = m // L
    ow = L * LANES
    n_state = GROUPS_PER_TILE * SSM_STATE
    ex = _ssm_expanders()
    per_oct = lambda *shape: pl.BlockSpec((None,) + shape, lambda p: (p,) + (0,) * len(shape))
    const = lambda a: pl.BlockSpec(a.shape, lambda p: (0,) * a.ndim, pipeline_mode=pl.Buffered(1))
    consts = [ex['spread_cols'], ex['keep_cols'], ex['spread_lag'], ex['keep_lag']]
    return pl.pallas_call(
        functools.partial(_ssm_kernel, bsz=bsz),
        grid=(n_oct,),
        in_specs=[
            pl.BlockSpec((m, LANES), lambda p: (0, p)),
            per_oct(L, C, LANES), per_oct(L * C, n_state), per_oct(L * C, n_state),
            per_oct(n_state, L * C), per_oct(n_state, L * C), per_oct(1, n_state), per_oct(1, n_state), per_oct(1, LANES),
        ] + [const(a) for a in consts],
        out_specs=pl.BlockSpec((m, LANES), lambda p: (0, p)),
        out_shape=jax.ShapeDtypeStruct((m, width), BF16),
        scratch_shapes=[pltpu.VMEM((rows, ow), BF16)] + [pltpu.VMEM((ow, n_state), BF16)] * 2
        + [pltpu.VMEM((n_state, ow), BF16)] * 2 + [pltpu.VMEM((rows, n_state), F32)] * 4 + [pltpu.VMEM((m, LANES), F32)],
        compiler_params=_params(1),
        name="ssm_core",
    )(proj, ops['lag'], ops['bc_re'], ops['bc_im'], ops['oc_re'], ops['oc_im_neg'], ops['al_re'], ops['al_im'], ops['d'], *consts)


def _glu_kernel(y_ref, wa_ref, wb_ref, ba_ref, bb_ref, gate_ref, o_ref):
    y = y_ref[...]
    ga = _dot(y, wa_ref[...]) + ba_ref[...]
    gb = _dot(y, wb_ref[...]) + bb_ref[...]
    o_ref[...] = ((ga * jax.nn.sigmoid(gb)) * _silu(gate_ref[...])).astype(o_ref.dtype)


def _glu(y, w_glu, layer, b_glu, proj, tm=1024, tn=512):
    m, k = y.shape
    width = w_glu.shape[2] // 2
    tm, tn = min(tm, m), min(tn, width)
    nt = width // tn
    w_tiles = _tile_cols(w_glu, layer, tn)
    return pl.pallas_call(
        _glu_kernel,
        grid=(m // tm, nt),
        in_specs=[
            pl.BlockSpec((tm, k), lambda i, j: (i, 0)),
            pl.BlockSpec((None, k, tn), lambda i, j: (j, 0, 0)),
            pl.BlockSpec((None, k, tn), lambda i, j: (nt + j, 0, 0)),
            pl.BlockSpec((1, tn), lambda i, j: (0, j)),
            pl.BlockSpec((1, tn), lambda i, j: (0, nt + j)),
            pl.BlockSpec((tm, tn), lambda i, j: (i, nt + j)),
        ],
        out_specs=pl.BlockSpec((tm, tn), lambda i, j: (i, j)),
        out_shape=jax.ShapeDtypeStruct((m, width), BF16),
        compiler_params=_params(2),
        name="ssm_glu",
    )(y, w_tiles, w_tiles, b_glu, b_glu, proj)


def _ssm_mixer(x, g, w_in, layer, lam_re, lam_im, log_dt, b_re, b_im, c_re, c_im, d_skip, w_glu, b_glu, bsz, seq):
    proj = _norm_matmul(x, g, _tile_cols(w_in, layer, MATMUL_TN), F32)
    ops = _ssm_operators(lam_re, lam_im, log_dt, b_re, b_im, c_re, c_im, d_skip)
    y = _ssm_core(proj, ops, bsz, seq)
    return _glu(y, w_glu, layer, b_glu.astype(F32).reshape(1, -1), proj)


CUM_BLOCK = 128


def _split3(x):
    x1 = x.astype(BF16)
    r1 = x - x1.astype(F32)
    x2 = r1.astype(BF16)
    x3 = (r1 - x2.astype(F32)).astype(BF16)
    return x1, x2, x3


def _forget_cumsum_kernel(z_ref, b_ref, o_ref):
    n_blocks = z_ref.shape[0] // CUM_BLOCK
    ri = lax.broadcasted_iota(jnp.int32, (CUM_BLOCK, CUM_BLOCK), 0)
    ci = lax.broadcasted_iota(jnp.int32, (CUM_BLOCK, CUM_BLOCK), 1)
    tri = jnp.where(ci <= ri, 1.0, 0.0).astype(BF16)

    def body(i, carry):
        rows = pl.ds(pl.multiple_of(i * CUM_BLOCK, CUM_BLOCK), CUM_BLOCK)
        z = z_ref[rows, :] + b_ref[...]
        log_f = jnp.minimum(z, 0.0) - jnp.log1p(jnp.exp(-jnp.abs(z)))
        x1, x2, x3 = _split3(log_f)
        c = _dot(tri, x1) + _dot(tri, x2) + _dot(tri, x3) + carry
        o_ref[rows, :] = c
        return c[CUM_BLOCK - 1:, :]

    lax.fori_loop(0, n_blocks, body, jnp.zeros((1, z_ref.shape[1]), F32))


def _forget_cumsum(z, b, bsz, seq):
    lanes = z.shape[1]
    return pl.pallas_call(
        _forget_cumsum_kernel,
        grid=(bsz,),
        in_specs=[pl.BlockSpec((seq, lanes), lambda i: (i, 0)), pl.BlockSpec((1, lanes), lambda i: (0, 0))],
        out_specs=pl.BlockSpec((seq, lanes), lambda i: (i, 0)),
        out_shape=jax.ShapeDtypeStruct(z.shape, F32),
        compiler_params=_params(1),
        name="forget_cumsum",
    )(z, b)


FOX_STEP_HEADS = 8


def _fox_kernel(q_ref, k_ref, v_ref, sg_ref, cq_ref, ck_ref, o_ref, ka_ref, vt_ref, *head_refs, tk):
    nh = FOX_STEP_HEADS
    hg, qi = pl.program_id(1), pl.program_id(2)
    tq = q_ref.shape[0]
    qa_ref, s_ref, p_ref, acc_ref, pv_ref, m_ref = (head_refs[i::6] for i in range(6))
    lane = lax.broadcasted_iota(jnp.int32, (1, LANES), 1)
    free = (HEAD_DIM, 0)
    n_pieces = 3
    pair_lanes = lambda h: slice(h // 2 * LANES, (h // 2 + 1) * LANES)

    def bias_lanes(c, h, c_offset, one_offset):
        rr = lax.broadcasted_iota(jnp.int32, (n_pieces * LANES, LANES), 0)
        cc = lax.broadcasted_iota(jnp.int32, (n_pieces * LANES, LANES), 1)
        place = jnp.logical_and(rr % LANES == nh * hg + h, cc == free[h % 2] + c_offset + rr // LANES)
        placed = _dot(jnp.concatenate(_split3(c), axis=1), jnp.where(place, 1.0, 0.0).astype(BF16))
        first = free[h % 2] + one_offset
        return (placed + jnp.where(jnp.logical_and(lane >= first, lane < first + n_pieces), 1.0, 0.0)).astype(BF16)

    @pl.when(qi == 0)
    def _():
        def chunk(j, carry):
            rows = pl.ds(pl.multiple_of(j * tk, tk), tk)
            neg_c = ck_ref[rows, :] * -LOG2E
            for h in range(nh):
                keys = k_ref[rows, pair_lanes(h)]
                ka_ref[h, j] = jnp.where((lane // HEAD_DIM) == h % 2, keys, bias_lanes(neg_c, h, 0, n_pieces))
            for pair in range(nh // 2):
                v_t = v_ref[rows, pair_lanes(2 * pair)].astype(F32).T.astype(BF16)
                for hh in range(2):
                    vt_ref[j, 2 * pair + hh, :HEAD_DIM, :] = v_t[hh * HEAD_DIM:(hh + 1) * HEAD_DIM]
                    vt_ref[j, 2 * pair + hh, HEAD_DIM:, :] = jnp.ones((ONES_ROWS, tk), BF16)
            return carry
        lax.fori_loop(0, k_ref.shape[0] // tk, chunk, 0)

    cq = cq_ref[...] * LOG2E
    for h in range(nh):
        qa = jnp.where((lane // HEAD_DIM) == h % 2, q_ref[:, pair_lanes(h)], bias_lanes(cq, h, n_pieces, 0))
        qa_ref[h][...] = qa.astype(F32).T.astype(BF16)

    def scores(h, j):
        s_ref[h][...] = _dot(ka_ref[h, j], qa_ref[h][...])

    rows8 = 8
    rows16 = 16

    def absorb(h, j, diagonal):
        def strip(r, n):
            blk = s_ref[h][r:r + n, :]
            if diagonal:
                kpos = lax.broadcasted_iota(jnp.int32, (n, tq), 0)
                qpos = lax.broadcasted_iota(jnp.int32, (n, tq), 1)
                blk = jnp.where(kpos + r <= qpos, blk, NEG)
            return blk
        top = strip(0, rows8)
        for r in range(rows8, tk, rows8):
            top = jnp.maximum(top, strip(r, rows8))
        m = m_ref[h][...]
        m_new = jnp.maximum(m, top.max(axis=0, keepdims=True))
        m_ref[h][...] = m_new
        alpha = jnp.exp2(m - m_new)
        m_rows = jnp.broadcast_to(m_new, (rows16, tq))
        for r in range(0, tk, rows16):
            p_ref[h][r:r + rows16, :] = jnp.exp2(strip(r, rows16) - m_rows).astype(BF16)
        acc_ref[h][...] = alpha * (acc_ref[h][...] + pv_ref[h][...])

    def values(h, j):
        pv_ref[h][...] = _dot(vt_ref[j, h], p_ref[h][...])

    def step(j, diagonal):
        values(nh - 1, jnp.maximum(j - 1, 0))
        scores(1, j)
        for h in range(nh):
            absorb(h, j, diagonal)
            if h < nh - 1:
                values(h, j)
            if h + 2 < nh:
                scores(h + 2, j)
            elif h + 2 == nh and not diagonal:
                scores(0, j + 1)

    for h in range(nh):
        acc_ref[h][...] = jnp.zeros(acc_ref[h].shape, F32)
        pv_ref[h][...] = jnp.zeros(pv_ref[h].shape, F32)
        m_ref[h][...] = jnp.full(m_ref[h].shape, NEG, F32)
    p_ref[nh - 1][...] = jnp.zeros(p_ref[nh - 1].shape, BF16)
    scores(0, 0)

    def two_blocks(i, carry):
        step(2 * i, False)
        step(2 * i + 1, False)
        return carry
    lax.fori_loop(0, qi // 2, two_blocks, 0)

    @pl.when(qi % 2 == 0)
    def _():
        step(qi, True)

    @pl.when(qi % 2 == 1)
    def _():
        step(qi - 1, False)
        step(qi, True)

    values(nh - 1, qi)
    outs = []
    for h in range(nh):
        total = acc_ref[h][...] + pv_ref[h][...]
        outs.append(total[:HEAD_DIM] / total[HEAD_DIM:HEAD_DIM + 1])
    out = jnp.concatenate(outs, axis=0)
    o_ref[...] = (out.T * sg_ref[...].astype(F32)).astype(o_ref.dtype)


def _fox_attention(proj, csum, bsz, seq, t=512):
    width = FOX_HEADS * HEAD_DIM
    nh = FOX_STEP_HEADS
    gw = nh * HEAD_DIM
    n_groups = width // gw
    t = min(t, seq)
    nq = seq // t
    return pl.pallas_call(
        functools.partial(_fox_kernel, tk=t),
        grid=(bsz, n_groups, nq),
        in_specs=[
            pl.BlockSpec((t, gw), lambda b, h, i: (b * nq + i, h)),
            pl.BlockSpec((seq, gw), lambda b, h, i: (b, n_groups + h), pipeline_mode=pl.Buffered(1)),
            pl.BlockSpec((seq, gw), lambda b, h, i: (b, 2 * n_groups + h), pipeline_mode=pl.Buffered(1)),
            pl.BlockSpec((t, gw), lambda b, h, i: (b * nq + i, 3 * n_groups + h)),
            pl.BlockSpec((t, LANES), lambda b, h, i: (b * nq + i, 0)),
            pl.BlockSpec((seq, LANES), lambda b, h, i: (b, 0), pipeline_mode=pl.Buffered(1)),
        ],
        out_specs=pl.BlockSpec((t, gw), lambda b, h, i: (b * nq + i, h)),
        out_shape=jax.ShapeDtypeStruct((bsz * seq, width), BF16),
        scratch_shapes=[
            pltpu.VMEM((nh, nq, t, LANES), BF16), pltpu.VMEM((nq, nh, HEAD_DIM + ONES_ROWS, t), BF16),
        ] + nh * [pltpu.VMEM((LANES, t), BF16), pltpu.VMEM((t, t), F32), pltpu.VMEM((t, t), BF16),
                  pltpu.VMEM((HEAD_DIM + ONES_ROWS, t), F32), pltpu.VMEM((HEAD_DIM + ONES_ROWS, t), F32),
                  pltpu.VMEM((1, t), F32)],
        compiler_params=_params(3, VMEM_LIMIT_BIG),
        name="fox_attention",
    )(proj, proj, proj, proj, csum, csum)


def _fox_mixer(x, g, w_in, layer, w_fg, b_fg, bsz, seq, t=512):
    width = FOX_HEADS * HEAD_DIM
    col_scale = jnp.where(jnp.arange(4 * width) < width, LOG2E * HEAD_DIM ** -0.5, 1.0).astype(F32)
    pad = LANES - FOX_HEADS
    w_gate_logits = jnp.pad(w_fg, ((0, 0), (0, pad))).astype(BF16)
    proj, z = _norm_matmul(x, g, _tile_cols(w_in, layer, MATMUL_TN, col_scale), BF16, silu_cols=(3 * width, 4 * width),
                           w_side=w_gate_logits)
    csum = _forget_cumsum(z, jnp.pad(b_fg.astype(F32), (0, pad)).reshape(1, LANES), bsz, seq)
    return _fox_attention(proj, csum, bsz, seq, t)


def kernel(x, p, norm_g, final_g, rel_bias, swa_w_in, swa_w_out, swa_sinks, conv_w_in, conv_kernel, conv_w_out, ssm_w_in, ssm_lam_re, ssm_lam_im, ssm_log_dt, ssm_b_re, ssm_b_im, ssm_c_re, ssm_c_im, ssm_d, ssm_w_glu, ssm_b_glu, ssm_w_out, fox_w_in, fox_w_fg, fox_b_fg, fox_w_out, ple_proj, ple_norm, ple_gate):
    bsz, seq, d_model = x.shape
    depth = p.shape[0]
    h = x.astype(F32).reshape(bsz * seq, d_model)
    for i in range(depth):
        mixer, j = i % N_MIXERS, i // N_MIXERS
        if mixer == 0:
            a, w_out = _swa_mixer(h, norm_g[i], swa_w_in, j, swa_sinks[j], rel_bias, bsz, seq), swa_w_out[j]
        elif mixer == 1:
            a, w_out = _conv_mixer(h, norm_g[i], conv_w_in, j, conv_kernel[j], seq), conv_w_out[j]
        elif mixer == 2:
            a = _ssm_mixer(h, norm_g[i], ssm_w_in, j, ssm_lam_re[j], ssm_lam_im[j], ssm_log_dt[j], ssm_b_re[j], ssm_b_im[j],
                           ssm_c_re[j], ssm_c_im[j], ssm_d[j], ssm_w_glu, ssm_b_glu[j], bsz, seq)
            w_out = ssm_w_out[j]
        else:
            a, w_out = _fox_mixer(h, norm_g[i], fox_w_in, j, fox_w_fg[j], fox_b_fg[j], bsz, seq), fox_w_out[j]
        h = _out_ple(a, w_out, h, ple_norm[i], ple_gate, p.reshape(depth, bsz * seq, -1), ple_proj, i,
                     final_g=final_g if i == depth - 1 else None)
    return h.reshape(bsz, seq, d_model).astype(x.dtype)
```

```python
import functools
import math

import numpy as np
import jax
import jax.numpy as jnp
from jax import lax
from jax.experimental import pallas as pl
from jax.experimental.pallas import tpu as pltpu

F32 = jnp.float32
BF16 = jnp.bfloat16

EPS = 1e-6
N_MIXERS = 4

SWA_HEADS = 32
SWA_KV_HEADS = 4
SWA_GROUP = SWA_HEADS // SWA_KV_HEADS
HEAD_DIM = 64
SWA_BLOCK = 128
WINDOW = 128
REL_BUCKETS = 32
REL_MAX_DIST = 128

CONV_TAPS = 3

SSM_GROUP = 16
SSM_STATE = 64
SSM_CHUNK = 16

FOX_HEADS = 32

LANES = 128
ONES_ROWS = 16
GROUPS_PER_TILE = LANES // SSM_GROUP
VMEM_LIMIT = 48 * 1024 * 1024
VMEM_LIMIT_BIG = 60 * 1024 * 1024

NEG = float(jnp.finfo(jnp.float32).min)
LOG2E = math.log2(math.e)


def _params(n_axes, vmem_limit=VMEM_LIMIT):
    return pltpu.CompilerParams(dimension_semantics=("arbitrary",) * n_axes, vmem_limit_bytes=vmem_limit)


def _dot(a, b):
    return jnp.dot(a, b, preferred_element_type=F32)


def _dot_nt(a, b):
    return lax.dot_general(a, b, (((1,), (1,)), ((), ())), preferred_element_type=F32)


def _rmsnorm_rows(x, g):
    return x * lax.rsqrt(jnp.mean(x * x, axis=-1, keepdims=True) + EPS) * g


def _silu(x):
    return x * jax.nn.sigmoid(x)


ROW_CHUNK = 64


def _for_row_chunks(n_rows, fn):
    def body(c, carry):
        fn(pl.ds(pl.multiple_of(c * ROW_CHUNK, ROW_CHUNK), ROW_CHUNK))
        return carry
    lax.fori_loop(0, n_rows // ROW_CHUNK, body, 0)


def _cast_tiles_kernel(w_ref, o_ref):
    def chunk(rows):
        o_ref[rows, :] = w_ref[rows, :].astype(BF16)
    _for_row_chunks(o_ref.shape[0], chunk)


def _scale_cast_tiles_kernel(w_ref, s_ref, o_ref):
    def chunk(rows):
        o_ref[rows, :] = (w_ref[rows, :] * s_ref[...]).astype(BF16)
    _for_row_chunks(o_ref.shape[0], chunk)


def _tile_cols(w, layer, tn, col_scale=None):
    _, k, n = w.shape
    tn = min(tn, n)
    w_spec = pl.BlockSpec((None, k, tn), lambda j: (layer, 0, j))
    common = dict(
        grid=(n // tn,),
        out_specs=pl.BlockSpec((None, k, tn), lambda j: (j, 0, 0)),
        out_shape=jax.ShapeDtypeStruct((n // tn, k, tn), BF16),
        compiler_params=_params(1),
        name="weight_tiles",
    )
    if col_scale is None:
        return pl.pallas_call(_cast_tiles_kernel, in_specs=[w_spec], **common)(w)
    s_spec = pl.BlockSpec((1, tn), lambda j: (0, j))
    return pl.pallas_call(_scale_cast_tiles_kernel, in_specs=[w_spec, s_spec], **common)(w, col_scale.reshape(1, n))


def _lookahead_x_spec(tm, k, n_row_tiles):
    def tile(i, j):
        first_step = jnp.logical_and(i == 0, j == 0)
        return jnp.where(first_step, 0, jnp.minimum(i + 1, n_row_tiles - 1))
    return pl.BlockSpec((tm, k), lambda i, j: (tile(i, j), 0))


def _lookahead_rows(tm, nt):
    return -(-tm // ((nt - 1) * ROW_CHUNK)) * ROW_CHUNK


def _lookahead_first(x_ref, g_ref, a_ref):
    @pl.when(jnp.logical_and(pl.program_id(0) == 0, pl.program_id(1) == 0))
    def _():
        def chunk(rows):
            a_ref[0, rows, :] = _rmsnorm_rows(x_ref[rows, :], g_ref[...]).astype(BF16)
        _for_row_chunks(a_ref.shape[1], chunk)


def _lookahead_next(x_ref, g_ref, a_ref, rows_per_step):
    i, j = pl.program_id(0), pl.program_id(1)
    tm = a_ref.shape[1]
    start = jnp.clip((j - 1) * rows_per_step, 0, tm - rows_per_step)
    for c in range(rows_per_step // ROW_CHUNK):
        rows = pl.ds(pl.multiple_of(start + c * ROW_CHUNK, ROW_CHUNK), ROW_CHUNK)
        a_ref[(i + 1) % 2, rows, :] = _rmsnorm_rows(x_ref[rows, :], g_ref[...]).astype(BF16)


def _norm_matmul_kernel(x_ref, g_ref, w_ref, *refs, silu_tiles, rows_per_step, side):
    if side:
        ws_ref, o_ref, os_ref, a_ref = refs
    else:
        o_ref, a_ref = refs
    i, j = pl.program_id(0), pl.program_id(1)
    if rows_per_step is None:
        def chunk(rows):
            a_ref[0, rows, :] = _rmsnorm_rows(x_ref[rows, :], g_ref[...]).astype(BF16)
        _for_row_chunks(a_ref.shape[1], chunk)
        acc = _dot(a_ref[0], w_ref[...])
    else:
        _lookahead_first(x_ref, g_ref, a_ref)
        acc = _dot(a_ref[i % 2], w_ref[...])
        _lookahead_next(x_ref, g_ref, a_ref, rows_per_step)
    if side:
        @pl.when(j == 0)
        def _():
            os_ref[...] = _dot(a_ref[i % 2], ws_ref[...])
    if silu_tiles is None:
        o_ref[...] = acc.astype(o_ref.dtype)
    else:
        gated = jnp.logical_and(j >= silu_tiles[0], j < silu_tiles[1])

        @pl.when(gated)
        def _():
            o_ref[...] = _silu(acc).astype(o_ref.dtype)

        @pl.when(jnp.logical_not(gated))
        def _():
            o_ref[...] = acc.astype(o_ref.dtype)


MATMUL_TN = 1024


def _norm_matmul(x, g, w_tiles, out_dtype, silu_cols=None, w_side=None, tm=1024):
    m, k = x.shape
    nt, _, tn = w_tiles.shape
    n = nt * tn
    tm = min(tm, m)
    n_row_tiles = m // tm
    silu_tiles = None if silu_cols is None else (silu_cols[0] // tn, silu_cols[1] // tn)
    lookahead = nt > 1
    side = w_side is not None
    assert lookahead or not side
    in_specs = [
        _lookahead_x_spec(tm, k, n_row_tiles) if lookahead else pl.BlockSpec((tm, k), lambda i, j: (i, 0)),
        pl.BlockSpec((1, k), lambda i, j: (0, 0)),
        pl.BlockSpec((None, k, tn), lambda i, j: (j, 0, 0)),
    ]
    out_specs = pl.BlockSpec((tm, tn), lambda i, j: (i, j))
    out_shape = jax.ShapeDtypeStruct((m, n), out_dtype)
    operands = (x, g.reshape(1, k), w_tiles)
    if side:
        ns = w_side.shape[1]
        in_specs.append(pl.BlockSpec((k, ns), lambda i, j: (0, 0)))
        out_specs = [out_specs, pl.BlockSpec((tm, ns), lambda i, j: (i, 0))]
        out_shape = [out_shape, jax.ShapeDtypeStruct((m, ns), F32)]
        operands += (w_side,)
    return pl.pallas_call(
        functools.partial(_norm_matmul_kernel, silu_tiles=silu_tiles, side=side,
                          rows_per_step=_lookahead_rows(tm, nt) if lookahead else None),
        grid=(n_row_tiles, nt),
        in_specs=in_specs,
        out_specs=out_specs,
        out_shape=out_shape,
        scratch_shapes=[pltpu.VMEM((2, tm, k), BF16)],
        compiler_params=_params(2, VMEM_LIMIT_BIG),
        name="norm_matmul",
    )(*operands)


def _out_ple_kernel(a_ref, wo_ref, x_ref, g_ref, wg_ref, p_ref, wp_ref, gf_ref, o_ref, x1_ref, hn_ref, ss_ref, *, nt, final):
    j = pl.program_id(1)
    tn = x_ref.shape[1]
    n = nt * tn

    @pl.when(j == 0)
    def _():
        ss_ref[...] = jnp.zeros(ss_ref.shape, F32)

    @pl.when(j < nt)
    def _():
        x1 = x_ref[...] + _dot(a_ref[...], wo_ref[j])
        x1_ref[j] = x1
        hn_ref[j] = (x1 * g_ref[j]).astype(BF16)
        ss_ref[0] += jnp.sum(x1 * x1, axis=-1, keepdims=True)

    @pl.when(j >= nt)
    def _():
        t = j - nt
        emb = _dot(p_ref[...].astype(BF16), wp_ref[t])
        acc = _dot(hn_ref[0], wg_ref[t, :tn, :])
        for kt in range(1, nt):
            acc = acc + _dot(hn_ref[kt], wg_ref[t, kt * tn:(kt + 1) * tn, :])
        x2 = x1_ref[t] + emb * jax.nn.sigmoid(acc * lax.rsqrt(ss_ref[0] / n + EPS))
        if final:
            x1_ref[t] = x2
            ss_ref[1] += jnp.sum(x2 * x2, axis=-1, keepdims=True)
        else:
            o_ref[...] = x2

    if final:
        @pl.when(j == 2 * nt - 1)
        def _():
            inv = lax.rsqrt(ss_ref[1] / n + EPS)
            for t in range(nt):
                o_ref[:, t * tn:(t + 1) * tn] = x1_ref[t] * inv * gf_ref[t]


def _out_ple(a, w_out, x, g, w_gate, p, w_proj, layer, final_g=None, tm=1024, tn=512):
    m, k = a.shape
    n = w_out.shape[1]
    pd = p.shape[2]
    final = final_g is not None
    tm, tn = min(tm, m), min(tn, n)
    nt = n // tn
    if final:
        out_spec = pl.BlockSpec((tm, n), lambda i, j: (i, 0), pipeline_mode=pl.Buffered(1))
    else:
        out_spec = pl.BlockSpec((tm, tn), lambda i, j: (i, jnp.maximum(j - nt, 0)))
    resident = lambda rows: pl.BlockSpec((nt, rows, tn), lambda i, j: (0, 0, 0), pipeline_mode=pl.Buffered(1))
    row_vec = pl.BlockSpec((nt, 1, tn), lambda i, j: (0, 0, 0))
    return pl.pallas_call(
        functools.partial(_out_ple_kernel, nt=nt, final=final),
        grid=(m // tm, 2 * nt),
        in_specs=[
            pl.BlockSpec((tm, k), lambda i, j: (i, 0)),
            resident(k),
            pl.BlockSpec((tm, tn), lambda i, j: (i, jnp.minimum(j, nt - 1))),
            row_vec,
            resident(n),
            pl.BlockSpec((None, tm, pd), lambda i, j: (layer, i, 0)),
            resident(pd),
            row_vec,
        ],
        out_specs=out_spec,
        out_shape=jax.ShapeDtypeStruct((m, n), F32),
        scratch_shapes=[pltpu.VMEM((nt, tm, tn), F32), pltpu.VMEM((nt, tm, tn), BF16), pltpu.VMEM((2, tm, 1), F32)],
        compiler_params=_params(2, VMEM_LIMIT_BIG),
        name="out_ple_final" if final else "out_ple",
    )(a, _tile_cols(w_out[None], 0, tn), x, g.reshape(nt, 1, tn), _tile_cols(w_gate, layer, tn), p,
      _tile_cols(w_proj, layer, tn), (final_g if final else g).reshape(nt, 1, tn))


def _t5_bucket(dist):
    max_exact = REL_BUCKETS // 2
    d = np.maximum(dist, 1).astype(np.float32)
    large = max_exact + (np.log(d / max_exact) / np.log(REL_MAX_DIST / max_exact) * (REL_BUCKETS - max_exact)).astype(np.int32)
    large = np.minimum(large, REL_BUCKETS - 1)
    return np.where(dist < max_exact, dist, large).astype(np.int32)


assert WINDOW == SWA_BLOCK


def _swa_bucket_table():
    qi = np.arange(SWA_BLOCK)[None, :]
    kj = np.arange(2 * SWA_BLOCK)[:, None]
    return _t5_bucket(np.clip(qi + SWA_BLOCK - kj, 0, None))


def _swa_kernel(sink_ref, rel_ref, bucket_ref, q_ref, sg_ref, kp_ref, kc_ref, vp_ref, vc_ref, o_ref, bias_ref, qs_ref):
    n = pl.program_id(1)
    blk = SWA_BLOCK
    lane = lax.broadcasted_iota(jnp.int32, (1, LANES), 1)

    @pl.when(jnp.logical_and(pl.program_id(0) == 0, n == 0))
    def _():
        bucket = bucket_ref[...]
        kj = lax.broadcasted_iota(jnp.int32, (2 * blk, blk), 0)
        qi = lax.broadcasted_iota(jnp.int32, (2 * blk, blk), 1)
        band = jnp.logical_or(jnp.logical_and(kj < blk, kj > qi), jnp.logical_and(kj >= blk, kj - blk <= qi))

        def per_head(head, carry):
            acc = jnp.zeros((2 * blk, blk), F32)
            for b in range(REL_BUCKETS):
                acc = jnp.where(bucket == b, rel_ref[b, head] * LOG2E, acc)
            bias_ref[head] = jnp.where(band, acc, NEG)
            return carry
        lax.fori_loop(0, SWA_HEADS, per_head, 0)

    for kvh in range(SWA_KV_HEADS):
        cols = slice(kvh * LANES, (kvh + 1) * LANES)
        heads = range(kvh * SWA_GROUP, (kvh + 1) * SWA_GROUP)
        for g, head in enumerate(heads):
            q_pair = q_ref[:, head // 2 * LANES:(head // 2 + 1) * LANES]
            qs_ref[g * blk:(g + 1) * blk, :] = jnp.where((lane // HEAD_DIM) == head % 2, q_pair, jnp.zeros_like(q_pair))
        keys = jnp.concatenate([kp_ref[:, cols], kc_ref[:, cols]], axis=0)
        vals = jnp.concatenate([vp_ref[:, cols], vc_ref[:, cols]], axis=0)
        s = _dot_nt(keys, qs_ref[...]) + jnp.concatenate([bias_ref[head] for head in heads], axis=1)
        s_prev = jnp.where(n > 0, s[:blk], NEG)
        s_cur = s[blk:]
        sink = jnp.concatenate([jnp.full((1, blk), sink_ref[head] * LOG2E, F32) for head in heads], axis=1)
        m = jnp.maximum(jnp.maximum(s_prev, s_cur).max(axis=0, keepdims=True), sink)
        e = jnp.concatenate([jnp.exp2(s_prev - m).astype(BF16), jnp.exp2(s_cur - m).astype(BF16)], axis=0)
        v_t = jnp.concatenate([vals.astype(F32).T.astype(BF16)[:HEAD_DIM], jnp.ones((ONES_ROWS, 2 * blk), BF16)], axis=0)
        out = _dot(v_t, e)
        out = out[:HEAD_DIM] / (out[HEAD_DIM:HEAD_DIM + 1] + jnp.exp2(sink - m))
        for pair in range(SWA_GROUP // 2):
            head = kvh * SWA_GROUP + 2 * pair
            even = out[:, 2 * pair * blk:(2 * pair + 1) * blk]
            odd = out[:, (2 * pair + 1) * blk:(2 * pair + 2) * blk]
            cols = slice(head // 2 * LANES, (head // 2 + 1) * LANES)
            o_ref[:, cols] = (jnp.concatenate([even, odd], axis=0).T * sg_ref[:, cols].astype(F32)).astype(o_ref.dtype)


def _swa_attention(proj, sinks, rel_bias, bsz, seq):
    nb = seq // SWA_BLOCK
    width = SWA_HEADS * HEAD_DIM
    kvw = SWA_KV_HEADS * LANES
    k_blk = 2 * width // kvw
    cur = lambda b, n: b * nb + n
    prev = lambda b, n: b * nb + jnp.maximum(n - 1, 0)
    smem = pl.BlockSpec(memory_space=pltpu.SMEM)
    return pl.pallas_call(
        _swa_kernel,
        grid=(bsz, nb),
        in_specs=[
            smem, smem,
            pl.BlockSpec((2 * SWA_BLOCK, SWA_BLOCK), lambda b, n: (0, 0)),
            pl.BlockSpec((SWA_BLOCK, width), lambda b, n: (cur(b, n), 0)),
            pl.BlockSpec((SWA_BLOCK, width), lambda b, n: (cur(b, n), 1)),
            pl.BlockSpec((SWA_BLOCK, kvw), lambda b, n: (prev(b, n), k_blk)),
            pl.BlockSpec((SWA_BLOCK, kvw), lambda b, n: (cur(b, n), k_blk)),
            pl.BlockSpec((SWA_BLOCK, kvw), lambda b, n: (prev(b, n), k_blk + 1)),
            pl.BlockSpec((SWA_BLOCK, kvw), lambda b, n: (cur(b, n), k_blk + 1)),
        ],
        out_specs=pl.BlockSpec((SWA_BLOCK, width), lambda b, n: (cur(b, n), 0)),
        out_shape=jax.ShapeDtypeStruct((bsz * seq, width), BF16),
        scratch_shapes=[
            pltpu.VMEM((SWA_HEADS, 2 * SWA_BLOCK, SWA_BLOCK), F32),
            pltpu.VMEM((SWA_GROUP * SWA_BLOCK, LANES), BF16),
        ],
        compiler_params=_params(2),
        name="swa_attention",
    )(sinks.astype(F32), rel_bias.astype(F32), jnp.asarray(_swa_bucket_table()), proj, proj, proj, proj, proj, proj)


SWA_HALF_TILE = 512


def _swa_weight_tiles_kernel(wl_ref, wr_ref, o_ref, *, q_tiles, gate_tiles, q_scale):
    j = pl.program_id(0)
    half = SWA_HALF_TILE
    kvw = SWA_KV_HEADS * HEAD_DIM

    def rows_do(fn):
        def chunk(rows):
            for side, w_ref in enumerate((wl_ref, wr_ref)):
                o_ref[rows, side * half:(side + 1) * half] = fn(side, w_ref, rows)
        _for_row_chunks(o_ref.shape[0], chunk)

    @pl.when(j < q_tiles)
    def _():
        rows_do(lambda side, w_ref, rows: (w_ref[rows, :] * q_scale).astype(BF16))

    @pl.when(jnp.logical_and(j >= q_tiles, j < q_tiles + gate_tiles))
    def _():
        rows_do(lambda side, w_ref, rows: w_ref[rows, :].astype(BF16))

    @pl.when(j == q_tiles + gate_tiles)
    def _():
        def twice(side, w_ref, rows):
            w = w_ref[rows, side * kvw:(side + 1) * kvw].astype(BF16)
            heads = [w[:, h * HEAD_DIM:(h + 1) * HEAD_DIM] for h in range(SWA_KV_HEADS)]
            return jnp.concatenate([piece for head in heads for piece in (head, head)], axis=1)
        rows_do(twice)


def _swa_weight_tiles(w_in, layer, q_scale):
    _, k, n = w_in.shape
    half = SWA_HALF_TILE
    width = SWA_HEADS * HEAD_DIM
    q_tiles = gate_tiles = width // (2 * half)
    kv_block = width // half
    n_out = q_tiles + gate_tiles + 1

    def src(j, side):
        in_q, in_gate = j < q_tiles, j < q_tiles + gate_tiles
        return jnp.where(in_q, 2 * j + side, jnp.where(in_gate, 2 * j + side + 1, kv_block))
    return pl.pallas_call(
        functools.partial(_swa_weight_tiles_kernel, q_tiles=q_tiles, gate_tiles=gate_tiles, q_scale=q_scale),
        grid=(n_out,),
        in_specs=[pl.BlockSpec((None, k, half), lambda j: (layer, 0, src(j, 0))),
                  pl.BlockSpec((None, k, half), lambda j: (layer, 0, src(j, 1)))],
        out_specs=pl.BlockSpec((None, k, 2 * half), lambda j: (j, 0, 0)),
        out_shape=jax.ShapeDtypeStruct((n_out, k, 2 * half), BF16),
        compiler_params=_params(1),
        name="swa_weight_tiles",
    )(w_in, w_in)


def _swa_mixer(x, g, w_in, layer, sinks, rel_bias, bsz, seq):
    width = SWA_HEADS * HEAD_DIM
    w_tiles = _swa_weight_tiles(w_in, layer, LOG2E * HEAD_DIM ** -0.5)
    proj = _norm_matmul(x, g, w_tiles, BF16, silu_cols=(width, 2 * width))
    return _swa_attention(proj, sinks, rel_bias, bsz, seq)


HALO = 8


def _conv_kernel(x_ref, g_ref, wb_ref, wc_ref, wu_ref, wg_ref, ck_ref, o_ref, a_ref, z_ref, carry_ref, *,
                 tiles_per_seq, rows_per_step):
    i, j = pl.program_id(0), pl.program_id(1)
    tm = a_ref.shape[1]
    _lookahead_first(x_ref, g_ref, a_ref)
    a = a_ref[i % 2]
    z = _dot(a, wc_ref[...]) * _dot(a, wu_ref[...])
    first = (i % tiles_per_seq) == 0

    @pl.when(first)
    def _():
        z_ref[:HALO, :] = jnp.zeros((HALO, z_ref.shape[1]), F32)

    @pl.when(jnp.logical_not(first))
    def _():
        z_ref[:HALO, :] = carry_ref[j]

    z_ref[HALO:, :] = z
    carry_ref[j] = z[tm - HALO:, :]
    conv = z_ref[HALO - 2:HALO - 2 + tm, :] * ck_ref[0:1, :]
    conv = conv + z_ref[HALO - 1:HALO - 1 + tm, :] * ck_ref[1:2, :]
    conv = conv + z * ck_ref[2:3, :]
    y = _dot(a, wb_ref[...]) * conv
    o_ref[...] = (y * _silu(_dot(a, wg_ref[...]))).astype(o_ref.dtype)
    _lookahead_next(x_ref, g_ref, a_ref, rows_per_step)


def _conv_mixer(x, g, w_in, layer, conv_kernel, seq, tm=1024, tn=512):
    m, k = x.shape
    width = w_in.shape[2] // 4
    tm, tn = min(tm, seq), min(tn, width)
    nt = width // tn
    w_spec = lambda q: pl.BlockSpec((None, k, tn), lambda i, j: (q * nt + j, 0, 0))
    w_in = _tile_cols(w_in, layer, tn)
    return pl.pallas_call(
        functools.partial(_conv_kernel, tiles_per_seq=seq // tm, rows_per_step=_lookahead_rows(tm, nt)),
        grid=(m // tm, nt),
        in_specs=[
            _lookahead_x_spec(tm, k, m // tm),
            pl.BlockSpec((1, k), lambda i, j: (0, 0)),
            w_spec(0), w_spec(1), w_spec(2), w_spec(3),
            pl.BlockSpec((CONV_TAPS, tn), lambda i, j: (0, j)),
        ],
        out_specs=pl.BlockSpec((tm, tn), lambda i, j: (i, j)),
        out_shape=jax.ShapeDtypeStruct((m, width), BF16),
        scratch_shapes=[
            pltpu.VMEM((2, tm, k), BF16),
            pltpu.VMEM((HALO + tm, tn), F32),
            pltpu.VMEM((nt, HALO, tn), F32),
        ],
        compiler_params=_params(2, VMEM_LIMIT_BIG),
        name="conv_mixer",
    )(x, g.reshape(1, k), w_in, w_in, w_in, w_in, conv_kernel.astype(F32))


def _ssm_operators(lam_re, lam_im, log_dt, b_re, b_im, c_re, c_im, d_skip):
    n_groups = lam_re.shape[0]
    n_oct = n_groups // GROUPS_PER_TILE
    L, C, N = SSM_CHUNK, SSM_GROUP, SSM_STATE
    dt = jnp.exp(log_dt.astype(F32))[None, :]
    lr, li = lam_re.astype(F32).T, lam_im.astype(F32).T
    mag = jnp.exp(lr * dt)
    ab_re, ab_im = mag * jnp.cos(li * dt), mag * jnp.sin(li * dt)
    den = lr * lr + li * li
    nr = ab_re - 1.0
    coef_re = (nr * lr + ab_im * li) / den
    coef_im = (ab_im * lr - nr * li) / den
    br, bi = b_re.astype(F32).transpose(2, 1, 0), b_im.astype(F32).transpose(2, 1, 0)
    bb_re = coef_re * br - coef_im * bi
    bb_im = coef_re * bi + coef_im * br
    cr, ci = c_re.astype(F32).transpose(1, 2, 0), c_im.astype(F32).transpose(1, 2, 0)
    pw_re, pw_im = [jnp.ones_like(ab_re)], [jnp.zeros_like(ab_im)]
    for _ in range(L):
        pw_re, pw_im = (pw_re + [pw_re[-1] * ab_re - pw_im[-1] * ab_im], pw_im + [pw_re[-1] * ab_im + pw_im[-1] * ab_re])
    p_re, p_im = jnp.stack(pw_re), jnp.stack(pw_im)
    w_re = cr[None] * p_re[:L, None] - ci[None] * p_im[:L, None]
    w_im = cr[None] * p_im[:L, None] + ci[None] * p_re[:L, None]
    lag_k = jnp.sum(w_re[:, None] * bb_re[None, :, None] - w_im[:, None] * bb_im[None, :, None], axis=3)
    q_re, q_im = jnp.stack(pw_re[L - 1::-1]), jnp.stack(pw_im[L - 1::-1])
    bc_re = q_re[:, None] * bb_re[None] - q_im[:, None] * bb_im[None]
    bc_im = q_re[:, None] * bb_im[None] + q_im[:, None] * bb_re[None]
    o_re = cr[None] * p_re[1:, None] - ci[None] * p_im[1:, None]
    o_im = cr[None] * p_im[1:, None] + ci[None] * p_re[1:, None]
    octs = lambda t: t.reshape(t.shape[:-1] + (n_oct, GROUPS_PER_TILE))
    to_lag = lambda t: octs(t).transpose(3, 0, 1, 4, 2).reshape(n_oct, L, C, LANES)
    to_rows = lambda t: octs(t).transpose(3, 0, 1, 4, 2).reshape(n_oct, L * C, GROUPS_PER_TILE * N)
    to_cols = lambda t: octs(t).transpose(3, 4, 2, 0, 1).reshape(n_oct, GROUPS_PER_TILE * N, L * C)
    to_vec = lambda t: octs(t).transpose(1, 2, 0).reshape(n_oct, 1, GROUPS_PER_TILE * N)
    return dict(
        lag=to_lag(lag_k).astype(BF16),
        bc_re=to_rows(bc_re).astype(BF16), bc_im=to_rows(bc_im).astype(BF16),
        oc_re=to_cols(o_re).astype(BF16), oc_im_neg=to_cols(-o_im).astype(BF16),
        al_re=to_vec(p_re[L]), al_im=to_vec(p_im[L]), d=d_skip.astype(F32).reshape(n_oct, 1, LANES),
    )


def _ssm_expanders():
    L, C, N, G8 = SSM_CHUNK, SSM_GROUP, SSM_STATE, GROUPS_PER_TILE
    wide = np.arange(L * LANES)
    w_step, w_group, w_chan = wide // LANES, (wide // C) % G8, wide % C
    small = np.arange(L * C)
    s_step, s_chan = small // C, small % C
    state_group = np.arange(G8 * N) // N
    spread = (w_step[:, None] == s_step[None, :]) & (w_chan[:, None] == s_chan[None, :])
    lane = np.arange(LANES)
    as_bf16 = lambda a: jnp.asarray(a.astype(np.float32), dtype=BF16)
    return dict(
        spread_cols=as_bf16(spread.T),
        keep_cols=as_bf16(state_group[:, None] == w_group[None, :]),
        spread_lag=as_bf16((lane % C)[:, None] == np.arange(C)[None, :]),
        keep_lag=jnp.asarray(((lane // C)[:, None] == (lane // C)[None, :]).astype(np.float32)),
    )


def _ssm_kernel(u_ref, lag_ref, bxr_ref, bxi_ref, cxr_ref, cxi_ref, alr_ref, ali_ref, d_ref,
                sc_ref, kc_ref, sl_ref, kl_ref, o_ref,
                ub_ref, bcr_ref, bci_ref, ocr_ref, oci_ref, zr_ref, zi_ref, hr_ref, hi_ref, y_ref, *, bsz):
    L = SSM_CHUNK
    rows = u_ref.shape[0] // L
    chunks = rows // bsz
    tile = 2 * LANES
    step_rows = lambda r: pl.ds(r, rows, stride=L)
    for r in range(L):
        ub_ref[:, r * LANES:(r + 1) * LANES] = u_ref[step_rows(r), :].astype(BF16)
    state_group = lax.broadcasted_iota(jnp.int32, (1, bxr_ref.shape[1]), 1) // SSM_STATE
    for compact_ref, full_ref in ((bxr_ref, bcr_ref), (bxi_ref, bci_ref)):
        for r in range(L):
            blk = compact_ref[r * SSM_GROUP:(r + 1) * SSM_GROUP, :]
            for grp in range(GROUPS_PER_TILE):
                first = r * LANES + grp * SSM_GROUP
                full_ref[first:first + SSM_GROUP, :] = jnp.where(state_group == grp, blk, jnp.zeros_like(blk))
    zr_ref[...] = _dot(ub_ref[...], bcr_ref[...])
    zi_ref[...] = _dot(ub_ref[...], bci_ref[...])
    a_re, a_im = alr_ref[...], ali_ref[...]

    def step(k, carry):
        new = []
        for b in range(bsz):
            h_re, h_im = carry[2 * b], carry[2 * b + 1]
            row = pl.ds(b * chunks + k, 1)
            hr_ref[row, :] = h_re
            hi_ref[row, :] = h_im
            new.append(a_re * h_re - a_im * h_im + zr_ref[row, :])
            new.append(a_re * h_im + a_im * h_re + zi_ref[row, :])
        return tuple(new)

    zero = jnp.zeros((1, a_re.shape[1]), F32)
    lax.fori_loop(0, chunks, step, (zero,) * (2 * bsz), unroll=True)

    ocr_ref[...] = (_dot(cxr_ref[...], sc_ref[...]) * kc_ref[...]).astype(BF16)
    oci_ref[...] = (_dot(cxi_ref[...], sc_ref[...]) * kc_ref[...]).astype(BF16)
    lag = [(_dot(sl_ref[...], lag_ref[d]) * kl_ref[...]).astype(BF16) for d in range(L)]
    lag_tile = []
    for dd in range(L // 2):
        below = lag[2 * dd - 1] if dd > 0 else jnp.zeros((LANES, LANES), BF16)
        lag_tile.append(jnp.concatenate([jnp.concatenate([lag[2 * dd], lag[2 * dd + 1]], axis=1),
                                         jnp.concatenate([below, lag[2 * dd]], axis=1)], axis=0))
    hb_re, hb_im = hr_ref[...].astype(BF16), hi_ref[...].astype(BF16)
    for t2 in range(L // 2):
        cols = slice(t2 * tile, (t2 + 1) * tile)
        y = _dot(hb_re, ocr_ref[:, cols]) + _dot(hb_im, oci_ref[:, cols])
        for r2 in range(t2 + 1):
            y = y + _dot(ub_ref[:, r2 * tile:(r2 + 1) * tile], lag_tile[t2 - r2])
        for half in range(2):
            t = 2 * t2 + half
            y_ref[step_rows(t), :] = jax.nn.gelu(y[:, half * LANES:(half + 1) * LANES] + d_ref[...] * u_ref[step_rows(t), :])

    def chunk(rows):
        o_ref[rows, :] = y_ref[rows, :].astype(o_ref.dtype)
    _for_row_chunks(o_ref.shape[0], chunk)


def _ssm_core(proj, ops, bsz, seq):
    m = proj.shape[0]
    width = proj.shape[1] // 2
    L, C = SSM_CHUNK, SSM_GROUP
    n_oct = width // LANES
    rows = m // L
    ow = L * LANES
    n_state = GROUPS_PER_TILE * SSM_STATE
    ex = _ssm_expanders()
    per_oct = lambda *shape: pl.BlockSpec((None,) + shape, lambda p: (p,) + (0,) * len(shape))
    const = lambda a: pl.BlockSpec(a.shape, lambda p: (0,) * a.ndim, pipeline_mode=pl.Buffered(1))
    consts = [ex['spread_cols'], ex['keep_cols'], ex['spread_lag'], ex['keep_lag']]
    return pl.pallas_call(
        functools.partial(_ssm_kernel, bsz=bsz),
        grid=(n_oct,),
        in_specs=[
            pl.BlockSpec((m, LANES), lambda p: (0, p)),
            per_oct(L, C, LANES), per_oct(L * C, n_state), per_oct(L * C, n_state),
            per_oct(n_state, L * C), per_oct(n_state, L * C), per_oct(1, n_state), per_oct(1, n_state), per_oct(1, LANES),
        ] + [const(a) for a in consts],
        out_specs=pl.BlockSpec((m, LANES), lambda p: (0, p)),
        out_shape=jax.ShapeDtypeStruct((m, width), BF16),
        scratch_shapes=[pltpu.VMEM((rows, ow), BF16)] + [pltpu.VMEM((ow, n_state), BF16)] * 2
        + [pltpu.VMEM((n_state, ow), BF16)] * 2 + [pltpu.VMEM((rows, n_state), F32)] * 4 + [pltpu.VMEM((m, LANES), F32)],
        compiler_params=_params(1),
        name="ssm_core",
    )(proj, ops['lag'], ops['bc_re'], ops['bc_im'], ops['oc_re'], ops['oc_im_neg'], ops['al_re'], ops['al_im'], ops['d'], *consts)


def _glu_kernel(y_ref, wa_ref, wb_ref, ba_ref, bb_ref, gate_ref, o_ref):
    y = y_ref[...]
    ga = _dot(y, wa_ref[...]) + ba_ref[...]
    gb = _dot(y, wb_ref[...]) + bb_ref[...]
    o_ref[...] = ((ga * jax.nn.sigmoid(gb)) * _silu(gate_ref[...])).astype(o_ref.dtype)


def _glu(y, w_glu, layer, b_glu, proj, tm=1024, tn=1024):
    m, k = y.shape
    width = w_glu.shape[2] // 2
    tm, tn = min(tm, m), min(tn, width)
    nt = width // tn
    w_tiles = _tile_cols(w_glu, layer, tn)
    return pl.pallas_call(
        _glu_kernel,
        grid=(m // tm, nt),
        in_specs=[
            pl.BlockSpec((tm, k), lambda i, j: (i, 0)),
            pl.BlockSpec((None, k, tn), lambda i, j: (j, 0, 0)),
            pl.BlockSpec((None, k, tn), lambda i, j: (nt + j, 0, 0)),
            pl.BlockSpec((1, tn), lambda i, j: (0, j)),
            pl.BlockSpec((1, tn), lambda i, j: (0, nt + j)),
            pl.BlockSpec((tm, tn), lambda i, j: (i, nt + j)),
        ],
        out_specs=pl.BlockSpec((tm, tn), lambda i, j: (i, j)),
        out_shape=jax.ShapeDtypeStruct((m, width), BF16),
        compiler_params=_params(2, VMEM_LIMIT_BIG),
        name="ssm_glu",
    )(y, w_tiles, w_tiles, b_glu, b_glu, proj)


def _ssm_mixer(x, g, w_in, layer, lam_re, lam_im, log_dt, b_re, b_im, c_re, c_im, d_skip, w_glu, b_glu, bsz, seq):
    proj = _norm_matmul(x, g, _tile_cols(w_in, layer, MATMUL_TN), F32)
    ops = _ssm_operators(lam_re, lam_im, log_dt, b_re, b_im, c_re, c_im, d_skip)
    y = _ssm_core(proj, ops, bsz, seq)
    return _glu(y, w_glu, layer, b_glu.astype(F32).reshape(1, -1), proj)


CUM_BLOCK = 128


def _split3(x):
    x1 = x.astype(BF16)
    r1 = x - x1.astype(F32)
    x2 = r1.astype(BF16)
    x3 = (r1 - x2.astype(F32)).astype(BF16)
    return x1, x2, x3


def _forget_cumsum_kernel(z_ref, b_ref, o_ref):
    n_blocks = z_ref.shape[0] // CUM_BLOCK
    ri = lax.broadcasted_iota(jnp.int32, (CUM_BLOCK, CUM_BLOCK), 0)
    ci = lax.broadcasted_iota(jnp.int32, (CUM_BLOCK, CUM_BLOCK), 1)
    tri = jnp.where(ci <= ri, 1.0, 0.0).astype(BF16)

    def body(i, carry):
        rows = pl.ds(pl.multiple_of(i * CUM_BLOCK, CUM_BLOCK), CUM_BLOCK)
        z = z_ref[rows, :] + b_ref[...]
        log_f = jnp.minimum(z, 0.0) - jnp.log1p(jnp.exp(-jnp.abs(z)))
        x1, x2, x3 = _split3(log_f)
        c = _dot(tri, x1) + _dot(tri, x2) + _dot(tri, x3) + carry
        o_ref[rows, :] = c
        return c[CUM_BLOCK - 1:, :]

    lax.fori_loop(0, n_blocks, body, jnp.zeros((1, z_ref.shape[1]), F32))


def _forget_cumsum(z, b, bsz, seq):
    lanes = z.shape[1]
    return pl.pallas_call(
        _forget_cumsum_kernel,
        grid=(bsz,),
        in_specs=[pl.BlockSpec((seq, lanes), lambda i: (i, 0)), pl.BlockSpec((1, lanes), lambda i: (0, 0))],
        out_specs=pl.BlockSpec((seq, lanes), lambda i: (i, 0)),
        out_shape=jax.ShapeDtypeStruct(z.shape, F32),
        compiler_params=_params(1),
        name="forget_cumsum",
    )(z, b)


FOX_STEP_HEADS = 8


def _fox_kernel(q_ref, k_ref, v_ref, sg_ref, cq_ref, ck_ref, o_ref, ka_ref, vt_ref, *head_refs, tk):
    nh = FOX_STEP_HEADS
    hg, qi = pl.program_id(1), pl.program_id(2)
    tq = q_ref.shape[0]
    qa_ref, s_ref, p_ref, acc_ref, pv_ref, m_ref = (head_refs[i::6] for i in range(6))
    lane = lax.broadcasted_iota(jnp.int32, (1, LANES), 1)
    free = (HEAD_DIM, 0)
    n_pieces = 3
    pair_lanes = lambda h: slice(h // 2 * LANES, (h // 2 + 1) * LANES)

    def bias_lanes(c, h, c_offset, one_offset):
        rr = lax.broadcasted_iota(jnp.int32, (n_pieces * LANES, LANES), 0)
        cc = lax.broadcasted_iota(jnp.int32, (n_pieces * LANES, LANES), 1)
        place = jnp.logical_and(rr % LANES == nh * hg + h, cc == free[h % 2] + c_offset + rr // LANES)
        placed = _dot(jnp.concatenate(_split3(c), axis=1), jnp.where(place, 1.0, 0.0).astype(BF16))
        first = free[h % 2] + one_offset
        return (placed + jnp.where(jnp.logical_and(lane >= first, lane < first + n_pieces), 1.0, 0.0)).astype(BF16)

    @pl.when(qi == 0)
    def _():
        def chunk(j, carry):
            rows = pl.ds(pl.multiple_of(j * tk, tk), tk)
            neg_c = ck_ref[rows, :] * -LOG2E
            for h in range(nh):
                keys = k_ref[rows, pair_lanes(h)]
                ka_ref[h, j] = jnp.where((lane // HEAD_DIM) == h % 2, keys, bias_lanes(neg_c, h, 0, n_pieces))
            for pair in range(nh // 2):
                v_t = v_ref[rows, pair_lanes(2 * pair)].astype(F32).T.astype(BF16)
                for hh in range(2):
                    vt_ref[j, 2 * pair + hh, :HEAD_DIM, :] = v_t[hh * HEAD_DIM:(hh + 1) * HEAD_DIM]
                    vt_ref[j, 2 * pair + hh, HEAD_DIM:, :] = jnp.ones((ONES_ROWS, tk), BF16)
            return carry
        lax.fori_loop(0, k_ref.shape[0] // tk, chunk, 0)

    cq = cq_ref[...] * LOG2E
    for h in range(nh):
        qa = jnp.where((lane // HEAD_DIM) == h % 2, q_ref[:, pair_lanes(h)], bias_lanes(cq, h, n_pieces, 0))
        qa_ref[h][...] = qa.astype(F32).T.astype(BF16)

    def scores(h, j):
        s_ref[h][...] = _dot(ka_ref[h, j], qa_ref[h][...])

    rows8 = 8
    rows16 = 16

    def absorb(h, j, diagonal):
        def strip(r, n):
            blk = s_ref[h][r:r + n, :]
            if diagonal:
                kpos = lax.broadcasted_iota(jnp.int32, (n, tq), 0)
                qpos = lax.broadcasted_iota(jnp.int32, (n, tq), 1)
                blk = jnp.where(kpos + r <= qpos, blk, NEG)
            return blk
        top = strip(0, rows8)
        for r in range(rows8, tk, rows8):
            top = jnp.maximum(top, strip(r, rows8))
        m = m_ref[h][...]
        m_new = jnp.maximum(m, top.max(axis=0, keepdims=True))
        m_ref[h][...] = m_new
        alpha = jnp.exp2(m - m_new)
        m_rows = jnp.broadcast_to(m_new, (rows16, tq))
        for r in range(0, tk, rows16):
            p_ref[h][r:r + rows16, :] = jnp.exp2(strip(r, rows16) - m_rows).astype(BF16)
        acc_ref[h][...] = alpha * (acc_ref[h][...] + pv_ref[h][...])

    def values(h, j):
        pv_ref[h][...] = _dot(vt_ref[j, h], p_ref[h][...])

    def step(j, diagonal):
        values(nh - 1, jnp.maximum(j - 1, 0))
        scores(1, j)
        for h in range(nh):
            absorb(h, j, diagonal)
            if h < nh - 1:
                values(h, j)
            if h + 2 < nh:
                scores(h + 2, j)
            elif h + 2 == nh and not diagonal:
                scores(0, j + 1)

    for h in range(nh):
        acc_ref[h][...] = jnp.zeros(acc_ref[h].shape, F32)
        pv_ref[h][...] = jnp.zeros(pv_ref[h].shape, F32)
        m_ref[h][...] = jnp.full(m_ref[h].shape, NEG, F32)
    p_ref[nh - 1][...] = jnp.zeros(p_ref[nh - 1].shape, BF16)
    scores(0, 0)

    def two_blocks(i, carry):
        step(2 * i, False)
        step(2 * i + 1, False)
        return carry
    lax.fori_loop(0, qi // 2, two_blocks, 0)

    @pl.when(qi % 2 == 0)
    def _():
        step(qi, True)

    @pl.when(qi % 2 == 1)
    def _():
        step(qi - 1, False)
        step(qi, True)

    values(nh - 1, qi)
    outs = []
    for h in range(nh):
        total = acc_ref[h][...] + pv_ref[h][...]
        outs.append(total[:HEAD_DIM] / total[HEAD_DIM:HEAD_DIM + 1])
    out = jnp.concatenate(outs, axis=0)
    o_ref[...] = (out.T * sg_ref[...].astype(F32)).astype(o_ref.dtype)


def _fox_attention(proj, csum, bsz, seq, t=512):
    width = FOX_HEADS * HEAD_DIM
    nh = FOX_STEP_HEADS
    gw = nh * HEAD_DIM
    n_groups = width // gw
    t = min(t, seq)
    nq = seq // t
    return pl.pallas_call(
        functools.partial(_fox_kernel, tk=t),
        grid=(bsz, n_groups, nq),
        in_specs=[
            pl.BlockSpec((t, gw), lambda b, h, i: (b * nq + i, h)),
            pl.BlockSpec((seq, gw), lambda b, h, i: (b, n_groups + h), pipeline_mode=pl.Buffered(1)),
            pl.BlockSpec((seq, gw), lambda b, h, i: (b, 2 * n_groups + h), pipeline_mode=pl.Buffered(1)),
            pl.BlockSpec((t, gw), lambda b, h, i: (b * nq + i, 3 * n_groups + h)),
            pl.BlockSpec((t, LANES), lambda b, h, i: (b * nq + i, 0)),
            pl.BlockSpec((seq, LANES), lambda b, h, i: (b, 0), pipeline_mode=pl.Buffered(1)),
        ],
        out_specs=pl.BlockSpec((t, gw), lambda b, h, i: (b * nq + i, h)),
        out_shape=jax.ShapeDtypeStruct((bsz * seq, width), BF16),
        scratch_shapes=[
            pltpu.VMEM((nh, nq, t, LANES), BF16), pltpu.VMEM((nq, nh, HEAD_DIM + ONES_ROWS, t), BF16),
        ] + nh * [pltpu.VMEM((LANES, t), BF16), pltpu.VMEM((t, t), F32), pltpu.VMEM((t, t), BF16),
                  pltpu.VMEM((HEAD_DIM + ONES_ROWS, t), F32), pltpu.VMEM((HEAD_DIM + ONES_ROWS, t), F32),
                  pltpu.VMEM((1, t), F32)],
        compiler_params=_params(3, VMEM_LIMIT_BIG),
        name="fox_attention",
    )(proj, proj, proj, proj, csum, csum)


def _fox_mixer(x, g, w_in, layer, w_fg, b_fg, bsz, seq, t=512):
    width = FOX_HEADS * HEAD_DIM
    col_scale = jnp.where(jnp.arange(4 * width) < width, LOG2E * HEAD_DIM ** -0.5, 1.0).astype(F32)
    pad = LANES - FOX_HEADS
    w_gate_logits = jnp.pad(w_fg, ((0, 0), (0, pad))).astype(BF16)
    proj, z = _norm_matmul(x, g, _tile_cols(w_in, layer, MATMUL_TN, col_scale), BF16, silu_cols=(3 * width, 4 * width),
                           w_side=w_gate_logits)
    csum = _forget_cumsum(z, jnp.pad(b_fg.astype(F32), (0, pad)).reshape(1, LANES), bsz, seq)
    return _fox_attention(proj, csum, bsz, seq, t)


def kernel(x, p, norm_g, final_g, rel_bias, swa_w_in, swa_w_out, swa_sinks, conv_w_in, conv_kernel, conv_w_out, ssm_w_in, ssm_lam_re, ssm_lam_im, ssm_log_dt, ssm_b_re, ssm_b_im, ssm_c_re, ssm_c_im, ssm_d, ssm_w_glu, ssm_b_glu, ssm_w_out, fox_w_in, fox_w_fg, fox_b_fg, fox_w_out, ple_proj, ple_norm, ple_gate):
    bsz, seq, d_model = x.shape
    depth = p.shape[0]
    h = x.astype(F32).reshape(bsz * seq, d_model)
    for i in range(depth):
        mixer, j = i % N_MIXERS, i // N_MIXERS
        if mixer == 0:
            a, w_out = _swa_mixer(h, norm_g[i], swa_w_in, j, swa_sinks[j], rel_bias, bsz, seq), swa_w_out[j]
        elif mixer == 1:
            a, w_out = _conv_mixer(h, norm_g[i], conv_w_in, j, conv_kernel[j], seq), conv_w_out[j]
        elif mixer == 2:
            a = _ssm_mixer(h, norm_g[i], ssm_w_in, j, ssm_lam_re[j], ssm_lam_im[j], ssm_log_dt[j], ssm_b_re[j], ssm_b_im[j],
                           ssm_c_re[j], ssm_c_im[j], ssm_d[j], ssm_w_glu, ssm_b_glu[j], bsz, seq)
            w_out = ssm_w_out[j]
        else:
            a, w_out = _fox_mixer(h, norm_g[i], fox_w_in, j, fox_w_fg[j], fox_b_fg[j], bsz, seq), fox_w_out[j]
        h = _out_ple(a, w_out, h, ple_norm[i], ple_gate, p.reshape(depth, bsz * seq, -1), ple_proj, i,
                     final_g=final_g if i == depth - 1 else None)
    return h.reshape(bsz, seq, d_model).astype(x.dtype)
```
